```python
import jax, jax.numpy as jnp
from jax import lax
import numpy as np

D_MODEL = 1024
BATCH = 8
SEQ = 4096
DEPTH = 2

N_A_LAYERS = DEPTH // 2
N_B_LAYERS = DEPTH - N_A_LAYERS
D_FF = 2816
D_RNN = 1344
N_LRU_BLOCKS = 16
LRU_BLOCK = D_RNN // N_LRU_BLOCKS
CONV_WIDTH = 4
LRU_C = 8.0
N_HEADS = 16
HEAD_DIM = 64
D_ATTN = N_HEADS * HEAD_DIM
Q_BLOCK = 128
EPS = 1e-6

kernel_name = "yoco_rglru_forgetting_attention_macaron"


def rms_norm(x, g):
    xf = x.astype(jnp.float32)
    y = xf * lax.rsqrt(jnp.mean(xf * xf, axis=-1, keepdims=True) + EPS)
    return (y * g.astype(jnp.float32)).astype(x.dtype)


def swiglu(x, w_gate, w_up, w_down):
    return (jax.nn.silu(x @ w_gate) * (x @ w_up)) @ w_down


def causal_depthwise_conv(x, w, b):
    S = x.shape[1]
    xp = jnp.pad(x, ((0, 0), (CONV_WIDTH - 1, 0), (0, 0)))
    y = b
    for k in range(CONV_WIDTH):
        y = y + xp[:, k:k + S] * w[k]
    return y


def rg_lru(x, w_a, b_a, w_x, b_x, lam):
    Bn, S, _ = x.shape
    xb = x.reshape(Bn, S, N_LRU_BLOCKS, LRU_BLOCK)
    r = jax.nn.sigmoid(jnp.einsum('bsnc,ncd->bsnd', xb, w_a).reshape(Bn, S, D_RNN) + b_a)
    i = jax.nn.sigmoid(jnp.einsum('bsnc,ncd->bsnd', xb, w_x).reshape(Bn, S, D_RNN) + b_x)
    log_a = -LRU_C * r.astype(jnp.float32) * jax.nn.softplus(-lam.astype(jnp.float32))
    a = jnp.exp(log_a)
    u = jnp.sqrt(-jnp.expm1(2.0 * log_a)) * (i * x).astype(jnp.float32)

    def combine(c1, c2):
        a1, b1 = c1
        a2, b2 = c2
        return a1 * a2, a2 * b1 + b2

    _, h = lax.associative_scan(combine, (a, u), axis=1)
    return h.astype(x.dtype)


def recurrent_block(x, w_in, conv_w, conv_b, w_a, b_a, w_x, b_x, lam, w_out):
    gx = x @ w_in
    gate, rec = gx[..., :D_RNN], gx[..., D_RNN:]
    rec = causal_depthwise_conv(rec, conv_w, conv_b)
    rec = rg_lru(rec, w_a, b_a, w_x, b_x, lam)
    return (jax.nn.gelu(gate) * rec) @ w_out


def shared_kv(h, g, w_kv, w_f, b_f):
    Bn, S, _ = h.shape
    hn = rms_norm(h, g)
    kv = (hn @ w_kv).reshape(Bn, S, 2, N_HEADS, HEAD_DIM)
    k = kv[:, :, 0].transpose(0, 2, 1, 3)
    v = kv[:, :, 1].transpose(0, 2, 1, 3)
    log_f = jax.nn.log_sigmoid((hn @ w_f + b_f).astype(jnp.float32))
    c = jnp.cumsum(log_f, axis=1).transpose(0, 2, 1)
    return k, v, c


def forgetting_attention(xq, w_q, w_o, k, v, c):
    Bn, S, _ = xq.shape
    q = (xq @ w_q).reshape(Bn, S, N_HEADS, HEAD_DIM).transpose(0, 2, 1, 3) * (HEAD_DIM ** -0.5)
    outs = []
    for blk in range(S // Q_BLOCK):
        q0 = blk * Q_BLOCK
        end = q0 + Q_BLOCK
        logits = jnp.einsum('bhqd,bhkd->bhqk', q[:, :, q0:end], k[:, :, :end]).astype(jnp.float32)
        logits = logits + c[:, :, q0:end, None] - c[:, :, None, :end]
        qpos = q0 + jnp.arange(Q_BLOCK)
        kpos = jnp.arange(end)
        logits = jnp.where(kpos[None, :] <= qpos[:, None], logits, -jnp.inf)
        p = jax.nn.softmax(logits, axis=-1).astype(v.dtype)
        outs.append(jnp.einsum('bhqk,bhkd->bhqd', p, v[:, :, :end]))
    o = jnp.concatenate(outs, axis=2).transpose(0, 2, 1, 3).reshape(Bn, S, D_ATTN)
    return o @ w_o


def _fwd_setup_inputs(seed: int = 0) -> dict:
    key = jax.random.key(seed)
    ks = iter(jax.random.split(key, 40))
    f32 = jnp.float32

    def nrm(shape, fan_in):
        return jax.random.normal(next(ks), shape, f32) * (fan_in ** -0.5)

    def gain(shape):
        return 1.0 + 0.05 * jax.random.normal(next(ks), shape, f32)

    def small(shape):
        return 0.02 * jax.random.normal(next(ks), shape, f32)

    L, NA, NB = DEPTH, N_A_LAYERS, N_B_LAYERS
    x = jax.random.normal(next(ks), (BATCH, SEQ, D_MODEL), f32)
    u = jax.random.uniform(next(ks), (NA, D_RNN), f32, minval=0.9, maxval=0.999)
    a0 = u ** (1.0 / LRU_C)
    rg_lambda = jnp.log(a0) - jnp.log1p(-a0)
    return {
        "x": x,
        "ffn1_pre_g": gain((L, D_MODEL)),
        "ffn1_w_gate": nrm((L, D_MODEL, D_FF), D_MODEL),
        "ffn1_w_up": nrm((L, D_MODEL, D_FF), D_MODEL),
        "ffn1_w_down": nrm((L, D_FF, D_MODEL), D_FF),
        "ffn1_post_g": gain((L, D_MODEL)),
        "mix_pre_g": gain((L, D_MODEL)),
        "mix_post_g": gain((L, D_MODEL)),
        "ffn2_pre_g": gain((L, D_MODEL)),
        "ffn2_w_gate": nrm((L, D_MODEL, D_FF), D_MODEL),
        "ffn2_w_up": nrm((L, D_MODEL, D_FF), D_MODEL),
        "ffn2_w_down": nrm((L, D_FF, D_MODEL), D_FF),
        "ffn2_post_g": gain((L, D_MODEL)),
        "rg_w_in": nrm((NA, D_MODEL, 2 * D_RNN), D_MODEL),
        "rg_conv_w": nrm((NA, CONV_WIDTH, D_RNN), CONV_WIDTH),
        "rg_conv_b": small((NA, D_RNN)),
        "rg_w_a": nrm((NA, N_LRU_BLOCKS, LRU_BLOCK, LRU_BLOCK), LRU_BLOCK),
        "rg_b_a": small((NA, D_RNN)),
        "rg_w_x": nrm((NA, N_LRU_BLOCKS, LRU_BLOCK, LRU_BLOCK), LRU_BLOCK),
        "rg_b_x": small((NA, D_RNN)),
        "rg_lambda": rg_lambda,
        "rg_w_out": nrm((NA, D_RNN, D_MODEL), D_RNN),
        "kv_norm_g": gain((D_MODEL,)),
        "w_kv": nrm((D_MODEL, 2 * D_ATTN), D_MODEL),
        "w_fgate": nrm((D_MODEL, N_HEADS), D_MODEL),
        "b_fgate": jax.random.uniform(next(ks), (N_HEADS,), f32, minval=1.0, maxval=4.0),
        "attn_w_q": nrm((NB, D_MODEL, D_ATTN), D_MODEL),
        "attn_w_o": nrm((NB, D_ATTN, D_MODEL), D_ATTN),
    }


def _fwd_reference(x, ffn1_pre_g, ffn1_w_gate, ffn1_w_up, ffn1_w_down, ffn1_post_g,
              mix_pre_g, mix_post_g,
              ffn2_pre_g, ffn2_w_gate, ffn2_w_up, ffn2_w_down, ffn2_post_g,
              rg_w_in, rg_conv_w, rg_conv_b, rg_w_a, rg_b_a, rg_w_x, rg_b_x, rg_lambda, rg_w_out,
              kv_norm_g, w_kv, w_fgate, b_fgate, attn_w_q, attn_w_o):
    h = x
    k = v = c = None
    for layer in range(DEPTH):
        if layer == N_A_LAYERS:
            k, v, c = shared_kv(h, kv_norm_g, w_kv, w_fgate, b_fgate)
        f = swiglu(rms_norm(h, ffn1_pre_g[layer]), ffn1_w_gate[layer], ffn1_w_up[layer], ffn1_w_down[layer])
        h = h + 0.5 * rms_norm(f, ffn1_post_g[layer])
        hn = rms_norm(h, mix_pre_g[layer])
        if layer < N_A_LAYERS:
            j = layer
            m = recurrent_block(hn, rg_w_in[j], rg_conv_w[j], rg_conv_b[j], rg_w_a[j], rg_b_a[j],
                                rg_w_x[j], rg_b_x[j], rg_lambda[j], rg_w_out[j])
        else:
            j = layer - N_A_LAYERS
            m = forgetting_attention(hn, attn_w_q[j], attn_w_o[j], k, v, c)
        h = h + rms_norm(m, mix_post_g[layer])
        f = swiglu(rms_norm(h, ffn2_pre_g[layer]), ffn2_w_gate[layer], ffn2_w_up[layer], ffn2_w_down[layer])
        h = h + 0.5 * rms_norm(f, ffn2_post_g[layer])
    return h


import jax as _jax
import jax.numpy as _jnp

TWIN_FORMAT = 'train_step'
FWD_PARAMS = ['x', 'ffn1_pre_g', 'ffn1_w_gate', 'ffn1_w_up', 'ffn1_w_down', 'ffn1_post_g', 'mix_pre_g', 'mix_post_g', 'ffn2_pre_g', 'ffn2_w_gate', 'ffn2_w_up', 'ffn2_w_down', 'ffn2_post_g', 'rg_w_in', 'rg_conv_w', 'rg_conv_b', 'rg_w_a', 'rg_b_a', 'rg_w_x', 'rg_b_x', 'rg_lambda', 'rg_w_out', 'kv_norm_g', 'w_kv', 'w_fgate', 'b_fgate', 'attn_w_q', 'attn_w_o']
TWIN_WEIGHTS = ['ffn1_pre_g', 'ffn1_w_gate', 'ffn1_w_up', 'ffn1_w_down', 'ffn1_post_g', 'mix_pre_g', 'mix_post_g', 'ffn2_pre_g', 'ffn2_w_gate', 'ffn2_w_up', 'ffn2_w_down', 'ffn2_post_g', 'rg_w_in', 'rg_conv_w', 'rg_conv_b', 'rg_w_a', 'rg_b_a', 'rg_w_x', 'rg_b_x', 'rg_lambda', 'rg_w_out', 'kv_norm_g', 'w_kv', 'w_fgate', 'b_fgate', 'attn_w_q', 'attn_w_o']
TWIN_DIFF_INPUT = 'x'
TWIN_INPUTS = ['x', 'ffn1_pre_g', 'ffn1_w_gate', 'ffn1_w_up', 'ffn1_w_down', 'ffn1_post_g', 'mix_pre_g', 'mix_post_g', 'ffn2_pre_g', 'ffn2_w_gate', 'ffn2_w_up', 'ffn2_w_down', 'ffn2_post_g', 'rg_w_in', 'rg_conv_w', 'rg_conv_b', 'rg_w_a', 'rg_b_a', 'rg_w_x', 'rg_b_x', 'rg_lambda', 'rg_w_out', 'kv_norm_g', 'w_kv', 'w_fgate', 'b_fgate', 'attn_w_q', 'attn_w_o', 'loss_target', 'm_ffn1_pre_g', 'm_ffn1_w_gate', 'm_ffn1_w_up', 'm_ffn1_w_down', 'm_ffn1_post_g', 'm_mix_pre_g', 'm_mix_post_g', 'm_ffn2_pre_g', 'm_ffn2_w_gate', 'm_ffn2_w_up', 'm_ffn2_w_down', 'm_ffn2_post_g', 'm_rg_w_in', 'm_rg_conv_w', 'm_rg_conv_b', 'm_rg_w_a', 'm_rg_b_a', 'm_rg_w_x', 'm_rg_b_x', 'm_rg_lambda', 'm_rg_w_out', 'm_kv_norm_g', 'm_w_kv', 'm_w_fgate', 'm_b_fgate', 'm_attn_w_q', 'm_attn_w_o', 'v_ffn1_pre_g', 'v_ffn1_w_gate', 'v_ffn1_w_up', 'v_ffn1_w_down', 'v_ffn1_post_g', 'v_mix_pre_g', 'v_mix_post_g', 'v_ffn2_pre_g', 'v_ffn2_w_gate', 'v_ffn2_w_up', 'v_ffn2_w_down', 'v_ffn2_post_g', 'v_rg_w_in', 'v_rg_conv_w', 'v_rg_conv_b', 'v_rg_w_a', 'v_rg_b_a', 'v_rg_w_x', 'v_rg_b_x', 'v_rg_lambda', 'v_rg_w_out', 'v_kv_norm_g', 'v_w_kv', 'v_w_fgate', 'v_b_fgate', 'v_attn_w_q', 'v_attn_w_o']
TWIN_OUTPUTS = ['loss', 'grad_x', 'grad_ffn1_pre_g', 'grad_ffn1_w_gate', 'grad_ffn1_w_up', 'grad_ffn1_w_down', 'grad_ffn1_post_g', 'grad_mix_pre_g', 'grad_mix_post_g', 'grad_ffn2_pre_g', 'grad_ffn2_w_gate', 'grad_ffn2_w_up', 'grad_ffn2_w_down', 'grad_ffn2_post_g', 'grad_rg_w_in', 'grad_rg_conv_w', 'grad_rg_conv_b', 'grad_rg_w_a', 'grad_rg_b_a', 'grad_rg_w_x', 'grad_rg_b_x', 'grad_rg_lambda', 'grad_rg_w_out', 'grad_kv_norm_g', 'grad_w_kv', 'grad_w_fgate', 'grad_b_fgate', 'grad_attn_w_q', 'grad_attn_w_o', 'delta_ffn1_pre_g', 'delta_ffn1_w_gate', 'delta_ffn1_w_up', 'delta_ffn1_w_down', 'delta_ffn1_post_g', 'delta_mix_pre_g', 'delta_mix_post_g', 'delta_ffn2_pre_g', 'delta_ffn2_w_gate', 'delta_ffn2_w_up', 'delta_ffn2_w_down', 'delta_ffn2_post_g', 'delta_rg_w_in', 'delta_rg_conv_w', 'delta_rg_conv_b', 'delta_rg_w_a', 'delta_rg_b_a', 'delta_rg_w_x', 'delta_rg_b_x', 'delta_rg_lambda', 'delta_rg_w_out', 'delta_kv_norm_g', 'delta_w_kv', 'delta_w_fgate', 'delta_b_fgate', 'delta_attn_w_q', 'delta_attn_w_o', 'new_m_ffn1_pre_g', 'new_m_ffn1_w_gate', 'new_m_ffn1_w_up', 'new_m_ffn1_w_down', 'new_m_ffn1_post_g', 'new_m_mix_pre_g', 'new_m_mix_post_g', 'new_m_ffn2_pre_g', 'new_m_ffn2_w_gate', 'new_m_ffn2_w_up', 'new_m_ffn2_w_down', 'new_m_ffn2_post_g', 'new_m_rg_w_in', 'new_m_rg_conv_w', 'new_m_rg_conv_b', 'new_m_rg_w_a', 'new_m_rg_b_a', 'new_m_rg_w_x', 'new_m_rg_b_x', 'new_m_rg_lambda', 'new_m_rg_w_out', 'new_m_kv_norm_g', 'new_m_w_kv', 'new_m_w_fgate', 'new_m_b_fgate', 'new_m_attn_w_q', 'new_m_attn_w_o', 'new_v_ffn1_pre_g', 'new_v_ffn1_w_gate', 'new_v_ffn1_w_up', 'new_v_ffn1_w_down', 'new_v_ffn1_post_g', 'new_v_mix_pre_g', 'new_v_mix_post_g', 'new_v_ffn2_pre_g', 'new_v_ffn2_w_gate', 'new_v_ffn2_w_up', 'new_v_ffn2_w_down', 'new_v_ffn2_post_g', 'new_v_rg_w_in', 'new_v_rg_conv_w', 'new_v_rg_conv_b', 'new_v_rg_w_a', 'new_v_rg_b_a', 'new_v_rg_w_x', 'new_v_rg_b_x', 'new_v_rg_lambda', 'new_v_rg_w_out', 'new_v_kv_norm_g', 'new_v_w_kv', 'new_v_w_fgate', 'new_v_b_fgate', 'new_v_attn_w_q', 'new_v_attn_w_o']
TWIN_LEAF_KINDS = {'loss': 'loss', 'grad_x': 'grad_x', 'grad_ffn1_pre_g': 'grad_w', 'grad_ffn1_w_gate': 'grad_w', 'grad_ffn1_w_up': 'grad_w', 'grad_ffn1_w_down': 'grad_w', 'grad_ffn1_post_g': 'grad_w', 'grad_mix_pre_g': 'grad_w', 'grad_mix_post_g': 'grad_w', 'grad_ffn2_pre_g': 'grad_w', 'grad_ffn2_w_gate': 'grad_w', 'grad_ffn2_w_up': 'grad_w', 'grad_ffn2_w_down': 'grad_w', 'grad_ffn2_post_g': 'grad_w', 'grad_rg_w_in': 'grad_w', 'grad_rg_conv_w': 'grad_w', 'grad_rg_conv_b': 'grad_w', 'grad_rg_w_a': 'grad_w', 'grad_rg_b_a': 'grad_w', 'grad_rg_w_x': 'grad_w', 'grad_rg_b_x': 'grad_w', 'grad_rg_lambda': 'grad_w', 'grad_rg_w_out': 'grad_w', 'grad_kv_norm_g': 'grad_w', 'grad_w_kv': 'grad_w', 'grad_w_fgate': 'grad_w', 'grad_b_fgate': 'grad_w', 'grad_attn_w_q': 'grad_w', 'grad_attn_w_o': 'grad_w', 'delta_ffn1_pre_g': 'delta_w', 'delta_ffn1_w_gate': 'delta_w', 'delta_ffn1_w_up': 'delta_w', 'delta_ffn1_w_down': 'delta_w', 'delta_ffn1_post_g': 'delta_w', 'delta_mix_pre_g': 'delta_w', 'delta_mix_post_g': 'delta_w', 'delta_ffn2_pre_g': 'delta_w', 'delta_ffn2_w_gate': 'delta_w', 'delta_ffn2_w_up': 'delta_w', 'delta_ffn2_w_down': 'delta_w', 'delta_ffn2_post_g': 'delta_w', 'delta_rg_w_in': 'delta_w', 'delta_rg_conv_w': 'delta_w', 'delta_rg_conv_b': 'delta_w', 'delta_rg_w_a': 'delta_w', 'delta_rg_b_a': 'delta_w', 'delta_rg_w_x': 'delta_w', 'delta_rg_b_x': 'delta_w', 'delta_rg_lambda': 'delta_w', 'delta_rg_w_out': 'delta_w', 'delta_kv_norm_g': 'delta_w', 'delta_w_kv': 'delta_w', 'delta_w_fgate': 'delta_w', 'delta_b_fgate': 'delta_w', 'delta_attn_w_q': 'delta_w', 'delta_attn_w_o': 'delta_w', 'new_m_ffn1_pre_g': 'new_m', 'new_m_ffn1_w_gate': 'new_m', 'new_m_ffn1_w_up': 'new_m', 'new_m_ffn1_w_down': 'new_m', 'new_m_ffn1_post_g': 'new_m', 'new_m_mix_pre_g': 'new_m', 'new_m_mix_post_g': 'new_m', 'new_m_ffn2_pre_g': 'new_m', 'new_m_ffn2_w_gate': 'new_m', 'new_m_ffn2_w_up': 'new_m', 'new_m_ffn2_w_down': 'new_m', 'new_m_ffn2_post_g': 'new_m', 'new_m_rg_w_in': 'new_m', 'new_m_rg_conv_w': 'new_m', 'new_m_rg_conv_b': 'new_m', 'new_m_rg_w_a': 'new_m', 'new_m_rg_b_a': 'new_m', 'new_m_rg_w_x': 'new_m', 'new_m_rg_b_x': 'new_m', 'new_m_rg_lambda': 'new_m', 'new_m_rg_w_out': 'new_m', 'new_m_kv_norm_g': 'new_m', 'new_m_w_kv': 'new_m', 'new_m_w_fgate': 'new_m', 'new_m_b_fgate': 'new_m', 'new_m_attn_w_q': 'new_m', 'new_m_attn_w_o': 'new_m', 'new_v_ffn1_pre_g': 'new_v', 'new_v_ffn1_w_gate': 'new_v', 'new_v_ffn1_w_up': 'new_v', 'new_v_ffn1_w_down': 'new_v', 'new_v_ffn1_post_g': 'new_v', 'new_v_mix_pre_g': 'new_v', 'new_v_mix_post_g': 'new_v', 'new_v_ffn2_pre_g': 'new_v', 'new_v_ffn2_w_gate': 'new_v', 'new_v_ffn2_w_up': 'new_v', 'new_v_ffn2_w_down': 'new_v', 'new_v_ffn2_post_g': 'new_v', 'new_v_rg_w_in': 'new_v', 'new_v_rg_conv_w': 'new_v', 'new_v_rg_conv_b': 'new_v', 'new_v_rg_w_a': 'new_v', 'new_v_rg_b_a': 'new_v', 'new_v_rg_w_x': 'new_v', 'new_v_rg_b_x': 'new_v', 'new_v_rg_lambda': 'new_v', 'new_v_rg_w_out': 'new_v', 'new_v_kv_norm_g': 'new_v', 'new_v_w_kv': 'new_v', 'new_v_w_fgate': 'new_v', 'new_v_b_fgate': 'new_v', 'new_v_attn_w_q': 'new_v', 'new_v_attn_w_o': 'new_v'}


def _forward(args):
    return _fwd_reference(*[args[k] for k in FWD_PARAMS])


def _output_shape():
    out = _jax.eval_shape(lambda: _forward(_fwd_setup_inputs(0)))
    return out.shape, out.dtype

N_MICROBATCH = 1
ADAM_LR = 0.001
ADAM_B1 = 0.9
ADAM_B2 = 0.999
ADAM_EPS = 1e-08
ADAM_WD = 0.01
ADAM_STEP = 10
PER_EXAMPLE_BATCH_AXIS = {'x': 0, 'loss_target': 0}
SHARED_INPUTS = []
_WEIGHT_DTYPES = {'ffn1_pre_g': _jnp.float32, 'ffn1_w_gate': _jnp.float32, 'ffn1_w_up': _jnp.float32, 'ffn1_w_down': _jnp.float32, 'ffn1_post_g': _jnp.float32, 'mix_pre_g': _jnp.float32, 'mix_post_g': _jnp.float32, 'ffn2_pre_g': _jnp.float32, 'ffn2_w_gate': _jnp.float32, 'ffn2_w_up': _jnp.float32, 'ffn2_w_down': _jnp.float32, 'ffn2_post_g': _jnp.float32, 'rg_w_in': _jnp.float32, 'rg_conv_w': _jnp.float32, 'rg_conv_b': _jnp.float32, 'rg_w_a': _jnp.float32, 'rg_b_a': _jnp.float32, 'rg_w_x': _jnp.float32, 'rg_b_x': _jnp.float32, 'rg_lambda': _jnp.float32, 'rg_w_out': _jnp.float32, 'kv_norm_g': _jnp.float32, 'w_kv': _jnp.float32, 'w_fgate': _jnp.float32, 'b_fgate': _jnp.float32, 'attn_w_q': _jnp.float32, 'attn_w_o': _jnp.float32}
MOMENT_SCALE = {'ffn1_pre_g': 9.613115e-01, 'ffn1_w_gate': 3.428882e-01, 'ffn1_w_up': 3.687900e-01, 'ffn1_w_down': 6.172009e-01, 'ffn1_post_g': 7.801553e+00, 'mix_pre_g': 1.058417e+00, 'mix_post_g': 3.288030e+01, 'ffn2_pre_g': 7.353236e-01, 'ffn2_w_gate': 2.316294e-01, 'ffn2_w_up': 3.825759e-01, 'ffn2_w_down': 6.277096e-01, 'ffn2_post_g': 8.084076e+00, 'rg_w_in': 8.070647e-01, 'rg_conv_w': 2.539672e+00, 'rg_conv_b': 4.522725e+01, 'rg_w_a': 1.531796e+00, 'rg_b_a': 1.006380e+00, 'rg_w_x': 2.771462e+00, 'rg_b_x': 7.598045e-01, 'rg_lambda': 1.305328e+00, 'rg_w_out': 4.409407e+00, 'kv_norm_g': 3.638081e+00, 'w_kv': 2.469654e+00, 'w_fgate': 8.781061e-01, 'b_fgate': 3.585083e+00, 'attn_w_q': 2.579382e-01, 'attn_w_o': 3.622510e+00}


def _to_microbatches(a, axis):
    t = _jnp.moveaxis(a, axis, 0)
    t = t.reshape((N_MICROBATCH, t.shape[0] // N_MICROBATCH) + t.shape[1:])
    return _jnp.moveaxis(t, 1, axis + 1)


def setup_inputs(seed: int = 0) -> dict:
    inp = _fwd_setup_inputs(seed)
    key = _jax.random.fold_in(_jax.random.key(seed), 7919)
    shape, _ = _output_shape()
    out = dict(inp)
    out["loss_target"] = _jax.random.normal(_jax.random.fold_in(key, 0), shape, _jnp.float32)
    for i, name in enumerate(TWIN_WEIGHTS):
        w = inp[name].astype(_jnp.float32)
        if MOMENT_SCALE is None:
            s = _jnp.sqrt(_jnp.mean(_jnp.square(w)) + 1e-30)
        else:
            s = MOMENT_SCALE[name]
        km, kv = _jax.random.split(_jax.random.fold_in(key, i + 1))
        out[name] = w
        out["m_" + name] = s * _jax.random.normal(km, w.shape, _jnp.float32)
        out["v_" + name] = (s * s) * _jax.random.uniform(kv, w.shape, _jnp.float32, 0.5, 1.5)
    if N_MICROBATCH > 1:
        for name, axis in PER_EXAMPLE_BATCH_AXIS.items():
            out[name] = _to_microbatches(out[name], axis)
    return {'x': out['x'], 'ffn1_pre_g': out['ffn1_pre_g'], 'ffn1_w_gate': out['ffn1_w_gate'], 'ffn1_w_up': out['ffn1_w_up'], 'ffn1_w_down': out['ffn1_w_down'], 'ffn1_post_g': out['ffn1_post_g'], 'mix_pre_g': out['mix_pre_g'], 'mix_post_g': out['mix_post_g'], 'ffn2_pre_g': out['ffn2_pre_g'], 'ffn2_w_gate': out['ffn2_w_gate'], 'ffn2_w_up': out['ffn2_w_up'], 'ffn2_w_down': out['ffn2_w_down'], 'ffn2_post_g': out['ffn2_post_g'], 'rg_w_in': out['rg_w_in'], 'rg_conv_w': out['rg_conv_w'], 'rg_conv_b': out['rg_conv_b'], 'rg_w_a': out['rg_w_a'], 'rg_b_a': out['rg_b_a'], 'rg_w_x': out['rg_w_x'], 'rg_b_x': out['rg_b_x'], 'rg_lambda': out['rg_lambda'], 'rg_w_out': out['rg_w_out'], 'kv_norm_g': out['kv_norm_g'], 'w_kv': out['w_kv'], 'w_fgate': out['w_fgate'], 'b_fgate': out['b_fgate'], 'attn_w_q': out['attn_w_q'], 'attn_w_o': out['attn_w_o'], 'loss_target': out['loss_target'], 'm_ffn1_pre_g': out['m_ffn1_pre_g'], 'm_ffn1_w_gate': out['m_ffn1_w_gate'], 'm_ffn1_w_up': out['m_ffn1_w_up'], 'm_ffn1_w_down': out['m_ffn1_w_down'], 'm_ffn1_post_g': out['m_ffn1_post_g'], 'm_mix_pre_g': out['m_mix_pre_g'], 'm_mix_post_g': out['m_mix_post_g'], 'm_ffn2_pre_g': out['m_ffn2_pre_g'], 'm_ffn2_w_gate': out['m_ffn2_w_gate'], 'm_ffn2_w_up': out['m_ffn2_w_up'], 'm_ffn2_w_down': out['m_ffn2_w_down'], 'm_ffn2_post_g': out['m_ffn2_post_g'], 'm_rg_w_in': out['m_rg_w_in'], 'm_rg_conv_w': out['m_rg_conv_w'], 'm_rg_conv_b': out['m_rg_conv_b'], 'm_rg_w_a': out['m_rg_w_a'], 'm_rg_b_a': out['m_rg_b_a'], 'm_rg_w_x': out['m_rg_w_x'], 'm_rg_b_x': out['m_rg_b_x'], 'm_rg_lambda': out['m_rg_lambda'], 'm_rg_w_out': out['m_rg_w_out'], 'm_kv_norm_g': out['m_kv_norm_g'], 'm_w_kv': out['m_w_kv'], 'm_w_fgate': out['m_w_fgate'], 'm_b_fgate': out['m_b_fgate'], 'm_attn_w_q': out['m_attn_w_q'], 'm_attn_w_o': out['m_attn_w_o'], 'v_ffn1_pre_g': out['v_ffn1_pre_g'], 'v_ffn1_w_gate': out['v_ffn1_w_gate'], 'v_ffn1_w_up': out['v_ffn1_w_up'], 'v_ffn1_w_down': out['v_ffn1_w_down'], 'v_ffn1_post_g': out['v_ffn1_post_g'], 'v_mix_pre_g': out['v_mix_pre_g'], 'v_mix_post_g': out['v_mix_post_g'], 'v_ffn2_pre_g': out['v_ffn2_pre_g'], 'v_ffn2_w_gate': out['v_ffn2_w_gate'], 'v_ffn2_w_up': out['v_ffn2_w_up'], 'v_ffn2_w_down': out['v_ffn2_w_down'], 'v_ffn2_post_g': out['v_ffn2_post_g'], 'v_rg_w_in': out['v_rg_w_in'], 'v_rg_conv_w': out['v_rg_conv_w'], 'v_rg_conv_b': out['v_rg_conv_b'], 'v_rg_w_a': out['v_rg_w_a'], 'v_rg_b_a': out['v_rg_b_a'], 'v_rg_w_x': out['v_rg_w_x'], 'v_rg_b_x': out['v_rg_b_x'], 'v_rg_lambda': out['v_rg_lambda'], 'v_rg_w_out': out['v_rg_w_out'], 'v_kv_norm_g': out['v_kv_norm_g'], 'v_w_kv': out['v_w_kv'], 'v_w_fgate': out['v_w_fgate'], 'v_b_fgate': out['v_b_fgate'], 'v_attn_w_q': out['v_attn_w_q'], 'v_attn_w_o': out['v_attn_w_o']}


def _loss(weights, diff, rest, loss_target):
    with _jax.named_scope("forward"):
        args = {**rest, TWIN_DIFF_INPUT: diff, **{k: w.astype(_WEIGHT_DTYPES[k]) for k, w in weights.items()}}
        y = _forward(args)
    with _jax.named_scope("loss_head"):
        err = _jnp.square(y.astype(_jnp.float32) - loss_target)
        return 0.5 * _jnp.sum(_jnp.mean(err, axis=-1)) if err.ndim else 0.5 * err


def _adamw(w, g, m, v):
    m = ADAM_B1 * m + (1.0 - ADAM_B1) * g
    v = ADAM_B2 * v + (1.0 - ADAM_B2) * _jnp.square(g)
    m_hat = m / (1.0 - ADAM_B1 ** ADAM_STEP)
    v_hat = v / (1.0 - ADAM_B2 ** ADAM_STEP)
    delta = -ADAM_LR * (m_hat / (_jnp.sqrt(v_hat) + ADAM_EPS) + ADAM_WD * w)
    return delta, m, v


def reference(x, ffn1_pre_g, ffn1_w_gate, ffn1_w_up, ffn1_w_down, ffn1_post_g, mix_pre_g, mix_post_g, ffn2_pre_g, ffn2_w_gate, ffn2_w_up, ffn2_w_down, ffn2_post_g, rg_w_in, rg_conv_w, rg_conv_b, rg_w_a, rg_b_a, rg_w_x, rg_b_x, rg_lambda, rg_w_out, kv_norm_g, w_kv, w_fgate, b_fgate, attn_w_q, attn_w_o, loss_target, m_ffn1_pre_g, m_ffn1_w_gate, m_ffn1_w_up, m_ffn1_w_down, m_ffn1_post_g, m_mix_pre_g, m_mix_post_g, m_ffn2_pre_g, m_ffn2_w_gate, m_ffn2_w_up, m_ffn2_w_down, m_ffn2_post_g, m_rg_w_in, m_rg_conv_w, m_rg_conv_b, m_rg_w_a, m_rg_b_a, m_rg_w_x, m_rg_b_x, m_rg_lambda, m_rg_w_out, m_kv_norm_g, m_w_kv, m_w_fgate, m_b_fgate, m_attn_w_q, m_attn_w_o, v_ffn1_pre_g, v_ffn1_w_gate, v_ffn1_w_up, v_ffn1_w_down, v_ffn1_post_g, v_mix_pre_g, v_mix_post_g, v_ffn2_pre_g, v_ffn2_w_gate, v_ffn2_w_up, v_ffn2_w_down, v_ffn2_post_g, v_rg_w_in, v_rg_conv_w, v_rg_conv_b, v_rg_w_a, v_rg_b_a, v_rg_w_x, v_rg_b_x, v_rg_lambda, v_rg_w_out, v_kv_norm_g, v_w_kv, v_w_fgate, v_b_fgate, v_attn_w_q, v_attn_w_o):
    given = dict(x=x, ffn1_pre_g=ffn1_pre_g, ffn1_w_gate=ffn1_w_gate, ffn1_w_up=ffn1_w_up, ffn1_w_down=ffn1_w_down, ffn1_post_g=ffn1_post_g, mix_pre_g=mix_pre_g, mix_post_g=mix_post_g, ffn2_pre_g=ffn2_pre_g, ffn2_w_gate=ffn2_w_gate, ffn2_w_up=ffn2_w_up, ffn2_w_down=ffn2_w_down, ffn2_post_g=ffn2_post_g, rg_w_in=rg_w_in, rg_conv_w=rg_conv_w, rg_conv_b=rg_conv_b, rg_w_a=rg_w_a, rg_b_a=rg_b_a, rg_w_x=rg_w_x, rg_b_x=rg_b_x, rg_lambda=rg_lambda, rg_w_out=rg_w_out, kv_norm_g=kv_norm_g, w_kv=w_kv, w_fgate=w_fgate, b_fgate=b_fgate, attn_w_q=attn_w_q, attn_w_o=attn_w_o, loss_target=loss_target, m_ffn1_pre_g=m_ffn1_pre_g, m_ffn1_w_gate=m_ffn1_w_gate, m_ffn1_w_up=m_ffn1_w_up, m_ffn1_w_down=m_ffn1_w_down, m_ffn1_post_g=m_ffn1_post_g, m_mix_pre_g=m_mix_pre_g, m_mix_post_g=m_mix_post_g, m_ffn2_pre_g=m_ffn2_pre_g, m_ffn2_w_gate=m_ffn2_w_gate, m_ffn2_w_up=m_ffn2_w_up, m_ffn2_w_down=m_ffn2_w_down, m_ffn2_post_g=m_ffn2_post_g, m_rg_w_in=m_rg_w_in, m_rg_conv_w=m_rg_conv_w, m_rg_conv_b=m_rg_conv_b, m_rg_w_a=m_rg_w_a, m_rg_b_a=m_rg_b_a, m_rg_w_x=m_rg_w_x, m_rg_b_x=m_rg_b_x, m_rg_lambda=m_rg_lambda, m_rg_w_out=m_rg_w_out, m_kv_norm_g=m_kv_norm_g, m_w_kv=m_w_kv, m_w_fgate=m_w_fgate, m_b_fgate=m_b_fgate, m_attn_w_q=m_attn_w_q, m_attn_w_o=m_attn_w_o, v_ffn1_pre_g=v_ffn1_pre_g, v_ffn1_w_gate=v_ffn1_w_gate, v_ffn1_w_up=v_ffn1_w_up, v_ffn1_w_down=v_ffn1_w_down, v_ffn1_post_g=v_ffn1_post_g, v_mix_pre_g=v_mix_pre_g, v_mix_post_g=v_mix_post_g, v_ffn2_pre_g=v_ffn2_pre_g, v_ffn2_w_gate=v_ffn2_w_gate, v_ffn2_w_up=v_ffn2_w_up, v_ffn2_w_down=v_ffn2_w_down, v_ffn2_post_g=v_ffn2_post_g, v_rg_w_in=v_rg_w_in, v_rg_conv_w=v_rg_conv_w, v_rg_conv_b=v_rg_conv_b, v_rg_w_a=v_rg_w_a, v_rg_b_a=v_rg_b_a, v_rg_w_x=v_rg_w_x, v_rg_b_x=v_rg_b_x, v_rg_lambda=v_rg_lambda, v_rg_w_out=v_rg_w_out, v_kv_norm_g=v_kv_norm_g, v_w_kv=v_w_kv, v_w_fgate=v_w_fgate, v_b_fgate=v_b_fgate, v_attn_w_q=v_attn_w_q, v_attn_w_o=v_attn_w_o)
    weights = {n: given[n] for n in TWIN_WEIGHTS}
    shared = {n: given[n] for n in SHARED_INPUTS}
    per_example = {n: given[n] for n in ['x']}
    grad_fn = _jax.value_and_grad(_loss, argnums=(0, 1))

    def one_microbatch(ex, loss_target):
        ex = dict(ex)
        diff = ex.pop(TWIN_DIFF_INPUT)
        return grad_fn(weights, diff, {**shared, **ex}, loss_target)

    if N_MICROBATCH == 1:
        loss, (grad_w, grad_x) = one_microbatch(per_example, given["loss_target"])
    else:
        def body(carry, xs):
            loss_sum, grad_sum = carry
            l_k, (gw_k, gx_k) = one_microbatch(xs[0], xs[1])
            with _jax.named_scope("update"):
                return (loss_sum + l_k, _jax.tree.map(_jnp.add, grad_sum, gw_k)), gx_k

        init = (_jnp.zeros((), _jnp.float32), _jax.tree.map(_jnp.zeros_like, weights))
        (loss, grad_w), grad_x = _jax.lax.scan(body, init, (per_example, given["loss_target"]))
    with _jax.named_scope("update"):
        delta_w, new_m, new_v = {}, {}, {}
        for n in TWIN_WEIGHTS:
            delta_w[n], new_m[n], new_v[n] = _adamw(weights[n], grad_w[n], given["m_" + n], given["v_" + n])
    return (loss, grad_x, *[grad_w[n] for n in TWIN_WEIGHTS], *[delta_w[n] for n in TWIN_WEIGHTS],
            *[new_m[n] for n in TWIN_WEIGHTS], *[new_v[n] for n in TWIN_WEIGHTS])
```

```python
import functools

import jax
import jax.numpy as jnp
from jax import lax
from jax.experimental import pallas as pl
from jax.experimental.pallas import tpu as pltpu

F32 = jnp.float32
MXU_DTYPE = jnp.bfloat16
WIRE_DTYPE = jnp.bfloat16
N_DEV = 8
AXES = ("x", "y", "c")
LANES = 128
WIRE_ROW_ALIGN = 16
VMEM_LIMIT_MIN = 32 * 2 ** 20
VMEM_LIMIT_MAX = 56 * 2 ** 20

RMS_EPS = 1e-6
LRU_C = 8.0
ADAM_LR, ADAM_B1, ADAM_B2, ADAM_EPS, ADAM_WD, ADAM_STEP = 0.001, 0.9, 0.999, 1e-08, 0.01, 10

WEIGHT_NAMES = (
    "ffn1_pre_g", "ffn1_w_gate", "ffn1_w_up", "ffn1_w_down", "ffn1_post_g", "mix_pre_g", "mix_post_g",
    "ffn2_pre_g", "ffn2_w_gate", "ffn2_w_up", "ffn2_w_down", "ffn2_post_g", "rg_w_in", "rg_conv_w",
    "rg_conv_b", "rg_w_a", "rg_b_a", "rg_w_x", "rg_b_x", "rg_lambda", "rg_w_out", "kv_norm_g", "w_kv",
    "w_fgate", "b_fgate", "attn_w_q", "attn_w_o")


def _round_up(n, m):
    return (n + m - 1) // m * m


def _tile(dim, target, align=LANES):
    if dim <= target:
        return dim
    best = None
    t = align
    while t <= target:
        if dim % t == 0:
            best = t
        t += align
    return dim if best is None else best


def _cparams(semantics, vmem_estimate):
    limit = min(VMEM_LIMIT_MAX, max(VMEM_LIMIT_MIN, 2 * int(vmem_estimate)))
    return pltpu.CompilerParams(dimension_semantics=semantics, vmem_limit_bytes=limit)


def _nbytes(shape, dtype):
    n = 1
    for s in shape:
        n *= s
    return n * jnp.dtype(dtype).itemsize


def _sigmoid(x):
    return jax.nn.sigmoid(x)


def _softplus(x):
    return jnp.maximum(x, 0.0) + jnp.log1p(jnp.exp(-jnp.abs(x)))


def _expm1(x):
    series = x * (1.0 + x * (0.5 + x * (1.0 / 6.0 + x * (1.0 / 24.0 + x * (1.0 / 120.0)))))
    return jnp.where(jnp.abs(x) < 0.25, series, jnp.exp(x) - 1.0)


_GELU_C = 0.7978845608028654
_GELU_A = 0.044715


def _gelu(x):
    return 0.5 * x * (1.0 + jnp.tanh(_GELU_C * (x + _GELU_A * x * x * x)))


def _gelu_grad(x):
    t = jnp.tanh(_GELU_C * (x + _GELU_A * x * x * x))
    return 0.5 * (1.0 + t) + 0.5 * x * (1.0 - t * t) * _GELU_C * (1.0 + 3.0 * _GELU_A * x * x)


_DOT_DIMS = {"nn": ((1,), (0,)), "nt": ((1,), (1,)), "tn": ((0,), (0,))}


def _dot(a, b, mode):
    return lax.dot_general(a.astype(MXU_DTYPE), b.astype(MXU_DTYPE), (_DOT_DIMS[mode], ((), ())),
                           preferred_element_type=F32)


def _mm(pairs, mode, out_dtype, name):
    a0, b0 = pairs[0]
    if mode == "tn":
        k_dim, m_dim = a0.shape
        n_dim = b0.shape[1]
    else:
        m_dim, k_dim = a0.shape
        n_dim = b0.shape[0] if mode == "nt" else b0.shape[1]
    for a, b in pairs:
        assert a.shape == a0.shape and b.shape == b0.shape
    tm = _tile(m_dim, 1408 if mode == "tn" else 512)
    tn = _tile(n_dim, 1408)
    tk = _tile(k_dim, 1408)
    nk = k_dim // tk
    n_pairs = len(pairs)

    if mode == "tn":
        a_spec = pl.BlockSpec((tk, tm), lambda i, j, k: (k, i))
    else:
        a_spec = pl.BlockSpec((tm, tk), lambda i, j, k: (i, k))
    if mode == "nt":
        b_spec = pl.BlockSpec((tn, tk), lambda i, j, k: (j, k))
    else:
        b_spec = pl.BlockSpec((tk, tn), lambda i, j, k: (k, j))

    def body(*refs):
        ins, o_ref, acc = refs[:2 * n_pairs], refs[2 * n_pairs], refs[2 * n_pairs + 1]
        k = pl.program_id(2)

        @pl.when(k == 0)
        def _():
            acc[...] = jnp.zeros_like(acc)

        s = acc[...]
        for p in range(n_pairs):
            s = s + _dot(ins[2 * p][...], ins[2 * p + 1][...], mode)
        acc[...] = s

        @pl.when(k == nk - 1)
        def _():
            o_ref[...] = acc[...].astype(out_dtype)

    est = (2 * n_pairs * (_nbytes((tm, tk), a0.dtype) + _nbytes((tk, tn), b0.dtype))
           + 2 * _nbytes((tm, tn), out_dtype) + 2 * _nbytes((tm, tn), F32))
    flat = [t for ab in pairs for t in ab]
    return pl.pallas_call(
        body, name=name, grid=(m_dim // tm, n_dim // tn, nk),
        in_specs=[a_spec, b_spec] * n_pairs,
        out_specs=pl.BlockSpec((tm, tn), lambda i, j, k: (i, j)),
        out_shape=jax.ShapeDtypeStruct((m_dim, n_dim), out_dtype),
        scratch_shapes=[pltpu.VMEM((tm, tn), F32)],
        compiler_params=_cparams(("parallel", "parallel", "arbitrary"), est),
    )(*flat)


def _rms_fwd(x, gain, name):
    s_dim, d = x.shape
    tm = _tile(s_dim, 512, 8)

    def body(x_ref, g_ref, o_ref):
        v = x_ref[...]
        r = lax.rsqrt(jnp.mean(v * v, axis=-1, keepdims=True) + RMS_EPS)
        o_ref[...] = (v * r * g_ref[...]).astype(MXU_DTYPE)

    return pl.pallas_call(
        body, name=name, grid=(s_dim // tm,),
        in_specs=[pl.BlockSpec((tm, d), lambda i: (i, 0)), pl.BlockSpec((1, d), lambda i: (0, 0))],
        out_specs=pl.BlockSpec((tm, d), lambda i: (i, 0)),
        out_shape=jax.ShapeDtypeStruct((s_dim, d), MXU_DTYPE),
        compiler_params=_cparams(("parallel",), 6 * _nbytes((tm, d), F32)),
    )(x, gain)


def _rms_bwd(x, gain, dys, res, scale, out_dtype, name):
    s_dim, d = x.shape
    tm = _tile(s_dim, 512, 8)
    n_dy = len(dys)
    has_res = res is not None

    def body(*refs):
        x_ref, g_ref = refs[0], refs[1]
        dy_refs = refs[2:2 + n_dy]
        res_ref = refs[2 + n_dy] if has_res else None
        dx_ref, dg_ref = refs[-2], refs[-1]

        @pl.when(pl.program_id(0) == 0)
        def _():
            dg_ref[...] = jnp.zeros_like(dg_ref)

        v = x_ref[...]
        r = lax.rsqrt(jnp.mean(v * v, axis=-1, keepdims=True) + RMS_EPS)
        xh = v * r
        dy = dy_refs[0][...].astype(F32)
        for extra in dy_refs[1:]:
            dy = dy + extra[...].astype(F32)
        gd = dy * g_ref[...]
        dx = scale * r * (gd - xh * jnp.mean(gd * xh, axis=-1, keepdims=True))
        if has_res:
            dx = dx + res_ref[...]
        dx_ref[...] = dx.astype(out_dtype)
        dg_ref[...] += scale * jnp.sum(dy * xh, axis=0, keepdims=True)

    row = pl.BlockSpec((tm, d), lambda i: (i, 0))
    vec = pl.BlockSpec((1, d), lambda i: (0, 0))
    ops = [x, gain] + list(dys) + ([res] if has_res else [])
    return pl.pallas_call(
        body, name=name, grid=(s_dim // tm,),
        in_specs=[row, vec] + [row] * (n_dy + int(has_res)),
        out_specs=[row, vec],
        out_shape=[jax.ShapeDtypeStruct((s_dim, d), out_dtype), jax.ShapeDtypeStruct((1, d), F32)],
        compiler_params=_cparams(("arbitrary",), (2 * len(ops) + 6) * _nbytes((tm, d), F32)),
    )(*ops)


def _mm_rms_res(a, b, h, gain, scale, name):
    s_dim, k_dim = a.shape
    d = b.shape[1]
    tm = _tile(s_dim, 512, 8)
    tk = _tile(k_dim, 1408)
    nk = k_dim // tk

    def body(a_ref, b_ref, h_ref, g_ref, f_ref, o_ref, acc):
        k = pl.program_id(1)

        @pl.when(k == 0)
        def _():
            acc[...] = jnp.zeros_like(acc)

        acc[...] += _dot(a_ref[...], b_ref[...], "nn")

        @pl.when(k == nk - 1)
        def _():
            f = acc[...]
            r = lax.rsqrt(jnp.mean(f * f, axis=-1, keepdims=True) + RMS_EPS)
            f_ref[...] = f
            o_ref[...] = h_ref[...] + scale * (f * r * g_ref[...])

    row = pl.BlockSpec((tm, d), lambda i, k: (i, 0))
    est = (2 * (_nbytes((tm, tk), a.dtype) + _nbytes((tk, d), b.dtype)) + 8 * _nbytes((tm, d), F32))
    return pl.pallas_call(
        body, name=name, grid=(s_dim // tm, nk),
        in_specs=[pl.BlockSpec((tm, tk), lambda i, k: (i, k)), pl.BlockSpec((tk, d), lambda i, k: (k, 0)),
                  row, pl.BlockSpec((1, d), lambda i, k: (0, 0))],
        out_specs=[row, row],
        out_shape=[jax.ShapeDtypeStruct((s_dim, d), F32), jax.ShapeDtypeStruct((s_dim, d), F32)],
        scratch_shapes=[pltpu.VMEM((tm, d), F32)],
        compiler_params=_cparams(("parallel", "arbitrary"), est),
    )(a, b, h, gain)


def _ffn_up(xn, wg_t, wu_t, name):
    s_dim, d = xn.shape
    f_dim = wg_t.shape[0]
    tm = _tile(s_dim, 1024, 8)
    tf = _tile(f_dim, 256)

    def body(x_ref, wg_ref, wu_ref, g_ref, u_ref, a_ref):
        x = x_ref[...]
        g = _dot(x, wg_ref[...], "nt")
        u = _dot(x, wu_ref[...], "nt")
        g_ref[...] = g.astype(MXU_DTYPE)
        u_ref[...] = u.astype(MXU_DTYPE)
        a_ref[...] = (g * _sigmoid(g) * u).astype(MXU_DTYPE)

    w_spec = pl.BlockSpec((tf, d), lambda i, j: (j, 0))
    o_spec = pl.BlockSpec((tm, tf), lambda i, j: (i, j))
    o_shape = jax.ShapeDtypeStruct((s_dim, f_dim), MXU_DTYPE)
    est = 2 * _nbytes((tm, d), xn.dtype) + 4 * _nbytes((tf, d), wg_t.dtype) + 10 * _nbytes((tm, tf), F32)
    return pl.pallas_call(
        body, name=name, grid=(s_dim // tm, f_dim // tf),
        in_specs=[pl.BlockSpec((tm, d), lambda i, j: (i, 0)), w_spec, w_spec],
        out_specs=[o_spec, o_spec, o_spec], out_shape=[o_shape, o_shape, o_shape],
        compiler_params=_cparams(("parallel", "parallel"), est),
    )(xn, wg_t, wu_t)


def _ffn_act_bwd(df, wd, g, u, name):
    s_dim, d = df.shape
    f_dim = wd.shape[0]
    tm = _tile(s_dim, 1024, 8)
    tf = _tile(f_dim, 256)

    def body(df_ref, wd_ref, g_ref, u_ref, dg_ref, du_ref):
        dh = _dot(df_ref[...], wd_ref[...], "nt")
        gv = g_ref[...].astype(F32)
        uv = u_ref[...].astype(F32)
        sg = _sigmoid(gv)
        dg_ref[...] = (dh * uv * (sg * (1.0 + gv * (1.0 - sg)))).astype(MXU_DTYPE)
        du_ref[...] = (dh * gv * sg).astype(MXU_DTYPE)

    t_spec = pl.BlockSpec((tm, tf), lambda i, j: (i, j))
    o_shape = jax.ShapeDtypeStruct((s_dim, f_dim), MXU_DTYPE)
    est = 2 * _nbytes((tm, d), df.dtype) + 2 * _nbytes((tf, d), wd.dtype) + 12 * _nbytes((tm, tf), F32)
    return pl.pallas_call(
        body, name=name, grid=(s_dim // tm, f_dim // tf),
        in_specs=[pl.BlockSpec((tm, d), lambda i, j: (i, 0)), pl.BlockSpec((tf, d), lambda i, j: (j, 0)),
                  t_spec, t_spec],
        out_specs=[t_spec, t_spec], out_shape=[o_shape, o_shape],
        compiler_params=_cparams(("parallel", "parallel"), est),
    )(df, wd, g, u)


def _loss_head(y, target, name):
    s_dim, d = y.shape
    tm = _tile(s_dim, 512, 8)
    nt = s_dim // tm

    def body(y_ref, t_ref, dy_ref, loss_ref, acc):
        i = pl.program_id(0)

        @pl.when(i == 0)
        def _():
            acc[...] = jnp.zeros_like(acc)

        e = y_ref[...] - t_ref[...]
        dy_ref[...] = e * (1.0 / d)
        acc[...] += jnp.sum(e * e, axis=0, keepdims=True)

        @pl.when(i == nt - 1)
        def _():
            loss_ref[...] = jnp.sum(acc[...], axis=1, keepdims=True) * (0.5 / d)

    row = pl.BlockSpec((tm, d), lambda i: (i, 0))
    return pl.pallas_call(
        body, name=name, grid=(nt,), in_specs=[row, row],
        out_specs=[row, pl.BlockSpec((1, 1), lambda i: (0, 0))],
        out_shape=[jax.ShapeDtypeStruct((s_dim, d), F32), jax.ShapeDtypeStruct((1, 1), F32)],
        scratch_shapes=[pltpu.VMEM((1, d), F32)],
        compiler_params=_cparams(("arbitrary",), 8 * _nbytes((tm, d), F32)),
    )(y, target)


def _shift_down(v, sh, row):
    if sh == 0:
        return v
    return jnp.where(row >= sh, pltpu.roll(v, sh, 0), 0.0)


def _shift_up(v, sh, row):
    if sh == 0:
        return v
    n = v.shape[0]
    return jnp.where(row < n - sh, pltpu.roll(v, n - sh, 0), 0.0)


def _conv_fwd(gx, pconv, width, name):
    s_dim, cp2 = gx.shape
    cp = cp2 // 2
    nc = cp // LANES

    def body(x_ref, p_ref, o_ref):
        x = x_ref[...]
        row = lax.broadcasted_iota(jnp.int32, x.shape, 0)
        y = jnp.zeros_like(x) + p_ref[pl.ds(width, 1), :]
        for k in range(width):
            y = y + p_ref[pl.ds(k, 1), :] * _shift_down(x, width - 1 - k, row)
        o_ref[...] = y

    return pl.pallas_call(
        body, name=name, grid=(nc,),
        in_specs=[pl.BlockSpec((s_dim, LANES), lambda j: (0, nc + j)), pl.BlockSpec((8, LANES), lambda j: (0, j))],
        out_specs=pl.BlockSpec((s_dim, LANES), lambda j: (0, j)),
        out_shape=jax.ShapeDtypeStruct((s_dim, cp), F32),
        compiler_params=_cparams(("parallel",), 10 * _nbytes((s_dim, LANES), F32)),
    )(gx, pconv)


def _conv_bwd(d1, d2, gx, pconv, width, name):
    s_dim, cp = d1.shape
    nc = cp // LANES

    def body(d1_ref, d2_ref, x_ref, p_ref, dx_ref, dp_ref):
        d = d1_ref[...] + d2_ref[...]
        x = x_ref[...]
        row = lax.broadcasted_iota(jnp.int32, x.shape, 0)
        dx = jnp.zeros_like(d)
        dp_ref[...] = jnp.zeros_like(dp_ref)
        for k in range(width):
            sh = width - 1 - k
            dx = dx + p_ref[pl.ds(k, 1), :] * _shift_up(d, sh, row)
            dp_ref[pl.ds(k, 1), :] = jnp.sum(d * _shift_down(x, sh, row), axis=0, keepdims=True)
        dp_ref[pl.ds(width, 1), :] = jnp.sum(d, axis=0, keepdims=True)
        dx_ref[...] = dx.astype(MXU_DTYPE)

    strip = pl.BlockSpec((s_dim, LANES), lambda j: (0, j))
    par = pl.BlockSpec((8, LANES), lambda j: (0, j))
    return pl.pallas_call(
        body, name=name, grid=(nc,),
        in_specs=[strip, strip, pl.BlockSpec((s_dim, LANES), lambda j: (0, nc + j)), par],
        out_specs=[strip, par],
        out_shape=[jax.ShapeDtypeStruct((s_dim, cp), MXU_DTYPE), jax.ShapeDtypeStruct((8, cp), F32)],
        compiler_params=_cparams(("parallel",), 14 * _nbytes((s_dim, LANES), F32)),
    )(d1, d2, gx, pconv)


def _lru_coeffs(ra, ia, p_ref):
    r = _sigmoid(ra + p_ref[pl.ds(0, 1), :])
    i = _sigmoid(ia + p_ref[pl.ds(1, 1), :])
    sp = _softplus(-p_ref[pl.ds(2, 1), :])
    log_a = -LRU_C * r * sp
    a = jnp.exp(log_a)
    mult = jnp.sqrt(-_expm1(2.0 * log_a))
    return r, i, sp, a, mult


def _scan_fwd(gx, rec, gates, pvec, name):
    s_dim, cp = rec.shape
    nc = cp // LANES
    ts = _tile(s_dim, 512, 8)
    nt = s_dim // ts

    def body(gate_ref, rec_ref, ra_ref, ia_ref, p_ref, h_ref, y_ref, a_s, u_s, carry):
        @pl.when(pl.program_id(1) == 0)
        def _():
            carry[...] = jnp.zeros_like(carry)

        rec_v = rec_ref[...]
        _, i, _, a, mult = _lru_coeffs(ra_ref[...], ia_ref[...], p_ref)
        a_s[...] = a
        u_s[...] = mult * (i * rec_v)

        def step(t, h):
            h = a_s[pl.ds(t, 1), :] * h + u_s[pl.ds(t, 1), :]
            h_ref[pl.ds(t, 1), :] = h
            return h

        carry[pl.ds(0, 1), :] = lax.fori_loop(0, ts, step, carry[pl.ds(0, 1), :], unroll=8)
        y_ref[...] = (_gelu(gate_ref[...]) * h_ref[...]).astype(MXU_DTYPE)

    blk = pl.BlockSpec((ts, LANES), lambda j, t: (t, j))
    return pl.pallas_call(
        body, name=name, grid=(nc, nt),
        in_specs=[blk, blk, blk, pl.BlockSpec((ts, LANES), lambda j, t: (t, nc + j)),
                  pl.BlockSpec((8, LANES), lambda j, t: (0, j))],
        out_specs=[blk, blk],
        out_shape=[jax.ShapeDtypeStruct((s_dim, cp), F32), jax.ShapeDtypeStruct((s_dim, cp), MXU_DTYPE)],
        scratch_shapes=[pltpu.VMEM((ts, LANES), F32), pltpu.VMEM((ts, LANES), F32), pltpu.VMEM((8, LANES), F32)],
        compiler_params=_cparams(("parallel", "arbitrary"), 30 * _nbytes((ts, LANES), F32)),
    )(gx, rec, gates, gates, pvec)


def _scan_bwd(dy, gx, hrec, rec, gates, pvec, name):
    s_dim, cp = rec.shape
    nc = cp // LANES
    ts = _tile(s_dim, 512, 8)
    nt = s_dim // ts

    def body(dy_ref, gate_ref, h_ref, hp_ref, rec_ref, ra_ref, ia_ref, p_ref,
             dgate_ref, dra_ref, dia_ref, drec_ref, dp_ref, a_s, d_s, carry):
        t_id = pl.program_id(1)

        @pl.when(t_id == 0)
        def _():
            carry[...] = jnp.zeros_like(carry)
            dp_ref[...] = jnp.zeros_like(dp_ref)

        rec_v = rec_ref[...]
        r, i, sp, a, mult = _lru_coeffs(ra_ref[...], ia_ref[...], p_ref)
        gate = gate_ref[...]
        dyv = dy_ref[...]
        h = h_ref[...]
        dgate_ref[...] = (dyv * h * _gelu_grad(gate)).astype(MXU_DTYPE)
        a_s[...] = a
        d_s[...] = dyv * _gelu(gate)

        def step(k, c):
            t = ts - 1 - k
            d = d_s[pl.ds(t, 1), :] + c
            d_s[pl.ds(t, 1), :] = d
            return a_s[pl.ds(t, 1), :] * d

        carry[pl.ds(0, 1), :] = lax.fori_loop(0, ts, step, carry[pl.ds(0, 1), :], unroll=8)
        dh = d_s[...]
        row = lax.broadcasted_iota(jnp.int32, h.shape, 0)
        first = jnp.where(t_id == nt - 1, 0.0, 1.0) * hp_ref[pl.ds(ts - 1, 1), :]
        h_prev = jnp.where(row == 0, first, pltpu.roll(h, 1, 0))
        dix = dh * mult
        dla = dh * h_prev * a - dh * (i * rec_v) * (a * a) / mult
        dra = dla * (-LRU_C * sp) * r * (1.0 - r)
        dia = dix * rec_v * i * (1.0 - i)
        dra_ref[...] = dra.astype(MXU_DTYPE)
        dia_ref[...] = dia.astype(MXU_DTYPE)
        drec_ref[...] = dix * i
        dsp = jnp.sum(dla * (-LRU_C * r), axis=0, keepdims=True)
        dp_ref[pl.ds(0, 1), :] += jnp.sum(dra, axis=0, keepdims=True)
        dp_ref[pl.ds(1, 1), :] += jnp.sum(dia, axis=0, keepdims=True)
        dp_ref[pl.ds(2, 1), :] += dsp * (-_sigmoid(-p_ref[pl.ds(2, 1), :]))

    blk = pl.BlockSpec((ts, LANES), lambda j, t: (nt - 1 - t, j))
    prev = pl.BlockSpec((ts, LANES), lambda j, t: (jnp.maximum(nt - 2 - t, 0), j))
    par = pl.BlockSpec((8, LANES), lambda j, t: (0, j))
    lo = jax.ShapeDtypeStruct((s_dim, cp), MXU_DTYPE)
    return pl.pallas_call(
        body, name=name, grid=(nc, nt),
        in_specs=[blk, blk, blk, prev, blk, blk, pl.BlockSpec((ts, LANES), lambda j, t: (nt - 1 - t, nc + j)), par],
        out_specs=[blk, blk, blk, blk, par],
        out_shape=[lo, lo, lo, jax.ShapeDtypeStruct((s_dim, cp), F32), jax.ShapeDtypeStruct((8, cp), F32)],
        scratch_shapes=[pltpu.VMEM((ts, LANES), F32), pltpu.VMEM((ts, LANES), F32), pltpu.VMEM((8, LANES), F32)],
        compiler_params=_cparams(("parallel", "arbitrary"), 48 * _nbytes((ts, LANES), F32)),
    )(dy, gx, hrec, hrec, rec, gates, gates, pvec)


def _fgate_fwd(fpre, bias, name):
    s_dim, w = fpre.shape
    ts = _tile(s_dim, 512, 8)

    def body(f_ref, b_ref, c_ref, lf_s, carry):
        @pl.when(pl.program_id(0) == 0)
        def _():
            carry[...] = jnp.zeros_like(carry)

        lf_s[...] = -_softplus(-(f_ref[...] + b_ref[pl.ds(0, 1), :]))

        def step(t, c):
            c = c + lf_s[pl.ds(t, 1), :]
            c_ref[pl.ds(t, 1), :] = c
            return c

        carry[pl.ds(0, 1), :] = lax.fori_loop(0, ts, step, carry[pl.ds(0, 1), :], unroll=8)

    blk = pl.BlockSpec((ts, w), lambda t: (t, 0))
    return pl.pallas_call(
        body, name=name, grid=(s_dim // ts,),
        in_specs=[blk, pl.BlockSpec((8, w), lambda t: (0, 0))], out_specs=blk,
        out_shape=jax.ShapeDtypeStruct((s_dim, w), F32),
        scratch_shapes=[pltpu.VMEM((ts, w), F32), pltpu.VMEM((8, w), F32)],
        compiler_params=_cparams(("arbitrary",), 12 * _nbytes((ts, w), F32)),
    )(fpre, bias)


def _fgate_bwd(dc, fpre, bias, name):
    s_dim, w = fpre.shape
    ts = _tile(s_dim, 512, 8)
    nt = s_dim // ts

    def body(dc_ref, f_ref, b_ref, df_ref, db_ref, d_s, carry):
        @pl.when(pl.program_id(0) == 0)
        def _():
            carry[...] = jnp.zeros_like(carry)
            db_ref[...] = jnp.zeros_like(db_ref)

        d_s[...] = dc_ref[...]

        def step(k, c):
            t = ts - 1 - k
            c = c + d_s[pl.ds(t, 1), :]
            d_s[pl.ds(t, 1), :] = c
            return c

        carry[pl.ds(0, 1), :] = lax.fori_loop(0, ts, step, carry[pl.ds(0, 1), :], unroll=8)
        df = d_s[...] * _sigmoid(-(f_ref[...] + b_ref[pl.ds(0, 1), :]))
        df_ref[...] = df
        db_ref[pl.ds(0, 1), :] += jnp.sum(df, axis=0, keepdims=True)

    blk = pl.BlockSpec((ts, w), lambda t: (nt - 1 - t, 0))
    par = pl.BlockSpec((8, w), lambda t: (0, 0))
    return pl.pallas_call(
        body, name=name, grid=(nt,), in_specs=[blk, blk, par], out_specs=[blk, par],
        out_shape=[jax.ShapeDtypeStruct((s_dim, w), F32), jax.ShapeDtypeStruct((8, w), F32)],
        scratch_shapes=[pltpu.VMEM((ts, w), F32), pltpu.VMEM((8, w), F32)],
        compiler_params=_cparams(("arbitrary",), 12 * _nbytes((ts, w), F32)),
    )(dc, fpre, bias)


def _attn_logits(q, k, cq, ck, scale, i, j, t):
    s = _dot(q, k, "nt") * scale + (cq - ck)
    row = i * t + lax.broadcasted_iota(jnp.int32, (t, t), 0)
    col = j * t + lax.broadcasted_iota(jnp.int32, (t, t), 1)
    return jnp.where(col <= row, s, -jnp.inf)


def _flash_fwd(q, k, v, c_col, c_row, scale, name):
    n_h, s_dim, dh = q.shape
    t = _tile(s_dim, 512, LANES)
    nb = s_dim // t

    def body(q_ref, k_ref, v_ref, cq_ref, ck_ref, o_ref, lse_ref, m_s, l_s, acc):
        i, j = pl.program_id(1), pl.program_id(2)

        @pl.when(j == 0)
        def _():
            m_s[...] = jnp.full_like(m_s, -jnp.inf)
            l_s[...] = jnp.zeros_like(l_s)
            acc[...] = jnp.zeros_like(acc)

        @pl.when(j <= i)
        def _():
            s = _attn_logits(q_ref[0], k_ref[0], cq_ref[0], ck_ref[0], scale, i, j, t)
            m_new = jnp.maximum(m_s[...], jnp.max(s, axis=-1, keepdims=True))
            alpha = jnp.exp(m_s[...] - m_new)
            p = jnp.exp(s - m_new)
            l_s[...] = alpha * l_s[...] + jnp.sum(p, axis=-1, keepdims=True)
            acc[...] = alpha * acc[...] + _dot(p, v_ref[0], "nn")
            m_s[...] = m_new

        @pl.when(j == nb - 1)
        def _():
            o_ref[0] = acc[...] / l_s[...]
            lse_ref[0] = m_s[...] + jnp.log(l_s[...])

    q_spec = pl.BlockSpec((1, t, dh), lambda h, i, j: (h, i, 0))
    kv_spec = pl.BlockSpec((1, t, dh), lambda h, i, j: (h, jnp.minimum(j, i), 0))
    col_spec = pl.BlockSpec((1, t, 1), lambda h, i, j: (h, i, 0))
    row_spec = pl.BlockSpec((1, 1, t), lambda h, i, j: (h, 0, jnp.minimum(j, i)))
    return pl.pallas_call(
        body, name=name, grid=(n_h, nb, nb),
        in_specs=[q_spec, kv_spec, kv_spec, col_spec, row_spec],
        out_specs=[q_spec, col_spec],
        out_shape=[jax.ShapeDtypeStruct((n_h, s_dim, dh), F32), jax.ShapeDtypeStruct((n_h, s_dim, 1), F32)],
        scratch_shapes=[pltpu.VMEM((t, 1), F32), pltpu.VMEM((t, 1), F32), pltpu.VMEM((t, dh), F32)],
        compiler_params=_cparams(("parallel", "parallel", "arbitrary"), 12 * _nbytes((t, t), F32)),
    )(q, k, v, c_col, c_row)


def _flash_bwd(q, k, v, c_col, c_row, o, lse, do, scale, name):
    n_h, s_dim, dh = q.shape
    t = _tile(s_dim, 512, LANES)
    nb = s_dim // t

    def body(q_ref, k_ref, v_ref, cq_ref, ck_ref, o_ref, lse_ref, do_ref,
             dq_ref, dcq_ref, dk_ref, dv_ref, dc_ref, dk_acc, dv_acc, dc_acc):
        j, i = pl.program_id(1), pl.program_id(2)

        @pl.when((j == 0) & (i == 0))
        def _():
            dq_ref[...] = jnp.zeros_like(dq_ref)
            dcq_ref[...] = jnp.zeros_like(dcq_ref)

        @pl.when(i == 0)
        def _():
            dk_acc[...] = jnp.zeros_like(dk_acc)
            dv_acc[...] = jnp.zeros_like(dv_acc)
            dc_acc[...] = jnp.zeros_like(dc_acc)

        @pl.when(i >= j)
        def _():
            qv, kv, dov = q_ref[0], k_ref[0], do_ref[0]
            s = _attn_logits(qv, kv, cq_ref[0], ck_ref[0], scale, i, j, t)
            p = jnp.exp(s - lse_ref[0])
            delta = jnp.sum(dov.astype(MXU_DTYPE).astype(F32) * o_ref[0], axis=-1, keepdims=True)
            ds = p * (_dot(dov, v_ref[0], "nt") - delta)
            dv_acc[...] += _dot(p, dov, "tn")
            dk_acc[...] += _dot(ds, qv, "tn") * scale
            start = pl.multiple_of(i * t, t)
            dq_ref[0, pl.ds(start, t), :] += _dot(ds, kv, "nn") * scale
            dcq_ref[0, pl.ds(start, t), :] += jnp.sum(ds, axis=-1, keepdims=True)
            dc_acc[...] -= jnp.sum(ds, axis=0, keepdims=True)

        @pl.when(i == nb - 1)
        def _():
            dk_ref[0] = dk_acc[...]
            dv_ref[0] = dv_acc[...]
            dc_ref[0] = dc_acc[...]

    q_spec = pl.BlockSpec((1, t, dh), lambda h, j, i: (h, jnp.maximum(i, j), 0))
    col_spec = pl.BlockSpec((1, t, 1), lambda h, j, i: (h, jnp.maximum(i, j), 0))
    k_spec = pl.BlockSpec((1, t, dh), lambda h, j, i: (h, j, 0))
    row_spec = pl.BlockSpec((1, 1, t), lambda h, j, i: (h, 0, j))
    full = jax.ShapeDtypeStruct((n_h, s_dim, dh), F32)
    return pl.pallas_call(
        body, name=name, grid=(n_h, nb, nb),
        in_specs=[q_spec, k_spec, k_spec, col_spec, row_spec, q_spec, col_spec, q_spec],
        out_specs=[pl.BlockSpec((1, s_dim, dh), lambda h, j, i: (h, 0, 0)),
                   pl.BlockSpec((1, s_dim, 1), lambda h, j, i: (h, 0, 0)), k_spec, k_spec, row_spec],
        out_shape=[full, jax.ShapeDtypeStruct((n_h, s_dim, 1), F32), full, full,
                   jax.ShapeDtypeStruct((n_h, 1, s_dim), F32)],
        scratch_shapes=[pltpu.VMEM((t, dh), F32), pltpu.VMEM((t, dh), F32), pltpu.VMEM((1, t), F32)],
        compiler_params=_cparams(("parallel", "arbitrary", "arbitrary"),
                                 16 * _nbytes((t, t), F32) + 8 * _nbytes((s_dim, LANES), F32)),
    )(q, k, v, c_col, c_row, o, lse, do)


_HBM = pl.BlockSpec(memory_space=pltpu.HBM)
_MESH_ID = pl.DeviceIdType.MESH


def _all_gather(block, name):
    r, w = block.shape

    def body(x_ref, out_ref, send_sems, recv_sems, local_sem):
        x, y, c = lax.axis_index("x"), lax.axis_index("y"), lax.axis_index("c")
        me, sibling = (x, y, c), (x, y, 1 - c)
        chips = [(1 - x, y), (x, 1 - y), (1 - x, 1 - y)]

        def slot(px, py, pc):
            return out_ref.at[4 * px + 2 * py + pc]

        def copy(k, blk, to, src=None):
            return pltpu.make_async_remote_copy(
                src_ref=slot(*blk) if src is None else src, dst_ref=slot(*blk),
                send_sem=send_sems.at[k], recv_sem=recv_sems.at[k], device_id=to, device_id_type=_MESH_ID)

        mine = pltpu.make_async_copy(x_ref, slot(*me), local_sem)
        mine.start()
        first = [copy(0, me, sibling, src=x_ref)]
        first += [copy(1 + n, me, (*chip, c), src=x_ref) for n, chip in enumerate(chips)]
        for cp in first:
            cp.start()
        passed = [copy(4 + n, (*chip, c), sibling) for n, chip in enumerate(chips)]
        for n, chip in enumerate(chips):
            copy(1 + n, (*chip, c), me).wait_recv()
            passed[n].start()
        copy(0, sibling, me).wait_recv()
        for n, chip in enumerate(chips):
            copy(4 + n, (*chip, 1 - c), me).wait_recv()
        for cp in first + passed:
            cp.wait_send()
        mine.wait()

    return pl.pallas_call(
        body, name=name, out_shape=jax.ShapeDtypeStruct((N_DEV, r, w), block.dtype),
        in_specs=[_HBM], out_specs=_HBM,
        scratch_shapes=[pltpu.SemaphoreType.DMA((7,)), pltpu.SemaphoreType.DMA((7,)), pltpu.SemaphoreType.DMA],
    )(block)


def _all_to_all(blocks, name):
    def body(x_ref, out_ref, send_sems, recv_sems, local_sem):
        x, y, c = lax.axis_index("x"), lax.axis_index("y"), lax.axis_index("c")
        mine = 4 * x + 2 * y + c
        local = pltpu.make_async_copy(x_ref.at[mine], out_ref.at[mine], local_sem)
        local.start()
        copies = []
        for k in range(1, N_DEV):
            px = 1 - x if k & 4 else x
            py = 1 - y if k & 2 else y
            pc = 1 - c if k & 1 else c
            cp = pltpu.make_async_remote_copy(
                src_ref=x_ref.at[4 * px + 2 * py + pc], dst_ref=out_ref.at[mine],
                send_sem=send_sems.at[k - 1], recv_sem=recv_sems.at[k - 1],
                device_id=(px, py, pc), device_id_type=_MESH_ID)
            cp.start()
            copies.append(cp)
        for cp in copies:
            cp.wait_recv()
        for cp in copies:
            cp.wait_send()
        local.wait()

    return pl.pallas_call(
        body, name=name, out_shape=jax.ShapeDtypeStruct(blocks.shape, blocks.dtype),
        in_specs=[_HBM], out_specs=_HBM,
        scratch_shapes=[pltpu.SemaphoreType.DMA((7,)), pltpu.SemaphoreType.DMA((7,)), pltpu.SemaphoreType.DMA],
    )(blocks)


def _sum_slots(slots, name):
    n, r, w = slots.shape
    tr = _tile(r, 128, WIRE_ROW_ALIGN)

    def body(s_ref, o_ref):
        acc = s_ref[0].astype(F32)
        for d in range(1, n):
            acc = acc + s_ref[d].astype(F32)
        o_ref[...] = acc

    return pl.pallas_call(
        body, name=name, grid=(r // tr,),
        in_specs=[pl.BlockSpec((n, tr, w), lambda i: (0, i, 0))],
        out_specs=pl.BlockSpec((tr, w), lambda i: (i, 0)),
        out_shape=jax.ShapeDtypeStruct((r, w), F32),
        compiler_params=_cparams(("parallel",), 2 * _nbytes((n, tr, w), slots.dtype) + 4 * _nbytes((tr, w), F32)),
    )(slots)


def _adamw(w, g, m, v, name):
    r, c = w.shape
    tr = _tile(r, 512, 8)

    def body(w_ref, g_ref, m_ref, v_ref, d_ref, mo_ref, vo_ref):
        gv = g_ref[...]
        m_new = ADAM_B1 * m_ref[...] + (1.0 - ADAM_B1) * gv
        v_new = ADAM_B2 * v_ref[...] + (1.0 - ADAM_B2) * (gv * gv)
        m_hat = m_new / (1.0 - ADAM_B1 ** ADAM_STEP)
        v_hat = v_new / (1.0 - ADAM_B2 ** ADAM_STEP)
        d_ref[...] = -ADAM_LR * (m_hat / (jnp.sqrt(v_hat) + ADAM_EPS) + ADAM_WD * w_ref[...])
        mo_ref[...] = m_new
        vo_ref[...] = v_new

    blk = pl.BlockSpec((tr, c), lambda i: (i, 0))
    shp = jax.ShapeDtypeStruct((r, c), F32)
    return pl.pallas_call(
        body, name=name, grid=(r // tr,), in_specs=[blk] * 4, out_specs=[blk] * 3, out_shape=[shp] * 3,
        compiler_params=_cparams(("parallel",), 16 * _nbytes((tr, _round_up(c, LANES)), F32)),
    )(w, g, m, v)


def _pack_rows(parts, width, dtype, row_align):
    rows, spans, off = [], [], 0
    for p in parts:
        flat = p.reshape(-1).astype(dtype)
        n_rows = _round_up(-(-flat.shape[0] // width), row_align)
        flat = jnp.pad(flat, (0, n_rows * width - flat.shape[0]))
        rows.append(flat.reshape(n_rows, width))
        spans.append((off, n_rows))
        off += n_rows
    return jnp.concatenate(rows, axis=0), spans


def _unpack_rows(mat, span, shape):
    off, n_rows = span
    n = 1
    for s in shape:
        n *= s
    return mat[..., off:off + n_rows, :].reshape(mat.shape[:-2] + (-1,))[..., :n].reshape(mat.shape[:-2] + tuple(shape))


def _block_diag(w, size):
    n, b, _ = w.shape
    eye = jnp.eye(n, dtype=w.dtype)
    dense = (w[:, :, None, :] * eye[:, None, :, None]).reshape(n * b, n * b)
    return jnp.pad(dense, ((0, size - n * b), (0, size - n * b)))


def _diag_blocks(dense, n, b):
    return jnp.stack([dense[k * b:(k + 1) * b, k * b:(k + 1) * b] for k in range(n)])


def _pad_rows(a, rows):
    return jnp.pad(a, ((0, rows - a.shape[0]), (0, 0)))


def _pad_cols(a, cols):
    return jnp.pad(a, ((0, 0), (0, cols - a.shape[1])))


def _train_step(a):
    x = a["x"][0]
    target = a["loss_target"][0]
    s_dim, d = x.shape
    n_layers = a["ffn1_pre_g"].shape[0]
    f_shard = a["ffn1_w_gate"].shape[2]
    c_shard = a["rg_conv_b"].shape[1]
    c_dim = c_shard * N_DEV
    cp = _round_up(c_dim, LANES)
    conv_width = a["rg_conv_w"].shape[1]
    n_blocks, lru_block = a["rg_w_a"].shape[1], a["rg_w_a"].shape[2]
    d_attn = a["attn_w_q"].shape[2]
    n_heads = a["b_fgate"].shape[0]
    d_head = d_attn // n_heads
    attn_scale = d_head ** -0.5
    assert conv_width < 8 and n_heads <= LANES and n_layers == 2
    assert d_attn == d
    me = 4 * lax.axis_index("x") + 2 * lax.axis_index("y") + lax.axis_index("c")

    big_names, big_parts = [], []
    for l in range(n_layers):
        for f in ("ffn1", "ffn2"):
            big_names += [(f, "gate", l), (f, "up", l), (f, "down", l)]
            big_parts += [a[f + "_w_gate"][l].T, a[f + "_w_up"][l].T, a[f + "_w_down"][l]]
    big_names += ["rg_w_in", "rg_w_out", "w_kv", "attn_w_q", "attn_w_o"]
    big_parts += [a["rg_w_in"][0].T, a["rg_w_out"][0], a["w_kv"].T, a["attn_w_q"][0], a["attn_w_o"][0]]
    big_rows = [p.shape[0] for p in big_parts]
    big_offs, off = [], 0
    for n_rows in big_rows:
        big_offs.append(off)
        off += _round_up(n_rows, WIRE_ROW_ALIGN)
    big_total = off
    pack = jnp.concatenate([_pad_rows(p.astype(WIRE_DTYPE), _round_up(p.shape[0], WIRE_ROW_ALIGN)) for p in big_parts], axis=0)
    gathered = _all_gather(pack, "gather_weights")
    full = {}
    for nm, o, n_rows in zip(big_names, big_offs, big_rows):
        full[nm] = gathered[:, o:o + n_rows, :].reshape(N_DEV * n_rows, d)

    small_parts = [a["rg_conv_w"][0], a["rg_conv_b"][0], a["rg_b_a"][0], a["rg_b_x"][0], a["rg_lambda"][0], a["w_fgate"]]
    small_pack, small_spans = _pack_rows(small_parts, d, F32, 8)
    small_all = _all_gather(small_pack, "gather_small")
    sm = [_unpack_rows(small_all, sp, p.shape) for sp, p in zip(small_spans, small_parts)]
    conv_w = jnp.moveaxis(sm[0], 0, 1).reshape(conv_width, c_dim)
    conv_b, b_a, b_x, lam = (v.reshape(1, c_dim) for v in sm[1:5])
    w_f = sm[5].reshape(d, n_heads)

    pconv = _pad_rows(_pad_cols(jnp.concatenate([conv_w, conv_b], axis=0), cp), 8)
    pvec = _pad_rows(_pad_cols(jnp.concatenate([b_a, b_x, lam], axis=0), cp), 8)
    wa_dense = _block_diag(a["rg_w_a"][0], cp).astype(MXU_DTYPE)
    wx_dense = _block_diag(a["rg_w_x"][0], cp).astype(MXU_DTYPE)
    wax = jnp.concatenate([wa_dense, wx_dense], axis=1)
    w_in_gate = _pad_rows(full["rg_w_in"][:c_dim], cp)
    w_in_rec = _pad_rows(full["rg_w_in"][c_dim:], cp)
    w_in_t = jnp.concatenate([w_in_gate, w_in_rec], axis=0)
    w_out = _pad_rows(full["rg_w_out"], cp)
    w_f_t = _pad_rows(w_f.T.astype(MXU_DTYPE), LANES)
    b_f = _pad_rows(_pad_cols(a["b_fgate"].reshape(1, n_heads), LANES), 8)

    def gain(name, l):
        return a[name][l].reshape(1, d)

    def ffn_fwd(h, f, l):
        xn = _rms_fwd(h, gain(f + "_pre_g", l), f"{f}_{l}_pre_norm")
        g, u, act = _ffn_up(xn, full[(f, "gate", l)], full[(f, "up", l)], f"{f}_{l}_up")
        fo, h_new = _mm_rms_res(act, full[(f, "down", l)], h, gain(f + "_post_g", l), 0.5, f"{f}_{l}_down")
        return h_new, (h, xn, g, u, act, fo)

    def heads(t2):
        return t2.reshape(s_dim, n_heads, d_head).transpose(1, 0, 2)

    def unheads(t3):
        return t3.transpose(1, 0, 2).reshape(s_dim, n_heads * d_head)

    h0 = x
    h0a, sv_f1_0 = ffn_fwd(h0, "ffn1", 0)
    hn_rg = _rms_fwd(h0a, gain("mix_pre_g", 0), "rg_pre_norm")
    gx = _mm([(hn_rg, w_in_t)], "nt", F32, "rg_in_proj")
    rec = _conv_fwd(gx, pconv, conv_width, "rg_conv")
    gates = _mm([(rec, wax)], "nn", F32, "rg_gate_proj")
    h_rec, y_rg = _scan_fwd(gx, rec, gates, pvec, "rg_scan")
    m_rg, h0b = _mm_rms_res(y_rg, w_out, h0a, gain("mix_post_g", 0), 1.0, "rg_out_proj")
    h1, sv_f2_0 = ffn_fwd(h0b, "ffn2", 0)
    hn_kv = _rms_fwd(h1, a["kv_norm_g"].reshape(1, d), "kv_norm")
    kv = _mm([(hn_kv, full["w_kv"])], "nt", MXU_DTYPE, "kv_proj")
    fpre = _mm([(hn_kv, w_f_t)], "nt", F32, "fgate_proj")
    c_cum = _fgate_fwd(fpre, b_f, "fgate_cumsum")
    k_h, v_h = heads(kv[:, :d_attn]), heads(kv[:, d_attn:])
    c_heads = c_cum[:, :n_heads].T
    c_col, c_row = c_heads[:, :, None], c_heads[:, None, :]
    h1a, sv_f1_1 = ffn_fwd(h1, "ffn1", 1)
    hn_at = _rms_fwd(h1a, gain("mix_pre_g", 1), "attn_pre_norm")
    q2 = _mm([(hn_at, full["attn_w_q"])], "nn", MXU_DTYPE, "q_proj")
    q_h = heads(q2)
    o_h, lse = _flash_fwd(q_h, k_h, v_h, c_col, c_row, attn_scale, "attn_fwd")
    o2 = unheads(o_h)
    m_at, h1b = _mm_rms_res(o2, full["attn_w_o"], h1a, gain("mix_post_g", 1), 1.0, "attn_out_proj")
    y, sv_f2_1 = ffn_fwd(h1b, "ffn2", 1)
    dy, loss_part = _loss_head(y, target, "loss_head")

    grads_big = {}
    grads_rep = {}

    def ffn_bwd(dh_out, saved, f, l):
        h, xn, g, u, act, fo = saved
        df, d_post = _rms_bwd(fo, gain(f + "_post_g", l), [dh_out], None, 0.5, MXU_DTYPE, f"{f}_{l}_post_norm_bwd")
        dg, du = _ffn_act_bwd(df, full[(f, "down", l)], g, u, f"{f}_{l}_act_bwd")
        grads_big[(f, "down", l)] = _mm([(act, df)], "tn", WIRE_DTYPE, f"{f}_{l}_dw_down")
        grads_big[(f, "gate", l)] = _mm([(dg, xn)], "tn", WIRE_DTYPE, f"{f}_{l}_dw_gate")
        grads_big[(f, "up", l)] = _mm([(du, xn)], "tn", WIRE_DTYPE, f"{f}_{l}_dw_up")
        dxn = _mm([(dg, full[(f, "gate", l)]), (du, full[(f, "up", l)])], "nn", F32, f"{f}_{l}_dx")
        dh_in, d_pre = _rms_bwd(h, gain(f + "_pre_g", l), [dxn], dh_out, 1.0, F32, f"{f}_{l}_pre_norm_bwd")
        grads_rep[(f + "_post_g", l)] = d_post
        grads_rep[(f + "_pre_g", l)] = d_pre
        return dh_in

    dh = ffn_bwd(dy, sv_f2_1, "ffn2", 1)
    dm, d_post = _rms_bwd(m_at, gain("mix_post_g", 1), [dh], None, 1.0, MXU_DTYPE, "attn_post_norm_bwd")
    grads_rep[("mix_post_g", 1)] = d_post
    do2 = _mm([(dm, full["attn_w_o"])], "nt", F32, "attn_out_proj_dx")
    grads_big["attn_w_o"] = _mm([(o2, dm)], "tn", WIRE_DTYPE, "attn_out_proj_dw")
    dq_h, dc_col, dk_h, dv_h, dc_row = _flash_bwd(q_h, k_h, v_h, c_col, c_row, o_h, lse, heads(do2), attn_scale, "attn_bwd")
    dq2 = unheads(dq_h)
    dhn = _mm([(dq2, full["attn_w_q"])], "nt", F32, "q_proj_dx")
    grads_big["attn_w_q"] = _mm([(hn_at, dq2)], "tn", WIRE_DTYPE, "q_proj_dw")
    dh, d_pre = _rms_bwd(h1a, gain("mix_pre_g", 1), [dhn], dh, 1.0, F32, "attn_pre_norm_bwd")
    grads_rep[("mix_pre_g", 1)] = d_pre
    dh = ffn_bwd(dh, sv_f1_1, "ffn1", 1)
    dkv = jnp.concatenate([unheads(dk_h), unheads(dv_h)], axis=1)
    dc_cum = _pad_cols((dc_col[:, :, 0] + dc_row[:, 0, :]).T, LANES)
    dfpre, db_f = _fgate_bwd(dc_cum, fpre, b_f, "fgate_cumsum_bwd")
    dhn_kv = _mm([(dkv, full["w_kv"])], "nn", F32, "kv_proj_dx")
    dhn_f = _mm([(dfpre, w_f_t)], "nn", F32, "fgate_proj_dx")
    grads_big["w_kv"] = _mm([(dkv, hn_kv)], "tn", WIRE_DTYPE, "kv_proj_dw")
    dw_f_t = _mm([(dfpre, hn_kv)], "tn", F32, "fgate_proj_dw")
    dh, d_kvg = _rms_bwd(h1, a["kv_norm_g"].reshape(1, d), [dhn_kv, dhn_f], dh, 1.0, F32, "kv_norm_bwd")
    dh = ffn_bwd(dh, sv_f2_0, "ffn2", 0)
    dm, d_post = _rms_bwd(m_rg, gain("mix_post_g", 0), [dh], None, 1.0, MXU_DTYPE, "rg_post_norm_bwd")
    grads_rep[("mix_post_g", 0)] = d_post
    dy_rg = _mm([(dm, w_out)], "nt", F32, "rg_out_proj_dx")
    dw_out = _mm([(y_rg, dm)], "tn", WIRE_DTYPE, "rg_out_proj_dw")
    dgate, dra, dia, drec1, dpvec = _scan_bwd(dy_rg, gx, h_rec, rec, gates, pvec, "rg_scan_bwd")
    drec2 = _mm([(dra, wa_dense), (dia, wx_dense)], "nt", F32, "rg_gate_proj_dx")
    dwa_dense = _mm([(rec, dra)], "tn", F32, "rg_gate_proj_dwa")
    dwx_dense = _mm([(rec, dia)], "tn", F32, "rg_gate_proj_dwx")
    drec0, dpconv = _conv_bwd(drec1, drec2, gx, pconv, conv_width, "rg_conv_bwd")
    dhn = _mm([(dgate, w_in_gate), (drec0, w_in_rec)], "nn", F32, "rg_in_proj_dx")
    dw_in_gate = _mm([(dgate, hn_rg)], "tn", WIRE_DTYPE, "rg_in_proj_dw_gate")
    dw_in_rec = _mm([(drec0, hn_rg)], "tn", WIRE_DTYPE, "rg_in_proj_dw_rec")
    dh, d_pre = _rms_bwd(h0a, gain("mix_pre_g", 0), [dhn], dh, 1.0, F32, "rg_pre_norm_bwd")
    grads_rep[("mix_pre_g", 0)] = d_pre
    grad_x = ffn_bwd(dh, sv_f1_0, "ffn1", 0)
    grads_big["rg_w_in"] = jnp.concatenate([dw_in_gate[:c_dim], dw_in_rec[:c_dim]], axis=0)
    grads_big["rg_w_out"] = dw_out[:c_dim]

    send = jnp.concatenate(
        [jnp.pad(grads_big[nm].reshape(N_DEV, n_rows, d), ((0, 0), (0, _round_up(n_rows, WIRE_ROW_ALIGN) - n_rows), (0, 0)))
         for nm, n_rows in zip(big_names, big_rows)], axis=1)
    assert send.shape == (N_DEV, big_total, d)
    received = _all_to_all(send, "exchange_weight_grads")
    g_big = _sum_slots(received, "sum_weight_grads")
    g_shard = {nm: g_big[o:o + n_rows] for nm, o, n_rows in zip(big_names, big_offs, big_rows)}

    def gain_grad(name):
        return jnp.concatenate([grads_rep[(name, l)] for l in range(n_layers)], axis=0)

    rep_names = ["ffn1_pre_g", "ffn1_post_g", "mix_pre_g", "mix_post_g", "ffn2_pre_g", "ffn2_post_g"]
    rep_parts = [gain_grad(nm) for nm in rep_names]
    rep_names += ["kv_norm_g", "b_fgate", "rg_w_a", "rg_w_x", "rg_conv_w", "rg_conv_b", "rg_b_a", "rg_b_x", "rg_lambda", "w_fgate"]
    rep_parts += [
        d_kvg, db_f[0, :n_heads],
        _diag_blocks(dwa_dense, n_blocks, lru_block), _diag_blocks(dwx_dense, n_blocks, lru_block),
        dpconv[:conv_width, :c_dim], dpconv[conv_width, :c_dim],
        dpvec[0, :c_dim], dpvec[1, :c_dim], dpvec[2, :c_dim],
        dw_f_t[:n_heads].T]
    rep_pack, rep_spans = _pack_rows(rep_parts, d, F32, WIRE_ROW_ALIGN)
    rep_sum = _sum_slots(_all_gather(rep_pack, "gather_small_grads"), "sum_small_grads")
    g_rep = {nm: _unpack_rows(rep_sum, sp, p.shape) for nm, sp, p in zip(rep_names, rep_spans, rep_parts)}

    def my_cols(full_grad, n):
        return lax.dynamic_slice_in_dim(full_grad, me * n, n, axis=full_grad.ndim - 1)

    grad = {}
    for nm in ("ffn1_pre_g", "ffn1_post_g", "mix_pre_g", "mix_post_g", "ffn2_pre_g", "ffn2_post_g"):
        grad[nm] = g_rep[nm]
    for f in ("ffn1", "ffn2"):
        grad[f + "_w_gate"] = jnp.stack([g_shard[(f, "gate", l)].T for l in range(n_layers)])
        grad[f + "_w_up"] = jnp.stack([g_shard[(f, "up", l)].T for l in range(n_layers)])
        grad[f + "_w_down"] = jnp.stack([g_shard[(f, "down", l)] for l in range(n_layers)])
    grad["rg_w_in"] = g_shard["rg_w_in"].T[None]
    grad["rg_conv_w"] = my_cols(g_rep["rg_conv_w"], c_shard)[None]
    for nm in ("rg_conv_b", "rg_b_a", "rg_b_x", "rg_lambda"):
        grad[nm] = my_cols(g_rep[nm], c_shard)[None]
    grad["rg_w_a"] = g_rep["rg_w_a"][None]
    grad["rg_w_x"] = g_rep["rg_w_x"][None]
    grad["rg_w_out"] = g_shard["rg_w_out"][None]
    grad["kv_norm_g"] = g_rep["kv_norm_g"].reshape(d)
    grad["w_kv"] = g_shard["w_kv"].T
    grad["w_fgate"] = lax.dynamic_slice_in_dim(g_rep["w_fgate"], me * (d // N_DEV), d // N_DEV, axis=0)
    grad["b_fgate"] = g_rep["b_fgate"]
    grad["attn_w_q"] = g_shard["attn_w_q"][None]
    grad["attn_w_o"] = g_shard["attn_w_o"][None]

    delta, new_m, new_v = {}, {}, {}
    for nm in WEIGHT_NAMES:
        w = a[nm]
        shape = w.shape
        two_d = (1, shape[0]) if w.ndim == 1 else (-1, shape[-1])
        dl, mo, vo = _adamw(w.reshape(two_d), grad[nm].reshape(two_d), a["m_" + nm].reshape(two_d),
                            a["v_" + nm].reshape(two_d), "adamw_" + nm)
        delta[nm], new_m[nm], new_v[nm] = dl.reshape(shape), mo.reshape(shape), vo.reshape(shape)
        grad[nm] = grad[nm].reshape(shape)

    loss = lax.psum(loss_part[0, 0], AXES)
    return (loss, grad_x[None], *[grad[n] for n in WEIGHT_NAMES], *[delta[n] for n in WEIGHT_NAMES],
            *[new_m[n] for n in WEIGHT_NAMES], *[new_v[n] for n in WEIGHT_NAMES])


def kernel(x, ffn1_pre_g, ffn1_w_gate, ffn1_w_up, ffn1_w_down, ffn1_post_g, mix_pre_g, mix_post_g, ffn2_pre_g, ffn2_w_gate, ffn2_w_up, ffn2_w_down, ffn2_post_g, rg_w_in, rg_conv_w, rg_conv_b, rg_w_a, rg_b_a, rg_w_x, rg_b_x, rg_lambda, rg_w_out, kv_norm_g, w_kv, w_fgate, b_fgate, attn_w_q, attn_w_o, loss_target, m_ffn1_pre_g, m_ffn1_w_gate, m_ffn1_w_up, m_ffn1_w_down, m_ffn1_post_g, m_mix_pre_g, m_mix_post_g, m_ffn2_pre_g, m_ffn2_w_gate, m_ffn2_w_up, m_ffn2_w_down, m_ffn2_post_g, m_rg_w_in, m_rg_conv_w, m_rg_conv_b, m_rg_w_a, m_rg_b_a, m_rg_w_x, m_rg_b_x, m_rg_lambda, m_rg_w_out, m_kv_norm_g, m_w_kv, m_w_fgate, m_b_fgate, m_attn_w_q, m_attn_w_o, v_ffn1_pre_g, v_ffn1_w_gate, v_ffn1_w_up, v_ffn1_w_down, v_ffn1_post_g, v_mix_pre_g, v_mix_post_g, v_ffn2_pre_g, v_ffn2_w_gate, v_ffn2_w_up, v_ffn2_w_down, v_ffn2_post_g, v_rg_w_in, v_rg_conv_w, v_rg_conv_b, v_rg_w_a, v_rg_b_a, v_rg_w_x, v_rg_b_x, v_rg_lambda, v_rg_w_out, v_kv_norm_g, v_w_kv, v_w_fgate, v_b_fgate, v_attn_w_q, v_attn_w_o):
    return _train_step(dict(locals()))
```

```python
import functools

import jax
import jax.numpy as jnp
from jax import lax
from jax.experimental import pallas as pl
from jax.experimental.pallas import tpu as pltpu

F32 = jnp.float32
MXU_DTYPE = jnp.bfloat16
WIRE_DTYPE = jnp.bfloat16
N_DEV = 8
AXES = ("x", "y", "c")
LANES = 128
WIRE_ROW_ALIGN = 16
VMEM_LIMIT_MIN = 32 * 2 ** 20
VMEM_LIMIT_MAX = 56 * 2 ** 20

RMS_EPS = 1e-6
LRU_C = 8.0
ADAM_LR, ADAM_B1, ADAM_B2, ADAM_EPS, ADAM_WD, ADAM_STEP = 0.001, 0.9, 0.999, 1e-08, 0.01, 10

WEIGHT_NAMES = (
    "ffn1_pre_g", "ffn1_w_gate", "ffn1_w_up", "ffn1_w_down", "ffn1_post_g", "mix_pre_g", "mix_post_g",
    "ffn2_pre_g", "ffn2_w_gate", "ffn2_w_up", "ffn2_w_down", "ffn2_post_g", "rg_w_in", "rg_conv_w",
    "rg_conv_b", "rg_w_a", "rg_b_a", "rg_w_x", "rg_b_x", "rg_lambda", "rg_w_out", "kv_norm_g", "w_kv",
    "w_fgate", "b_fgate", "attn_w_q", "attn_w_o")


def _round_up(n, m):
    return (n + m - 1) // m * m


def _tile(dim, target, align=LANES):
    if dim <= target:
        return dim
    best = None
    t = align
    while t <= target:
        if dim % t == 0:
            best = t
        t += align
    return dim if best is None else best


def _cparams(semantics, vmem_estimate):
    limit = min(VMEM_LIMIT_MAX, max(VMEM_LIMIT_MIN, 2 * int(vmem_estimate)))
    return pltpu.CompilerParams(dimension_semantics=semantics, vmem_limit_bytes=limit)


def _nbytes(shape, dtype):
    n = 1
    for s in shape:
        n *= s
    return n * jnp.dtype(dtype).itemsize


def _sigmoid(x):
    return jax.nn.sigmoid(x)


def _softplus(x):
    return jnp.maximum(x, 0.0) + jnp.log1p(jnp.exp(-jnp.abs(x)))


def _expm1(x):
    series = x * (1.0 + x * (0.5 + x * (1.0 / 6.0 + x * (1.0 / 24.0 + x * (1.0 / 120.0)))))
    return jnp.where(jnp.abs(x) < 0.25, series, jnp.exp(x) - 1.0)


_GELU_C = 0.7978845608028654
_GELU_A = 0.044715


def _gelu(x):
    return 0.5 * x * (1.0 + jnp.tanh(_GELU_C * (x + _GELU_A * x * x * x)))


def _gelu_grad(x):
    t = jnp.tanh(_GELU_C * (x + _GELU_A * x * x * x))
    return 0.5 * (1.0 + t) + 0.5 * x * (1.0 - t * t) * _GELU_C * (1.0 + 3.0 * _GELU_A * x * x)


_DOT_DIMS = {"nn": ((1,), (0,)), "nt": ((1,), (1,)), "tn": ((0,), (0,))}


def _dot(a, b, mode):
    return lax.dot_general(a.astype(MXU_DTYPE), b.astype(MXU_DTYPE), (_DOT_DIMS[mode], ((), ())),
                           preferred_element_type=F32)


def _mm(pairs, mode, out_dtype, name):
    a0, b0 = pairs[0]
    if mode == "tn":
        k_dim, m_dim = a0.shape
        n_dim = b0.shape[1]
    else:
        m_dim, k_dim = a0.shape
        n_dim = b0.shape[0] if mode == "nt" else b0.shape[1]
    for a, b in pairs:
        assert a.shape == a0.shape and b.shape == b0.shape
    tm = _tile(m_dim, 1408 if mode == "tn" else 512)
    tn = _tile(n_dim, 1408)
    tk = _tile(k_dim, 1408)
    nk = k_dim // tk
    n_pairs = len(pairs)

    if mode == "tn":
        a_spec = pl.BlockSpec((tk, tm), lambda i, j, k: (k, i))
    else:
        a_spec = pl.BlockSpec((tm, tk), lambda i, j, k: (i, k))
    if mode == "nt":
        b_spec = pl.BlockSpec((tn, tk), lambda i, j, k: (j, k))
    else:
        b_spec = pl.BlockSpec((tk, tn), lambda i, j, k: (k, j))

    def body(*refs):
        ins, o_ref, acc = refs[:2 * n_pairs], refs[2 * n_pairs], refs[2 * n_pairs + 1]
        k = pl.program_id(2)

        @pl.when(k == 0)
        def _():
            acc[...] = jnp.zeros_like(acc)

        s = acc[...]
        for p in range(n_pairs):
            s = s + _dot(ins[2 * p][...], ins[2 * p + 1][...], mode)
        acc[...] = s

        @pl.when(k == nk - 1)
        def _():
            o_ref[...] = acc[...].astype(out_dtype)

    est = (2 * n_pairs * (_nbytes((tm, tk), a0.dtype) + _nbytes((tk, tn), b0.dtype))
           + 2 * _nbytes((tm, tn), out_dtype) + 2 * _nbytes((tm, tn), F32))
    flat = [t for ab in pairs for t in ab]
    return pl.pallas_call(
        body, name=name, grid=(m_dim // tm, n_dim // tn, nk),
        in_specs=[a_spec, b_spec] * n_pairs,
        out_specs=pl.BlockSpec((tm, tn), lambda i, j, k: (i, j)),
        out_shape=jax.ShapeDtypeStruct((m_dim, n_dim), out_dtype),
        scratch_shapes=[pltpu.VMEM((tm, tn), F32)],
        compiler_params=_cparams(("parallel", "parallel", "arbitrary"), est),
    )(*flat)


def _rms_fwd(x, gain, name):
    s_dim, d = x.shape
    tm = _tile(s_dim, 512, 8)

    def body(x_ref, g_ref, o_ref):
        v = x_ref[...]
        r = lax.rsqrt(jnp.mean(v * v, axis=-1, keepdims=True) + RMS_EPS)
        o_ref[...] = (v * r * g_ref[...]).astype(MXU_DTYPE)

    return pl.pallas_call(
        body, name=name, grid=(s_dim // tm,),
        in_specs=[pl.BlockSpec((tm, d), lambda i: (i, 0)), pl.BlockSpec((1, d), lambda i: (0, 0))],
        out_specs=pl.BlockSpec((tm, d), lambda i: (i, 0)),
        out_shape=jax.ShapeDtypeStruct((s_dim, d), MXU_DTYPE),
        compiler_params=_cparams(("parallel",), 6 * _nbytes((tm, d), F32)),
    )(x, gain)


def _rms_bwd(x, gain, dys, res, scale, out_dtype, name):
    s_dim, d = x.shape
    tm = _tile(s_dim, 512, 8)
    n_dy = len(dys)
    has_res = res is not None

    def body(*refs):
        x_ref, g_ref = refs[0], refs[1]
        dy_refs = refs[2:2 + n_dy]
        res_ref = refs[2 + n_dy] if has_res else None
        dx_ref, dg_ref = refs[-2], refs[-1]

        @pl.when(pl.program_id(0) == 0)
        def _():
            dg_ref[...] = jnp.zeros_like(dg_ref)

        v = x_ref[...]
        r = lax.rsqrt(jnp.mean(v * v, axis=-1, keepdims=True) + RMS_EPS)
        xh = v * r
        dy = dy_refs[0][...].astype(F32)
        for extra in dy_refs[1:]:
            dy = dy + extra[...].astype(F32)
        gd = dy * g_ref[...]
        dx = scale * r * (gd - xh * jnp.mean(gd * xh, axis=-1, keepdims=True))
        if has_res:
            dx = dx + res_ref[...]
        dx_ref[...] = dx.astype(out_dtype)
        dg_ref[...] += scale * jnp.sum(dy * xh, axis=0, keepdims=True)

    row = pl.BlockSpec((tm, d), lambda i: (i, 0))
    vec = pl.BlockSpec((1, d), lambda i: (0, 0))
    ops = [x, gain] + list(dys) + ([res] if has_res else [])
    return pl.pallas_call(
        body, name=name, grid=(s_dim // tm,),
        in_specs=[row, vec] + [row] * (n_dy + int(has_res)),
        out_specs=[row, vec],
        out_shape=[jax.ShapeDtypeStruct((s_dim, d), out_dtype), jax.ShapeDtypeStruct((1, d), F32)],
        compiler_params=_cparams(("arbitrary",), (2 * len(ops) + 6) * _nbytes((tm, d), F32)),
    )(*ops)


def _mm_rms_res(a, b, h, gain, scale, name):
    s_dim, k_dim = a.shape
    d = b.shape[1]
    tm = _tile(s_dim, 512, 8)
    tk = _tile(k_dim, 1408)
    nk = k_dim // tk

    def body(a_ref, b_ref, h_ref, g_ref, f_ref, o_ref, acc):
        k = pl.program_id(1)

        @pl.when(k == 0)
        def _():
            acc[...] = jnp.zeros_like(acc)

        acc[...] += _dot(a_ref[...], b_ref[...], "nn")

        @pl.when(k == nk - 1)
        def _():
            f = acc[...]
            r = lax.rsqrt(jnp.mean(f * f, axis=-1, keepdims=True) + RMS_EPS)
            f_ref[...] = f
            o_ref[...] = h_ref[...] + scale * (f * r * g_ref[...])

    row = pl.BlockSpec((tm, d), lambda i, k: (i, 0))
    est = (2 * (_nbytes((tm, tk), a.dtype) + _nbytes((tk, d), b.dtype)) + 8 * _nbytes((tm, d), F32))
    return pl.pallas_call(
        body, name=name, grid=(s_dim // tm, nk),
        in_specs=[pl.BlockSpec((tm, tk), lambda i, k: (i, k)), pl.BlockSpec((tk, d), lambda i, k: (k, 0)),
                  row, pl.BlockSpec((1, d), lambda i, k: (0, 0))],
        out_specs=[row, row],
        out_shape=[jax.ShapeDtypeStruct((s_dim, d), F32), jax.ShapeDtypeStruct((s_dim, d), F32)],
        scratch_shapes=[pltpu.VMEM((tm, d), F32)],
        compiler_params=_cparams(("parallel", "arbitrary"), est),
    )(a, b, h, gain)


def _ffn_up(xn, wg_t, wu_t, name):
    s_dim, d = xn.shape
    f_dim = wg_t.shape[0]
    tm = _tile(s_dim, 1024, 8)
    tf = _tile(f_dim, 256)

    def body(x_ref, wg_ref, wu_ref, g_ref, u_ref, a_ref):
        x = x_ref[...]
        g = _dot(x, wg_ref[...], "nt")
        u = _dot(x, wu_ref[...], "nt")
        g_ref[...] = g.astype(MXU_DTYPE)
        u_ref[...] = u.astype(MXU_DTYPE)
        a_ref[...] = (g * _sigmoid(g) * u).astype(MXU_DTYPE)

    w_spec = pl.BlockSpec((tf, d), lambda i, j: (j, 0))
    o_spec = pl.BlockSpec((tm, tf), lambda i, j: (i, j))
    o_shape = jax.ShapeDtypeStruct((s_dim, f_dim), MXU_DTYPE)
    est = 2 * _nbytes((tm, d), xn.dtype) + 4 * _nbytes((tf, d), wg_t.dtype) + 10 * _nbytes((tm, tf), F32)
    return pl.pallas_call(
        body, name=name, grid=(s_dim // tm, f_dim // tf),
        in_specs=[pl.BlockSpec((tm, d), lambda i, j: (i, 0)), w_spec, w_spec],
        out_specs=[o_spec, o_spec, o_spec], out_shape=[o_shape, o_shape, o_shape],
        compiler_params=_cparams(("parallel", "parallel"), est),
    )(xn, wg_t, wu_t)


def _ffn_act_bwd(df, wd, g, u, name):
    s_dim, d = df.shape
    f_dim = wd.shape[0]
    tm = _tile(s_dim, 1024, 8)
    tf = _tile(f_dim, 256)

    def body(df_ref, wd_ref, g_ref, u_ref, dg_ref, du_ref):
        dh = _dot(df_ref[...], wd_ref[...], "nt")
        gv = g_ref[...].astype(F32)
        uv = u_ref[...].astype(F32)
        sg = _sigmoid(gv)
        dg_ref[...] = (dh * uv * (sg * (1.0 + gv * (1.0 - sg)))).astype(MXU_DTYPE)
        du_ref[...] = (dh * gv * sg).astype(MXU_DTYPE)

    t_spec = pl.BlockSpec((tm, tf), lambda i, j: (i, j))
    o_shape = jax.ShapeDtypeStruct((s_dim, f_dim), MXU_DTYPE)
    est = 2 * _nbytes((tm, d), df.dtype) + 2 * _nbytes((tf, d), wd.dtype) + 12 * _nbytes((tm, tf), F32)
    return pl.pallas_call(
        body, name=name, grid=(s_dim // tm, f_dim // tf),
        in_specs=[pl.BlockSpec((tm, d), lambda i, j: (i, 0)), pl.BlockSpec((tf, d), lambda i, j: (j, 0)),
                  t_spec, t_spec],
        out_specs=[t_spec, t_spec], out_shape=[o_shape, o_shape],
        compiler_params=_cparams(("parallel", "parallel"), est),
    )(df, wd, g, u)


def _loss_head(y, target, name):
    s_dim, d = y.shape
    tm = _tile(s_dim, 512, 8)
    nt = s_dim // tm

    def body(y_ref, t_ref, dy_ref, loss_ref, acc):
        i = pl.program_id(0)

        @pl.when(i == 0)
        def _():
            acc[...] = jnp.zeros_like(acc)

        e = y_ref[...] - t_ref[...]
        dy_ref[...] = e * (1.0 / d)
        acc[...] += jnp.sum(e * e, axis=0, keepdims=True)

        @pl.when(i == nt - 1)
        def _():
            loss_ref[...] = jnp.sum(acc[...], axis=1, keepdims=True) * (0.5 / d)

    row = pl.BlockSpec((tm, d), lambda i: (i, 0))
    return pl.pallas_call(
        body, name=name, grid=(nt,), in_specs=[row, row],
        out_specs=[row, pl.BlockSpec((1, 1), lambda i: (0, 0))],
        out_shape=[jax.ShapeDtypeStruct((s_dim, d), F32), jax.ShapeDtypeStruct((1, 1), F32)],
        scratch_shapes=[pltpu.VMEM((1, d), F32)],
        compiler_params=_cparams(("arbitrary",), 8 * _nbytes((tm, d), F32)),
    )(y, target)


def _shift_down(v, sh, row):
    if sh == 0:
        return v
    return jnp.where(row >= sh, pltpu.roll(v, sh, 0), 0.0)


def _shift_up(v, sh, row):
    if sh == 0:
        return v
    n = v.shape[0]
    return jnp.where(row < n - sh, pltpu.roll(v, n - sh, 0), 0.0)


def _conv_fwd(gx, pconv, width, name):
    s_dim, cp2 = gx.shape
    cp = cp2 // 2
    nc = cp // LANES

    def body(x_ref, p_ref, o_ref):
        x = x_ref[...]
        row = lax.broadcasted_iota(jnp.int32, x.shape, 0)
        y = jnp.zeros_like(x) + p_ref[pl.ds(width, 1), :]
        for k in range(width):
            y = y + p_ref[pl.ds(k, 1), :] * _shift_down(x, width - 1 - k, row)
        o_ref[...] = y

    return pl.pallas_call(
        body, name=name, grid=(nc,),
        in_specs=[pl.BlockSpec((s_dim, LANES), lambda j: (0, nc + j)), pl.BlockSpec((8, LANES), lambda j: (0, j))],
        out_specs=pl.BlockSpec((s_dim, LANES), lambda j: (0, j)),
        out_shape=jax.ShapeDtypeStruct((s_dim, cp), F32),
        compiler_params=_cparams(("parallel",), 10 * _nbytes((s_dim, LANES), F32)),
    )(gx, pconv)


def _conv_bwd(d1, d2, gx, pconv, width, name):
    s_dim, cp = d1.shape
    nc = cp // LANES

    def body(d1_ref, d2_ref, x_ref, p_ref, dx_ref, dp_ref):
        d = d1_ref[...] + d2_ref[...]
        x = x_ref[...]
        row = lax.broadcasted_iota(jnp.int32, x.shape, 0)
        dx = jnp.zeros_like(d)
        dp_ref[...] = jnp.zeros_like(dp_ref)
        for k in range(width):
            sh = width - 1 - k
            dx = dx + p_ref[pl.ds(k, 1), :] * _shift_up(d, sh, row)
            dp_ref[pl.ds(k, 1), :] = jnp.sum(d * _shift_down(x, sh, row), axis=0, keepdims=True)
        dp_ref[pl.ds(width, 1), :] = jnp.sum(d, axis=0, keepdims=True)
        dx_ref[...] = dx.astype(MXU_DTYPE)

    strip = pl.BlockSpec((s_dim, LANES), lambda j: (0, j))
    par = pl.BlockSpec((8, LANES), lambda j: (0, j))
    return pl.pallas_call(
        body, name=name, grid=(nc,),
        in_specs=[strip, strip, pl.BlockSpec((s_dim, LANES), lambda j: (0, nc + j)), par],
        out_specs=[strip, par],
        out_shape=[jax.ShapeDtypeStruct((s_dim, cp), MXU_DTYPE), jax.ShapeDtypeStruct((8, cp), F32)],
        compiler_params=_cparams(("parallel",), 14 * _nbytes((s_dim, LANES), F32)),
    )(d1, d2, gx, pconv)


def _lru_coeffs(ra, ia, p_ref):
    r = _sigmoid(ra + p_ref[pl.ds(0, 1), :])
    i = _sigmoid(ia + p_ref[pl.ds(1, 1), :])
    sp = _softplus(-p_ref[pl.ds(2, 1), :])
    log_a = -LRU_C * r * sp
    a = jnp.exp(log_a)
    mult = jnp.sqrt(-_expm1(2.0 * log_a))
    return r, i, sp, a, mult


def _scan_fwd(gx, rec, gates, pvec, name):
    s_dim, cp = rec.shape
    ts = _tile(s_dim, 256, 8)
    nt = s_dim // ts

    def body(gate_ref, rec_ref, ra_ref, ia_ref, p_ref, h_ref, y_ref, a_s, u_s, carry):
        @pl.when(pl.program_id(0) == 0)
        def _():
            carry[...] = jnp.zeros_like(carry)

        rec_v = rec_ref[...]
        _, i, _, a, mult = _lru_coeffs(ra_ref[...], ia_ref[...], p_ref)
        a_s[...] = a
        u_s[...] = mult * (i * rec_v)

        def step(t, h):
            h = a_s[pl.ds(t, 1), :] * h + u_s[pl.ds(t, 1), :]
            h_ref[pl.ds(t, 1), :] = h
            return h

        carry[pl.ds(0, 1), :] = lax.fori_loop(0, ts, step, carry[pl.ds(0, 1), :], unroll=8)
        y_ref[...] = (_gelu(gate_ref[...]) * h_ref[...]).astype(MXU_DTYPE)

    blk = pl.BlockSpec((ts, cp), lambda t: (t, 0))
    return pl.pallas_call(
        body, name=name, grid=(nt,),
        in_specs=[blk, blk, blk, pl.BlockSpec((ts, cp), lambda t: (t, 1)), pl.BlockSpec((8, cp), lambda t: (0, 0))],
        out_specs=[blk, blk],
        out_shape=[jax.ShapeDtypeStruct((s_dim, cp), F32), jax.ShapeDtypeStruct((s_dim, cp), MXU_DTYPE)],
        scratch_shapes=[pltpu.VMEM((ts, cp), F32), pltpu.VMEM((ts, cp), F32), pltpu.VMEM((8, cp), F32)],
        compiler_params=_cparams(("arbitrary",), 14 * _nbytes((ts, cp), F32)),
    )(gx, rec, gates, gates, pvec)


def _scan_bwd(dy, gx, hrec, rec, gates, pvec, name):
    s_dim, cp = rec.shape
    ts = _tile(s_dim, 128, 8)
    nt = s_dim // ts

    def body(dy_ref, gate_ref, h_ref, hp_ref, rec_ref, ra_ref, ia_ref, p_ref,
             dgate_ref, dra_ref, dia_ref, drec_ref, dp_ref, a_s, d_s, carry):
        t_id = pl.program_id(0)

        @pl.when(t_id == 0)
        def _():
            carry[...] = jnp.zeros_like(carry)
            dp_ref[...] = jnp.zeros_like(dp_ref)

        rec_v = rec_ref[...]
        r, i, sp, a, mult = _lru_coeffs(ra_ref[...], ia_ref[...], p_ref)
        gate = gate_ref[...]
        dyv = dy_ref[...]
        h = h_ref[...]
        dgate_ref[...] = (dyv * h * _gelu_grad(gate)).astype(MXU_DTYPE)
        a_s[...] = a
        d_s[...] = dyv * _gelu(gate)

        def step(k, c):
            t = ts - 1 - k
            d = d_s[pl.ds(t, 1), :] + c
            d_s[pl.ds(t, 1), :] = d
            return a_s[pl.ds(t, 1), :] * d

        carry[pl.ds(0, 1), :] = lax.fori_loop(0, ts, step, carry[pl.ds(0, 1), :], unroll=8)
        dh = d_s[...]
        row = lax.broadcasted_iota(jnp.int32, h.shape, 0)
        first = jnp.where(t_id == nt - 1, 0.0, 1.0) * hp_ref[pl.ds(7, 1), :]
        h_prev = jnp.where(row == 0, first, pltpu.roll(h, 1, 0))
        dix = dh * mult
        dla = dh * h_prev * a - dh * (i * rec_v) * (a * a) / mult
        dra = dla * (-LRU_C * sp) * r * (1.0 - r)
        dia = dix * rec_v * i * (1.0 - i)
        dra_ref[...] = dra.astype(MXU_DTYPE)
        dia_ref[...] = dia.astype(MXU_DTYPE)
        drec_ref[...] = dix * i
        dsp = jnp.sum(dla * (-LRU_C * r), axis=0, keepdims=True)
        dp_ref[pl.ds(0, 1), :] += jnp.sum(dra, axis=0, keepdims=True)
        dp_ref[pl.ds(1, 1), :] += jnp.sum(dia, axis=0, keepdims=True)
        dp_ref[pl.ds(2, 1), :] += dsp * (-_sigmoid(-p_ref[pl.ds(2, 1), :]))

    blk = pl.BlockSpec((ts, cp), lambda t: (nt - 1 - t, 0))
    prev = pl.BlockSpec((8, cp), lambda t: (jnp.maximum((nt - 1 - t) * (ts // 8) - 1, 0), 0))
    par = pl.BlockSpec((8, cp), lambda t: (0, 0))
    lo = jax.ShapeDtypeStruct((s_dim, cp), MXU_DTYPE)
    return pl.pallas_call(
        body, name=name, grid=(nt,),
        in_specs=[blk, blk, blk, prev, blk, blk, pl.BlockSpec((ts, cp), lambda t: (nt - 1 - t, 1)), par],
        out_specs=[blk, blk, blk, blk, par],
        out_shape=[lo, lo, lo, jax.ShapeDtypeStruct((s_dim, cp), F32), jax.ShapeDtypeStruct((8, cp), F32)],
        scratch_shapes=[pltpu.VMEM((ts, cp), F32), pltpu.VMEM((ts, cp), F32), pltpu.VMEM((8, cp), F32)],
        compiler_params=_cparams(("arbitrary",), 40 * _nbytes((ts, cp), F32)),
    )(dy, gx, hrec, hrec, rec, gates, gates, pvec)


def _fgate_fwd(fpre, bias, name):
    s_dim, w = fpre.shape
    ts = _tile(s_dim, 512, 8)

    def body(f_ref, b_ref, c_ref, lf_s, carry):
        @pl.when(pl.program_id(0) == 0)
        def _():
            carry[...] = jnp.zeros_like(carry)

        lf_s[...] = -_softplus(-(f_ref[...] + b_ref[pl.ds(0, 1), :]))

        def step(t, c):
            c = c + lf_s[pl.ds(t, 1), :]
            c_ref[pl.ds(t, 1), :] = c
            return c

        carry[pl.ds(0, 1), :] = lax.fori_loop(0, ts, step, carry[pl.ds(0, 1), :], unroll=8)

    blk = pl.BlockSpec((ts, w), lambda t: (t, 0))
    return pl.pallas_call(
        body, name=name, grid=(s_dim // ts,),
        in_specs=[blk, pl.BlockSpec((8, w), lambda t: (0, 0))], out_specs=blk,
        out_shape=jax.ShapeDtypeStruct((s_dim, w), F32),
        scratch_shapes=[pltpu.VMEM((ts, w), F32), pltpu.VMEM((8, w), F32)],
        compiler_params=_cparams(("arbitrary",), 12 * _nbytes((ts, w), F32)),
    )(fpre, bias)


def _fgate_bwd(dc, fpre, bias, name):
    s_dim, w = fpre.shape
    ts = _tile(s_dim, 512, 8)
    nt = s_dim // ts

    def body(dc_ref, f_ref, b_ref, df_ref, db_ref, d_s, carry):
        @pl.when(pl.program_id(0) == 0)
        def _():
            carry[...] = jnp.zeros_like(carry)
            db_ref[...] = jnp.zeros_like(db_ref)

        d_s[...] = dc_ref[...]

        def step(k, c):
            t = ts - 1 - k
            c = c + d_s[pl.ds(t, 1), :]
            d_s[pl.ds(t, 1), :] = c
            return c

        carry[pl.ds(0, 1), :] = lax.fori_loop(0, ts, step, carry[pl.ds(0, 1), :], unroll=8)
        df = d_s[...] * _sigmoid(-(f_ref[...] + b_ref[pl.ds(0, 1), :]))
        df_ref[...] = df
        db_ref[pl.ds(0, 1), :] += jnp.sum(df, axis=0, keepdims=True)

    blk = pl.BlockSpec((ts, w), lambda t: (nt - 1 - t, 0))
    par = pl.BlockSpec((8, w), lambda t: (0, 0))
    return pl.pallas_call(
        body, name=name, grid=(nt,), in_specs=[blk, blk, par], out_specs=[blk, par],
        out_shape=[jax.ShapeDtypeStruct((s_dim, w), F32), jax.ShapeDtypeStruct((8, w), F32)],
        scratch_shapes=[pltpu.VMEM((ts, w), F32), pltpu.VMEM((8, w), F32)],
        compiler_params=_cparams(("arbitrary",), 12 * _nbytes((ts, w), F32)),
    )(dc, fpre, bias)


ATTN_HEADS_PER_STEP = 2


def _augment_qk(q_h, k_h, scale):
    n_h, s_dim, dh = q_h.shape
    assert dh + 2 <= LANES
    one = jnp.ones((n_h, s_dim, 1), MXU_DTYPE)
    zero = jnp.zeros((n_h, s_dim, 1), MXU_DTYPE)
    pad = jnp.zeros((n_h, s_dim, LANES - dh - 2), MXU_DTYPE)
    q_aug = jnp.concatenate([(q_h * scale).astype(MXU_DTYPE), zero, one, pad], axis=-1)
    k_aug = jnp.concatenate([k_h.astype(MXU_DTYPE), one, zero, pad], axis=-1)
    return q_aug, k_aug


def _causal_keep(t):
    return lax.broadcasted_iota(jnp.int32, (t, t), 1) <= lax.broadcasted_iota(jnp.int32, (t, t), 0)


def _flash_fwd(q_aug, k_aug, v, c_col, c_row, name):
    n_h, s_dim, w = q_aug.shape
    dh = v.shape[2]
    t = _tile(s_dim, 512, LANES)
    nb = s_dim // t
    hb = ATTN_HEADS_PER_STEP if n_h % ATTN_HEADS_PER_STEP == 0 else 1

    def body(q_ref, k_ref, v_ref, cq_ref, ck_ref, o_ref, lse_ref, m_s, l_s, acc):
        i, j = pl.program_id(1), pl.program_id(2)

        @pl.when(j == 0)
        def _():
            m_s[...] = jnp.full_like(m_s, -jnp.inf)
            l_s[...] = jnp.zeros_like(l_s)
            acc[...] = jnp.zeros_like(acc)

        def tile(masked):
            for hh in range(hb):
                s = _dot(q_ref[hh], k_ref[hh], "nt") + (cq_ref[hh] - ck_ref[hh])
                if masked:
                    s = jnp.where(_causal_keep(t), s, -jnp.inf)
                m_prev = m_s[hh]
                m_new = jnp.maximum(m_prev, jnp.max(s, axis=-1, keepdims=True))
                alpha = jnp.exp(m_prev - m_new)
                p = jnp.exp(s - m_new)
                l_s[hh] = alpha * l_s[hh] + jnp.sum(p, axis=-1, keepdims=True)
                acc[hh] = alpha * acc[hh] + _dot(p, v_ref[hh], "nn")
                m_s[hh] = m_new

        pl.when(j < i)(functools.partial(tile, False))
        pl.when(j == i)(functools.partial(tile, True))

        @pl.when(j == nb - 1)
        def _():
            for hh in range(hb):
                o_ref[hh] = acc[hh] / l_s[hh]
                lse_ref[hh] = m_s[hh] + jnp.log(l_s[hh])

    q_spec = pl.BlockSpec((hb, t, w), lambda h, i, j: (h, i, 0))
    k_spec = pl.BlockSpec((hb, t, w), lambda h, i, j: (h, jnp.minimum(j, i), 0))
    v_spec = pl.BlockSpec((hb, t, dh), lambda h, i, j: (h, jnp.minimum(j, i), 0))
    o_spec = pl.BlockSpec((hb, t, dh), lambda h, i, j: (h, i, 0))
    col_spec = pl.BlockSpec((hb, t, 1), lambda h, i, j: (h, i, 0))
    row_spec = pl.BlockSpec((hb, 1, t), lambda h, i, j: (h, 0, jnp.minimum(j, i)))
    return pl.pallas_call(
        body, name=name, grid=(n_h // hb, nb, nb),
        in_specs=[q_spec, k_spec, v_spec, col_spec, row_spec], out_specs=[o_spec, col_spec],
        out_shape=[jax.ShapeDtypeStruct((n_h, s_dim, dh), F32), jax.ShapeDtypeStruct((n_h, s_dim, 1), F32)],
        scratch_shapes=[pltpu.VMEM((hb, t, 1), F32), pltpu.VMEM((hb, t, 1), F32), pltpu.VMEM((hb, t, dh), F32)],
        compiler_params=_cparams(("parallel", "parallel", "arbitrary"), 10 * hb * _nbytes((t, t), F32)),
    )(q_aug, k_aug, v, c_col, c_row)


def _flash_bwd(q_aug, k_aug, v, c_col, c_row, o, lse, do, name):
    n_h, s_dim, w = q_aug.shape
    dh = v.shape[2]
    t = _tile(s_dim, 512, LANES)
    nb = s_dim // t
    hb = ATTN_HEADS_PER_STEP if n_h % ATTN_HEADS_PER_STEP == 0 else 1

    def body(q_ref, k_ref, v_ref, cq_ref, ck_ref, o_ref, lse_ref, do_ref, dq_ref, dk_ref, dv_ref, dk_acc, dv_acc):
        j, i = pl.program_id(1), pl.program_id(2)

        @pl.when((j == 0) & (i == 0))
        def _():
            dq_ref[...] = jnp.zeros_like(dq_ref)

        @pl.when(i == 0)
        def _():
            dk_acc[...] = jnp.zeros_like(dk_acc)
            dv_acc[...] = jnp.zeros_like(dv_acc)

        def tile(masked):
            start = pl.multiple_of(i * t, t)
            for hh in range(hb):
                qv, kv, dov = q_ref[hh], k_ref[hh], do_ref[hh]
                s = _dot(qv, kv, "nt") + (cq_ref[hh] - ck_ref[hh])
                if masked:
                    s = jnp.where(_causal_keep(t), s, -jnp.inf)
                p = jnp.exp(s - lse_ref[hh])
                delta = jnp.sum(dov.astype(MXU_DTYPE).astype(F32) * o_ref[hh], axis=-1, keepdims=True)
                ds = p * (_dot(dov, v_ref[hh], "nt") - delta)
                dv_acc[hh] += _dot(p, dov, "tn")
                dk_acc[hh] += _dot(ds, qv, "tn")
                dq_ref[hh, pl.ds(start, t), :] += _dot(ds, kv, "nn")

        pl.when(i > j)(functools.partial(tile, False))
        pl.when(i == j)(functools.partial(tile, True))

        @pl.when(i == nb - 1)
        def _():
            dk_ref[...] = dk_acc[...]
            dv_ref[...] = dv_acc[...]

    def q_side(width):
        return pl.BlockSpec((hb, t, width), lambda h, j, i: (h, jnp.maximum(i, j), 0))

    def k_side(width):
        return pl.BlockSpec((hb, t, width), lambda h, j, i: (h, j, 0))

    wide = jax.ShapeDtypeStruct((n_h, s_dim, w), F32)
    return pl.pallas_call(
        body, name=name, grid=(n_h // hb, nb, nb),
        in_specs=[q_side(w), k_side(w), k_side(dh), q_side(1), pl.BlockSpec((hb, 1, t), lambda h, j, i: (h, 0, j)),
                  q_side(dh), q_side(1), q_side(dh)],
        out_specs=[pl.BlockSpec((hb, s_dim, w), lambda h, j, i: (h, 0, 0)), k_side(w), k_side(dh)],
        out_shape=[wide, wide, jax.ShapeDtypeStruct((n_h, s_dim, dh), F32)],
        scratch_shapes=[pltpu.VMEM((hb, t, w), F32), pltpu.VMEM((hb, t, dh), F32)],
        compiler_params=_cparams(("parallel", "arbitrary", "arbitrary"),
                                 10 * hb * _nbytes((t, t), F32) + 2 * hb * _nbytes((s_dim, w), F32)),
    )(q_aug, k_aug, v, c_col, c_row, o, lse, do)


_HBM = pl.BlockSpec(memory_space=pltpu.HBM)
_MESH_ID = pl.DeviceIdType.MESH


def _all_gather(block, name):
    r, w = block.shape

    def body(x_ref, out_ref, send_sems, recv_sems, local_sem):
        x, y, c = lax.axis_index("x"), lax.axis_index("y"), lax.axis_index("c")
        me, sibling = (x, y, c), (x, y, 1 - c)
        chips = [(1 - x, y), (x, 1 - y), (1 - x, 1 - y)]

        def slot(px, py, pc):
            return out_ref.at[4 * px + 2 * py + pc]

        def copy(k, blk, to, src=None):
            return pltpu.make_async_remote_copy(
                src_ref=slot(*blk) if src is None else src, dst_ref=slot(*blk),
                send_sem=send_sems.at[k], recv_sem=recv_sems.at[k], device_id=to, device_id_type=_MESH_ID)

        mine = pltpu.make_async_copy(x_ref, slot(*me), local_sem)
        mine.start()
        first = [copy(0, me, sibling, src=x_ref)]
        first += [copy(1 + n, me, (*chip, c), src=x_ref) for n, chip in enumerate(chips)]
        for cp in first:
            cp.start()
        passed = [copy(4 + n, (*chip, c), sibling) for n, chip in enumerate(chips)]
        for n, chip in enumerate(chips):
            copy(1 + n, (*chip, c), me).wait_recv()
            passed[n].start()
        copy(0, sibling, me).wait_recv()
        for n, chip in enumerate(chips):
            copy(4 + n, (*chip, 1 - c), me).wait_recv()
        for cp in first + passed:
            cp.wait_send()
        mine.wait()

    return pl.pallas_call(
        body, name=name, out_shape=jax.ShapeDtypeStruct((N_DEV, r, w), block.dtype),
        in_specs=[_HBM], out_specs=_HBM,
        scratch_shapes=[pltpu.SemaphoreType.DMA((7,)), pltpu.SemaphoreType.DMA((7,)), pltpu.SemaphoreType.DMA],
    )(block)


def _all_to_all(blocks, name):
    def body(x_ref, out_ref, send_sems, recv_sems, local_sem):
        x, y, c = lax.axis_index("x"), lax.axis_index("y"), lax.axis_index("c")
        mine = 4 * x + 2 * y + c
        local = pltpu.make_async_copy(x_ref.at[mine], out_ref.at[mine], local_sem)
        local.start()
        copies = []
        for k in range(1, N_DEV):
            px = 1 - x if k & 4 else x
            py = 1 - y if k & 2 else y
            pc = 1 - c if k & 1 else c
            cp = pltpu.make_async_remote_copy(
                src_ref=x_ref.at[4 * px + 2 * py + pc], dst_ref=out_ref.at[mine],
                send_sem=send_sems.at[k - 1], recv_sem=recv_sems.at[k - 1],
                device_id=(px, py, pc), device_id_type=_MESH_ID)
            cp.start()
            copies.append(cp)
        for cp in copies:
            cp.wait_recv()
        for cp in copies:
            cp.wait_send()
        local.wait()

    return pl.pallas_call(
        body, name=name, out_shape=jax.ShapeDtypeStruct(blocks.shape, blocks.dtype),
        in_specs=[_HBM], out_specs=_HBM,
        scratch_shapes=[pltpu.SemaphoreType.DMA((7,)), pltpu.SemaphoreType.DMA((7,)), pltpu.SemaphoreType.DMA],
    )(blocks)


def _sum_slots(slots, name):
    n, r, w = slots.shape
    tr = _tile(r, 128, WIRE_ROW_ALIGN)

    def body(s_ref, o_ref):
        acc = s_ref[0].astype(F32)
        for d in range(1, n):
            acc = acc + s_ref[d].astype(F32)
        o_ref[...] = acc

    return pl.pallas_call(
        body, name=name, grid=(r // tr,),
        in_specs=[pl.BlockSpec((n, tr, w), lambda i: (0, i, 0))],
        out_specs=pl.BlockSpec((tr, w), lambda i: (i, 0)),
        out_shape=jax.ShapeDtypeStruct((r, w), F32),
        compiler_params=_cparams(("parallel",), 2 * _nbytes((n, tr, w), slots.dtype) + 4 * _nbytes((tr, w), F32)),
    )(slots)


def _adamw(w, g, m, v, name):
    r, c = w.shape
    tr = _tile(r, 512, 8)

    def body(w_ref, g_ref, m_ref, v_ref, d_ref, mo_ref, vo_ref):
        gv = g_ref[...]
        m_new = ADAM_B1 * m_ref[...] + (1.0 - ADAM_B1) * gv
        v_new = ADAM_B2 * v_ref[...] + (1.0 - ADAM_B2) * (gv * gv)
        m_hat = m_new / (1.0 - ADAM_B1 ** ADAM_STEP)
        v_hat = v_new / (1.0 - ADAM_B2 ** ADAM_STEP)
        d_ref[...] = -ADAM_LR * (m_hat / (jnp.sqrt(v_hat) + ADAM_EPS) + ADAM_WD * w_ref[...])
        mo_ref[...] = m_new
        vo_ref[...] = v_new

    blk = pl.BlockSpec((tr, c), lambda i: (i, 0))
    shp = jax.ShapeDtypeStruct((r, c), F32)
    return pl.pallas_call(
        body, name=name, grid=(r // tr,), in_specs=[blk] * 4, out_specs=[blk] * 3, out_shape=[shp] * 3,
        compiler_params=_cparams(("parallel",), 16 * _nbytes((tr, _round_up(c, LANES)), F32)),
    )(w, g, m, v)


def _pack_rows(parts, width, dtype, row_align):
    rows, spans, off = [], [], 0
    for p in parts:
        flat = p.reshape(-1).astype(dtype)
        n_rows = _round_up(-(-flat.shape[0] // width), row_align)
        flat = jnp.pad(flat, (0, n_rows * width - flat.shape[0]))
        rows.append(flat.reshape(n_rows, width))
        spans.append((off, n_rows))
        off += n_rows
    return jnp.concatenate(rows, axis=0), spans


def _unpack_rows(mat, span, shape):
    off, n_rows = span
    n = 1
    for s in shape:
        n *= s
    return mat[..., off:off + n_rows, :].reshape(mat.shape[:-2] + (-1,))[..., :n].reshape(mat.shape[:-2] + tuple(shape))


def _block_diag(w, size):
    n, b, _ = w.shape
    eye = jnp.eye(n, dtype=w.dtype)
    dense = (w[:, :, None, :] * eye[:, None, :, None]).reshape(n * b, n * b)
    return jnp.pad(dense, ((0, size - n * b), (0, size - n * b)))


def _diag_blocks(dense, n, b):
    return jnp.stack([dense[k * b:(k + 1) * b, k * b:(k + 1) * b] for k in range(n)])


def _pad_rows(a, rows):
    return jnp.pad(a, ((0, rows - a.shape[0]), (0, 0)))


def _pad_cols(a, cols):
    return jnp.pad(a, ((0, 0), (0, cols - a.shape[1])))


def _train_step(a):
    x = a["x"][0]
    target = a["loss_target"][0]
    s_dim, d = x.shape
    n_layers = a["ffn1_pre_g"].shape[0]
    f_shard = a["ffn1_w_gate"].shape[2]
    c_shard = a["rg_conv_b"].shape[1]
    c_dim = c_shard * N_DEV
    cp = _round_up(c_dim, LANES)
    conv_width = a["rg_conv_w"].shape[1]
    n_blocks, lru_block = a["rg_w_a"].shape[1], a["rg_w_a"].shape[2]
    d_attn = a["attn_w_q"].shape[2]
    n_heads = a["b_fgate"].shape[0]
    d_head = d_attn // n_heads
    attn_scale = d_head ** -0.5
    assert conv_width < 8 and n_heads <= LANES and n_layers == 2
    assert d_attn == d
    me = 4 * lax.axis_index("x") + 2 * lax.axis_index("y") + lax.axis_index("c")

    big_names, big_parts = [], []
    for l in range(n_layers):
        for f in ("ffn1", "ffn2"):
            big_names += [(f, "gate", l), (f, "up", l), (f, "down", l)]
            big_parts += [a[f + "_w_gate"][l].T, a[f + "_w_up"][l].T, a[f + "_w_down"][l]]
    big_names += ["rg_w_in", "rg_w_out", "w_kv", "attn_w_q", "attn_w_o"]
    big_parts += [a["rg_w_in"][0].T, a["rg_w_out"][0], a["w_kv"].T, a["attn_w_q"][0], a["attn_w_o"][0]]
    big_rows = [p.shape[0] for p in big_parts]
    big_offs, off = [], 0
    for n_rows in big_rows:
        big_offs.append(off)
        off += _round_up(n_rows, WIRE_ROW_ALIGN)
    big_total = off
    pack = jnp.concatenate([_pad_rows(p.astype(WIRE_DTYPE), _round_up(p.shape[0], WIRE_ROW_ALIGN)) for p in big_parts], axis=0)
    gathered = _all_gather(pack, "gather_weights")
    full = {}
    for nm, o, n_rows in zip(big_names, big_offs, big_rows):
        full[nm] = gathered[:, o:o + n_rows, :].reshape(N_DEV * n_rows, d)

    small_parts = [a["rg_conv_w"][0], a["rg_conv_b"][0], a["rg_b_a"][0], a["rg_b_x"][0], a["rg_lambda"][0], a["w_fgate"]]
    small_pack, small_spans = _pack_rows(small_parts, d, F32, 8)
    small_all = _all_gather(small_pack, "gather_small")
    sm = [_unpack_rows(small_all, sp, p.shape) for sp, p in zip(small_spans, small_parts)]
    conv_w = jnp.moveaxis(sm[0], 0, 1).reshape(conv_width, c_dim)
    conv_b, b_a, b_x, lam = (v.reshape(1, c_dim) for v in sm[1:5])
    w_f = sm[5].reshape(d, n_heads)

    pconv = _pad_rows(_pad_cols(jnp.concatenate([conv_w, conv_b], axis=0), cp), 8)
    pvec = _pad_rows(_pad_cols(jnp.concatenate([b_a, b_x, lam], axis=0), cp), 8)
    wa_dense = _block_diag(a["rg_w_a"][0], cp).astype(MXU_DTYPE)
    wx_dense = _block_diag(a["rg_w_x"][0], cp).astype(MXU_DTYPE)
    wax = jnp.concatenate([wa_dense, wx_dense], axis=1)
    w_in_gate = _pad_rows(full["rg_w_in"][:c_dim], cp)
    w_in_rec = _pad_rows(full["rg_w_in"][c_dim:], cp)
    w_in_t = jnp.concatenate([w_in_gate, w_in_rec], axis=0)
    w_out = _pad_rows(full["rg_w_out"], cp)
    w_f_t = _pad_rows(w_f.T.astype(MXU_DTYPE), LANES)
    b_f = _pad_rows(_pad_cols(a["b_fgate"].reshape(1, n_heads), LANES), 8)

    def gain(name, l):
        return a[name][l].reshape(1, d)

    def ffn_fwd(h, f, l):
        xn = _rms_fwd(h, gain(f + "_pre_g", l), f"{f}_{l}_pre_norm")
        g, u, act = _ffn_up(xn, full[(f, "gate", l)], full[(f, "up", l)], f"{f}_{l}_up")
        fo, h_new = _mm_rms_res(act, full[(f, "down", l)], h, gain(f + "_post_g", l), 0.5, f"{f}_{l}_down")
        return h_new, (h, xn, g, u, act, fo)

    def heads(t2):
        return t2.reshape(s_dim, n_heads, d_head).transpose(1, 0, 2)

    def unheads(t3):
        return t3.transpose(1, 0, 2).reshape(s_dim, n_heads * d_head)

    h0 = x
    h0a, sv_f1_0 = ffn_fwd(h0, "ffn1", 0)
    hn_rg = _rms_fwd(h0a, gain("mix_pre_g", 0), "rg_pre_norm")
    gx = _mm([(hn_rg, w_in_t)], "nt", F32, "rg_in_proj")
    rec = _conv_fwd(gx, pconv, conv_width, "rg_conv")
    gates = _mm([(rec, wax)], "nn", F32, "rg_gate_proj")
    h_rec, y_rg = _scan_fwd(gx, rec, gates, pvec, "rg_scan")
    m_rg, h0b = _mm_rms_res(y_rg, w_out, h0a, gain("mix_post_g", 0), 1.0, "rg_out_proj")
    h1, sv_f2_0 = ffn_fwd(h0b, "ffn2", 0)
    hn_kv = _rms_fwd(h1, a["kv_norm_g"].reshape(1, d), "kv_norm")
    kv = _mm([(hn_kv, full["w_kv"])], "nt", MXU_DTYPE, "kv_proj")
    fpre = _mm([(hn_kv, w_f_t)], "nt", F32, "fgate_proj")
    c_cum = _fgate_fwd(fpre, b_f, "fgate_cumsum")
    k_h, v_h = heads(kv[:, :d_attn]), heads(kv[:, d_attn:])
    c_heads = c_cum[:, :n_heads].T
    c_col, c_row = c_heads[:, :, None], c_heads[:, None, :]
    h1a, sv_f1_1 = ffn_fwd(h1, "ffn1", 1)
    hn_at = _rms_fwd(h1a, gain("mix_pre_g", 1), "attn_pre_norm")
    q2 = _mm([(hn_at, full["attn_w_q"])], "nn", F32, "q_proj")
    q_aug, k_aug = _augment_qk(heads(q2), k_h, attn_scale)
    o_h, lse = _flash_fwd(q_aug, k_aug, v_h, c_col, c_row, "attn_fwd")
    o2 = unheads(o_h)
    m_at, h1b = _mm_rms_res(o2, full["attn_w_o"], h1a, gain("mix_post_g", 1), 1.0, "attn_out_proj")
    y, sv_f2_1 = ffn_fwd(h1b, "ffn2", 1)
    dy, loss_part = _loss_head(y, target, "loss_head")

    grads_big = {}
    grads_rep = {}

    def ffn_bwd(dh_out, saved, f, l):
        h, xn, g, u, act, fo = saved
        df, d_post = _rms_bwd(fo, gain(f + "_post_g", l), [dh_out], None, 0.5, MXU_DTYPE, f"{f}_{l}_post_norm_bwd")
        dg, du = _ffn_act_bwd(df, full[(f, "down", l)], g, u, f"{f}_{l}_act_bwd")
        grads_big[(f, "down", l)] = _mm([(act, df)], "tn", WIRE_DTYPE, f"{f}_{l}_dw_down")
        grads_big[(f, "gate", l)] = _mm([(dg, xn)], "tn", WIRE_DTYPE, f"{f}_{l}_dw_gate")
        grads_big[(f, "up", l)] = _mm([(du, xn)], "tn", WIRE_DTYPE, f"{f}_{l}_dw_up")
        dxn = _mm([(dg, full[(f, "gate", l)]), (du, full[(f, "up", l)])], "nn", F32, f"{f}_{l}_dx")
        dh_in, d_pre = _rms_bwd(h, gain(f + "_pre_g", l), [dxn], dh_out, 1.0, F32, f"{f}_{l}_pre_norm_bwd")
        grads_rep[(f + "_post_g", l)] = d_post
        grads_rep[(f + "_pre_g", l)] = d_pre
        return dh_in

    dh = ffn_bwd(dy, sv_f2_1, "ffn2", 1)
    dm, d_post = _rms_bwd(m_at, gain("mix_post_g", 1), [dh], None, 1.0, MXU_DTYPE, "attn_post_norm_bwd")
    grads_rep[("mix_post_g", 1)] = d_post
    do2 = _mm([(dm, full["attn_w_o"])], "nt", F32, "attn_out_proj_dx")
    grads_big["attn_w_o"] = _mm([(o2, dm)], "tn", WIRE_DTYPE, "attn_out_proj_dw")
    dq_aug, dk_aug, dv_h = _flash_bwd(q_aug, k_aug, v_h, c_col, c_row, o_h, lse, heads(do2), "attn_bwd")
    dk_h = dk_aug[:, :, :d_head]
    dc_heads = dq_aug[:, :, d_head] - dk_aug[:, :, d_head + 1]
    dq2 = unheads(dq_aug[:, :, :d_head] * attn_scale)
    dhn = _mm([(dq2, full["attn_w_q"])], "nt", F32, "q_proj_dx")
    grads_big["attn_w_q"] = _mm([(hn_at, dq2)], "tn", WIRE_DTYPE, "q_proj_dw")
    dh, d_pre = _rms_bwd(h1a, gain("mix_pre_g", 1), [dhn], dh, 1.0, F32, "attn_pre_norm_bwd")
    grads_rep[("mix_pre_g", 1)] = d_pre
    dh = ffn_bwd(dh, sv_f1_1, "ffn1", 1)
    dkv = jnp.concatenate([unheads(dk_h), unheads(dv_h)], axis=1)
    dc_cum = _pad_cols(dc_heads.T, LANES)
    dfpre, db_f = _fgate_bwd(dc_cum, fpre, b_f, "fgate_cumsum_bwd")
    dhn_kv = _mm([(dkv, full["w_kv"])], "nn", F32, "kv_proj_dx")
    dhn_f = _mm([(dfpre, w_f_t)], "nn", F32, "fgate_proj_dx")
    grads_big["w_kv"] = _mm([(dkv, hn_kv)], "tn", WIRE_DTYPE, "kv_proj_dw")
    dw_f_t = _mm([(dfpre, hn_kv)], "tn", F32, "fgate_proj_dw")
    dh, d_kvg = _rms_bwd(h1, a["kv_norm_g"].reshape(1, d), [dhn_kv, dhn_f], dh, 1.0, F32, "kv_norm_bwd")
    dh = ffn_bwd(dh, sv_f2_0, "ffn2", 0)
    dm, d_post = _rms_bwd(m_rg, gain("mix_post_g", 0), [dh], None, 1.0, MXU_DTYPE, "rg_post_norm_bwd")
    grads_rep[("mix_post_g", 0)] = d_post
    dy_rg = _mm([(dm, w_out)], "nt", F32, "rg_out_proj_dx")
    dw_out = _mm([(y_rg, dm)], "tn", WIRE_DTYPE, "rg_out_proj_dw")
    dgate, dra, dia, drec1, dpvec = _scan_bwd(dy_rg, gx, h_rec, rec, gates, pvec, "rg_scan_bwd")
    drec2 = _mm([(dra, wa_dense), (dia, wx_dense)], "nt", F32, "rg_gate_proj_dx")
    dwa_dense = _mm([(rec, dra)], "tn", F32, "rg_gate_proj_dwa")
    dwx_dense = _mm([(rec, dia)], "tn", F32, "rg_gate_proj_dwx")
    drec0, dpconv = _conv_bwd(drec1, drec2, gx, pconv, conv_width, "rg_conv_bwd")
    dhn = _mm([(dgate, w_in_gate), (drec0, w_in_rec)], "nn", F32, "rg_in_proj_dx")
    dw_in_gate = _mm([(dgate, hn_rg)], "tn", WIRE_DTYPE, "rg_in_proj_dw_gate")
    dw_in_rec = _mm([(drec0, hn_rg)], "tn", WIRE_DTYPE, "rg_in_proj_dw_rec")
    dh, d_pre = _rms_bwd(h0a, gain("mix_pre_g", 0), [dhn], dh, 1.0, F32, "rg_pre_norm_bwd")
    grads_rep[("mix_pre_g", 0)] = d_pre
    grad_x = ffn_bwd(dh, sv_f1_0, "ffn1", 0)
    grads_big["rg_w_in"] = jnp.concatenate([dw_in_gate[:c_dim], dw_in_rec[:c_dim]], axis=0)
    grads_big["rg_w_out"] = dw_out[:c_dim]

    send = jnp.concatenate(
        [jnp.pad(grads_big[nm].reshape(N_DEV, n_rows, d), ((0, 0), (0, _round_up(n_rows, WIRE_ROW_ALIGN) - n_rows), (0, 0)))
         for nm, n_rows in zip(big_names, big_rows)], axis=1)
    assert send.shape == (N_DEV, big_total, d)
    received = _all_to_all(send, "exchange_weight_grads")
    g_big = _sum_slots(received, "sum_weight_grads")
    g_shard = {nm: g_big[o:o + n_rows] for nm, o, n_rows in zip(big_names, big_offs, big_rows)}

    def gain_grad(name):
        return jnp.concatenate([grads_rep[(name, l)] for l in range(n_layers)], axis=0)

    rep_names = ["ffn1_pre_g", "ffn1_post_g", "mix_pre_g", "mix_post_g", "ffn2_pre_g", "ffn2_post_g"]
    rep_parts = [gain_grad(nm) for nm in rep_names]
    rep_names += ["kv_norm_g", "b_fgate", "rg_w_a", "rg_w_x", "rg_conv_w", "rg_conv_b", "rg_b_a", "rg_b_x", "rg_lambda", "w_fgate"]
    rep_parts += [
        d_kvg, db_f[0, :n_heads],
        _diag_blocks(dwa_dense, n_blocks, lru_block), _diag_blocks(dwx_dense, n_blocks, lru_block),
        dpconv[:conv_width, :c_dim], dpconv[conv_width, :c_dim],
        dpvec[0, :c_dim], dpvec[1, :c_dim], dpvec[2, :c_dim],
        dw_f_t[:n_heads].T]
    rep_pack, rep_spans = _pack_rows(rep_parts, d, F32, WIRE_ROW_ALIGN)
    rep_sum = _sum_slots(_all_gather(rep_pack, "gather_small_grads"), "sum_small_grads")
    g_rep = {nm: _unpack_rows(rep_sum, sp, p.shape) for nm, sp, p in zip(rep_names, rep_spans, rep_parts)}

    def my_cols(full_grad, n):
        return lax.dynamic_slice_in_dim(full_grad, me * n, n, axis=full_grad.ndim - 1)

    grad = {}
    for nm in ("ffn1_pre_g", "ffn1_post_g", "mix_pre_g", "mix_post_g", "ffn2_pre_g", "ffn2_post_g"):
        grad[nm] = g_rep[nm]
    for f in ("ffn1", "ffn2"):
        grad[f + "_w_gate"] = jnp.stack([g_shard[(f, "gate", l)].T for l in range(n_layers)])
        grad[f + "_w_up"] = jnp.stack([g_shard[(f, "up", l)].T for l in range(n_layers)])
        grad[f + "_w_down"] = jnp.stack([g_shard[(f, "down", l)] for l in range(n_layers)])
    grad["rg_w_in"] = g_shard["rg_w_in"].T[None]
    grad["rg_conv_w"] = my_cols(g_rep["rg_conv_w"], c_shard)[None]
    for nm in ("rg_conv_b", "rg_b_a", "rg_b_x", "rg_lambda"):
        grad[nm] = my_cols(g_rep[nm], c_shard)[None]
    grad["rg_w_a"] = g_rep["rg_w_a"][None]
    grad["rg_w_x"] = g_rep["rg_w_x"][None]
    grad["rg_w_out"] = g_shard["rg_w_out"][None]
    grad["kv_norm_g"] = g_rep["kv_norm_g"].reshape(d)
    grad["w_kv"] = g_shard["w_kv"].T
    grad["w_fgate"] = lax.dynamic_slice_in_dim(g_rep["w_fgate"], me * (d // N_DEV), d // N_DEV, axis=0)
    grad["b_fgate"] = g_rep["b_fgate"]
    grad["attn_w_q"] = g_shard["attn_w_q"][None]
    grad["attn_w_o"] = g_shard["attn_w_o"][None]

    delta, new_m, new_v = {}, {}, {}
    for nm in WEIGHT_NAMES:
        w = a[nm]
        shape = w.shape
        two_d = (1, shape[0]) if w.ndim == 1 else (-1, shape[-1])
        dl, mo, vo = _adamw(w.reshape(two_d), grad[nm].reshape(two_d), a["m_" + nm].reshape(two_d),
                            a["v_" + nm].reshape(two_d), "adamw_" + nm)
        delta[nm], new_m[nm], new_v[nm] = dl.reshape(shape), mo.reshape(shape), vo.reshape(shape)
        grad[nm] = grad[nm].reshape(shape)

    loss = lax.psum(loss_part[0, 0], AXES)
    return (loss, grad_x[None], *[grad[n] for n in WEIGHT_NAMES], *[delta[n] for n in WEIGHT_NAMES],
            *[new_m[n] for n in WEIGHT_NAMES], *[new_v[n] for n in WEIGHT_NAMES])


def kernel(x, ffn1_pre_g, ffn1_w_gate, ffn1_w_up, ffn1_w_down, ffn1_post_g, mix_pre_g, mix_post_g, ffn2_pre_g, ffn2_w_gate, ffn2_w_up, ffn2_w_down, ffn2_post_g, rg_w_in, rg_conv_w, rg_conv_b, rg_w_a, rg_b_a, rg_w_x, rg_b_x, rg_lambda, rg_w_out, kv_norm_g, w_kv, w_fgate, b_fgate, attn_w_q, attn_w_o, loss_target, m_ffn1_pre_g, m_ffn1_w_gate, m_ffn1_w_up, m_ffn1_w_down, m_ffn1_post_g, m_mix_pre_g, m_mix_post_g, m_ffn2_pre_g, m_ffn2_w_gate, m_ffn2_w_up, m_ffn2_w_down, m_ffn2_post_g, m_rg_w_in, m_rg_conv_w, m_rg_conv_b, m_rg_w_a, m_rg_b_a, m_rg_w_x, m_rg_b_x, m_rg_lambda, m_rg_w_out, m_kv_norm_g, m_w_kv, m_w_fgate, m_b_fgate, m_attn_w_q, m_attn_w_o, v_ffn1_pre_g, v_ffn1_w_gate, v_ffn1_w_up, v_ffn1_w_down, v_ffn1_post_g, v_mix_pre_g, v_mix_post_g, v_ffn2_pre_g, v_ffn2_w_gate, v_ffn2_w_up, v_ffn2_w_down, v_ffn2_post_g, v_rg_w_in, v_rg_conv_w, v_rg_conv_b, v_rg_w_a, v_rg_b_a, v_rg_w_x, v_rg_b_x, v_rg_lambda, v_rg_w_out, v_kv_norm_g, v_w_kv, v_w_fgate, v_b_fgate, v_attn_w_q, v_attn_w_o):
    return _train_step(dict(locals()))
```

```python
import functools

import jax
import jax.numpy as jnp
from jax import lax
from jax.experimental import pallas as pl
from jax.experimental.pallas import tpu as pltpu

F32 = jnp.float32
MXU_DTYPE = jnp.bfloat16
WIRE_DTYPE = jnp.bfloat16
N_DEV = 8
AXES = ("x", "y", "c")
LANES = 128
WIRE_ROW_ALIGN = 16
VMEM_LIMIT_MIN = 32 * 2 ** 20
VMEM_LIMIT_MAX = 56 * 2 ** 20

RMS_EPS = 1e-6
LRU_C = 8.0
ADAM_LR, ADAM_B1, ADAM_B2, ADAM_EPS, ADAM_WD, ADAM_STEP = 0.001, 0.9, 0.999, 1e-08, 0.01, 10

WEIGHT_NAMES = (
    "ffn1_pre_g", "ffn1_w_gate", "ffn1_w_up", "ffn1_w_down", "ffn1_post_g", "mix_pre_g", "mix_post_g",
    "ffn2_pre_g", "ffn2_w_gate", "ffn2_w_up", "ffn2_w_down", "ffn2_post_g", "rg_w_in", "rg_conv_w",
    "rg_conv_b", "rg_w_a", "rg_b_a", "rg_w_x", "rg_b_x", "rg_lambda", "rg_w_out", "kv_norm_g", "w_kv",
    "w_fgate", "b_fgate", "attn_w_q", "attn_w_o")


def _round_up(n, m):
    return (n + m - 1) // m * m


def _tile(dim, target, align=LANES):
    if dim <= target:
        return dim
    best = None
    t = align
    while t <= target:
        if dim % t == 0:
            best = t
        t += align
    return dim if best is None else best


def _cparams(semantics, vmem_estimate):
    limit = min(VMEM_LIMIT_MAX, max(VMEM_LIMIT_MIN, 2 * int(vmem_estimate)))
    return pltpu.CompilerParams(dimension_semantics=semantics, vmem_limit_bytes=limit)


def _nbytes(shape, dtype):
    n = 1
    for s in shape:
        n *= s
    return n * jnp.dtype(dtype).itemsize


def _sigmoid(x):
    return jax.nn.sigmoid(x)


def _softplus(x):
    return jnp.maximum(x, 0.0) + jnp.log1p(jnp.exp(-jnp.abs(x)))


def _expm1(x):
    series = x * (1.0 + x * (0.5 + x * (1.0 / 6.0 + x * (1.0 / 24.0 + x * (1.0 / 120.0)))))
    return jnp.where(jnp.abs(x) < 0.25, series, jnp.exp(x) - 1.0)


_GELU_C = 0.7978845608028654
_GELU_A = 0.044715


def _gelu(x):
    return 0.5 * x * (1.0 + jnp.tanh(_GELU_C * (x + _GELU_A * x * x * x)))


def _gelu_grad(x):
    t = jnp.tanh(_GELU_C * (x + _GELU_A * x * x * x))
    return 0.5 * (1.0 + t) + 0.5 * x * (1.0 - t * t) * _GELU_C * (1.0 + 3.0 * _GELU_A * x * x)


_DOT_DIMS = {"nn": ((1,), (0,)), "nt": ((1,), (1,)), "tn": ((0,), (0,))}


def _dot(a, b, mode):
    return lax.dot_general(a.astype(MXU_DTYPE), b.astype(MXU_DTYPE), (_DOT_DIMS[mode], ((), ())),
                           preferred_element_type=F32)


def _mm(pairs, mode, out_dtype, name):
    a0, b0 = pairs[0]
    if mode == "tn":
        k_dim, m_dim = a0.shape
        n_dim = b0.shape[1]
    else:
        m_dim, k_dim = a0.shape
        n_dim = b0.shape[0] if mode == "nt" else b0.shape[1]
    for a, b in pairs:
        assert a.shape == a0.shape and b.shape == b0.shape
    tm = _tile(m_dim, 1408 if mode == "tn" else 512)
    tn = _tile(n_dim, 1408)
    tk = _tile(k_dim, 1408)
    nk = k_dim // tk
    n_pairs = len(pairs)

    if mode == "tn":
        a_spec = pl.BlockSpec((tk, tm), lambda i, j, k: (k, i))
    else:
        a_spec = pl.BlockSpec((tm, tk), lambda i, j, k: (i, k))
    if mode == "nt":
        b_spec = pl.BlockSpec((tn, tk), lambda i, j, k: (j, k))
    else:
        b_spec = pl.BlockSpec((tk, tn), lambda i, j, k: (k, j))

    def body(*refs):
        ins, o_ref, acc = refs[:2 * n_pairs], refs[2 * n_pairs], refs[2 * n_pairs + 1]
        k = pl.program_id(2)

        @pl.when(k == 0)
        def _():
            acc[...] = jnp.zeros_like(acc)

        s = acc[...]
        for p in range(n_pairs):
            s = s + _dot(ins[2 * p][...], ins[2 * p + 1][...], mode)
        acc[...] = s

        @pl.when(k == nk - 1)
        def _():
            o_ref[...] = acc[...].astype(out_dtype)

    est = (2 * n_pairs * (_nbytes((tm, tk), a0.dtype) + _nbytes((tk, tn), b0.dtype))
           + 2 * _nbytes((tm, tn), out_dtype) + 2 * _nbytes((tm, tn), F32))
    flat = [t for ab in pairs for t in ab]
    return pl.pallas_call(
        body, name=name, grid=(m_dim // tm, n_dim // tn, nk),
        in_specs=[a_spec, b_spec] * n_pairs,
        out_specs=pl.BlockSpec((tm, tn), lambda i, j, k: (i, j)),
        out_shape=jax.ShapeDtypeStruct((m_dim, n_dim), out_dtype),
        scratch_shapes=[pltpu.VMEM((tm, tn), F32)],
        compiler_params=_cparams(("parallel", "parallel", "arbitrary"), est),
    )(*flat)


_ANY = pl.BlockSpec(memory_space=pl.ANY)


def _rms_fwd(x, gain, name, after=None):
    s_dim, d = x.shape
    tm = _tile(s_dim, 512, 8)

    def body(*refs):
        x_ref, g_ref, o_ref = refs[0], refs[1], refs[-1]
        v = x_ref[...]
        r = lax.rsqrt(jnp.mean(v * v, axis=-1, keepdims=True) + RMS_EPS)
        o_ref[...] = (v * r * g_ref[...]).astype(MXU_DTYPE)

    order = [] if after is None else [after]
    return pl.pallas_call(
        body, name=name, grid=(s_dim // tm,),
        in_specs=[pl.BlockSpec((tm, d), lambda i: (i, 0)), pl.BlockSpec((1, d), lambda i: (0, 0))] + [_ANY] * len(order),
        out_specs=pl.BlockSpec((tm, d), lambda i: (i, 0)),
        out_shape=jax.ShapeDtypeStruct((s_dim, d), MXU_DTYPE),
        compiler_params=_cparams(("parallel",), 6 * _nbytes((tm, d), F32)),
    )(x, gain, *order)


def _rms_bwd(x, gain, dys, res, scale, out_dtype, name, after=None):
    s_dim, d = x.shape
    tm = _tile(s_dim, 512, 8)
    n_dy = len(dys)
    has_res = res is not None
    order = [] if after is None else [after]

    def body(*refs):
        x_ref, g_ref = refs[0], refs[1]
        dy_refs = refs[2:2 + n_dy]
        res_ref = refs[2 + n_dy] if has_res else None
        dx_ref, dg_ref = refs[-2], refs[-1]

        @pl.when(pl.program_id(0) == 0)
        def _():
            dg_ref[...] = jnp.zeros_like(dg_ref)

        v = x_ref[...]
        r = lax.rsqrt(jnp.mean(v * v, axis=-1, keepdims=True) + RMS_EPS)
        xh = v * r
        dy = dy_refs[0][...].astype(F32)
        for extra in dy_refs[1:]:
            dy = dy + extra[...].astype(F32)
        gd = dy * g_ref[...]
        dx = scale * r * (gd - xh * jnp.mean(gd * xh, axis=-1, keepdims=True))
        if has_res:
            dx = dx + res_ref[...]
        dx_ref[...] = dx.astype(out_dtype)
        dg_ref[...] += scale * jnp.sum(dy * xh, axis=0, keepdims=True)

    row = pl.BlockSpec((tm, d), lambda i: (i, 0))
    vec = pl.BlockSpec((1, d), lambda i: (0, 0))
    ops = [x, gain] + list(dys) + ([res] if has_res else [])
    return pl.pallas_call(
        body, name=name, grid=(s_dim // tm,),
        in_specs=[row, vec] + [row] * (n_dy + int(has_res)) + [_ANY] * len(order),
        out_specs=[row, vec],
        out_shape=[jax.ShapeDtypeStruct((s_dim, d), out_dtype), jax.ShapeDtypeStruct((1, d), F32)],
        compiler_params=_cparams(("arbitrary",), (2 * len(ops) + 6) * _nbytes((tm, d), F32)),
    )(*ops, *order)


def _mm_rms_res(a, b, h, gain, scale, name):
    s_dim, k_dim = a.shape
    d = b.shape[1]
    tm = _tile(s_dim, 512, 8)
    tk = _tile(k_dim, 1408)
    nk = k_dim // tk

    def body(a_ref, b_ref, h_ref, g_ref, f_ref, o_ref, acc):
        k = pl.program_id(1)

        @pl.when(k == 0)
        def _():
            acc[...] = jnp.zeros_like(acc)

        acc[...] += _dot(a_ref[...], b_ref[...], "nn")

        @pl.when(k == nk - 1)
        def _():
            f = acc[...]
            r = lax.rsqrt(jnp.mean(f * f, axis=-1, keepdims=True) + RMS_EPS)
            f_ref[...] = f
            o_ref[...] = h_ref[...] + scale * (f * r * g_ref[...])

    row = pl.BlockSpec((tm, d), lambda i, k: (i, 0))
    est = (2 * (_nbytes((tm, tk), a.dtype) + _nbytes((tk, d), b.dtype)) + 8 * _nbytes((tm, d), F32))
    return pl.pallas_call(
        body, name=name, grid=(s_dim // tm, nk),
        in_specs=[pl.BlockSpec((tm, tk), lambda i, k: (i, k)), pl.BlockSpec((tk, d), lambda i, k: (k, 0)),
                  row, pl.BlockSpec((1, d), lambda i, k: (0, 0))],
        out_specs=[row, row],
        out_shape=[jax.ShapeDtypeStruct((s_dim, d), F32), jax.ShapeDtypeStruct((s_dim, d), F32)],
        scratch_shapes=[pltpu.VMEM((tm, d), F32)],
        compiler_params=_cparams(("parallel", "arbitrary"), est),
    )(a, b, h, gain)


def _ffn_up(xn, wg_t, wu_t, name):
    s_dim, d = xn.shape
    f_dim = wg_t.shape[0]
    tm = _tile(s_dim, 1024, 8)
    tf = _tile(f_dim, 256)

    def body(x_ref, wg_ref, wu_ref, g_ref, u_ref, a_ref):
        x = x_ref[...]
        g = _dot(x, wg_ref[...], "nt")
        u = _dot(x, wu_ref[...], "nt")
        g_ref[...] = g.astype(MXU_DTYPE)
        u_ref[...] = u.astype(MXU_DTYPE)
        a_ref[...] = (g * _sigmoid(g) * u).astype(MXU_DTYPE)

    w_spec = pl.BlockSpec((tf, d), lambda i, j: (j, 0))
    o_spec = pl.BlockSpec((tm, tf), lambda i, j: (i, j))
    o_shape = jax.ShapeDtypeStruct((s_dim, f_dim), MXU_DTYPE)
    est = 2 * _nbytes((tm, d), xn.dtype) + 4 * _nbytes((tf, d), wg_t.dtype) + 10 * _nbytes((tm, tf), F32)
    return pl.pallas_call(
        body, name=name, grid=(s_dim // tm, f_dim // tf),
        in_specs=[pl.BlockSpec((tm, d), lambda i, j: (i, 0)), w_spec, w_spec],
        out_specs=[o_spec, o_spec, o_spec], out_shape=[o_shape, o_shape, o_shape],
        compiler_params=_cparams(("parallel", "parallel"), est),
    )(xn, wg_t, wu_t)


def _ffn_act_bwd(df, wd, g, u, name):
    s_dim, d = df.shape
    f_dim = wd.shape[0]
    tm = _tile(s_dim, 1024, 8)
    tf = _tile(f_dim, 256)

    def body(df_ref, wd_ref, g_ref, u_ref, dg_ref, du_ref):
        dh = _dot(df_ref[...], wd_ref[...], "nt")
        gv = g_ref[...].astype(F32)
        uv = u_ref[...].astype(F32)
        sg = _sigmoid(gv)
        dg_ref[...] = (dh * uv * (sg * (1.0 + gv * (1.0 - sg)))).astype(MXU_DTYPE)
        du_ref[...] = (dh * gv * sg).astype(MXU_DTYPE)

    t_spec = pl.BlockSpec((tm, tf), lambda i, j: (i, j))
    o_shape = jax.ShapeDtypeStruct((s_dim, f_dim), MXU_DTYPE)
    est = 2 * _nbytes((tm, d), df.dtype) + 2 * _nbytes((tf, d), wd.dtype) + 12 * _nbytes((tm, tf), F32)
    return pl.pallas_call(
        body, name=name, grid=(s_dim // tm, f_dim // tf),
        in_specs=[pl.BlockSpec((tm, d), lambda i, j: (i, 0)), pl.BlockSpec((tf, d), lambda i, j: (j, 0)),
                  t_spec, t_spec],
        out_specs=[t_spec, t_spec], out_shape=[o_shape, o_shape],
        compiler_params=_cparams(("parallel", "parallel"), est),
    )(df, wd, g, u)


def _loss_head(y, target, name):
    s_dim, d = y.shape
    tm = _tile(s_dim, 512, 8)
    nt = s_dim // tm

    def body(y_ref, t_ref, dy_ref, loss_ref, acc):
        i = pl.program_id(0)

        @pl.when(i == 0)
        def _():
            acc[...] = jnp.zeros_like(acc)

        e = y_ref[...] - t_ref[...]
        dy_ref[...] = e * (1.0 / d)
        acc[...] += jnp.sum(e * e, axis=0, keepdims=True)

        @pl.when(i == nt - 1)
        def _():
            loss_ref[...] = jnp.sum(acc[...], axis=1, keepdims=True) * (0.5 / d)

    row = pl.BlockSpec((tm, d), lambda i: (i, 0))
    return pl.pallas_call(
        body, name=name, grid=(nt,), in_specs=[row, row],
        out_specs=[row, pl.BlockSpec((1, 1), lambda i: (0, 0))],
        out_shape=[jax.ShapeDtypeStruct((s_dim, d), F32), jax.ShapeDtypeStruct((1, 1), F32)],
        scratch_shapes=[pltpu.VMEM((1, d), F32)],
        compiler_params=_cparams(("arbitrary",), 8 * _nbytes((tm, d), F32)),
    )(y, target)


def _shift_down(v, sh, row):
    if sh == 0:
        return v
    return jnp.where(row >= sh, pltpu.roll(v, sh, 0), 0.0)


def _shift_up(v, sh, row):
    if sh == 0:
        return v
    n = v.shape[0]
    return jnp.where(row < n - sh, pltpu.roll(v, n - sh, 0), 0.0)


def _conv_fwd(gx, pconv, width, name):
    s_dim, cp2 = gx.shape
    cp = cp2 // 2
    nc = cp // LANES

    def body(x_ref, p_ref, o_ref):
        x = x_ref[...]
        row = lax.broadcasted_iota(jnp.int32, x.shape, 0)
        y = jnp.zeros_like(x) + p_ref[pl.ds(width, 1), :]
        for k in range(width):
            y = y + p_ref[pl.ds(k, 1), :] * _shift_down(x, width - 1 - k, row)
        o_ref[...] = y

    return pl.pallas_call(
        body, name=name, grid=(nc,),
        in_specs=[pl.BlockSpec((s_dim, LANES), lambda j: (0, nc + j)), pl.BlockSpec((8, LANES), lambda j: (0, j))],
        out_specs=pl.BlockSpec((s_dim, LANES), lambda j: (0, j)),
        out_shape=jax.ShapeDtypeStruct((s_dim, cp), F32),
        compiler_params=_cparams(("parallel",), 10 * _nbytes((s_dim, LANES), F32)),
    )(gx, pconv)


def _conv_bwd(d1, d2, gx, pconv, width, name):
    s_dim, cp = d1.shape
    nc = cp // LANES

    def body(d1_ref, d2_ref, x_ref, p_ref, dx_ref, dp_ref):
        d = d1_ref[...] + d2_ref[...]
        x = x_ref[...]
        row = lax.broadcasted_iota(jnp.int32, x.shape, 0)
        dx = jnp.zeros_like(d)
        dp_ref[...] = jnp.zeros_like(dp_ref)
        for k in range(width):
            sh = width - 1 - k
            dx = dx + p_ref[pl.ds(k, 1), :] * _shift_up(d, sh, row)
            dp_ref[pl.ds(k, 1), :] = jnp.sum(d * _shift_down(x, sh, row), axis=0, keepdims=True)
        dp_ref[pl.ds(width, 1), :] = jnp.sum(d, axis=0, keepdims=True)
        dx_ref[...] = dx.astype(MXU_DTYPE)

    strip = pl.BlockSpec((s_dim, LANES), lambda j: (0, j))
    par = pl.BlockSpec((8, LANES), lambda j: (0, j))
    return pl.pallas_call(
        body, name=name, grid=(nc,),
        in_specs=[strip, strip, pl.BlockSpec((s_dim, LANES), lambda j: (0, nc + j)), par],
        out_specs=[strip, par],
        out_shape=[jax.ShapeDtypeStruct((s_dim, cp), MXU_DTYPE), jax.ShapeDtypeStruct((8, cp), F32)],
        compiler_params=_cparams(("parallel",), 14 * _nbytes((s_dim, LANES), F32)),
    )(d1, d2, gx, pconv)


def _lru_coeffs(ra, ia, p_ref):
    r = _sigmoid(ra + p_ref[pl.ds(0, 1), :])
    i = _sigmoid(ia + p_ref[pl.ds(1, 1), :])
    sp = _softplus(-p_ref[pl.ds(2, 1), :])
    log_a = -LRU_C * r * sp
    a = jnp.exp(log_a)
    mult = jnp.sqrt(-_expm1(2.0 * log_a))
    return r, i, sp, a, mult


def _scan_fwd(gx, rec, gates, pvec, name):
    s_dim, cp = rec.shape
    ts = _tile(s_dim, 256, 8)
    nt = s_dim // ts

    def body(gate_ref, rec_ref, ra_ref, ia_ref, p_ref, h_ref, y_ref, a_s, u_s, carry):
        @pl.when(pl.program_id(0) == 0)
        def _():
            carry[...] = jnp.zeros_like(carry)

        rec_v = rec_ref[...]
        _, i, _, a, mult = _lru_coeffs(ra_ref[...], ia_ref[...], p_ref)
        a_s[...] = a
        u_s[...] = mult * (i * rec_v)

        def step(t, h):
            h = a_s[pl.ds(t, 1), :] * h + u_s[pl.ds(t, 1), :]
            h_ref[pl.ds(t, 1), :] = h
            return h

        carry[pl.ds(0, 1), :] = lax.fori_loop(0, ts, step, carry[pl.ds(0, 1), :], unroll=8)
        y_ref[...] = (_gelu(gate_ref[...]) * h_ref[...]).astype(MXU_DTYPE)

    blk = pl.BlockSpec((ts, cp), lambda t: (t, 0))
    return pl.pallas_call(
        body, name=name, grid=(nt,),
        in_specs=[blk, blk, blk, pl.BlockSpec((ts, cp), lambda t: (t, 1)), pl.BlockSpec((8, cp), lambda t: (0, 0))],
        out_specs=[blk, blk],
        out_shape=[jax.ShapeDtypeStruct((s_dim, cp), F32), jax.ShapeDtypeStruct((s_dim, cp), MXU_DTYPE)],
        scratch_shapes=[pltpu.VMEM((ts, cp), F32), pltpu.VMEM((ts, cp), F32), pltpu.VMEM((8, cp), F32)],
        compiler_params=_cparams(("arbitrary",), 14 * _nbytes((ts, cp), F32)),
    )(gx, rec, gates, gates, pvec)


def _scan_bwd(dy, gx, hrec, rec, gates, pvec, name):
    s_dim, cp = rec.shape
    ts = _tile(s_dim, 128, 8)
    nt = s_dim // ts

    def body(dy_ref, gate_ref, h_ref, hp_ref, rec_ref, ra_ref, ia_ref, p_ref,
             dgate_ref, dra_ref, dia_ref, drec_ref, dp_ref, a_s, d_s, carry):
        t_id = pl.program_id(0)

        @pl.when(t_id == 0)
        def _():
            carry[...] = jnp.zeros_like(carry)
            dp_ref[...] = jnp.zeros_like(dp_ref)

        rec_v = rec_ref[...]
        r, i, sp, a, mult = _lru_coeffs(ra_ref[...], ia_ref[...], p_ref)
        gate = gate_ref[...]
        dyv = dy_ref[...]
        h = h_ref[...]
        dgate_ref[...] = (dyv * h * _gelu_grad(gate)).astype(MXU_DTYPE)
        a_s[...] = a
        d_s[...] = dyv * _gelu(gate)

        def step(k, c):
            t = ts - 1 - k
            d = d_s[pl.ds(t, 1), :] + c
            d_s[pl.ds(t, 1), :] = d
            return a_s[pl.ds(t, 1), :] * d

        carry[pl.ds(0, 1), :] = lax.fori_loop(0, ts, step, carry[pl.ds(0, 1), :], unroll=8)
        dh = d_s[...]
        row = lax.broadcasted_iota(jnp.int32, h.shape, 0)
        first = jnp.where(t_id == nt - 1, 0.0, 1.0) * hp_ref[pl.ds(7, 1), :]
        h_prev = jnp.where(row == 0, first, pltpu.roll(h, 1, 0))
        dix = dh * mult
        dla = dh * h_prev * a - dh * (i * rec_v) * (a * a) / mult
        dra = dla * (-LRU_C * sp) * r * (1.0 - r)
        dia = dix * rec_v * i * (1.0 - i)
        dra_ref[...] = dra.astype(MXU_DTYPE)
        dia_ref[...] = dia.astype(MXU_DTYPE)
        drec_ref[...] = dix * i
        dsp = jnp.sum(dla * (-LRU_C * r), axis=0, keepdims=True)
        dp_ref[pl.ds(0, 1), :] += jnp.sum(dra, axis=0, keepdims=True)
        dp_ref[pl.ds(1, 1), :] += jnp.sum(dia, axis=0, keepdims=True)
        dp_ref[pl.ds(2, 1), :] += dsp * (-_sigmoid(-p_ref[pl.ds(2, 1), :]))

    blk = pl.BlockSpec((ts, cp), lambda t: (nt - 1 - t, 0))
    prev = pl.BlockSpec((8, cp), lambda t: (jnp.maximum((nt - 1 - t) * (ts // 8) - 1, 0), 0))
    par = pl.BlockSpec((8, cp), lambda t: (0, 0))
    lo = jax.ShapeDtypeStruct((s_dim, cp), MXU_DTYPE)
    return pl.pallas_call(
        body, name=name, grid=(nt,),
        in_specs=[blk, blk, blk, prev, blk, blk, pl.BlockSpec((ts, cp), lambda t: (nt - 1 - t, 1)), par],
        out_specs=[blk, blk, blk, blk, par],
        out_shape=[lo, lo, lo, jax.ShapeDtypeStruct((s_dim, cp), F32), jax.ShapeDtypeStruct((8, cp), F32)],
        scratch_shapes=[pltpu.VMEM((ts, cp), F32), pltpu.VMEM((ts, cp), F32), pltpu.VMEM((8, cp), F32)],
        compiler_params=_cparams(("arbitrary",), 40 * _nbytes((ts, cp), F32)),
    )(dy, gx, hrec, hrec, rec, gates, gates, pvec)


def _fgate_fwd(fpre, bias, name):
    s_dim, w = fpre.shape
    ts = _tile(s_dim, 512, 8)

    def body(f_ref, b_ref, c_ref, lf_s, carry):
        @pl.when(pl.program_id(0) == 0)
        def _():
            carry[...] = jnp.zeros_like(carry)

        lf_s[...] = -_softplus(-(f_ref[...] + b_ref[pl.ds(0, 1), :]))

        def step(t, c):
            c = c + lf_s[pl.ds(t, 1), :]
            c_ref[pl.ds(t, 1), :] = c
            return c

        carry[pl.ds(0, 1), :] = lax.fori_loop(0, ts, step, carry[pl.ds(0, 1), :], unroll=8)

    blk = pl.BlockSpec((ts, w), lambda t: (t, 0))
    return pl.pallas_call(
        body, name=name, grid=(s_dim // ts,),
        in_specs=[blk, pl.BlockSpec((8, w), lambda t: (0, 0))], out_specs=blk,
        out_shape=jax.ShapeDtypeStruct((s_dim, w), F32),
        scratch_shapes=[pltpu.VMEM((ts, w), F32), pltpu.VMEM((8, w), F32)],
        compiler_params=_cparams(("arbitrary",), 12 * _nbytes((ts, w), F32)),
    )(fpre, bias)


def _fgate_bwd(dc, fpre, bias, name):
    s_dim, w = fpre.shape
    ts = _tile(s_dim, 512, 8)
    nt = s_dim // ts

    def body(dc_ref, f_ref, b_ref, df_ref, db_ref, d_s, carry):
        @pl.when(pl.program_id(0) == 0)
        def _():
            carry[...] = jnp.zeros_like(carry)
            db_ref[...] = jnp.zeros_like(db_ref)

        d_s[...] = dc_ref[...]

        def step(k, c):
            t = ts - 1 - k
            c = c + d_s[pl.ds(t, 1), :]
            d_s[pl.ds(t, 1), :] = c
            return c

        carry[pl.ds(0, 1), :] = lax.fori_loop(0, ts, step, carry[pl.ds(0, 1), :], unroll=8)
        df = d_s[...] * _sigmoid(-(f_ref[...] + b_ref[pl.ds(0, 1), :]))
        df_ref[...] = df
        db_ref[pl.ds(0, 1), :] += jnp.sum(df, axis=0, keepdims=True)

    blk = pl.BlockSpec((ts, w), lambda t: (nt - 1 - t, 0))
    par = pl.BlockSpec((8, w), lambda t: (0, 0))
    return pl.pallas_call(
        body, name=name, grid=(nt,), in_specs=[blk, blk, par], out_specs=[blk, par],
        out_shape=[jax.ShapeDtypeStruct((s_dim, w), F32), jax.ShapeDtypeStruct((8, w), F32)],
        scratch_shapes=[pltpu.VMEM((ts, w), F32), pltpu.VMEM((8, w), F32)],
        compiler_params=_cparams(("arbitrary",), 12 * _nbytes((ts, w), F32)),
    )(dc, fpre, bias)


ATTN_HEADS_PER_STEP = 2


def _augment_qk(q_h, k_h, scale):
    n_h, s_dim, dh = q_h.shape
    assert dh + 2 <= LANES
    one = jnp.ones((n_h, s_dim, 1), MXU_DTYPE)
    zero = jnp.zeros((n_h, s_dim, 1), MXU_DTYPE)
    pad = jnp.zeros((n_h, s_dim, LANES - dh - 2), MXU_DTYPE)
    q_aug = jnp.concatenate([(q_h * scale).astype(MXU_DTYPE), zero, one, pad], axis=-1)
    k_aug = jnp.concatenate([k_h.astype(MXU_DTYPE), one, zero, pad], axis=-1)
    return q_aug, k_aug


def _causal_keep(t):
    return lax.broadcasted_iota(jnp.int32, (t, t), 1) <= lax.broadcasted_iota(jnp.int32, (t, t), 0)


def _flash_fwd(q_aug, k_aug, v, c_col, c_row, name):
    n_h, s_dim, w = q_aug.shape
    dh = v.shape[2]
    t = _tile(s_dim, 512, LANES)
    nb = s_dim // t
    hb = ATTN_HEADS_PER_STEP if n_h % ATTN_HEADS_PER_STEP == 0 else 1

    def body(q_ref, k_ref, v_ref, cq_ref, ck_ref, o_ref, lse_ref, m_s, l_s, acc):
        i, j = pl.program_id(1), pl.program_id(2)

        @pl.when(j == 0)
        def _():
            m_s[...] = jnp.full_like(m_s, -jnp.inf)
            l_s[...] = jnp.zeros_like(l_s)
            acc[...] = jnp.zeros_like(acc)

        def tile(masked):
            for hh in range(hb):
                s = _dot(q_ref[hh], k_ref[hh], "nt") + (cq_ref[hh] - ck_ref[hh])
                if masked:
                    s = jnp.where(_causal_keep(t), s, -jnp.inf)
                m_prev = m_s[hh]
                m_new = jnp.maximum(m_prev, jnp.max(s, axis=-1, keepdims=True))
                alpha = jnp.exp(m_prev - m_new)
                p = jnp.exp(s - m_new)
                l_s[hh] = alpha * l_s[hh] + jnp.sum(p, axis=-1, keepdims=True)
                acc[hh] = alpha * acc[hh] + _dot(p, v_ref[hh], "nn")
                m_s[hh] = m_new

        pl.when(j < i)(functools.partial(tile, False))
        pl.when(j == i)(functools.partial(tile, True))

        @pl.when(j == nb - 1)
        def _():
            for hh in range(hb):
                o_ref[hh] = acc[hh] / l_s[hh]
                lse_ref[hh] = m_s[hh] + jnp.log(l_s[hh])

    q_spec = pl.BlockSpec((hb, t, w), lambda h, i, j: (h, i, 0))
    k_spec = pl.BlockSpec((hb, t, w), lambda h, i, j: (h, jnp.minimum(j, i), 0))
    v_spec = pl.BlockSpec((hb, t, dh), lambda h, i, j: (h, jnp.minimum(j, i), 0))
    o_spec = pl.BlockSpec((hb, t, dh), lambda h, i, j: (h, i, 0))
    col_spec = pl.BlockSpec((hb, t, 1), lambda h, i, j: (h, i, 0))
    row_spec = pl.BlockSpec((hb, 1, t), lambda h, i, j: (h, 0, jnp.minimum(j, i)))
    return pl.pallas_call(
        body, name=name, grid=(n_h // hb, nb, nb),
        in_specs=[q_spec, k_spec, v_spec, col_spec, row_spec], out_specs=[o_spec, col_spec],
        out_shape=[jax.ShapeDtypeStruct((n_h, s_dim, dh), F32), jax.ShapeDtypeStruct((n_h, s_dim, 1), F32)],
        scratch_shapes=[pltpu.VMEM((hb, t, 1), F32), pltpu.VMEM((hb, t, 1), F32), pltpu.VMEM((hb, t, dh), F32)],
        compiler_params=_cparams(("parallel", "parallel", "arbitrary"), 10 * hb * _nbytes((t, t), F32)),
    )(q_aug, k_aug, v, c_col, c_row)


def _flash_bwd(q_aug, k_aug, v, c_col, c_row, o, lse, do, name):
    n_h, s_dim, w = q_aug.shape
    dh = v.shape[2]
    t = _tile(s_dim, 512, LANES)
    nb = s_dim // t
    hb = ATTN_HEADS_PER_STEP if n_h % ATTN_HEADS_PER_STEP == 0 else 1

    def body(q_ref, k_ref, v_ref, cq_ref, ck_ref, o_ref, lse_ref, do_ref, dq_ref, dk_ref, dv_ref, dk_acc, dv_acc):
        j, i = pl.program_id(1), pl.program_id(2)

        @pl.when((j == 0) & (i == 0))
        def _():
            dq_ref[...] = jnp.zeros_like(dq_ref)

        @pl.when(i == 0)
        def _():
            dk_acc[...] = jnp.zeros_like(dk_acc)
            dv_acc[...] = jnp.zeros_like(dv_acc)

        def tile(masked):
            start = pl.multiple_of(i * t, t)
            for hh in range(hb):
                qv, kv, dov = q_ref[hh], k_ref[hh], do_ref[hh]
                s = _dot(qv, kv, "nt") + (cq_ref[hh] - ck_ref[hh])
                if masked:
                    s = jnp.where(_causal_keep(t), s, -jnp.inf)
                p = jnp.exp(s - lse_ref[hh])
                delta = jnp.sum(dov.astype(MXU_DTYPE).astype(F32) * o_ref[hh], axis=-1, keepdims=True)
                ds = p * (_dot(dov, v_ref[hh], "nt") - delta)
                dv_acc[hh] += _dot(p, dov, "tn")
                dk_acc[hh] += _dot(ds, qv, "tn")
                dq_ref[hh, pl.ds(start, t), :] += _dot(ds, kv, "nn")

        pl.when(i > j)(functools.partial(tile, False))
        pl.when(i == j)(functools.partial(tile, True))

        @pl.when(i == nb - 1)
        def _():
            dk_ref[...] = dk_acc[...]
            dv_ref[...] = dv_acc[...]

    def q_side(width):
        return pl.BlockSpec((hb, t, width), lambda h, j, i: (h, jnp.maximum(i, j), 0))

    def k_side(width):
        return pl.BlockSpec((hb, t, width), lambda h, j, i: (h, j, 0))

    wide = jax.ShapeDtypeStruct((n_h, s_dim, w), F32)
    return pl.pallas_call(
        body, name=name, grid=(n_h // hb, nb, nb),
        in_specs=[q_side(w), k_side(w), k_side(dh), q_side(1), pl.BlockSpec((hb, 1, t), lambda h, j, i: (h, 0, j)),
                  q_side(dh), q_side(1), q_side(dh)],
        out_specs=[pl.BlockSpec((hb, s_dim, w), lambda h, j, i: (h, 0, 0)), k_side(w), k_side(dh)],
        out_shape=[wide, wide, jax.ShapeDtypeStruct((n_h, s_dim, dh), F32)],
        scratch_shapes=[pltpu.VMEM((hb, t, w), F32), pltpu.VMEM((hb, t, dh), F32)],
        compiler_params=_cparams(("parallel", "arbitrary", "arbitrary"),
                                 10 * hb * _nbytes((t, t), F32) + 2 * hb * _nbytes((s_dim, w), F32)),
    )(q_aug, k_aug, v, c_col, c_row, o, lse, do)


_HBM = pl.BlockSpec(memory_space=pltpu.HBM)
_MESH_ID = pl.DeviceIdType.MESH


def _all_gather(block, name):
    r, w = block.shape

    def body(x_ref, out_ref, send_sems, recv_sems, local_sem):
        x, y, c = lax.axis_index("x"), lax.axis_index("y"), lax.axis_index("c")
        me, sibling = (x, y, c), (x, y, 1 - c)
        chips = [(1 - x, y), (x, 1 - y), (1 - x, 1 - y)]

        def slot(px, py, pc):
            return out_ref.at[4 * px + 2 * py + pc]

        def copy(k, blk, to, src=None):
            return pltpu.make_async_remote_copy(
                src_ref=slot(*blk) if src is None else src, dst_ref=slot(*blk),
                send_sem=send_sems.at[k], recv_sem=recv_sems.at[k], device_id=to, device_id_type=_MESH_ID)

        mine = pltpu.make_async_copy(x_ref, slot(*me), local_sem)
        mine.start()
        first = [copy(0, me, sibling, src=x_ref)]
        first += [copy(1 + n, me, (*chip, c), src=x_ref) for n, chip in enumerate(chips)]
        for cp in first:
            cp.start()
        passed = [copy(4 + n, (*chip, c), sibling) for n, chip in enumerate(chips)]
        for n, chip in enumerate(chips):
            copy(1 + n, (*chip, c), me).wait_recv()
            passed[n].start()
        copy(0, sibling, me).wait_recv()
        for n, chip in enumerate(chips):
            copy(4 + n, (*chip, 1 - c), me).wait_recv()
        for cp in first + passed:
            cp.wait_send()
        mine.wait()

    return pl.pallas_call(
        body, name=name, out_shape=jax.ShapeDtypeStruct((N_DEV, r, w), block.dtype),
        in_specs=[_HBM], out_specs=_HBM,
        scratch_shapes=[pltpu.SemaphoreType.DMA((7,)), pltpu.SemaphoreType.DMA((7,)), pltpu.SemaphoreType.DMA],
    )(block)


def _all_to_all(blocks, name):
    def body(x_ref, out_ref, send_sems, recv_sems, local_sem):
        x, y, c = lax.axis_index("x"), lax.axis_index("y"), lax.axis_index("c")
        mine = 4 * x + 2 * y + c
        local = pltpu.make_async_copy(x_ref.at[mine], out_ref.at[mine], local_sem)
        local.start()
        copies = []
        for k in range(1, N_DEV):
            px = 1 - x if k & 4 else x
            py = 1 - y if k & 2 else y
            pc = 1 - c if k & 1 else c
            cp = pltpu.make_async_remote_copy(
                src_ref=x_ref.at[4 * px + 2 * py + pc], dst_ref=out_ref.at[mine],
                send_sem=send_sems.at[k - 1], recv_sem=recv_sems.at[k - 1],
                device_id=(px, py, pc), device_id_type=_MESH_ID)
            cp.start()
            copies.append(cp)
        for cp in copies:
            cp.wait_recv()
        for cp in copies:
            cp.wait_send()
        local.wait()

    return pl.pallas_call(
        body, name=name, out_shape=jax.ShapeDtypeStruct(blocks.shape, blocks.dtype),
        in_specs=[_HBM], out_specs=_HBM,
        scratch_shapes=[pltpu.SemaphoreType.DMA((7,)), pltpu.SemaphoreType.DMA((7,)), pltpu.SemaphoreType.DMA],
    )(blocks)


_SEM = pl.BlockSpec(memory_space=pltpu.SEMAPHORE)
_EFFECT = pltpu.SideEffectType.DATAFLOW_SIDE_EFFECTING


def _exchange_start(srcs, personalized, after, name):
    n = len(srcs)
    lands = [lax.empty((N_DEV,) + s.shape[-2:], s.dtype) for s in srcs]

    def body(*refs):
        src_refs, land_refs = refs[:n], refs[n:2 * n]
        outs = refs[2 * n + 1:]
        send_sems, recv_sems, token = outs[:n], outs[n:2 * n], outs[-1]
        x, y, c = lax.axis_index("x"), lax.axis_index("y"), lax.axis_index("c")
        mine = 4 * x + 2 * y + c
        for ci in range(n):
            for k in range(1, N_DEV):
                px = 1 - x if k & 4 else x
                py = 1 - y if k & 2 else y
                pc = 1 - c if k & 1 else c
                src = src_refs[ci].at[4 * px + 2 * py + pc] if personalized else src_refs[ci]
                pltpu.make_async_remote_copy(
                    src_ref=src, dst_ref=land_refs[ci].at[mine], send_sem=send_sems[ci], recv_sem=recv_sems[ci],
                    device_id=(px, py, pc), device_id_type=_MESH_ID).start()
        token[...] = jnp.zeros_like(token)

    sem = pltpu.SemaphoreType.DMA(())
    out_shape = ([sem] * (2 * n) + [pltpu.HBM(s.shape, s.dtype) for s in srcs]
                 + [pltpu.HBM(l.shape, l.dtype) for l in lands] + [jax.ShapeDtypeStruct((8, LANES), F32)])
    res = pl.pallas_call(
        body, name=name, out_shape=tuple(out_shape),
        in_specs=[_HBM] * (2 * n) + [_ANY],
        out_specs=tuple([_SEM] * (2 * n) + [_HBM] * (2 * n) + [pl.BlockSpec(memory_space=pltpu.VMEM)]),
        input_output_aliases={i: 2 * n + i for i in range(2 * n)},
        compiler_params=pltpu.CompilerParams(has_side_effects=_EFFECT),
    )(*[pltpu.with_memory_space_constraint(s, pltpu.HBM) for s in srcs],
      *[pltpu.with_memory_space_constraint(l, pltpu.HBM) for l in lands], after)
    handles = [(res[ci], res[n + ci], res[2 * n + ci], res[3 * n + ci]) for ci in range(n)]
    return handles, res[-1]


def _exchange_wait(handle, after, name):
    send_sem, recv_sem, src_thru, land_thru = handle

    def body(src_ref, land_ref, send_ref, recv_ref, after_ref, src_out, land_out):
        seven = land_ref.at[pl.ds(0, N_DEV - 1)]
        copies = pltpu.make_async_remote_copy(
            src_ref=seven, dst_ref=seven, send_sem=send_ref, recv_sem=recv_ref,
            device_id=(lax.axis_index("x"), lax.axis_index("y"), lax.axis_index("c")), device_id_type=_MESH_ID)
        copies.wait_send()
        copies.wait_recv()

    return pl.pallas_call(
        body, name=name,
        out_shape=(pltpu.HBM(src_thru.shape, src_thru.dtype), pltpu.HBM(land_thru.shape, land_thru.dtype)),
        in_specs=(_HBM, _HBM, _SEM, _SEM, _ANY), out_specs=(_HBM, _HBM), input_output_aliases={0: 0, 1: 1},
        compiler_params=pltpu.CompilerParams(has_side_effects=_EFFECT),
    )(src_thru, land_thru, send_sem, recv_sem, after)[1]


def _own_slot(land, own, me):
    slot = lax.broadcasted_iota(jnp.int32, (N_DEV, 1, 1), 0)
    return jnp.where(slot == me, own[None], land)


def _sum_slots(slots, name):
    n, r, w = slots.shape
    tr = _tile(r, 128, WIRE_ROW_ALIGN)

    def body(s_ref, o_ref):
        acc = s_ref[0].astype(F32)
        for d in range(1, n):
            acc = acc + s_ref[d].astype(F32)
        o_ref[...] = acc

    return pl.pallas_call(
        body, name=name, grid=(r // tr,),
        in_specs=[pl.BlockSpec((n, tr, w), lambda i: (0, i, 0))],
        out_specs=pl.BlockSpec((tr, w), lambda i: (i, 0)),
        out_shape=jax.ShapeDtypeStruct((r, w), F32),
        compiler_params=_cparams(("parallel",), 2 * _nbytes((n, tr, w), slots.dtype) + 4 * _nbytes((tr, w), F32)),
    )(slots)


def _adamw(w, g, m, v, name):
    r, c = w.shape
    tr = _tile(r, 512, 8)

    def body(w_ref, g_ref, m_ref, v_ref, d_ref, mo_ref, vo_ref):
        gv = g_ref[...]
        m_new = ADAM_B1 * m_ref[...] + (1.0 - ADAM_B1) * gv
        v_new = ADAM_B2 * v_ref[...] + (1.0 - ADAM_B2) * (gv * gv)
        m_hat = m_new / (1.0 - ADAM_B1 ** ADAM_STEP)
        v_hat = v_new / (1.0 - ADAM_B2 ** ADAM_STEP)
        d_ref[...] = -ADAM_LR * (m_hat / (jnp.sqrt(v_hat) + ADAM_EPS) + ADAM_WD * w_ref[...])
        mo_ref[...] = m_new
        vo_ref[...] = v_new

    blk = pl.BlockSpec((tr, c), lambda i: (i, 0))
    shp = jax.ShapeDtypeStruct((r, c), F32)
    return pl.pallas_call(
        body, name=name, grid=(r // tr,), in_specs=[blk] * 4, out_specs=[blk] * 3, out_shape=[shp] * 3,
        compiler_params=_cparams(("parallel",), 16 * _nbytes((tr, _round_up(c, LANES)), F32)),
    )(w, g, m, v)


def _pack_rows(parts, width, dtype, row_align):
    rows, spans, off = [], [], 0
    for p in parts:
        flat = p.reshape(-1).astype(dtype)
        n_rows = _round_up(-(-flat.shape[0] // width), row_align)
        flat = jnp.pad(flat, (0, n_rows * width - flat.shape[0]))
        rows.append(flat.reshape(n_rows, width))
        spans.append((off, n_rows))
        off += n_rows
    return jnp.concatenate(rows, axis=0), spans


def _unpack_rows(mat, span, shape):
    off, n_rows = span
    n = 1
    for s in shape:
        n *= s
    return mat[..., off:off + n_rows, :].reshape(mat.shape[:-2] + (-1,))[..., :n].reshape(mat.shape[:-2] + tuple(shape))


def _block_diag(w, size):
    n, b, _ = w.shape
    eye = jnp.eye(n, dtype=w.dtype)
    dense = (w[:, :, None, :] * eye[:, None, :, None]).reshape(n * b, n * b)
    return jnp.pad(dense, ((0, size - n * b), (0, size - n * b)))


def _diag_blocks(dense, n, b):
    return jnp.stack([dense[k * b:(k + 1) * b, k * b:(k + 1) * b] for k in range(n)])


def _pad_rows(a, rows):
    return jnp.pad(a, ((0, rows - a.shape[0]), (0, 0)))


def _pad_cols(a, cols):
    return jnp.pad(a, ((0, 0), (0, cols - a.shape[1])))


def _train_step(a):
    x = a["x"][0]
    target = a["loss_target"][0]
    s_dim, d = x.shape
    n_layers = a["ffn1_pre_g"].shape[0]
    f_shard = a["ffn1_w_gate"].shape[2]
    c_shard = a["rg_conv_b"].shape[1]
    c_dim = c_shard * N_DEV
    cp = _round_up(c_dim, LANES)
    conv_width = a["rg_conv_w"].shape[1]
    n_blocks, lru_block = a["rg_w_a"].shape[1], a["rg_w_a"].shape[2]
    d_attn = a["attn_w_q"].shape[2]
    n_heads = a["b_fgate"].shape[0]
    d_head = d_attn // n_heads
    attn_scale = d_head ** -0.5
    assert conv_width < 8 and n_heads <= LANES and n_layers == 2
    assert d_attn == d
    me = 4 * lax.axis_index("x") + 2 * lax.axis_index("y") + lax.axis_index("c")

    shard = {"rg_w_in": a["rg_w_in"][0].T, "rg_w_out": a["rg_w_out"][0], "w_kv": a["w_kv"].T,
             "attn_w_q": a["attn_w_q"][0], "attn_w_o": a["attn_w_o"][0]}
    for l in range(n_layers):
        for f in ("ffn1", "ffn2"):
            shard[(f, "gate", l)] = a[f + "_w_gate"][l].T
            shard[(f, "up", l)] = a[f + "_w_up"][l].T
            shard[(f, "down", l)] = a[f + "_w_down"][l]

    def ffn_names(f, l):
        return [(f, "gate", l), (f, "up", l), (f, "down", l)]

    def chunk_layout(names):
        spans, off = [], 0
        for nm in names:
            spans.append((nm, off, shard[nm].shape[0]))
            off += _round_up(shard[nm].shape[0], WIRE_ROW_ALIGN)
        return spans, off

    def pack_chunk(names, parts):
        return jnp.concatenate(
            [_pad_rows(parts[nm].astype(WIRE_DTYPE), _round_up(parts[nm].shape[0], WIRE_ROW_ALIGN)) for nm in names], axis=0)

    full = {}

    def unpack_chunk(names, gathered):
        for nm, o, n_rows in chunk_layout(names)[0]:
            full[nm] = gathered[:, o:o + n_rows, :].reshape(N_DEV * n_rows, d)

    fwd_chunks = [ffn_names("ffn1", 0), ["rg_w_in", "rg_w_out"], ffn_names("ffn2", 0) + ["w_kv"],
                  ffn_names("ffn1", 1) + ["attn_w_q", "attn_w_o"], ffn_names("ffn2", 1)]
    fwd_packs = [pack_chunk(names, shard) for names in fwd_chunks]
    unpack_chunk(fwd_chunks[0], _all_gather(fwd_packs[0], "gather_weights_first"))
    fwd_handles, fwd_token = _exchange_start(fwd_packs[1:], False, full[("ffn1", "down", 0)], "gather_weights_start")

    def land_weights(n, after):
        land = _exchange_wait(fwd_handles[n - 1], after, f"gather_weights_wait_{n}")
        unpack_chunk(fwd_chunks[n], _own_slot(land, fwd_packs[n], me))

    small_parts = [a["rg_conv_w"][0], a["rg_conv_b"][0], a["rg_b_a"][0], a["rg_b_x"][0], a["rg_lambda"][0], a["w_fgate"]]
    small_pack, small_spans = _pack_rows(small_parts, d, F32, 8)
    small_all = _all_gather(small_pack, "gather_small")
    sm = [_unpack_rows(small_all, sp, p.shape) for sp, p in zip(small_spans, small_parts)]
    conv_w = jnp.moveaxis(sm[0], 0, 1).reshape(conv_width, c_dim)
    conv_b, b_a, b_x, lam = (v.reshape(1, c_dim) for v in sm[1:5])
    w_f = sm[5].reshape(d, n_heads)

    pconv = _pad_rows(_pad_cols(jnp.concatenate([conv_w, conv_b], axis=0), cp), 8)
    pvec = _pad_rows(_pad_cols(jnp.concatenate([b_a, b_x, lam], axis=0), cp), 8)
    wa_dense = _block_diag(a["rg_w_a"][0], cp).astype(MXU_DTYPE)
    wx_dense = _block_diag(a["rg_w_x"][0], cp).astype(MXU_DTYPE)
    wax = jnp.concatenate([wa_dense, wx_dense], axis=1)
    w_f_t = _pad_rows(w_f.T.astype(MXU_DTYPE), LANES)
    b_f = _pad_rows(_pad_cols(a["b_fgate"].reshape(1, n_heads), LANES), 8)

    def gain(name, l):
        return a[name][l].reshape(1, d)

    def ffn_fwd(h, f, l, after=None):
        xn = _rms_fwd(h, gain(f + "_pre_g", l), f"{f}_{l}_pre_norm", after)
        g, u, act = _ffn_up(xn, full[(f, "gate", l)], full[(f, "up", l)], f"{f}_{l}_up")
        fo, h_new = _mm_rms_res(act, full[(f, "down", l)], h, gain(f + "_post_g", l), 0.5, f"{f}_{l}_down")
        return h_new, (h, xn, g, u, act, fo)

    def heads(t2):
        return t2.reshape(s_dim, n_heads, d_head).transpose(1, 0, 2)

    def unheads(t3):
        return t3.transpose(1, 0, 2).reshape(s_dim, n_heads * d_head)

    h0 = x
    h0a, sv_f1_0 = ffn_fwd(h0, "ffn1", 0, fwd_token)
    land_weights(1, h0a)
    w_in_gate = _pad_rows(full["rg_w_in"][:c_dim], cp)
    w_in_rec = _pad_rows(full["rg_w_in"][c_dim:], cp)
    w_in_t = jnp.concatenate([w_in_gate, w_in_rec], axis=0)
    w_out = _pad_rows(full["rg_w_out"], cp)
    hn_rg = _rms_fwd(h0a, gain("mix_pre_g", 0), "rg_pre_norm")
    gx = _mm([(hn_rg, w_in_t)], "nt", F32, "rg_in_proj")
    rec = _conv_fwd(gx, pconv, conv_width, "rg_conv")
    gates = _mm([(rec, wax)], "nn", F32, "rg_gate_proj")
    h_rec, y_rg = _scan_fwd(gx, rec, gates, pvec, "rg_scan")
    m_rg, h0b = _mm_rms_res(y_rg, w_out, h0a, gain("mix_post_g", 0), 1.0, "rg_out_proj")
    land_weights(2, h0b)
    h1, sv_f2_0 = ffn_fwd(h0b, "ffn2", 0)
    hn_kv = _rms_fwd(h1, a["kv_norm_g"].reshape(1, d), "kv_norm")
    kv = _mm([(hn_kv, full["w_kv"])], "nt", MXU_DTYPE, "kv_proj")
    fpre = _mm([(hn_kv, w_f_t)], "nt", F32, "fgate_proj")
    c_cum = _fgate_fwd(fpre, b_f, "fgate_cumsum")
    k_h, v_h = heads(kv[:, :d_attn]), heads(kv[:, d_attn:])
    c_heads = c_cum[:, :n_heads].T
    c_col, c_row = c_heads[:, :, None], c_heads[:, None, :]
    land_weights(3, c_cum)
    h1a, sv_f1_1 = ffn_fwd(h1, "ffn1", 1)
    hn_at = _rms_fwd(h1a, gain("mix_pre_g", 1), "attn_pre_norm")
    q2 = _mm([(hn_at, full["attn_w_q"])], "nn", F32, "q_proj")
    q_aug, k_aug = _augment_qk(heads(q2), k_h, attn_scale)
    o_h, lse = _flash_fwd(q_aug, k_aug, v_h, c_col, c_row, "attn_fwd")
    o2 = unheads(o_h)
    m_at, h1b = _mm_rms_res(o2, full["attn_w_o"], h1a, gain("mix_post_g", 1), 1.0, "attn_out_proj")
    land_weights(4, h1b)
    y, sv_f2_1 = ffn_fwd(h1b, "ffn2", 1)
    dy, loss_part = _loss_head(y, target, "loss_head")

    grads_big = {}
    grads_rep = {}

    bwd_chunks = [ffn_names("ffn2", 1), ["attn_w_q", "attn_w_o"] + ffn_names("ffn1", 1),
                  ["w_kv"] + ffn_names("ffn2", 0), ["rg_w_in", "rg_w_out"], ffn_names("ffn1", 0)]
    bwd_sends, bwd_handles = [], []

    def send_grads(after):
        n = len(bwd_sends)
        send = jnp.concatenate(
            [jnp.pad(grads_big[nm].reshape(N_DEV, n_rows, d), ((0, 0), (0, _round_up(n_rows, WIRE_ROW_ALIGN) - n_rows), (0, 0)))
             for nm, _, n_rows in chunk_layout(bwd_chunks[n])[0]], axis=1)
        handles, token = _exchange_start([send], True, after, f"exchange_grads_start_{n}")
        bwd_sends.append(send)
        bwd_handles.append(handles[0])
        return token

    def ffn_bwd(dh_out, saved, f, l, after=None):
        h, xn, g, u, act, fo = saved
        df, d_post = _rms_bwd(fo, gain(f + "_post_g", l), [dh_out], None, 0.5, MXU_DTYPE, f"{f}_{l}_post_norm_bwd", after)
        dg, du = _ffn_act_bwd(df, full[(f, "down", l)], g, u, f"{f}_{l}_act_bwd")
        grads_big[(f, "down", l)] = _mm([(act, df)], "tn", WIRE_DTYPE, f"{f}_{l}_dw_down")
        grads_big[(f, "gate", l)] = _mm([(dg, xn)], "tn", WIRE_DTYPE, f"{f}_{l}_dw_gate")
        grads_big[(f, "up", l)] = _mm([(du, xn)], "tn", WIRE_DTYPE, f"{f}_{l}_dw_up")
        dxn = _mm([(dg, full[(f, "gate", l)]), (du, full[(f, "up", l)])], "nn", F32, f"{f}_{l}_dx")
        dh_in, d_pre = _rms_bwd(h, gain(f + "_pre_g", l), [dxn], dh_out, 1.0, F32, f"{f}_{l}_pre_norm_bwd")
        grads_rep[(f + "_post_g", l)] = d_post
        grads_rep[(f + "_pre_g", l)] = d_pre
        return dh_in

    dh = ffn_bwd(dy, sv_f2_1, "ffn2", 1)
    token = send_grads(dh)
    dm, d_post = _rms_bwd(m_at, gain("mix_post_g", 1), [dh], None, 1.0, MXU_DTYPE, "attn_post_norm_bwd", token)
    grads_rep[("mix_post_g", 1)] = d_post
    do2 = _mm([(dm, full["attn_w_o"])], "nt", F32, "attn_out_proj_dx")
    grads_big["attn_w_o"] = _mm([(o2, dm)], "tn", WIRE_DTYPE, "attn_out_proj_dw")
    dq_aug, dk_aug, dv_h = _flash_bwd(q_aug, k_aug, v_h, c_col, c_row, o_h, lse, heads(do2), "attn_bwd")
    dk_h = dk_aug[:, :, :d_head]
    dc_heads = dq_aug[:, :, d_head] - dk_aug[:, :, d_head + 1]
    dq2 = unheads(dq_aug[:, :, :d_head] * attn_scale)
    dhn = _mm([(dq2, full["attn_w_q"])], "nt", F32, "q_proj_dx")
    grads_big["attn_w_q"] = _mm([(hn_at, dq2)], "tn", WIRE_DTYPE, "q_proj_dw")
    dh, d_pre = _rms_bwd(h1a, gain("mix_pre_g", 1), [dhn], dh, 1.0, F32, "attn_pre_norm_bwd")
    grads_rep[("mix_pre_g", 1)] = d_pre
    dh = ffn_bwd(dh, sv_f1_1, "ffn1", 1)
    token = send_grads(dh)
    dkv = jnp.concatenate([unheads(dk_h), unheads(dv_h)], axis=1)
    dc_cum = _pad_cols(dc_heads.T, LANES)
    dfpre, db_f = _fgate_bwd(dc_cum, fpre, b_f, "fgate_cumsum_bwd")
    dhn_kv = _mm([(dkv, full["w_kv"])], "nn", F32, "kv_proj_dx")
    dhn_f = _mm([(dfpre, w_f_t)], "nn", F32, "fgate_proj_dx")
    grads_big["w_kv"] = _mm([(dkv, hn_kv)], "tn", WIRE_DTYPE, "kv_proj_dw")
    dw_f_t = _mm([(dfpre, hn_kv)], "tn", F32, "fgate_proj_dw")
    dh, d_kvg = _rms_bwd(h1, a["kv_norm_g"].reshape(1, d), [dhn_kv, dhn_f], dh, 1.0, F32, "kv_norm_bwd", token)
    dh = ffn_bwd(dh, sv_f2_0, "ffn2", 0)
    token = send_grads(dh)
    dm, d_post = _rms_bwd(m_rg, gain("mix_post_g", 0), [dh], None, 1.0, MXU_DTYPE, "rg_post_norm_bwd", token)
    grads_rep[("mix_post_g", 0)] = d_post
    dy_rg = _mm([(dm, w_out)], "nt", F32, "rg_out_proj_dx")
    dw_out = _mm([(y_rg, dm)], "tn", WIRE_DTYPE, "rg_out_proj_dw")
    dgate, dra, dia, drec1, dpvec = _scan_bwd(dy_rg, gx, h_rec, rec, gates, pvec, "rg_scan_bwd")
    drec2 = _mm([(dra, wa_dense), (dia, wx_dense)], "nt", F32, "rg_gate_proj_dx")
    dwa_dense = _mm([(rec, dra)], "tn", F32, "rg_gate_proj_dwa")
    dwx_dense = _mm([(rec, dia)], "tn", F32, "rg_gate_proj_dwx")
    drec0, dpconv = _conv_bwd(drec1, drec2, gx, pconv, conv_width, "rg_conv_bwd")
    dhn = _mm([(dgate, w_in_gate), (drec0, w_in_rec)], "nn", F32, "rg_in_proj_dx")
    dw_in_gate = _mm([(dgate, hn_rg)], "tn", WIRE_DTYPE, "rg_in_proj_dw_gate")
    dw_in_rec = _mm([(drec0, hn_rg)], "tn", WIRE_DTYPE, "rg_in_proj_dw_rec")
    dh, d_pre = _rms_bwd(h0a, gain("mix_pre_g", 0), [dhn], dh, 1.0, F32, "rg_pre_norm_bwd")
    grads_rep[("mix_pre_g", 0)] = d_pre
    grads_big["rg_w_in"] = jnp.concatenate([dw_in_gate[:c_dim], dw_in_rec[:c_dim]], axis=0)
    grads_big["rg_w_out"] = dw_out[:c_dim]
    token = send_grads(dh)
    grad_x = ffn_bwd(dh, sv_f1_0, "ffn1", 0, token)
    send_grads(grad_x)

    g_shard = {}
    for n, names in enumerate(bwd_chunks):
        land = _exchange_wait(bwd_handles[n], grad_x, f"exchange_grads_wait_{n}")
        own = lax.dynamic_index_in_dim(bwd_sends[n], me, axis=0, keepdims=False)
        g_chunk = _sum_slots(_own_slot(land, own, me), f"sum_weight_grads_{n}")
        for nm, o, n_rows in chunk_layout(names)[0]:
            g_shard[nm] = g_chunk[o:o + n_rows]

    def gain_grad(name):
        return jnp.concatenate([grads_rep[(name, l)] for l in range(n_layers)], axis=0)

    rep_names = ["ffn1_pre_g", "ffn1_post_g", "mix_pre_g", "mix_post_g", "ffn2_pre_g", "ffn2_post_g"]
    rep_parts = [gain_grad(nm) for nm in rep_names]
    rep_names += ["kv_norm_g", "b_fgate", "rg_w_a", "rg_w_x", "rg_conv_w", "rg_conv_b", "rg_b_a", "rg_b_x", "rg_lambda", "w_fgate"]
    rep_parts += [
        d_kvg, db_f[0, :n_heads],
        _diag_blocks(dwa_dense, n_blocks, lru_block), _diag_blocks(dwx_dense, n_blocks, lru_block),
        dpconv[:conv_width, :c_dim], dpconv[conv_width, :c_dim],
        dpvec[0, :c_dim], dpvec[1, :c_dim], dpvec[2, :c_dim],
        dw_f_t[:n_heads].T]
    rep_pack, rep_spans = _pack_rows(rep_parts, d, F32, WIRE_ROW_ALIGN)
    rep_sum = _sum_slots(_all_gather(rep_pack, "gather_small_grads"), "sum_small_grads")
    g_rep = {nm: _unpack_rows(rep_sum, sp, p.shape) for nm, sp, p in zip(rep_names, rep_spans, rep_parts)}

    def my_cols(full_grad, n):
        return lax.dynamic_slice_in_dim(full_grad, me * n, n, axis=full_grad.ndim - 1)

    grad = {}
    for nm in ("ffn1_pre_g", "ffn1_post_g", "mix_pre_g", "mix_post_g", "ffn2_pre_g", "ffn2_post_g"):
        grad[nm] = g_rep[nm]
    for f in ("ffn1", "ffn2"):
        grad[f + "_w_gate"] = jnp.stack([g_shard[(f, "gate", l)].T for l in range(n_layers)])
        grad[f + "_w_up"] = jnp.stack([g_shard[(f, "up", l)].T for l in range(n_layers)])
        grad[f + "_w_down"] = jnp.stack([g_shard[(f, "down", l)] for l in range(n_layers)])
    grad["rg_w_in"] = g_shard["rg_w_in"].T[None]
    grad["rg_conv_w"] = my_cols(g_rep["rg_conv_w"], c_shard)[None]
    for nm in ("rg_conv_b", "rg_b_a", "rg_b_x", "rg_lambda"):
        grad[nm] = my_cols(g_rep[nm], c_shard)[None]
    grad["rg_w_a"] = g_rep["rg_w_a"][None]
    grad["rg_w_x"] = g_rep["rg_w_x"][None]
    grad["rg_w_out"] = g_shard["rg_w_out"][None]
    grad["kv_norm_g"] = g_rep["kv_norm_g"].reshape(d)
    grad["w_kv"] = g_shard["w_kv"].T
    grad["w_fgate"] = lax.dynamic_slice_in_dim(g_rep["w_fgate"], me * (d // N_DEV), d // N_DEV, axis=0)
    grad["b_fgate"] = g_rep["b_fgate"]
    grad["attn_w_q"] = g_shard["attn_w_q"][None]
    grad["attn_w_o"] = g_shard["attn_w_o"][None]

    delta, new_m, new_v = {}, {}, {}
    for nm in WEIGHT_NAMES:
        w = a[nm]
        shape = w.shape
        two_d = (1, shape[0]) if w.ndim == 1 else (-1, shape[-1])
        dl, mo, vo = _adamw(w.reshape(two_d), grad[nm].reshape(two_d), a["m_" + nm].reshape(two_d),
                            a["v_" + nm].reshape(two_d), "adamw_" + nm)
        delta[nm], new_m[nm], new_v[nm] = dl.reshape(shape), mo.reshape(shape), vo.reshape(shape)
        grad[nm] = grad[nm].reshape(shape)

    loss = lax.psum(loss_part[0, 0], AXES)
    return (loss, grad_x[None], *[grad[n] for n in WEIGHT_NAMES], *[delta[n] for n in WEIGHT_NAMES],
            *[new_m[n] for n in WEIGHT_NAMES], *[new_v[n] for n in WEIGHT_NAMES])


def kernel(x, ffn1_pre_g, ffn1_w_gate, ffn1_w_up, ffn1_w_down, ffn1_post_g, mix_pre_g, mix_post_g, ffn2_pre_g, ffn2_w_gate, ffn2_w_up, ffn2_w_down, ffn2_post_g, rg_w_in, rg_conv_w, rg_conv_b, rg_w_a, rg_b_a, rg_w_x, rg_b_x, rg_lambda, rg_w_out, kv_norm_g, w_kv, w_fgate, b_fgate, attn_w_q, attn_w_o, loss_target, m_ffn1_pre_g, m_ffn1_w_gate, m_ffn1_w_up, m_ffn1_w_down, m_ffn1_post_g, m_mix_pre_g, m_mix_post_g, m_ffn2_pre_g, m_ffn2_w_gate, m_ffn2_w_up, m_ffn2_w_down, m_ffn2_post_g, m_rg_w_in, m_rg_conv_w, m_rg_conv_b, m_rg_w_a, m_rg_b_a, m_rg_w_x, m_rg_b_x, m_rg_lambda, m_rg_w_out, m_kv_norm_g, m_w_kv, m_w_fgate, m_b_fgate, m_attn_w_q, m_attn_w_o, v_ffn1_pre_g, v_ffn1_w_gate, v_ffn1_w_up, v_ffn1_w_down, v_ffn1_post_g, v_mix_pre_g, v_mix_post_g, v_ffn2_pre_g, v_ffn2_w_gate, v_ffn2_w_up, v_ffn2_w_down, v_ffn2_post_g, v_rg_w_in, v_rg_conv_w, v_rg_conv_b, v_rg_w_a, v_rg_b_a, v_rg_w_x, v_rg_b_x, v_rg_lambda, v_rg_w_out, v_kv_norm_g, v_w_kv, v_w_fgate, v_b_fgate, v_attn_w_q, v_attn_w_o):
    return _train_step(dict(locals()))
```

```python
import functools

import jax
import jax.numpy as jnp
from jax import lax
from jax.experimental import pallas as pl
from jax.experimental.pallas import tpu as pltpu

F32 = jnp.float32
MXU_DTYPE = jnp.bfloat16
WIRE_DTYPE = jnp.bfloat16
N_DEV = 8
AXES = ("x", "y", "c")
LANES = 128
WIRE_ROW_ALIGN = 16
VMEM_LIMIT_MIN = 32 * 2 ** 20
VMEM_LIMIT_MAX = 56 * 2 ** 20

RMS_EPS = 1e-6
LRU_C = 8.0
ADAM_LR, ADAM_B1, ADAM_B2, ADAM_EPS, ADAM_WD, ADAM_STEP = 0.001, 0.9, 0.999, 1e-08, 0.01, 10

WEIGHT_NAMES = (
    "ffn1_pre_g", "ffn1_w_gate", "ffn1_w_up", "ffn1_w_down", "ffn1_post_g", "mix_pre_g", "mix_post_g",
    "ffn2_pre_g", "ffn2_w_gate", "ffn2_w_up", "ffn2_w_down", "ffn2_post_g", "rg_w_in", "rg_conv_w",
    "rg_conv_b", "rg_w_a", "rg_b_a", "rg_w_x", "rg_b_x", "rg_lambda", "rg_w_out", "kv_norm_g", "w_kv",
    "w_fgate", "b_fgate", "attn_w_q", "attn_w_o")


def _round_up(n, m):
    return (n + m - 1) // m * m


def _tile(dim, target, align=LANES):
    if dim <= target:
        return dim
    best = None
    t = align
    while t <= target:
        if dim % t == 0:
            best = t
        t += align
    return dim if best is None else best


def _cparams(semantics, vmem_estimate):
    limit = min(VMEM_LIMIT_MAX, max(VMEM_LIMIT_MIN, 2 * int(vmem_estimate)))
    return pltpu.CompilerParams(dimension_semantics=semantics, vmem_limit_bytes=limit)


def _nbytes(shape, dtype):
    n = 1
    for s in shape:
        n *= s
    return n * jnp.dtype(dtype).itemsize


def _sigmoid(x):
    return jax.nn.sigmoid(x)


def _softplus(x):
    return jnp.maximum(x, 0.0) + jnp.log1p(jnp.exp(-jnp.abs(x)))


def _expm1(x):
    series = x * (1.0 + x * (0.5 + x * (1.0 / 6.0 + x * (1.0 / 24.0 + x * (1.0 / 120.0)))))
    return jnp.where(jnp.abs(x) < 0.25, series, jnp.exp(x) - 1.0)


_GELU_C = 0.7978845608028654
_GELU_A = 0.044715


def _gelu(x):
    return 0.5 * x * (1.0 + jnp.tanh(_GELU_C * (x + _GELU_A * x * x * x)))


def _gelu_grad(x):
    t = jnp.tanh(_GELU_C * (x + _GELU_A * x * x * x))
    return 0.5 * (1.0 + t) + 0.5 * x * (1.0 - t * t) * _GELU_C * (1.0 + 3.0 * _GELU_A * x * x)


_DOT_DIMS = {"nn": ((1,), (0,)), "nt": ((1,), (1,)), "tn": ((0,), (0,))}


def _dot(a, b, mode):
    return lax.dot_general(a.astype(MXU_DTYPE), b.astype(MXU_DTYPE), (_DOT_DIMS[mode], ((), ())),
                           preferred_element_type=F32)


def _mm(pairs, mode, out_dtype, name, after=None):
    a0, b0 = pairs[0]
    if mode == "tn":
        k_dim, m_dim = a0.shape
        n_dim = b0.shape[1]
    else:
        m_dim, k_dim = a0.shape
        n_dim = b0.shape[0] if mode == "nt" else b0.shape[1]
    for a, b in pairs:
        assert a.shape == a0.shape and b.shape == b0.shape
    tm = _tile(m_dim, 1408 if mode == "tn" else 512)
    tn = _tile(n_dim, 1408)
    tk = _tile(k_dim, 1408)
    nk = k_dim // tk
    n_pairs = len(pairs)

    if mode == "tn":
        a_spec = pl.BlockSpec((tk, tm), lambda i, j, k: (k, i))
    else:
        a_spec = pl.BlockSpec((tm, tk), lambda i, j, k: (i, k))
    if mode == "nt":
        b_spec = pl.BlockSpec((tn, tk), lambda i, j, k: (j, k))
    else:
        b_spec = pl.BlockSpec((tk, tn), lambda i, j, k: (k, j))

    order = [] if after is None else [after]

    def body(*refs):
        ins, o_ref, acc = refs[:2 * n_pairs], refs[-2], refs[-1]
        k = pl.program_id(2)

        @pl.when(k == 0)
        def _():
            acc[...] = jnp.zeros_like(acc)

        s = acc[...]
        for p in range(n_pairs):
            s = s + _dot(ins[2 * p][...], ins[2 * p + 1][...], mode)
        acc[...] = s

        @pl.when(k == nk - 1)
        def _():
            o_ref[...] = acc[...].astype(out_dtype)

    est = (2 * n_pairs * (_nbytes((tm, tk), a0.dtype) + _nbytes((tk, tn), b0.dtype))
           + 2 * _nbytes((tm, tn), out_dtype) + 2 * _nbytes((tm, tn), F32))
    flat = [t for ab in pairs for t in ab]
    return pl.pallas_call(
        body, name=name, grid=(m_dim // tm, n_dim // tn, nk),
        in_specs=[a_spec, b_spec] * n_pairs + [_ANY] * len(order),
        out_specs=pl.BlockSpec((tm, tn), lambda i, j, k: (i, j)),
        out_shape=jax.ShapeDtypeStruct((m_dim, n_dim), out_dtype),
        scratch_shapes=[pltpu.VMEM((tm, tn), F32)],
        compiler_params=_cparams(("parallel", "parallel", "arbitrary"), est),
    )(*flat, *order)


_ANY = pl.BlockSpec(memory_space=pl.ANY)


def _rms_fwd(x, gain, name, after=None):
    s_dim, d = x.shape
    tm = _tile(s_dim, 512, 8)

    def body(*refs):
        x_ref, g_ref, o_ref = refs[0], refs[1], refs[-1]
        v = x_ref[...]
        r = lax.rsqrt(jnp.mean(v * v, axis=-1, keepdims=True) + RMS_EPS)
        o_ref[...] = (v * r * g_ref[...]).astype(MXU_DTYPE)

    order = [] if after is None else [after]
    return pl.pallas_call(
        body, name=name, grid=(s_dim // tm,),
        in_specs=[pl.BlockSpec((tm, d), lambda i: (i, 0)), pl.BlockSpec((1, d), lambda i: (0, 0))] + [_ANY] * len(order),
        out_specs=pl.BlockSpec((tm, d), lambda i: (i, 0)),
        out_shape=jax.ShapeDtypeStruct((s_dim, d), MXU_DTYPE),
        compiler_params=_cparams(("parallel",), 6 * _nbytes((tm, d), F32)),
    )(x, gain, *order)


def _rms_bwd(x, gain, dys, res, scale, out_dtype, name, after=None):
    s_dim, d = x.shape
    tm = _tile(s_dim, 512, 8)
    n_dy = len(dys)
    has_res = res is not None
    order = [] if after is None else [after]

    def body(*refs):
        x_ref, g_ref = refs[0], refs[1]
        dy_refs = refs[2:2 + n_dy]
        res_ref = refs[2 + n_dy] if has_res else None
        dx_ref, dg_ref = refs[-2], refs[-1]

        @pl.when(pl.program_id(0) == 0)
        def _():
            dg_ref[...] = jnp.zeros_like(dg_ref)

        v = x_ref[...]
        r = lax.rsqrt(jnp.mean(v * v, axis=-1, keepdims=True) + RMS_EPS)
        xh = v * r
        dy = dy_refs[0][...].astype(F32)
        for extra in dy_refs[1:]:
            dy = dy + extra[...].astype(F32)
        gd = dy * g_ref[...]
        dx = scale * r * (gd - xh * jnp.mean(gd * xh, axis=-1, keepdims=True))
        if has_res:
            dx = dx + res_ref[...]
        dx_ref[...] = dx.astype(out_dtype)
        dg_ref[...] += scale * jnp.sum(dy * xh, axis=0, keepdims=True)

    row = pl.BlockSpec((tm, d), lambda i: (i, 0))
    vec = pl.BlockSpec((1, d), lambda i: (0, 0))
    ops = [x, gain] + list(dys) + ([res] if has_res else [])
    return pl.pallas_call(
        body, name=name, grid=(s_dim // tm,),
        in_specs=[row, vec] + [row] * (n_dy + int(has_res)) + [_ANY] * len(order),
        out_specs=[row, vec],
        out_shape=[jax.ShapeDtypeStruct((s_dim, d), out_dtype), jax.ShapeDtypeStruct((1, d), F32)],
        compiler_params=_cparams(("arbitrary",), (2 * len(ops) + 6) * _nbytes((tm, d), F32)),
    )(*ops, *order)


def _mm_rms_res(a, b, h, gain, scale, name):
    s_dim, k_dim = a.shape
    d = b.shape[1]
    tm = _tile(s_dim, 512, 8)
    tk = _tile(k_dim, 1408)
    nk = k_dim // tk

    def body(a_ref, b_ref, h_ref, g_ref, f_ref, o_ref, acc):
        k = pl.program_id(1)

        @pl.when(k == 0)
        def _():
            acc[...] = jnp.zeros_like(acc)

        acc[...] += _dot(a_ref[...], b_ref[...], "nn")

        @pl.when(k == nk - 1)
        def _():
            f = acc[...]
            r = lax.rsqrt(jnp.mean(f * f, axis=-1, keepdims=True) + RMS_EPS)
            f_ref[...] = f
            o_ref[...] = h_ref[...] + scale * (f * r * g_ref[...])

    row = pl.BlockSpec((tm, d), lambda i, k: (i, 0))
    est = (2 * (_nbytes((tm, tk), a.dtype) + _nbytes((tk, d), b.dtype)) + 8 * _nbytes((tm, d), F32))
    return pl.pallas_call(
        body, name=name, grid=(s_dim // tm, nk),
        in_specs=[pl.BlockSpec((tm, tk), lambda i, k: (i, k)), pl.BlockSpec((tk, d), lambda i, k: (k, 0)),
                  row, pl.BlockSpec((1, d), lambda i, k: (0, 0))],
        out_specs=[row, row],
        out_shape=[jax.ShapeDtypeStruct((s_dim, d), F32), jax.ShapeDtypeStruct((s_dim, d), F32)],
        scratch_shapes=[pltpu.VMEM((tm, d), F32)],
        compiler_params=_cparams(("parallel", "arbitrary"), est),
    )(a, b, h, gain)


def _ffn_up(xn, wg_t, wu_t, name):
    s_dim, d = xn.shape
    f_dim = wg_t.shape[0]
    tm = _tile(s_dim, 1024, 8)
    tf = _tile(f_dim, 256)

    def body(x_ref, wg_ref, wu_ref, g_ref, u_ref, a_ref):
        x = x_ref[...]
        g = _dot(x, wg_ref[...], "nt")
        u = _dot(x, wu_ref[...], "nt")
        g_ref[...] = g.astype(MXU_DTYPE)
        u_ref[...] = u.astype(MXU_DTYPE)
        a_ref[...] = (g * _sigmoid(g) * u).astype(MXU_DTYPE)

    w_spec = pl.BlockSpec((tf, d), lambda i, j: (j, 0))
    o_spec = pl.BlockSpec((tm, tf), lambda i, j: (i, j))
    o_shape = jax.ShapeDtypeStruct((s_dim, f_dim), MXU_DTYPE)
    est = 2 * _nbytes((tm, d), xn.dtype) + 4 * _nbytes((tf, d), wg_t.dtype) + 10 * _nbytes((tm, tf), F32)
    return pl.pallas_call(
        body, name=name, grid=(s_dim // tm, f_dim // tf),
        in_specs=[pl.BlockSpec((tm, d), lambda i, j: (i, 0)), w_spec, w_spec],
        out_specs=[o_spec, o_spec, o_spec], out_shape=[o_shape, o_shape, o_shape],
        compiler_params=_cparams(("parallel", "parallel"), est),
    )(xn, wg_t, wu_t)


def _ffn_act_bwd(df, wd, g, u, name):
    s_dim, d = df.shape
    f_dim = wd.shape[0]
    tm = _tile(s_dim, 1024, 8)
    tf = _tile(f_dim, 256)

    def body(df_ref, wd_ref, g_ref, u_ref, dg_ref, du_ref):
        dh = _dot(df_ref[...], wd_ref[...], "nt")
        gv = g_ref[...].astype(F32)
        uv = u_ref[...].astype(F32)
        sg = _sigmoid(gv)
        dg_ref[...] = (dh * uv * (sg * (1.0 + gv * (1.0 - sg)))).astype(MXU_DTYPE)
        du_ref[...] = (dh * gv * sg).astype(MXU_DTYPE)

    t_spec = pl.BlockSpec((tm, tf), lambda i, j: (i, j))
    o_shape = jax.ShapeDtypeStruct((s_dim, f_dim), MXU_DTYPE)
    est = 2 * _nbytes((tm, d), df.dtype) + 2 * _nbytes((tf, d), wd.dtype) + 12 * _nbytes((tm, tf), F32)
    return pl.pallas_call(
        body, name=name, grid=(s_dim // tm, f_dim // tf),
        in_specs=[pl.BlockSpec((tm, d), lambda i, j: (i, 0)), pl.BlockSpec((tf, d), lambda i, j: (j, 0)),
                  t_spec, t_spec],
        out_specs=[t_spec, t_spec], out_shape=[o_shape, o_shape],
        compiler_params=_cparams(("parallel", "parallel"), est),
    )(df, wd, g, u)


def _loss_head(y, target, name):
    s_dim, d = y.shape
    tm = _tile(s_dim, 512, 8)
    nt = s_dim // tm

    def body(y_ref, t_ref, dy_ref, loss_ref, acc):
        i = pl.program_id(0)

        @pl.when(i == 0)
        def _():
            acc[...] = jnp.zeros_like(acc)

        e = y_ref[...] - t_ref[...]
        dy_ref[...] = e * (1.0 / d)
        acc[...] += jnp.sum(e * e, axis=0, keepdims=True)

        @pl.when(i == nt - 1)
        def _():
            loss_ref[...] = jnp.sum(acc[...], axis=1, keepdims=True) * (0.5 / d)

    row = pl.BlockSpec((tm, d), lambda i: (i, 0))
    return pl.pallas_call(
        body, name=name, grid=(nt,), in_specs=[row, row],
        out_specs=[row, pl.BlockSpec((1, 1), lambda i: (0, 0))],
        out_shape=[jax.ShapeDtypeStruct((s_dim, d), F32), jax.ShapeDtypeStruct((1, 1), F32)],
        scratch_shapes=[pltpu.VMEM((1, d), F32)],
        compiler_params=_cparams(("arbitrary",), 8 * _nbytes((tm, d), F32)),
    )(y, target)


def _shift_down(v, sh, row):
    if sh == 0:
        return v
    return jnp.where(row >= sh, pltpu.roll(v, sh, 0), 0.0)


def _shift_up(v, sh, row):
    if sh == 0:
        return v
    n = v.shape[0]
    return jnp.where(row < n - sh, pltpu.roll(v, n - sh, 0), 0.0)


def _conv_fwd(gx, pconv, width, name):
    s_dim, cp2 = gx.shape
    cp = cp2 // 2
    nc = cp // LANES

    def body(x_ref, p_ref, o_ref):
        x = x_ref[...]
        row = lax.broadcasted_iota(jnp.int32, x.shape, 0)
        y = jnp.zeros_like(x) + p_ref[pl.ds(width, 1), :]
        for k in range(width):
            y = y + p_ref[pl.ds(k, 1), :] * _shift_down(x, width - 1 - k, row)
        o_ref[...] = y

    return pl.pallas_call(
        body, name=name, grid=(nc,),
        in_specs=[pl.BlockSpec((s_dim, LANES), lambda j: (0, nc + j)), pl.BlockSpec((8, LANES), lambda j: (0, j))],
        out_specs=pl.BlockSpec((s_dim, LANES), lambda j: (0, j)),
        out_shape=jax.ShapeDtypeStruct((s_dim, cp), F32),
        compiler_params=_cparams(("parallel",), 10 * _nbytes((s_dim, LANES), F32)),
    )(gx, pconv)


def _conv_bwd(d1, d2, gx, pconv, width, name):
    s_dim, cp = d1.shape
    nc = cp // LANES

    def body(d1_ref, d2_ref, x_ref, p_ref, dx_ref, dp_ref):
        d = d1_ref[...] + d2_ref[...]
        x = x_ref[...]
        row = lax.broadcasted_iota(jnp.int32, x.shape, 0)
        dx = jnp.zeros_like(d)
        dp_ref[...] = jnp.zeros_like(dp_ref)
        for k in range(width):
            sh = width - 1 - k
            dx = dx + p_ref[pl.ds(k, 1), :] * _shift_up(d, sh, row)
            dp_ref[pl.ds(k, 1), :] = jnp.sum(d * _shift_down(x, sh, row), axis=0, keepdims=True)
        dp_ref[pl.ds(width, 1), :] = jnp.sum(d, axis=0, keepdims=True)
        dx_ref[...] = dx.astype(MXU_DTYPE)

    strip = pl.BlockSpec((s_dim, LANES), lambda j: (0, j))
    par = pl.BlockSpec((8, LANES), lambda j: (0, j))
    return pl.pallas_call(
        body, name=name, grid=(nc,),
        in_specs=[strip, strip, pl.BlockSpec((s_dim, LANES), lambda j: (0, nc + j)), par],
        out_specs=[strip, par],
        out_shape=[jax.ShapeDtypeStruct((s_dim, cp), MXU_DTYPE), jax.ShapeDtypeStruct((8, cp), F32)],
        compiler_params=_cparams(("parallel",), 14 * _nbytes((s_dim, LANES), F32)),
    )(d1, d2, gx, pconv)


def _lru_coeffs(ra, ia, p_ref):
    r = _sigmoid(ra + p_ref[pl.ds(0, 1), :])
    i = _sigmoid(ia + p_ref[pl.ds(1, 1), :])
    sp = _softplus(-p_ref[pl.ds(2, 1), :])
    log_a = -LRU_C * r * sp
    a = jnp.exp(log_a)
    mult = jnp.sqrt(-_expm1(2.0 * log_a))
    return r, i, sp, a, mult


def _scan_fwd(gx, rec, gates, pvec, name):
    s_dim, cp = rec.shape
    ts = _tile(s_dim, 256, 8)
    nt = s_dim // ts

    def body(gate_ref, rec_ref, ra_ref, ia_ref, p_ref, h_ref, y_ref, a_s, u_s, carry):
        @pl.when(pl.program_id(0) == 0)
        def _():
            carry[...] = jnp.zeros_like(carry)

        rec_v = rec_ref[...]
        _, i, _, a, mult = _lru_coeffs(ra_ref[...], ia_ref[...], p_ref)
        a_s[...] = a
        u_s[...] = mult * (i * rec_v)

        def step(t, h):
            h = a_s[pl.ds(t, 1), :] * h + u_s[pl.ds(t, 1), :]
            h_ref[pl.ds(t, 1), :] = h
            return h

        carry[pl.ds(0, 1), :] = lax.fori_loop(0, ts, step, carry[pl.ds(0, 1), :], unroll=8)
        y_ref[...] = (_gelu(gate_ref[...]) * h_ref[...]).astype(MXU_DTYPE)

    blk = pl.BlockSpec((ts, cp), lambda t: (t, 0))
    return pl.pallas_call(
        body, name=name, grid=(nt,),
        in_specs=[blk, blk, blk, pl.BlockSpec((ts, cp), lambda t: (t, 1)), pl.BlockSpec((8, cp), lambda t: (0, 0))],
        out_specs=[blk, blk],
        out_shape=[jax.ShapeDtypeStruct((s_dim, cp), F32), jax.ShapeDtypeStruct((s_dim, cp), MXU_DTYPE)],
        scratch_shapes=[pltpu.VMEM((ts, cp), F32), pltpu.VMEM((ts, cp), F32), pltpu.VMEM((8, cp), F32)],
        compiler_params=_cparams(("arbitrary",), 14 * _nbytes((ts, cp), F32)),
    )(gx, rec, gates, gates, pvec)


def _scan_bwd(dy, gx, hrec, rec, gates, pvec, name):
    s_dim, cp = rec.shape
    ts = _tile(s_dim, 128, 8)
    nt = s_dim // ts

    def body(dy_ref, gate_ref, h_ref, hp_ref, rec_ref, ra_ref, ia_ref, p_ref,
             dgate_ref, dra_ref, dia_ref, drec_ref, dp_ref, a_s, d_s, carry):
        t_id = pl.program_id(0)

        @pl.when(t_id == 0)
        def _():
            carry[...] = jnp.zeros_like(carry)
            dp_ref[...] = jnp.zeros_like(dp_ref)

        rec_v = rec_ref[...]
        r, i, sp, a, mult = _lru_coeffs(ra_ref[...], ia_ref[...], p_ref)
        gate = gate_ref[...]
        dyv = dy_ref[...]
        h = h_ref[...]
        dgate_ref[...] = (dyv * h * _gelu_grad(gate)).astype(MXU_DTYPE)
        a_s[...] = a
        d_s[...] = dyv * _gelu(gate)

        def step(k, c):
            t = ts - 1 - k
            d = d_s[pl.ds(t, 1), :] + c
            d_s[pl.ds(t, 1), :] = d
            return a_s[pl.ds(t, 1), :] * d

        carry[pl.ds(0, 1), :] = lax.fori_loop(0, ts, step, carry[pl.ds(0, 1), :], unroll=8)
        dh = d_s[...]
        row = lax.broadcasted_iota(jnp.int32, h.shape, 0)
        first = jnp.where(t_id == nt - 1, 0.0, 1.0) * hp_ref[pl.ds(7, 1), :]
        h_prev = jnp.where(row == 0, first, pltpu.roll(h, 1, 0))
        dix = dh * mult
        dla = dh * h_prev * a - dh * (i * rec_v) * (a * a) / mult
        dra = dla * (-LRU_C * sp) * r * (1.0 - r)
        dia = dix * rec_v * i * (1.0 - i)
        dra_ref[...] = dra.astype(MXU_DTYPE)
        dia_ref[...] = dia.astype(MXU_DTYPE)
        drec_ref[...] = dix * i
        dsp = jnp.sum(dla * (-LRU_C * r), axis=0, keepdims=True)
        dp_ref[pl.ds(0, 1), :] += jnp.sum(dra, axis=0, keepdims=True)
        dp_ref[pl.ds(1, 1), :] += jnp.sum(dia, axis=0, keepdims=True)
        dp_ref[pl.ds(2, 1), :] += dsp * (-_sigmoid(-p_ref[pl.ds(2, 1), :]))

    blk = pl.BlockSpec((ts, cp), lambda t: (nt - 1 - t, 0))
    prev = pl.BlockSpec((8, cp), lambda t: (jnp.maximum((nt - 1 - t) * (ts // 8) - 1, 0), 0))
    par = pl.BlockSpec((8, cp), lambda t: (0, 0))
    lo = jax.ShapeDtypeStruct((s_dim, cp), MXU_DTYPE)
    return pl.pallas_call(
        body, name=name, grid=(nt,),
        in_specs=[blk, blk, blk, prev, blk, blk, pl.BlockSpec((ts, cp), lambda t: (nt - 1 - t, 1)), par],
        out_specs=[blk, blk, blk, blk, par],
        out_shape=[lo, lo, lo, jax.ShapeDtypeStruct((s_dim, cp), F32), jax.ShapeDtypeStruct((8, cp), F32)],
        scratch_shapes=[pltpu.VMEM((ts, cp), F32), pltpu.VMEM((ts, cp), F32), pltpu.VMEM((8, cp), F32)],
        compiler_params=_cparams(("arbitrary",), 40 * _nbytes((ts, cp), F32)),
    )(dy, gx, hrec, hrec, rec, gates, gates, pvec)


def _fgate_fwd(fpre, bias, name):
    s_dim, w = fpre.shape
    ts = _tile(s_dim, 512, 8)

    def body(f_ref, b_ref, c_ref, lf_s, carry):
        @pl.when(pl.program_id(0) == 0)
        def _():
            carry[...] = jnp.zeros_like(carry)

        lf_s[...] = -_softplus(-(f_ref[...] + b_ref[pl.ds(0, 1), :]))

        def step(t, c):
            c = c + lf_s[pl.ds(t, 1), :]
            c_ref[pl.ds(t, 1), :] = c
            return c

        carry[pl.ds(0, 1), :] = lax.fori_loop(0, ts, step, carry[pl.ds(0, 1), :], unroll=8)

    blk = pl.BlockSpec((ts, w), lambda t: (t, 0))
    return pl.pallas_call(
        body, name=name, grid=(s_dim // ts,),
        in_specs=[blk, pl.BlockSpec((8, w), lambda t: (0, 0))], out_specs=blk,
        out_shape=jax.ShapeDtypeStruct((s_dim, w), F32),
        scratch_shapes=[pltpu.VMEM((ts, w), F32), pltpu.VMEM((8, w), F32)],
        compiler_params=_cparams(("arbitrary",), 12 * _nbytes((ts, w), F32)),
    )(fpre, bias)


def _fgate_bwd(dc, fpre, bias, name):
    s_dim, w = fpre.shape
    ts = _tile(s_dim, 512, 8)
    nt = s_dim // ts

    def body(dc_ref, f_ref, b_ref, df_ref, db_ref, d_s, carry):
        @pl.when(pl.program_id(0) == 0)
        def _():
            carry[...] = jnp.zeros_like(carry)
            db_ref[...] = jnp.zeros_like(db_ref)

        d_s[...] = dc_ref[...]

        def step(k, c):
            t = ts - 1 - k
            c = c + d_s[pl.ds(t, 1), :]
            d_s[pl.ds(t, 1), :] = c
            return c

        carry[pl.ds(0, 1), :] = lax.fori_loop(0, ts, step, carry[pl.ds(0, 1), :], unroll=8)
        df = d_s[...] * _sigmoid(-(f_ref[...] + b_ref[pl.ds(0, 1), :]))
        df_ref[...] = df
        db_ref[pl.ds(0, 1), :] += jnp.sum(df, axis=0, keepdims=True)

    blk = pl.BlockSpec((ts, w), lambda t: (nt - 1 - t, 0))
    par = pl.BlockSpec((8, w), lambda t: (0, 0))
    return pl.pallas_call(
        body, name=name, grid=(nt,), in_specs=[blk, blk, par], out_specs=[blk, par],
        out_shape=[jax.ShapeDtypeStruct((s_dim, w), F32), jax.ShapeDtypeStruct((8, w), F32)],
        scratch_shapes=[pltpu.VMEM((ts, w), F32), pltpu.VMEM((8, w), F32)],
        compiler_params=_cparams(("arbitrary",), 12 * _nbytes((ts, w), F32)),
    )(dc, fpre, bias)


ATTN_HEADS_PER_STEP = 2


def _augment_qk(q_h, k_h, scale):
    n_h, s_dim, dh = q_h.shape
    assert dh + 2 <= LANES
    one = jnp.ones((n_h, s_dim, 1), MXU_DTYPE)
    zero = jnp.zeros((n_h, s_dim, 1), MXU_DTYPE)
    pad = jnp.zeros((n_h, s_dim, LANES - dh - 2), MXU_DTYPE)
    q_aug = jnp.concatenate([(q_h * scale).astype(MXU_DTYPE), zero, one, pad], axis=-1)
    k_aug = jnp.concatenate([k_h.astype(MXU_DTYPE), one, zero, pad], axis=-1)
    return q_aug, k_aug


def _causal_keep(t):
    return lax.broadcasted_iota(jnp.int32, (t, t), 1) <= lax.broadcasted_iota(jnp.int32, (t, t), 0)


def _flash_fwd(q_aug, k_aug, v, c_col, c_row, name):
    n_h, s_dim, w = q_aug.shape
    dh = v.shape[2]
    t = _tile(s_dim, 512, LANES)
    nb = s_dim // t
    hb = ATTN_HEADS_PER_STEP if n_h % ATTN_HEADS_PER_STEP == 0 else 1

    def body(q_ref, k_ref, v_ref, cq_ref, ck_ref, o_ref, lse_ref, m_s, l_s, acc):
        i, j = pl.program_id(1), pl.program_id(2)

        @pl.when(j == 0)
        def _():
            m_s[...] = jnp.full_like(m_s, -jnp.inf)
            l_s[...] = jnp.zeros_like(l_s)
            acc[...] = jnp.zeros_like(acc)

        def tile(masked):
            for hh in range(hb):
                s = _dot(q_ref[hh], k_ref[hh], "nt") + (cq_ref[hh] - ck_ref[hh])
                if masked:
                    s = jnp.where(_causal_keep(t), s, -jnp.inf)
                m_prev = m_s[hh]
                m_new = jnp.maximum(m_prev, jnp.max(s, axis=-1, keepdims=True))
                alpha = jnp.exp(m_prev - m_new)
                p = jnp.exp(s - m_new)
                l_s[hh] = alpha * l_s[hh] + jnp.sum(p, axis=-1, keepdims=True)
                acc[hh] = alpha * acc[hh] + _dot(p, v_ref[hh], "nn")
                m_s[hh] = m_new

        pl.when(j < i)(functools.partial(tile, False))
        pl.when(j == i)(functools.partial(tile, True))

        @pl.when(j == nb - 1)
        def _():
            for hh in range(hb):
                o_ref[hh] = acc[hh] / l_s[hh]
                lse_ref[hh] = m_s[hh] + jnp.log(l_s[hh])

    q_spec = pl.BlockSpec((hb, t, w), lambda h, i, j: (h, i, 0))
    k_spec = pl.BlockSpec((hb, t, w), lambda h, i, j: (h, jnp.minimum(j, i), 0))
    v_spec = pl.BlockSpec((hb, t, dh), lambda h, i, j: (h, jnp.minimum(j, i), 0))
    o_spec = pl.BlockSpec((hb, t, dh), lambda h, i, j: (h, i, 0))
    col_spec = pl.BlockSpec((hb, t, 1), lambda h, i, j: (h, i, 0))
    row_spec = pl.BlockSpec((hb, 1, t), lambda h, i, j: (h, 0, jnp.minimum(j, i)))
    return pl.pallas_call(
        body, name=name, grid=(n_h // hb, nb, nb),
        in_specs=[q_spec, k_spec, v_spec, col_spec, row_spec], out_specs=[o_spec, col_spec],
        out_shape=[jax.ShapeDtypeStruct((n_h, s_dim, dh), F32), jax.ShapeDtypeStruct((n_h, s_dim, 1), F32)],
        scratch_shapes=[pltpu.VMEM((hb, t, 1), F32), pltpu.VMEM((hb, t, 1), F32), pltpu.VMEM((hb, t, dh), F32)],
        compiler_params=_cparams(("parallel", "parallel", "arbitrary"), 10 * hb * _nbytes((t, t), F32)),
    )(q_aug, k_aug, v, c_col, c_row)


def _flash_bwd(q_aug, k_aug, v, c_col, c_row, o, lse, do, name):
    n_h, s_dim, w = q_aug.shape
    dh = v.shape[2]
    t = _tile(s_dim, 512, LANES)
    nb = s_dim // t
    hb = ATTN_HEADS_PER_STEP if n_h % ATTN_HEADS_PER_STEP == 0 else 1

    def body(q_ref, k_ref, v_ref, cq_ref, ck_ref, o_ref, lse_ref, do_ref, dq_ref, dk_ref, dv_ref, dk_acc, dv_acc):
        j, i = pl.program_id(1), pl.program_id(2)

        @pl.when((j == 0) & (i == 0))
        def _():
            dq_ref[...] = jnp.zeros_like(dq_ref)

        @pl.when(i == 0)
        def _():
            dk_acc[...] = jnp.zeros_like(dk_acc)
            dv_acc[...] = jnp.zeros_like(dv_acc)

        def tile(masked):
            start = pl.multiple_of(i * t, t)
            for hh in range(hb):
                qv, kv, dov = q_ref[hh], k_ref[hh], do_ref[hh]
                s = _dot(qv, kv, "nt") + (cq_ref[hh] - ck_ref[hh])
                if masked:
                    s = jnp.where(_causal_keep(t), s, -jnp.inf)
                p = jnp.exp(s - lse_ref[hh])
                delta = jnp.sum(dov.astype(MXU_DTYPE).astype(F32) * o_ref[hh], axis=-1, keepdims=True)
                ds = p * (_dot(dov, v_ref[hh], "nt") - delta)
                dv_acc[hh] += _dot(p, dov, "tn")
                dk_acc[hh] += _dot(ds, qv, "tn")
                dq_ref[hh, pl.ds(start, t), :] += _dot(ds, kv, "nn")

        pl.when(i > j)(functools.partial(tile, False))
        pl.when(i == j)(functools.partial(tile, True))

        @pl.when(i == nb - 1)
        def _():
            dk_ref[...] = dk_acc[...]
            dv_ref[...] = dv_acc[...]

    def q_side(width):
        return pl.BlockSpec((hb, t, width), lambda h, j, i: (h, jnp.maximum(i, j), 0))

    def k_side(width):
        return pl.BlockSpec((hb, t, width), lambda h, j, i: (h, j, 0))

    wide = jax.ShapeDtypeStruct((n_h, s_dim, w), F32)
    return pl.pallas_call(
        body, name=name, grid=(n_h // hb, nb, nb),
        in_specs=[q_side(w), k_side(w), k_side(dh), q_side(1), pl.BlockSpec((hb, 1, t), lambda h, j, i: (h, 0, j)),
                  q_side(dh), q_side(1), q_side(dh)],
        out_specs=[pl.BlockSpec((hb, s_dim, w), lambda h, j, i: (h, 0, 0)), k_side(w), k_side(dh)],
        out_shape=[wide, wide, jax.ShapeDtypeStruct((n_h, s_dim, dh), F32)],
        scratch_shapes=[pltpu.VMEM((hb, t, w), F32), pltpu.VMEM((hb, t, dh), F32)],
        compiler_params=_cparams(("parallel", "arbitrary", "arbitrary"),
                                 10 * hb * _nbytes((t, t), F32) + 2 * hb * _nbytes((s_dim, w), F32)),
    )(q_aug, k_aug, v, c_col, c_row, o, lse, do)


_HBM = pl.BlockSpec(memory_space=pltpu.HBM)
_MESH_ID = pl.DeviceIdType.MESH


def _all_gather(block, name):
    r, w = block.shape

    def body(x_ref, out_ref, send_sems, recv_sems, local_sem):
        x, y, c = lax.axis_index("x"), lax.axis_index("y"), lax.axis_index("c")
        me, sibling = (x, y, c), (x, y, 1 - c)
        chips = [(1 - x, y), (x, 1 - y), (1 - x, 1 - y)]

        def slot(px, py, pc):
            return out_ref.at[4 * px + 2 * py + pc]

        def copy(k, blk, to, src=None):
            return pltpu.make_async_remote_copy(
                src_ref=slot(*blk) if src is None else src, dst_ref=slot(*blk),
                send_sem=send_sems.at[k], recv_sem=recv_sems.at[k], device_id=to, device_id_type=_MESH_ID)

        mine = pltpu.make_async_copy(x_ref, slot(*me), local_sem)
        mine.start()
        first = [copy(0, me, sibling, src=x_ref)]
        first += [copy(1 + n, me, (*chip, c), src=x_ref) for n, chip in enumerate(chips)]
        for cp in first:
            cp.start()
        passed = [copy(4 + n, (*chip, c), sibling) for n, chip in enumerate(chips)]
        for n, chip in enumerate(chips):
            copy(1 + n, (*chip, c), me).wait_recv()
            passed[n].start()
        copy(0, sibling, me).wait_recv()
        for n, chip in enumerate(chips):
            copy(4 + n, (*chip, 1 - c), me).wait_recv()
        for cp in first + passed:
            cp.wait_send()
        mine.wait()

    return pl.pallas_call(
        body, name=name, out_shape=jax.ShapeDtypeStruct((N_DEV, r, w), block.dtype),
        in_specs=[_HBM], out_specs=_HBM,
        scratch_shapes=[pltpu.SemaphoreType.DMA((7,)), pltpu.SemaphoreType.DMA((7,)), pltpu.SemaphoreType.DMA],
    )(block)


_SEM = pl.BlockSpec(memory_space=pltpu.SEMAPHORE)
_EFFECT = pltpu.SideEffectType.DATAFLOW_SIDE_EFFECTING


def _exchange_start(srcs, personalized, after, name):
    n = len(srcs)
    n_after = len(after)
    lands = [lax.empty((N_DEV,) + s.shape[-2:], s.dtype) for s in srcs]

    def body(*refs):
        src_refs, land_refs = refs[:n], refs[n:2 * n]
        outs = refs[2 * n + n_after:]
        send_sems, recv_sems, token = outs[:n], outs[n:2 * n], outs[-1]
        x, y, c = lax.axis_index("x"), lax.axis_index("y"), lax.axis_index("c")
        mine = 4 * x + 2 * y + c
        for ci in range(n):
            for k in range(1, N_DEV):
                px = 1 - x if k & 4 else x
                py = 1 - y if k & 2 else y
                pc = 1 - c if k & 1 else c
                src = src_refs[ci].at[4 * px + 2 * py + pc] if personalized else src_refs[ci]
                pltpu.make_async_remote_copy(
                    src_ref=src, dst_ref=land_refs[ci].at[mine], send_sem=send_sems[ci], recv_sem=recv_sems[ci],
                    device_id=(px, py, pc), device_id_type=_MESH_ID).start()
        token[...] = jnp.zeros_like(token)

    sem = pltpu.SemaphoreType.DMA(())
    out_shape = ([sem] * (2 * n) + [pltpu.HBM(s.shape, s.dtype) for s in srcs]
                 + [pltpu.HBM(l.shape, l.dtype) for l in lands] + [jax.ShapeDtypeStruct((8, LANES), F32)])
    res = pl.pallas_call(
        body, name=name, out_shape=tuple(out_shape),
        in_specs=[_HBM] * (2 * n) + [_ANY] * n_after,
        out_specs=tuple([_SEM] * (2 * n) + [_HBM] * (2 * n) + [pl.BlockSpec(memory_space=pltpu.VMEM)]),
        input_output_aliases={i: 2 * n + i for i in range(2 * n)},
        compiler_params=pltpu.CompilerParams(has_side_effects=_EFFECT),
    )(*[pltpu.with_memory_space_constraint(s, pltpu.HBM) for s in srcs],
      *[pltpu.with_memory_space_constraint(l, pltpu.HBM) for l in lands], *after)
    handles = [(res[ci], res[n + ci], res[2 * n + ci], res[3 * n + ci]) for ci in range(n)]
    return handles, res[-1]


def _exchange_wait(handle, after, name):
    send_sem, recv_sem, src_thru, land_thru = handle

    def body(src_ref, land_ref, send_ref, recv_ref, after_ref, src_out, land_out):
        seven = land_ref.at[pl.ds(0, N_DEV - 1)]
        copies = pltpu.make_async_remote_copy(
            src_ref=seven, dst_ref=seven, send_sem=send_ref, recv_sem=recv_ref,
            device_id=(lax.axis_index("x"), lax.axis_index("y"), lax.axis_index("c")), device_id_type=_MESH_ID)
        copies.wait_send()
        copies.wait_recv()

    return pl.pallas_call(
        body, name=name,
        out_shape=(pltpu.HBM(src_thru.shape, src_thru.dtype), pltpu.HBM(land_thru.shape, land_thru.dtype)),
        in_specs=(_HBM, _HBM, _SEM, _SEM, _ANY), out_specs=(_HBM, _HBM), input_output_aliases={0: 0, 1: 1},
        compiler_params=pltpu.CompilerParams(has_side_effects=_EFFECT),
    )(src_thru, land_thru, send_sem, recv_sem, after)[1]


def _own_slot(land, own, me):
    slot = lax.broadcasted_iota(jnp.int32, (N_DEV, 1, 1), 0)
    return jnp.where(slot == me, own[None], land)


def _sum_slots(slots, name):
    n, r, w = slots.shape
    tr = _tile(r, 128, WIRE_ROW_ALIGN)

    def body(s_ref, o_ref):
        acc = s_ref[0].astype(F32)
        for d in range(1, n):
            acc = acc + s_ref[d].astype(F32)
        o_ref[...] = acc

    return pl.pallas_call(
        body, name=name, grid=(r // tr,),
        in_specs=[pl.BlockSpec((n, tr, w), lambda i: (0, i, 0))],
        out_specs=pl.BlockSpec((tr, w), lambda i: (i, 0)),
        out_shape=jax.ShapeDtypeStruct((r, w), F32),
        compiler_params=_cparams(("parallel",), 2 * _nbytes((n, tr, w), slots.dtype) + 4 * _nbytes((tr, w), F32)),
    )(slots)


def _adamw(w, g, m, v, name):
    r, c = w.shape
    tr = _tile(r, 512, 8)

    def body(w_ref, g_ref, m_ref, v_ref, d_ref, mo_ref, vo_ref):
        gv = g_ref[...]
        m_new = ADAM_B1 * m_ref[...] + (1.0 - ADAM_B1) * gv
        v_new = ADAM_B2 * v_ref[...] + (1.0 - ADAM_B2) * (gv * gv)
        m_hat = m_new / (1.0 - ADAM_B1 ** ADAM_STEP)
        v_hat = v_new / (1.0 - ADAM_B2 ** ADAM_STEP)
        d_ref[...] = -ADAM_LR * (m_hat / (jnp.sqrt(v_hat) + ADAM_EPS) + ADAM_WD * w_ref[...])
        mo_ref[...] = m_new
        vo_ref[...] = v_new

    blk = pl.BlockSpec((tr, c), lambda i: (i, 0))
    shp = jax.ShapeDtypeStruct((r, c), F32)
    return pl.pallas_call(
        body, name=name, grid=(r // tr,), in_specs=[blk] * 4, out_specs=[blk] * 3, out_shape=[shp] * 3,
        compiler_params=_cparams(("parallel",), 16 * _nbytes((tr, _round_up(c, LANES)), F32)),
    )(w, g, m, v)


def _pack_rows(parts, width, dtype, row_align):
    rows, spans, off = [], [], 0
    for p in parts:
        flat = p.reshape(-1).astype(dtype)
        n_rows = _round_up(-(-flat.shape[0] // width), row_align)
        flat = jnp.pad(flat, (0, n_rows * width - flat.shape[0]))
        rows.append(flat.reshape(n_rows, width))
        spans.append((off, n_rows))
        off += n_rows
    return jnp.concatenate(rows, axis=0), spans


def _unpack_rows(mat, span, shape):
    off, n_rows = span
    n = 1
    for s in shape:
        n *= s
    return mat[..., off:off + n_rows, :].reshape(mat.shape[:-2] + (-1,))[..., :n].reshape(mat.shape[:-2] + tuple(shape))


def _block_diag(w, size):
    n, b, _ = w.shape
    eye = jnp.eye(n, dtype=w.dtype)
    dense = (w[:, :, None, :] * eye[:, None, :, None]).reshape(n * b, n * b)
    return jnp.pad(dense, ((0, size - n * b), (0, size - n * b)))


def _diag_blocks(dense, n, b):
    return jnp.stack([dense[k * b:(k + 1) * b, k * b:(k + 1) * b] for k in range(n)])


def _pad_rows(a, rows):
    return jnp.pad(a, ((0, rows - a.shape[0]), (0, 0)))


def _pad_cols(a, cols):
    return jnp.pad(a, ((0, 0), (0, cols - a.shape[1])))


def _train_step(a):
    x = a["x"][0]
    target = a["loss_target"][0]
    s_dim, d = x.shape
    n_layers = a["ffn1_pre_g"].shape[0]
    f_shard = a["ffn1_w_gate"].shape[2]
    c_shard = a["rg_conv_b"].shape[1]
    c_dim = c_shard * N_DEV
    cp = _round_up(c_dim, LANES)
    conv_width = a["rg_conv_w"].shape[1]
    n_blocks, lru_block = a["rg_w_a"].shape[1], a["rg_w_a"].shape[2]
    d_attn = a["attn_w_q"].shape[2]
    n_heads = a["b_fgate"].shape[0]
    d_head = d_attn // n_heads
    attn_scale = d_head ** -0.5
    assert conv_width < 8 and n_heads <= LANES and n_layers == 2
    assert d_attn == d
    me = 4 * lax.axis_index("x") + 2 * lax.axis_index("y") + lax.axis_index("c")

    shard = {"rg_w_in": a["rg_w_in"][0].T, "rg_w_out": a["rg_w_out"][0], "w_kv": a["w_kv"].T,
             "attn_w_q": a["attn_w_q"][0], "attn_w_o": a["attn_w_o"][0]}
    for l in range(n_layers):
        for f in ("ffn1", "ffn2"):
            shard[(f, "gate", l)] = a[f + "_w_gate"][l].T
            shard[(f, "up", l)] = a[f + "_w_up"][l].T
            shard[(f, "down", l)] = a[f + "_w_down"][l]

    def ffn_names(f, l):
        return [(f, "gate", l), (f, "up", l), (f, "down", l)]

    def chunk_layout(names):
        spans, off = [], 0
        for nm in names:
            spans.append((nm, off, shard[nm].shape[0]))
            off += _round_up(shard[nm].shape[0], WIRE_ROW_ALIGN)
        return spans, off

    def pack_chunk(names, parts):
        return jnp.concatenate(
            [_pad_rows(parts[nm].astype(WIRE_DTYPE), _round_up(parts[nm].shape[0], WIRE_ROW_ALIGN)) for nm in names], axis=0)

    full = {}

    def unpack_chunk(names, gathered):
        for nm, o, n_rows in chunk_layout(names)[0]:
            full[nm] = gathered[:, o:o + n_rows, :].reshape(N_DEV * n_rows, d)

    fwd_chunks = [ffn_names("ffn1", 0), ["rg_w_in", "rg_w_out"], ffn_names("ffn2", 0) + ["w_kv"],
                  ffn_names("ffn1", 1) + ["attn_w_q", "attn_w_o"], ffn_names("ffn2", 1)]
    fwd_packs = [pack_chunk(names, shard) for names in fwd_chunks]
    unpack_chunk(fwd_chunks[0], _all_gather(fwd_packs[0], "gather_weights_first"))

    small_parts = [a["rg_conv_w"][0], a["rg_conv_b"][0], a["rg_b_a"][0], a["rg_b_x"][0], a["rg_lambda"][0], a["w_fgate"]]
    small_pack, small_spans = _pack_rows(small_parts, d, F32, 8)
    small_all = _all_gather(small_pack, "gather_small")
    fwd_handles, fwd_token = _exchange_start(fwd_packs[1:], False, [full[("ffn1", "down", 0)], small_all],
                                             "gather_weights_start")

    def land_weights(n, after):
        land = _exchange_wait(fwd_handles[n - 1], after, f"gather_weights_wait_{n}")
        unpack_chunk(fwd_chunks[n], _own_slot(land, fwd_packs[n], me))

    sm = [_unpack_rows(small_all, sp, p.shape) for sp, p in zip(small_spans, small_parts)]
    conv_w = jnp.moveaxis(sm[0], 0, 1).reshape(conv_width, c_dim)
    conv_b, b_a, b_x, lam = (v.reshape(1, c_dim) for v in sm[1:5])
    w_f = sm[5].reshape(d, n_heads)

    pconv = _pad_rows(_pad_cols(jnp.concatenate([conv_w, conv_b], axis=0), cp), 8)
    pvec = _pad_rows(_pad_cols(jnp.concatenate([b_a, b_x, lam], axis=0), cp), 8)
    wa_dense = _block_diag(a["rg_w_a"][0], cp).astype(MXU_DTYPE)
    wx_dense = _block_diag(a["rg_w_x"][0], cp).astype(MXU_DTYPE)
    wax = jnp.concatenate([wa_dense, wx_dense], axis=1)
    w_f_t = _pad_rows(w_f.T.astype(MXU_DTYPE), LANES)
    b_f = _pad_rows(_pad_cols(a["b_fgate"].reshape(1, n_heads), LANES), 8)

    def gain(name, l):
        return a[name][l].reshape(1, d)

    def ffn_fwd(h, f, l, after=None):
        xn = _rms_fwd(h, gain(f + "_pre_g", l), f"{f}_{l}_pre_norm", after)
        g, u, act = _ffn_up(xn, full[(f, "gate", l)], full[(f, "up", l)], f"{f}_{l}_up")
        fo, h_new = _mm_rms_res(act, full[(f, "down", l)], h, gain(f + "_post_g", l), 0.5, f"{f}_{l}_down")
        return h_new, (h, xn, g, u, act, fo)

    def heads(t2):
        return t2.reshape(s_dim, n_heads, d_head).transpose(1, 0, 2)

    def unheads(t3):
        return t3.transpose(1, 0, 2).reshape(s_dim, n_heads * d_head)

    h0 = x
    h0a, sv_f1_0 = ffn_fwd(h0, "ffn1", 0, fwd_token)
    land_weights(1, h0a)
    w_in_gate = _pad_rows(full["rg_w_in"][:c_dim], cp)
    w_in_rec = _pad_rows(full["rg_w_in"][c_dim:], cp)
    w_in_t = jnp.concatenate([w_in_gate, w_in_rec], axis=0)
    w_out = _pad_rows(full["rg_w_out"], cp)
    hn_rg = _rms_fwd(h0a, gain("mix_pre_g", 0), "rg_pre_norm")
    gx = _mm([(hn_rg, w_in_t)], "nt", F32, "rg_in_proj")
    rec = _conv_fwd(gx, pconv, conv_width, "rg_conv")
    gates = _mm([(rec, wax)], "nn", F32, "rg_gate_proj")
    h_rec, y_rg = _scan_fwd(gx, rec, gates, pvec, "rg_scan")
    m_rg, h0b = _mm_rms_res(y_rg, w_out, h0a, gain("mix_post_g", 0), 1.0, "rg_out_proj")
    land_weights(2, h0b)
    h1, sv_f2_0 = ffn_fwd(h0b, "ffn2", 0)
    hn_kv = _rms_fwd(h1, a["kv_norm_g"].reshape(1, d), "kv_norm")
    kv = _mm([(hn_kv, full["w_kv"])], "nt", MXU_DTYPE, "kv_proj")
    fpre = _mm([(hn_kv, w_f_t)], "nt", F32, "fgate_proj")
    c_cum = _fgate_fwd(fpre, b_f, "fgate_cumsum")
    k_h, v_h = heads(kv[:, :d_attn]), heads(kv[:, d_attn:])
    c_heads = c_cum[:, :n_heads].T
    c_col, c_row = c_heads[:, :, None], c_heads[:, None, :]
    land_weights(3, c_cum)
    h1a, sv_f1_1 = ffn_fwd(h1, "ffn1", 1)
    hn_at = _rms_fwd(h1a, gain("mix_pre_g", 1), "attn_pre_norm")
    q2 = _mm([(hn_at, full["attn_w_q"])], "nn", F32, "q_proj")
    q_aug, k_aug = _augment_qk(heads(q2), k_h, attn_scale)
    o_h, lse = _flash_fwd(q_aug, k_aug, v_h, c_col, c_row, "attn_fwd")
    o2 = unheads(o_h)
    m_at, h1b = _mm_rms_res(o2, full["attn_w_o"], h1a, gain("mix_post_g", 1), 1.0, "attn_out_proj")
    land_weights(4, h1b)
    y, sv_f2_1 = ffn_fwd(h1b, "ffn2", 1)
    dy, loss_part = _loss_head(y, target, "loss_head")

    grads_big = {}
    grads_rep = {}

    bwd_chunks = [ffn_names("ffn2", 1), ["attn_w_q", "attn_w_o"] + ffn_names("ffn1", 1),
                  ["w_kv"] + ffn_names("ffn2", 0), ["rg_w_in", "rg_w_out"], ffn_names("ffn1", 0)]
    bwd_sends, bwd_handles = [], []

    def send_grads(after):
        n = len(bwd_sends)
        send = jnp.concatenate(
            [jnp.pad(grads_big[nm].reshape(N_DEV, n_rows, d), ((0, 0), (0, _round_up(n_rows, WIRE_ROW_ALIGN) - n_rows), (0, 0)))
             for nm, _, n_rows in chunk_layout(bwd_chunks[n])[0]], axis=1)
        handles, token = _exchange_start([send], True, [after], f"exchange_grads_start_{n}")
        bwd_sends.append(send)
        bwd_handles.append(handles[0])
        return token

    def ffn_bwd(dh_out, saved, f, l, after=None, send_now=False):
        h, xn, g, u, act, fo = saved
        df, d_post = _rms_bwd(fo, gain(f + "_post_g", l), [dh_out], None, 0.5, MXU_DTYPE, f"{f}_{l}_post_norm_bwd", after)
        dg, du = _ffn_act_bwd(df, full[(f, "down", l)], g, u, f"{f}_{l}_act_bwd")
        grads_big[(f, "down", l)] = _mm([(act, df)], "tn", WIRE_DTYPE, f"{f}_{l}_dw_down")
        grads_big[(f, "gate", l)] = _mm([(dg, xn)], "tn", WIRE_DTYPE, f"{f}_{l}_dw_gate")
        grads_big[(f, "up", l)] = _mm([(du, xn)], "tn", WIRE_DTYPE, f"{f}_{l}_dw_up")
        sent = send_grads(df) if send_now else None
        dxn = _mm([(dg, full[(f, "gate", l)]), (du, full[(f, "up", l)])], "nn", F32, f"{f}_{l}_dx", sent)
        dh_in, d_pre = _rms_bwd(h, gain(f + "_pre_g", l), [dxn], dh_out, 1.0, F32, f"{f}_{l}_pre_norm_bwd")
        grads_rep[(f + "_post_g", l)] = d_post
        grads_rep[(f + "_pre_g", l)] = d_pre
        return dh_in

    dh = ffn_bwd(dy, sv_f2_1, "ffn2", 1)
    token = send_grads(dh)
    dm, d_post = _rms_bwd(m_at, gain("mix_post_g", 1), [dh], None, 1.0, MXU_DTYPE, "attn_post_norm_bwd", token)
    grads_rep[("mix_post_g", 1)] = d_post
    do2 = _mm([(dm, full["attn_w_o"])], "nt", F32, "attn_out_proj_dx")
    grads_big["attn_w_o"] = _mm([(o2, dm)], "tn", WIRE_DTYPE, "attn_out_proj_dw")
    dq_aug, dk_aug, dv_h = _flash_bwd(q_aug, k_aug, v_h, c_col, c_row, o_h, lse, heads(do2), "attn_bwd")
    dk_h = dk_aug[:, :, :d_head]
    dc_heads = dq_aug[:, :, d_head] - dk_aug[:, :, d_head + 1]
    dq2 = unheads(dq_aug[:, :, :d_head] * attn_scale)
    dhn = _mm([(dq2, full["attn_w_q"])], "nt", F32, "q_proj_dx")
    grads_big["attn_w_q"] = _mm([(hn_at, dq2)], "tn", WIRE_DTYPE, "q_proj_dw")
    dh, d_pre = _rms_bwd(h1a, gain("mix_pre_g", 1), [dhn], dh, 1.0, F32, "attn_pre_norm_bwd")
    grads_rep[("mix_pre_g", 1)] = d_pre
    dh = ffn_bwd(dh, sv_f1_1, "ffn1", 1)
    token = send_grads(dh)
    dkv = jnp.concatenate([unheads(dk_h), unheads(dv_h)], axis=1)
    dc_cum = _pad_cols(dc_heads.T, LANES)
    dfpre, db_f = _fgate_bwd(dc_cum, fpre, b_f, "fgate_cumsum_bwd")
    dhn_kv = _mm([(dkv, full["w_kv"])], "nn", F32, "kv_proj_dx")
    dhn_f = _mm([(dfpre, w_f_t)], "nn", F32, "fgate_proj_dx")
    grads_big["w_kv"] = _mm([(dkv, hn_kv)], "tn", WIRE_DTYPE, "kv_proj_dw")
    dw_f_t = _mm([(dfpre, hn_kv)], "tn", F32, "fgate_proj_dw")
    dh, d_kvg = _rms_bwd(h1, a["kv_norm_g"].reshape(1, d), [dhn_kv, dhn_f], dh, 1.0, F32, "kv_norm_bwd", token)
    dh = ffn_bwd(dh, sv_f2_0, "ffn2", 0)
    token = send_grads(dh)
    dm, d_post = _rms_bwd(m_rg, gain("mix_post_g", 0), [dh], None, 1.0, MXU_DTYPE, "rg_post_norm_bwd", token)
    grads_rep[("mix_post_g", 0)] = d_post
    dy_rg = _mm([(dm, w_out)], "nt", F32, "rg_out_proj_dx")
    dw_out = _mm([(y_rg, dm)], "tn", WIRE_DTYPE, "rg_out_proj_dw")
    dgate, dra, dia, drec1, dpvec = _scan_bwd(dy_rg, gx, h_rec, rec, gates, pvec, "rg_scan_bwd")
    drec2 = _mm([(dra, wa_dense), (dia, wx_dense)], "nt", F32, "rg_gate_proj_dx")
    dwa_dense = _mm([(rec, dra)], "tn", F32, "rg_gate_proj_dwa")
    dwx_dense = _mm([(rec, dia)], "tn", F32, "rg_gate_proj_dwx")
    drec0, dpconv = _conv_bwd(drec1, drec2, gx, pconv, conv_width, "rg_conv_bwd")
    dhn = _mm([(dgate, w_in_gate), (drec0, w_in_rec)], "nn", F32, "rg_in_proj_dx")
    dw_in_gate = _mm([(dgate, hn_rg)], "tn", WIRE_DTYPE, "rg_in_proj_dw_gate")
    dw_in_rec = _mm([(drec0, hn_rg)], "tn", WIRE_DTYPE, "rg_in_proj_dw_rec")
    dh, d_pre = _rms_bwd(h0a, gain("mix_pre_g", 0), [dhn], dh, 1.0, F32, "rg_pre_norm_bwd")
    grads_rep[("mix_pre_g", 0)] = d_pre
    grads_big["rg_w_in"] = jnp.concatenate([dw_in_gate[:c_dim], dw_in_rec[:c_dim]], axis=0)
    grads_big["rg_w_out"] = dw_out[:c_dim]
    token = send_grads(dh)
    grad_x = ffn_bwd(dh, sv_f1_0, "ffn1", 0, token, send_now=True)

    g_shard = {}

    def land_grads(n, after):
        land = _exchange_wait(bwd_handles[n], after, f"exchange_grads_wait_{n}")
        own = lax.dynamic_index_in_dim(bwd_sends[n], me, axis=0, keepdims=False)
        g_chunk = _sum_slots(_own_slot(land, own, me), f"sum_weight_grads_{n}")
        for nm, o, n_rows in chunk_layout(bwd_chunks[n])[0]:
            g_shard[nm] = g_chunk[o:o + n_rows]

    for n in range(len(bwd_chunks) - 1):
        land_grads(n, grad_x)

    def gain_grad(name):
        return jnp.concatenate([grads_rep[(name, l)] for l in range(n_layers)], axis=0)

    rep_names = ["ffn1_pre_g", "ffn1_post_g", "mix_pre_g", "mix_post_g", "ffn2_pre_g", "ffn2_post_g"]
    rep_parts = [gain_grad(nm) for nm in rep_names]
    rep_names += ["kv_norm_g", "b_fgate", "rg_w_a", "rg_w_x", "rg_conv_w", "rg_conv_b", "rg_b_a", "rg_b_x", "rg_lambda", "w_fgate"]
    rep_parts += [
        d_kvg, db_f[0, :n_heads],
        _diag_blocks(dwa_dense, n_blocks, lru_block), _diag_blocks(dwx_dense, n_blocks, lru_block),
        dpconv[:conv_width, :c_dim], dpconv[conv_width, :c_dim],
        dpvec[0, :c_dim], dpvec[1, :c_dim], dpvec[2, :c_dim],
        dw_f_t[:n_heads].T]
    rep_pack, rep_spans = _pack_rows(rep_parts, d, F32, WIRE_ROW_ALIGN)
    rep_sum = _sum_slots(_all_gather(rep_pack, "gather_small_grads"), "sum_small_grads")
    g_rep = {nm: _unpack_rows(rep_sum, sp, p.shape) for nm, sp, p in zip(rep_names, rep_spans, rep_parts)}

    def my_cols(full_grad, n):
        return lax.dynamic_slice_in_dim(full_grad, me * n, n, axis=full_grad.ndim - 1)

    def ffn_grads(f):
        grad[f + "_w_gate"] = jnp.stack([g_shard[(f, "gate", l)].T for l in range(n_layers)])
        grad[f + "_w_up"] = jnp.stack([g_shard[(f, "up", l)].T for l in range(n_layers)])
        grad[f + "_w_down"] = jnp.stack([g_shard[(f, "down", l)] for l in range(n_layers)])

    grad = {}
    for nm in ("ffn1_pre_g", "ffn1_post_g", "mix_pre_g", "mix_post_g", "ffn2_pre_g", "ffn2_post_g"):
        grad[nm] = g_rep[nm]
    ffn_grads("ffn2")
    grad["rg_w_in"] = g_shard["rg_w_in"].T[None]
    grad["rg_conv_w"] = my_cols(g_rep["rg_conv_w"], c_shard)[None]
    for nm in ("rg_conv_b", "rg_b_a", "rg_b_x", "rg_lambda"):
        grad[nm] = my_cols(g_rep[nm], c_shard)[None]
    grad["rg_w_a"] = g_rep["rg_w_a"][None]
    grad["rg_w_x"] = g_rep["rg_w_x"][None]
    grad["rg_w_out"] = g_shard["rg_w_out"][None]
    grad["kv_norm_g"] = g_rep["kv_norm_g"].reshape(d)
    grad["w_kv"] = g_shard["w_kv"].T
    grad["w_fgate"] = lax.dynamic_slice_in_dim(g_rep["w_fgate"], me * (d // N_DEV), d // N_DEV, axis=0)
    grad["b_fgate"] = g_rep["b_fgate"]
    grad["attn_w_q"] = g_shard["attn_w_q"][None]
    grad["attn_w_o"] = g_shard["attn_w_o"][None]

    delta, new_m, new_v = {}, {}, {}

    def adamw(nm):
        w = a[nm]
        shape = w.shape
        two_d = (1, shape[0]) if w.ndim == 1 else (-1, shape[-1])
        dl, mo, vo = _adamw(w.reshape(two_d), grad[nm].reshape(two_d), a["m_" + nm].reshape(two_d),
                            a["v_" + nm].reshape(two_d), "adamw_" + nm)
        delta[nm], new_m[nm], new_v[nm] = dl.reshape(shape), mo.reshape(shape), vo.reshape(shape)
        grad[nm] = grad[nm].reshape(shape)

    last_names = ("ffn1_w_gate", "ffn1_w_up", "ffn1_w_down")
    for nm in WEIGHT_NAMES:
        if nm not in last_names:
            adamw(nm)
    land_grads(len(bwd_chunks) - 1, delta["attn_w_o"])
    ffn_grads("ffn1")
    for nm in last_names:
        adamw(nm)

    loss = lax.psum(loss_part[0, 0], AXES)
    return (loss, grad_x[None], *[grad[n] for n in WEIGHT_NAMES], *[delta[n] for n in WEIGHT_NAMES],
            *[new_m[n] for n in WEIGHT_NAMES], *[new_v[n] for n in WEIGHT_NAMES])


def kernel(x, ffn1_pre_g, ffn1_w_gate, ffn1_w_up, ffn1_w_down, ffn1_post_g, mix_pre_g, mix_post_g, ffn2_pre_g, ffn2_w_gate, ffn2_w_up, ffn2_w_down, ffn2_post_g, rg_w_in, rg_conv_w, rg_conv_b, rg_w_a, rg_b_a, rg_w_x, rg_b_x, rg_lambda, rg_w_out, kv_norm_g, w_kv, w_fgate, b_fgate, attn_w_q, attn_w_o, loss_target, m_ffn1_pre_g, m_ffn1_w_gate, m_ffn1_w_up, m_ffn1_w_down, m_ffn1_post_g, m_mix_pre_g, m_mix_post_g, m_ffn2_pre_g, m_ffn2_w_gate, m_ffn2_w_up, m_ffn2_w_down, m_ffn2_post_g, m_rg_w_in, m_rg_conv_w, m_rg_conv_b, m_rg_w_a, m_rg_b_a, m_rg_w_x, m_rg_b_x, m_rg_lambda, m_rg_w_out, m_kv_norm_g, m_w_kv, m_w_fgate, m_b_fgate, m_attn_w_q, m_attn_w_o, v_ffn1_pre_g, v_ffn1_w_gate, v_ffn1_w_up, v_ffn1_w_down, v_ffn1_post_g, v_mix_pre_g, v_mix_post_g, v_ffn2_pre_g, v_ffn2_w_gate, v_ffn2_w_up, v_ffn2_w_down, v_ffn2_post_g, v_rg_w_in, v_rg_conv_w, v_rg_conv_b, v_rg_w_a, v_rg_b_a, v_rg_w_x, v_rg_b_x, v_rg_lambda, v_rg_w_out, v_kv_norm_g, v_w_kv, v_w_fgate, v_b_fgate, v_attn_w_q, v_attn_w_o):
    return _train_step(dict(locals()))
```

```python
import functools

import jax
import jax.numpy as jnp
from jax import lax
from jax.experimental import pallas as pl
from jax.experimental.pallas import tpu as pltpu

F32 = jnp.float32
MXU_DTYPE = jnp.bfloat16
WIRE_DTYPE = jnp.bfloat16
N_DEV = 8
AXES = ("x", "y", "c")
LANES = 128
WIRE_ROW_ALIGN = 16
VMEM_LIMIT_MIN = 32 * 2 ** 20
VMEM_LIMIT_MAX = 56 * 2 ** 20

RMS_EPS = 1e-6
LRU_C = 8.0
ADAM_LR, ADAM_B1, ADAM_B2, ADAM_EPS, ADAM_WD, ADAM_STEP = 0.001, 0.9, 0.999, 1e-08, 0.01, 10

WEIGHT_NAMES = (
    "ffn1_pre_g", "ffn1_w_gate", "ffn1_w_up", "ffn1_w_down", "ffn1_post_g", "mix_pre_g", "mix_post_g",
    "ffn2_pre_g", "ffn2_w_gate", "ffn2_w_up", "ffn2_w_down", "ffn2_post_g", "rg_w_in", "rg_conv_w",
    "rg_conv_b", "rg_w_a", "rg_b_a", "rg_w_x", "rg_b_x", "rg_lambda", "rg_w_out", "kv_norm_g", "w_kv",
    "w_fgate", "b_fgate", "attn_w_q", "attn_w_o")


def _round_up(n, m):
    return (n + m - 1) // m * m


def _tile(dim, target, align=LANES):
    if dim <= target:
        return dim
    best = None
    t = align
    while t <= target:
        if dim % t == 0:
            best = t
        t += align
    return dim if best is None else best


def _cparams(semantics, vmem_estimate):
    limit = min(VMEM_LIMIT_MAX, max(VMEM_LIMIT_MIN, 2 * int(vmem_estimate)))
    return pltpu.CompilerParams(dimension_semantics=semantics, vmem_limit_bytes=limit)


def _nbytes(shape, dtype):
    n = 1
    for s in shape:
        n *= s
    return n * jnp.dtype(dtype).itemsize


def _sigmoid(x):
    return jax.nn.sigmoid(x)


def _softplus(x):
    return jnp.maximum(x, 0.0) + jnp.log1p(jnp.exp(-jnp.abs(x)))


def _expm1(x):
    series = x * (1.0 + x * (0.5 + x * (1.0 / 6.0 + x * (1.0 / 24.0 + x * (1.0 / 120.0)))))
    return jnp.where(jnp.abs(x) < 0.25, series, jnp.exp(x) - 1.0)


_GELU_C = 0.7978845608028654
_GELU_A = 0.044715


def _gelu(x):
    return 0.5 * x * (1.0 + jnp.tanh(_GELU_C * (x + _GELU_A * x * x * x)))


def _gelu_grad(x):
    t = jnp.tanh(_GELU_C * (x + _GELU_A * x * x * x))
    return 0.5 * (1.0 + t) + 0.5 * x * (1.0 - t * t) * _GELU_C * (1.0 + 3.0 * _GELU_A * x * x)


_DOT_DIMS = {"nn": ((1,), (0,)), "nt": ((1,), (1,)), "tn": ((0,), (0,))}


def _dot(a, b, mode):
    return lax.dot_general(a.astype(MXU_DTYPE), b.astype(MXU_DTYPE), (_DOT_DIMS[mode], ((), ())),
                           preferred_element_type=F32)


def _mm(pairs, mode, out_dtype, name, after=None, out_scale=None):
    a0, b0 = pairs[0]
    if mode == "tn":
        k_dim, m_dim = a0.shape
        n_dim = b0.shape[1]
    else:
        m_dim, k_dim = a0.shape
        n_dim = b0.shape[0] if mode == "nt" else b0.shape[1]
    for a, b in pairs:
        assert a.shape == a0.shape and b.shape == b0.shape
    tm = _tile(m_dim, 1408 if mode == "tn" else 512)
    tn = _tile(n_dim, 1408)
    tk = _tile(k_dim, 1408)
    nk = k_dim // tk
    n_pairs = len(pairs)

    if mode == "tn":
        a_spec = pl.BlockSpec((tk, tm), lambda i, j, k: (k, i))
    else:
        a_spec = pl.BlockSpec((tm, tk), lambda i, j, k: (i, k))
    if mode == "nt":
        b_spec = pl.BlockSpec((tn, tk), lambda i, j, k: (j, k))
    else:
        b_spec = pl.BlockSpec((tk, tn), lambda i, j, k: (k, j))

    order = [] if after is None else [after]

    def body(*refs):
        ins, o_ref, acc = refs[:2 * n_pairs], refs[-2], refs[-1]
        k = pl.program_id(2)

        @pl.when(k == 0)
        def _():
            acc[...] = jnp.zeros_like(acc)

        s = acc[...]
        for p in range(n_pairs):
            s = s + _dot(ins[2 * p][...], ins[2 * p + 1][...], mode)
        acc[...] = s

        @pl.when(k == nk - 1)
        def _():
            r = acc[...] if out_scale is None else acc[...] * out_scale
            o_ref[...] = r.astype(out_dtype)

    est = (2 * n_pairs * (_nbytes((tm, tk), a0.dtype) + _nbytes((tk, tn), b0.dtype))
           + 2 * _nbytes((tm, tn), out_dtype) + 2 * _nbytes((tm, tn), F32))
    flat = [t for ab in pairs for t in ab]
    return pl.pallas_call(
        body, name=name, grid=(m_dim // tm, n_dim // tn, nk),
        in_specs=[a_spec, b_spec] * n_pairs + [_ANY] * len(order),
        out_specs=pl.BlockSpec((tm, tn), lambda i, j, k: (i, j)),
        out_shape=jax.ShapeDtypeStruct((m_dim, n_dim), out_dtype),
        scratch_shapes=[pltpu.VMEM((tm, tn), F32)],
        compiler_params=_cparams(("parallel", "parallel", "arbitrary"), est),
    )(*flat, *order)


_ANY = pl.BlockSpec(memory_space=pl.ANY)


def _rms_fwd(x, gain, name, after=None):
    s_dim, d = x.shape
    tm = _tile(s_dim, 512, 8)

    def body(*refs):
        x_ref, g_ref, o_ref = refs[0], refs[1], refs[-1]
        v = x_ref[...]
        r = lax.rsqrt(jnp.mean(v * v, axis=-1, keepdims=True) + RMS_EPS)
        o_ref[...] = (v * r * g_ref[...]).astype(MXU_DTYPE)

    order = [] if after is None else [after]
    return pl.pallas_call(
        body, name=name, grid=(s_dim // tm,),
        in_specs=[pl.BlockSpec((tm, d), lambda i: (i, 0)), pl.BlockSpec((1, d), lambda i: (0, 0))] + [_ANY] * len(order),
        out_specs=pl.BlockSpec((tm, d), lambda i: (i, 0)),
        out_shape=jax.ShapeDtypeStruct((s_dim, d), MXU_DTYPE),
        compiler_params=_cparams(("parallel",), 6 * _nbytes((tm, d), F32)),
    )(x, gain, *order)


def _rms_bwd(x, gain, dys, res, scale, out_dtype, name, after=None):
    s_dim, d = x.shape
    tm = _tile(s_dim, 512, 8)
    n_dy = len(dys)
    has_res = res is not None
    order = [] if after is None else [after]

    def body(*refs):
        x_ref, g_ref = refs[0], refs[1]
        dy_refs = refs[2:2 + n_dy]
        res_ref = refs[2 + n_dy] if has_res else None
        dx_ref, dg_ref = refs[-2], refs[-1]

        @pl.when(pl.program_id(0) == 0)
        def _():
            dg_ref[...] = jnp.zeros_like(dg_ref)

        v = x_ref[...]
        r = lax.rsqrt(jnp.mean(v * v, axis=-1, keepdims=True) + RMS_EPS)
        xh = v * r
        dy = dy_refs[0][...].astype(F32)
        for extra in dy_refs[1:]:
            dy = dy + extra[...].astype(F32)
        gd = dy * g_ref[...]
        dx = scale * r * (gd - xh * jnp.mean(gd * xh, axis=-1, keepdims=True))
        if has_res:
            dx = dx + res_ref[...]
        dx_ref[...] = dx.astype(out_dtype)
        dg_ref[...] += scale * jnp.sum(dy * xh, axis=0, keepdims=True)

    row = pl.BlockSpec((tm, d), lambda i: (i, 0))
    vec = pl.BlockSpec((1, d), lambda i: (0, 0))
    ops = [x, gain] + list(dys) + ([res] if has_res else [])
    return pl.pallas_call(
        body, name=name, grid=(s_dim // tm,),
        in_specs=[row, vec] + [row] * (n_dy + int(has_res)) + [_ANY] * len(order),
        out_specs=[row, vec],
        out_shape=[jax.ShapeDtypeStruct((s_dim, d), out_dtype), jax.ShapeDtypeStruct((1, d), F32)],
        compiler_params=_cparams(("arbitrary",), (2 * len(ops) + 6) * _nbytes((tm, d), F32)),
    )(*ops, *order)


def _mm_rms_res(a, b, h, gain, scale, name):
    s_dim, k_dim = a.shape
    d = b.shape[1]
    tm = _tile(s_dim, 512, 8)
    tk = _tile(k_dim, 1408)
    nk = k_dim // tk

    def body(a_ref, b_ref, h_ref, g_ref, f_ref, o_ref, acc):
        k = pl.program_id(1)

        @pl.when(k == 0)
        def _():
            acc[...] = jnp.zeros_like(acc)

        acc[...] += _dot(a_ref[...], b_ref[...], "nn")

        @pl.when(k == nk - 1)
        def _():
            f = acc[...]
            r = lax.rsqrt(jnp.mean(f * f, axis=-1, keepdims=True) + RMS_EPS)
            f_ref[...] = f
            o_ref[...] = h_ref[...] + scale * (f * r * g_ref[...])

    row = pl.BlockSpec((tm, d), lambda i, k: (i, 0))
    est = (2 * (_nbytes((tm, tk), a.dtype) + _nbytes((tk, d), b.dtype)) + 8 * _nbytes((tm, d), F32))
    return pl.pallas_call(
        body, name=name, grid=(s_dim // tm, nk),
        in_specs=[pl.BlockSpec((tm, tk), lambda i, k: (i, k)), pl.BlockSpec((tk, d), lambda i, k: (k, 0)),
                  row, pl.BlockSpec((1, d), lambda i, k: (0, 0))],
        out_specs=[row, row],
        out_shape=[jax.ShapeDtypeStruct((s_dim, d), F32), jax.ShapeDtypeStruct((s_dim, d), F32)],
        scratch_shapes=[pltpu.VMEM((tm, d), F32)],
        compiler_params=_cparams(("parallel", "arbitrary"), est),
    )(a, b, h, gain)


def _ffn_up(xn, wg_t, wu_t, name):
    s_dim, d = xn.shape
    f_dim = wg_t.shape[0]
    tm = _tile(s_dim, 1024, 8)
    tf = _tile(f_dim, 256)

    def body(x_ref, wg_ref, wu_ref, g_ref, u_ref, a_ref):
        x = x_ref[...]
        g = _dot(x, wg_ref[...], "nt")
        u = _dot(x, wu_ref[...], "nt")
        g_ref[...] = g.astype(MXU_DTYPE)
        u_ref[...] = u.astype(MXU_DTYPE)
        a_ref[...] = (g * _sigmoid(g) * u).astype(MXU_DTYPE)

    w_spec = pl.BlockSpec((tf, d), lambda i, j: (j, 0))
    o_spec = pl.BlockSpec((tm, tf), lambda i, j: (i, j))
    o_shape = jax.ShapeDtypeStruct((s_dim, f_dim), MXU_DTYPE)
    est = 2 * _nbytes((tm, d), xn.dtype) + 4 * _nbytes((tf, d), wg_t.dtype) + 10 * _nbytes((tm, tf), F32)
    return pl.pallas_call(
        body, name=name, grid=(s_dim // tm, f_dim // tf),
        in_specs=[pl.BlockSpec((tm, d), lambda i, j: (i, 0)), w_spec, w_spec],
        out_specs=[o_spec, o_spec, o_spec], out_shape=[o_shape, o_shape, o_shape],
        compiler_params=_cparams(("parallel", "parallel"), est),
    )(xn, wg_t, wu_t)


def _ffn_act_bwd(df, wd, g, u, name):
    s_dim, d = df.shape
    f_dim = wd.shape[0]
    tm = _tile(s_dim, 1024, 8)
    tf = _tile(f_dim, 256)

    def body(df_ref, wd_ref, g_ref, u_ref, dg_ref, du_ref):
        dh = _dot(df_ref[...], wd_ref[...], "nt")
        gv = g_ref[...].astype(F32)
        uv = u_ref[...].astype(F32)
        sg = _sigmoid(gv)
        dg_ref[...] = (dh * uv * (sg * (1.0 + gv * (1.0 - sg)))).astype(MXU_DTYPE)
        du_ref[...] = (dh * gv * sg).astype(MXU_DTYPE)

    t_spec = pl.BlockSpec((tm, tf), lambda i, j: (i, j))
    o_shape = jax.ShapeDtypeStruct((s_dim, f_dim), MXU_DTYPE)
    est = 2 * _nbytes((tm, d), df.dtype) + 2 * _nbytes((tf, d), wd.dtype) + 12 * _nbytes((tm, tf), F32)
    return pl.pallas_call(
        body, name=name, grid=(s_dim // tm, f_dim // tf),
        in_specs=[pl.BlockSpec((tm, d), lambda i, j: (i, 0)), pl.BlockSpec((tf, d), lambda i, j: (j, 0)),
                  t_spec, t_spec],
        out_specs=[t_spec, t_spec], out_shape=[o_shape, o_shape],
        compiler_params=_cparams(("parallel", "parallel"), est),
    )(df, wd, g, u)


def _loss_head(y, target, name):
    s_dim, d = y.shape
    tm = _tile(s_dim, 512, 8)
    nt = s_dim // tm

    def body(y_ref, t_ref, dy_ref, loss_ref, acc):
        i = pl.program_id(0)

        @pl.when(i == 0)
        def _():
            acc[...] = jnp.zeros_like(acc)

        e = y_ref[...] - t_ref[...]
        dy_ref[...] = e * (1.0 / d)
        acc[...] += jnp.sum(e * e, axis=0, keepdims=True)

        @pl.when(i == nt - 1)
        def _():
            loss_ref[...] = jnp.sum(acc[...], axis=1, keepdims=True) * (0.5 / d)

    row = pl.BlockSpec((tm, d), lambda i: (i, 0))
    return pl.pallas_call(
        body, name=name, grid=(nt,), in_specs=[row, row],
        out_specs=[row, pl.BlockSpec((1, 1), lambda i: (0, 0))],
        out_shape=[jax.ShapeDtypeStruct((s_dim, d), F32), jax.ShapeDtypeStruct((1, 1), F32)],
        scratch_shapes=[pltpu.VMEM((1, d), F32)],
        compiler_params=_cparams(("arbitrary",), 8 * _nbytes((tm, d), F32)),
    )(y, target)


def _shift_down(v, sh, row):
    if sh == 0:
        return v
    return jnp.where(row >= sh, pltpu.roll(v, sh, 0), 0.0)


def _shift_up(v, sh, row):
    if sh == 0:
        return v
    n = v.shape[0]
    return jnp.where(row < n - sh, pltpu.roll(v, n - sh, 0), 0.0)


def _conv_fwd(gx, pconv, width, name):
    s_dim, cp2 = gx.shape
    cp = cp2 // 2
    nc = cp // LANES

    def body(x_ref, p_ref, o_ref):
        x = x_ref[...]
        row = lax.broadcasted_iota(jnp.int32, x.shape, 0)
        y = jnp.zeros_like(x) + p_ref[pl.ds(width, 1), :]
        for k in range(width):
            y = y + p_ref[pl.ds(k, 1), :] * _shift_down(x, width - 1 - k, row)
        o_ref[...] = y

    return pl.pallas_call(
        body, name=name, grid=(nc,),
        in_specs=[pl.BlockSpec((s_dim, LANES), lambda j: (0, nc + j)), pl.BlockSpec((8, LANES), lambda j: (0, j))],
        out_specs=pl.BlockSpec((s_dim, LANES), lambda j: (0, j)),
        out_shape=jax.ShapeDtypeStruct((s_dim, cp), F32),
        compiler_params=_cparams(("parallel",), 10 * _nbytes((s_dim, LANES), F32)),
    )(gx, pconv)


def _conv_bwd(d1, d2, gx, pconv, width, name):
    s_dim, cp = d1.shape
    nc = cp // LANES

    def body(d1_ref, d2_ref, x_ref, p_ref, dx_ref, dp_ref):
        d = d1_ref[...] + d2_ref[...]
        x = x_ref[...]
        row = lax.broadcasted_iota(jnp.int32, x.shape, 0)
        dx = jnp.zeros_like(d)
        dp_ref[...] = jnp.zeros_like(dp_ref)
        for k in range(width):
            sh = width - 1 - k
            dx = dx + p_ref[pl.ds(k, 1), :] * _shift_up(d, sh, row)
            dp_ref[pl.ds(k, 1), :] = jnp.sum(d * _shift_down(x, sh, row), axis=0, keepdims=True)
        dp_ref[pl.ds(width, 1), :] = jnp.sum(d, axis=0, keepdims=True)
        dx_ref[...] = dx.astype(MXU_DTYPE)

    strip = pl.BlockSpec((s_dim, LANES), lambda j: (0, j))
    par = pl.BlockSpec((8, LANES), lambda j: (0, j))
    return pl.pallas_call(
        body, name=name, grid=(nc,),
        in_specs=[strip, strip, pl.BlockSpec((s_dim, LANES), lambda j: (0, nc + j)), par],
        out_specs=[strip, par],
        out_shape=[jax.ShapeDtypeStruct((s_dim, cp), MXU_DTYPE), jax.ShapeDtypeStruct((8, cp), F32)],
        compiler_params=_cparams(("parallel",), 14 * _nbytes((s_dim, LANES), F32)),
    )(d1, d2, gx, pconv)


def _lru_coeffs(ra, ia, p_ref):
    r = _sigmoid(ra + p_ref[pl.ds(0, 1), :])
    i = _sigmoid(ia + p_ref[pl.ds(1, 1), :])
    sp = _softplus(-p_ref[pl.ds(2, 1), :])
    log_a = -LRU_C * r * sp
    a = jnp.exp(log_a)
    mult = jnp.sqrt(-_expm1(2.0 * log_a))
    return r, i, sp, a, mult


def _scan_fwd(gx, rec, gates, pvec, name):
    s_dim, cp = rec.shape
    ts = _tile(s_dim, 256, 8)
    nt = s_dim // ts

    def body(gate_ref, rec_ref, ra_ref, ia_ref, p_ref, h_ref, y_ref, a_s, u_s, carry):
        @pl.when(pl.program_id(0) == 0)
        def _():
            carry[...] = jnp.zeros_like(carry)

        rec_v = rec_ref[...]
        _, i, _, a, mult = _lru_coeffs(ra_ref[...], ia_ref[...], p_ref)
        a_s[...] = a
        u_s[...] = mult * (i * rec_v)

        def step(t, h):
            h = a_s[pl.ds(t, 1), :] * h + u_s[pl.ds(t, 1), :]
            h_ref[pl.ds(t, 1), :] = h
            return h

        carry[pl.ds(0, 1), :] = lax.fori_loop(0, ts, step, carry[pl.ds(0, 1), :], unroll=8)
        y_ref[...] = (_gelu(gate_ref[...]) * h_ref[...]).astype(MXU_DTYPE)

    blk = pl.BlockSpec((ts, cp), lambda t: (t, 0))
    return pl.pallas_call(
        body, name=name, grid=(nt,),
        in_specs=[blk, blk, blk, pl.BlockSpec((ts, cp), lambda t: (t, 1)), pl.BlockSpec((8, cp), lambda t: (0, 0))],
        out_specs=[blk, blk],
        out_shape=[jax.ShapeDtypeStruct((s_dim, cp), F32), jax.ShapeDtypeStruct((s_dim, cp), MXU_DTYPE)],
        scratch_shapes=[pltpu.VMEM((ts, cp), F32), pltpu.VMEM((ts, cp), F32), pltpu.VMEM((8, cp), F32)],
        compiler_params=_cparams(("arbitrary",), 14 * _nbytes((ts, cp), F32)),
    )(gx, rec, gates, gates, pvec)


def _scan_bwd(dy, gx, hrec, rec, gates, pvec, name):
    s_dim, cp = rec.shape
    ts = _tile(s_dim, 128, 8)
    nt = s_dim // ts

    def body(dy_ref, gate_ref, h_ref, hp_ref, rec_ref, ra_ref, ia_ref, p_ref,
             dgate_ref, dra_ref, dia_ref, drec_ref, dp_ref, a_s, d_s, carry):
        t_id = pl.program_id(0)

        @pl.when(t_id == 0)
        def _():
            carry[...] = jnp.zeros_like(carry)
            dp_ref[...] = jnp.zeros_like(dp_ref)

        rec_v = rec_ref[...]
        r, i, sp, a, mult = _lru_coeffs(ra_ref[...], ia_ref[...], p_ref)
        gate = gate_ref[...]
        dyv = dy_ref[...]
        h = h_ref[...]
        dgate_ref[...] = (dyv * h * _gelu_grad(gate)).astype(MXU_DTYPE)
        a_s[...] = a
        d_s[...] = dyv * _gelu(gate)

        def step(k, c):
            t = ts - 1 - k
            d = d_s[pl.ds(t, 1), :] + c
            d_s[pl.ds(t, 1), :] = d
            return a_s[pl.ds(t, 1), :] * d

        carry[pl.ds(0, 1), :] = lax.fori_loop(0, ts, step, carry[pl.ds(0, 1), :], unroll=8)
        dh = d_s[...]
        row = lax.broadcasted_iota(jnp.int32, h.shape, 0)
        first = jnp.where(t_id == nt - 1, 0.0, 1.0) * hp_ref[pl.ds(7, 1), :]
        h_prev = jnp.where(row == 0, first, pltpu.roll(h, 1, 0))
        dix = dh * mult
        dla = dh * h_prev * a - dh * (i * rec_v) * (a * a) / mult
        dra = dla * (-LRU_C * sp) * r * (1.0 - r)
        dia = dix * rec_v * i * (1.0 - i)
        dra_ref[...] = dra.astype(MXU_DTYPE)
        dia_ref[...] = dia.astype(MXU_DTYPE)
        drec_ref[...] = dix * i
        dsp = jnp.sum(dla * (-LRU_C * r), axis=0, keepdims=True)
        dp_ref[pl.ds(0, 1), :] += jnp.sum(dra, axis=0, keepdims=True)
        dp_ref[pl.ds(1, 1), :] += jnp.sum(dia, axis=0, keepdims=True)
        dp_ref[pl.ds(2, 1), :] += dsp * (-_sigmoid(-p_ref[pl.ds(2, 1), :]))

    blk = pl.BlockSpec((ts, cp), lambda t: (nt - 1 - t, 0))
    prev = pl.BlockSpec((8, cp), lambda t: (jnp.maximum((nt - 1 - t) * (ts // 8) - 1, 0), 0))
    par = pl.BlockSpec((8, cp), lambda t: (0, 0))
    lo = jax.ShapeDtypeStruct((s_dim, cp), MXU_DTYPE)
    return pl.pallas_call(
        body, name=name, grid=(nt,),
        in_specs=[blk, blk, blk, prev, blk, blk, pl.BlockSpec((ts, cp), lambda t: (nt - 1 - t, 1)), par],
        out_specs=[blk, blk, blk, blk, par],
        out_shape=[lo, lo, lo, jax.ShapeDtypeStruct((s_dim, cp), F32), jax.ShapeDtypeStruct((8, cp), F32)],
        scratch_shapes=[pltpu.VMEM((ts, cp), F32), pltpu.VMEM((ts, cp), F32), pltpu.VMEM((8, cp), F32)],
        compiler_params=_cparams(("arbitrary",), 40 * _nbytes((ts, cp), F32)),
    )(dy, gx, hrec, hrec, rec, gates, gates, pvec)


def _fgate_fwd(fpre, bias, name):
    s_dim, w = fpre.shape
    ts = _tile(s_dim, 512, 8)

    def body(f_ref, b_ref, c_ref, lf_s, carry):
        @pl.when(pl.program_id(0) == 0)
        def _():
            carry[...] = jnp.zeros_like(carry)

        lf_s[...] = -_softplus(-(f_ref[...] + b_ref[pl.ds(0, 1), :]))

        def step(t, c):
            c = c + lf_s[pl.ds(t, 1), :]
            c_ref[pl.ds(t, 1), :] = c
            return c

        carry[pl.ds(0, 1), :] = lax.fori_loop(0, ts, step, carry[pl.ds(0, 1), :], unroll=8)

    blk = pl.BlockSpec((ts, w), lambda t: (t, 0))
    return pl.pallas_call(
        body, name=name, grid=(s_dim // ts,),
        in_specs=[blk, pl.BlockSpec((8, w), lambda t: (0, 0))], out_specs=blk,
        out_shape=jax.ShapeDtypeStruct((s_dim, w), F32),
        scratch_shapes=[pltpu.VMEM((ts, w), F32), pltpu.VMEM((8, w), F32)],
        compiler_params=_cparams(("arbitrary",), 12 * _nbytes((ts, w), F32)),
    )(fpre, bias)


def _fgate_bwd(dc, fpre, bias, name):
    s_dim, w = fpre.shape
    ts = _tile(s_dim, 512, 8)
    nt = s_dim // ts

    def body(dc_ref, f_ref, b_ref, df_ref, db_ref, d_s, carry):
        @pl.when(pl.program_id(0) == 0)
        def _():
            carry[...] = jnp.zeros_like(carry)
            db_ref[...] = jnp.zeros_like(db_ref)

        d_s[...] = dc_ref[...]

        def step(k, c):
            t = ts - 1 - k
            c = c + d_s[pl.ds(t, 1), :]
            d_s[pl.ds(t, 1), :] = c
            return c

        carry[pl.ds(0, 1), :] = lax.fori_loop(0, ts, step, carry[pl.ds(0, 1), :], unroll=8)
        df = d_s[...] * _sigmoid(-(f_ref[...] + b_ref[pl.ds(0, 1), :]))
        df_ref[...] = df
        db_ref[pl.ds(0, 1), :] += jnp.sum(df, axis=0, keepdims=True)

    blk = pl.BlockSpec((ts, w), lambda t: (nt - 1 - t, 0))
    par = pl.BlockSpec((8, w), lambda t: (0, 0))
    return pl.pallas_call(
        body, name=name, grid=(nt,), in_specs=[blk, blk, par], out_specs=[blk, par],
        out_shape=[jax.ShapeDtypeStruct((s_dim, w), F32), jax.ShapeDtypeStruct((8, w), F32)],
        scratch_shapes=[pltpu.VMEM((ts, w), F32), pltpu.VMEM((8, w), F32)],
        compiler_params=_cparams(("arbitrary",), 12 * _nbytes((ts, w), F32)),
    )(dc, fpre, bias)


def _causal_keep(t):
    return lax.broadcasted_iota(jnp.int32, (t, t), 1) <= lax.broadcasted_iota(jnp.int32, (t, t), 0)


def _head_lanes(hh, dh):
    lane = lax.broadcasted_iota(jnp.int32, (1, LANES), 1)
    return (lane >= hh * dh) & (lane < (hh + 1) * dh)


def _pair_attn_fwd(q, kv, c_col, c_row, name):
    s_dim, da = q.shape
    n_h = c_col.shape[0]
    dh = da // n_h
    assert LANES % dh == 0 and da % LANES == 0
    hb = LANES // dh
    n_blocks = da // LANES
    t = _tile(s_dim, 512, LANES)
    nb = s_dim // t

    def body(q_ref, k_ref, v_ref, cq_ref, ck_ref, o_ref, lse_ref, m_s, l_s, acc):
        i, j = pl.program_id(1), pl.program_id(2)

        @pl.when(j == 0)
        def _():
            m_s[...] = jnp.full_like(m_s, -jnp.inf)
            l_s[...] = jnp.zeros_like(l_s)
            acc[...] = jnp.zeros_like(acc)

        def tile(masked):
            qv = q_ref[...]
            for hh in range(hb):
                s = _dot(jnp.where(_head_lanes(hh, dh), qv, jnp.zeros_like(qv)), k_ref[...], "nt")
                s = s + (cq_ref[hh] - ck_ref[hh])
                if masked:
                    s = jnp.where(_causal_keep(t), s, -jnp.inf)
                m_prev = m_s[hh]
                m_new = jnp.maximum(m_prev, jnp.max(s, axis=-1, keepdims=True))
                alpha = jnp.exp(m_prev - m_new)
                p = jnp.exp(s - m_new)
                l_s[hh] = alpha * l_s[hh] + jnp.sum(p, axis=-1, keepdims=True)
                acc[hh] = alpha * acc[hh] + _dot(p, v_ref[...], "nn")
                m_s[hh] = m_new

        pl.when(j < i)(functools.partial(tile, False))
        pl.when(j == i)(functools.partial(tile, True))

        @pl.when(j == nb - 1)
        def _():
            out = jnp.zeros((t, LANES), F32)
            for hh in range(hb):
                out = jnp.where(_head_lanes(hh, dh), acc[hh] / l_s[hh], out)
                lse_ref[hh] = m_s[hh] + jnp.log(l_s[hh])
            o_ref[...] = out

    q_spec = pl.BlockSpec((t, LANES), lambda b, i, j: (i, b))
    k_spec = pl.BlockSpec((t, LANES), lambda b, i, j: (jnp.minimum(j, i), b))
    v_spec = pl.BlockSpec((t, LANES), lambda b, i, j: (jnp.minimum(j, i), n_blocks + b))
    col_spec = pl.BlockSpec((hb, t, 1), lambda b, i, j: (b, i, 0))
    row_spec = pl.BlockSpec((hb, 1, t), lambda b, i, j: (b, 0, jnp.minimum(j, i)))
    return pl.pallas_call(
        body, name=name, grid=(n_blocks, nb, nb),
        in_specs=[q_spec, k_spec, v_spec, col_spec, row_spec], out_specs=[q_spec, col_spec],
        out_shape=[jax.ShapeDtypeStruct((s_dim, da), F32), jax.ShapeDtypeStruct((n_h, s_dim, 1), F32)],
        scratch_shapes=[pltpu.VMEM((hb, t, 1), F32), pltpu.VMEM((hb, t, 1), F32), pltpu.VMEM((hb, t, LANES), F32)],
        compiler_params=_cparams(("parallel", "parallel", "arbitrary"), 10 * hb * _nbytes((t, t), F32)),
    )(q, kv, kv, c_col, c_row)


def _pair_attn_bwd(q, kv, c_col, c_row, o, lse, do, scale, name):
    s_dim, da = q.shape
    n_h = c_col.shape[0]
    dh = da // n_h
    hb = LANES // dh
    n_blocks = da // LANES
    t = _tile(s_dim, 512, LANES)
    nb = s_dim // t

    def body(q_ref, k_ref, v_ref, cq_ref, ck_ref, o_ref, lse_ref, do_ref,
             dq_ref, dcq_ref, dk_ref, dv_ref, dck_ref, dk_acc, dv_acc, dck_acc):
        j, i = pl.program_id(1), pl.program_id(2)

        @pl.when((j == 0) & (i == 0))
        def _():
            dq_ref[...] = jnp.zeros_like(dq_ref)
            dcq_ref[...] = jnp.zeros_like(dcq_ref)

        @pl.when(i == 0)
        def _():
            dk_acc[...] = jnp.zeros_like(dk_acc)
            dv_acc[...] = jnp.zeros_like(dv_acc)
            dck_acc[...] = jnp.zeros_like(dck_acc)

        def tile(masked):
            start = pl.multiple_of(i * t, t)
            qv, kv_, ov = q_ref[...], k_ref[...], o_ref[...]
            dov = do_ref[...].astype(MXU_DTYPE)
            for hh in range(hb):
                lanes = _head_lanes(hh, dh)
                qm = jnp.where(lanes, qv, jnp.zeros_like(qv))
                km = jnp.where(lanes, kv_, jnp.zeros_like(kv_))
                dom = jnp.where(lanes, dov, jnp.zeros_like(dov))
                s = _dot(qm, kv_, "nt") + (cq_ref[hh] - ck_ref[hh])
                if masked:
                    s = jnp.where(_causal_keep(t), s, -jnp.inf)
                p = jnp.exp(s - lse_ref[hh])
                delta = jnp.sum(dom.astype(F32) * ov, axis=-1, keepdims=True)
                ds = p * (_dot(dom, v_ref[...], "nt") - delta)
                dv_acc[...] += _dot(p, dom, "tn")
                dk_acc[...] += _dot(ds, qm, "tn")
                dq_ref[pl.ds(start, t), :] += _dot(ds, km, "nn") * scale
                dcq_ref[hh, pl.ds(start, t), :] += jnp.sum(ds, axis=-1, keepdims=True)
                dck_acc[hh] -= jnp.sum(ds, axis=0, keepdims=True)

        pl.when(i > j)(functools.partial(tile, False))
        pl.when(i == j)(functools.partial(tile, True))

        @pl.when(i == nb - 1)
        def _():
            dk_ref[...] = dk_acc[...]
            dv_ref[...] = dv_acc[...]
            dck_ref[...] = dck_acc[...]

    q_spec = pl.BlockSpec((t, LANES), lambda b, j, i: (jnp.maximum(i, j), b))
    qcol_spec = pl.BlockSpec((hb, t, 1), lambda b, j, i: (b, jnp.maximum(i, j), 0))
    k_spec = pl.BlockSpec((t, LANES), lambda b, j, i: (j, b))
    v_spec = pl.BlockSpec((t, LANES), lambda b, j, i: (j, n_blocks + b))
    krow_spec = pl.BlockSpec((hb, 1, t), lambda b, j, i: (b, 0, j))
    wide = jax.ShapeDtypeStruct((s_dim, da), F32)
    return pl.pallas_call(
        body, name=name, grid=(n_blocks, nb, nb),
        in_specs=[q_spec, k_spec, v_spec, qcol_spec, krow_spec, q_spec, qcol_spec, q_spec],
        out_specs=[pl.BlockSpec((s_dim, LANES), lambda b, j, i: (0, b)),
                   pl.BlockSpec((hb, s_dim, 1), lambda b, j, i: (b, 0, 0)), k_spec, k_spec, krow_spec],
        out_shape=[wide, jax.ShapeDtypeStruct((n_h, s_dim, 1), F32), wide, wide,
                   jax.ShapeDtypeStruct((n_h, 1, s_dim), F32)],
        scratch_shapes=[pltpu.VMEM((t, LANES), F32), pltpu.VMEM((t, LANES), F32), pltpu.VMEM((hb, 1, t), F32)],
        compiler_params=_cparams(("parallel", "arbitrary", "arbitrary"),
                                 10 * hb * _nbytes((t, t), F32) + 2 * (1 + hb) * _nbytes((s_dim, LANES), F32)),
    )(q, kv, kv, c_col, c_row, o, lse, do)


_HBM = pl.BlockSpec(memory_space=pltpu.HBM)
_MESH_ID = pl.DeviceIdType.MESH


def _all_gather(block, name):
    r, w = block.shape

    def body(x_ref, out_ref, send_sems, recv_sems, local_sem):
        x, y, c = lax.axis_index("x"), lax.axis_index("y"), lax.axis_index("c")
        me, sibling = (x, y, c), (x, y, 1 - c)
        chips = [(1 - x, y), (x, 1 - y), (1 - x, 1 - y)]

        def slot(px, py, pc):
            return out_ref.at[4 * px + 2 * py + pc]

        def copy(k, blk, to, src=None):
            return pltpu.make_async_remote_copy(
                src_ref=slot(*blk) if src is None else src, dst_ref=slot(*blk),
                send_sem=send_sems.at[k], recv_sem=recv_sems.at[k], device_id=to, device_id_type=_MESH_ID)

        mine = pltpu.make_async_copy(x_ref, slot(*me), local_sem)
        mine.start()
        first = [copy(0, me, sibling, src=x_ref)]
        first += [copy(1 + n, me, (*chip, c), src=x_ref) for n, chip in enumerate(chips)]
        for cp in first:
            cp.start()
        passed = [copy(4 + n, (*chip, c), sibling) for n, chip in enumerate(chips)]
        for n, chip in enumerate(chips):
            copy(1 + n, (*chip, c), me).wait_recv()
            passed[n].start()
        copy(0, sibling, me).wait_recv()
        for n, chip in enumerate(chips):
            copy(4 + n, (*chip, 1 - c), me).wait_recv()
        for cp in first + passed:
            cp.wait_send()
        mine.wait()

    return pl.pallas_call(
        body, name=name, out_shape=jax.ShapeDtypeStruct((N_DEV, r, w), block.dtype),
        in_specs=[_HBM], out_specs=_HBM,
        scratch_shapes=[pltpu.SemaphoreType.DMA((7,)), pltpu.SemaphoreType.DMA((7,)), pltpu.SemaphoreType.DMA],
    )(block)


_SEM = pl.BlockSpec(memory_space=pltpu.SEMAPHORE)
_EFFECT = pltpu.SideEffectType.DATAFLOW_SIDE_EFFECTING


def _exchange_start(srcs, personalized, after, name):
    n = len(srcs)
    n_after = len(after)
    lands = [lax.empty((N_DEV,) + s.shape[-2:], s.dtype) for s in srcs]

    def body(*refs):
        src_refs, land_refs = refs[:n], refs[n:2 * n]
        outs = refs[2 * n + n_after:]
        send_sems, recv_sems, token = outs[:n], outs[n:2 * n], outs[-1]
        x, y, c = lax.axis_index("x"), lax.axis_index("y"), lax.axis_index("c")
        mine = 4 * x + 2 * y + c
        for ci in range(n):
            for k in range(1, N_DEV):
                px = 1 - x if k & 4 else x
                py = 1 - y if k & 2 else y
                pc = 1 - c if k & 1 else c
                src = src_refs[ci].at[4 * px + 2 * py + pc] if personalized else src_refs[ci]
                pltpu.make_async_remote_copy(
                    src_ref=src, dst_ref=land_refs[ci].at[mine], send_sem=send_sems[ci], recv_sem=recv_sems[ci],
                    device_id=(px, py, pc), device_id_type=_MESH_ID).start()
        token[...] = jnp.zeros_like(token)

    sem = pltpu.SemaphoreType.DMA(())
    out_shape = ([sem] * (2 * n) + [pltpu.HBM(s.shape, s.dtype) for s in srcs]
                 + [pltpu.HBM(l.shape, l.dtype) for l in lands] + [jax.ShapeDtypeStruct((8, LANES), F32)])
    res = pl.pallas_call(
        body, name=name, out_shape=tuple(out_shape),
        in_specs=[_HBM] * (2 * n) + [_ANY] * n_after,
        out_specs=tuple([_SEM] * (2 * n) + [_HBM] * (2 * n) + [pl.BlockSpec(memory_space=pltpu.VMEM)]),
        input_output_aliases={i: 2 * n + i for i in range(2 * n)},
        compiler_params=pltpu.CompilerParams(has_side_effects=_EFFECT),
    )(*[pltpu.with_memory_space_constraint(s, pltpu.HBM) for s in srcs],
      *[pltpu.with_memory_space_constraint(l, pltpu.HBM) for l in lands], *after)
    handles = [(res[ci], res[n + ci], res[2 * n + ci], res[3 * n + ci]) for ci in range(n)]
    return handles, res[-1]


def _exchange_wait(handle, after, name):
    send_sem, recv_sem, src_thru, land_thru = handle

    def body(src_ref, land_ref, send_ref, recv_ref, after_ref, src_out, land_out):
        seven = land_ref.at[pl.ds(0, N_DEV - 1)]
        copies = pltpu.make_async_remote_copy(
            src_ref=seven, dst_ref=seven, send_sem=send_ref, recv_sem=recv_ref,
            device_id=(lax.axis_index("x"), lax.axis_index("y"), lax.axis_index("c")), device_id_type=_MESH_ID)
        copies.wait_send()
        copies.wait_recv()

    return pl.pallas_call(
        body, name=name,
        out_shape=(pltpu.HBM(src_thru.shape, src_thru.dtype), pltpu.HBM(land_thru.shape, land_thru.dtype)),
        in_specs=(_HBM, _HBM, _SEM, _SEM, _ANY), out_specs=(_HBM, _HBM), input_output_aliases={0: 0, 1: 1},
        compiler_params=pltpu.CompilerParams(has_side_effects=_EFFECT),
    )(src_thru, land_thru, send_sem, recv_sem, after)[1]


def _own_slot(land, own, me):
    return lax.dynamic_update_index_in_dim(land, own, me, axis=0)


def _sum_slots(slots, name):
    n, r, w = slots.shape
    tr = _tile(r, 128, WIRE_ROW_ALIGN)

    def body(s_ref, o_ref):
        acc = s_ref[0].astype(F32)
        for d in range(1, n):
            acc = acc + s_ref[d].astype(F32)
        o_ref[...] = acc

    return pl.pallas_call(
        body, name=name, grid=(r // tr,),
        in_specs=[pl.BlockSpec((n, tr, w), lambda i: (0, i, 0))],
        out_specs=pl.BlockSpec((tr, w), lambda i: (i, 0)),
        out_shape=jax.ShapeDtypeStruct((r, w), F32),
        compiler_params=_cparams(("parallel",), 2 * _nbytes((n, tr, w), slots.dtype) + 4 * _nbytes((tr, w), F32)),
    )(slots)


def _adamw(w, g, m, v, name):
    r, c = w.shape
    tr = _tile(r, 512, 8)

    def body(w_ref, g_ref, m_ref, v_ref, d_ref, mo_ref, vo_ref):
        gv = g_ref[...]
        m_new = ADAM_B1 * m_ref[...] + (1.0 - ADAM_B1) * gv
        v_new = ADAM_B2 * v_ref[...] + (1.0 - ADAM_B2) * (gv * gv)
        m_hat = m_new / (1.0 - ADAM_B1 ** ADAM_STEP)
        v_hat = v_new / (1.0 - ADAM_B2 ** ADAM_STEP)
        d_ref[...] = -ADAM_LR * (m_hat / (jnp.sqrt(v_hat) + ADAM_EPS) + ADAM_WD * w_ref[...])
        mo_ref[...] = m_new
        vo_ref[...] = v_new

    blk = pl.BlockSpec((tr, c), lambda i: (i, 0))
    shp = jax.ShapeDtypeStruct((r, c), F32)
    return pl.pallas_call(
        body, name=name, grid=(r // tr,), in_specs=[blk] * 4, out_specs=[blk] * 3, out_shape=[shp] * 3,
        compiler_params=_cparams(("parallel",), 16 * _nbytes((tr, _round_up(c, LANES)), F32)),
    )(w, g, m, v)


def _pack_rows(parts, width, dtype, row_align):
    rows, spans, off = [], [], 0
    for p in parts:
        flat = p.reshape(-1).astype(dtype)
        n_rows = _round_up(-(-flat.shape[0] // width), row_align)
        flat = jnp.pad(flat, (0, n_rows * width - flat.shape[0]))
        rows.append(flat.reshape(n_rows, width))
        spans.append((off, n_rows))
        off += n_rows
    return jnp.concatenate(rows, axis=0), spans


def _unpack_rows(mat, span, shape):
    off, n_rows = span
    n = 1
    for s in shape:
        n *= s
    return mat[..., off:off + n_rows, :].reshape(mat.shape[:-2] + (-1,))[..., :n].reshape(mat.shape[:-2] + tuple(shape))


def _block_diag(w, size):
    n, b, _ = w.shape
    eye = jnp.eye(n, dtype=w.dtype)
    dense = (w[:, :, None, :] * eye[:, None, :, None]).reshape(n * b, n * b)
    return jnp.pad(dense, ((0, size - n * b), (0, size - n * b)))


def _diag_blocks(dense, n, b):
    return jnp.stack([dense[k * b:(k + 1) * b, k * b:(k + 1) * b] for k in range(n)])


def _pad_rows(a, rows):
    return jnp.pad(a, ((0, rows - a.shape[0]), (0, 0)))


def _pad_cols(a, cols):
    return jnp.pad(a, ((0, 0), (0, cols - a.shape[1])))


def _train_step(a):
    x = a["x"][0]
    target = a["loss_target"][0]
    s_dim, d = x.shape
    n_layers = a["ffn1_pre_g"].shape[0]
    f_shard = a["ffn1_w_gate"].shape[2]
    c_shard = a["rg_conv_b"].shape[1]
    c_dim = c_shard * N_DEV
    cp = _round_up(c_dim, LANES)
    conv_width = a["rg_conv_w"].shape[1]
    n_blocks, lru_block = a["rg_w_a"].shape[1], a["rg_w_a"].shape[2]
    d_attn = a["attn_w_q"].shape[2]
    n_heads = a["b_fgate"].shape[0]
    d_head = d_attn // n_heads
    attn_scale = d_head ** -0.5
    assert conv_width < 8 and n_heads <= LANES and n_layers == 2
    assert d_attn == d
    me = 4 * lax.axis_index("x") + 2 * lax.axis_index("y") + lax.axis_index("c")

    shard = {"rg_w_in": a["rg_w_in"][0].T, "rg_w_out": a["rg_w_out"][0], "w_kv": a["w_kv"].T,
             "attn_w_q": a["attn_w_q"][0], "attn_w_o": a["attn_w_o"][0]}
    for l in range(n_layers):
        for f in ("ffn1", "ffn2"):
            shard[(f, "gate", l)] = a[f + "_w_gate"][l].T
            shard[(f, "up", l)] = a[f + "_w_up"][l].T
            shard[(f, "down", l)] = a[f + "_w_down"][l]

    def ffn_names(f, l):
        return [(f, "gate", l), (f, "up", l), (f, "down", l)]

    def chunk_layout(names):
        spans, off = [], 0
        for nm in names:
            spans.append((nm, off, shard[nm].shape[0]))
            off += _round_up(shard[nm].shape[0], WIRE_ROW_ALIGN)
        return spans, off

    def pack_chunk(names, parts):
        return jnp.concatenate(
            [_pad_rows(parts[nm].astype(WIRE_DTYPE), _round_up(parts[nm].shape[0], WIRE_ROW_ALIGN)) for nm in names], axis=0)

    full = {}

    def unpack_chunk(names, gathered):
        for nm, o, n_rows in chunk_layout(names)[0]:
            full[nm] = gathered[:, o:o + n_rows, :].reshape(N_DEV * n_rows, d)

    fwd_chunks = [ffn_names("ffn1", 0), ["rg_w_in", "rg_w_out"], ffn_names("ffn2", 0) + ["w_kv"],
                  ffn_names("ffn1", 1) + ["attn_w_q", "attn_w_o"], ffn_names("ffn2", 1)]
    fwd_packs = [pack_chunk(names, shard) for names in fwd_chunks]
    unpack_chunk(fwd_chunks[0], _all_gather(fwd_packs[0], "gather_weights_first"))

    small_parts = [a["rg_conv_w"][0], a["rg_conv_b"][0], a["rg_b_a"][0], a["rg_b_x"][0], a["rg_lambda"][0], a["w_fgate"]]
    small_pack, small_spans = _pack_rows(small_parts, d, F32, 8)
    small_all = _all_gather(small_pack, "gather_small")
    fwd_handles, fwd_token = _exchange_start(fwd_packs[1:], False, [full[("ffn1", "down", 0)], small_all],
                                             "gather_weights_start")

    def land_weights(n, after):
        land = _exchange_wait(fwd_handles[n - 1], after, f"gather_weights_wait_{n}")
        unpack_chunk(fwd_chunks[n], _own_slot(land, fwd_packs[n], me))

    sm = [_unpack_rows(small_all, sp, p.shape) for sp, p in zip(small_spans, small_parts)]
    conv_w = jnp.moveaxis(sm[0], 0, 1).reshape(conv_width, c_dim)
    conv_b, b_a, b_x, lam = (v.reshape(1, c_dim) for v in sm[1:5])
    w_f = sm[5].reshape(d, n_heads)

    pconv = _pad_rows(_pad_cols(jnp.concatenate([conv_w, conv_b], axis=0), cp), 8)
    pvec = _pad_rows(_pad_cols(jnp.concatenate([b_a, b_x, lam], axis=0), cp), 8)
    wa_dense = _block_diag(a["rg_w_a"][0], cp).astype(MXU_DTYPE)
    wx_dense = _block_diag(a["rg_w_x"][0], cp).astype(MXU_DTYPE)
    wax = jnp.concatenate([wa_dense, wx_dense], axis=1)
    w_f_t = _pad_rows(w_f.T.astype(MXU_DTYPE), LANES)
    b_f = _pad_rows(_pad_cols(a["b_fgate"].reshape(1, n_heads), LANES), 8)

    def gain(name, l):
        return a[name][l].reshape(1, d)

    def ffn_fwd(h, f, l, after=None):
        xn = _rms_fwd(h, gain(f + "_pre_g", l), f"{f}_{l}_pre_norm", after)
        g, u, act = _ffn_up(xn, full[(f, "gate", l)], full[(f, "up", l)], f"{f}_{l}_up")
        fo, h_new = _mm_rms_res(act, full[(f, "down", l)], h, gain(f + "_post_g", l), 0.5, f"{f}_{l}_down")
        return h_new, (h, xn, g, u, act, fo)

    h0 = x
    h0a, sv_f1_0 = ffn_fwd(h0, "ffn1", 0, fwd_token)
    land_weights(1, h0a)
    w_in_gate = _pad_rows(full["rg_w_in"][:c_dim], cp)
    w_in_rec = _pad_rows(full["rg_w_in"][c_dim:], cp)
    w_in_t = jnp.concatenate([w_in_gate, w_in_rec], axis=0)
    w_out = _pad_rows(full["rg_w_out"], cp)
    hn_rg = _rms_fwd(h0a, gain("mix_pre_g", 0), "rg_pre_norm")
    gx = _mm([(hn_rg, w_in_t)], "nt", F32, "rg_in_proj")
    rec = _conv_fwd(gx, pconv, conv_width, "rg_conv")
    gates = _mm([(rec, wax)], "nn", F32, "rg_gate_proj")
    h_rec, y_rg = _scan_fwd(gx, rec, gates, pvec, "rg_scan")
    m_rg, h0b = _mm_rms_res(y_rg, w_out, h0a, gain("mix_post_g", 0), 1.0, "rg_out_proj")
    land_weights(2, h0b)
    h1, sv_f2_0 = ffn_fwd(h0b, "ffn2", 0)
    hn_kv = _rms_fwd(h1, a["kv_norm_g"].reshape(1, d), "kv_norm")
    kv = _mm([(hn_kv, full["w_kv"])], "nt", MXU_DTYPE, "kv_proj")
    fpre = _mm([(hn_kv, w_f_t)], "nt", F32, "fgate_proj")
    c_cum = _fgate_fwd(fpre, b_f, "fgate_cumsum")
    c_heads = c_cum[:, :n_heads].T
    c_col, c_row = c_heads[:, :, None], c_heads[:, None, :]
    land_weights(3, c_cum)
    h1a, sv_f1_1 = ffn_fwd(h1, "ffn1", 1)
    hn_at = _rms_fwd(h1a, gain("mix_pre_g", 1), "attn_pre_norm")
    q_s = _mm([(hn_at, full["attn_w_q"])], "nn", MXU_DTYPE, "q_proj", out_scale=attn_scale)
    o2, lse = _pair_attn_fwd(q_s, kv, c_col, c_row, "attn_fwd")
    m_at, h1b = _mm_rms_res(o2, full["attn_w_o"], h1a, gain("mix_post_g", 1), 1.0, "attn_out_proj")
    land_weights(4, h1b)
    y, sv_f2_1 = ffn_fwd(h1b, "ffn2", 1)
    dy, loss_part = _loss_head(y, target, "loss_head")

    grads_big = {}
    grads_rep = {}

    bwd_chunks = [ffn_names("ffn2", 1), ["attn_w_q", "attn_w_o"] + ffn_names("ffn1", 1),
                  ["w_kv"] + ffn_names("ffn2", 0), ["rg_w_in", "rg_w_out"], ffn_names("ffn1", 0)]
    bwd_sends, bwd_handles = [], []

    def send_grads(after):
        n = len(bwd_sends)
        send = jnp.concatenate(
            [jnp.pad(grads_big[nm].reshape(N_DEV, n_rows, d), ((0, 0), (0, _round_up(n_rows, WIRE_ROW_ALIGN) - n_rows), (0, 0)))
             for nm, _, n_rows in chunk_layout(bwd_chunks[n])[0]], axis=1)
        handles, token = _exchange_start([send], True, [after], f"exchange_grads_start_{n}")
        bwd_sends.append(send)
        bwd_handles.append(handles[0])
        return token

    def ffn_bwd(dh_out, saved, f, l, after=None, send_now=False):
        h, xn, g, u, act, fo = saved
        df, d_post = _rms_bwd(fo, gain(f + "_post_g", l), [dh_out], None, 0.5, MXU_DTYPE, f"{f}_{l}_post_norm_bwd", after)
        dg, du = _ffn_act_bwd(df, full[(f, "down", l)], g, u, f"{f}_{l}_act_bwd")
        grads_big[(f, "down", l)] = _mm([(act, df)], "tn", WIRE_DTYPE, f"{f}_{l}_dw_down")
        grads_big[(f, "gate", l)] = _mm([(dg, xn)], "tn", WIRE_DTYPE, f"{f}_{l}_dw_gate")
        grads_big[(f, "up", l)] = _mm([(du, xn)], "tn", WIRE_DTYPE, f"{f}_{l}_dw_up")
        sent = send_grads(df) if send_now else None
        dxn = _mm([(dg, full[(f, "gate", l)]), (du, full[(f, "up", l)])], "nn", F32, f"{f}_{l}_dx", sent)
        dh_in, d_pre = _rms_bwd(h, gain(f + "_pre_g", l), [dxn], dh_out, 1.0, F32, f"{f}_{l}_pre_norm_bwd")
        grads_rep[(f + "_post_g", l)] = d_post
        grads_rep[(f + "_pre_g", l)] = d_pre
        return dh_in

    dh = ffn_bwd(dy, sv_f2_1, "ffn2", 1)
    token = send_grads(dh)
    dm, d_post = _rms_bwd(m_at, gain("mix_post_g", 1), [dh], None, 1.0, MXU_DTYPE, "attn_post_norm_bwd", token)
    grads_rep[("mix_post_g", 1)] = d_post
    do2 = _mm([(dm, full["attn_w_o"])], "nt", F32, "attn_out_proj_dx")
    grads_big["attn_w_o"] = _mm([(o2, dm)], "tn", WIRE_DTYPE, "attn_out_proj_dw")
    dq2, dc_col, dk2, dv2, dc_row = _pair_attn_bwd(q_s, kv, c_col, c_row, o2, lse, do2, attn_scale, "attn_bwd")
    dc_heads = dc_col[:, :, 0] + dc_row[:, 0, :]
    dhn = _mm([(dq2, full["attn_w_q"])], "nt", F32, "q_proj_dx")
    grads_big["attn_w_q"] = _mm([(hn_at, dq2)], "tn", WIRE_DTYPE, "q_proj_dw")
    dh, d_pre = _rms_bwd(h1a, gain("mix_pre_g", 1), [dhn], dh, 1.0, F32, "attn_pre_norm_bwd")
    grads_rep[("mix_pre_g", 1)] = d_pre
    dh = ffn_bwd(dh, sv_f1_1, "ffn1", 1)
    token = send_grads(dh)
    dc_cum = _pad_cols(dc_heads.T, LANES)
    dfpre, db_f = _fgate_bwd(dc_cum, fpre, b_f, "fgate_cumsum_bwd")
    dhn_kv = _mm([(dk2, full["w_kv"][:d_attn]), (dv2, full["w_kv"][d_attn:])], "nn", F32, "kv_proj_dx")
    dhn_f = _mm([(dfpre, w_f_t)], "nn", F32, "fgate_proj_dx")
    grads_big["w_kv"] = jnp.concatenate([_mm([(dk2, hn_kv)], "tn", WIRE_DTYPE, "kv_proj_dw_k"),
                                         _mm([(dv2, hn_kv)], "tn", WIRE_DTYPE, "kv_proj_dw_v")], axis=0)
    dw_f_t = _mm([(dfpre, hn_kv)], "tn", F32, "fgate_proj_dw")
    dh, d_kvg = _rms_bwd(h1, a["kv_norm_g"].reshape(1, d), [dhn_kv, dhn_f], dh, 1.0, F32, "kv_norm_bwd", token)
    dh = ffn_bwd(dh, sv_f2_0, "ffn2", 0)
    token = send_grads(dh)
    dm, d_post = _rms_bwd(m_rg, gain("mix_post_g", 0), [dh], None, 1.0, MXU_DTYPE, "rg_post_norm_bwd", token)
    grads_rep[("mix_post_g", 0)] = d_post
    dy_rg = _mm([(dm, w_out)], "nt", F32, "rg_out_proj_dx")
    dw_out = _mm([(y_rg, dm)], "tn", WIRE_DTYPE, "rg_out_proj_dw")
    dgate, dra, dia, drec1, dpvec = _scan_bwd(dy_rg, gx, h_rec, rec, gates, pvec, "rg_scan_bwd")
    drec2 = _mm([(dra, wa_dense), (dia, wx_dense)], "nt", F32, "rg_gate_proj_dx")
    dwa_dense = _mm([(rec, dra)], "tn", F32, "rg_gate_proj_dwa")
    dwx_dense = _mm([(rec, dia)], "tn", F32, "rg_gate_proj_dwx")
    drec0, dpconv = _conv_bwd(drec1, drec2, gx, pconv, conv_width, "rg_conv_bwd")
    dhn = _mm([(dgate, w_in_gate), (drec0, w_in_rec)], "nn", F32, "rg_in_proj_dx")
    dw_in_gate = _mm([(dgate, hn_rg)], "tn", WIRE_DTYPE, "rg_in_proj_dw_gate")
    dw_in_rec = _mm([(drec0, hn_rg)], "tn", WIRE_DTYPE, "rg_in_proj_dw_rec")
    dh, d_pre = _rms_bwd(h0a, gain("mix_pre_g", 0), [dhn], dh, 1.0, F32, "rg_pre_norm_bwd")
    grads_rep[("mix_pre_g", 0)] = d_pre
    grads_big["rg_w_in"] = jnp.concatenate([dw_in_gate[:c_dim], dw_in_rec[:c_dim]], axis=0)
    grads_big["rg_w_out"] = dw_out[:c_dim]
    token = send_grads(dh)
    grad_x = ffn_bwd(dh, sv_f1_0, "ffn1", 0, token, send_now=True)

    g_shard = {}

    def land_grads(n, after):
        land = _exchange_wait(bwd_handles[n], after, f"exchange_grads_wait_{n}")
        own = lax.dynamic_index_in_dim(bwd_sends[n], me, axis=0, keepdims=False)
        g_chunk = _sum_slots(_own_slot(land, own, me), f"sum_weight_grads_{n}")
        for nm, o, n_rows in chunk_layout(bwd_chunks[n])[0]:
            g_shard[nm] = g_chunk[o:o + n_rows]

    for n in range(len(bwd_chunks) - 1):
        land_grads(n, grad_x)

    def gain_grad(name):
        return jnp.concatenate([grads_rep[(name, l)] for l in range(n_layers)], axis=0)

    rep_names = ["ffn1_pre_g", "ffn1_post_g", "mix_pre_g", "mix_post_g", "ffn2_pre_g", "ffn2_post_g"]
    rep_parts = [gain_grad(nm) for nm in rep_names]
    rep_names += ["kv_norm_g", "b_fgate", "rg_w_a", "rg_w_x", "rg_conv_w", "rg_conv_b", "rg_b_a", "rg_b_x", "rg_lambda", "w_fgate"]
    rep_parts += [
        d_kvg, db_f[0, :n_heads],
        _diag_blocks(dwa_dense, n_blocks, lru_block), _diag_blocks(dwx_dense, n_blocks, lru_block),
        dpconv[:conv_width, :c_dim], dpconv[conv_width, :c_dim],
        dpvec[0, :c_dim], dpvec[1, :c_dim], dpvec[2, :c_dim],
        dw_f_t[:n_heads].T]
    rep_pack, rep_spans = _pack_rows(rep_parts, d, F32, WIRE_ROW_ALIGN)
    rep_sum = _sum_slots(_all_gather(rep_pack, "gather_small_grads"), "sum_small_grads")
    g_rep = {nm: _unpack_rows(rep_sum, sp, p.shape) for nm, sp, p in zip(rep_names, rep_spans, rep_parts)}

    def my_cols(full_grad, n):
        return lax.dynamic_slice_in_dim(full_grad, me * n, n, axis=full_grad.ndim - 1)

    def ffn_grads(f):
        grad[f + "_w_gate"] = jnp.stack([g_shard[(f, "gate", l)].T for l in range(n_layers)])
        grad[f + "_w_up"] = jnp.stack([g_shard[(f, "up", l)].T for l in range(n_layers)])
        grad[f + "_w_down"] = jnp.stack([g_shard[(f, "down", l)] for l in range(n_layers)])

    grad = {}
    for nm in ("ffn1_pre_g", "ffn1_post_g", "mix_pre_g", "mix_post_g", "ffn2_pre_g", "ffn2_post_g"):
        grad[nm] = g_rep[nm]
    ffn_grads("ffn2")
    grad["rg_w_in"] = g_shard["rg_w_in"].T[None]
    grad["rg_conv_w"] = my_cols(g_rep["rg_conv_w"], c_shard)[None]
    for nm in ("rg_conv_b", "rg_b_a", "rg_b_x", "rg_lambda"):
        grad[nm] = my_cols(g_rep[nm], c_shard)[None]
    grad["rg_w_a"] = g_rep["rg_w_a"][None]
    grad["rg_w_x"] = g_rep["rg_w_x"][None]
    grad["rg_w_out"] = g_shard["rg_w_out"][None]
    grad["kv_norm_g"] = g_rep["kv_norm_g"].reshape(d)
    grad["w_kv"] = g_shard["w_kv"].T
    grad["w_fgate"] = lax.dynamic_slice_in_dim(g_rep["w_fgate"], me * (d // N_DEV), d // N_DEV, axis=0)
    grad["b_fgate"] = g_rep["b_fgate"]
    grad["attn_w_q"] = g_shard["attn_w_q"][None]
    grad["attn_w_o"] = g_shard["attn_w_o"][None]

    delta, new_m, new_v = {}, {}, {}

    def adamw(nm):
        w = a[nm]
        shape = w.shape
        two_d = (1, shape[0]) if w.ndim == 1 else (-1, shape[-1])
        dl, mo, vo = _adamw(w.reshape(two_d), grad[nm].reshape(two_d), a["m_" + nm].reshape(two_d),
                            a["v_" + nm].reshape(two_d), "adamw_" + nm)
        delta[nm], new_m[nm], new_v[nm] = dl.reshape(shape), mo.reshape(shape), vo.reshape(shape)
        grad[nm] = grad[nm].reshape(shape)

    last_names = ("ffn1_w_gate", "ffn1_w_up", "ffn1_w_down")
    for nm in WEIGHT_NAMES:
        if nm not in last_names:
            adamw(nm)
    land_grads(len(bwd_chunks) - 1, delta["attn_w_o"])
    ffn_grads("ffn1")
    for nm in last_names:
        adamw(nm)

    loss = lax.psum(loss_part[0, 0], AXES)
    return (loss, grad_x[None], *[grad[n] for n in WEIGHT_NAMES], *[delta[n] for n in WEIGHT_NAMES],
            *[new_m[n] for n in WEIGHT_NAMES], *[new_v[n] for n in WEIGHT_NAMES])


def kernel(x, ffn1_pre_g, ffn1_w_gate, ffn1_w_up, ffn1_w_down, ffn1_post_g, mix_pre_g, mix_post_g, ffn2_pre_g, ffn2_w_gate, ffn2_w_up, ffn2_w_down, ffn2_post_g, rg_w_in, rg_conv_w, rg_conv_b, rg_w_a, rg_b_a, rg_w_x, rg_b_x, rg_lambda, rg_w_out, kv_norm_g, w_kv, w_fgate, b_fgate, attn_w_q, attn_w_o, loss_target, m_ffn1_pre_g, m_ffn1_w_gate, m_ffn1_w_up, m_ffn1_w_down, m_ffn1_post_g, m_mix_pre_g, m_mix_post_g, m_ffn2_pre_g, m_ffn2_w_gate, m_ffn2_w_up, m_ffn2_w_down, m_ffn2_post_g, m_rg_w_in, m_rg_conv_w, m_rg_conv_b, m_rg_w_a, m_rg_b_a, m_rg_w_x, m_rg_b_x, m_rg_lambda, m_rg_w_out, m_kv_norm_g, m_w_kv, m_w_fgate, m_b_fgate, m_attn_w_q, m_attn_w_o, v_ffn1_pre_g, v_ffn1_w_gate, v_ffn1_w_up, v_ffn1_w_down, v_ffn1_post_g, v_mix_pre_g, v_mix_post_g, v_ffn2_pre_g, v_ffn2_w_gate, v_ffn2_w_up, v_ffn2_w_down, v_ffn2_post_g, v_rg_w_in, v_rg_conv_w, v_rg_conv_b, v_rg_w_a, v_rg_b_a, v_rg_w_x, v_rg_b_x, v_rg_lambda, v_rg_w_out, v_kv_norm_g, v_w_kv, v_w_fgate, v_b_fgate, v_attn_w_q, v_attn_w_o):
    return _train_step(dict(locals()))
```

```python
import functools

import jax
import jax.numpy as jnp
from jax import lax
from jax.experimental import pallas as pl
from jax.experimental.pallas import tpu as pltpu

F32 = jnp.float32
MXU_DTYPE = jnp.bfloat16
WIRE_DTYPE = jnp.bfloat16
N_DEV = 8
AXES = ("x", "y", "c")
LANES = 128
WIRE_ROW_ALIGN = 16
VMEM_LIMIT_MIN = 32 * 2 ** 20
VMEM_LIMIT_MAX = 56 * 2 ** 20

RMS_EPS = 1e-6
LRU_C = 8.0
ADAM_LR, ADAM_B1, ADAM_B2, ADAM_EPS, ADAM_WD, ADAM_STEP = 0.001, 0.9, 0.999, 1e-08, 0.01, 10

WEIGHT_NAMES = (
    "ffn1_pre_g", "ffn1_w_gate", "ffn1_w_up", "ffn1_w_down", "ffn1_post_g", "mix_pre_g", "mix_post_g",
    "ffn2_pre_g", "ffn2_w_gate", "ffn2_w_up", "ffn2_w_down", "ffn2_post_g", "rg_w_in", "rg_conv_w",
    "rg_conv_b", "rg_w_a", "rg_b_a", "rg_w_x", "rg_b_x", "rg_lambda", "rg_w_out", "kv_norm_g", "w_kv",
    "w_fgate", "b_fgate", "attn_w_q", "attn_w_o")


def _round_up(n, m):
    return (n + m - 1) // m * m


def _tile(dim, target, align=LANES):
    if dim <= target:
        return dim
    best = None
    t = align
    while t <= target:
        if dim % t == 0:
            best = t
        t += align
    return dim if best is None else best


def _cparams(semantics, vmem_estimate):
    limit = min(VMEM_LIMIT_MAX, max(VMEM_LIMIT_MIN, 2 * int(vmem_estimate)))
    return pltpu.CompilerParams(dimension_semantics=semantics, vmem_limit_bytes=limit)


def _nbytes(shape, dtype):
    n = 1
    for s in shape:
        n *= s
    return n * jnp.dtype(dtype).itemsize


def _sigmoid(x):
    return jax.nn.sigmoid(x)


def _softplus(x):
    return jnp.maximum(x, 0.0) + jnp.log1p(jnp.exp(-jnp.abs(x)))


def _expm1(x):
    series = x * (1.0 + x * (0.5 + x * (1.0 / 6.0 + x * (1.0 / 24.0 + x * (1.0 / 120.0)))))
    return jnp.where(jnp.abs(x) < 0.25, series, jnp.exp(x) - 1.0)


_GELU_C = 0.7978845608028654
_GELU_A = 0.044715


def _gelu(x):
    return 0.5 * x * (1.0 + jnp.tanh(_GELU_C * (x + _GELU_A * x * x * x)))


def _gelu_grad(x):
    t = jnp.tanh(_GELU_C * (x + _GELU_A * x * x * x))
    return 0.5 * (1.0 + t) + 0.5 * x * (1.0 - t * t) * _GELU_C * (1.0 + 3.0 * _GELU_A * x * x)


_DOT_DIMS = {"nn": ((1,), (0,)), "nt": ((1,), (1,)), "tn": ((0,), (0,))}


def _dot(a, b, mode):
    return lax.dot_general(a.astype(MXU_DTYPE), b.astype(MXU_DTYPE), (_DOT_DIMS[mode], ((), ())),
                           preferred_element_type=F32)


def _mm(pairs, mode, out_dtype, name, after=None, out_scale=None):
    a0, b0 = pairs[0]
    if mode == "tn":
        k_dim, m_dim = a0.shape
        n_dim = b0.shape[1]
    else:
        m_dim, k_dim = a0.shape
        n_dim = b0.shape[0] if mode == "nt" else b0.shape[1]
    for a, b in pairs:
        assert a.shape == a0.shape and b.shape == b0.shape
    tm = _tile(m_dim, 1408 if mode == "tn" else 512)
    tn = _tile(n_dim, 1408)
    tk = _tile(k_dim, 1408)
    nk = k_dim // tk
    n_pairs = len(pairs)

    if mode == "tn":
        a_spec = pl.BlockSpec((tk, tm), lambda i, j, k: (k, i))
    else:
        a_spec = pl.BlockSpec((tm, tk), lambda i, j, k: (i, k))
    if mode == "nt":
        b_spec = pl.BlockSpec((tn, tk), lambda i, j, k: (j, k))
    else:
        b_spec = pl.BlockSpec((tk, tn), lambda i, j, k: (k, j))

    order = [] if after is None else [after]

    def body(*refs):
        ins, o_ref, acc = refs[:2 * n_pairs], refs[-2], refs[-1]
        k = pl.program_id(2)

        @pl.when(k == 0)
        def _():
            acc[...] = jnp.zeros_like(acc)

        s = acc[...]
        for p in range(n_pairs):
            s = s + _dot(ins[2 * p][...], ins[2 * p + 1][...], mode)
        acc[...] = s

        @pl.when(k == nk - 1)
        def _():
            r = acc[...] if out_scale is None else acc[...] * out_scale
            o_ref[...] = r.astype(out_dtype)

    est = (2 * n_pairs * (_nbytes((tm, tk), a0.dtype) + _nbytes((tk, tn), b0.dtype))
           + 2 * _nbytes((tm, tn), out_dtype) + 2 * _nbytes((tm, tn), F32))
    flat = [t for ab in pairs for t in ab]
    return pl.pallas_call(
        body, name=name, grid=(m_dim // tm, n_dim // tn, nk),
        in_specs=[a_spec, b_spec] * n_pairs + [_ANY] * len(order),
        out_specs=pl.BlockSpec((tm, tn), lambda i, j, k: (i, j)),
        out_shape=jax.ShapeDtypeStruct((m_dim, n_dim), out_dtype),
        scratch_shapes=[pltpu.VMEM((tm, tn), F32)],
        compiler_params=_cparams(("parallel", "parallel", "arbitrary"), est),
    )(*flat, *order)


_ANY = pl.BlockSpec(memory_space=pl.ANY)


def _rms_fwd(x, gain, name, after=None):
    s_dim, d = x.shape
    tm = _tile(s_dim, 512, 8)

    def body(*refs):
        x_ref, g_ref, o_ref = refs[0], refs[1], refs[-1]
        v = x_ref[...]
        r = lax.rsqrt(jnp.mean(v * v, axis=-1, keepdims=True) + RMS_EPS)
        o_ref[...] = (v * r * g_ref[...]).astype(MXU_DTYPE)

    order = [] if after is None else [after]
    return pl.pallas_call(
        body, name=name, grid=(s_dim // tm,),
        in_specs=[pl.BlockSpec((tm, d), lambda i: (i, 0)), pl.BlockSpec((1, d), lambda i: (0, 0))] + [_ANY] * len(order),
        out_specs=pl.BlockSpec((tm, d), lambda i: (i, 0)),
        out_shape=jax.ShapeDtypeStruct((s_dim, d), MXU_DTYPE),
        compiler_params=_cparams(("parallel",), 6 * _nbytes((tm, d), F32)),
    )(x, gain, *order)


def _rms_bwd(x, gain, dys, res, scale, out_dtype, name, after=None):
    s_dim, d = x.shape
    tm = _tile(s_dim, 512, 8)
    n_dy = len(dys)
    has_res = res is not None
    order = [] if after is None else [after]

    def body(*refs):
        x_ref, g_ref = refs[0], refs[1]
        dy_refs = refs[2:2 + n_dy]
        res_ref = refs[2 + n_dy] if has_res else None
        dx_ref, dg_ref = refs[-2], refs[-1]

        @pl.when(pl.program_id(0) == 0)
        def _():
            dg_ref[...] = jnp.zeros_like(dg_ref)

        v = x_ref[...]
        r = lax.rsqrt(jnp.mean(v * v, axis=-1, keepdims=True) + RMS_EPS)
        xh = v * r
        dy = dy_refs[0][...].astype(F32)
        for extra in dy_refs[1:]:
            dy = dy + extra[...].astype(F32)
        gd = dy * g_ref[...]
        dx = scale * r * (gd - xh * jnp.mean(gd * xh, axis=-1, keepdims=True))
        if has_res:
            dx = dx + res_ref[...]
        dx_ref[...] = dx.astype(out_dtype)
        dg_ref[...] += scale * jnp.sum(dy * xh, axis=0, keepdims=True)

    row = pl.BlockSpec((tm, d), lambda i: (i, 0))
    vec = pl.BlockSpec((1, d), lambda i: (0, 0))
    ops = [x, gain] + list(dys) + ([res] if has_res else [])
    return pl.pallas_call(
        body, name=name, grid=(s_dim // tm,),
        in_specs=[row, vec] + [row] * (n_dy + int(has_res)) + [_ANY] * len(order),
        out_specs=[row, vec],
        out_shape=[jax.ShapeDtypeStruct((s_dim, d), out_dtype), jax.ShapeDtypeStruct((1, d), F32)],
        compiler_params=_cparams(("arbitrary",), (2 * len(ops) + 6) * _nbytes((tm, d), F32)),
    )(*ops, *order)


def _mm_rms_res(a, b, h, gain, scale, name):
    s_dim, k_dim = a.shape
    d = b.shape[1]
    tm = _tile(s_dim, 512, 8)
    tk = _tile(k_dim, 1408)
    nk = k_dim // tk

    def body(a_ref, b_ref, h_ref, g_ref, f_ref, o_ref, acc):
        k = pl.program_id(1)

        @pl.when(k == 0)
        def _():
            acc[...] = jnp.zeros_like(acc)

        acc[...] += _dot(a_ref[...], b_ref[...], "nn")

        @pl.when(k == nk - 1)
        def _():
            f = acc[...]
            r = lax.rsqrt(jnp.mean(f * f, axis=-1, keepdims=True) + RMS_EPS)
            f_ref[...] = f
            o_ref[...] = h_ref[...] + scale * (f * r * g_ref[...])

    row = pl.BlockSpec((tm, d), lambda i, k: (i, 0))
    est = (2 * (_nbytes((tm, tk), a.dtype) + _nbytes((tk, d), b.dtype)) + 8 * _nbytes((tm, d), F32))
    return pl.pallas_call(
        body, name=name, grid=(s_dim // tm, nk),
        in_specs=[pl.BlockSpec((tm, tk), lambda i, k: (i, k)), pl.BlockSpec((tk, d), lambda i, k: (k, 0)),
                  row, pl.BlockSpec((1, d), lambda i, k: (0, 0))],
        out_specs=[row, row],
        out_shape=[jax.ShapeDtypeStruct((s_dim, d), F32), jax.ShapeDtypeStruct((s_dim, d), F32)],
        scratch_shapes=[pltpu.VMEM((tm, d), F32)],
        compiler_params=_cparams(("parallel", "arbitrary"), est),
    )(a, b, h, gain)


def _ffn_up(xn, wg_t, wu_t, name):
    s_dim, d = xn.shape
    f_dim = wg_t.shape[0]
    tm = _tile(s_dim, 1024, 8)
    tf = _tile(f_dim, 256)

    def body(x_ref, wg_ref, wu_ref, g_ref, u_ref, a_ref):
        x = x_ref[...]
        g = _dot(x, wg_ref[...], "nt")
        u = _dot(x, wu_ref[...], "nt")
        g_ref[...] = g.astype(MXU_DTYPE)
        u_ref[...] = u.astype(MXU_DTYPE)
        a_ref[...] = (g * _sigmoid(g) * u).astype(MXU_DTYPE)

    w_spec = pl.BlockSpec((tf, d), lambda i, j: (j, 0))
    o_spec = pl.BlockSpec((tm, tf), lambda i, j: (i, j))
    o_shape = jax.ShapeDtypeStruct((s_dim, f_dim), MXU_DTYPE)
    est = 2 * _nbytes((tm, d), xn.dtype) + 4 * _nbytes((tf, d), wg_t.dtype) + 10 * _nbytes((tm, tf), F32)
    return pl.pallas_call(
        body, name=name, grid=(s_dim // tm, f_dim // tf),
        in_specs=[pl.BlockSpec((tm, d), lambda i, j: (i, 0)), w_spec, w_spec],
        out_specs=[o_spec, o_spec, o_spec], out_shape=[o_shape, o_shape, o_shape],
        compiler_params=_cparams(("parallel", "parallel"), est),
    )(xn, wg_t, wu_t)


def _ffn_act_bwd(df, wd, g, u, name):
    s_dim, d = df.shape
    f_dim = wd.shape[0]
    tm = _tile(s_dim, 1024, 8)
    tf = _tile(f_dim, 256)

    def body(df_ref, wd_ref, g_ref, u_ref, dg_ref, du_ref):
        dh = _dot(df_ref[...], wd_ref[...], "nt")
        gv = g_ref[...].astype(F32)
        uv = u_ref[...].astype(F32)
        sg = _sigmoid(gv)
        dg_ref[...] = (dh * uv * (sg * (1.0 + gv * (1.0 - sg)))).astype(MXU_DTYPE)
        du_ref[...] = (dh * gv * sg).astype(MXU_DTYPE)

    t_spec = pl.BlockSpec((tm, tf), lambda i, j: (i, j))
    o_shape = jax.ShapeDtypeStruct((s_dim, f_dim), MXU_DTYPE)
    est = 2 * _nbytes((tm, d), df.dtype) + 2 * _nbytes((tf, d), wd.dtype) + 12 * _nbytes((tm, tf), F32)
    return pl.pallas_call(
        body, name=name, grid=(s_dim // tm, f_dim // tf),
        in_specs=[pl.BlockSpec((tm, d), lambda i, j: (i, 0)), pl.BlockSpec((tf, d), lambda i, j: (j, 0)),
                  t_spec, t_spec],
        out_specs=[t_spec, t_spec], out_shape=[o_shape, o_shape],
        compiler_params=_cparams(("parallel", "parallel"), est),
    )(df, wd, g, u)


def _loss_head(y, target, name):
    s_dim, d = y.shape
    tm = _tile(s_dim, 512, 8)
    nt = s_dim // tm

    def body(y_ref, t_ref, dy_ref, loss_ref, acc):
        i = pl.program_id(0)

        @pl.when(i == 0)
        def _():
            acc[...] = jnp.zeros_like(acc)

        e = y_ref[...] - t_ref[...]
        dy_ref[...] = e * (1.0 / d)
        acc[...] += jnp.sum(e * e, axis=0, keepdims=True)

        @pl.when(i == nt - 1)
        def _():
            loss_ref[...] = jnp.sum(acc[...], axis=1, keepdims=True) * (0.5 / d)

    row = pl.BlockSpec((tm, d), lambda i: (i, 0))
    return pl.pallas_call(
        body, name=name, grid=(nt,), in_specs=[row, row],
        out_specs=[row, pl.BlockSpec((1, 1), lambda i: (0, 0))],
        out_shape=[jax.ShapeDtypeStruct((s_dim, d), F32), jax.ShapeDtypeStruct((1, 1), F32)],
        scratch_shapes=[pltpu.VMEM((1, d), F32)],
        compiler_params=_cparams(("arbitrary",), 8 * _nbytes((tm, d), F32)),
    )(y, target)


def _shift_down(v, sh, row):
    if sh == 0:
        return v
    return jnp.where(row >= sh, pltpu.roll(v, sh, 0), 0.0)


def _shift_up(v, sh, row):
    if sh == 0:
        return v
    n = v.shape[0]
    return jnp.where(row < n - sh, pltpu.roll(v, n - sh, 0), 0.0)


def _conv_fwd(gx, pconv, width, name):
    s_dim, cp2 = gx.shape
    cp = cp2 // 2
    nc = cp // LANES

    def body(x_ref, p_ref, o_ref):
        x = x_ref[...]
        row = lax.broadcasted_iota(jnp.int32, x.shape, 0)
        y = jnp.zeros_like(x) + p_ref[pl.ds(width, 1), :]
        for k in range(width):
            y = y + p_ref[pl.ds(k, 1), :] * _shift_down(x, width - 1 - k, row)
        o_ref[...] = y

    return pl.pallas_call(
        body, name=name, grid=(nc,),
        in_specs=[pl.BlockSpec((s_dim, LANES), lambda j: (0, nc + j)), pl.BlockSpec((8, LANES), lambda j: (0, j))],
        out_specs=pl.BlockSpec((s_dim, LANES), lambda j: (0, j)),
        out_shape=jax.ShapeDtypeStruct((s_dim, cp), F32),
        compiler_params=_cparams(("parallel",), 10 * _nbytes((s_dim, LANES), F32)),
    )(gx, pconv)


def _conv_bwd(d1, d2, gx, pconv, width, name):
    s_dim, cp = d1.shape
    nc = cp // LANES

    def body(d1_ref, d2_ref, x_ref, p_ref, dx_ref, dp_ref):
        d = d1_ref[...] + d2_ref[...]
        x = x_ref[...]
        row = lax.broadcasted_iota(jnp.int32, x.shape, 0)
        dx = jnp.zeros_like(d)
        dp_ref[...] = jnp.zeros_like(dp_ref)
        for k in range(width):
            sh = width - 1 - k
            dx = dx + p_ref[pl.ds(k, 1), :] * _shift_up(d, sh, row)
            dp_ref[pl.ds(k, 1), :] = jnp.sum(d * _shift_down(x, sh, row), axis=0, keepdims=True)
        dp_ref[pl.ds(width, 1), :] = jnp.sum(d, axis=0, keepdims=True)
        dx_ref[...] = dx.astype(MXU_DTYPE)

    strip = pl.BlockSpec((s_dim, LANES), lambda j: (0, j))
    par = pl.BlockSpec((8, LANES), lambda j: (0, j))
    return pl.pallas_call(
        body, name=name, grid=(nc,),
        in_specs=[strip, strip, pl.BlockSpec((s_dim, LANES), lambda j: (0, nc + j)), par],
        out_specs=[strip, par],
        out_shape=[jax.ShapeDtypeStruct((s_dim, cp), MXU_DTYPE), jax.ShapeDtypeStruct((8, cp), F32)],
        compiler_params=_cparams(("parallel",), 14 * _nbytes((s_dim, LANES), F32)),
    )(d1, d2, gx, pconv)


def _lru_coeffs(ra, ia, p_ref):
    r = _sigmoid(ra + p_ref[pl.ds(0, 1), :])
    i = _sigmoid(ia + p_ref[pl.ds(1, 1), :])
    sp = _softplus(-p_ref[pl.ds(2, 1), :])
    log_a = -LRU_C * r * sp
    a = jnp.exp(log_a)
    mult = jnp.sqrt(-_expm1(2.0 * log_a))
    return r, i, sp, a, mult


def _scan_fwd(gx, rec, gates, pvec, name):
    s_dim, cp = rec.shape
    ts = _tile(s_dim, 256, 8)
    nt = s_dim // ts

    def body(gate_ref, rec_ref, ra_ref, ia_ref, p_ref, h_ref, y_ref, a_s, u_s, carry):
        @pl.when(pl.program_id(0) == 0)
        def _():
            carry[...] = jnp.zeros_like(carry)

        rec_v = rec_ref[...]
        _, i, _, a, mult = _lru_coeffs(ra_ref[...], ia_ref[...], p_ref)
        a_s[...] = a
        u_s[...] = mult * (i * rec_v)

        def step(t, h):
            h = a_s[pl.ds(t, 1), :] * h + u_s[pl.ds(t, 1), :]
            h_ref[pl.ds(t, 1), :] = h
            return h

        carry[pl.ds(0, 1), :] = lax.fori_loop(0, ts, step, carry[pl.ds(0, 1), :], unroll=8)
        y_ref[...] = (_gelu(gate_ref[...]) * h_ref[...]).astype(MXU_DTYPE)

    blk = pl.BlockSpec((ts, cp), lambda t: (t, 0))
    return pl.pallas_call(
        body, name=name, grid=(nt,),
        in_specs=[blk, blk, blk, pl.BlockSpec((ts, cp), lambda t: (t, 1)), pl.BlockSpec((8, cp), lambda t: (0, 0))],
        out_specs=[blk, blk],
        out_shape=[jax.ShapeDtypeStruct((s_dim, cp), F32), jax.ShapeDtypeStruct((s_dim, cp), MXU_DTYPE)],
        scratch_shapes=[pltpu.VMEM((ts, cp), F32), pltpu.VMEM((ts, cp), F32), pltpu.VMEM((8, cp), F32)],
        compiler_params=_cparams(("arbitrary",), 14 * _nbytes((ts, cp), F32)),
    )(gx, rec, gates, gates, pvec)


def _scan_bwd(dy, gx, hrec, rec, gates, pvec, name):
    s_dim, cp = rec.shape
    ts = _tile(s_dim, 128, 8)
    nt = s_dim // ts

    def body(dy_ref, gate_ref, h_ref, hp_ref, rec_ref, ra_ref, ia_ref, p_ref,
             dgate_ref, dra_ref, dia_ref, drec_ref, dp_ref, a_s, d_s, carry):
        t_id = pl.program_id(0)

        @pl.when(t_id == 0)
        def _():
            carry[...] = jnp.zeros_like(carry)
            dp_ref[...] = jnp.zeros_like(dp_ref)

        rec_v = rec_ref[...]
        r, i, sp, a, mult = _lru_coeffs(ra_ref[...], ia_ref[...], p_ref)
        gate = gate_ref[...]
        dyv = dy_ref[...]
        h = h_ref[...]
        dgate_ref[...] = (dyv * h * _gelu_grad(gate)).astype(MXU_DTYPE)
        a_s[...] = a
        d_s[...] = dyv * _gelu(gate)

        def step(k, c):
            t = ts - 1 - k
            d = d_s[pl.ds(t, 1), :] + c
            d_s[pl.ds(t, 1), :] = d
            return a_s[pl.ds(t, 1), :] * d

        carry[pl.ds(0, 1), :] = lax.fori_loop(0, ts, step, carry[pl.ds(0, 1), :], unroll=8)
        dh = d_s[...]
        row = lax.broadcasted_iota(jnp.int32, h.shape, 0)
        first = jnp.where(t_id == nt - 1, 0.0, 1.0) * hp_ref[pl.ds(7, 1), :]
        h_prev = jnp.where(row == 0, first, pltpu.roll(h, 1, 0))
        dix = dh * mult
        dla = dh * h_prev * a - dh * (i * rec_v) * (a * a) / mult
        dra = dla * (-LRU_C * sp) * r * (1.0 - r)
        dia = dix * rec_v * i * (1.0 - i)
        dra_ref[...] = dra.astype(MXU_DTYPE)
        dia_ref[...] = dia.astype(MXU_DTYPE)
        drec_ref[...] = dix * i
        dsp = jnp.sum(dla * (-LRU_C * r), axis=0, keepdims=True)
        dp_ref[pl.ds(0, 1), :] += jnp.sum(dra, axis=0, keepdims=True)
        dp_ref[pl.ds(1, 1), :] += jnp.sum(dia, axis=0, keepdims=True)
        dp_ref[pl.ds(2, 1), :] += dsp * (-_sigmoid(-p_ref[pl.ds(2, 1), :]))

    blk = pl.BlockSpec((ts, cp), lambda t: (nt - 1 - t, 0))
    prev = pl.BlockSpec((8, cp), lambda t: (jnp.maximum((nt - 1 - t) * (ts // 8) - 1, 0), 0))
    par = pl.BlockSpec((8, cp), lambda t: (0, 0))
    lo = jax.ShapeDtypeStruct((s_dim, cp), MXU_DTYPE)
    return pl.pallas_call(
        body, name=name, grid=(nt,),
        in_specs=[blk, blk, blk, prev, blk, blk, pl.BlockSpec((ts, cp), lambda t: (nt - 1 - t, 1)), par],
        out_specs=[blk, blk, blk, blk, par],
        out_shape=[lo, lo, lo, jax.ShapeDtypeStruct((s_dim, cp), F32), jax.ShapeDtypeStruct((8, cp), F32)],
        scratch_shapes=[pltpu.VMEM((ts, cp), F32), pltpu.VMEM((ts, cp), F32), pltpu.VMEM((8, cp), F32)],
        compiler_params=_cparams(("arbitrary",), 40 * _nbytes((ts, cp), F32)),
    )(dy, gx, hrec, hrec, rec, gates, gates, pvec)


def _fgate_fwd(fpre, bias, name):
    s_dim, w = fpre.shape
    ts = _tile(s_dim, 512, 8)

    def body(f_ref, b_ref, c_ref, lf_s, carry):
        @pl.when(pl.program_id(0) == 0)
        def _():
            carry[...] = jnp.zeros_like(carry)

        lf_s[...] = -_softplus(-(f_ref[...] + b_ref[pl.ds(0, 1), :]))

        def step(t, c):
            c = c + lf_s[pl.ds(t, 1), :]
            c_ref[pl.ds(t, 1), :] = c
            return c

        carry[pl.ds(0, 1), :] = lax.fori_loop(0, ts, step, carry[pl.ds(0, 1), :], unroll=8)

    blk = pl.BlockSpec((ts, w), lambda t: (t, 0))
    return pl.pallas_call(
        body, name=name, grid=(s_dim // ts,),
        in_specs=[blk, pl.BlockSpec((8, w), lambda t: (0, 0))], out_specs=blk,
        out_shape=jax.ShapeDtypeStruct((s_dim, w), F32),
        scratch_shapes=[pltpu.VMEM((ts, w), F32), pltpu.VMEM((8, w), F32)],
        compiler_params=_cparams(("arbitrary",), 12 * _nbytes((ts, w), F32)),
    )(fpre, bias)


def _fgate_bwd(dc, fpre, bias, name):
    s_dim, w = fpre.shape
    ts = _tile(s_dim, 512, 8)
    nt = s_dim // ts

    def body(dc_ref, f_ref, b_ref, df_ref, db_ref, d_s, carry):
        @pl.when(pl.program_id(0) == 0)
        def _():
            carry[...] = jnp.zeros_like(carry)
            db_ref[...] = jnp.zeros_like(db_ref)

        d_s[...] = dc_ref[...]

        def step(k, c):
            t = ts - 1 - k
            c = c + d_s[pl.ds(t, 1), :]
            d_s[pl.ds(t, 1), :] = c
            return c

        carry[pl.ds(0, 1), :] = lax.fori_loop(0, ts, step, carry[pl.ds(0, 1), :], unroll=8)
        df = d_s[...] * _sigmoid(-(f_ref[...] + b_ref[pl.ds(0, 1), :]))
        df_ref[...] = df
        db_ref[pl.ds(0, 1), :] += jnp.sum(df, axis=0, keepdims=True)

    blk = pl.BlockSpec((ts, w), lambda t: (nt - 1 - t, 0))
    par = pl.BlockSpec((8, w), lambda t: (0, 0))
    return pl.pallas_call(
        body, name=name, grid=(nt,), in_specs=[blk, blk, par], out_specs=[blk, par],
        out_shape=[jax.ShapeDtypeStruct((s_dim, w), F32), jax.ShapeDtypeStruct((8, w), F32)],
        scratch_shapes=[pltpu.VMEM((ts, w), F32), pltpu.VMEM((8, w), F32)],
        compiler_params=_cparams(("arbitrary",), 12 * _nbytes((ts, w), F32)),
    )(dc, fpre, bias)


def _head_lanes(hh, dh):
    lane = lax.broadcasted_iota(jnp.int32, (1, LANES), 1)
    return (lane >= hh * dh) & (lane < (hh + 1) * dh)


def _pair_attn_fwd(q, kv, v_t, c_col, c_row, name):
    s_dim, da = q.shape
    n_h = c_col.shape[0]
    dh = da // n_h
    assert LANES % dh == 0 and da % LANES == 0
    hb = LANES // dh
    n_blocks = da // LANES
    t = _tile(s_dim, 512, LANES)
    nb = s_dim // t

    def body(q_ref, k_ref, vt_ref, cq_ref, ck_ref, o_ref, lse_ref, m_s, l_s, acc):
        i, j = pl.program_id(1), pl.program_id(2)

        @pl.when(j == 0)
        def _():
            m_s[...] = jnp.full_like(m_s, -jnp.inf)
            l_s[...] = jnp.zeros_like(l_s)
            acc[...] = jnp.zeros_like(acc)

        def tile(masked):
            qv = q_ref[...]
            for hh in range(hb):
                st = _dot(k_ref[...], jnp.where(_head_lanes(hh, dh), qv, jnp.zeros_like(qv)), "nt")
                st = st + (cq_ref[hh] - ck_ref[hh])
                if masked:
                    keep = lax.broadcasted_iota(jnp.int32, (t, t), 0) <= lax.broadcasted_iota(jnp.int32, (t, t), 1)
                    st = jnp.where(keep, st, -jnp.inf)
                m_prev = m_s[hh]
                m_new = jnp.maximum(m_prev, jnp.max(st, axis=0, keepdims=True))
                alpha = jnp.exp(m_prev - m_new)
                p = jnp.exp(st - m_new)
                l_s[hh] = alpha * l_s[hh] + jnp.sum(p, axis=0, keepdims=True)
                acc[hh] = alpha * acc[hh] + _dot(vt_ref[...], p, "nn")
                m_s[hh] = m_new

        pl.when(j < i)(functools.partial(tile, False))
        pl.when(j == i)(functools.partial(tile, True))

        @pl.when(j == nb - 1)
        def _():
            feat = lax.broadcasted_iota(jnp.int32, (LANES, 1), 0)
            out_t = jnp.zeros((LANES, t), F32)
            for hh in range(hb):
                out_t = jnp.where((feat >= hh * dh) & (feat < (hh + 1) * dh), acc[hh] / l_s[hh], out_t)
                lse_ref[hh] = m_s[hh] + jnp.log(l_s[hh])
            o_ref[...] = out_t.T

    q_spec = pl.BlockSpec((t, LANES), lambda b, i, j: (i, b))
    k_spec = pl.BlockSpec((t, LANES), lambda b, i, j: (jnp.minimum(j, i), b))
    vt_spec = pl.BlockSpec((LANES, t), lambda b, i, j: (b, jnp.minimum(j, i)))
    cq_spec = pl.BlockSpec((hb, 1, t), lambda b, i, j: (b, 0, i))
    ck_spec = pl.BlockSpec((hb, t, 1), lambda b, i, j: (b, jnp.minimum(j, i), 0))
    return pl.pallas_call(
        body, name=name, grid=(n_blocks, nb, nb),
        in_specs=[q_spec, k_spec, vt_spec, cq_spec, ck_spec], out_specs=[q_spec, cq_spec],
        out_shape=[jax.ShapeDtypeStruct((s_dim, da), F32), jax.ShapeDtypeStruct((n_h, 1, s_dim), F32)],
        scratch_shapes=[pltpu.VMEM((hb, 1, t), F32), pltpu.VMEM((hb, 1, t), F32), pltpu.VMEM((hb, LANES, t), F32)],
        compiler_params=_cparams(("parallel", "parallel", "arbitrary"), 10 * hb * _nbytes((t, t), F32)),
    )(q, kv, v_t, c_row, c_col)


def _attn_delta(do, o, n_h, name):
    s_dim, da = o.shape
    dh = da // n_h
    hb = LANES // dh
    t = _tile(s_dim, 512, LANES)

    def body(do_ref, o_ref, d_ref):
        prod_t = (do_ref[...].astype(MXU_DTYPE).astype(F32) * o_ref[...]).T
        for hh in range(hb):
            d_ref[hh] = jnp.sum(prod_t[hh * dh:(hh + 1) * dh], axis=0, keepdims=True)

    blk = pl.BlockSpec((t, LANES), lambda b, i: (i, b))
    return pl.pallas_call(
        body, name=name, grid=(da // LANES, s_dim // t), in_specs=[blk, blk],
        out_specs=pl.BlockSpec((hb, 1, t), lambda b, i: (b, 0, i)),
        out_shape=jax.ShapeDtypeStruct((n_h, 1, s_dim), F32),
        compiler_params=_cparams(("parallel", "parallel"), 8 * _nbytes((t, LANES), F32)),
    )(do, o)


def _pair_attn_bwd(q, kv, c_col, c_row, lse, delta, do, scale, name):
    s_dim, da = q.shape
    n_h = c_col.shape[0]
    dh = da // n_h
    hb = LANES // dh
    n_blocks = da // LANES
    t = _tile(s_dim, 512, LANES)
    nb = s_dim // t

    def body(q_ref, k_ref, v_ref, cq_ref, ck_ref, lse_ref, dl_ref, do_ref,
             dq_ref, dcq_ref, dk_ref, dv_ref, dck_ref, dk_acc, dv_acc, dck_acc):
        j, i = pl.program_id(1), pl.program_id(2)

        @pl.when((j == 0) & (i == 0))
        def _():
            dq_ref[...] = jnp.zeros_like(dq_ref)
            dcq_ref[...] = jnp.zeros_like(dcq_ref)

        @pl.when(i == 0)
        def _():
            dk_acc[...] = jnp.zeros_like(dk_acc)
            dv_acc[...] = jnp.zeros_like(dv_acc)
            dck_acc[...] = jnp.zeros_like(dck_acc)

        def tile(masked):
            start = pl.multiple_of(i * t, t)
            qv, kv_ = q_ref[...], k_ref[...]
            dov = do_ref[...].astype(MXU_DTYPE)
            for hh in range(hb):
                lanes = _head_lanes(hh, dh)
                qm = jnp.where(lanes, qv, jnp.zeros_like(qv))
                km = jnp.where(lanes, kv_, jnp.zeros_like(kv_))
                dom = jnp.where(lanes, dov, jnp.zeros_like(dov))
                st = _dot(kv_, qm, "nt") + (cq_ref[hh] - ck_ref[hh])
                if masked:
                    keep = lax.broadcasted_iota(jnp.int32, (t, t), 0) <= lax.broadcasted_iota(jnp.int32, (t, t), 1)
                    st = jnp.where(keep, st, -jnp.inf)
                pt = jnp.exp(st - lse_ref[hh])
                dst = pt * (_dot(v_ref[...], dom, "nt") - dl_ref[hh])
                dv_acc[...] += _dot(pt, dom, "nn")
                dk_acc[...] += _dot(dst, qm, "nn")
                dq_ref[pl.ds(start, t), :] += _dot(dst, km, "tn") * scale
                dcq_ref[hh, :, pl.ds(start, t)] += jnp.sum(dst, axis=0, keepdims=True)
                dck_acc[hh] -= jnp.sum(dst, axis=1, keepdims=True)

        pl.when(i > j)(functools.partial(tile, False))
        pl.when(i == j)(functools.partial(tile, True))

        @pl.when(i == nb - 1)
        def _():
            dk_ref[...] = dk_acc[...]
            dv_ref[...] = dv_acc[...]
            dck_ref[...] = dck_acc[...]

    q_spec = pl.BlockSpec((t, LANES), lambda b, j, i: (jnp.maximum(i, j), b))
    qrow_spec = pl.BlockSpec((hb, 1, t), lambda b, j, i: (b, 0, jnp.maximum(i, j)))
    k_spec = pl.BlockSpec((t, LANES), lambda b, j, i: (j, b))
    v_spec = pl.BlockSpec((t, LANES), lambda b, j, i: (j, n_blocks + b))
    kcol_spec = pl.BlockSpec((hb, t, 1), lambda b, j, i: (b, j, 0))
    wide = jax.ShapeDtypeStruct((s_dim, da), F32)
    return pl.pallas_call(
        body, name=name, grid=(n_blocks, nb, nb),
        in_specs=[q_spec, k_spec, v_spec, qrow_spec, kcol_spec, qrow_spec, qrow_spec, q_spec],
        out_specs=[pl.BlockSpec((s_dim, LANES), lambda b, j, i: (0, b)),
                   pl.BlockSpec((hb, 1, s_dim), lambda b, j, i: (b, 0, 0)), k_spec, k_spec, kcol_spec],
        out_shape=[wide, jax.ShapeDtypeStruct((n_h, 1, s_dim), F32), wide, wide,
                   jax.ShapeDtypeStruct((n_h, s_dim, 1), F32)],
        scratch_shapes=[pltpu.VMEM((t, LANES), F32), pltpu.VMEM((t, LANES), F32), pltpu.VMEM((hb, t, 1), F32)],
        compiler_params=_cparams(("parallel", "arbitrary", "arbitrary"),
                                 10 * hb * _nbytes((t, t), F32) + 4 * _nbytes((s_dim, LANES), F32)),
    )(q, kv, kv, c_row, c_col, lse, delta, do)


_HBM = pl.BlockSpec(memory_space=pltpu.HBM)
_MESH_ID = pl.DeviceIdType.MESH


def _all_gather(block, name):
    r, w = block.shape

    def body(x_ref, out_ref, send_sems, recv_sems, local_sem):
        x, y, c = lax.axis_index("x"), lax.axis_index("y"), lax.axis_index("c")
        me, sibling = (x, y, c), (x, y, 1 - c)
        chips = [(1 - x, y), (x, 1 - y), (1 - x, 1 - y)]

        def slot(px, py, pc):
            return out_ref.at[4 * px + 2 * py + pc]

        def copy(k, blk, to, src=None):
            return pltpu.make_async_remote_copy(
                src_ref=slot(*blk) if src is None else src, dst_ref=slot(*blk),
                send_sem=send_sems.at[k], recv_sem=recv_sems.at[k], device_id=to, device_id_type=_MESH_ID)

        mine = pltpu.make_async_copy(x_ref, slot(*me), local_sem)
        mine.start()
        first = [copy(0, me, sibling, src=x_ref)]
        first += [copy(1 + n, me, (*chip, c), src=x_ref) for n, chip in enumerate(chips)]
        for cp in first:
            cp.start()
        passed = [copy(4 + n, (*chip, c), sibling) for n, chip in enumerate(chips)]
        for n, chip in enumerate(chips):
            copy(1 + n, (*chip, c), me).wait_recv()
            passed[n].start()
        copy(0, sibling, me).wait_recv()
        for n, chip in enumerate(chips):
            copy(4 + n, (*chip, 1 - c), me).wait_recv()
        for cp in first + passed:
            cp.wait_send()
        mine.wait()

    return pl.pallas_call(
        body, name=name, out_shape=jax.ShapeDtypeStruct((N_DEV, r, w), block.dtype),
        in_specs=[_HBM], out_specs=_HBM,
        scratch_shapes=[pltpu.SemaphoreType.DMA((7,)), pltpu.SemaphoreType.DMA((7,)), pltpu.SemaphoreType.DMA],
    )(block)


_SEM = pl.BlockSpec(memory_space=pltpu.SEMAPHORE)
_EFFECT = pltpu.SideEffectType.DATAFLOW_SIDE_EFFECTING


def _exchange_start(srcs, personalized, after, name):
    n = len(srcs)
    n_after = len(after)
    lands = [lax.empty((N_DEV,) + s.shape[-2:], s.dtype) for s in srcs]

    def body(*refs):
        src_refs, land_refs = refs[:n], refs[n:2 * n]
        outs = refs[2 * n + n_after:]
        send_sems, recv_sems, token = outs[:n], outs[n:2 * n], outs[-1]
        x, y, c = lax.axis_index("x"), lax.axis_index("y"), lax.axis_index("c")
        mine = 4 * x + 2 * y + c
        for ci in range(n):
            for k in range(1, N_DEV):
                px = 1 - x if k & 4 else x
                py = 1 - y if k & 2 else y
                pc = 1 - c if k & 1 else c
                src = src_refs[ci].at[4 * px + 2 * py + pc] if personalized else src_refs[ci]
                pltpu.make_async_remote_copy(
                    src_ref=src, dst_ref=land_refs[ci].at[mine], send_sem=send_sems[ci], recv_sem=recv_sems[ci],
                    device_id=(px, py, pc), device_id_type=_MESH_ID).start()
        token[...] = jnp.zeros_like(token)

    sem = pltpu.SemaphoreType.DMA(())
    out_shape = ([sem] * (2 * n) + [pltpu.HBM(s.shape, s.dtype) for s in srcs]
                 + [pltpu.HBM(l.shape, l.dtype) for l in lands] + [jax.ShapeDtypeStruct((8, LANES), F32)])
    res = pl.pallas_call(
        body, name=name, out_shape=tuple(out_shape),
        in_specs=[_HBM] * (2 * n) + [_ANY] * n_after,
        out_specs=tuple([_SEM] * (2 * n) + [_HBM] * (2 * n) + [pl.BlockSpec(memory_space=pltpu.VMEM)]),
        input_output_aliases={i: 2 * n + i for i in range(2 * n)},
        compiler_params=pltpu.CompilerParams(has_side_effects=_EFFECT),
    )(*[pltpu.with_memory_space_constraint(s, pltpu.HBM) for s in srcs],
      *[pltpu.with_memory_space_constraint(l, pltpu.HBM) for l in lands], *after)
    handles = [(res[ci], res[n + ci], res[2 * n + ci], res[3 * n + ci]) for ci in range(n)]
    return handles, res[-1]


def _exchange_wait(handle, after, name):
    send_sem, recv_sem, src_thru, land_thru = handle

    def body(src_ref, land_ref, send_ref, recv_ref, after_ref, src_out, land_out):
        seven = land_ref.at[pl.ds(0, N_DEV - 1)]
        copies = pltpu.make_async_remote_copy(
            src_ref=seven, dst_ref=seven, send_sem=send_ref, recv_sem=recv_ref,
            device_id=(lax.axis_index("x"), lax.axis_index("y"), lax.axis_index("c")), device_id_type=_MESH_ID)
        copies.wait_send()
        copies.wait_recv()

    return pl.pallas_call(
        body, name=name,
        out_shape=(pltpu.HBM(src_thru.shape, src_thru.dtype), pltpu.HBM(land_thru.shape, land_thru.dtype)),
        in_specs=(_HBM, _HBM, _SEM, _SEM, _ANY), out_specs=(_HBM, _HBM), input_output_aliases={0: 0, 1: 1},
        compiler_params=pltpu.CompilerParams(has_side_effects=_EFFECT),
    )(src_thru, land_thru, send_sem, recv_sem, after)[1]


def _own_slot(land, own, me):
    return lax.dynamic_update_index_in_dim(land, own, me, axis=0)


def _sum_slots(slots, name):
    n, r, w = slots.shape
    tr = _tile(r, 128, WIRE_ROW_ALIGN)

    def body(s_ref, o_ref):
        acc = s_ref[0].astype(F32)
        for d in range(1, n):
            acc = acc + s_ref[d].astype(F32)
        o_ref[...] = acc

    return pl.pallas_call(
        body, name=name, grid=(r // tr,),
        in_specs=[pl.BlockSpec((n, tr, w), lambda i: (0, i, 0))],
        out_specs=pl.BlockSpec((tr, w), lambda i: (i, 0)),
        out_shape=jax.ShapeDtypeStruct((r, w), F32),
        compiler_params=_cparams(("parallel",), 2 * _nbytes((n, tr, w), slots.dtype) + 4 * _nbytes((tr, w), F32)),
    )(slots)


def _adamw(w, g, m, v, name):
    r, c = w.shape
    tr = _tile(r, 512, 8)

    def body(w_ref, g_ref, m_ref, v_ref, d_ref, mo_ref, vo_ref):
        gv = g_ref[...]
        m_new = ADAM_B1 * m_ref[...] + (1.0 - ADAM_B1) * gv
        v_new = ADAM_B2 * v_ref[...] + (1.0 - ADAM_B2) * (gv * gv)
        m_hat = m_new / (1.0 - ADAM_B1 ** ADAM_STEP)
        v_hat = v_new / (1.0 - ADAM_B2 ** ADAM_STEP)
        d_ref[...] = -ADAM_LR * (m_hat / (jnp.sqrt(v_hat) + ADAM_EPS) + ADAM_WD * w_ref[...])
        mo_ref[...] = m_new
        vo_ref[...] = v_new

    blk = pl.BlockSpec((tr, c), lambda i: (i, 0))
    shp = jax.ShapeDtypeStruct((r, c), F32)
    return pl.pallas_call(
        body, name=name, grid=(r // tr,), in_specs=[blk] * 4, out_specs=[blk] * 3, out_shape=[shp] * 3,
        compiler_params=_cparams(("parallel",), 16 * _nbytes((tr, _round_up(c, LANES)), F32)),
    )(w, g, m, v)


def _pack_rows(parts, width, dtype, row_align):
    rows, spans, off = [], [], 0
    for p in parts:
        flat = p.reshape(-1).astype(dtype)
        n_rows = _round_up(-(-flat.shape[0] // width), row_align)
        flat = jnp.pad(flat, (0, n_rows * width - flat.shape[0]))
        rows.append(flat.reshape(n_rows, width))
        spans.append((off, n_rows))
        off += n_rows
    return jnp.concatenate(rows, axis=0), spans


def _unpack_rows(mat, span, shape):
    off, n_rows = span
    n = 1
    for s in shape:
        n *= s
    return mat[..., off:off + n_rows, :].reshape(mat.shape[:-2] + (-1,))[..., :n].reshape(mat.shape[:-2] + tuple(shape))


def _block_diag(w, size):
    n, b, _ = w.shape
    eye = jnp.eye(n, dtype=w.dtype)
    dense = (w[:, :, None, :] * eye[:, None, :, None]).reshape(n * b, n * b)
    return jnp.pad(dense, ((0, size - n * b), (0, size - n * b)))


def _diag_blocks(dense, n, b):
    return jnp.stack([dense[k * b:(k + 1) * b, k * b:(k + 1) * b] for k in range(n)])


def _pad_rows(a, rows):
    return jnp.pad(a, ((0, rows - a.shape[0]), (0, 0)))


def _pad_cols(a, cols):
    return jnp.pad(a, ((0, 0), (0, cols - a.shape[1])))


def _train_step(a):
    x = a["x"][0]
    target = a["loss_target"][0]
    s_dim, d = x.shape
    n_layers = a["ffn1_pre_g"].shape[0]
    f_shard = a["ffn1_w_gate"].shape[2]
    c_shard = a["rg_conv_b"].shape[1]
    c_dim = c_shard * N_DEV
    cp = _round_up(c_dim, LANES)
    conv_width = a["rg_conv_w"].shape[1]
    n_blocks, lru_block = a["rg_w_a"].shape[1], a["rg_w_a"].shape[2]
    d_attn = a["attn_w_q"].shape[2]
    n_heads = a["b_fgate"].shape[0]
    d_head = d_attn // n_heads
    attn_scale = d_head ** -0.5
    assert conv_width < 8 and n_heads <= LANES and n_layers == 2
    assert d_attn == d
    me = 4 * lax.axis_index("x") + 2 * lax.axis_index("y") + lax.axis_index("c")

    shard = {"rg_w_in": a["rg_w_in"][0].T, "rg_w_out": a["rg_w_out"][0], "w_kv": a["w_kv"].T,
             "attn_w_q": a["attn_w_q"][0], "attn_w_o": a["attn_w_o"][0]}
    for l in range(n_layers):
        for f in ("ffn1", "ffn2"):
            shard[(f, "gate", l)] = a[f + "_w_gate"][l].T
            shard[(f, "up", l)] = a[f + "_w_up"][l].T
            shard[(f, "down", l)] = a[f + "_w_down"][l]

    def ffn_names(f, l):
        return [(f, "gate", l), (f, "up", l), (f, "down", l)]

    def chunk_layout(names):
        spans, off = [], 0
        for nm in names:
            spans.append((nm, off, shard[nm].shape[0]))
            off += _round_up(shard[nm].shape[0], WIRE_ROW_ALIGN)
        return spans, off

    def pack_chunk(names, parts):
        return jnp.concatenate(
            [_pad_rows(parts[nm].astype(WIRE_DTYPE), _round_up(parts[nm].shape[0], WIRE_ROW_ALIGN)) for nm in names], axis=0)

    full = {}

    def unpack_chunk(names, gathered):
        for nm, o, n_rows in chunk_layout(names)[0]:
            full[nm] = gathered[:, o:o + n_rows, :].reshape(N_DEV * n_rows, d)

    fwd_chunks = [ffn_names("ffn1", 0), ["rg_w_in", "rg_w_out"], ffn_names("ffn2", 0) + ["w_kv"],
                  ffn_names("ffn1", 1) + ["attn_w_q", "attn_w_o"], ffn_names("ffn2", 1)]
    fwd_packs = [pack_chunk(names, shard) for names in fwd_chunks]
    unpack_chunk(fwd_chunks[0], _all_gather(fwd_packs[0], "gather_weights_first"))

    small_parts = [a["rg_conv_w"][0], a["rg_conv_b"][0], a["rg_b_a"][0], a["rg_b_x"][0], a["rg_lambda"][0], a["w_fgate"]]
    small_pack, small_spans = _pack_rows(small_parts, d, F32, 8)
    small_all = _all_gather(small_pack, "gather_small")
    fwd_handles, fwd_token = _exchange_start(fwd_packs[1:], False, [full[("ffn1", "down", 0)], small_all],
                                             "gather_weights_start")

    def land_weights(n, after):
        land = _exchange_wait(fwd_handles[n - 1], after, f"gather_weights_wait_{n}")
        unpack_chunk(fwd_chunks[n], _own_slot(land, fwd_packs[n], me))

    sm = [_unpack_rows(small_all, sp, p.shape) for sp, p in zip(small_spans, small_parts)]
    conv_w = jnp.moveaxis(sm[0], 0, 1).reshape(conv_width, c_dim)
    conv_b, b_a, b_x, lam = (v.reshape(1, c_dim) for v in sm[1:5])
    w_f = sm[5].reshape(d, n_heads)

    pconv = _pad_rows(_pad_cols(jnp.concatenate([conv_w, conv_b], axis=0), cp), 8)
    pvec = _pad_rows(_pad_cols(jnp.concatenate([b_a, b_x, lam], axis=0), cp), 8)
    wa_dense = _block_diag(a["rg_w_a"][0], cp).astype(MXU_DTYPE)
    wx_dense = _block_diag(a["rg_w_x"][0], cp).astype(MXU_DTYPE)
    wax = jnp.concatenate([wa_dense, wx_dense], axis=1)
    w_f_t = _pad_rows(w_f.T.astype(MXU_DTYPE), LANES)
    b_f = _pad_rows(_pad_cols(a["b_fgate"].reshape(1, n_heads), LANES), 8)

    def gain(name, l):
        return a[name][l].reshape(1, d)

    def ffn_fwd(h, f, l, after=None):
        xn = _rms_fwd(h, gain(f + "_pre_g", l), f"{f}_{l}_pre_norm", after)
        g, u, act = _ffn_up(xn, full[(f, "gate", l)], full[(f, "up", l)], f"{f}_{l}_up")
        fo, h_new = _mm_rms_res(act, full[(f, "down", l)], h, gain(f + "_post_g", l), 0.5, f"{f}_{l}_down")
        return h_new, (h, xn, g, u, act, fo)

    h0 = x
    h0a, sv_f1_0 = ffn_fwd(h0, "ffn1", 0, fwd_token)
    land_weights(1, h0a)
    w_in_gate = _pad_rows(full["rg_w_in"][:c_dim], cp)
    w_in_rec = _pad_rows(full["rg_w_in"][c_dim:], cp)
    w_in_t = jnp.concatenate([w_in_gate, w_in_rec], axis=0)
    w_out = _pad_rows(full["rg_w_out"], cp)
    hn_rg = _rms_fwd(h0a, gain("mix_pre_g", 0), "rg_pre_norm")
    gx = _mm([(hn_rg, w_in_t)], "nt", F32, "rg_in_proj")
    rec = _conv_fwd(gx, pconv, conv_width, "rg_conv")
    gates = _mm([(rec, wax)], "nn", F32, "rg_gate_proj")
    h_rec, y_rg = _scan_fwd(gx, rec, gates, pvec, "rg_scan")
    m_rg, h0b = _mm_rms_res(y_rg, w_out, h0a, gain("mix_post_g", 0), 1.0, "rg_out_proj")
    land_weights(2, h0b)
    h1, sv_f2_0 = ffn_fwd(h0b, "ffn2", 0)
    hn_kv = _rms_fwd(h1, a["kv_norm_g"].reshape(1, d), "kv_norm")
    kv = _mm([(hn_kv, full["w_kv"])], "nt", MXU_DTYPE, "kv_proj")
    fpre = _mm([(hn_kv, w_f_t)], "nt", F32, "fgate_proj")
    c_cum = _fgate_fwd(fpre, b_f, "fgate_cumsum")
    c_heads = c_cum[:, :n_heads].T
    c_col, c_row = c_heads[:, :, None], c_heads[:, None, :]
    land_weights(3, c_cum)
    h1a, sv_f1_1 = ffn_fwd(h1, "ffn1", 1)
    hn_at = _rms_fwd(h1a, gain("mix_pre_g", 1), "attn_pre_norm")
    q_s = _mm([(hn_at, full["attn_w_q"])], "nn", MXU_DTYPE, "q_proj", out_scale=attn_scale)
    o2, lse = _pair_attn_fwd(q_s, kv, kv[:, d_attn:].T, c_col, c_row, "attn_fwd")
    m_at, h1b = _mm_rms_res(o2, full["attn_w_o"], h1a, gain("mix_post_g", 1), 1.0, "attn_out_proj")
    land_weights(4, h1b)
    y, sv_f2_1 = ffn_fwd(h1b, "ffn2", 1)
    dy, loss_part = _loss_head(y, target, "loss_head")

    grads_big = {}
    grads_rep = {}

    bwd_chunks = [ffn_names("ffn2", 1), ["attn_w_q", "attn_w_o"] + ffn_names("ffn1", 1),
                  ["w_kv"] + ffn_names("ffn2", 0), ["rg_w_in", "rg_w_out"], ffn_names("ffn1", 0)]
    bwd_sends, bwd_handles = [], []

    def send_grads(after):
        n = len(bwd_sends)
        send = jnp.concatenate(
            [jnp.pad(grads_big[nm].reshape(N_DEV, n_rows, d), ((0, 0), (0, _round_up(n_rows, WIRE_ROW_ALIGN) - n_rows), (0, 0)))
             for nm, _, n_rows in chunk_layout(bwd_chunks[n])[0]], axis=1)
        handles, token = _exchange_start([send], True, [after], f"exchange_grads_start_{n}")
        bwd_sends.append(send)
        bwd_handles.append(handles[0])
        return token

    def ffn_bwd(dh_out, saved, f, l, after=None, send_now=False):
        h, xn, g, u, act, fo = saved
        df, d_post = _rms_bwd(fo, gain(f + "_post_g", l), [dh_out], None, 0.5, MXU_DTYPE, f"{f}_{l}_post_norm_bwd", after)
        dg, du = _ffn_act_bwd(df, full[(f, "down", l)], g, u, f"{f}_{l}_act_bwd")
        grads_big[(f, "down", l)] = _mm([(act, df)], "tn", WIRE_DTYPE, f"{f}_{l}_dw_down")
        grads_big[(f, "gate", l)] = _mm([(dg, xn)], "tn", WIRE_DTYPE, f"{f}_{l}_dw_gate")
        grads_big[(f, "up", l)] = _mm([(du, xn)], "tn", WIRE_DTYPE, f"{f}_{l}_dw_up")
        sent = send_grads(df) if send_now else None
        dxn = _mm([(dg, full[(f, "gate", l)]), (du, full[(f, "up", l)])], "nn", F32, f"{f}_{l}_dx", sent)
        dh_in, d_pre = _rms_bwd(h, gain(f + "_pre_g", l), [dxn], dh_out, 1.0, F32, f"{f}_{l}_pre_norm_bwd")
        grads_rep[(f + "_post_g", l)] = d_post
        grads_rep[(f + "_pre_g", l)] = d_pre
        return dh_in

    dh = ffn_bwd(dy, sv_f2_1, "ffn2", 1)
    token = send_grads(dh)
    dm, d_post = _rms_bwd(m_at, gain("mix_post_g", 1), [dh], None, 1.0, MXU_DTYPE, "attn_post_norm_bwd", token)
    grads_rep[("mix_post_g", 1)] = d_post
    do2 = _mm([(dm, full["attn_w_o"])], "nt", F32, "attn_out_proj_dx")
    grads_big["attn_w_o"] = _mm([(o2, dm)], "tn", WIRE_DTYPE, "attn_out_proj_dw")
    delta = _attn_delta(do2, o2, n_heads, "attn_delta")
    dq2, dc_q, dk2, dv2, dc_k = _pair_attn_bwd(q_s, kv, c_col, c_row, lse, delta, do2, attn_scale, "attn_bwd")
    dc_heads = dc_q[:, 0, :] + dc_k[:, :, 0]
    dhn = _mm([(dq2, full["attn_w_q"])], "nt", F32, "q_proj_dx")
    grads_big["attn_w_q"] = _mm([(hn_at, dq2)], "tn", WIRE_DTYPE, "q_proj_dw")
    dh, d_pre = _rms_bwd(h1a, gain("mix_pre_g", 1), [dhn], dh, 1.0, F32, "attn_pre_norm_bwd")
    grads_rep[("mix_pre_g", 1)] = d_pre
    dh = ffn_bwd(dh, sv_f1_1, "ffn1", 1)
    token = send_grads(dh)
    dc_cum = _pad_cols(dc_heads.T, LANES)
    dfpre, db_f = _fgate_bwd(dc_cum, fpre, b_f, "fgate_cumsum_bwd")
    dhn_kv = _mm([(dk2, full["w_kv"][:d_attn]), (dv2, full["w_kv"][d_attn:])], "nn", F32, "kv_proj_dx")
    dhn_f = _mm([(dfpre, w_f_t)], "nn", F32, "fgate_proj_dx")
    grads_big["w_kv"] = jnp.concatenate([_mm([(dk2, hn_kv)], "tn", WIRE_DTYPE, "kv_proj_dw_k"),
                                         _mm([(dv2, hn_kv)], "tn", WIRE_DTYPE, "kv_proj_dw_v")], axis=0)
    dw_f_t = _mm([(dfpre, hn_kv)], "tn", F32, "fgate_proj_dw")
    dh, d_kvg = _rms_bwd(h1, a["kv_norm_g"].reshape(1, d), [dhn_kv, dhn_f], dh, 1.0, F32, "kv_norm_bwd", token)
    dh = ffn_bwd(dh, sv_f2_0, "ffn2", 0)
    token = send_grads(dh)
    dm, d_post = _rms_bwd(m_rg, gain("mix_post_g", 0), [dh], None, 1.0, MXU_DTYPE, "rg_post_norm_bwd", token)
    grads_rep[("mix_post_g", 0)] = d_post
    dy_rg = _mm([(dm, w_out)], "nt", F32, "rg_out_proj_dx")
    dw_out = _mm([(y_rg, dm)], "tn", WIRE_DTYPE, "rg_out_proj_dw")
    dgate, dra, dia, drec1, dpvec = _scan_bwd(dy_rg, gx, h_rec, rec, gates, pvec, "rg_scan_bwd")
    drec2 = _mm([(dra, wa_dense), (dia, wx_dense)], "nt", F32, "rg_gate_proj_dx")
    dwa_dense = _mm([(rec, dra)], "tn", F32, "rg_gate_proj_dwa")
    dwx_dense = _mm([(rec, dia)], "tn", F32, "rg_gate_proj_dwx")
    drec0, dpconv = _conv_bwd(drec1, drec2, gx, pconv, conv_width, "rg_conv_bwd")
    dhn = _mm([(dgate, w_in_gate), (drec0, w_in_rec)], "nn", F32, "rg_in_proj_dx")
    dw_in_gate = _mm([(dgate, hn_rg)], "tn", WIRE_DTYPE, "rg_in_proj_dw_gate")
    dw_in_rec = _mm([(drec0, hn_rg)], "tn", WIRE_DTYPE, "rg_in_proj_dw_rec")
    dh, d_pre = _rms_bwd(h0a, gain("mix_pre_g", 0), [dhn], dh, 1.0, F32, "rg_pre_norm_bwd")
    grads_rep[("mix_pre_g", 0)] = d_pre
    grads_big["rg_w_in"] = jnp.concatenate([dw_in_gate[:c_dim], dw_in_rec[:c_dim]], axis=0)
    grads_big["rg_w_out"] = dw_out[:c_dim]
    token = send_grads(dh)
    grad_x = ffn_bwd(dh, sv_f1_0, "ffn1", 0, token, send_now=True)

    g_shard = {}

    def land_grads(n, after):
        land = _exchange_wait(bwd_handles[n], after, f"exchange_grads_wait_{n}")
        own = lax.dynamic_index_in_dim(bwd_sends[n], me, axis=0, keepdims=False)
        g_chunk = _sum_slots(_own_slot(land, own, me), f"sum_weight_grads_{n}")
        for nm, o, n_rows in chunk_layout(bwd_chunks[n])[0]:
            g_shard[nm] = g_chunk[o:o + n_rows]

    for n in range(len(bwd_chunks) - 1):
        land_grads(n, grad_x)

    def gain_grad(name):
        return jnp.concatenate([grads_rep[(name, l)] for l in range(n_layers)], axis=0)

    rep_names = ["ffn1_pre_g", "ffn1_post_g", "mix_pre_g", "mix_post_g", "ffn2_pre_g", "ffn2_post_g"]
    rep_parts = [gain_grad(nm) for nm in rep_names]
    rep_names += ["kv_norm_g", "b_fgate", "rg_w_a", "rg_w_x", "rg_conv_w", "rg_conv_b", "rg_b_a", "rg_b_x", "rg_lambda", "w_fgate"]
    rep_parts += [
        d_kvg, db_f[0, :n_heads],
        _diag_blocks(dwa_dense, n_blocks, lru_block), _diag_blocks(dwx_dense, n_blocks, lru_block),
        dpconv[:conv_width, :c_dim], dpconv[conv_width, :c_dim],
        dpvec[0, :c_dim], dpvec[1, :c_dim], dpvec[2, :c_dim],
        dw_f_t[:n_heads].T]
    rep_pack, rep_spans = _pack_rows(rep_parts, d, F32, WIRE_ROW_ALIGN)
    rep_sum = _sum_slots(_all_gather(rep_pack, "gather_small_grads"), "sum_small_grads")
    g_rep = {nm: _unpack_rows(rep_sum, sp, p.shape) for nm, sp, p in zip(rep_names, rep_spans, rep_parts)}

    def my_cols(full_grad, n):
        return lax.dynamic_slice_in_dim(full_grad, me * n, n, axis=full_grad.ndim - 1)

    def ffn_grads(f):
        grad[f + "_w_gate"] = jnp.stack([g_shard[(f, "gate", l)].T for l in range(n_layers)])
        grad[f + "_w_up"] = jnp.stack([g_shard[(f, "up", l)].T for l in range(n_layers)])
        grad[f + "_w_down"] = jnp.stack([g_shard[(f, "down", l)] for l in range(n_layers)])

    grad = {}
    for nm in ("ffn1_pre_g", "ffn1_post_g", "mix_pre_g", "mix_post_g", "ffn2_pre_g", "ffn2_post_g"):
        grad[nm] = g_rep[nm]
    ffn_grads("ffn2")
    grad["rg_w_in"] = g_shard["rg_w_in"].T[None]
    grad["rg_conv_w"] = my_cols(g_rep["rg_conv_w"], c_shard)[None]
    for nm in ("rg_conv_b", "rg_b_a", "rg_b_x", "rg_lambda"):
        grad[nm] = my_cols(g_rep[nm], c_shard)[None]
    grad["rg_w_a"] = g_rep["rg_w_a"][None]
    grad["rg_w_x"] = g_rep["rg_w_x"][None]
    grad["rg_w_out"] = g_shard["rg_w_out"][None]
    grad["kv_norm_g"] = g_rep["kv_norm_g"].reshape(d)
    grad["w_kv"] = g_shard["w_kv"].T
    grad["w_fgate"] = lax.dynamic_slice_in_dim(g_rep["w_fgate"], me * (d // N_DEV), d // N_DEV, axis=0)
    grad["b_fgate"] = g_rep["b_fgate"]
    grad["attn_w_q"] = g_shard["attn_w_q"][None]
    grad["attn_w_o"] = g_shard["attn_w_o"][None]

    delta, new_m, new_v = {}, {}, {}

    def adamw(nm):
        w = a[nm]
        shape = w.shape
        two_d = (1, shape[0]) if w.ndim == 1 else (-1, shape[-1])
        dl, mo, vo = _adamw(w.reshape(two_d), grad[nm].reshape(two_d), a["m_" + nm].reshape(two_d),
                            a["v_" + nm].reshape(two_d), "adamw_" + nm)
        delta[nm], new_m[nm], new_v[nm] = dl.reshape(shape), mo.reshape(shape), vo.reshape(shape)
        grad[nm] = grad[nm].reshape(shape)

    last_names = ("ffn1_w_gate", "ffn1_w_up", "ffn1_w_down")
    for nm in WEIGHT_NAMES:
        if nm not in last_names:
            adamw(nm)
    land_grads(len(bwd_chunks) - 1, delta["attn_w_o"])
    ffn_grads("ffn1")
    for nm in last_names:
        adamw(nm)

    loss = lax.psum(loss_part[0, 0], AXES)
    return (loss, grad_x[None], *[grad[n] for n in WEIGHT_NAMES], *[delta[n] for n in WEIGHT_NAMES],
            *[new_m[n] for n in WEIGHT_NAMES], *[new_v[n] for n in WEIGHT_NAMES])


def kernel(x, ffn1_pre_g, ffn1_w_gate, ffn1_w_up, ffn1_w_down, ffn1_post_g, mix_pre_g, mix_post_g, ffn2_pre_g, ffn2_w_gate, ffn2_w_up, ffn2_w_down, ffn2_post_g, rg_w_in, rg_conv_w, rg_conv_b, rg_w_a, rg_b_a, rg_w_x, rg_b_x, rg_lambda, rg_w_out, kv_norm_g, w_kv, w_fgate, b_fgate, attn_w_q, attn_w_o, loss_target, m_ffn1_pre_g, m_ffn1_w_gate, m_ffn1_w_up, m_ffn1_w_down, m_ffn1_post_g, m_mix_pre_g, m_mix_post_g, m_ffn2_pre_g, m_ffn2_w_gate, m_ffn2_w_up, m_ffn2_w_down, m_ffn2_post_g, m_rg_w_in, m_rg_conv_w, m_rg_conv_b, m_rg_w_a, m_rg_b_a, m_rg_w_x, m_rg_b_x, m_rg_lambda, m_rg_w_out, m_kv_norm_g, m_w_kv, m_w_fgate, m_b_fgate, m_attn_w_q, m_attn_w_o, v_ffn1_pre_g, v_ffn1_w_gate, v_ffn1_w_up, v_ffn1_w_down, v_ffn1_post_g, v_mix_pre_g, v_mix_post_g, v_ffn2_pre_g, v_ffn2_w_gate, v_ffn2_w_up, v_ffn2_w_down, v_ffn2_post_g, v_rg_w_in, v_rg_conv_w, v_rg_conv_b, v_rg_w_a, v_rg_b_a, v_rg_w_x, v_rg_b_x, v_rg_lambda, v_rg_w_out, v_kv_norm_g, v_w_kv, v_w_fgate, v_b_fgate, v_attn_w_q, v_attn_w_o):
    return _train_step(dict(locals()))
```

```python
import functools

import jax
import jax.numpy as jnp
from jax import lax
from jax.experimental import pallas as pl
from jax.experimental.pallas import tpu as pltpu

F32 = jnp.float32
MXU_DTYPE = jnp.bfloat16
WIRE_DTYPE = jnp.bfloat16
N_DEV = 8
AXES = ("x", "y", "c")
LANES = 128
WIRE_ROW_ALIGN = 16
VMEM_LIMIT_MIN = 32 * 2 ** 20
VMEM_LIMIT_MAX = 56 * 2 ** 20

RMS_EPS = 1e-6
LRU_C = 8.0
ADAM_LR, ADAM_B1, ADAM_B2, ADAM_EPS, ADAM_WD, ADAM_STEP = 0.001, 0.9, 0.999, 1e-08, 0.01, 10

WEIGHT_NAMES = (
    "ffn1_pre_g", "ffn1_w_gate", "ffn1_w_up", "ffn1_w_down", "ffn1_post_g", "mix_pre_g", "mix_post_g",
    "ffn2_pre_g", "ffn2_w_gate", "ffn2_w_up", "ffn2_w_down", "ffn2_post_g", "rg_w_in", "rg_conv_w",
    "rg_conv_b", "rg_w_a", "rg_b_a", "rg_w_x", "rg_b_x", "rg_lambda", "rg_w_out", "kv_norm_g", "w_kv",
    "w_fgate", "b_fgate", "attn_w_q", "attn_w_o")


def _round_up(n, m):
    return (n + m - 1) // m * m


def _tile(dim, target, align=LANES):
    if dim <= target:
        return dim
    best = None
    t = align
    while t <= target:
        if dim % t == 0:
            best = t
        t += align
    return dim if best is None else best


def _cparams(semantics, vmem_estimate):
    limit = min(VMEM_LIMIT_MAX, max(VMEM_LIMIT_MIN, 2 * int(vmem_estimate)))
    return pltpu.CompilerParams(dimension_semantics=semantics, vmem_limit_bytes=limit)


def _nbytes(shape, dtype):
    n = 1
    for s in shape:
        n *= s
    return n * jnp.dtype(dtype).itemsize


def _sigmoid(x):
    return jax.nn.sigmoid(x)


def _softplus(x):
    return jnp.maximum(x, 0.0) + jnp.log1p(jnp.exp(-jnp.abs(x)))


def _expm1(x):
    series = x * (1.0 + x * (0.5 + x * (1.0 / 6.0 + x * (1.0 / 24.0 + x * (1.0 / 120.0)))))
    return jnp.where(jnp.abs(x) < 0.25, series, jnp.exp(x) - 1.0)


_GELU_C = 0.7978845608028654
_GELU_A = 0.044715


def _gelu(x):
    return 0.5 * x * (1.0 + jnp.tanh(_GELU_C * (x + _GELU_A * x * x * x)))


def _gelu_grad(x):
    t = jnp.tanh(_GELU_C * (x + _GELU_A * x * x * x))
    return 0.5 * (1.0 + t) + 0.5 * x * (1.0 - t * t) * _GELU_C * (1.0 + 3.0 * _GELU_A * x * x)


_DOT_DIMS = {"nn": ((1,), (0,)), "nt": ((1,), (1,)), "tn": ((0,), (0,))}


def _dot(a, b, mode):
    return lax.dot_general(a.astype(MXU_DTYPE), b.astype(MXU_DTYPE), (_DOT_DIMS[mode], ((), ())),
                           preferred_element_type=F32)


def _mm(pairs, mode, out_dtype, name, after=None, out_scale=None):
    a0, b0 = pairs[0]
    if mode == "tn":
        k_dim, m_dim = a0.shape
        n_dim = b0.shape[1]
    else:
        m_dim, k_dim = a0.shape
        n_dim = b0.shape[0] if mode == "nt" else b0.shape[1]
    for a, b in pairs:
        assert a.shape == a0.shape and b.shape == b0.shape
    tm = _tile(m_dim, 1408 if mode == "tn" else 512)
    tn = _tile(n_dim, 1408)
    tk = _tile(k_dim, 1408)
    nk = k_dim // tk
    n_pairs = len(pairs)

    if mode == "tn":
        a_spec = pl.BlockSpec((tk, tm), lambda i, j, k: (k, i))
    else:
        a_spec = pl.BlockSpec((tm, tk), lambda i, j, k: (i, k))
    if mode == "nt":
        b_spec = pl.BlockSpec((tn, tk), lambda i, j, k: (j, k))
    else:
        b_spec = pl.BlockSpec((tk, tn), lambda i, j, k: (k, j))

    order = [] if after is None else [after]

    def body(*refs):
        ins, o_ref, acc = refs[:2 * n_pairs], refs[-2], refs[-1]
        k = pl.program_id(2)

        @pl.when(k == 0)
        def _():
            acc[...] = jnp.zeros_like(acc)

        s = acc[...]
        for p in range(n_pairs):
            s = s + _dot(ins[2 * p][...], ins[2 * p + 1][...], mode)
        acc[...] = s

        @pl.when(k == nk - 1)
        def _():
            r = acc[...] if out_scale is None else acc[...] * out_scale
            o_ref[...] = r.astype(out_dtype)

    est = (2 * n_pairs * (_nbytes((tm, tk), a0.dtype) + _nbytes((tk, tn), b0.dtype))
           + 2 * _nbytes((tm, tn), out_dtype) + 2 * _nbytes((tm, tn), F32))
    flat = [t for ab in pairs for t in ab]
    return pl.pallas_call(
        body, name=name, grid=(m_dim // tm, n_dim // tn, nk),
        in_specs=[a_spec, b_spec] * n_pairs + [_ANY] * len(order),
        out_specs=pl.BlockSpec((tm, tn), lambda i, j, k: (i, j)),
        out_shape=jax.ShapeDtypeStruct((m_dim, n_dim), out_dtype),
        scratch_shapes=[pltpu.VMEM((tm, tn), F32)],
        compiler_params=_cparams(("parallel", "parallel", "arbitrary"), est),
    )(*flat, *order)


_ANY = pl.BlockSpec(memory_space=pl.ANY)


def _rms_fwd(x, gain, name, after=None):
    s_dim, d = x.shape
    tm = _tile(s_dim, 512, 8)

    def body(*refs):
        x_ref, g_ref, o_ref = refs[0], refs[1], refs[-1]
        v = x_ref[...]
        r = lax.rsqrt(jnp.mean(v * v, axis=-1, keepdims=True) + RMS_EPS)
        o_ref[...] = (v * r * g_ref[...]).astype(MXU_DTYPE)

    order = [] if after is None else [after]
    return pl.pallas_call(
        body, name=name, grid=(s_dim // tm,),
        in_specs=[pl.BlockSpec((tm, d), lambda i: (i, 0)), pl.BlockSpec((1, d), lambda i: (0, 0))] + [_ANY] * len(order),
        out_specs=pl.BlockSpec((tm, d), lambda i: (i, 0)),
        out_shape=jax.ShapeDtypeStruct((s_dim, d), MXU_DTYPE),
        compiler_params=_cparams(("parallel",), 6 * _nbytes((tm, d), F32)),
    )(x, gain, *order)


def _rms_bwd(x, gain, dys, res, scale, out_dtype, name, after=None):
    s_dim, d = x.shape
    tm = _tile(s_dim, 512, 8)
    n_dy = len(dys)
    has_res = res is not None
    order = [] if after is None else [after]

    def body(*refs):
        x_ref, g_ref = refs[0], refs[1]
        dy_refs = refs[2:2 + n_dy]
        res_ref = refs[2 + n_dy] if has_res else None
        dx_ref, dg_ref = refs[-2], refs[-1]

        @pl.when(pl.program_id(0) == 0)
        def _():
            dg_ref[...] = jnp.zeros_like(dg_ref)

        v = x_ref[...]
        r = lax.rsqrt(jnp.mean(v * v, axis=-1, keepdims=True) + RMS_EPS)
        xh = v * r
        dy = dy_refs[0][...].astype(F32)
        for extra in dy_refs[1:]:
            dy = dy + extra[...].astype(F32)
        gd = dy * g_ref[...]
        dx = scale * r * (gd - xh * jnp.mean(gd * xh, axis=-1, keepdims=True))
        if has_res:
            dx = dx + res_ref[...]
        dx_ref[...] = dx.astype(out_dtype)
        dg_ref[...] += scale * jnp.sum(dy * xh, axis=0, keepdims=True)

    row = pl.BlockSpec((tm, d), lambda i: (i, 0))
    vec = pl.BlockSpec((1, d), lambda i: (0, 0))
    ops = [x, gain] + list(dys) + ([res] if has_res else [])
    return pl.pallas_call(
        body, name=name, grid=(s_dim // tm,),
        in_specs=[row, vec] + [row] * (n_dy + int(has_res)) + [_ANY] * len(order),
        out_specs=[row, vec],
        out_shape=[jax.ShapeDtypeStruct((s_dim, d), out_dtype), jax.ShapeDtypeStruct((1, d), F32)],
        compiler_params=_cparams(("arbitrary",), (2 * len(ops) + 6) * _nbytes((tm, d), F32)),
    )(*ops, *order)


def _mm_rms_res(a, b, h, gain, scale, name):
    s_dim, k_dim = a.shape
    d = b.shape[1]
    tm = _tile(s_dim, 512, 8)
    tk = _tile(k_dim, 1408)
    nk = k_dim // tk

    def body(a_ref, b_ref, h_ref, g_ref, f_ref, o_ref, acc):
        k = pl.program_id(1)

        @pl.when(k == 0)
        def _():
            acc[...] = jnp.zeros_like(acc)

        acc[...] += _dot(a_ref[...], b_ref[...], "nn")

        @pl.when(k == nk - 1)
        def _():
            f = acc[...]
            r = lax.rsqrt(jnp.mean(f * f, axis=-1, keepdims=True) + RMS_EPS)
            f_ref[...] = f
            o_ref[...] = h_ref[...] + scale * (f * r * g_ref[...])

    row = pl.BlockSpec((tm, d), lambda i, k: (i, 0))
    est = (2 * (_nbytes((tm, tk), a.dtype) + _nbytes((tk, d), b.dtype)) + 8 * _nbytes((tm, d), F32))
    return pl.pallas_call(
        body, name=name, grid=(s_dim // tm, nk),
        in_specs=[pl.BlockSpec((tm, tk), lambda i, k: (i, k)), pl.BlockSpec((tk, d), lambda i, k: (k, 0)),
                  row, pl.BlockSpec((1, d), lambda i, k: (0, 0))],
        out_specs=[row, row],
        out_shape=[jax.ShapeDtypeStruct((s_dim, d), F32), jax.ShapeDtypeStruct((s_dim, d), F32)],
        scratch_shapes=[pltpu.VMEM((tm, d), F32)],
        compiler_params=_cparams(("parallel", "arbitrary"), est),
    )(a, b, h, gain)


def _ffn_up(xn, wg_t, wu_t, name):
    s_dim, d = xn.shape
    f_dim = wg_t.shape[0]
    tm = _tile(s_dim, 1024, 8)
    tf = _tile(f_dim, 256)

    def body(x_ref, wg_ref, wu_ref, g_ref, u_ref, a_ref):
        x = x_ref[...]
        g = _dot(x, wg_ref[...], "nt")
        u = _dot(x, wu_ref[...], "nt")
        g_ref[...] = g.astype(MXU_DTYPE)
        u_ref[...] = u.astype(MXU_DTYPE)
        a_ref[...] = (g * _sigmoid(g) * u).astype(MXU_DTYPE)

    w_spec = pl.BlockSpec((tf, d), lambda i, j: (j, 0))
    o_spec = pl.BlockSpec((tm, tf), lambda i, j: (i, j))
    o_shape = jax.ShapeDtypeStruct((s_dim, f_dim), MXU_DTYPE)
    est = 2 * _nbytes((tm, d), xn.dtype) + 4 * _nbytes((tf, d), wg_t.dtype) + 10 * _nbytes((tm, tf), F32)
    return pl.pallas_call(
        body, name=name, grid=(s_dim // tm, f_dim // tf),
        in_specs=[pl.BlockSpec((tm, d), lambda i, j: (i, 0)), w_spec, w_spec],
        out_specs=[o_spec, o_spec, o_spec], out_shape=[o_shape, o_shape, o_shape],
        compiler_params=_cparams(("parallel", "parallel"), est),
    )(xn, wg_t, wu_t)


def _ffn_act_bwd(df, wd, g, u, name):
    s_dim, d = df.shape
    f_dim = wd.shape[0]
    tm = _tile(s_dim, 1024, 8)
    tf = _tile(f_dim, 256)

    def body(df_ref, wd_ref, g_ref, u_ref, dg_ref, du_ref):
        dh = _dot(df_ref[...], wd_ref[...], "nt")
        gv = g_ref[...].astype(F32)
        uv = u_ref[...].astype(F32)
        sg = _sigmoid(gv)
        dg_ref[...] = (dh * uv * (sg * (1.0 + gv * (1.0 - sg)))).astype(MXU_DTYPE)
        du_ref[...] = (dh * gv * sg).astype(MXU_DTYPE)

    t_spec = pl.BlockSpec((tm, tf), lambda i, j: (i, j))
    o_shape = jax.ShapeDtypeStruct((s_dim, f_dim), MXU_DTYPE)
    est = 2 * _nbytes((tm, d), df.dtype) + 2 * _nbytes((tf, d), wd.dtype) + 12 * _nbytes((tm, tf), F32)
    return pl.pallas_call(
        body, name=name, grid=(s_dim // tm, f_dim // tf),
        in_specs=[pl.BlockSpec((tm, d), lambda i, j: (i, 0)), pl.BlockSpec((tf, d), lambda i, j: (j, 0)),
                  t_spec, t_spec],
        out_specs=[t_spec, t_spec], out_shape=[o_shape, o_shape],
        compiler_params=_cparams(("parallel", "parallel"), est),
    )(df, wd, g, u)


def _loss_head(y, target, name):
    s_dim, d = y.shape
    tm = _tile(s_dim, 512, 8)
    nt = s_dim // tm

    def body(y_ref, t_ref, dy_ref, loss_ref, acc):
        i = pl.program_id(0)

        @pl.when(i == 0)
        def _():
            acc[...] = jnp.zeros_like(acc)

        e = y_ref[...] - t_ref[...]
        dy_ref[...] = e * (1.0 / d)
        acc[...] += jnp.sum(e * e, axis=0, keepdims=True)

        @pl.when(i == nt - 1)
        def _():
            loss_ref[...] = jnp.sum(acc[...], axis=1, keepdims=True) * (0.5 / d)

    row = pl.BlockSpec((tm, d), lambda i: (i, 0))
    return pl.pallas_call(
        body, name=name, grid=(nt,), in_specs=[row, row],
        out_specs=[row, pl.BlockSpec((1, 1), lambda i: (0, 0))],
        out_shape=[jax.ShapeDtypeStruct((s_dim, d), F32), jax.ShapeDtypeStruct((1, 1), F32)],
        scratch_shapes=[pltpu.VMEM((1, d), F32)],
        compiler_params=_cparams(("arbitrary",), 8 * _nbytes((tm, d), F32)),
    )(y, target)


def _shift_down(v, sh, row):
    if sh == 0:
        return v
    return jnp.where(row >= sh, pltpu.roll(v, sh, 0), 0.0)


def _shift_up(v, sh, row):
    if sh == 0:
        return v
    n = v.shape[0]
    return jnp.where(row < n - sh, pltpu.roll(v, n - sh, 0), 0.0)


def _conv_fwd(gx, pconv, width, name):
    s_dim, cp2 = gx.shape
    cp = cp2 // 2
    nc = cp // LANES

    def body(x_ref, p_ref, o_ref):
        x = x_ref[...]
        row = lax.broadcasted_iota(jnp.int32, x.shape, 0)
        y = jnp.zeros_like(x) + p_ref[pl.ds(width, 1), :]
        for k in range(width):
            y = y + p_ref[pl.ds(k, 1), :] * _shift_down(x, width - 1 - k, row)
        o_ref[...] = y

    return pl.pallas_call(
        body, name=name, grid=(nc,),
        in_specs=[pl.BlockSpec((s_dim, LANES), lambda j: (0, nc + j)), pl.BlockSpec((8, LANES), lambda j: (0, j))],
        out_specs=pl.BlockSpec((s_dim, LANES), lambda j: (0, j)),
        out_shape=jax.ShapeDtypeStruct((s_dim, cp), F32),
        compiler_params=_cparams(("parallel",), 10 * _nbytes((s_dim, LANES), F32)),
    )(gx, pconv)


def _conv_bwd(d1, d2, gx, pconv, width, name):
    s_dim, cp = d1.shape
    nc = cp // LANES

    def body(d1_ref, d2_ref, x_ref, p_ref, dx_ref, dp_ref):
        d = d1_ref[...] + d2_ref[...]
        x = x_ref[...]
        row = lax.broadcasted_iota(jnp.int32, x.shape, 0)
        dx = jnp.zeros_like(d)
        dp_ref[...] = jnp.zeros_like(dp_ref)
        for k in range(width):
            sh = width - 1 - k
            dx = dx + p_ref[pl.ds(k, 1), :] * _shift_up(d, sh, row)
            dp_ref[pl.ds(k, 1), :] = jnp.sum(d * _shift_down(x, sh, row), axis=0, keepdims=True)
        dp_ref[pl.ds(width, 1), :] = jnp.sum(d, axis=0, keepdims=True)
        dx_ref[...] = dx.astype(MXU_DTYPE)

    strip = pl.BlockSpec((s_dim, LANES), lambda j: (0, j))
    par = pl.BlockSpec((8, LANES), lambda j: (0, j))
    return pl.pallas_call(
        body, name=name, grid=(nc,),
        in_specs=[strip, strip, pl.BlockSpec((s_dim, LANES), lambda j: (0, nc + j)), par],
        out_specs=[strip, par],
        out_shape=[jax.ShapeDtypeStruct((s_dim, cp), MXU_DTYPE), jax.ShapeDtypeStruct((8, cp), F32)],
        compiler_params=_cparams(("parallel",), 14 * _nbytes((s_dim, LANES), F32)),
    )(d1, d2, gx, pconv)


def _lru_coeffs(ra, ia, p_ref):
    r = _sigmoid(ra + p_ref[pl.ds(0, 1), :])
    i = _sigmoid(ia + p_ref[pl.ds(1, 1), :])
    sp = _softplus(-p_ref[pl.ds(2, 1), :])
    log_a = -LRU_C * r * sp
    a = jnp.exp(log_a)
    mult = jnp.sqrt(-_expm1(2.0 * log_a))
    return r, i, sp, a, mult


def _scan_fwd(gx, rec, gates, pvec, name):
    s_dim, cp = rec.shape
    ts = _tile(s_dim, 256, 8)
    nt = s_dim // ts

    def body(gate_ref, rec_ref, ra_ref, ia_ref, p_ref, h_ref, y_ref, a_s, u_s, carry):
        @pl.when(pl.program_id(0) == 0)
        def _():
            carry[...] = jnp.zeros_like(carry)

        rec_v = rec_ref[...]
        _, i, _, a, mult = _lru_coeffs(ra_ref[...], ia_ref[...], p_ref)
        a_s[...] = a
        u_s[...] = mult * (i * rec_v)

        def step(t, h):
            h = a_s[pl.ds(t, 1), :] * h + u_s[pl.ds(t, 1), :]
            h_ref[pl.ds(t, 1), :] = h
            return h

        carry[pl.ds(0, 1), :] = lax.fori_loop(0, ts, step, carry[pl.ds(0, 1), :], unroll=8)
        y_ref[...] = (_gelu(gate_ref[...]) * h_ref[...]).astype(MXU_DTYPE)

    blk = pl.BlockSpec((ts, cp), lambda t: (t, 0))
    return pl.pallas_call(
        body, name=name, grid=(nt,),
        in_specs=[blk, blk, blk, pl.BlockSpec((ts, cp), lambda t: (t, 1)), pl.BlockSpec((8, cp), lambda t: (0, 0))],
        out_specs=[blk, blk],
        out_shape=[jax.ShapeDtypeStruct((s_dim, cp), F32), jax.ShapeDtypeStruct((s_dim, cp), MXU_DTYPE)],
        scratch_shapes=[pltpu.VMEM((ts, cp), F32), pltpu.VMEM((ts, cp), F32), pltpu.VMEM((8, cp), F32)],
        compiler_params=_cparams(("arbitrary",), 14 * _nbytes((ts, cp), F32)),
    )(gx, rec, gates, gates, pvec)


def _scan_bwd(dy, gx, hrec, rec, gates, pvec, name):
    s_dim, cp = rec.shape
    ts = _tile(s_dim, 128, 8)
    nt = s_dim // ts

    def body(dy_ref, gate_ref, h_ref, hp_ref, rec_ref, ra_ref, ia_ref, p_ref,
             dgate_ref, dra_ref, dia_ref, drec_ref, dp_ref, a_s, d_s, carry):
        t_id = pl.program_id(0)

        @pl.when(t_id == 0)
        def _():
            carry[...] = jnp.zeros_like(carry)
            dp_ref[...] = jnp.zeros_like(dp_ref)

        rec_v = rec_ref[...]
        r, i, sp, a, mult = _lru_coeffs(ra_ref[...], ia_ref[...], p_ref)
        gate = gate_ref[...]
        dyv = dy_ref[...]
        h = h_ref[...]
        dgate_ref[...] = (dyv * h * _gelu_grad(gate)).astype(MXU_DTYPE)
        a_s[...] = a
        d_s[...] = dyv * _gelu(gate)

        def step(k, c):
            t = ts - 1 - k
            d = d_s[pl.ds(t, 1), :] + c
            d_s[pl.ds(t, 1), :] = d
            return a_s[pl.ds(t, 1), :] * d

        carry[pl.ds(0, 1), :] = lax.fori_loop(0, ts, step, carry[pl.ds(0, 1), :], unroll=8)
        dh = d_s[...]
        row = lax.broadcasted_iota(jnp.int32, h.shape, 0)
        first = jnp.where(t_id == nt - 1, 0.0, 1.0) * hp_ref[pl.ds(7, 1), :]
        h_prev = jnp.where(row == 0, first, pltpu.roll(h, 1, 0))
        dix = dh * mult
        dla = dh * h_prev * a - dh * (i * rec_v) * (a * a) / mult
        dra = dla * (-LRU_C * sp) * r * (1.0 - r)
        dia = dix * rec_v * i * (1.0 - i)
        dra_ref[...] = dra.astype(MXU_DTYPE)
        dia_ref[...] = dia.astype(MXU_DTYPE)
        drec_ref[...] = dix * i
        dsp = jnp.sum(dla * (-LRU_C * r), axis=0, keepdims=True)
        dp_ref[pl.ds(0, 1), :] += jnp.sum(dra, axis=0, keepdims=True)
        dp_ref[pl.ds(1, 1), :] += jnp.sum(dia, axis=0, keepdims=True)
        dp_ref[pl.ds(2, 1), :] += dsp * (-_sigmoid(-p_ref[pl.ds(2, 1), :]))

    blk = pl.BlockSpec((ts, cp), lambda t: (nt - 1 - t, 0))
    prev = pl.BlockSpec((8, cp), lambda t: (jnp.maximum((nt - 1 - t) * (ts // 8) - 1, 0), 0))
    par = pl.BlockSpec((8, cp), lambda t: (0, 0))
    lo = jax.ShapeDtypeStruct((s_dim, cp), MXU_DTYPE)
    return pl.pallas_call(
        body, name=name, grid=(nt,),
        in_specs=[blk, blk, blk, prev, blk, blk, pl.BlockSpec((ts, cp), lambda t: (nt - 1 - t, 1)), par],
        out_specs=[blk, blk, blk, blk, par],
        out_shape=[lo, lo, lo, jax.ShapeDtypeStruct((s_dim, cp), F32), jax.ShapeDtypeStruct((8, cp), F32)],
        scratch_shapes=[pltpu.VMEM((ts, cp), F32), pltpu.VMEM((ts, cp), F32), pltpu.VMEM((8, cp), F32)],
        compiler_params=_cparams(("arbitrary",), 40 * _nbytes((ts, cp), F32)),
    )(dy, gx, hrec, hrec, rec, gates, gates, pvec)


def _fgate_fwd(fpre, bias, name):
    s_dim, w = fpre.shape
    ts = _tile(s_dim, 512, 8)

    def body(f_ref, b_ref, c_ref, lf_s, carry):
        @pl.when(pl.program_id(0) == 0)
        def _():
            carry[...] = jnp.zeros_like(carry)

        lf_s[...] = -_softplus(-(f_ref[...] + b_ref[pl.ds(0, 1), :]))

        def step(t, c):
            c = c + lf_s[pl.ds(t, 1), :]
            c_ref[pl.ds(t, 1), :] = c
            return c

        carry[pl.ds(0, 1), :] = lax.fori_loop(0, ts, step, carry[pl.ds(0, 1), :], unroll=8)

    blk = pl.BlockSpec((ts, w), lambda t: (t, 0))
    return pl.pallas_call(
        body, name=name, grid=(s_dim // ts,),
        in_specs=[blk, pl.BlockSpec((8, w), lambda t: (0, 0))], out_specs=blk,
        out_shape=jax.ShapeDtypeStruct((s_dim, w), F32),
        scratch_shapes=[pltpu.VMEM((ts, w), F32), pltpu.VMEM((8, w), F32)],
        compiler_params=_cparams(("arbitrary",), 12 * _nbytes((ts, w), F32)),
    )(fpre, bias)


def _fgate_bwd(dc, fpre, bias, name):
    s_dim, w = fpre.shape
    ts = _tile(s_dim, 512, 8)
    nt = s_dim // ts

    def body(dc_ref, f_ref, b_ref, df_ref, db_ref, d_s, carry):
        @pl.when(pl.program_id(0) == 0)
        def _():
            carry[...] = jnp.zeros_like(carry)
            db_ref[...] = jnp.zeros_like(db_ref)

        d_s[...] = dc_ref[...]

        def step(k, c):
            t = ts - 1 - k
            c = c + d_s[pl.ds(t, 1), :]
            d_s[pl.ds(t, 1), :] = c
            return c

        carry[pl.ds(0, 1), :] = lax.fori_loop(0, ts, step, carry[pl.ds(0, 1), :], unroll=8)
        df = d_s[...] * _sigmoid(-(f_ref[...] + b_ref[pl.ds(0, 1), :]))
        df_ref[...] = df
        db_ref[pl.ds(0, 1), :] += jnp.sum(df, axis=0, keepdims=True)

    blk = pl.BlockSpec((ts, w), lambda t: (nt - 1 - t, 0))
    par = pl.BlockSpec((8, w), lambda t: (0, 0))
    return pl.pallas_call(
        body, name=name, grid=(nt,), in_specs=[blk, blk, par], out_specs=[blk, par],
        out_shape=[jax.ShapeDtypeStruct((s_dim, w), F32), jax.ShapeDtypeStruct((8, w), F32)],
        scratch_shapes=[pltpu.VMEM((ts, w), F32), pltpu.VMEM((8, w), F32)],
        compiler_params=_cparams(("arbitrary",), 12 * _nbytes((ts, w), F32)),
    )(dc, fpre, bias)


def _head_lanes(hh, dh):
    lane = lax.broadcasted_iota(jnp.int32, (1, LANES), 1)
    return (lane >= hh * dh) & (lane < (hh + 1) * dh)


def _pair_attn_fwd(q, kv, v_t, c_col, c_row, name):
    s_dim, da = q.shape
    n_h = c_col.shape[0]
    dh = da // n_h
    assert LANES % dh == 0 and da % LANES == 0
    hb = LANES // dh
    n_blocks = da // LANES
    t = _tile(s_dim, 512, LANES)
    nb = s_dim // t

    pairs = [(i, j) for i in range(nb) for j in range(i + 1)]
    i_tab = jnp.asarray([p[0] for p in pairs], jnp.int32)
    j_tab = jnp.asarray([p[1] for p in pairs], jnp.int32)

    def body(i_ref, j_ref, q_ref, k_ref, vt_ref, cq_ref, ck_ref, o_ref, lse_ref, m_s, l_s, acc):
        i, j = i_ref[pl.program_id(1)], j_ref[pl.program_id(1)]

        @pl.when(j == 0)
        def _():
            m_s[...] = jnp.full_like(m_s, -jnp.inf)
            l_s[...] = jnp.zeros_like(l_s)
            acc[...] = jnp.zeros_like(acc)

        def tile(masked):
            qv = q_ref[...]
            for hh in range(hb):
                st = _dot(k_ref[...], jnp.where(_head_lanes(hh, dh), qv, jnp.zeros_like(qv)), "nt")
                st = st + (cq_ref[hh] - ck_ref[hh])
                if masked:
                    keep = lax.broadcasted_iota(jnp.int32, (t, t), 0) <= lax.broadcasted_iota(jnp.int32, (t, t), 1)
                    st = jnp.where(keep, st, -jnp.inf)
                m_prev = m_s[hh]
                m_new = jnp.maximum(m_prev, jnp.max(st, axis=0, keepdims=True))
                alpha = jnp.exp(m_prev - m_new)
                p = jnp.exp(st - m_new)
                l_s[hh] = alpha * l_s[hh] + jnp.sum(p, axis=0, keepdims=True)
                acc[hh] = alpha * acc[hh] + _dot(vt_ref[...], p, "nn")
                m_s[hh] = m_new

        pl.when(j < i)(functools.partial(tile, False))
        pl.when(j == i)(functools.partial(tile, True))

        @pl.when(j == i)
        def _():
            feat = lax.broadcasted_iota(jnp.int32, (LANES, 1), 0)
            out_t = jnp.zeros((LANES, t), F32)
            for hh in range(hb):
                out_t = jnp.where((feat >= hh * dh) & (feat < (hh + 1) * dh), acc[hh] / l_s[hh], out_t)
                lse_ref[hh] = m_s[hh] + jnp.log(l_s[hh])
            o_ref[...] = out_t.T

    q_spec = pl.BlockSpec((t, LANES), lambda b, p, it, jt: (it[p], b))
    k_spec = pl.BlockSpec((t, LANES), lambda b, p, it, jt: (jt[p], b))
    vt_spec = pl.BlockSpec((LANES, t), lambda b, p, it, jt: (b, jt[p]))
    cq_spec = pl.BlockSpec((hb, 1, t), lambda b, p, it, jt: (b, 0, it[p]))
    ck_spec = pl.BlockSpec((hb, t, 1), lambda b, p, it, jt: (b, jt[p], 0))
    return pl.pallas_call(
        body, name=name,
        grid_spec=pltpu.PrefetchScalarGridSpec(
            num_scalar_prefetch=2, grid=(n_blocks, len(pairs)),
            in_specs=[q_spec, k_spec, vt_spec, cq_spec, ck_spec], out_specs=[q_spec, cq_spec],
            scratch_shapes=[pltpu.VMEM((hb, 1, t), F32), pltpu.VMEM((hb, 1, t), F32), pltpu.VMEM((hb, LANES, t), F32)]),
        out_shape=[jax.ShapeDtypeStruct((s_dim, da), F32), jax.ShapeDtypeStruct((n_h, 1, s_dim), F32)],
        compiler_params=_cparams(("parallel", "arbitrary"), 10 * hb * _nbytes((t, t), F32)),
    )(i_tab, j_tab, q, kv, v_t, c_row, c_col)


def _attn_delta(do, o, n_h, name):
    s_dim, da = o.shape
    dh = da // n_h
    hb = LANES // dh
    t = _tile(s_dim, 512, LANES)

    def body(do_ref, o_ref, d_ref):
        prod_t = (do_ref[...].astype(MXU_DTYPE).astype(F32) * o_ref[...]).T
        for hh in range(hb):
            d_ref[hh] = jnp.sum(prod_t[hh * dh:(hh + 1) * dh], axis=0, keepdims=True)

    blk = pl.BlockSpec((t, LANES), lambda b, i: (i, b))
    return pl.pallas_call(
        body, name=name, grid=(da // LANES, s_dim // t), in_specs=[blk, blk],
        out_specs=pl.BlockSpec((hb, 1, t), lambda b, i: (b, 0, i)),
        out_shape=jax.ShapeDtypeStruct((n_h, 1, s_dim), F32),
        compiler_params=_cparams(("parallel", "parallel"), 8 * _nbytes((t, LANES), F32)),
    )(do, o)


def _pair_attn_bwd(q, kv, c_col, c_row, lse, delta, do, scale, name):
    s_dim, da = q.shape
    n_h = c_col.shape[0]
    dh = da // n_h
    hb = LANES // dh
    n_blocks = da // LANES
    t = _tile(s_dim, 512, LANES)
    nb = s_dim // t

    pairs = [(i, j) for j in range(nb) for i in range(j, nb)]
    i_tab = jnp.asarray([p[0] for p in pairs], jnp.int32)
    j_tab = jnp.asarray([p[1] for p in pairs], jnp.int32)

    def body(i_ref, j_ref, q_ref, k_ref, v_ref, cq_ref, ck_ref, lse_ref, dl_ref, do_ref,
             dq_ref, dcq_ref, dk_ref, dv_ref, dck_ref, dk_acc, dv_acc, dck_acc):
        i, j = i_ref[pl.program_id(1)], j_ref[pl.program_id(1)]

        @pl.when(pl.program_id(1) == 0)
        def _():
            dq_ref[...] = jnp.zeros_like(dq_ref)
            dcq_ref[...] = jnp.zeros_like(dcq_ref)

        @pl.when(i == j)
        def _():
            dk_acc[...] = jnp.zeros_like(dk_acc)
            dv_acc[...] = jnp.zeros_like(dv_acc)
            dck_acc[...] = jnp.zeros_like(dck_acc)

        def tile(masked):
            start = pl.multiple_of(i * t, t)
            qv, kv_ = q_ref[...], k_ref[...]
            dov = do_ref[...].astype(MXU_DTYPE)
            for hh in range(hb):
                lanes = _head_lanes(hh, dh)
                qm = jnp.where(lanes, qv, jnp.zeros_like(qv))
                km = jnp.where(lanes, kv_, jnp.zeros_like(kv_))
                dom = jnp.where(lanes, dov, jnp.zeros_like(dov))
                st = _dot(kv_, qm, "nt") + (cq_ref[hh] - ck_ref[hh])
                if masked:
                    keep = lax.broadcasted_iota(jnp.int32, (t, t), 0) <= lax.broadcasted_iota(jnp.int32, (t, t), 1)
                    st = jnp.where(keep, st, -jnp.inf)
                pt = jnp.exp(st - lse_ref[hh])
                dst = pt * (_dot(v_ref[...], dom, "nt") - dl_ref[hh])
                dv_acc[...] += _dot(pt, dom, "nn")
                dk_acc[...] += _dot(dst, qm, "nn")
                dq_ref[pl.ds(start, t), :] += _dot(dst, km, "tn") * scale
                dcq_ref[hh, :, pl.ds(start, t)] += jnp.sum(dst, axis=0, keepdims=True)
                dck_acc[hh] -= jnp.sum(dst, axis=1, keepdims=True)

        pl.when(i > j)(functools.partial(tile, False))
        pl.when(i == j)(functools.partial(tile, True))

        @pl.when(i == nb - 1)
        def _():
            dk_ref[...] = dk_acc[...]
            dv_ref[...] = dv_acc[...]
            dck_ref[...] = dck_acc[...]

    q_spec = pl.BlockSpec((t, LANES), lambda b, p, it, jt: (it[p], b))
    qrow_spec = pl.BlockSpec((hb, 1, t), lambda b, p, it, jt: (b, 0, it[p]))
    k_spec = pl.BlockSpec((t, LANES), lambda b, p, it, jt: (jt[p], b))
    v_spec = pl.BlockSpec((t, LANES), lambda b, p, it, jt: (jt[p], n_blocks + b))
    kcol_spec = pl.BlockSpec((hb, t, 1), lambda b, p, it, jt: (b, jt[p], 0))
    wide = jax.ShapeDtypeStruct((s_dim, da), F32)
    return pl.pallas_call(
        body, name=name,
        grid_spec=pltpu.PrefetchScalarGridSpec(
            num_scalar_prefetch=2, grid=(n_blocks, len(pairs)),
            in_specs=[q_spec, k_spec, v_spec, qrow_spec, kcol_spec, qrow_spec, qrow_spec, q_spec],
            out_specs=[pl.BlockSpec((s_dim, LANES), lambda b, p, it, jt: (0, b)),
                       pl.BlockSpec((hb, 1, s_dim), lambda b, p, it, jt: (b, 0, 0)), k_spec, k_spec, kcol_spec],
            scratch_shapes=[pltpu.VMEM((t, LANES), F32), pltpu.VMEM((t, LANES), F32), pltpu.VMEM((hb, t, 1), F32)]),
        out_shape=[wide, jax.ShapeDtypeStruct((n_h, 1, s_dim), F32), wide, wide,
                   jax.ShapeDtypeStruct((n_h, s_dim, 1), F32)],
        compiler_params=_cparams(("parallel", "arbitrary"),
                                 10 * hb * _nbytes((t, t), F32) + 4 * _nbytes((s_dim, LANES), F32)),
    )(i_tab, j_tab, q, kv, kv, c_row, c_col, lse, delta, do)


_HBM = pl.BlockSpec(memory_space=pltpu.HBM)
_MESH_ID = pl.DeviceIdType.MESH


def _all_gather(block, name):
    r, w = block.shape

    def body(x_ref, out_ref, send_sems, recv_sems, local_sem):
        x, y, c = lax.axis_index("x"), lax.axis_index("y"), lax.axis_index("c")
        me, sibling = (x, y, c), (x, y, 1 - c)
        chips = [(1 - x, y), (x, 1 - y), (1 - x, 1 - y)]

        def slot(px, py, pc):
            return out_ref.at[4 * px + 2 * py + pc]

        def copy(k, blk, to, src=None):
            return pltpu.make_async_remote_copy(
                src_ref=slot(*blk) if src is None else src, dst_ref=slot(*blk),
                send_sem=send_sems.at[k], recv_sem=recv_sems.at[k], device_id=to, device_id_type=_MESH_ID)

        mine = pltpu.make_async_copy(x_ref, slot(*me), local_sem)
        mine.start()
        first = [copy(0, me, sibling, src=x_ref)]
        first += [copy(1 + n, me, (*chip, c), src=x_ref) for n, chip in enumerate(chips)]
        for cp in first:
            cp.start()
        passed = [copy(4 + n, (*chip, c), sibling) for n, chip in enumerate(chips)]
        for n, chip in enumerate(chips):
            copy(1 + n, (*chip, c), me).wait_recv()
            passed[n].start()
        copy(0, sibling, me).wait_recv()
        for n, chip in enumerate(chips):
            copy(4 + n, (*chip, 1 - c), me).wait_recv()
        for cp in first + passed:
            cp.wait_send()
        mine.wait()

    return pl.pallas_call(
        body, name=name, out_shape=jax.ShapeDtypeStruct((N_DEV, r, w), block.dtype),
        in_specs=[_HBM], out_specs=_HBM,
        scratch_shapes=[pltpu.SemaphoreType.DMA((7,)), pltpu.SemaphoreType.DMA((7,)), pltpu.SemaphoreType.DMA],
    )(block)


_SEM = pl.BlockSpec(memory_space=pltpu.SEMAPHORE)
_EFFECT = pltpu.SideEffectType.DATAFLOW_SIDE_EFFECTING


def _exchange_start(srcs, personalized, after, name):
    n = len(srcs)
    n_after = len(after)
    lands = [lax.empty((N_DEV,) + s.shape[-2:], s.dtype) for s in srcs]

    def body(*refs):
        src_refs, land_refs = refs[:n], refs[n:2 * n]
        outs = refs[2 * n + n_after:]
        send_sems, recv_sems, token = outs[:n], outs[n:2 * n], outs[-1]
        x, y, c = lax.axis_index("x"), lax.axis_index("y"), lax.axis_index("c")
        mine = 4 * x + 2 * y + c
        for ci in range(n):
            for k in range(1, N_DEV):
                px = 1 - x if k & 4 else x
                py = 1 - y if k & 2 else y
                pc = 1 - c if k & 1 else c
                src = src_refs[ci].at[4 * px + 2 * py + pc] if personalized else src_refs[ci]
                pltpu.make_async_remote_copy(
                    src_ref=src, dst_ref=land_refs[ci].at[mine], send_sem=send_sems[ci], recv_sem=recv_sems[ci],
                    device_id=(px, py, pc), device_id_type=_MESH_ID).start()
        token[...] = jnp.zeros_like(token)

    sem = pltpu.SemaphoreType.DMA(())
    out_shape = ([sem] * (2 * n) + [pltpu.HBM(s.shape, s.dtype) for s in srcs]
                 + [pltpu.HBM(l.shape, l.dtype) for l in lands] + [jax.ShapeDtypeStruct((8, LANES), F32)])
    res = pl.pallas_call(
        body, name=name, out_shape=tuple(out_shape),
        in_specs=[_HBM] * (2 * n) + [_ANY] * n_after,
        out_specs=tuple([_SEM] * (2 * n) + [_HBM] * (2 * n) + [pl.BlockSpec(memory_space=pltpu.VMEM)]),
        input_output_aliases={i: 2 * n + i for i in range(2 * n)},
        compiler_params=pltpu.CompilerParams(has_side_effects=_EFFECT),
    )(*[pltpu.with_memory_space_constraint(s, pltpu.HBM) for s in srcs],
      *[pltpu.with_memory_space_constraint(l, pltpu.HBM) for l in lands], *after)
    handles = [(res[ci], res[n + ci], res[2 * n + ci], res[3 * n + ci]) for ci in range(n)]
    return handles, res[-1]


def _exchange_wait(handle, after, name):
    send_sem, recv_sem, src_thru, land_thru = handle

    def body(src_ref, land_ref, send_ref, recv_ref, after_ref, src_out, land_out):
        seven = land_ref.at[pl.ds(0, N_DEV - 1)]
        copies = pltpu.make_async_remote_copy(
            src_ref=seven, dst_ref=seven, send_sem=send_ref, recv_sem=recv_ref,
            device_id=(lax.axis_index("x"), lax.axis_index("y"), lax.axis_index("c")), device_id_type=_MESH_ID)
        copies.wait_send()
        copies.wait_recv()

    return pl.pallas_call(
        body, name=name,
        out_shape=(pltpu.HBM(src_thru.shape, src_thru.dtype), pltpu.HBM(land_thru.shape, land_thru.dtype)),
        in_specs=(_HBM, _HBM, _SEM, _SEM, _ANY), out_specs=(_HBM, _HBM), input_output_aliases={0: 0, 1: 1},
        compiler_params=pltpu.CompilerParams(has_side_effects=_EFFECT),
    )(src_thru, land_thru, send_sem, recv_sem, after)[1]


def _own_slot(land, own, me):
    return lax.dynamic_update_index_in_dim(land, own, me, axis=0)


def _sum_slots(slots, name):
    n, r, w = slots.shape
    tr = _tile(r, 128, WIRE_ROW_ALIGN)

    def body(s_ref, o_ref):
        acc = s_ref[0].astype(F32)
        for d in range(1, n):
            acc = acc + s_ref[d].astype(F32)
        o_ref[...] = acc

    return pl.pallas_call(
        body, name=name, grid=(r // tr,),
        in_specs=[pl.BlockSpec((n, tr, w), lambda i: (0, i, 0))],
        out_specs=pl.BlockSpec((tr, w), lambda i: (i, 0)),
        out_shape=jax.ShapeDtypeStruct((r, w), F32),
        compiler_params=_cparams(("parallel",), 2 * _nbytes((n, tr, w), slots.dtype) + 4 * _nbytes((tr, w), F32)),
    )(slots)


def _adamw(w, g, m, v, name):
    r, c = w.shape
    tr = _tile(r, 512, 8)

    def body(w_ref, g_ref, m_ref, v_ref, d_ref, mo_ref, vo_ref):
        gv = g_ref[...]
        m_new = ADAM_B1 * m_ref[...] + (1.0 - ADAM_B1) * gv
        v_new = ADAM_B2 * v_ref[...] + (1.0 - ADAM_B2) * (gv * gv)
        m_hat = m_new / (1.0 - ADAM_B1 ** ADAM_STEP)
        v_hat = v_new / (1.0 - ADAM_B2 ** ADAM_STEP)
        d_ref[...] = -ADAM_LR * (m_hat / (jnp.sqrt(v_hat) + ADAM_EPS) + ADAM_WD * w_ref[...])
        mo_ref[...] = m_new
        vo_ref[...] = v_new

    blk = pl.BlockSpec((tr, c), lambda i: (i, 0))
    shp = jax.ShapeDtypeStruct((r, c), F32)
    return pl.pallas_call(
        body, name=name, grid=(r // tr,), in_specs=[blk] * 4, out_specs=[blk] * 3, out_shape=[shp] * 3,
        compiler_params=_cparams(("parallel",), 16 * _nbytes((tr, _round_up(c, LANES)), F32)),
    )(w, g, m, v)


def _pack_rows(parts, width, dtype, row_align):
    rows, spans, off = [], [], 0
    for p in parts:
        flat = p.reshape(-1).astype(dtype)
        n_rows = _round_up(-(-flat.shape[0] // width), row_align)
        flat = jnp.pad(flat, (0, n_rows * width - flat.shape[0]))
        rows.append(flat.reshape(n_rows, width))
        spans.append((off, n_rows))
        off += n_rows
    return jnp.concatenate(rows, axis=0), spans


def _unpack_rows(mat, span, shape):
    off, n_rows = span
    n = 1
    for s in shape:
        n *= s
    return mat[..., off:off + n_rows, :].reshape(mat.shape[:-2] + (-1,))[..., :n].reshape(mat.shape[:-2] + tuple(shape))


def _block_diag(w, size):
    n, b, _ = w.shape
    eye = jnp.eye(n, dtype=w.dtype)
    dense = (w[:, :, None, :] * eye[:, None, :, None]).reshape(n * b, n * b)
    return jnp.pad(dense, ((0, size - n * b), (0, size - n * b)))


def _diag_blocks(dense, n, b):
    return jnp.stack([dense[k * b:(k + 1) * b, k * b:(k + 1) * b] for k in range(n)])


def _pad_rows(a, rows):
    return jnp.pad(a, ((0, rows - a.shape[0]), (0, 0)))


def _pad_cols(a, cols):
    return jnp.pad(a, ((0, 0), (0, cols - a.shape[1])))


def _train_step(a):
    x = a["x"][0]
    target = a["loss_target"][0]
    s_dim, d = x.shape
    n_layers = a["ffn1_pre_g"].shape[0]
    f_shard = a["ffn1_w_gate"].shape[2]
    c_shard = a["rg_conv_b"].shape[1]
    c_dim = c_shard * N_DEV
    cp = _round_up(c_dim, LANES)
    conv_width = a["rg_conv_w"].shape[1]
    n_blocks, lru_block = a["rg_w_a"].shape[1], a["rg_w_a"].shape[2]
    d_attn = a["attn_w_q"].shape[2]
    n_heads = a["b_fgate"].shape[0]
    d_head = d_attn // n_heads
    attn_scale = d_head ** -0.5
    assert conv_width < 8 and n_heads <= LANES and n_layers == 2
    assert d_attn == d
    me = 4 * lax.axis_index("x") + 2 * lax.axis_index("y") + lax.axis_index("c")

    shard = {"rg_w_in": a["rg_w_in"][0].T, "rg_w_out": a["rg_w_out"][0], "w_kv": a["w_kv"].T,
             "attn_w_q": a["attn_w_q"][0], "attn_w_o": a["attn_w_o"][0]}
    for l in range(n_layers):
        for f in ("ffn1", "ffn2"):
            shard[(f, "gate", l)] = a[f + "_w_gate"][l].T
            shard[(f, "up", l)] = a[f + "_w_up"][l].T
            shard[(f, "down", l)] = a[f + "_w_down"][l]

    def ffn_names(f, l):
        return [(f, "gate", l), (f, "up", l), (f, "down", l)]

    def chunk_layout(names):
        spans, off = [], 0
        for nm in names:
            spans.append((nm, off, shard[nm].shape[0]))
            off += _round_up(shard[nm].shape[0], WIRE_ROW_ALIGN)
        return spans, off

    def pack_chunk(names, parts):
        return jnp.concatenate(
            [_pad_rows(parts[nm].astype(WIRE_DTYPE), _round_up(parts[nm].shape[0], WIRE_ROW_ALIGN)) for nm in names], axis=0)

    full = {}

    def unpack_chunk(names, gathered):
        for nm, o, n_rows in chunk_layout(names)[0]:
            full[nm] = gathered[:, o:o + n_rows, :].reshape(N_DEV * n_rows, d)

    fwd_chunks = [ffn_names("ffn1", 0), ["rg_w_in", "rg_w_out"], ffn_names("ffn2", 0) + ["w_kv"],
                  ffn_names("ffn1", 1) + ["attn_w_q", "attn_w_o"], ffn_names("ffn2", 1)]
    fwd_packs = [pack_chunk(names, shard) for names in fwd_chunks]
    unpack_chunk(fwd_chunks[0], _all_gather(fwd_packs[0], "gather_weights_first"))

    small_parts = [a["rg_conv_w"][0], a["rg_conv_b"][0], a["rg_b_a"][0], a["rg_b_x"][0], a["rg_lambda"][0], a["w_fgate"]]
    small_pack, small_spans = _pack_rows(small_parts, d, F32, 8)
    small_all = _all_gather(small_pack, "gather_small")
    fwd_handles, fwd_token = _exchange_start(fwd_packs[1:], False, [full[("ffn1", "down", 0)], small_all],
                                             "gather_weights_start")

    def land_weights(n, after):
        land = _exchange_wait(fwd_handles[n - 1], after, f"gather_weights_wait_{n}")
        unpack_chunk(fwd_chunks[n], _own_slot(land, fwd_packs[n], me))

    sm = [_unpack_rows(small_all, sp, p.shape) for sp, p in zip(small_spans, small_parts)]
    conv_w = jnp.moveaxis(sm[0], 0, 1).reshape(conv_width, c_dim)
    conv_b, b_a, b_x, lam = (v.reshape(1, c_dim) for v in sm[1:5])
    w_f = sm[5].reshape(d, n_heads)

    pconv = _pad_rows(_pad_cols(jnp.concatenate([conv_w, conv_b], axis=0), cp), 8)
    pvec = _pad_rows(_pad_cols(jnp.concatenate([b_a, b_x, lam], axis=0), cp), 8)
    wa_dense = _block_diag(a["rg_w_a"][0], cp).astype(MXU_DTYPE)
    wx_dense = _block_diag(a["rg_w_x"][0], cp).astype(MXU_DTYPE)
    wax = jnp.concatenate([wa_dense, wx_dense], axis=1)
    w_f_t = _pad_rows(w_f.T.astype(MXU_DTYPE), LANES)
    b_f = _pad_rows(_pad_cols(a["b_fgate"].reshape(1, n_heads), LANES), 8)

    def gain(name, l):
        return a[name][l].reshape(1, d)

    def ffn_fwd(h, f, l, after=None):
        xn = _rms_fwd(h, gain(f + "_pre_g", l), f"{f}_{l}_pre_norm", after)
        g, u, act = _ffn_up(xn, full[(f, "gate", l)], full[(f, "up", l)], f"{f}_{l}_up")
        fo, h_new = _mm_rms_res(act, full[(f, "down", l)], h, gain(f + "_post_g", l), 0.5, f"{f}_{l}_down")
        return h_new, (h, xn, g, u, act, fo)

    h0 = x
    h0a, sv_f1_0 = ffn_fwd(h0, "ffn1", 0, fwd_token)
    land_weights(1, h0a)
    w_in_gate = _pad_rows(full["rg_w_in"][:c_dim], cp)
    w_in_rec = _pad_rows(full["rg_w_in"][c_dim:], cp)
    w_in_t = jnp.concatenate([w_in_gate, w_in_rec], axis=0)
    w_out = _pad_rows(full["rg_w_out"], cp)
    hn_rg = _rms_fwd(h0a, gain("mix_pre_g", 0), "rg_pre_norm")
    gx = _mm([(hn_rg, w_in_t)], "nt", F32, "rg_in_proj")
    rec = _conv_fwd(gx, pconv, conv_width, "rg_conv")
    gates = _mm([(rec, wax)], "nn", F32, "rg_gate_proj")
    h_rec, y_rg = _scan_fwd(gx, rec, gates, pvec, "rg_scan")
    m_rg, h0b = _mm_rms_res(y_rg, w_out, h0a, gain("mix_post_g", 0), 1.0, "rg_out_proj")
    land_weights(2, h0b)
    h1, sv_f2_0 = ffn_fwd(h0b, "ffn2", 0)
    hn_kv = _rms_fwd(h1, a["kv_norm_g"].reshape(1, d), "kv_norm")
    kv = _mm([(hn_kv, full["w_kv"])], "nt", MXU_DTYPE, "kv_proj")
    fpre = _mm([(hn_kv, w_f_t)], "nt", F32, "fgate_proj")
    c_cum = _fgate_fwd(fpre, b_f, "fgate_cumsum")
    c_heads = c_cum[:, :n_heads].T
    c_col, c_row = c_heads[:, :, None], c_heads[:, None, :]
    land_weights(3, c_cum)
    h1a, sv_f1_1 = ffn_fwd(h1, "ffn1", 1)
    hn_at = _rms_fwd(h1a, gain("mix_pre_g", 1), "attn_pre_norm")
    q_s = _mm([(hn_at, full["attn_w_q"])], "nn", MXU_DTYPE, "q_proj", out_scale=attn_scale)
    o2, lse = _pair_attn_fwd(q_s, kv, kv[:, d_attn:].T, c_col, c_row, "attn_fwd")
    m_at, h1b = _mm_rms_res(o2, full["attn_w_o"], h1a, gain("mix_post_g", 1), 1.0, "attn_out_proj")
    land_weights(4, h1b)
    y, sv_f2_1 = ffn_fwd(h1b, "ffn2", 1)
    dy, loss_part = _loss_head(y, target, "loss_head")

    grads_big = {}
    grads_rep = {}

    bwd_chunks = [ffn_names("ffn2", 1), ["attn_w_q", "attn_w_o"] + ffn_names("ffn1", 1),
                  ["w_kv"] + ffn_names("ffn2", 0), ["rg_w_in", "rg_w_out"], ffn_names("ffn1", 0)]
    bwd_sends, bwd_handles = [], []

    def send_grads(after):
        n = len(bwd_sends)
        send = jnp.concatenate(
            [jnp.pad(grads_big[nm].reshape(N_DEV, n_rows, d), ((0, 0), (0, _round_up(n_rows, WIRE_ROW_ALIGN) - n_rows), (0, 0)))
             for nm, _, n_rows in chunk_layout(bwd_chunks[n])[0]], axis=1)
        handles, token = _exchange_start([send], True, [after], f"exchange_grads_start_{n}")
        bwd_sends.append(send)
        bwd_handles.append(handles[0])
        return token

    def ffn_bwd(dh_out, saved, f, l, after=None, send_now=False):
        h, xn, g, u, act, fo = saved
        df, d_post = _rms_bwd(fo, gain(f + "_post_g", l), [dh_out], None, 0.5, MXU_DTYPE, f"{f}_{l}_post_norm_bwd", after)
        dg, du = _ffn_act_bwd(df, full[(f, "down", l)], g, u, f"{f}_{l}_act_bwd")
        grads_big[(f, "down", l)] = _mm([(act, df)], "tn", WIRE_DTYPE, f"{f}_{l}_dw_down")
        grads_big[(f, "gate", l)] = _mm([(dg, xn)], "tn", WIRE_DTYPE, f"{f}_{l}_dw_gate")
        grads_big[(f, "up", l)] = _mm([(du, xn)], "tn", WIRE_DTYPE, f"{f}_{l}_dw_up")
        sent = send_grads(df) if send_now else None
        dxn = _mm([(dg, full[(f, "gate", l)]), (du, full[(f, "up", l)])], "nn", F32, f"{f}_{l}_dx", sent)
        dh_in, d_pre = _rms_bwd(h, gain(f + "_pre_g", l), [dxn], dh_out, 1.0, F32, f"{f}_{l}_pre_norm_bwd")
        grads_rep[(f + "_post_g", l)] = d_post
        grads_rep[(f + "_pre_g", l)] = d_pre
        return dh_in

    dh = ffn_bwd(dy, sv_f2_1, "ffn2", 1)
    token = send_grads(dh)
    dm, d_post = _rms_bwd(m_at, gain("mix_post_g", 1), [dh], None, 1.0, MXU_DTYPE, "attn_post_norm_bwd", token)
    grads_rep[("mix_post_g", 1)] = d_post
    do2 = _mm([(dm, full["attn_w_o"])], "nt", F32, "attn_out_proj_dx")
    grads_big["attn_w_o"] = _mm([(o2, dm)], "tn", WIRE_DTYPE, "attn_out_proj_dw")
    delta = _attn_delta(do2, o2, n_heads, "attn_delta")
    dq2, dc_q, dk2, dv2, dc_k = _pair_attn_bwd(q_s, kv, c_col, c_row, lse, delta, do2, attn_scale, "attn_bwd")
    dc_heads = dc_q[:, 0, :] + dc_k[:, :, 0]
    dhn = _mm([(dq2, full["attn_w_q"])], "nt", F32, "q_proj_dx")
    grads_big["attn_w_q"] = _mm([(hn_at, dq2)], "tn", WIRE_DTYPE, "q_proj_dw")
    dh, d_pre = _rms_bwd(h1a, gain("mix_pre_g", 1), [dhn], dh, 1.0, F32, "attn_pre_norm_bwd")
    grads_rep[("mix_pre_g", 1)] = d_pre
    dh = ffn_bwd(dh, sv_f1_1, "ffn1", 1)
    token = send_grads(dh)
    dc_cum = _pad_cols(dc_heads.T, LANES)
    dfpre, db_f = _fgate_bwd(dc_cum, fpre, b_f, "fgate_cumsum_bwd")
    dhn_kv = _mm([(dk2, full["w_kv"][:d_attn]), (dv2, full["w_kv"][d_attn:])], "nn", F32, "kv_proj_dx")
    dhn_f = _mm([(dfpre, w_f_t)], "nn", F32, "fgate_proj_dx")
    grads_big["w_kv"] = jnp.concatenate([_mm([(dk2, hn_kv)], "tn", WIRE_DTYPE, "kv_proj_dw_k"),
                                         _mm([(dv2, hn_kv)], "tn", WIRE_DTYPE, "kv_proj_dw_v")], axis=0)
    dw_f_t = _mm([(dfpre, hn_kv)], "tn", F32, "fgate_proj_dw")
    dh, d_kvg = _rms_bwd(h1, a["kv_norm_g"].reshape(1, d), [dhn_kv, dhn_f], dh, 1.0, F32, "kv_norm_bwd", token)
    dh = ffn_bwd(dh, sv_f2_0, "ffn2", 0)
    token = send_grads(dh)
    dm, d_post = _rms_bwd(m_rg, gain("mix_post_g", 0), [dh], None, 1.0, MXU_DTYPE, "rg_post_norm_bwd", token)
    grads_rep[("mix_post_g", 0)] = d_post
    dy_rg = _mm([(dm, w_out)], "nt", F32, "rg_out_proj_dx")
    dw_out = _mm([(y_rg, dm)], "tn", WIRE_DTYPE, "rg_out_proj_dw")
    dgate, dra, dia, drec1, dpvec = _scan_bwd(dy_rg, gx, h_rec, rec, gates, pvec, "rg_scan_bwd")
    drec2 = _mm([(dra, wa_dense), (dia, wx_dense)], "nt", F32, "rg_gate_proj_dx")
    dwa_dense = _mm([(rec, dra)], "tn", F32, "rg_gate_proj_dwa")
    dwx_dense = _mm([(rec, dia)], "tn", F32, "rg_gate_proj_dwx")
    drec0, dpconv = _conv_bwd(drec1, drec2, gx, pconv, conv_width, "rg_conv_bwd")
    dhn = _mm([(dgate, w_in_gate), (drec0, w_in_rec)], "nn", F32, "rg_in_proj_dx")
    dw_in_gate = _mm([(dgate, hn_rg)], "tn", WIRE_DTYPE, "rg_in_proj_dw_gate")
    dw_in_rec = _mm([(drec0, hn_rg)], "tn", WIRE_DTYPE, "rg_in_proj_dw_rec")
    dh, d_pre = _rms_bwd(h0a, gain("mix_pre_g", 0), [dhn], dh, 1.0, F32, "rg_pre_norm_bwd")
    grads_rep[("mix_pre_g", 0)] = d_pre
    grads_big["rg_w_in"] = jnp.concatenate([dw_in_gate[:c_dim], dw_in_rec[:c_dim]], axis=0)
    grads_big["rg_w_out"] = dw_out[:c_dim]
    token = send_grads(dh)
    grad_x = ffn_bwd(dh, sv_f1_0, "ffn1", 0, token, send_now=True)

    g_shard = {}

    def land_grads(n, after):
        land = _exchange_wait(bwd_handles[n], after, f"exchange_grads_wait_{n}")
        own = lax.dynamic_index_in_dim(bwd_sends[n], me, axis=0, keepdims=False)
        g_chunk = _sum_slots(_own_slot(land, own, me), f"sum_weight_grads_{n}")
        for nm, o, n_rows in chunk_layout(bwd_chunks[n])[0]:
            g_shard[nm] = g_chunk[o:o + n_rows]

    for n in range(len(bwd_chunks) - 1):
        land_grads(n, grad_x)

    def gain_grad(name):
        return jnp.concatenate([grads_rep[(name, l)] for l in range(n_layers)], axis=0)

    rep_names = ["ffn1_pre_g", "ffn1_post_g", "mix_pre_g", "mix_post_g", "ffn2_pre_g", "ffn2_post_g"]
    rep_parts = [gain_grad(nm) for nm in rep_names]
    rep_names += ["kv_norm_g", "b_fgate", "rg_w_a", "rg_w_x", "rg_conv_w", "rg_conv_b", "rg_b_a", "rg_b_x", "rg_lambda", "w_fgate"]
    rep_parts += [
        d_kvg, db_f[0, :n_heads],
        _diag_blocks(dwa_dense, n_blocks, lru_block), _diag_blocks(dwx_dense, n_blocks, lru_block),
        dpconv[:conv_width, :c_dim], dpconv[conv_width, :c_dim],
        dpvec[0, :c_dim], dpvec[1, :c_dim], dpvec[2, :c_dim],
        dw_f_t[:n_heads].T]
    rep_pack, rep_spans = _pack_rows(rep_parts, d, F32, WIRE_ROW_ALIGN)
    rep_sum = _sum_slots(_all_gather(rep_pack, "gather_small_grads"), "sum_small_grads")
    g_rep = {nm: _unpack_rows(rep_sum, sp, p.shape) for nm, sp, p in zip(rep_names, rep_spans, rep_parts)}

    def my_cols(full_grad, n):
        return lax.dynamic_slice_in_dim(full_grad, me * n, n, axis=full_grad.ndim - 1)

    def ffn_grads(f):
        grad[f + "_w_gate"] = jnp.stack([g_shard[(f, "gate", l)].T for l in range(n_layers)])
        grad[f + "_w_up"] = jnp.stack([g_shard[(f, "up", l)].T for l in range(n_layers)])
        grad[f + "_w_down"] = jnp.stack([g_shard[(f, "down", l)] for l in range(n_layers)])

    grad = {}
    for nm in ("ffn1_pre_g", "ffn1_post_g", "mix_pre_g", "mix_post_g", "ffn2_pre_g", "ffn2_post_g"):
        grad[nm] = g_rep[nm]
    ffn_grads("ffn2")
    grad["rg_w_in"] = g_shard["rg_w_in"].T[None]
    grad["rg_conv_w"] = my_cols(g_rep["rg_conv_w"], c_shard)[None]
    for nm in ("rg_conv_b", "rg_b_a", "rg_b_x", "rg_lambda"):
        grad[nm] = my_cols(g_rep[nm], c_shard)[None]
    grad["rg_w_a"] = g_rep["rg_w_a"][None]
    grad["rg_w_x"] = g_rep["rg_w_x"][None]
    grad["rg_w_out"] = g_shard["rg_w_out"][None]
    grad["kv_norm_g"] = g_rep["kv_norm_g"].reshape(d)
    grad["w_kv"] = g_shard["w_kv"].T
    grad["w_fgate"] = lax.dynamic_slice_in_dim(g_rep["w_fgate"], me * (d // N_DEV), d // N_DEV, axis=0)
    grad["b_fgate"] = g_rep["b_fgate"]
    grad["attn_w_q"] = g_shard["attn_w_q"][None]
    grad["attn_w_o"] = g_shard["attn_w_o"][None]

    delta, new_m, new_v = {}, {}, {}

    def adamw(nm):
        w = a[nm]
        shape = w.shape
        two_d = (1, shape[0]) if w.ndim == 1 else (-1, shape[-1])
        dl, mo, vo = _adamw(w.reshape(two_d), grad[nm].reshape(two_d), a["m_" + nm].reshape(two_d),
                            a["v_" + nm].reshape(two_d), "adamw_" + nm)
        delta[nm], new_m[nm], new_v[nm] = dl.reshape(shape), mo.reshape(shape), vo.reshape(shape)
        grad[nm] = grad[nm].reshape(shape)

    last_names = ("ffn1_w_gate", "ffn1_w_up", "ffn1_w_down")
    for nm in WEIGHT_NAMES:
        if nm not in last_names:
            adamw(nm)
    land_grads(len(bwd_chunks) - 1, delta["attn_w_o"])
    ffn_grads("ffn1")
    for nm in last_names:
        adamw(nm)

    loss = lax.psum(loss_part[0, 0], AXES)
    return (loss, grad_x[None], *[grad[n] for n in WEIGHT_NAMES], *[delta[n] for n in WEIGHT_NAMES],
            *[new_m[n] for n in WEIGHT_NAMES], *[new_v[n] for n in WEIGHT_NAMES])


def kernel(x, ffn1_pre_g, ffn1_w_gate, ffn1_w_up, ffn1_w_down, ffn1_post_g, mix_pre_g, mix_post_g, ffn2_pre_g, ffn2_w_gate, ffn2_w_up, ffn2_w_down, ffn2_post_g, rg_w_in, rg_conv_w, rg_conv_b, rg_w_a, rg_b_a, rg_w_x, rg_b_x, rg_lambda, rg_w_out, kv_norm_g, w_kv, w_fgate, b_fgate, attn_w_q, attn_w_o, loss_target, m_ffn1_pre_g, m_ffn1_w_gate, m_ffn1_w_up, m_ffn1_w_down, m_ffn1_post_g, m_mix_pre_g, m_mix_post_g, m_ffn2_pre_g, m_ffn2_w_gate, m_ffn2_w_up, m_ffn2_w_down, m_ffn2_post_g, m_rg_w_in, m_rg_conv_w, m_rg_conv_b, m_rg_w_a, m_rg_b_a, m_rg_w_x, m_rg_b_x, m_rg_lambda, m_rg_w_out, m_kv_norm_g, m_w_kv, m_w_fgate, m_b_fgate, m_attn_w_q, m_attn_w_o, v_ffn1_pre_g, v_ffn1_w_gate, v_ffn1_w_up, v_ffn1_w_down, v_ffn1_post_g, v_mix_pre_g, v_mix_post_g, v_ffn2_pre_g, v_ffn2_w_gate, v_ffn2_w_up, v_ffn2_w_down, v_ffn2_post_g, v_rg_w_in, v_rg_conv_w, v_rg_conv_b, v_rg_w_a, v_rg_b_a, v_rg_w_x, v_rg_b_x, v_rg_lambda, v_rg_w_out, v_kv_norm_g, v_w_kv, v_w_fgate, v_b_fgate, v_attn_w_q, v_attn_w_o):
    return _train_step(dict(locals()))
```

```python
import functools

import jax
import jax.numpy as jnp
from jax import lax
from jax.experimental import pallas as pl
from jax.experimental.pallas import tpu as pltpu

F32 = jnp.float32
MXU_DTYPE = jnp.bfloat16
WIRE_DTYPE = jnp.bfloat16
N_DEV = 8
AXES = ("x", "y", "c")
LANES = 128
WIRE_ROW_ALIGN = 16
VMEM_LIMIT_MIN = 32 * 2 ** 20
VMEM_LIMIT_MAX = 56 * 2 ** 20

RMS_EPS = 1e-6
LRU_C = 8.0
ADAM_LR, ADAM_B1, ADAM_B2, ADAM_EPS, ADAM_WD, ADAM_STEP = 0.001, 0.9, 0.999, 1e-08, 0.01, 10

WEIGHT_NAMES = (
    "ffn1_pre_g", "ffn1_w_gate", "ffn1_w_up", "ffn1_w_down", "ffn1_post_g", "mix_pre_g", "mix_post_g",
    "ffn2_pre_g", "ffn2_w_gate", "ffn2_w_up", "ffn2_w_down", "ffn2_post_g", "rg_w_in", "rg_conv_w",
    "rg_conv_b", "rg_w_a", "rg_b_a", "rg_w_x", "rg_b_x", "rg_lambda", "rg_w_out", "kv_norm_g", "w_kv",
    "w_fgate", "b_fgate", "attn_w_q", "attn_w_o")


def _round_up(n, m):
    return (n + m - 1) // m * m


def _tile(dim, target, align=LANES):
    if dim <= target:
        return dim
    best = None
    t = align
    while t <= target:
        if dim % t == 0:
            best = t
        t += align
    return dim if best is None else best


def _cparams(semantics, vmem_estimate):
    limit = min(VMEM_LIMIT_MAX, max(VMEM_LIMIT_MIN, 2 * int(vmem_estimate)))
    return pltpu.CompilerParams(dimension_semantics=semantics, vmem_limit_bytes=limit)


def _nbytes(shape, dtype):
    n = 1
    for s in shape:
        n *= s
    return n * jnp.dtype(dtype).itemsize


def _sigmoid(x):
    return jax.nn.sigmoid(x)


def _softplus(x):
    return jnp.maximum(x, 0.0) + jnp.log1p(jnp.exp(-jnp.abs(x)))


def _expm1(x):
    series = x * (1.0 + x * (0.5 + x * (1.0 / 6.0 + x * (1.0 / 24.0 + x * (1.0 / 120.0)))))
    return jnp.where(jnp.abs(x) < 0.25, series, jnp.exp(x) - 1.0)


_GELU_C = 0.7978845608028654
_GELU_A = 0.044715


def _gelu(x):
    return 0.5 * x * (1.0 + jnp.tanh(_GELU_C * (x + _GELU_A * x * x * x)))


def _gelu_grad(x):
    t = jnp.tanh(_GELU_C * (x + _GELU_A * x * x * x))
    return 0.5 * (1.0 + t) + 0.5 * x * (1.0 - t * t) * _GELU_C * (1.0 + 3.0 * _GELU_A * x * x)


_DOT_DIMS = {"nn": ((1,), (0,)), "nt": ((1,), (1,)), "tn": ((0,), (0,))}


def _dot(a, b, mode):
    return lax.dot_general(a.astype(MXU_DTYPE), b.astype(MXU_DTYPE), (_DOT_DIMS[mode], ((), ())),
                           preferred_element_type=F32)


def _mm(pairs, mode, out_dtype, name, after=None, out_scale=None):
    a0, b0 = pairs[0]
    if mode == "tn":
        k_dim, m_dim = a0.shape
        n_dim = b0.shape[1]
    else:
        m_dim, k_dim = a0.shape
        n_dim = b0.shape[0] if mode == "nt" else b0.shape[1]
    for a, b in pairs:
        assert a.shape == a0.shape and b.shape == b0.shape
    tm = _tile(m_dim, 1408 if mode == "tn" else 512)
    tn = _tile(n_dim, 1408)
    tk = _tile(k_dim, 1408)
    nk = k_dim // tk
    n_pairs = len(pairs)

    if mode == "tn":
        a_spec = pl.BlockSpec((tk, tm), lambda i, j, k: (k, i))
    else:
        a_spec = pl.BlockSpec((tm, tk), lambda i, j, k: (i, k))
    if mode == "nt":
        b_spec = pl.BlockSpec((tn, tk), lambda i, j, k: (j, k))
    else:
        b_spec = pl.BlockSpec((tk, tn), lambda i, j, k: (k, j))

    order = [] if after is None else [after]

    def body(*refs):
        ins, o_ref, acc = refs[:2 * n_pairs], refs[-2], refs[-1]
        k = pl.program_id(2)

        @pl.when(k == 0)
        def _():
            acc[...] = jnp.zeros_like(acc)

        s = acc[...]
        for p in range(n_pairs):
            s = s + _dot(ins[2 * p][...], ins[2 * p + 1][...], mode)
        acc[...] = s

        @pl.when(k == nk - 1)
        def _():
            r = acc[...] if out_scale is None else acc[...] * out_scale
            o_ref[...] = r.astype(out_dtype)

    est = (2 * n_pairs * (_nbytes((tm, tk), a0.dtype) + _nbytes((tk, tn), b0.dtype))
           + 2 * _nbytes((tm, tn), out_dtype) + 2 * _nbytes((tm, tn), F32))
    flat = [t for ab in pairs for t in ab]
    return pl.pallas_call(
        body, name=name, grid=(m_dim // tm, n_dim // tn, nk),
        in_specs=[a_spec, b_spec] * n_pairs + [_ANY] * len(order),
        out_specs=pl.BlockSpec((tm, tn), lambda i, j, k: (i, j)),
        out_shape=jax.ShapeDtypeStruct((m_dim, n_dim), out_dtype),
        scratch_shapes=[pltpu.VMEM((tm, tn), F32)],
        compiler_params=_cparams(("parallel", "parallel", "arbitrary"), est),
    )(*flat, *order)


_ANY = pl.BlockSpec(memory_space=pl.ANY)


def _rms_fwd(x, gain, name, after=None):
    s_dim, d = x.shape
    tm = _tile(s_dim, 512, 8)

    def body(*refs):
        x_ref, g_ref, o_ref = refs[0], refs[1], refs[-1]
        v = x_ref[...]
        r = lax.rsqrt(jnp.mean(v * v, axis=-1, keepdims=True) + RMS_EPS)
        o_ref[...] = (v * r * g_ref[...]).astype(MXU_DTYPE)

    order = [] if after is None else [after]
    return pl.pallas_call(
        body, name=name, grid=(s_dim // tm,),
        in_specs=[pl.BlockSpec((tm, d), lambda i: (i, 0)), pl.BlockSpec((1, d), lambda i: (0, 0))] + [_ANY] * len(order),
        out_specs=pl.BlockSpec((tm, d), lambda i: (i, 0)),
        out_shape=jax.ShapeDtypeStruct((s_dim, d), MXU_DTYPE),
        compiler_params=_cparams(("parallel",), 6 * _nbytes((tm, d), F32)),
    )(x, gain, *order)


def _rms_bwd(x, gain, dys, res, scale, out_dtype, name, after=None):
    s_dim, d = x.shape
    tm = _tile(s_dim, 512, 8)
    n_dy = len(dys)
    has_res = res is not None
    order = [] if after is None else [after]

    def body(*refs):
        x_ref, g_ref = refs[0], refs[1]
        dy_refs = refs[2:2 + n_dy]
        res_ref = refs[2 + n_dy] if has_res else None
        dx_ref, dg_ref = refs[-2], refs[-1]

        @pl.when(pl.program_id(0) == 0)
        def _():
            dg_ref[...] = jnp.zeros_like(dg_ref)

        v = x_ref[...]
        r = lax.rsqrt(jnp.mean(v * v, axis=-1, keepdims=True) + RMS_EPS)
        xh = v * r
        dy = dy_refs[0][...].astype(F32)
        for extra in dy_refs[1:]:
            dy = dy + extra[...].astype(F32)
        gd = dy * g_ref[...]
        dx = scale * r * (gd - xh * jnp.mean(gd * xh, axis=-1, keepdims=True))
        if has_res:
            dx = dx + res_ref[...]
        dx_ref[...] = dx.astype(out_dtype)
        dg_ref[...] += scale * jnp.sum(dy * xh, axis=0, keepdims=True)

    row = pl.BlockSpec((tm, d), lambda i: (i, 0))
    vec = pl.BlockSpec((1, d), lambda i: (0, 0))
    ops = [x, gain] + list(dys) + ([res] if has_res else [])
    return pl.pallas_call(
        body, name=name, grid=(s_dim // tm,),
        in_specs=[row, vec] + [row] * (n_dy + int(has_res)) + [_ANY] * len(order),
        out_specs=[row, vec],
        out_shape=[jax.ShapeDtypeStruct((s_dim, d), out_dtype), jax.ShapeDtypeStruct((1, d), F32)],
        compiler_params=_cparams(("arbitrary",), (2 * len(ops) + 6) * _nbytes((tm, d), F32)),
    )(*ops, *order)


def _mm_rms_res(a, b, h, gain, scale, name):
    s_dim, k_dim = a.shape
    d = b.shape[1]
    tm = _tile(s_dim, 512, 8)
    tk = _tile(k_dim, 1408)
    nk = k_dim // tk

    def body(a_ref, b_ref, h_ref, g_ref, f_ref, o_ref, acc):
        k = pl.program_id(1)

        @pl.when(k == 0)
        def _():
            acc[...] = jnp.zeros_like(acc)

        acc[...] += _dot(a_ref[...], b_ref[...], "nn")

        @pl.when(k == nk - 1)
        def _():
            f = acc[...]
            r = lax.rsqrt(jnp.mean(f * f, axis=-1, keepdims=True) + RMS_EPS)
            f_ref[...] = f
            o_ref[...] = h_ref[...] + scale * (f * r * g_ref[...])

    row = pl.BlockSpec((tm, d), lambda i, k: (i, 0))
    est = (2 * (_nbytes((tm, tk), a.dtype) + _nbytes((tk, d), b.dtype)) + 8 * _nbytes((tm, d), F32))
    return pl.pallas_call(
        body, name=name, grid=(s_dim // tm, nk),
        in_specs=[pl.BlockSpec((tm, tk), lambda i, k: (i, k)), pl.BlockSpec((tk, d), lambda i, k: (k, 0)),
                  row, pl.BlockSpec((1, d), lambda i, k: (0, 0))],
        out_specs=[row, row],
        out_shape=[jax.ShapeDtypeStruct((s_dim, d), F32), jax.ShapeDtypeStruct((s_dim, d), F32)],
        scratch_shapes=[pltpu.VMEM((tm, d), F32)],
        compiler_params=_cparams(("parallel", "arbitrary"), est),
    )(a, b, h, gain)


def _ffn_up(xn, wg_t, wu_t, name):
    s_dim, d = xn.shape
    f_dim = wg_t.shape[0]
    tm = _tile(s_dim, 1024, 8)
    tf = _tile(f_dim, 256)

    def body(x_ref, wg_ref, wu_ref, g_ref, u_ref, a_ref):
        x = x_ref[...]
        g = _dot(x, wg_ref[...], "nt")
        u = _dot(x, wu_ref[...], "nt")
        g_ref[...] = g.astype(MXU_DTYPE)
        u_ref[...] = u.astype(MXU_DTYPE)
        a_ref[...] = (g * _sigmoid(g) * u).astype(MXU_DTYPE)

    w_spec = pl.BlockSpec((tf, d), lambda i, j: (j, 0))
    o_spec = pl.BlockSpec((tm, tf), lambda i, j: (i, j))
    o_shape = jax.ShapeDtypeStruct((s_dim, f_dim), MXU_DTYPE)
    est = 2 * _nbytes((tm, d), xn.dtype) + 4 * _nbytes((tf, d), wg_t.dtype) + 10 * _nbytes((tm, tf), F32)
    return pl.pallas_call(
        body, name=name, grid=(s_dim // tm, f_dim // tf),
        in_specs=[pl.BlockSpec((tm, d), lambda i, j: (i, 0)), w_spec, w_spec],
        out_specs=[o_spec, o_spec, o_spec], out_shape=[o_shape, o_shape, o_shape],
        compiler_params=_cparams(("parallel", "parallel"), est),
    )(xn, wg_t, wu_t)


def _ffn_act_bwd(df, wd, g, u, name):
    s_dim, d = df.shape
    f_dim = wd.shape[0]
    tm = _tile(s_dim, 1024, 8)
    tf = _tile(f_dim, 256)

    def body(df_ref, wd_ref, g_ref, u_ref, dg_ref, du_ref):
        dh = _dot(df_ref[...], wd_ref[...], "nt")
        gv = g_ref[...].astype(F32)
        uv = u_ref[...].astype(F32)
        sg = _sigmoid(gv)
        dg_ref[...] = (dh * uv * (sg * (1.0 + gv * (1.0 - sg)))).astype(MXU_DTYPE)
        du_ref[...] = (dh * gv * sg).astype(MXU_DTYPE)

    t_spec = pl.BlockSpec((tm, tf), lambda i, j: (i, j))
    o_shape = jax.ShapeDtypeStruct((s_dim, f_dim), MXU_DTYPE)
    est = 2 * _nbytes((tm, d), df.dtype) + 2 * _nbytes((tf, d), wd.dtype) + 12 * _nbytes((tm, tf), F32)
    return pl.pallas_call(
        body, name=name, grid=(s_dim // tm, f_dim // tf),
        in_specs=[pl.BlockSpec((tm, d), lambda i, j: (i, 0)), pl.BlockSpec((tf, d), lambda i, j: (j, 0)),
                  t_spec, t_spec],
        out_specs=[t_spec, t_spec], out_shape=[o_shape, o_shape],
        compiler_params=_cparams(("parallel", "parallel"), est),
    )(df, wd, g, u)


def _ffn_bwd_fused(df, wd, wg_t, wu_t, g, u, h, gain, dh_out, name, after=None):
    s_dim, d = df.shape
    f_dim = wd.shape[0]
    tm = _tile(s_dim, 512, 8)
    tf = _tile(f_dim, 256)
    nf = f_dim // tf
    order = [] if after is None else [after]

    def body(*refs):
        df_ref, wd_ref, wg_ref, wu_ref, g_ref, u_ref, h_ref, gain_ref, res_ref = refs[:9]
        dg_ref, du_ref, dx_ref, dgain_ref, acc = refs[-5:]
        i, j = pl.program_id(0), pl.program_id(1)

        @pl.when((i == 0) & (j == 0))
        def _():
            dgain_ref[...] = jnp.zeros_like(dgain_ref)

        @pl.when(j == 0)
        def _():
            acc[...] = jnp.zeros_like(acc)

        dh = _dot(df_ref[...], wd_ref[...], "nt")
        gv = g_ref[...].astype(F32)
        uv = u_ref[...].astype(F32)
        sg = _sigmoid(gv)
        dg = (dh * uv * (sg * (1.0 + gv * (1.0 - sg)))).astype(MXU_DTYPE)
        du = (dh * gv * sg).astype(MXU_DTYPE)
        dg_ref[...] = dg
        du_ref[...] = du
        acc[...] += _dot(dg, wg_ref[...], "nn") + _dot(du, wu_ref[...], "nn")

        @pl.when(j == nf - 1)
        def _():
            v = h_ref[...]
            r = lax.rsqrt(jnp.mean(v * v, axis=-1, keepdims=True) + RMS_EPS)
            xh = v * r
            dy = acc[...]
            gd = dy * gain_ref[...]
            dx_ref[...] = r * (gd - xh * jnp.mean(gd * xh, axis=-1, keepdims=True)) + res_ref[...]
            dgain_ref[...] += jnp.sum(dy * xh, axis=0, keepdims=True)

    row = pl.BlockSpec((tm, d), lambda i, j: (i, 0))
    w_spec = pl.BlockSpec((tf, d), lambda i, j: (j, 0))
    t_spec = pl.BlockSpec((tm, tf), lambda i, j: (i, j))
    vec = pl.BlockSpec((1, d), lambda i, j: (0, 0))
    lo = jax.ShapeDtypeStruct((s_dim, f_dim), MXU_DTYPE)
    est = (2 * _nbytes((tm, d), df.dtype) + 6 * _nbytes((tf, d), wd.dtype) + 10 * _nbytes((tm, tf), F32)
           + 9 * _nbytes((tm, d), F32))
    return pl.pallas_call(
        body, name=name, grid=(s_dim // tm, nf),
        in_specs=[row, w_spec, w_spec, w_spec, t_spec, t_spec, row, vec, row] + [_ANY] * len(order),
        out_specs=[t_spec, t_spec, row, vec],
        out_shape=[lo, lo, jax.ShapeDtypeStruct((s_dim, d), F32), jax.ShapeDtypeStruct((1, d), F32)],
        scratch_shapes=[pltpu.VMEM((tm, d), F32)],
        compiler_params=_cparams(("arbitrary", "arbitrary"), est),
    )(df, wd, wg_t, wu_t, g, u, h, gain, dh_out, *order)


def _loss_head(y, target, name):
    s_dim, d = y.shape
    tm = _tile(s_dim, 512, 8)
    nt = s_dim // tm

    def body(y_ref, t_ref, dy_ref, loss_ref, acc):
        i = pl.program_id(0)

        @pl.when(i == 0)
        def _():
            acc[...] = jnp.zeros_like(acc)

        e = y_ref[...] - t_ref[...]
        dy_ref[...] = e * (1.0 / d)
        acc[...] += jnp.sum(e * e, axis=0, keepdims=True)

        @pl.when(i == nt - 1)
        def _():
            loss_ref[...] = jnp.sum(acc[...], axis=1, keepdims=True) * (0.5 / d)

    row = pl.BlockSpec((tm, d), lambda i: (i, 0))
    return pl.pallas_call(
        body, name=name, grid=(nt,), in_specs=[row, row],
        out_specs=[row, pl.BlockSpec((1, 1), lambda i: (0, 0))],
        out_shape=[jax.ShapeDtypeStruct((s_dim, d), F32), jax.ShapeDtypeStruct((1, 1), F32)],
        scratch_shapes=[pltpu.VMEM((1, d), F32)],
        compiler_params=_cparams(("arbitrary",), 8 * _nbytes((tm, d), F32)),
    )(y, target)


def _shift_down(v, sh, row):
    if sh == 0:
        return v
    return jnp.where(row >= sh, pltpu.roll(v, sh, 0), 0.0)


def _shift_up(v, sh, row):
    if sh == 0:
        return v
    n = v.shape[0]
    return jnp.where(row < n - sh, pltpu.roll(v, n - sh, 0), 0.0)


def _conv_fwd(gx, pconv, width, name):
    s_dim, cp2 = gx.shape
    cp = cp2 // 2
    nc = cp // LANES

    def body(x_ref, p_ref, o_ref):
        x = x_ref[...]
        row = lax.broadcasted_iota(jnp.int32, x.shape, 0)
        y = jnp.zeros_like(x) + p_ref[pl.ds(width, 1), :]
        for k in range(width):
            y = y + p_ref[pl.ds(k, 1), :] * _shift_down(x, width - 1 - k, row)
        o_ref[...] = y

    return pl.pallas_call(
        body, name=name, grid=(nc,),
        in_specs=[pl.BlockSpec((s_dim, LANES), lambda j: (0, nc + j)), pl.BlockSpec((8, LANES), lambda j: (0, j))],
        out_specs=pl.BlockSpec((s_dim, LANES), lambda j: (0, j)),
        out_shape=jax.ShapeDtypeStruct((s_dim, cp), F32),
        compiler_params=_cparams(("parallel",), 10 * _nbytes((s_dim, LANES), F32)),
    )(gx, pconv)


def _conv_bwd(d1, d2, gx, pconv, width, name):
    s_dim, cp = d1.shape
    nc = cp // LANES

    def body(d1_ref, d2_ref, x_ref, p_ref, dx_ref, dp_ref):
        d = d1_ref[...] + d2_ref[...]
        x = x_ref[...]
        row = lax.broadcasted_iota(jnp.int32, x.shape, 0)
        dx = jnp.zeros_like(d)
        dp_ref[...] = jnp.zeros_like(dp_ref)
        for k in range(width):
            sh = width - 1 - k
            dx = dx + p_ref[pl.ds(k, 1), :] * _shift_up(d, sh, row)
            dp_ref[pl.ds(k, 1), :] = jnp.sum(d * _shift_down(x, sh, row), axis=0, keepdims=True)
        dp_ref[pl.ds(width, 1), :] = jnp.sum(d, axis=0, keepdims=True)
        dx_ref[...] = dx.astype(MXU_DTYPE)

    strip = pl.BlockSpec((s_dim, LANES), lambda j: (0, j))
    par = pl.BlockSpec((8, LANES), lambda j: (0, j))
    return pl.pallas_call(
        body, name=name, grid=(nc,),
        in_specs=[strip, strip, pl.BlockSpec((s_dim, LANES), lambda j: (0, nc + j)), par],
        out_specs=[strip, par],
        out_shape=[jax.ShapeDtypeStruct((s_dim, cp), MXU_DTYPE), jax.ShapeDtypeStruct((8, cp), F32)],
        compiler_params=_cparams(("parallel",), 14 * _nbytes((s_dim, LANES), F32)),
    )(d1, d2, gx, pconv)


def _lru_coeffs(ra, ia, p_ref):
    r = _sigmoid(ra + p_ref[pl.ds(0, 1), :])
    i = _sigmoid(ia + p_ref[pl.ds(1, 1), :])
    sp = _softplus(-p_ref[pl.ds(2, 1), :])
    log_a = -LRU_C * r * sp
    a = jnp.exp(log_a)
    mult = jnp.sqrt(-_expm1(2.0 * log_a))
    return r, i, sp, a, mult


def _scan_fwd(gx, rec, gates, pvec, name):
    s_dim, cp = rec.shape
    ts = _tile(s_dim, 256, 8)
    nt = s_dim // ts

    def body(gate_ref, rec_ref, ra_ref, ia_ref, p_ref, h_ref, y_ref, a_s, u_s, carry):
        @pl.when(pl.program_id(0) == 0)
        def _():
            carry[...] = jnp.zeros_like(carry)

        rec_v = rec_ref[...]
        _, i, _, a, mult = _lru_coeffs(ra_ref[...], ia_ref[...], p_ref)
        a_s[...] = a
        u_s[...] = mult * (i * rec_v)

        def step(t, h):
            h = a_s[pl.ds(t, 1), :] * h + u_s[pl.ds(t, 1), :]
            h_ref[pl.ds(t, 1), :] = h
            return h

        carry[pl.ds(0, 1), :] = lax.fori_loop(0, ts, step, carry[pl.ds(0, 1), :], unroll=8)
        y_ref[...] = (_gelu(gate_ref[...]) * h_ref[...]).astype(MXU_DTYPE)

    blk = pl.BlockSpec((ts, cp), lambda t: (t, 0))
    return pl.pallas_call(
        body, name=name, grid=(nt,),
        in_specs=[blk, blk, blk, pl.BlockSpec((ts, cp), lambda t: (t, 1)), pl.BlockSpec((8, cp), lambda t: (0, 0))],
        out_specs=[blk, blk],
        out_shape=[jax.ShapeDtypeStruct((s_dim, cp), F32), jax.ShapeDtypeStruct((s_dim, cp), MXU_DTYPE)],
        scratch_shapes=[pltpu.VMEM((ts, cp), F32), pltpu.VMEM((ts, cp), F32), pltpu.VMEM((8, cp), F32)],
        compiler_params=_cparams(("arbitrary",), 14 * _nbytes((ts, cp), F32)),
    )(gx, rec, gates, gates, pvec)


def _scan_bwd(dy, gx, hrec, rec, gates, pvec, name):
    s_dim, cp = rec.shape
    ts = _tile(s_dim, 128, 8)
    nt = s_dim // ts

    def body(dy_ref, gate_ref, h_ref, hp_ref, rec_ref, ra_ref, ia_ref, p_ref,
             dgate_ref, dra_ref, dia_ref, drec_ref, dp_ref, a_s, d_s, carry):
        t_id = pl.program_id(0)

        @pl.when(t_id == 0)
        def _():
            carry[...] = jnp.zeros_like(carry)
            dp_ref[...] = jnp.zeros_like(dp_ref)

        rec_v = rec_ref[...]
        r, i, sp, a, mult = _lru_coeffs(ra_ref[...], ia_ref[...], p_ref)
        gate = gate_ref[...]
        dyv = dy_ref[...]
        h = h_ref[...]
        dgate_ref[...] = (dyv * h * _gelu_grad(gate)).astype(MXU_DTYPE)
        a_s[...] = a
        d_s[...] = dyv * _gelu(gate)

        def step(k, c):
            t = ts - 1 - k
            d = d_s[pl.ds(t, 1), :] + c
            d_s[pl.ds(t, 1), :] = d
            return a_s[pl.ds(t, 1), :] * d

        carry[pl.ds(0, 1), :] = lax.fori_loop(0, ts, step, carry[pl.ds(0, 1), :], unroll=8)
        dh = d_s[...]
        row = lax.broadcasted_iota(jnp.int32, h.shape, 0)
        first = jnp.where(t_id == nt - 1, 0.0, 1.0) * hp_ref[pl.ds(7, 1), :]
        h_prev = jnp.where(row == 0, first, pltpu.roll(h, 1, 0))
        dix = dh * mult
        dla = dh * h_prev * a - dh * (i * rec_v) * (a * a) / mult
        dra = dla * (-LRU_C * sp) * r * (1.0 - r)
        dia = dix * rec_v * i * (1.0 - i)
        dra_ref[...] = dra.astype(MXU_DTYPE)
        dia_ref[...] = dia.astype(MXU_DTYPE)
        drec_ref[...] = dix * i
        dsp = jnp.sum(dla * (-LRU_C * r), axis=0, keepdims=True)
        dp_ref[pl.ds(0, 1), :] += jnp.sum(dra, axis=0, keepdims=True)
        dp_ref[pl.ds(1, 1), :] += jnp.sum(dia, axis=0, keepdims=True)
        dp_ref[pl.ds(2, 1), :] += dsp * (-_sigmoid(-p_ref[pl.ds(2, 1), :]))

    blk = pl.BlockSpec((ts, cp), lambda t: (nt - 1 - t, 0))
    prev = pl.BlockSpec((8, cp), lambda t: (jnp.maximum((nt - 1 - t) * (ts // 8) - 1, 0), 0))
    par = pl.BlockSpec((8, cp), lambda t: (0, 0))
    lo = jax.ShapeDtypeStruct((s_dim, cp), MXU_DTYPE)
    return pl.pallas_call(
        body, name=name, grid=(nt,),
        in_specs=[blk, blk, blk, prev, blk, blk, pl.BlockSpec((ts, cp), lambda t: (nt - 1 - t, 1)), par],
        out_specs=[blk, blk, blk, blk, par],
        out_shape=[lo, lo, lo, jax.ShapeDtypeStruct((s_dim, cp), F32), jax.ShapeDtypeStruct((8, cp), F32)],
        scratch_shapes=[pltpu.VMEM((ts, cp), F32), pltpu.VMEM((ts, cp), F32), pltpu.VMEM((8, cp), F32)],
        compiler_params=_cparams(("arbitrary",), 40 * _nbytes((ts, cp), F32)),
    )(dy, gx, hrec, hrec, rec, gates, gates, pvec)


def _fgate_fwd(fpre, bias, name):
    s_dim, w = fpre.shape
    ts = _tile(s_dim, 512, 8)

    def body(f_ref, b_ref, c_ref, lf_s, carry):
        @pl.when(pl.program_id(0) == 0)
        def _():
            carry[...] = jnp.zeros_like(carry)

        lf_s[...] = -_softplus(-(f_ref[...] + b_ref[pl.ds(0, 1), :]))

        def step(t, c):
            c = c + lf_s[pl.ds(t, 1), :]
            c_ref[pl.ds(t, 1), :] = c
            return c

        carry[pl.ds(0, 1), :] = lax.fori_loop(0, ts, step, carry[pl.ds(0, 1), :], unroll=8)

    blk = pl.BlockSpec((ts, w), lambda t: (t, 0))
    return pl.pallas_call(
        body, name=name, grid=(s_dim // ts,),
        in_specs=[blk, pl.BlockSpec((8, w), lambda t: (0, 0))], out_specs=blk,
        out_shape=jax.ShapeDtypeStruct((s_dim, w), F32),
        scratch_shapes=[pltpu.VMEM((ts, w), F32), pltpu.VMEM((8, w), F32)],
        compiler_params=_cparams(("arbitrary",), 12 * _nbytes((ts, w), F32)),
    )(fpre, bias)


def _fgate_bwd(dc, fpre, bias, name):
    s_dim, w = fpre.shape
    ts = _tile(s_dim, 512, 8)
    nt = s_dim // ts

    def body(dc_ref, f_ref, b_ref, df_ref, db_ref, d_s, carry):
        @pl.when(pl.program_id(0) == 0)
        def _():
            carry[...] = jnp.zeros_like(carry)
            db_ref[...] = jnp.zeros_like(db_ref)

        d_s[...] = dc_ref[...]

        def step(k, c):
            t = ts - 1 - k
            c = c + d_s[pl.ds(t, 1), :]
            d_s[pl.ds(t, 1), :] = c
            return c

        carry[pl.ds(0, 1), :] = lax.fori_loop(0, ts, step, carry[pl.ds(0, 1), :], unroll=8)
        df = d_s[...] * _sigmoid(-(f_ref[...] + b_ref[pl.ds(0, 1), :]))
        df_ref[...] = df
        db_ref[pl.ds(0, 1), :] += jnp.sum(df, axis=0, keepdims=True)

    blk = pl.BlockSpec((ts, w), lambda t: (nt - 1 - t, 0))
    par = pl.BlockSpec((8, w), lambda t: (0, 0))
    return pl.pallas_call(
        body, name=name, grid=(nt,), in_specs=[blk, blk, par], out_specs=[blk, par],
        out_shape=[jax.ShapeDtypeStruct((s_dim, w), F32), jax.ShapeDtypeStruct((8, w), F32)],
        scratch_shapes=[pltpu.VMEM((ts, w), F32), pltpu.VMEM((8, w), F32)],
        compiler_params=_cparams(("arbitrary",), 12 * _nbytes((ts, w), F32)),
    )(dc, fpre, bias)


def _head_lanes(hh, dh):
    lane = lax.broadcasted_iota(jnp.int32, (1, LANES), 1)
    return (lane >= hh * dh) & (lane < (hh + 1) * dh)


def _pair_attn_fwd(q, kv, v_t, c_col, c_row, name):
    s_dim, da = q.shape
    n_h = c_col.shape[0]
    dh = da // n_h
    assert LANES % dh == 0 and da % LANES == 0
    hb = LANES // dh
    n_blocks = da // LANES
    t = _tile(s_dim, 512, LANES)
    nb = s_dim // t

    pairs = [(i, j) for i in range(nb) for j in range(i + 1)]
    i_tab = jnp.asarray([p[0] for p in pairs], jnp.int32)
    j_tab = jnp.asarray([p[1] for p in pairs], jnp.int32)

    def body(i_ref, j_ref, q_ref, k_ref, vt_ref, cq_ref, ck_ref, o_ref, lse_ref, m_s, l_s, acc):
        i, j = i_ref[pl.program_id(1)], j_ref[pl.program_id(1)]

        @pl.when(j == 0)
        def _():
            m_s[...] = jnp.full_like(m_s, -jnp.inf)
            l_s[...] = jnp.zeros_like(l_s)
            acc[...] = jnp.zeros_like(acc)

        def tile(masked):
            qv = q_ref[...]
            for hh in range(hb):
                st = _dot(k_ref[...], jnp.where(_head_lanes(hh, dh), qv, jnp.zeros_like(qv)), "nt")
                st = st + (cq_ref[hh] - ck_ref[hh])
                if masked:
                    keep = lax.broadcasted_iota(jnp.int32, (t, t), 0) <= lax.broadcasted_iota(jnp.int32, (t, t), 1)
                    st = jnp.where(keep, st, -jnp.inf)
                m_prev = m_s[hh]
                m_new = jnp.maximum(m_prev, jnp.max(st, axis=0, keepdims=True))
                alpha = jnp.exp(m_prev - m_new)
                p = jnp.exp(st - m_new)
                l_s[hh] = alpha * l_s[hh] + jnp.sum(p, axis=0, keepdims=True)
                acc[hh] = alpha * acc[hh] + _dot(vt_ref[...], p, "nn")
                m_s[hh] = m_new

        pl.when(j < i)(functools.partial(tile, False))
        pl.when(j == i)(functools.partial(tile, True))

        @pl.when(j == i)
        def _():
            feat = lax.broadcasted_iota(jnp.int32, (LANES, 1), 0)
            out_t = jnp.zeros((LANES, t), F32)
            for hh in range(hb):
                out_t = jnp.where((feat >= hh * dh) & (feat < (hh + 1) * dh), acc[hh] / l_s[hh], out_t)
                lse_ref[hh] = m_s[hh] + jnp.log(l_s[hh])
            o_ref[...] = out_t.T

    q_spec = pl.BlockSpec((t, LANES), lambda b, p, it, jt: (it[p], b))
    k_spec = pl.BlockSpec((t, LANES), lambda b, p, it, jt: (jt[p], b))
    vt_spec = pl.BlockSpec((LANES, t), lambda b, p, it, jt: (b, jt[p]))
    cq_spec = pl.BlockSpec((hb, 1, t), lambda b, p, it, jt: (b, 0, it[p]))
    ck_spec = pl.BlockSpec((hb, t, 1), lambda b, p, it, jt: (b, jt[p], 0))
    return pl.pallas_call(
        body, name=name,
        grid_spec=pltpu.PrefetchScalarGridSpec(
            num_scalar_prefetch=2, grid=(n_blocks, len(pairs)),
            in_specs=[q_spec, k_spec, vt_spec, cq_spec, ck_spec], out_specs=[q_spec, cq_spec],
            scratch_shapes=[pltpu.VMEM((hb, 1, t), F32), pltpu.VMEM((hb, 1, t), F32), pltpu.VMEM((hb, LANES, t), F32)]),
        out_shape=[jax.ShapeDtypeStruct((s_dim, da), F32), jax.ShapeDtypeStruct((n_h, 1, s_dim), F32)],
        compiler_params=_cparams(("parallel", "arbitrary"), 10 * hb * _nbytes((t, t), F32)),
    )(i_tab, j_tab, q, kv, v_t, c_row, c_col)


def _attn_delta(do, o, n_h, name):
    s_dim, da = o.shape
    dh = da // n_h
    hb = LANES // dh
    t = _tile(s_dim, 512, LANES)

    def body(do_ref, o_ref, d_ref):
        prod_t = (do_ref[...].astype(MXU_DTYPE).astype(F32) * o_ref[...]).T
        for hh in range(hb):
            d_ref[hh] = jnp.sum(prod_t[hh * dh:(hh + 1) * dh], axis=0, keepdims=True)

    blk = pl.BlockSpec((t, LANES), lambda b, i: (i, b))
    return pl.pallas_call(
        body, name=name, grid=(da // LANES, s_dim // t), in_specs=[blk, blk],
        out_specs=pl.BlockSpec((hb, 1, t), lambda b, i: (b, 0, i)),
        out_shape=jax.ShapeDtypeStruct((n_h, 1, s_dim), F32),
        compiler_params=_cparams(("parallel", "parallel"), 8 * _nbytes((t, LANES), F32)),
    )(do, o)


def _pair_attn_bwd(q, kv, c_col, c_row, lse, delta, do, scale, name):
    s_dim, da = q.shape
    n_h = c_col.shape[0]
    dh = da // n_h
    hb = LANES // dh
    n_blocks = da // LANES
    t = _tile(s_dim, 512, LANES)
    nb = s_dim // t

    pairs = [(i, j) for j in range(nb) for i in range(j, nb)]
    i_tab = jnp.asarray([p[0] for p in pairs], jnp.int32)
    j_tab = jnp.asarray([p[1] for p in pairs], jnp.int32)

    def body(i_ref, j_ref, q_ref, k_ref, v_ref, cq_ref, ck_ref, lse_ref, dl_ref, do_ref,
             dq_ref, dcq_ref, dk_ref, dv_ref, dck_ref, dk_acc, dv_acc, dck_acc):
        i, j = i_ref[pl.program_id(1)], j_ref[pl.program_id(1)]

        @pl.when(pl.program_id(1) == 0)
        def _():
            dq_ref[...] = jnp.zeros_like(dq_ref)
            dcq_ref[...] = jnp.zeros_like(dcq_ref)

        @pl.when(i == j)
        def _():
            dk_acc[...] = jnp.zeros_like(dk_acc)
            dv_acc[...] = jnp.zeros_like(dv_acc)
            dck_acc[...] = jnp.zeros_like(dck_acc)

        def tile(masked):
            start = pl.multiple_of(i * t, t)
            qv, kv_ = q_ref[...], k_ref[...]
            dov = do_ref[...].astype(MXU_DTYPE)
            for hh in range(hb):
                lanes = _head_lanes(hh, dh)
                qm = jnp.where(lanes, qv, jnp.zeros_like(qv))
                km = jnp.where(lanes, kv_, jnp.zeros_like(kv_))
                dom = jnp.where(lanes, dov, jnp.zeros_like(dov))
                st = _dot(kv_, qm, "nt") + (cq_ref[hh] - ck_ref[hh])
                if masked:
                    keep = lax.broadcasted_iota(jnp.int32, (t, t), 0) <= lax.broadcasted_iota(jnp.int32, (t, t), 1)
                    st = jnp.where(keep, st, -jnp.inf)
                pt = jnp.exp(st - lse_ref[hh])
                dst = pt * (_dot(v_ref[...], dom, "nt") - dl_ref[hh])
                dv_acc[...] += _dot(pt, dom, "nn")
                dk_acc[...] += _dot(dst, qm, "nn")
                dq_ref[pl.ds(start, t), :] += _dot(dst, km, "tn") * scale
                dcq_ref[hh, :, pl.ds(start, t)] += jnp.sum(dst, axis=0, keepdims=True)
                dck_acc[hh] -= jnp.sum(dst, axis=1, keepdims=True)

        pl.when(i > j)(functools.partial(tile, False))
        pl.when(i == j)(functools.partial(tile, True))

        @pl.when(i == nb - 1)
        def _():
            dk_ref[...] = dk_acc[...]
            dv_ref[...] = dv_acc[...]
            dck_ref[...] = dck_acc[...]

    q_spec = pl.BlockSpec((t, LANES), lambda b, p, it, jt: (it[p], b))
    qrow_spec = pl.BlockSpec((hb, 1, t), lambda b, p, it, jt: (b, 0, it[p]))
    k_spec = pl.BlockSpec((t, LANES), lambda b, p, it, jt: (jt[p], b))
    v_spec = pl.BlockSpec((t, LANES), lambda b, p, it, jt: (jt[p], n_blocks + b))
    kcol_spec = pl.BlockSpec((hb, t, 1), lambda b, p, it, jt: (b, jt[p], 0))
    wide = jax.ShapeDtypeStruct((s_dim, da), F32)
    return pl.pallas_call(
        body, name=name,
        grid_spec=pltpu.PrefetchScalarGridSpec(
            num_scalar_prefetch=2, grid=(n_blocks, len(pairs)),
            in_specs=[q_spec, k_spec, v_spec, qrow_spec, kcol_spec, qrow_spec, qrow_spec, q_spec],
            out_specs=[pl.BlockSpec((s_dim, LANES), lambda b, p, it, jt: (0, b)),
                       pl.BlockSpec((hb, 1, s_dim), lambda b, p, it, jt: (b, 0, 0)), k_spec, k_spec, kcol_spec],
            scratch_shapes=[pltpu.VMEM((t, LANES), F32), pltpu.VMEM((t, LANES), F32), pltpu.VMEM((hb, t, 1), F32)]),
        out_shape=[wide, jax.ShapeDtypeStruct((n_h, 1, s_dim), F32), wide, wide,
                   jax.ShapeDtypeStruct((n_h, s_dim, 1), F32)],
        compiler_params=_cparams(("parallel", "arbitrary"),
                                 10 * hb * _nbytes((t, t), F32) + 4 * _nbytes((s_dim, LANES), F32)),
    )(i_tab, j_tab, q, kv, kv, c_row, c_col, lse, delta, do)


_HBM = pl.BlockSpec(memory_space=pltpu.HBM)
_MESH_ID = pl.DeviceIdType.MESH


def _all_gather(block, name):
    r, w = block.shape

    def body(x_ref, out_ref, send_sems, recv_sems, local_sem):
        x, y, c = lax.axis_index("x"), lax.axis_index("y"), lax.axis_index("c")
        me, sibling = (x, y, c), (x, y, 1 - c)
        chips = [(1 - x, y), (x, 1 - y), (1 - x, 1 - y)]

        def slot(px, py, pc):
            return out_ref.at[4 * px + 2 * py + pc]

        def copy(k, blk, to, src=None):
            return pltpu.make_async_remote_copy(
                src_ref=slot(*blk) if src is None else src, dst_ref=slot(*blk),
                send_sem=send_sems.at[k], recv_sem=recv_sems.at[k], device_id=to, device_id_type=_MESH_ID)

        mine = pltpu.make_async_copy(x_ref, slot(*me), local_sem)
        mine.start()
        first = [copy(0, me, sibling, src=x_ref)]
        first += [copy(1 + n, me, (*chip, c), src=x_ref) for n, chip in enumerate(chips)]
        for cp in first:
            cp.start()
        passed = [copy(4 + n, (*chip, c), sibling) for n, chip in enumerate(chips)]
        for n, chip in enumerate(chips):
            copy(1 + n, (*chip, c), me).wait_recv()
            passed[n].start()
        copy(0, sibling, me).wait_recv()
        for n, chip in enumerate(chips):
            copy(4 + n, (*chip, 1 - c), me).wait_recv()
        for cp in first + passed:
            cp.wait_send()
        mine.wait()

    return pl.pallas_call(
        body, name=name, out_shape=jax.ShapeDtypeStruct((N_DEV, r, w), block.dtype),
        in_specs=[_HBM], out_specs=_HBM,
        scratch_shapes=[pltpu.SemaphoreType.DMA((7,)), pltpu.SemaphoreType.DMA((7,)), pltpu.SemaphoreType.DMA],
    )(block)


_SEM = pl.BlockSpec(memory_space=pltpu.SEMAPHORE)
_EFFECT = pltpu.SideEffectType.DATAFLOW_SIDE_EFFECTING


def _exchange_start(srcs, personalized, after, name):
    n = len(srcs)
    n_after = len(after)
    lands = [lax.empty((N_DEV,) + s.shape[-2:], s.dtype) for s in srcs]

    def body(*refs):
        src_refs, land_refs = refs[:n], refs[n:2 * n]
        outs = refs[2 * n + n_after:]
        send_sems, recv_sems, token = outs[:n], outs[n:2 * n], outs[-1]
        x, y, c = lax.axis_index("x"), lax.axis_index("y"), lax.axis_index("c")
        mine = 4 * x + 2 * y + c
        for ci in range(n):
            for k in range(1, N_DEV):
                px = 1 - x if k & 4 else x
                py = 1 - y if k & 2 else y
                pc = 1 - c if k & 1 else c
                src = src_refs[ci].at[4 * px + 2 * py + pc] if personalized else src_refs[ci]
                pltpu.make_async_remote_copy(
                    src_ref=src, dst_ref=land_refs[ci].at[mine], send_sem=send_sems[ci], recv_sem=recv_sems[ci],
                    device_id=(px, py, pc), device_id_type=_MESH_ID).start()
        token[...] = jnp.zeros_like(token)

    sem = pltpu.SemaphoreType.DMA(())
    out_shape = ([sem] * (2 * n) + [pltpu.HBM(s.shape, s.dtype) for s in srcs]
                 + [pltpu.HBM(l.shape, l.dtype) for l in lands] + [jax.ShapeDtypeStruct((8, LANES), F32)])
    res = pl.pallas_call(
        body, name=name, out_shape=tuple(out_shape),
        in_specs=[_HBM] * (2 * n) + [_ANY] * n_after,
        out_specs=tuple([_SEM] * (2 * n) + [_HBM] * (2 * n) + [pl.BlockSpec(memory_space=pltpu.VMEM)]),
        input_output_aliases={i: 2 * n + i for i in range(2 * n)},
        compiler_params=pltpu.CompilerParams(has_side_effects=_EFFECT),
    )(*[pltpu.with_memory_space_constraint(s, pltpu.HBM) for s in srcs],
      *[pltpu.with_memory_space_constraint(l, pltpu.HBM) for l in lands], *after)
    handles = [(res[ci], res[n + ci], res[2 * n + ci], res[3 * n + ci]) for ci in range(n)]
    return handles, res[-1]


def _exchange_wait(handle, after, name):
    send_sem, recv_sem, src_thru, land_thru = handle

    def body(src_ref, land_ref, send_ref, recv_ref, after_ref, src_out, land_out):
        seven = land_ref.at[pl.ds(0, N_DEV - 1)]
        copies = pltpu.make_async_remote_copy(
            src_ref=seven, dst_ref=seven, send_sem=send_ref, recv_sem=recv_ref,
            device_id=(lax.axis_index("x"), lax.axis_index("y"), lax.axis_index("c")), device_id_type=_MESH_ID)
        copies.wait_send()
        copies.wait_recv()

    return pl.pallas_call(
        body, name=name,
        out_shape=(pltpu.HBM(src_thru.shape, src_thru.dtype), pltpu.HBM(land_thru.shape, land_thru.dtype)),
        in_specs=(_HBM, _HBM, _SEM, _SEM, _ANY), out_specs=(_HBM, _HBM), input_output_aliases={0: 0, 1: 1},
        compiler_params=pltpu.CompilerParams(has_side_effects=_EFFECT),
    )(src_thru, land_thru, send_sem, recv_sem, after)[1]


def _own_slot(land, own, me):
    return lax.dynamic_update_index_in_dim(land, own, me, axis=0)


def _sum_slots(slots, name):
    n, r, w = slots.shape
    tr = _tile(r, 128, WIRE_ROW_ALIGN)

    def body(s_ref, o_ref):
        acc = s_ref[0].astype(F32)
        for d in range(1, n):
            acc = acc + s_ref[d].astype(F32)
        o_ref[...] = acc

    return pl.pallas_call(
        body, name=name, grid=(r // tr,),
        in_specs=[pl.BlockSpec((n, tr, w), lambda i: (0, i, 0))],
        out_specs=pl.BlockSpec((tr, w), lambda i: (i, 0)),
        out_shape=jax.ShapeDtypeStruct((r, w), F32),
        compiler_params=_cparams(("parallel",), 2 * _nbytes((n, tr, w), slots.dtype) + 4 * _nbytes((tr, w), F32)),
    )(slots)


def _adamw(w, g, m, v, name):
    r, c = w.shape
    tr = _tile(r, 512, 8)

    def body(w_ref, g_ref, m_ref, v_ref, d_ref, mo_ref, vo_ref):
        gv = g_ref[...]
        m_new = ADAM_B1 * m_ref[...] + (1.0 - ADAM_B1) * gv
        v_new = ADAM_B2 * v_ref[...] + (1.0 - ADAM_B2) * (gv * gv)
        m_hat = m_new / (1.0 - ADAM_B1 ** ADAM_STEP)
        v_hat = v_new / (1.0 - ADAM_B2 ** ADAM_STEP)
        d_ref[...] = -ADAM_LR * (m_hat / (jnp.sqrt(v_hat) + ADAM_EPS) + ADAM_WD * w_ref[...])
        mo_ref[...] = m_new
        vo_ref[...] = v_new

    blk = pl.BlockSpec((tr, c), lambda i: (i, 0))
    shp = jax.ShapeDtypeStruct((r, c), F32)
    return pl.pallas_call(
        body, name=name, grid=(r // tr,), in_specs=[blk] * 4, out_specs=[blk] * 3, out_shape=[shp] * 3,
        compiler_params=_cparams(("parallel",), 16 * _nbytes((tr, _round_up(c, LANES)), F32)),
    )(w, g, m, v)


def _pack_rows(parts, width, dtype, row_align):
    rows, spans, off = [], [], 0
    for p in parts:
        flat = p.reshape(-1).astype(dtype)
        n_rows = _round_up(-(-flat.shape[0] // width), row_align)
        flat = jnp.pad(flat, (0, n_rows * width - flat.shape[0]))
        rows.append(flat.reshape(n_rows, width))
        spans.append((off, n_rows))
        off += n_rows
    return jnp.concatenate(rows, axis=0), spans


def _unpack_rows(mat, span, shape):
    off, n_rows = span
    n = 1
    for s in shape:
        n *= s
    return mat[..., off:off + n_rows, :].reshape(mat.shape[:-2] + (-1,))[..., :n].reshape(mat.shape[:-2] + tuple(shape))


def _block_diag(w, size):
    n, b, _ = w.shape
    eye = jnp.eye(n, dtype=w.dtype)
    dense = (w[:, :, None, :] * eye[:, None, :, None]).reshape(n * b, n * b)
    return jnp.pad(dense, ((0, size - n * b), (0, size - n * b)))


def _diag_blocks(dense, n, b):
    return jnp.stack([dense[k * b:(k + 1) * b, k * b:(k + 1) * b] for k in range(n)])


def _pad_rows(a, rows):
    return jnp.pad(a, ((0, rows - a.shape[0]), (0, 0)))


def _pad_cols(a, cols):
    return jnp.pad(a, ((0, 0), (0, cols - a.shape[1])))


def _train_step(a):
    x = a["x"][0]
    target = a["loss_target"][0]
    s_dim, d = x.shape
    n_layers = a["ffn1_pre_g"].shape[0]
    f_shard = a["ffn1_w_gate"].shape[2]
    c_shard = a["rg_conv_b"].shape[1]
    c_dim = c_shard * N_DEV
    cp = _round_up(c_dim, LANES)
    conv_width = a["rg_conv_w"].shape[1]
    n_blocks, lru_block = a["rg_w_a"].shape[1], a["rg_w_a"].shape[2]
    d_attn = a["attn_w_q"].shape[2]
    n_heads = a["b_fgate"].shape[0]
    d_head = d_attn // n_heads
    attn_scale = d_head ** -0.5
    assert conv_width < 8 and n_heads <= LANES and n_layers == 2
    assert d_attn == d
    me = 4 * lax.axis_index("x") + 2 * lax.axis_index("y") + lax.axis_index("c")

    shard = {"rg_w_in": a["rg_w_in"][0].T, "rg_w_out": a["rg_w_out"][0], "w_kv": a["w_kv"].T,
             "attn_w_q": a["attn_w_q"][0], "attn_w_o": a["attn_w_o"][0]}
    for l in range(n_layers):
        for f in ("ffn1", "ffn2"):
            shard[(f, "gate", l)] = a[f + "_w_gate"][l].T
            shard[(f, "up", l)] = a[f + "_w_up"][l].T
            shard[(f, "down", l)] = a[f + "_w_down"][l]

    def ffn_names(f, l):
        return [(f, "gate", l), (f, "up", l), (f, "down", l)]

    def chunk_layout(names):
        spans, off = [], 0
        for nm in names:
            spans.append((nm, off, shard[nm].shape[0]))
            off += _round_up(shard[nm].shape[0], WIRE_ROW_ALIGN)
        return spans, off

    def pack_chunk(names, parts):
        return jnp.concatenate(
            [_pad_rows(parts[nm].astype(WIRE_DTYPE), _round_up(parts[nm].shape[0], WIRE_ROW_ALIGN)) for nm in names], axis=0)

    full = {}

    def unpack_chunk(names, gathered):
        for nm, o, n_rows in chunk_layout(names)[0]:
            full[nm] = gathered[:, o:o + n_rows, :].reshape(N_DEV * n_rows, d)

    fwd_chunks = [ffn_names("ffn1", 0), ["rg_w_in", "rg_w_out"], ffn_names("ffn2", 0) + ["w_kv"],
                  ffn_names("ffn1", 1) + ["attn_w_q", "attn_w_o"], ffn_names("ffn2", 1)]
    fwd_packs = [pack_chunk(names, shard) for names in fwd_chunks]
    unpack_chunk(fwd_chunks[0], _all_gather(fwd_packs[0], "gather_weights_first"))

    small_parts = [a["rg_conv_w"][0], a["rg_conv_b"][0], a["rg_b_a"][0], a["rg_b_x"][0], a["rg_lambda"][0], a["w_fgate"]]
    small_pack, small_spans = _pack_rows(small_parts, d, F32, 8)
    small_all = _all_gather(small_pack, "gather_small")
    fwd_handles, fwd_token = _exchange_start(fwd_packs[1:], False, [full[("ffn1", "down", 0)], small_all],
                                             "gather_weights_start")

    def land_weights(n, after):
        land = _exchange_wait(fwd_handles[n - 1], after, f"gather_weights_wait_{n}")
        unpack_chunk(fwd_chunks[n], _own_slot(land, fwd_packs[n], me))

    sm = [_unpack_rows(small_all, sp, p.shape) for sp, p in zip(small_spans, small_parts)]
    conv_w = jnp.moveaxis(sm[0], 0, 1).reshape(conv_width, c_dim)
    conv_b, b_a, b_x, lam = (v.reshape(1, c_dim) for v in sm[1:5])
    w_f = sm[5].reshape(d, n_heads)

    pconv = _pad_rows(_pad_cols(jnp.concatenate([conv_w, conv_b], axis=0), cp), 8)
    pvec = _pad_rows(_pad_cols(jnp.concatenate([b_a, b_x, lam], axis=0), cp), 8)
    wa_dense = _block_diag(a["rg_w_a"][0], cp).astype(MXU_DTYPE)
    wx_dense = _block_diag(a["rg_w_x"][0], cp).astype(MXU_DTYPE)
    wax = jnp.concatenate([wa_dense, wx_dense], axis=1)
    w_f_t = _pad_rows(w_f.T.astype(MXU_DTYPE), LANES)
    b_f = _pad_rows(_pad_cols(a["b_fgate"].reshape(1, n_heads), LANES), 8)

    def gain(name, l):
        return a[name][l].reshape(1, d)

    def ffn_fwd(h, f, l, after=None):
        xn = _rms_fwd(h, gain(f + "_pre_g", l), f"{f}_{l}_pre_norm", after)
        g, u, act = _ffn_up(xn, full[(f, "gate", l)], full[(f, "up", l)], f"{f}_{l}_up")
        fo, h_new = _mm_rms_res(act, full[(f, "down", l)], h, gain(f + "_post_g", l), 0.5, f"{f}_{l}_down")
        return h_new, (h, xn, g, u, act, fo)

    h0 = x
    h0a, sv_f1_0 = ffn_fwd(h0, "ffn1", 0, fwd_token)
    land_weights(1, h0a)
    w_in_gate = _pad_rows(full["rg_w_in"][:c_dim], cp)
    w_in_rec = _pad_rows(full["rg_w_in"][c_dim:], cp)
    w_in_t = jnp.concatenate([w_in_gate, w_in_rec], axis=0)
    w_out = _pad_rows(full["rg_w_out"], cp)
    hn_rg = _rms_fwd(h0a, gain("mix_pre_g", 0), "rg_pre_norm")
    gx = _mm([(hn_rg, w_in_t)], "nt", F32, "rg_in_proj")
    rec = _conv_fwd(gx, pconv, conv_width, "rg_conv")
    gates = _mm([(rec, wax)], "nn", F32, "rg_gate_proj")
    h_rec, y_rg = _scan_fwd(gx, rec, gates, pvec, "rg_scan")
    m_rg, h0b = _mm_rms_res(y_rg, w_out, h0a, gain("mix_post_g", 0), 1.0, "rg_out_proj")
    land_weights(2, h0b)
    h1, sv_f2_0 = ffn_fwd(h0b, "ffn2", 0)
    hn_kv = _rms_fwd(h1, a["kv_norm_g"].reshape(1, d), "kv_norm")
    kv = _mm([(hn_kv, full["w_kv"])], "nt", MXU_DTYPE, "kv_proj")
    fpre = _mm([(hn_kv, w_f_t)], "nt", F32, "fgate_proj")
    c_cum = _fgate_fwd(fpre, b_f, "fgate_cumsum")
    c_heads = c_cum[:, :n_heads].T
    c_col, c_row = c_heads[:, :, None], c_heads[:, None, :]
    land_weights(3, c_cum)
    h1a, sv_f1_1 = ffn_fwd(h1, "ffn1", 1)
    hn_at = _rms_fwd(h1a, gain("mix_pre_g", 1), "attn_pre_norm")
    q_s = _mm([(hn_at, full["attn_w_q"])], "nn", MXU_DTYPE, "q_proj", out_scale=attn_scale)
    o2, lse = _pair_attn_fwd(q_s, kv, kv[:, d_attn:].T, c_col, c_row, "attn_fwd")
    m_at, h1b = _mm_rms_res(o2, full["attn_w_o"], h1a, gain("mix_post_g", 1), 1.0, "attn_out_proj")
    land_weights(4, h1b)
    y, sv_f2_1 = ffn_fwd(h1b, "ffn2", 1)
    dy, loss_part = _loss_head(y, target, "loss_head")

    grads_big = {}
    grads_rep = {}

    bwd_chunks = [ffn_names("ffn2", 1), ["attn_w_q", "attn_w_o"] + ffn_names("ffn1", 1),
                  ["w_kv"] + ffn_names("ffn2", 0), ["rg_w_in", "rg_w_out"], ffn_names("ffn1", 0)]
    bwd_sends, bwd_handles = [], []

    def send_grads(after):
        n = len(bwd_sends)
        send = jnp.concatenate(
            [jnp.pad(grads_big[nm].reshape(N_DEV, n_rows, d), ((0, 0), (0, _round_up(n_rows, WIRE_ROW_ALIGN) - n_rows), (0, 0)))
             for nm, _, n_rows in chunk_layout(bwd_chunks[n])[0]], axis=1)
        handles, token = _exchange_start([send], True, [after], f"exchange_grads_start_{n}")
        bwd_sends.append(send)
        bwd_handles.append(handles[0])
        return token

    def ffn_bwd(dh_out, saved, f, l, after=None, send_now=False):
        h, xn, g, u, act, fo = saved
        df, d_post = _rms_bwd(fo, gain(f + "_post_g", l), [dh_out], None, 0.5, MXU_DTYPE, f"{f}_{l}_post_norm_bwd", after)
        w_gate, w_up, w_down = full[(f, "gate", l)], full[(f, "up", l)], full[(f, "down", l)]
        if send_now:
            dg, du = _ffn_act_bwd(df, w_down, g, u, f"{f}_{l}_act_bwd")
        else:
            dg, du, dh_in, d_pre = _ffn_bwd_fused(df, w_down, w_gate, w_up, g, u, h, gain(f + "_pre_g", l), dh_out,
                                                  f"{f}_{l}_bwd_fused")
        grads_big[(f, "down", l)] = _mm([(act, df)], "tn", WIRE_DTYPE, f"{f}_{l}_dw_down")
        grads_big[(f, "gate", l)] = _mm([(dg, xn)], "tn", WIRE_DTYPE, f"{f}_{l}_dw_gate")
        grads_big[(f, "up", l)] = _mm([(du, xn)], "tn", WIRE_DTYPE, f"{f}_{l}_dw_up")
        if send_now:
            sent = send_grads(df)
            dxn = _mm([(dg, w_gate), (du, w_up)], "nn", F32, f"{f}_{l}_dx", sent)
            dh_in, d_pre = _rms_bwd(h, gain(f + "_pre_g", l), [dxn], dh_out, 1.0, F32, f"{f}_{l}_pre_norm_bwd")
        grads_rep[(f + "_post_g", l)] = d_post
        grads_rep[(f + "_pre_g", l)] = d_pre
        return dh_in

    dh = ffn_bwd(dy, sv_f2_1, "ffn2", 1)
    token = send_grads(dh)
    dm, d_post = _rms_bwd(m_at, gain("mix_post_g", 1), [dh], None, 1.0, MXU_DTYPE, "attn_post_norm_bwd", token)
    grads_rep[("mix_post_g", 1)] = d_post
    do2 = _mm([(dm, full["attn_w_o"])], "nt", F32, "attn_out_proj_dx")
    grads_big["attn_w_o"] = _mm([(o2, dm)], "tn", WIRE_DTYPE, "attn_out_proj_dw")
    delta = _attn_delta(do2, o2, n_heads, "attn_delta")
    dq2, dc_q, dk2, dv2, dc_k = _pair_attn_bwd(q_s, kv, c_col, c_row, lse, delta, do2, attn_scale, "attn_bwd")
    dc_heads = dc_q[:, 0, :] + dc_k[:, :, 0]
    dhn = _mm([(dq2, full["attn_w_q"])], "nt", F32, "q_proj_dx")
    grads_big["attn_w_q"] = _mm([(hn_at, dq2)], "tn", WIRE_DTYPE, "q_proj_dw")
    dh, d_pre = _rms_bwd(h1a, gain("mix_pre_g", 1), [dhn], dh, 1.0, F32, "attn_pre_norm_bwd")
    grads_rep[("mix_pre_g", 1)] = d_pre
    dh = ffn_bwd(dh, sv_f1_1, "ffn1", 1)
    token = send_grads(dh)
    dc_cum = _pad_cols(dc_heads.T, LANES)
    dfpre, db_f = _fgate_bwd(dc_cum, fpre, b_f, "fgate_cumsum_bwd")
    dhn_kv = _mm([(dk2, full["w_kv"][:d_attn]), (dv2, full["w_kv"][d_attn:])], "nn", F32, "kv_proj_dx")
    dhn_f = _mm([(dfpre, w_f_t)], "nn", F32, "fgate_proj_dx")
    grads_big["w_kv"] = jnp.concatenate([_mm([(dk2, hn_kv)], "tn", WIRE_DTYPE, "kv_proj_dw_k"),
                                         _mm([(dv2, hn_kv)], "tn", WIRE_DTYPE, "kv_proj_dw_v")], axis=0)
    dw_f_t = _mm([(dfpre, hn_kv)], "tn", F32, "fgate_proj_dw")
    dh, d_kvg = _rms_bwd(h1, a["kv_norm_g"].reshape(1, d), [dhn_kv, dhn_f], dh, 1.0, F32, "kv_norm_bwd", token)
    dh = ffn_bwd(dh, sv_f2_0, "ffn2", 0)
    token = send_grads(dh)
    dm, d_post = _rms_bwd(m_rg, gain("mix_post_g", 0), [dh], None, 1.0, MXU_DTYPE, "rg_post_norm_bwd", token)
    grads_rep[("mix_post_g", 0)] = d_post
    dy_rg = _mm([(dm, w_out)], "nt", F32, "rg_out_proj_dx")
    dw_out = _mm([(y_rg, dm)], "tn", WIRE_DTYPE, "rg_out_proj_dw")
    dgate, dra, dia, drec1, dpvec = _scan_bwd(dy_rg, gx, h_rec, rec, gates, pvec, "rg_scan_bwd")
    drec2 = _mm([(dra, wa_dense), (dia, wx_dense)], "nt", F32, "rg_gate_proj_dx")
    dwa_dense = _mm([(rec, dra)], "tn", F32, "rg_gate_proj_dwa")
    dwx_dense = _mm([(rec, dia)], "tn", F32, "rg_gate_proj_dwx")
    drec0, dpconv = _conv_bwd(drec1, drec2, gx, pconv, conv_width, "rg_conv_bwd")
    dhn = _mm([(dgate, w_in_gate), (drec0, w_in_rec)], "nn", F32, "rg_in_proj_dx")
    dw_in_gate = _mm([(dgate, hn_rg)], "tn", WIRE_DTYPE, "rg_in_proj_dw_gate")
    dw_in_rec = _mm([(drec0, hn_rg)], "tn", WIRE_DTYPE, "rg_in_proj_dw_rec")
    dh, d_pre = _rms_bwd(h0a, gain("mix_pre_g", 0), [dhn], dh, 1.0, F32, "rg_pre_norm_bwd")
    grads_rep[("mix_pre_g", 0)] = d_pre
    grads_big["rg_w_in"] = jnp.concatenate([dw_in_gate[:c_dim], dw_in_rec[:c_dim]], axis=0)
    grads_big["rg_w_out"] = dw_out[:c_dim]
    token = send_grads(dh)
    grad_x = ffn_bwd(dh, sv_f1_0, "ffn1", 0, token, send_now=True)

    g_shard = {}

    def land_grads(n, after):
        land = _exchange_wait(bwd_handles[n], after, f"exchange_grads_wait_{n}")
        own = lax.dynamic_index_in_dim(bwd_sends[n], me, axis=0, keepdims=False)
        g_chunk = _sum_slots(_own_slot(land, own, me), f"sum_weight_grads_{n}")
        for nm, o, n_rows in chunk_layout(bwd_chunks[n])[0]:
            g_shard[nm] = g_chunk[o:o + n_rows]

    for n in range(len(bwd_chunks) - 1):
        land_grads(n, grad_x)

    def gain_grad(name):
        return jnp.concatenate([grads_rep[(name, l)] for l in range(n_layers)], axis=0)

    rep_names = ["ffn1_pre_g", "ffn1_post_g", "mix_pre_g", "mix_post_g", "ffn2_pre_g", "ffn2_post_g"]
    rep_parts = [gain_grad(nm) for nm in rep_names]
    rep_names += ["kv_norm_g", "b_fgate", "rg_w_a", "rg_w_x", "rg_conv_w", "rg_conv_b", "rg_b_a", "rg_b_x", "rg_lambda", "w_fgate"]
    rep_parts += [
        d_kvg, db_f[0, :n_heads],
        _diag_blocks(dwa_dense, n_blocks, lru_block), _diag_blocks(dwx_dense, n_blocks, lru_block),
        dpconv[:conv_width, :c_dim], dpconv[conv_width, :c_dim],
        dpvec[0, :c_dim], dpvec[1, :c_dim], dpvec[2, :c_dim],
        dw_f_t[:n_heads].T]
    rep_pack, rep_spans = _pack_rows(rep_parts, d, F32, WIRE_ROW_ALIGN)
    rep_sum = _sum_slots(_all_gather(rep_pack, "gather_small_grads"), "sum_small_grads")
    g_rep = {nm: _unpack_rows(rep_sum, sp, p.shape) for nm, sp, p in zip(rep_names, rep_spans, rep_parts)}

    def my_cols(full_grad, n):
        return lax.dynamic_slice_in_dim(full_grad, me * n, n, axis=full_grad.ndim - 1)

    def ffn_grads(f):
        grad[f + "_w_gate"] = jnp.stack([g_shard[(f, "gate", l)].T for l in range(n_layers)])
        grad[f + "_w_up"] = jnp.stack([g_shard[(f, "up", l)].T for l in range(n_layers)])
        grad[f + "_w_down"] = jnp.stack([g_shard[(f, "down", l)] for l in range(n_layers)])

    grad = {}
    for nm in ("ffn1_pre_g", "ffn1_post_g", "mix_pre_g", "mix_post_g", "ffn2_pre_g", "ffn2_post_g"):
        grad[nm] = g_rep[nm]
    ffn_grads("ffn2")
    grad["rg_w_in"] = g_shard["rg_w_in"].T[None]
    grad["rg_conv_w"] = my_cols(g_rep["rg_conv_w"], c_shard)[None]
    for nm in ("rg_conv_b", "rg_b_a", "rg_b_x", "rg_lambda"):
        grad[nm] = my_cols(g_rep[nm], c_shard)[None]
    grad["rg_w_a"] = g_rep["rg_w_a"][None]
    grad["rg_w_x"] = g_rep["rg_w_x"][None]
    grad["rg_w_out"] = g_shard["rg_w_out"][None]
    grad["kv_norm_g"] = g_rep["kv_norm_g"].reshape(d)
    grad["w_kv"] = g_shard["w_kv"].T
    grad["w_fgate"] = lax.dynamic_slice_in_dim(g_rep["w_fgate"], me * (d // N_DEV), d // N_DEV, axis=0)
    grad["b_fgate"] = g_rep["b_fgate"]
    grad["attn_w_q"] = g_shard["attn_w_q"][None]
    grad["attn_w_o"] = g_shard["attn_w_o"][None]

    delta, new_m, new_v = {}, {}, {}

    def adamw(nm):
        w = a[nm]
        shape = w.shape
        two_d = (1, shape[0]) if w.ndim == 1 else (-1, shape[-1])
        dl, mo, vo = _adamw(w.reshape(two_d), grad[nm].reshape(two_d), a["m_" + nm].reshape(two_d),
                            a["v_" + nm].reshape(two_d), "adamw_" + nm)
        delta[nm], new_m[nm], new_v[nm] = dl.reshape(shape), mo.reshape(shape), vo.reshape(shape)
        grad[nm] = grad[nm].reshape(shape)

    last_names = ("ffn1_w_gate", "ffn1_w_up", "ffn1_w_down")
    for nm in WEIGHT_NAMES:
        if nm not in last_names:
            adamw(nm)
    land_grads(len(bwd_chunks) - 1, delta["attn_w_o"])
    ffn_grads("ffn1")
    for nm in last_names:
        adamw(nm)

    loss = lax.psum(loss_part[0, 0], AXES)
    return (loss, grad_x[None], *[grad[n] for n in WEIGHT_NAMES], *[delta[n] for n in WEIGHT_NAMES],
            *[new_m[n] for n in WEIGHT_NAMES], *[new_v[n] for n in WEIGHT_NAMES])


def kernel(x, ffn1_pre_g, ffn1_w_gate, ffn1_w_up, ffn1_w_down, ffn1_post_g, mix_pre_g, mix_post_g, ffn2_pre_g, ffn2_w_gate, ffn2_w_up, ffn2_w_down, ffn2_post_g, rg_w_in, rg_conv_w, rg_conv_b, rg_w_a, rg_b_a, rg_w_x, rg_b_x, rg_lambda, rg_w_out, kv_norm_g, w_kv, w_fgate, b_fgate, attn_w_q, attn_w_o, loss_target, m_ffn1_pre_g, m_ffn1_w_gate, m_ffn1_w_up, m_ffn1_w_down, m_ffn1_post_g, m_mix_pre_g, m_mix_post_g, m_ffn2_pre_g, m_ffn2_w_gate, m_ffn2_w_up, m_ffn2_w_down, m_ffn2_post_g, m_rg_w_in, m_rg_conv_w, m_rg_conv_b, m_rg_w_a, m_rg_b_a, m_rg_w_x, m_rg_b_x, m_rg_lambda, m_rg_w_out, m_kv_norm_g, m_w_kv, m_w_fgate, m_b_fgate, m_attn_w_q, m_attn_w_o, v_ffn1_pre_g, v_ffn1_w_gate, v_ffn1_w_up, v_ffn1_w_down, v_ffn1_post_g, v_mix_pre_g, v_mix_post_g, v_ffn2_pre_g, v_ffn2_w_gate, v_ffn2_w_up, v_ffn2_w_down, v_ffn2_post_g, v_rg_w_in, v_rg_conv_w, v_rg_conv_b, v_rg_w_a, v_rg_b_a, v_rg_w_x, v_rg_b_x, v_rg_lambda, v_rg_w_out, v_kv_norm_g, v_w_kv, v_w_fgate, v_b_fgate, v_attn_w_q, v_attn_w_o):
    return _train_step(dict(locals()))
```

```python
import functools

import jax
import jax.numpy as jnp
from jax import lax
from jax.experimental import pallas as pl
from jax.experimental.pallas import tpu as pltpu

F32 = jnp.float32
MXU_DTYPE = jnp.bfloat16
WIRE_DTYPE = jnp.bfloat16
N_DEV = 8
AXES = ("x", "y", "c")
LANES = 128
WIRE_ROW_ALIGN = 16
VMEM_LIMIT_MIN = 32 * 2 ** 20
VMEM_LIMIT_MAX = 56 * 2 ** 20

RMS_EPS = 1e-6
LRU_C = 8.0
ADAM_LR, ADAM_B1, ADAM_B2, ADAM_EPS, ADAM_WD, ADAM_STEP = 0.001, 0.9, 0.999, 1e-08, 0.01, 10

WEIGHT_NAMES = (
    "ffn1_pre_g", "ffn1_w_gate", "ffn1_w_up", "ffn1_w_down", "ffn1_post_g", "mix_pre_g", "mix_post_g",
    "ffn2_pre_g", "ffn2_w_gate", "ffn2_w_up", "ffn2_w_down", "ffn2_post_g", "rg_w_in", "rg_conv_w",
    "rg_conv_b", "rg_w_a", "rg_b_a", "rg_w_x", "rg_b_x", "rg_lambda", "rg_w_out", "kv_norm_g", "w_kv",
    "w_fgate", "b_fgate", "attn_w_q", "attn_w_o")


def _round_up(n, m):
    return (n + m - 1) // m * m


def _tile(dim, target, align=LANES):
    if dim <= target:
        return dim
    best = None
    t = align
    while t <= target:
        if dim % t == 0:
            best = t
        t += align
    return dim if best is None else best


def _cparams(semantics, vmem_estimate):
    limit = min(VMEM_LIMIT_MAX, max(VMEM_LIMIT_MIN, 2 * int(vmem_estimate)))
    return pltpu.CompilerParams(dimension_semantics=semantics, vmem_limit_bytes=limit)


def _nbytes(shape, dtype):
    n = 1
    for s in shape:
        n *= s
    return n * jnp.dtype(dtype).itemsize


def _sigmoid(x):
    return jax.nn.sigmoid(x)


def _softplus(x):
    return jnp.maximum(x, 0.0) + jnp.log1p(jnp.exp(-jnp.abs(x)))


def _expm1(x):
    series = x * (1.0 + x * (0.5 + x * (1.0 / 6.0 + x * (1.0 / 24.0 + x * (1.0 / 120.0)))))
    return jnp.where(jnp.abs(x) < 0.25, series, jnp.exp(x) - 1.0)


_GELU_C = 0.7978845608028654
_GELU_A = 0.044715


def _gelu(x):
    return 0.5 * x * (1.0 + jnp.tanh(_GELU_C * (x + _GELU_A * x * x * x)))


def _gelu_grad(x):
    t = jnp.tanh(_GELU_C * (x + _GELU_A * x * x * x))
    return 0.5 * (1.0 + t) + 0.5 * x * (1.0 - t * t) * _GELU_C * (1.0 + 3.0 * _GELU_A * x * x)


_DOT_DIMS = {"nn": ((1,), (0,)), "nt": ((1,), (1,)), "tn": ((0,), (0,))}


def _dot(a, b, mode):
    return lax.dot_general(a.astype(MXU_DTYPE), b.astype(MXU_DTYPE), (_DOT_DIMS[mode], ((), ())),
                           preferred_element_type=F32)


def _mm(pairs, mode, out_dtype, name, after=None, out_scale=None):
    a0, b0 = pairs[0]
    if mode == "tn":
        k_dim, m_dim = a0.shape
        n_dim = b0.shape[1]
    else:
        m_dim, k_dim = a0.shape
        n_dim = b0.shape[0] if mode == "nt" else b0.shape[1]
    for a, b in pairs:
        assert a.shape == a0.shape and b.shape == b0.shape
    tm = _tile(m_dim, 1408 if mode == "tn" else 512)
    tn = _tile(n_dim, 1408)
    tk = _tile(k_dim, 1408)
    nk = k_dim // tk
    n_pairs = len(pairs)

    if mode == "tn":
        a_spec = pl.BlockSpec((tk, tm), lambda i, j, k: (k, i))
    else:
        a_spec = pl.BlockSpec((tm, tk), lambda i, j, k: (i, k))
    if mode == "nt":
        b_spec = pl.BlockSpec((tn, tk), lambda i, j, k: (j, k))
    else:
        b_spec = pl.BlockSpec((tk, tn), lambda i, j, k: (k, j))

    order = [] if after is None else [after]

    def body(*refs):
        ins, o_ref, acc = refs[:2 * n_pairs], refs[-2], refs[-1]
        k = pl.program_id(2)

        @pl.when(k == 0)
        def _():
            acc[...] = jnp.zeros_like(acc)

        s = acc[...]
        for p in range(n_pairs):
            s = s + _dot(ins[2 * p][...], ins[2 * p + 1][...], mode)
        acc[...] = s

        @pl.when(k == nk - 1)
        def _():
            r = acc[...] if out_scale is None else acc[...] * out_scale
            o_ref[...] = r.astype(out_dtype)

    est = (2 * n_pairs * (_nbytes((tm, tk), a0.dtype) + _nbytes((tk, tn), b0.dtype))
           + 2 * _nbytes((tm, tn), out_dtype) + 2 * _nbytes((tm, tn), F32))
    flat = [t for ab in pairs for t in ab]
    return pl.pallas_call(
        body, name=name, grid=(m_dim // tm, n_dim // tn, nk),
        in_specs=[a_spec, b_spec] * n_pairs + [_ANY] * len(order),
        out_specs=pl.BlockSpec((tm, tn), lambda i, j, k: (i, j)),
        out_shape=jax.ShapeDtypeStruct((m_dim, n_dim), out_dtype),
        scratch_shapes=[pltpu.VMEM((tm, tn), F32)],
        compiler_params=_cparams(("parallel", "parallel", "arbitrary"), est),
    )(*flat, *order)


_ANY = pl.BlockSpec(memory_space=pl.ANY)


def _rms_fwd(x, gain, name, after=None):
    s_dim, d = x.shape
    tm = _tile(s_dim, 512, 8)

    def body(*refs):
        x_ref, g_ref, o_ref = refs[0], refs[1], refs[-1]
        v = x_ref[...]
        r = lax.rsqrt(jnp.mean(v * v, axis=-1, keepdims=True) + RMS_EPS)
        o_ref[...] = (v * r * g_ref[...]).astype(MXU_DTYPE)

    order = [] if after is None else [after]
    return pl.pallas_call(
        body, name=name, grid=(s_dim // tm,),
        in_specs=[pl.BlockSpec((tm, d), lambda i: (i, 0)), pl.BlockSpec((1, d), lambda i: (0, 0))] + [_ANY] * len(order),
        out_specs=pl.BlockSpec((tm, d), lambda i: (i, 0)),
        out_shape=jax.ShapeDtypeStruct((s_dim, d), MXU_DTYPE),
        compiler_params=_cparams(("parallel",), 6 * _nbytes((tm, d), F32)),
    )(x, gain, *order)


def _rms_bwd(x, gain, dys, res, scale, out_dtype, name, after=None):
    s_dim, d = x.shape
    tm = _tile(s_dim, 512, 8)
    n_dy = len(dys)
    has_res = res is not None
    order = [] if after is None else [after]

    def body(*refs):
        x_ref, g_ref = refs[0], refs[1]
        dy_refs = refs[2:2 + n_dy]
        res_ref = refs[2 + n_dy] if has_res else None
        dx_ref, dg_ref = refs[-2], refs[-1]

        @pl.when(pl.program_id(0) == 0)
        def _():
            dg_ref[...] = jnp.zeros_like(dg_ref)

        v = x_ref[...]
        r = lax.rsqrt(jnp.mean(v * v, axis=-1, keepdims=True) + RMS_EPS)
        xh = v * r
        dy = dy_refs[0][...].astype(F32)
        for extra in dy_refs[1:]:
            dy = dy + extra[...].astype(F32)
        gd = dy * g_ref[...]
        dx = scale * r * (gd - xh * jnp.mean(gd * xh, axis=-1, keepdims=True))
        if has_res:
            dx = dx + res_ref[...]
        dx_ref[...] = dx.astype(out_dtype)
        dg_ref[...] += scale * jnp.sum(dy * xh, axis=0, keepdims=True)

    row = pl.BlockSpec((tm, d), lambda i: (i, 0))
    vec = pl.BlockSpec((1, d), lambda i: (0, 0))
    ops = [x, gain] + list(dys) + ([res] if has_res else [])
    return pl.pallas_call(
        body, name=name, grid=(s_dim // tm,),
        in_specs=[row, vec] + [row] * (n_dy + int(has_res)) + [_ANY] * len(order),
        out_specs=[row, vec],
        out_shape=[jax.ShapeDtypeStruct((s_dim, d), out_dtype), jax.ShapeDtypeStruct((1, d), F32)],
        compiler_params=_cparams(("arbitrary",), (2 * len(ops) + 6) * _nbytes((tm, d), F32)),
    )(*ops, *order)


def _mm_rms_res(a, b, h, gain, scale, name):
    s_dim, k_dim = a.shape
    d = b.shape[1]
    tm = _tile(s_dim, 512, 8)
    tk = _tile(k_dim, 2816)
    nk = k_dim // tk

    def body(a_ref, b_ref, h_ref, g_ref, f_ref, o_ref, acc):
        k = pl.program_id(1)

        @pl.when(k == 0)
        def _():
            acc[...] = jnp.zeros_like(acc)

        acc[...] += _dot(a_ref[...], b_ref[...], "nn")

        @pl.when(k == nk - 1)
        def _():
            f = acc[...]
            r = lax.rsqrt(jnp.mean(f * f, axis=-1, keepdims=True) + RMS_EPS)
            f_ref[...] = f
            o_ref[...] = h_ref[...] + scale * (f * r * g_ref[...])

    row = pl.BlockSpec((tm, d), lambda i, k: (i, 0))
    est = (2 * (_nbytes((tm, tk), a.dtype) + _nbytes((tk, d), b.dtype)) + 8 * _nbytes((tm, d), F32))
    return pl.pallas_call(
        body, name=name, grid=(s_dim // tm, nk),
        in_specs=[pl.BlockSpec((tm, tk), lambda i, k: (i, k)), pl.BlockSpec((tk, d), lambda i, k: (k, 0)),
                  row, pl.BlockSpec((1, d), lambda i, k: (0, 0))],
        out_specs=[row, row],
        out_shape=[jax.ShapeDtypeStruct((s_dim, d), F32), jax.ShapeDtypeStruct((s_dim, d), F32)],
        scratch_shapes=[pltpu.VMEM((tm, d), F32)],
        compiler_params=_cparams(("parallel", "arbitrary"), est),
    )(a, b, h, gain)


def _ffn_up(xn, wg_t, wu_t, name):
    s_dim, d = xn.shape
    f_dim = wg_t.shape[0]
    tm = _tile(s_dim, 1024, 8)
    tf = _tile(f_dim, 256)

    def body(x_ref, wg_ref, wu_ref, g_ref, u_ref, a_ref):
        x = x_ref[...]
        g = _dot(x, wg_ref[...], "nt")
        u = _dot(x, wu_ref[...], "nt")
        g_ref[...] = g.astype(MXU_DTYPE)
        u_ref[...] = u.astype(MXU_DTYPE)
        a_ref[...] = (g * _sigmoid(g) * u).astype(MXU_DTYPE)

    w_spec = pl.BlockSpec((tf, d), lambda i, j: (j, 0))
    o_spec = pl.BlockSpec((tm, tf), lambda i, j: (i, j))
    o_shape = jax.ShapeDtypeStruct((s_dim, f_dim), MXU_DTYPE)
    est = 2 * _nbytes((tm, d), xn.dtype) + 4 * _nbytes((tf, d), wg_t.dtype) + 10 * _nbytes((tm, tf), F32)
    return pl.pallas_call(
        body, name=name, grid=(s_dim // tm, f_dim // tf),
        in_specs=[pl.BlockSpec((tm, d), lambda i, j: (i, 0)), w_spec, w_spec],
        out_specs=[o_spec, o_spec, o_spec], out_shape=[o_shape, o_shape, o_shape],
        compiler_params=_cparams(("parallel", "parallel"), est),
    )(xn, wg_t, wu_t)


def _ffn_act_bwd(df, wd, g, u, name):
    s_dim, d = df.shape
    f_dim = wd.shape[0]
    tm = _tile(s_dim, 1024, 8)
    tf = _tile(f_dim, 256)

    def body(df_ref, wd_ref, g_ref, u_ref, dg_ref, du_ref):
        dh = _dot(df_ref[...], wd_ref[...], "nt")
        gv = g_ref[...].astype(F32)
        uv = u_ref[...].astype(F32)
        sg = _sigmoid(gv)
        dg_ref[...] = (dh * uv * (sg * (1.0 + gv * (1.0 - sg)))).astype(MXU_DTYPE)
        du_ref[...] = (dh * gv * sg).astype(MXU_DTYPE)

    t_spec = pl.BlockSpec((tm, tf), lambda i, j: (i, j))
    o_shape = jax.ShapeDtypeStruct((s_dim, f_dim), MXU_DTYPE)
    est = 2 * _nbytes((tm, d), df.dtype) + 2 * _nbytes((tf, d), wd.dtype) + 12 * _nbytes((tm, tf), F32)
    return pl.pallas_call(
        body, name=name, grid=(s_dim // tm, f_dim // tf),
        in_specs=[pl.BlockSpec((tm, d), lambda i, j: (i, 0)), pl.BlockSpec((tf, d), lambda i, j: (j, 0)),
                  t_spec, t_spec],
        out_specs=[t_spec, t_spec], out_shape=[o_shape, o_shape],
        compiler_params=_cparams(("parallel", "parallel"), est),
    )(df, wd, g, u)


def _loss_head(y, target, name):
    s_dim, d = y.shape
    tm = _tile(s_dim, 512, 8)
    nt = s_dim // tm

    def body(y_ref, t_ref, dy_ref, loss_ref, acc):
        i = pl.program_id(0)

        @pl.when(i == 0)
        def _():
            acc[...] = jnp.zeros_like(acc)

        e = y_ref[...] - t_ref[...]
        dy_ref[...] = e * (1.0 / d)
        acc[...] += jnp.sum(e * e, axis=0, keepdims=True)

        @pl.when(i == nt - 1)
        def _():
            loss_ref[...] = jnp.sum(acc[...], axis=1, keepdims=True) * (0.5 / d)

    row = pl.BlockSpec((tm, d), lambda i: (i, 0))
    return pl.pallas_call(
        body, name=name, grid=(nt,), in_specs=[row, row],
        out_specs=[row, pl.BlockSpec((1, 1), lambda i: (0, 0))],
        out_shape=[jax.ShapeDtypeStruct((s_dim, d), F32), jax.ShapeDtypeStruct((1, 1), F32)],
        scratch_shapes=[pltpu.VMEM((1, d), F32)],
        compiler_params=_cparams(("arbitrary",), 8 * _nbytes((tm, d), F32)),
    )(y, target)


def _shift_down(v, sh, row):
    if sh == 0:
        return v
    return jnp.where(row >= sh, pltpu.roll(v, sh, 0), 0.0)


def _shift_up(v, sh, row):
    if sh == 0:
        return v
    n = v.shape[0]
    return jnp.where(row < n - sh, pltpu.roll(v, n - sh, 0), 0.0)


def _conv_fwd(gx, pconv, width, name):
    s_dim, cp2 = gx.shape
    cp = cp2 // 2
    nc = cp // LANES

    def body(x_ref, p_ref, o_ref):
        x = x_ref[...]
        row = lax.broadcasted_iota(jnp.int32, x.shape, 0)
        y = jnp.zeros_like(x) + p_ref[pl.ds(width, 1), :]
        for k in range(width):
            y = y + p_ref[pl.ds(k, 1), :] * _shift_down(x, width - 1 - k, row)
        o_ref[...] = y

    return pl.pallas_call(
        body, name=name, grid=(nc,),
        in_specs=[pl.BlockSpec((s_dim, LANES), lambda j: (0, nc + j)), pl.BlockSpec((8, LANES), lambda j: (0, j))],
        out_specs=pl.BlockSpec((s_dim, LANES), lambda j: (0, j)),
        out_shape=jax.ShapeDtypeStruct((s_dim, cp), F32),
        compiler_params=_cparams(("parallel",), 10 * _nbytes((s_dim, LANES), F32)),
    )(gx, pconv)


def _conv_bwd(d1, d2, gx, pconv, width, name):
    s_dim, cp = d1.shape
    nc = cp // LANES

    def body(d1_ref, d2_ref, x_ref, p_ref, dx_ref, dp_ref):
        d = d1_ref[...] + d2_ref[...]
        x = x_ref[...]
        row = lax.broadcasted_iota(jnp.int32, x.shape, 0)
        dx = jnp.zeros_like(d)
        dp_ref[...] = jnp.zeros_like(dp_ref)
        for k in range(width):
            sh = width - 1 - k
            dx = dx + p_ref[pl.ds(k, 1), :] * _shift_up(d, sh, row)
            dp_ref[pl.ds(k, 1), :] = jnp.sum(d * _shift_down(x, sh, row), axis=0, keepdims=True)
        dp_ref[pl.ds(width, 1), :] = jnp.sum(d, axis=0, keepdims=True)
        dx_ref[...] = dx.astype(MXU_DTYPE)

    strip = pl.BlockSpec((s_dim, LANES), lambda j: (0, j))
    par = pl.BlockSpec((8, LANES), lambda j: (0, j))
    return pl.pallas_call(
        body, name=name, grid=(nc,),
        in_specs=[strip, strip, pl.BlockSpec((s_dim, LANES), lambda j: (0, nc + j)), par],
        out_specs=[strip, par],
        out_shape=[jax.ShapeDtypeStruct((s_dim, cp), MXU_DTYPE), jax.ShapeDtypeStruct((8, cp), F32)],
        compiler_params=_cparams(("parallel",), 14 * _nbytes((s_dim, LANES), F32)),
    )(d1, d2, gx, pconv)


def _lru_coeffs(ra, ia, p_ref):
    r = _sigmoid(ra + p_ref[pl.ds(0, 1), :])
    i = _sigmoid(ia + p_ref[pl.ds(1, 1), :])
    sp = _softplus(-p_ref[pl.ds(2, 1), :])
    log_a = -LRU_C * r * sp
    a = jnp.exp(log_a)
    mult = jnp.sqrt(-_expm1(2.0 * log_a))
    return r, i, sp, a, mult


def _scan_fwd(gx, rec, gates, pvec, name):
    s_dim, cp = rec.shape
    ts = _tile(s_dim, 256, 8)
    nt = s_dim // ts

    def body(gate_ref, rec_ref, ra_ref, ia_ref, p_ref, h_ref, y_ref, a_s, u_s, carry):
        @pl.when(pl.program_id(0) == 0)
        def _():
            carry[...] = jnp.zeros_like(carry)

        rec_v = rec_ref[...]
        _, i, _, a, mult = _lru_coeffs(ra_ref[...], ia_ref[...], p_ref)
        a_s[...] = a
        u_s[...] = mult * (i * rec_v)

        def step(t, h):
            h = a_s[pl.ds(t, 1), :] * h + u_s[pl.ds(t, 1), :]
            h_ref[pl.ds(t, 1), :] = h
            return h

        carry[pl.ds(0, 1), :] = lax.fori_loop(0, ts, step, carry[pl.ds(0, 1), :], unroll=8)
        y_ref[...] = (_gelu(gate_ref[...]) * h_ref[...]).astype(MXU_DTYPE)

    blk = pl.BlockSpec((ts, cp), lambda t: (t, 0))
    return pl.pallas_call(
        body, name=name, grid=(nt,),
        in_specs=[blk, blk, blk, pl.BlockSpec((ts, cp), lambda t: (t, 1)), pl.BlockSpec((8, cp), lambda t: (0, 0))],
        out_specs=[blk, blk],
        out_shape=[jax.ShapeDtypeStruct((s_dim, cp), F32), jax.ShapeDtypeStruct((s_dim, cp), MXU_DTYPE)],
        scratch_shapes=[pltpu.VMEM((ts, cp), F32), pltpu.VMEM((ts, cp), F32), pltpu.VMEM((8, cp), F32)],
        compiler_params=_cparams(("arbitrary",), 14 * _nbytes((ts, cp), F32)),
    )(gx, rec, gates, gates, pvec)


def _scan_bwd(dy, gx, hrec, rec, gates, pvec, name):
    s_dim, cp = rec.shape
    ts = _tile(s_dim, 128, 8)
    nt = s_dim // ts

    def body(dy_ref, gate_ref, h_ref, hp_ref, rec_ref, ra_ref, ia_ref, p_ref,
             dgate_ref, dra_ref, dia_ref, drec_ref, dp_ref, a_s, d_s, carry):
        t_id = pl.program_id(0)

        @pl.when(t_id == 0)
        def _():
            carry[...] = jnp.zeros_like(carry)
            dp_ref[...] = jnp.zeros_like(dp_ref)

        rec_v = rec_ref[...]
        r, i, sp, a, mult = _lru_coeffs(ra_ref[...], ia_ref[...], p_ref)
        gate = gate_ref[...]
        dyv = dy_ref[...]
        h = h_ref[...]
        dgate_ref[...] = (dyv * h * _gelu_grad(gate)).astype(MXU_DTYPE)
        a_s[...] = a
        d_s[...] = dyv * _gelu(gate)

        def step(k, c):
            t = ts - 1 - k
            d = d_s[pl.ds(t, 1), :] + c
            d_s[pl.ds(t, 1), :] = d
            return a_s[pl.ds(t, 1), :] * d

        carry[pl.ds(0, 1), :] = lax.fori_loop(0, ts, step, carry[pl.ds(0, 1), :], unroll=8)
        dh = d_s[...]
        row = lax.broadcasted_iota(jnp.int32, h.shape, 0)
        first = jnp.where(t_id == nt - 1, 0.0, 1.0) * hp_ref[pl.ds(7, 1), :]
        h_prev = jnp.where(row == 0, first, pltpu.roll(h, 1, 0))
        dix = dh * mult
        dla = dh * h_prev * a - dh * (i * rec_v) * (a * a) / mult
        dra = dla * (-LRU_C * sp) * r * (1.0 - r)
        dia = dix * rec_v * i * (1.0 - i)
        dra_ref[...] = dra.astype(MXU_DTYPE)
        dia_ref[...] = dia.astype(MXU_DTYPE)
        drec_ref[...] = dix * i
        dsp = jnp.sum(dla * (-LRU_C * r), axis=0, keepdims=True)
        dp_ref[pl.ds(0, 1), :] += jnp.sum(dra, axis=0, keepdims=True)
        dp_ref[pl.ds(1, 1), :] += jnp.sum(dia, axis=0, keepdims=True)
        dp_ref[pl.ds(2, 1), :] += dsp * (-_sigmoid(-p_ref[pl.ds(2, 1), :]))

    blk = pl.BlockSpec((ts, cp), lambda t: (nt - 1 - t, 0))
    prev = pl.BlockSpec((8, cp), lambda t: (jnp.maximum((nt - 1 - t) * (ts // 8) - 1, 0), 0))
    par = pl.BlockSpec((8, cp), lambda t: (0, 0))
    lo = jax.ShapeDtypeStruct((s_dim, cp), MXU_DTYPE)
    return pl.pallas_call(
        body, name=name, grid=(nt,),
        in_specs=[blk, blk, blk, prev, blk, blk, pl.BlockSpec((ts, cp), lambda t: (nt - 1 - t, 1)), par],
        out_specs=[blk, blk, blk, blk, par],
        out_shape=[lo, lo, lo, jax.ShapeDtypeStruct((s_dim, cp), F32), jax.ShapeDtypeStruct((8, cp), F32)],
        scratch_shapes=[pltpu.VMEM((ts, cp), F32), pltpu.VMEM((ts, cp), F32), pltpu.VMEM((8, cp), F32)],
        compiler_params=_cparams(("arbitrary",), 40 * _nbytes((ts, cp), F32)),
    )(dy, gx, hrec, hrec, rec, gates, gates, pvec)


def _fgate_fwd(fpre, bias, name):
    s_dim, w = fpre.shape
    ts = _tile(s_dim, 512, 8)

    def body(f_ref, b_ref, c_ref, lf_s, carry):
        @pl.when(pl.program_id(0) == 0)
        def _():
            carry[...] = jnp.zeros_like(carry)

        lf_s[...] = -_softplus(-(f_ref[...] + b_ref[pl.ds(0, 1), :]))

        def step(t, c):
            c = c + lf_s[pl.ds(t, 1), :]
            c_ref[pl.ds(t, 1), :] = c
            return c

        carry[pl.ds(0, 1), :] = lax.fori_loop(0, ts, step, carry[pl.ds(0, 1), :], unroll=8)

    blk = pl.BlockSpec((ts, w), lambda t: (t, 0))
    return pl.pallas_call(
        body, name=name, grid=(s_dim // ts,),
        in_specs=[blk, pl.BlockSpec((8, w), lambda t: (0, 0))], out_specs=blk,
        out_shape=jax.ShapeDtypeStruct((s_dim, w), F32),
        scratch_shapes=[pltpu.VMEM((ts, w), F32), pltpu.VMEM((8, w), F32)],
        compiler_params=_cparams(("arbitrary",), 12 * _nbytes((ts, w), F32)),
    )(fpre, bias)


def _fgate_bwd(dc, fpre, bias, name):
    s_dim, w = fpre.shape
    ts = _tile(s_dim, 512, 8)
    nt = s_dim // ts

    def body(dc_ref, f_ref, b_ref, df_ref, db_ref, d_s, carry):
        @pl.when(pl.program_id(0) == 0)
        def _():
            carry[...] = jnp.zeros_like(carry)
            db_ref[...] = jnp.zeros_like(db_ref)

        d_s[...] = dc_ref[...]

        def step(k, c):
            t = ts - 1 - k
            c = c + d_s[pl.ds(t, 1), :]
            d_s[pl.ds(t, 1), :] = c
            return c

        carry[pl.ds(0, 1), :] = lax.fori_loop(0, ts, step, carry[pl.ds(0, 1), :], unroll=8)
        df = d_s[...] * _sigmoid(-(f_ref[...] + b_ref[pl.ds(0, 1), :]))
        df_ref[...] = df
        db_ref[pl.ds(0, 1), :] += jnp.sum(df, axis=0, keepdims=True)

    blk = pl.BlockSpec((ts, w), lambda t: (nt - 1 - t, 0))
    par = pl.BlockSpec((8, w), lambda t: (0, 0))
    return pl.pallas_call(
        body, name=name, grid=(nt,), in_specs=[blk, blk, par], out_specs=[blk, par],
        out_shape=[jax.ShapeDtypeStruct((s_dim, w), F32), jax.ShapeDtypeStruct((8, w), F32)],
        scratch_shapes=[pltpu.VMEM((ts, w), F32), pltpu.VMEM((8, w), F32)],
        compiler_params=_cparams(("arbitrary",), 12 * _nbytes((ts, w), F32)),
    )(dc, fpre, bias)


def _head_lanes(hh, dh):
    lane = lax.broadcasted_iota(jnp.int32, (1, LANES), 1)
    return (lane >= hh * dh) & (lane < (hh + 1) * dh)


def _pair_attn_fwd(q, kv, v_t, c_col, c_row, name):
    s_dim, da = q.shape
    n_h = c_col.shape[0]
    dh = da // n_h
    assert LANES % dh == 0 and da % LANES == 0
    hb = LANES // dh
    n_blocks = da // LANES
    t = _tile(s_dim, 512, LANES)
    nb = s_dim // t

    pairs = [(i, j) for i in range(nb) for j in range(i + 1)]
    i_tab = jnp.asarray([p[0] for p in pairs], jnp.int32)
    j_tab = jnp.asarray([p[1] for p in pairs], jnp.int32)

    def body(i_ref, j_ref, q_ref, k_ref, vt_ref, cq_ref, ck_ref, o_ref, lse_ref, m_s, l_s, acc):
        i, j = i_ref[pl.program_id(1)], j_ref[pl.program_id(1)]

        @pl.when(j == 0)
        def _():
            m_s[...] = jnp.full_like(m_s, -jnp.inf)
            l_s[...] = jnp.zeros_like(l_s)
            acc[...] = jnp.zeros_like(acc)

        def tile(masked):
            qv = q_ref[...]
            for hh in range(hb):
                st = _dot(k_ref[...], jnp.where(_head_lanes(hh, dh), qv, jnp.zeros_like(qv)), "nt")
                st = st + (cq_ref[hh] - ck_ref[hh])
                if masked:
                    keep = lax.broadcasted_iota(jnp.int32, (t, t), 0) <= lax.broadcasted_iota(jnp.int32, (t, t), 1)
                    st = jnp.where(keep, st, -jnp.inf)
                m_prev = m_s[hh]
                m_new = jnp.maximum(m_prev, jnp.max(st, axis=0, keepdims=True))
                alpha = jnp.exp(m_prev - m_new)
                p = jnp.exp(st - m_new)
                l_s[hh] = alpha * l_s[hh] + jnp.sum(p, axis=0, keepdims=True)
                acc[hh] = alpha * acc[hh] + _dot(vt_ref[...], p, "nn")
                m_s[hh] = m_new

        pl.when(j < i)(functools.partial(tile, False))
        pl.when(j == i)(functools.partial(tile, True))

        @pl.when(j == i)
        def _():
            feat = lax.broadcasted_iota(jnp.int32, (LANES, 1), 0)
            out_t = jnp.zeros((LANES, t), F32)
            for hh in range(hb):
                out_t = jnp.where((feat >= hh * dh) & (feat < (hh + 1) * dh), acc[hh] / l_s[hh], out_t)
                lse_ref[hh] = m_s[hh] + jnp.log(l_s[hh])
            o_ref[...] = out_t.T

    q_spec = pl.BlockSpec((t, LANES), lambda b, p, it, jt: (it[p], b))
    k_spec = pl.BlockSpec((t, LANES), lambda b, p, it, jt: (jt[p], b))
    vt_spec = pl.BlockSpec((LANES, t), lambda b, p, it, jt: (b, jt[p]))
    cq_spec = pl.BlockSpec((hb, 1, t), lambda b, p, it, jt: (b, 0, it[p]))
    ck_spec = pl.BlockSpec((hb, t, 1), lambda b, p, it, jt: (b, jt[p], 0))
    return pl.pallas_call(
        body, name=name,
        grid_spec=pltpu.PrefetchScalarGridSpec(
            num_scalar_prefetch=2, grid=(n_blocks, len(pairs)),
            in_specs=[q_spec, k_spec, vt_spec, cq_spec, ck_spec], out_specs=[q_spec, cq_spec],
            scratch_shapes=[pltpu.VMEM((hb, 1, t), F32), pltpu.VMEM((hb, 1, t), F32), pltpu.VMEM((hb, LANES, t), F32)]),
        out_shape=[jax.ShapeDtypeStruct((s_dim, da), F32), jax.ShapeDtypeStruct((n_h, 1, s_dim), F32)],
        compiler_params=_cparams(("parallel", "arbitrary"), 10 * hb * _nbytes((t, t), F32)),
    )(i_tab, j_tab, q, kv, v_t, c_row, c_col)


def _attn_delta(do, o, n_h, name):
    s_dim, da = o.shape
    dh = da // n_h
    hb = LANES // dh
    t = _tile(s_dim, 512, LANES)

    def body(do_ref, o_ref, d_ref):
        prod_t = (do_ref[...].astype(MXU_DTYPE).astype(F32) * o_ref[...]).T
        for hh in range(hb):
            d_ref[hh] = jnp.sum(prod_t[hh * dh:(hh + 1) * dh], axis=0, keepdims=True)

    blk = pl.BlockSpec((t, LANES), lambda b, i: (i, b))
    return pl.pallas_call(
        body, name=name, grid=(da // LANES, s_dim // t), in_specs=[blk, blk],
        out_specs=pl.BlockSpec((hb, 1, t), lambda b, i: (b, 0, i)),
        out_shape=jax.ShapeDtypeStruct((n_h, 1, s_dim), F32),
        compiler_params=_cparams(("parallel", "parallel"), 8 * _nbytes((t, LANES), F32)),
    )(do, o)


def _pair_attn_bwd(q, kv, c_col, c_row, lse, delta, do, scale, name):
    s_dim, da = q.shape
    n_h = c_col.shape[0]
    dh = da // n_h
    hb = LANES // dh
    n_blocks = da // LANES
    t = _tile(s_dim, 512, LANES)
    nb = s_dim // t

    pairs = [(i, j) for j in range(nb) for i in range(j, nb)]
    i_tab = jnp.asarray([p[0] for p in pairs], jnp.int32)
    j_tab = jnp.asarray([p[1] for p in pairs], jnp.int32)

    def body(i_ref, j_ref, q_ref, k_ref, v_ref, cq_ref, ck_ref, lse_ref, dl_ref, do_ref,
             dq_ref, dcq_ref, dk_ref, dv_ref, dck_ref, dk_acc, dv_acc, dck_acc):
        i, j = i_ref[pl.program_id(1)], j_ref[pl.program_id(1)]

        @pl.when(pl.program_id(1) == 0)
        def _():
            dq_ref[...] = jnp.zeros_like(dq_ref)
            dcq_ref[...] = jnp.zeros_like(dcq_ref)

        @pl.when(i == j)
        def _():
            dk_acc[...] = jnp.zeros_like(dk_acc)
            dv_acc[...] = jnp.zeros_like(dv_acc)
            dck_acc[...] = jnp.zeros_like(dck_acc)

        def tile(masked):
            start = pl.multiple_of(i * t, t)
            qv, kv_ = q_ref[...], k_ref[...]
            dov = do_ref[...].astype(MXU_DTYPE)
            for hh in range(hb):
                lanes = _head_lanes(hh, dh)
                qm = jnp.where(lanes, qv, jnp.zeros_like(qv))
                km = jnp.where(lanes, kv_, jnp.zeros_like(kv_))
                dom = jnp.where(lanes, dov, jnp.zeros_like(dov))
                st = _dot(kv_, qm, "nt") + (cq_ref[hh] - ck_ref[hh])
                if masked:
                    keep = lax.broadcasted_iota(jnp.int32, (t, t), 0) <= lax.broadcasted_iota(jnp.int32, (t, t), 1)
                    st = jnp.where(keep, st, -jnp.inf)
                pt = jnp.exp(st - lse_ref[hh])
                dst = pt * (_dot(v_ref[...], dom, "nt") - dl_ref[hh])
                dv_acc[...] += _dot(pt, dom, "nn")
                dk_acc[...] += _dot(dst, qm, "nn")
                dq_ref[pl.ds(start, t), :] += _dot(dst, km, "tn") * scale
                dcq_ref[hh, :, pl.ds(start, t)] += jnp.sum(dst, axis=0, keepdims=True)
                dck_acc[hh] -= jnp.sum(dst, axis=1, keepdims=True)

        pl.when(i > j)(functools.partial(tile, False))
        pl.when(i == j)(functools.partial(tile, True))

        @pl.when(i == nb - 1)
        def _():
            dk_ref[...] = dk_acc[...]
            dv_ref[...] = dv_acc[...]
            dck_ref[...] = dck_acc[...]

    q_spec = pl.BlockSpec((t, LANES), lambda b, p, it, jt: (it[p], b))
    qrow_spec = pl.BlockSpec((hb, 1, t), lambda b, p, it, jt: (b, 0, it[p]))
    k_spec = pl.BlockSpec((t, LANES), lambda b, p, it, jt: (jt[p], b))
    v_spec = pl.BlockSpec((t, LANES), lambda b, p, it, jt: (jt[p], n_blocks + b))
    kcol_spec = pl.BlockSpec((hb, t, 1), lambda b, p, it, jt: (b, jt[p], 0))
    wide = jax.ShapeDtypeStruct((s_dim, da), F32)
    return pl.pallas_call(
        body, name=name,
        grid_spec=pltpu.PrefetchScalarGridSpec(
            num_scalar_prefetch=2, grid=(n_blocks, len(pairs)),
            in_specs=[q_spec, k_spec, v_spec, qrow_spec, kcol_spec, qrow_spec, qrow_spec, q_spec],
            out_specs=[pl.BlockSpec((s_dim, LANES), lambda b, p, it, jt: (0, b)),
                       pl.BlockSpec((hb, 1, s_dim), lambda b, p, it, jt: (b, 0, 0)), k_spec, k_spec, kcol_spec],
            scratch_shapes=[pltpu.VMEM((t, LANES), F32), pltpu.VMEM((t, LANES), F32), pltpu.VMEM((hb, t, 1), F32)]),
        out_shape=[wide, jax.ShapeDtypeStruct((n_h, 1, s_dim), F32), wide, wide,
                   jax.ShapeDtypeStruct((n_h, s_dim, 1), F32)],
        compiler_params=_cparams(("parallel", "arbitrary"),
                                 10 * hb * _nbytes((t, t), F32) + 4 * _nbytes((s_dim, LANES), F32)),
    )(i_tab, j_tab, q, kv, kv, c_row, c_col, lse, delta, do)


_HBM = pl.BlockSpec(memory_space=pltpu.HBM)
_MESH_ID = pl.DeviceIdType.MESH


def _all_gather(block, name):
    r, w = block.shape

    def body(x_ref, out_ref, send_sems, recv_sems, local_sem):
        x, y, c = lax.axis_index("x"), lax.axis_index("y"), lax.axis_index("c")
        me, sibling = (x, y, c), (x, y, 1 - c)
        chips = [(1 - x, y), (x, 1 - y), (1 - x, 1 - y)]

        def slot(px, py, pc):
            return out_ref.at[4 * px + 2 * py + pc]

        def copy(k, blk, to, src=None):
            return pltpu.make_async_remote_copy(
                src_ref=slot(*blk) if src is None else src, dst_ref=slot(*blk),
                send_sem=send_sems.at[k], recv_sem=recv_sems.at[k], device_id=to, device_id_type=_MESH_ID)

        mine = pltpu.make_async_copy(x_ref, slot(*me), local_sem)
        mine.start()
        first = [copy(0, me, sibling, src=x_ref)]
        first += [copy(1 + n, me, (*chip, c), src=x_ref) for n, chip in enumerate(chips)]
        for cp in first:
            cp.start()
        passed = [copy(4 + n, (*chip, c), sibling) for n, chip in enumerate(chips)]
        for n, chip in enumerate(chips):
            copy(1 + n, (*chip, c), me).wait_recv()
            passed[n].start()
        copy(0, sibling, me).wait_recv()
        for n, chip in enumerate(chips):
            copy(4 + n, (*chip, 1 - c), me).wait_recv()
        for cp in first + passed:
            cp.wait_send()
        mine.wait()

    return pl.pallas_call(
        body, name=name, out_shape=jax.ShapeDtypeStruct((N_DEV, r, w), block.dtype),
        in_specs=[_HBM], out_specs=_HBM,
        scratch_shapes=[pltpu.SemaphoreType.DMA((7,)), pltpu.SemaphoreType.DMA((7,)), pltpu.SemaphoreType.DMA],
    )(block)


_SEM = pl.BlockSpec(memory_space=pltpu.SEMAPHORE)
_EFFECT = pltpu.SideEffectType.DATAFLOW_SIDE_EFFECTING


def _exchange_start(srcs, personalized, after, name):
    n = len(srcs)
    n_after = len(after)
    lands = [lax.empty((N_DEV,) + s.shape[-2:], s.dtype) for s in srcs]

    def body(*refs):
        src_refs, land_refs = refs[:n], refs[n:2 * n]
        outs = refs[2 * n + n_after:]
        send_sems, recv_sems, token = outs[:n], outs[n:2 * n], outs[-1]
        x, y, c = lax.axis_index("x"), lax.axis_index("y"), lax.axis_index("c")
        mine = 4 * x + 2 * y + c
        for ci in range(n):
            for k in range(1, N_DEV):
                px = 1 - x if k & 4 else x
                py = 1 - y if k & 2 else y
                pc = 1 - c if k & 1 else c
                src = src_refs[ci].at[4 * px + 2 * py + pc] if personalized else src_refs[ci]
                pltpu.make_async_remote_copy(
                    src_ref=src, dst_ref=land_refs[ci].at[mine], send_sem=send_sems[ci], recv_sem=recv_sems[ci],
                    device_id=(px, py, pc), device_id_type=_MESH_ID).start()
        token[...] = jnp.zeros_like(token)

    sem = pltpu.SemaphoreType.DMA(())
    out_shape = ([sem] * (2 * n) + [pltpu.HBM(s.shape, s.dtype) for s in srcs]
                 + [pltpu.HBM(l.shape, l.dtype) for l in lands] + [jax.ShapeDtypeStruct((8, LANES), F32)])
    res = pl.pallas_call(
        body, name=name, out_shape=tuple(out_shape),
        in_specs=[_HBM] * (2 * n) + [_ANY] * n_after,
        out_specs=tuple([_SEM] * (2 * n) + [_HBM] * (2 * n) + [pl.BlockSpec(memory_space=pltpu.VMEM)]),
        input_output_aliases={i: 2 * n + i for i in range(2 * n)},
        compiler_params=pltpu.CompilerParams(has_side_effects=_EFFECT),
    )(*[pltpu.with_memory_space_constraint(s, pltpu.HBM) for s in srcs],
      *[pltpu.with_memory_space_constraint(l, pltpu.HBM) for l in lands], *after)
    handles = [(res[ci], res[n + ci], res[2 * n + ci], res[3 * n + ci]) for ci in range(n)]
    return handles, res[-1]


def _exchange_wait(handle, after, name):
    send_sem, recv_sem, src_thru, land_thru = handle

    def body(src_ref, land_ref, send_ref, recv_ref, after_ref, src_out, land_out):
        seven = land_ref.at[pl.ds(0, N_DEV - 1)]
        copies = pltpu.make_async_remote_copy(
            src_ref=seven, dst_ref=seven, send_sem=send_ref, recv_sem=recv_ref,
            device_id=(lax.axis_index("x"), lax.axis_index("y"), lax.axis_index("c")), device_id_type=_MESH_ID)
        copies.wait_send()
        copies.wait_recv()

    return pl.pallas_call(
        body, name=name,
        out_shape=(pltpu.HBM(src_thru.shape, src_thru.dtype), pltpu.HBM(land_thru.shape, land_thru.dtype)),
        in_specs=(_HBM, _HBM, _SEM, _SEM, _ANY), out_specs=(_HBM, _HBM), input_output_aliases={0: 0, 1: 1},
        compiler_params=pltpu.CompilerParams(has_side_effects=_EFFECT),
    )(src_thru, land_thru, send_sem, recv_sem, after)[1]


def _own_slot(land, own, me):
    return lax.dynamic_update_index_in_dim(land, own, me, axis=0)


def _sum_slots(slots, name):
    n, r, w = slots.shape
    tr = _tile(r, 128, WIRE_ROW_ALIGN)

    def body(s_ref, o_ref):
        acc = s_ref[0].astype(F32)
        for d in range(1, n):
            acc = acc + s_ref[d].astype(F32)
        o_ref[...] = acc

    return pl.pallas_call(
        body, name=name, grid=(r // tr,),
        in_specs=[pl.BlockSpec((n, tr, w), lambda i: (0, i, 0))],
        out_specs=pl.BlockSpec((tr, w), lambda i: (i, 0)),
        out_shape=jax.ShapeDtypeStruct((r, w), F32),
        compiler_params=_cparams(("parallel",), 2 * _nbytes((n, tr, w), slots.dtype) + 4 * _nbytes((tr, w), F32)),
    )(slots)


def _adamw(w, g, m, v, name):
    r, c = w.shape
    tr = _tile(r, 512, 8)

    def body(w_ref, g_ref, m_ref, v_ref, d_ref, mo_ref, vo_ref):
        gv = g_ref[...]
        m_new = ADAM_B1 * m_ref[...] + (1.0 - ADAM_B1) * gv
        v_new = ADAM_B2 * v_ref[...] + (1.0 - ADAM_B2) * (gv * gv)
        m_hat = m_new / (1.0 - ADAM_B1 ** ADAM_STEP)
        v_hat = v_new / (1.0 - ADAM_B2 ** ADAM_STEP)
        d_ref[...] = -ADAM_LR * (m_hat / (jnp.sqrt(v_hat) + ADAM_EPS) + ADAM_WD * w_ref[...])
        mo_ref[...] = m_new
        vo_ref[...] = v_new

    blk = pl.BlockSpec((tr, c), lambda i: (i, 0))
    shp = jax.ShapeDtypeStruct((r, c), F32)
    return pl.pallas_call(
        body, name=name, grid=(r // tr,), in_specs=[blk] * 4, out_specs=[blk] * 3, out_shape=[shp] * 3,
        compiler_params=_cparams(("parallel",), 16 * _nbytes((tr, _round_up(c, LANES)), F32)),
    )(w, g, m, v)


def _pack_rows(parts, width, dtype, row_align):
    rows, spans, off = [], [], 0
    for p in parts:
        flat = p.reshape(-1).astype(dtype)
        n_rows = _round_up(-(-flat.shape[0] // width), row_align)
        flat = jnp.pad(flat, (0, n_rows * width - flat.shape[0]))
        rows.append(flat.reshape(n_rows, width))
        spans.append((off, n_rows))
        off += n_rows
    return jnp.concatenate(rows, axis=0), spans


def _unpack_rows(mat, span, shape):
    off, n_rows = span
    n = 1
    for s in shape:
        n *= s
    return mat[..., off:off + n_rows, :].reshape(mat.shape[:-2] + (-1,))[..., :n].reshape(mat.shape[:-2] + tuple(shape))


def _block_diag(w, size):
    n, b, _ = w.shape
    eye = jnp.eye(n, dtype=w.dtype)
    dense = (w[:, :, None, :] * eye[:, None, :, None]).reshape(n * b, n * b)
    return jnp.pad(dense, ((0, size - n * b), (0, size - n * b)))


def _diag_blocks(dense, n, b):
    return jnp.stack([dense[k * b:(k + 1) * b, k * b:(k + 1) * b] for k in range(n)])


def _pad_rows(a, rows):
    return jnp.pad(a, ((0, rows - a.shape[0]), (0, 0)))


def _pad_cols(a, cols):
    return jnp.pad(a, ((0, 0), (0, cols - a.shape[1])))


def _train_step(a):
    x = a["x"][0]
    target = a["loss_target"][0]
    s_dim, d = x.shape
    n_layers = a["ffn1_pre_g"].shape[0]
    f_shard = a["ffn1_w_gate"].shape[2]
    c_shard = a["rg_conv_b"].shape[1]
    c_dim = c_shard * N_DEV
    cp = _round_up(c_dim, LANES)
    conv_width = a["rg_conv_w"].shape[1]
    n_blocks, lru_block = a["rg_w_a"].shape[1], a["rg_w_a"].shape[2]
    d_attn = a["attn_w_q"].shape[2]
    n_heads = a["b_fgate"].shape[0]
    d_head = d_attn // n_heads
    attn_scale = d_head ** -0.5
    assert conv_width < 8 and n_heads <= LANES and n_layers == 2
    assert d_attn == d
    me = 4 * lax.axis_index("x") + 2 * lax.axis_index("y") + lax.axis_index("c")

    shard = {"rg_w_in": a["rg_w_in"][0].T, "rg_w_out": a["rg_w_out"][0], "w_kv": a["w_kv"].T,
             "attn_w_q": a["attn_w_q"][0], "attn_w_o": a["attn_w_o"][0]}
    for l in range(n_layers):
        for f in ("ffn1", "ffn2"):
            shard[(f, "gate", l)] = a[f + "_w_gate"][l].T
            shard[(f, "up", l)] = a[f + "_w_up"][l].T
            shard[(f, "down", l)] = a[f + "_w_down"][l]

    def ffn_names(f, l):
        return [(f, "gate", l), (f, "up", l), (f, "down", l)]

    def chunk_layout(names):
        spans, off = [], 0
        for nm in names:
            spans.append((nm, off, shard[nm].shape[0]))
            off += _round_up(shard[nm].shape[0], WIRE_ROW_ALIGN)
        return spans, off

    def pack_chunk(names, parts):
        return jnp.concatenate(
            [_pad_rows(parts[nm].astype(WIRE_DTYPE), _round_up(parts[nm].shape[0], WIRE_ROW_ALIGN)) for nm in names], axis=0)

    full = {}

    def unpack_chunk(names, gathered):
        for nm, o, n_rows in chunk_layout(names)[0]:
            full[nm] = gathered[:, o:o + n_rows, :].reshape(N_DEV * n_rows, d)

    fwd_chunks = [ffn_names("ffn1", 0)[:2], ffn_names("ffn1", 0)[2:], ["rg_w_in", "rg_w_out"],
                  ffn_names("ffn2", 0) + ["w_kv"], ffn_names("ffn1", 1) + ["attn_w_q", "attn_w_o"], ffn_names("ffn2", 1)]
    fwd_packs = [pack_chunk(names, shard) for names in fwd_chunks]
    unpack_chunk(fwd_chunks[0], _all_gather(fwd_packs[0], "gather_weights_first"))

    small_parts = [a["rg_conv_w"][0], a["rg_conv_b"][0], a["rg_b_a"][0], a["rg_b_x"][0], a["rg_lambda"][0], a["w_fgate"]]
    small_pack, small_spans = _pack_rows(small_parts, d, F32, 8)
    small_all = _all_gather(small_pack, "gather_small")
    fwd_handles, fwd_token = _exchange_start(fwd_packs[1:], False, [full[("ffn1", "up", 0)], small_all],
                                             "gather_weights_start")

    def land_weights(n, after):
        land = _exchange_wait(fwd_handles[n - 1], after, f"gather_weights_wait_{n}")
        unpack_chunk(fwd_chunks[n], _own_slot(land, fwd_packs[n], me))

    sm = [_unpack_rows(small_all, sp, p.shape) for sp, p in zip(small_spans, small_parts)]
    conv_w = jnp.moveaxis(sm[0], 0, 1).reshape(conv_width, c_dim)
    conv_b, b_a, b_x, lam = (v.reshape(1, c_dim) for v in sm[1:5])
    w_f = sm[5].reshape(d, n_heads)

    pconv = _pad_rows(_pad_cols(jnp.concatenate([conv_w, conv_b], axis=0), cp), 8)
    pvec = _pad_rows(_pad_cols(jnp.concatenate([b_a, b_x, lam], axis=0), cp), 8)
    wa_dense = _block_diag(a["rg_w_a"][0], cp).astype(MXU_DTYPE)
    wx_dense = _block_diag(a["rg_w_x"][0], cp).astype(MXU_DTYPE)
    wax = jnp.concatenate([wa_dense, wx_dense], axis=1)
    w_f_t = _pad_rows(w_f.T.astype(MXU_DTYPE), LANES)
    b_f = _pad_rows(_pad_cols(a["b_fgate"].reshape(1, n_heads), LANES), 8)

    def gain(name, l):
        return a[name][l].reshape(1, d)

    def ffn_fwd(h, f, l, after=None, down_chunk=None):
        xn = _rms_fwd(h, gain(f + "_pre_g", l), f"{f}_{l}_pre_norm", after)
        g, u, act = _ffn_up(xn, full[(f, "gate", l)], full[(f, "up", l)], f"{f}_{l}_up")
        if down_chunk is not None:
            land_weights(down_chunk, act)
        fo, h_new = _mm_rms_res(act, full[(f, "down", l)], h, gain(f + "_post_g", l), 0.5, f"{f}_{l}_down")
        return h_new, (h, xn, g, u, act, fo)

    h0 = x
    h0a, sv_f1_0 = ffn_fwd(h0, "ffn1", 0, fwd_token, down_chunk=1)
    land_weights(2, h0a)
    w_in_gate = _pad_rows(full["rg_w_in"][:c_dim], cp)
    w_in_rec = _pad_rows(full["rg_w_in"][c_dim:], cp)
    w_in_t = jnp.concatenate([w_in_gate, w_in_rec], axis=0)
    w_out = _pad_rows(full["rg_w_out"], cp)
    hn_rg = _rms_fwd(h0a, gain("mix_pre_g", 0), "rg_pre_norm")
    gx = _mm([(hn_rg, w_in_t)], "nt", F32, "rg_in_proj")
    rec = _conv_fwd(gx, pconv, conv_width, "rg_conv")
    gates = _mm([(rec, wax)], "nn", F32, "rg_gate_proj")
    h_rec, y_rg = _scan_fwd(gx, rec, gates, pvec, "rg_scan")
    m_rg, h0b = _mm_rms_res(y_rg, w_out, h0a, gain("mix_post_g", 0), 1.0, "rg_out_proj")
    land_weights(3, h0b)
    h1, sv_f2_0 = ffn_fwd(h0b, "ffn2", 0)
    hn_kv = _rms_fwd(h1, a["kv_norm_g"].reshape(1, d), "kv_norm")
    kv = _mm([(hn_kv, full["w_kv"])], "nt", MXU_DTYPE, "kv_proj")
    fpre = _mm([(hn_kv, w_f_t)], "nt", F32, "fgate_proj")
    c_cum = _fgate_fwd(fpre, b_f, "fgate_cumsum")
    c_heads = c_cum[:, :n_heads].T
    c_col, c_row = c_heads[:, :, None], c_heads[:, None, :]
    land_weights(4, c_cum)
    h1a, sv_f1_1 = ffn_fwd(h1, "ffn1", 1)
    hn_at = _rms_fwd(h1a, gain("mix_pre_g", 1), "attn_pre_norm")
    q_s = _mm([(hn_at, full["attn_w_q"])], "nn", MXU_DTYPE, "q_proj", out_scale=attn_scale)
    o2, lse = _pair_attn_fwd(q_s, kv, kv[:, d_attn:].T, c_col, c_row, "attn_fwd")
    m_at, h1b = _mm_rms_res(o2, full["attn_w_o"], h1a, gain("mix_post_g", 1), 1.0, "attn_out_proj")
    land_weights(5, h1b)
    y, sv_f2_1 = ffn_fwd(h1b, "ffn2", 1)
    dy, loss_part = _loss_head(y, target, "loss_head")

    grads_big = {}
    grads_rep = {}

    bwd_chunks = [ffn_names("ffn2", 1), ["attn_w_q", "attn_w_o"] + ffn_names("ffn1", 1),
                  ["w_kv"] + ffn_names("ffn2", 0), ["rg_w_in", "rg_w_out"], ffn_names("ffn1", 0)]
    bwd_sends, bwd_handles = [], []

    def send_grads(after):
        n = len(bwd_sends)
        send = jnp.concatenate(
            [jnp.pad(grads_big[nm].reshape(N_DEV, n_rows, d), ((0, 0), (0, _round_up(n_rows, WIRE_ROW_ALIGN) - n_rows), (0, 0)))
             for nm, _, n_rows in chunk_layout(bwd_chunks[n])[0]], axis=1)
        handles, token = _exchange_start([send], True, [after], f"exchange_grads_start_{n}")
        bwd_sends.append(send)
        bwd_handles.append(handles[0])
        return token

    def ffn_bwd(dh_out, saved, f, l, after=None, send_now=False):
        h, xn, g, u, act, fo = saved
        df, d_post = _rms_bwd(fo, gain(f + "_post_g", l), [dh_out], None, 0.5, MXU_DTYPE, f"{f}_{l}_post_norm_bwd", after)
        dg, du = _ffn_act_bwd(df, full[(f, "down", l)], g, u, f"{f}_{l}_act_bwd")
        grads_big[(f, "down", l)] = _mm([(act, df)], "tn", WIRE_DTYPE, f"{f}_{l}_dw_down")
        grads_big[(f, "gate", l)] = _mm([(dg, xn)], "tn", WIRE_DTYPE, f"{f}_{l}_dw_gate")
        grads_big[(f, "up", l)] = _mm([(du, xn)], "tn", WIRE_DTYPE, f"{f}_{l}_dw_up")
        sent = send_grads(df) if send_now else None
        dxn = _mm([(dg, full[(f, "gate", l)]), (du, full[(f, "up", l)])], "nn", F32, f"{f}_{l}_dx", sent)
        dh_in, d_pre = _rms_bwd(h, gain(f + "_pre_g", l), [dxn], dh_out, 1.0, F32, f"{f}_{l}_pre_norm_bwd")
        grads_rep[(f + "_post_g", l)] = d_post
        grads_rep[(f + "_pre_g", l)] = d_pre
        return dh_in

    dh = ffn_bwd(dy, sv_f2_1, "ffn2", 1)
    token = send_grads(dh)
    dm, d_post = _rms_bwd(m_at, gain("mix_post_g", 1), [dh], None, 1.0, MXU_DTYPE, "attn_post_norm_bwd", token)
    grads_rep[("mix_post_g", 1)] = d_post
    do2 = _mm([(dm, full["attn_w_o"])], "nt", F32, "attn_out_proj_dx")
    grads_big["attn_w_o"] = _mm([(o2, dm)], "tn", WIRE_DTYPE, "attn_out_proj_dw")
    delta = _attn_delta(do2, o2, n_heads, "attn_delta")
    dq2, dc_q, dk2, dv2, dc_k = _pair_attn_bwd(q_s, kv, c_col, c_row, lse, delta, do2, attn_scale, "attn_bwd")
    dc_heads = dc_q[:, 0, :] + dc_k[:, :, 0]
    dhn = _mm([(dq2, full["attn_w_q"])], "nt", F32, "q_proj_dx")
    grads_big["attn_w_q"] = _mm([(hn_at, dq2)], "tn", WIRE_DTYPE, "q_proj_dw")
    dh, d_pre = _rms_bwd(h1a, gain("mix_pre_g", 1), [dhn], dh, 1.0, F32, "attn_pre_norm_bwd")
    grads_rep[("mix_pre_g", 1)] = d_pre
    dh = ffn_bwd(dh, sv_f1_1, "ffn1", 1)
    token = send_grads(dh)
    dc_cum = _pad_cols(dc_heads.T, LANES)
    dfpre, db_f = _fgate_bwd(dc_cum, fpre, b_f, "fgate_cumsum_bwd")
    dhn_kv = _mm([(dk2, full["w_kv"][:d_attn]), (dv2, full["w_kv"][d_attn:])], "nn", F32, "kv_proj_dx")
    dhn_f = _mm([(dfpre, w_f_t)], "nn", F32, "fgate_proj_dx")
    grads_big["w_kv"] = jnp.concatenate([_mm([(dk2, hn_kv)], "tn", WIRE_DTYPE, "kv_proj_dw_k"),
                                         _mm([(dv2, hn_kv)], "tn", WIRE_DTYPE, "kv_proj_dw_v")], axis=0)
    dw_f_t = _mm([(dfpre, hn_kv)], "tn", F32, "fgate_proj_dw")
    dh, d_kvg = _rms_bwd(h1, a["kv_norm_g"].reshape(1, d), [dhn_kv, dhn_f], dh, 1.0, F32, "kv_norm_bwd", token)
    dh = ffn_bwd(dh, sv_f2_0, "ffn2", 0)
    token = send_grads(dh)
    dm, d_post = _rms_bwd(m_rg, gain("mix_post_g", 0), [dh], None, 1.0, MXU_DTYPE, "rg_post_norm_bwd", token)
    grads_rep[("mix_post_g", 0)] = d_post
    dy_rg = _mm([(dm, w_out)], "nt", F32, "rg_out_proj_dx")
    dw_out = _mm([(y_rg, dm)], "tn", WIRE_DTYPE, "rg_out_proj_dw")
    dgate, dra, dia, drec1, dpvec = _scan_bwd(dy_rg, gx, h_rec, rec, gates, pvec, "rg_scan_bwd")
    drec2 = _mm([(dra, wa_dense), (dia, wx_dense)], "nt", F32, "rg_gate_proj_dx")
    dwa_dense = _mm([(rec, dra)], "tn", F32, "rg_gate_proj_dwa")
    dwx_dense = _mm([(rec, dia)], "tn", F32, "rg_gate_proj_dwx")
    drec0, dpconv = _conv_bwd(drec1, drec2, gx, pconv, conv_width, "rg_conv_bwd")
    dhn = _mm([(dgate, w_in_gate), (drec0, w_in_rec)], "nn", F32, "rg_in_proj_dx")
    dw_in_gate = _mm([(dgate, hn_rg)], "tn", WIRE_DTYPE, "rg_in_proj_dw_gate")
    dw_in_rec = _mm([(drec0, hn_rg)], "tn", WIRE_DTYPE, "rg_in_proj_dw_rec")
    dh, d_pre = _rms_bwd(h0a, gain("mix_pre_g", 0), [dhn], dh, 1.0, F32, "rg_pre_norm_bwd")
    grads_rep[("mix_pre_g", 0)] = d_pre
    grads_big["rg_w_in"] = jnp.concatenate([dw_in_gate[:c_dim], dw_in_rec[:c_dim]], axis=0)
    grads_big["rg_w_out"] = dw_out[:c_dim]
    token = send_grads(dh)
    grad_x = ffn_bwd(dh, sv_f1_0, "ffn1", 0, token, send_now=True)

    g_shard = {}

    def land_grads(n, after):
        land = _exchange_wait(bwd_handles[n], after, f"exchange_grads_wait_{n}")
        own = lax.dynamic_index_in_dim(bwd_sends[n], me, axis=0, keepdims=False)
        g_chunk = _sum_slots(_own_slot(land, own, me), f"sum_weight_grads_{n}")
        for nm, o, n_rows in chunk_layout(bwd_chunks[n])[0]:
            g_shard[nm] = g_chunk[o:o + n_rows]

    for n in range(len(bwd_chunks) - 1):
        land_grads(n, grad_x)

    def gain_grad(name):
        return jnp.concatenate([grads_rep[(name, l)] for l in range(n_layers)], axis=0)

    rep_names = ["ffn1_pre_g", "ffn1_post_g", "mix_pre_g", "mix_post_g", "ffn2_pre_g", "ffn2_post_g"]
    rep_parts = [gain_grad(nm) for nm in rep_names]
    rep_names += ["kv_norm_g", "b_fgate", "rg_w_a", "rg_w_x", "rg_conv_w", "rg_conv_b", "rg_b_a", "rg_b_x", "rg_lambda", "w_fgate"]
    rep_parts += [
        d_kvg, db_f[0, :n_heads],
        _diag_blocks(dwa_dense, n_blocks, lru_block), _diag_blocks(dwx_dense, n_blocks, lru_block),
        dpconv[:conv_width, :c_dim], dpconv[conv_width, :c_dim],
        dpvec[0, :c_dim], dpvec[1, :c_dim], dpvec[2, :c_dim],
        dw_f_t[:n_heads].T]
    rep_pack, rep_spans = _pack_rows(rep_parts, d, F32, WIRE_ROW_ALIGN)
    rep_sum = _sum_slots(_all_gather(rep_pack, "gather_small_grads"), "sum_small_grads")
    g_rep = {nm: _unpack_rows(rep_sum, sp, p.shape) for nm, sp, p in zip(rep_names, rep_spans, rep_parts)}

    def my_cols(full_grad, n):
        return lax.dynamic_slice_in_dim(full_grad, me * n, n, axis=full_grad.ndim - 1)

    def ffn_grads(f):
        grad[f + "_w_gate"] = jnp.stack([g_shard[(f, "gate", l)].T for l in range(n_layers)])
        grad[f + "_w_up"] = jnp.stack([g_shard[(f, "up", l)].T for l in range(n_layers)])
        grad[f + "_w_down"] = jnp.stack([g_shard[(f, "down", l)] for l in range(n_layers)])

    grad = {}
    for nm in ("ffn1_pre_g", "ffn1_post_g", "mix_pre_g", "mix_post_g", "ffn2_pre_g", "ffn2_post_g"):
        grad[nm] = g_rep[nm]
    ffn_grads("ffn2")
    grad["rg_w_in"] = g_shard["rg_w_in"].T[None]
    grad["rg_conv_w"] = my_cols(g_rep["rg_conv_w"], c_shard)[None]
    for nm in ("rg_conv_b", "rg_b_a", "rg_b_x", "rg_lambda"):
        grad[nm] = my_cols(g_rep[nm], c_shard)[None]
    grad["rg_w_a"] = g_rep["rg_w_a"][None]
    grad["rg_w_x"] = g_rep["rg_w_x"][None]
    grad["rg_w_out"] = g_shard["rg_w_out"][None]
    grad["kv_norm_g"] = g_rep["kv_norm_g"].reshape(d)
    grad["w_kv"] = g_shard["w_kv"].T
    grad["w_fgate"] = lax.dynamic_slice_in_dim(g_rep["w_fgate"], me * (d // N_DEV), d // N_DEV, axis=0)
    grad["b_fgate"] = g_rep["b_fgate"]
    grad["attn_w_q"] = g_shard["attn_w_q"][None]
    grad["attn_w_o"] = g_shard["attn_w_o"][None]

    delta, new_m, new_v = {}, {}, {}

    def adamw(nm):
        w = a[nm]
        shape = w.shape
        two_d = (1, shape[0]) if w.ndim == 1 else (-1, shape[-1])
        dl, mo, vo = _adamw(w.reshape(two_d), grad[nm].reshape(two_d), a["m_" + nm].reshape(two_d),
                            a["v_" + nm].reshape(two_d), "adamw_" + nm)
        delta[nm], new_m[nm], new_v[nm] = dl.reshape(shape), mo.reshape(shape), vo.reshape(shape)
        grad[nm] = grad[nm].reshape(shape)

    last_names = ("ffn1_w_gate", "ffn1_w_up", "ffn1_w_down")
    for nm in WEIGHT_NAMES:
        if nm not in last_names:
            adamw(nm)
    land_grads(len(bwd_chunks) - 1, delta["attn_w_o"])
    ffn_grads("ffn1")
    for nm in last_names:
        adamw(nm)

    loss = lax.psum(loss_part[0, 0], AXES)
    return (loss, grad_x[None], *[grad[n] for n in WEIGHT_NAMES], *[delta[n] for n in WEIGHT_NAMES],
            *[new_m[n] for n in WEIGHT_NAMES], *[new_v[n] for n in WEIGHT_NAMES])


def kernel(x, ffn1_pre_g, ffn1_w_gate, ffn1_w_up, ffn1_w_down, ffn1_post_g, mix_pre_g, mix_post_g, ffn2_pre_g, ffn2_w_gate, ffn2_w_up, ffn2_w_down, ffn2_post_g, rg_w_in, rg_conv_w, rg_conv_b, rg_w_a, rg_b_a, rg_w_x, rg_b_x, rg_lambda, rg_w_out, kv_norm_g, w_kv, w_fgate, b_fgate, attn_w_q, attn_w_o, loss_target, m_ffn1_pre_g, m_ffn1_w_gate, m_ffn1_w_up, m_ffn1_w_down, m_ffn1_post_g, m_mix_pre_g, m_mix_post_g, m_ffn2_pre_g, m_ffn2_w_gate, m_ffn2_w_up, m_ffn2_w_down, m_ffn2_post_g, m_rg_w_in, m_rg_conv_w, m_rg_conv_b, m_rg_w_a, m_rg_b_a, m_rg_w_x, m_rg_b_x, m_rg_lambda, m_rg_w_out, m_kv_norm_g, m_w_kv, m_w_fgate, m_b_fgate, m_attn_w_q, m_attn_w_o, v_ffn1_pre_g, v_ffn1_w_gate, v_ffn1_w_up, v_ffn1_w_down, v_ffn1_post_g, v_mix_pre_g, v_mix_post_g, v_ffn2_pre_g, v_ffn2_w_gate, v_ffn2_w_up, v_ffn2_w_down, v_ffn2_post_g, v_rg_w_in, v_rg_conv_w, v_rg_conv_b, v_rg_w_a, v_rg_b_a, v_rg_w_x, v_rg_b_x, v_rg_lambda, v_rg_w_out, v_kv_norm_g, v_w_kv, v_w_fgate, v_b_fgate, v_attn_w_q, v_attn_w_o):
    return _train_step(dict(locals()))
```

```python
import functools

import jax
import jax.numpy as jnp
from jax import lax
from jax.experimental import pallas as pl
from jax.experimental.pallas import tpu as pltpu

F32 = jnp.float32
MXU_DTYPE = jnp.bfloat16
WIRE_DTYPE = jnp.bfloat16
N_DEV = 8
AXES = ("x", "y", "c")
LANES = 128
WIRE_ROW_ALIGN = 16
VMEM_LIMIT_MIN = 32 * 2 ** 20
VMEM_LIMIT_MAX = 56 * 2 ** 20

RMS_EPS = 1e-6
LRU_C = 8.0
ADAM_LR, ADAM_B1, ADAM_B2, ADAM_EPS, ADAM_WD, ADAM_STEP = 0.001, 0.9, 0.999, 1e-08, 0.01, 10

WEIGHT_NAMES = (
    "ffn1_pre_g", "ffn1_w_gate", "ffn1_w_up", "ffn1_w_down", "ffn1_post_g", "mix_pre_g", "mix_post_g",
    "ffn2_pre_g", "ffn2_w_gate", "ffn2_w_up", "ffn2_w_down", "ffn2_post_g", "rg_w_in", "rg_conv_w",
    "rg_conv_b", "rg_w_a", "rg_b_a", "rg_w_x", "rg_b_x", "rg_lambda", "rg_w_out", "kv_norm_g", "w_kv",
    "w_fgate", "b_fgate", "attn_w_q", "attn_w_o")


def _round_up(n, m):
    return (n + m - 1) // m * m


def _tile(dim, target, align=LANES):
    if dim <= target:
        return dim
    best = None
    t = align
    while t <= target:
        if dim % t == 0:
            best = t
        t += align
    return dim if best is None else best


def _cparams(semantics, vmem_estimate):
    limit = min(VMEM_LIMIT_MAX, max(VMEM_LIMIT_MIN, 2 * int(vmem_estimate)))
    return pltpu.CompilerParams(dimension_semantics=semantics, vmem_limit_bytes=limit)


def _nbytes(shape, dtype):
    n = 1
    for s in shape:
        n *= s
    return n * jnp.dtype(dtype).itemsize


def _sigmoid(x):
    return jax.nn.sigmoid(x)


def _softplus(x):
    return jnp.maximum(x, 0.0) + jnp.log1p(jnp.exp(-jnp.abs(x)))


def _expm1(x):
    series = x * (1.0 + x * (0.5 + x * (1.0 / 6.0 + x * (1.0 / 24.0 + x * (1.0 / 120.0)))))
    return jnp.where(jnp.abs(x) < 0.25, series, jnp.exp(x) - 1.0)


_GELU_C = 0.7978845608028654
_GELU_A = 0.044715


def _gelu(x):
    return 0.5 * x * (1.0 + jnp.tanh(_GELU_C * (x + _GELU_A * x * x * x)))


def _gelu_grad(x):
    t = jnp.tanh(_GELU_C * (x + _GELU_A * x * x * x))
    return 0.5 * (1.0 + t) + 0.5 * x * (1.0 - t * t) * _GELU_C * (1.0 + 3.0 * _GELU_A * x * x)


_DOT_DIMS = {"nn": ((1,), (0,)), "nt": ((1,), (1,)), "tn": ((0,), (0,))}


def _dot(a, b, mode):
    return lax.dot_general(a.astype(MXU_DTYPE), b.astype(MXU_DTYPE), (_DOT_DIMS[mode], ((), ())),
                           preferred_element_type=F32)


def _mm(pairs, mode, out_dtype, name, after=None, out_scale=None):
    a0, b0 = pairs[0]
    if mode == "tn":
        k_dim, m_dim = a0.shape
        n_dim = b0.shape[1]
    else:
        m_dim, k_dim = a0.shape
        n_dim = b0.shape[0] if mode == "nt" else b0.shape[1]
    for a, b in pairs:
        assert a.shape == a0.shape and b.shape == b0.shape
    tm = _tile(m_dim, 1408 if mode == "tn" else 512)
    whole = 1408 if mode == "tn" else 2816
    tn = _tile(n_dim, whole)
    tk = _tile(k_dim, whole)
    nk = k_dim // tk
    n_pairs = len(pairs)

    if mode == "tn":
        a_spec = pl.BlockSpec((tk, tm), lambda i, j, k: (k, i))
    else:
        a_spec = pl.BlockSpec((tm, tk), lambda i, j, k: (i, k))
    if mode == "nt":
        b_spec = pl.BlockSpec((tn, tk), lambda i, j, k: (j, k))
    else:
        b_spec = pl.BlockSpec((tk, tn), lambda i, j, k: (k, j))

    order = [] if after is None else [after]

    def body(*refs):
        ins, o_ref, acc = refs[:2 * n_pairs], refs[-2], refs[-1]
        k = pl.program_id(2)

        @pl.when(k == 0)
        def _():
            acc[...] = jnp.zeros_like(acc)

        s = acc[...]
        for p in range(n_pairs):
            s = s + _dot(ins[2 * p][...], ins[2 * p + 1][...], mode)
        acc[...] = s

        @pl.when(k == nk - 1)
        def _():
            r = acc[...] if out_scale is None else acc[...] * out_scale
            o_ref[...] = r.astype(out_dtype)

    est = (2 * n_pairs * (_nbytes((tm, tk), a0.dtype) + _nbytes((tk, tn), b0.dtype))
           + 2 * _nbytes((tm, tn), out_dtype) + 2 * _nbytes((tm, tn), F32))
    flat = [t for ab in pairs for t in ab]
    return pl.pallas_call(
        body, name=name, grid=(m_dim // tm, n_dim // tn, nk),
        in_specs=[a_spec, b_spec] * n_pairs + [_ANY] * len(order),
        out_specs=pl.BlockSpec((tm, tn), lambda i, j, k: (i, j)),
        out_shape=jax.ShapeDtypeStruct((m_dim, n_dim), out_dtype),
        scratch_shapes=[pltpu.VMEM((tm, tn), F32)],
        compiler_params=_cparams(("parallel", "parallel", "arbitrary"), est),
    )(*flat, *order)


_ANY = pl.BlockSpec(memory_space=pl.ANY)


def _rms_fwd(x, gain, name, after=None):
    s_dim, d = x.shape
    tm = _tile(s_dim, 512, 8)

    def body(*refs):
        x_ref, g_ref, o_ref = refs[0], refs[1], refs[-1]
        v = x_ref[...]
        r = lax.rsqrt(jnp.mean(v * v, axis=-1, keepdims=True) + RMS_EPS)
        o_ref[...] = (v * r * g_ref[...]).astype(MXU_DTYPE)

    order = [] if after is None else [after]
    return pl.pallas_call(
        body, name=name, grid=(s_dim // tm,),
        in_specs=[pl.BlockSpec((tm, d), lambda i: (i, 0)), pl.BlockSpec((1, d), lambda i: (0, 0))] + [_ANY] * len(order),
        out_specs=pl.BlockSpec((tm, d), lambda i: (i, 0)),
        out_shape=jax.ShapeDtypeStruct((s_dim, d), MXU_DTYPE),
        compiler_params=_cparams(("parallel",), 6 * _nbytes((tm, d), F32)),
    )(x, gain, *order)


def _rms_bwd(x, gain, dys, res, scale, out_dtype, name, after=None):
    s_dim, d = x.shape
    tm = _tile(s_dim, 512, 8)
    n_dy = len(dys)
    has_res = res is not None
    order = [] if after is None else [after]

    def body(*refs):
        x_ref, g_ref = refs[0], refs[1]
        dy_refs = refs[2:2 + n_dy]
        res_ref = refs[2 + n_dy] if has_res else None
        dx_ref, dg_ref = refs[-2], refs[-1]

        @pl.when(pl.program_id(0) == 0)
        def _():
            dg_ref[...] = jnp.zeros_like(dg_ref)

        v = x_ref[...]
        r = lax.rsqrt(jnp.mean(v * v, axis=-1, keepdims=True) + RMS_EPS)
        xh = v * r
        dy = dy_refs[0][...].astype(F32)
        for extra in dy_refs[1:]:
            dy = dy + extra[...].astype(F32)
        gd = dy * g_ref[...]
        dx = scale * r * (gd - xh * jnp.mean(gd * xh, axis=-1, keepdims=True))
        if has_res:
            dx = dx + res_ref[...]
        dx_ref[...] = dx.astype(out_dtype)
        dg_ref[...] += scale * jnp.sum(dy * xh, axis=0, keepdims=True)

    row = pl.BlockSpec((tm, d), lambda i: (i, 0))
    vec = pl.BlockSpec((1, d), lambda i: (0, 0))
    ops = [x, gain] + list(dys) + ([res] if has_res else [])
    return pl.pallas_call(
        body, name=name, grid=(s_dim // tm,),
        in_specs=[row, vec] + [row] * (n_dy + int(has_res)) + [_ANY] * len(order),
        out_specs=[row, vec],
        out_shape=[jax.ShapeDtypeStruct((s_dim, d), out_dtype), jax.ShapeDtypeStruct((1, d), F32)],
        compiler_params=_cparams(("arbitrary",), (2 * len(ops) + 6) * _nbytes((tm, d), F32)),
    )(*ops, *order)


def _mm_rms_res(a, b, h, gain, scale, name):
    s_dim, k_dim = a.shape
    d = b.shape[1]
    tm = _tile(s_dim, 512, 8)
    tk = _tile(k_dim, 2816)
    nk = k_dim // tk

    def body(a_ref, b_ref, h_ref, g_ref, f_ref, o_ref, acc):
        k = pl.program_id(1)

        @pl.when(k == 0)
        def _():
            acc[...] = jnp.zeros_like(acc)

        acc[...] += _dot(a_ref[...], b_ref[...], "nn")

        @pl.when(k == nk - 1)
        def _():
            f = acc[...]
            r = lax.rsqrt(jnp.mean(f * f, axis=-1, keepdims=True) + RMS_EPS)
            f_ref[...] = f
            o_ref[...] = h_ref[...] + scale * (f * r * g_ref[...])

    row = pl.BlockSpec((tm, d), lambda i, k: (i, 0))
    est = (2 * (_nbytes((tm, tk), a.dtype) + _nbytes((tk, d), b.dtype)) + 8 * _nbytes((tm, d), F32))
    return pl.pallas_call(
        body, name=name, grid=(s_dim // tm, nk),
        in_specs=[pl.BlockSpec((tm, tk), lambda i, k: (i, k)), pl.BlockSpec((tk, d), lambda i, k: (k, 0)),
                  row, pl.BlockSpec((1, d), lambda i, k: (0, 0))],
        out_specs=[row, row],
        out_shape=[jax.ShapeDtypeStruct((s_dim, d), F32), jax.ShapeDtypeStruct((s_dim, d), F32)],
        scratch_shapes=[pltpu.VMEM((tm, d), F32)],
        compiler_params=_cparams(("parallel", "arbitrary"), est),
    )(a, b, h, gain)


def _ffn_up(xn, wg_t, wu_t, name):
    s_dim, d = xn.shape
    f_dim = wg_t.shape[0]
    tm = _tile(s_dim, 1024, 8)
    tf = _tile(f_dim, 256)

    def body(x_ref, wg_ref, wu_ref, g_ref, u_ref, a_ref):
        x = x_ref[...]
        g = _dot(x, wg_ref[...], "nt")
        u = _dot(x, wu_ref[...], "nt")
        g_ref[...] = g.astype(MXU_DTYPE)
        u_ref[...] = u.astype(MXU_DTYPE)
        a_ref[...] = (g * _sigmoid(g) * u).astype(MXU_DTYPE)

    w_spec = pl.BlockSpec((tf, d), lambda i, j: (j, 0))
    o_spec = pl.BlockSpec((tm, tf), lambda i, j: (i, j))
    o_shape = jax.ShapeDtypeStruct((s_dim, f_dim), MXU_DTYPE)
    est = 2 * _nbytes((tm, d), xn.dtype) + 4 * _nbytes((tf, d), wg_t.dtype) + 10 * _nbytes((tm, tf), F32)
    return pl.pallas_call(
        body, name=name, grid=(s_dim // tm, f_dim // tf),
        in_specs=[pl.BlockSpec((tm, d), lambda i, j: (i, 0)), w_spec, w_spec],
        out_specs=[o_spec, o_spec, o_spec], out_shape=[o_shape, o_shape, o_shape],
        compiler_params=_cparams(("parallel", "parallel"), est),
    )(xn, wg_t, wu_t)


def _ffn_act_bwd(df, wd, g, u, name):
    s_dim, d = df.shape
    f_dim = wd.shape[0]
    tm = _tile(s_dim, 1024, 8)
    tf = _tile(f_dim, 256)

    def body(df_ref, wd_ref, g_ref, u_ref, dg_ref, du_ref):
        dh = _dot(df_ref[...], wd_ref[...], "nt")
        gv = g_ref[...].astype(F32)
        uv = u_ref[...].astype(F32)
        sg = _sigmoid(gv)
        dg_ref[...] = (dh * uv * (sg * (1.0 + gv * (1.0 - sg)))).astype(MXU_DTYPE)
        du_ref[...] = (dh * gv * sg).astype(MXU_DTYPE)

    t_spec = pl.BlockSpec((tm, tf), lambda i, j: (i, j))
    o_shape = jax.ShapeDtypeStruct((s_dim, f_dim), MXU_DTYPE)
    est = 2 * _nbytes((tm, d), df.dtype) + 2 * _nbytes((tf, d), wd.dtype) + 12 * _nbytes((tm, tf), F32)
    return pl.pallas_call(
        body, name=name, grid=(s_dim // tm, f_dim // tf),
        in_specs=[pl.BlockSpec((tm, d), lambda i, j: (i, 0)), pl.BlockSpec((tf, d), lambda i, j: (j, 0)),
                  t_spec, t_spec],
        out_specs=[t_spec, t_spec], out_shape=[o_shape, o_shape],
        compiler_params=_cparams(("parallel", "parallel"), est),
    )(df, wd, g, u)


def _loss_head(y, target, name):
    s_dim, d = y.shape
    tm = _tile(s_dim, 512, 8)
    nt = s_dim // tm

    def body(y_ref, t_ref, dy_ref, loss_ref, acc):
        i = pl.program_id(0)

        @pl.when(i == 0)
        def _():
            acc[...] = jnp.zeros_like(acc)

        e = y_ref[...] - t_ref[...]
        dy_ref[...] = e * (1.0 / d)
        acc[...] += jnp.sum(e * e, axis=0, keepdims=True)

        @pl.when(i == nt - 1)
        def _():
            loss_ref[...] = jnp.sum(acc[...], axis=1, keepdims=True) * (0.5 / d)

    row = pl.BlockSpec((tm, d), lambda i: (i, 0))
    return pl.pallas_call(
        body, name=name, grid=(nt,), in_specs=[row, row],
        out_specs=[row, pl.BlockSpec((1, 1), lambda i: (0, 0))],
        out_shape=[jax.ShapeDtypeStruct((s_dim, d), F32), jax.ShapeDtypeStruct((1, 1), F32)],
        scratch_shapes=[pltpu.VMEM((1, d), F32)],
        compiler_params=_cparams(("arbitrary",), 8 * _nbytes((tm, d), F32)),
    )(y, target)


def _shift_down(v, sh, row):
    if sh == 0:
        return v
    return jnp.where(row >= sh, pltpu.roll(v, sh, 0), 0.0)


def _shift_up(v, sh, row):
    if sh == 0:
        return v
    n = v.shape[0]
    return jnp.where(row < n - sh, pltpu.roll(v, n - sh, 0), 0.0)


def _conv_fwd(gx, pconv, width, name):
    s_dim, cp2 = gx.shape
    cp = cp2 // 2
    nc = cp // LANES

    def body(x_ref, p_ref, o_ref):
        x = x_ref[...]
        row = lax.broadcasted_iota(jnp.int32, x.shape, 0)
        y = jnp.zeros_like(x) + p_ref[pl.ds(width, 1), :]
        for k in range(width):
            y = y + p_ref[pl.ds(k, 1), :] * _shift_down(x, width - 1 - k, row)
        o_ref[...] = y

    return pl.pallas_call(
        body, name=name, grid=(nc,),
        in_specs=[pl.BlockSpec((s_dim, LANES), lambda j: (0, nc + j)), pl.BlockSpec((8, LANES), lambda j: (0, j))],
        out_specs=pl.BlockSpec((s_dim, LANES), lambda j: (0, j)),
        out_shape=jax.ShapeDtypeStruct((s_dim, cp), F32),
        compiler_params=_cparams(("parallel",), 10 * _nbytes((s_dim, LANES), F32)),
    )(gx, pconv)


def _conv_bwd(d1, d2, gx, pconv, width, name):
    s_dim, cp = d1.shape
    nc = cp // LANES

    def body(d1_ref, d2_ref, x_ref, p_ref, dx_ref, dp_ref):
        d = d1_ref[...] + d2_ref[...]
        x = x_ref[...]
        row = lax.broadcasted_iota(jnp.int32, x.shape, 0)
        dx = jnp.zeros_like(d)
        dp_ref[...] = jnp.zeros_like(dp_ref)
        for k in range(width):
            sh = width - 1 - k
            dx = dx + p_ref[pl.ds(k, 1), :] * _shift_up(d, sh, row)
            dp_ref[pl.ds(k, 1), :] = jnp.sum(d * _shift_down(x, sh, row), axis=0, keepdims=True)
        dp_ref[pl.ds(width, 1), :] = jnp.sum(d, axis=0, keepdims=True)
        dx_ref[...] = dx.astype(MXU_DTYPE)

    strip = pl.BlockSpec((s_dim, LANES), lambda j: (0, j))
    par = pl.BlockSpec((8, LANES), lambda j: (0, j))
    return pl.pallas_call(
        body, name=name, grid=(nc,),
        in_specs=[strip, strip, pl.BlockSpec((s_dim, LANES), lambda j: (0, nc + j)), par],
        out_specs=[strip, par],
        out_shape=[jax.ShapeDtypeStruct((s_dim, cp), MXU_DTYPE), jax.ShapeDtypeStruct((8, cp), F32)],
        compiler_params=_cparams(("parallel",), 14 * _nbytes((s_dim, LANES), F32)),
    )(d1, d2, gx, pconv)


def _lru_coeffs(ra, ia, p_ref):
    r = _sigmoid(ra + p_ref[pl.ds(0, 1), :])
    i = _sigmoid(ia + p_ref[pl.ds(1, 1), :])
    sp = _softplus(-p_ref[pl.ds(2, 1), :])
    log_a = -LRU_C * r * sp
    a = jnp.exp(log_a)
    mult = jnp.sqrt(-_expm1(2.0 * log_a))
    return r, i, sp, a, mult


def _scan_fwd(gx, rec, gates, pvec, name):
    s_dim, cp = rec.shape
    ts = _tile(s_dim, 256, 8)
    nt = s_dim // ts

    def body(gate_ref, rec_ref, ra_ref, ia_ref, p_ref, h_ref, y_ref, a_s, u_s, carry):
        @pl.when(pl.program_id(0) == 0)
        def _():
            carry[...] = jnp.zeros_like(carry)

        rec_v = rec_ref[...]
        _, i, _, a, mult = _lru_coeffs(ra_ref[...], ia_ref[...], p_ref)
        a_s[...] = a
        u_s[...] = mult * (i * rec_v)

        def step(t, h):
            h = a_s[pl.ds(t, 1), :] * h + u_s[pl.ds(t, 1), :]
            h_ref[pl.ds(t, 1), :] = h
            return h

        carry[pl.ds(0, 1), :] = lax.fori_loop(0, ts, step, carry[pl.ds(0, 1), :], unroll=8)
        y_ref[...] = (_gelu(gate_ref[...]) * h_ref[...]).astype(MXU_DTYPE)

    blk = pl.BlockSpec((ts, cp), lambda t: (t, 0))
    return pl.pallas_call(
        body, name=name, grid=(nt,),
        in_specs=[blk, blk, blk, pl.BlockSpec((ts, cp), lambda t: (t, 1)), pl.BlockSpec((8, cp), lambda t: (0, 0))],
        out_specs=[blk, blk],
        out_shape=[jax.ShapeDtypeStruct((s_dim, cp), F32), jax.ShapeDtypeStruct((s_dim, cp), MXU_DTYPE)],
        scratch_shapes=[pltpu.VMEM((ts, cp), F32), pltpu.VMEM((ts, cp), F32), pltpu.VMEM((8, cp), F32)],
        compiler_params=_cparams(("arbitrary",), 14 * _nbytes((ts, cp), F32)),
    )(gx, rec, gates, gates, pvec)


def _scan_bwd(dy, gx, hrec, rec, gates, pvec, name):
    s_dim, cp = rec.shape
    ts = _tile(s_dim, 128, 8)
    nt = s_dim // ts

    def body(dy_ref, gate_ref, h_ref, hp_ref, rec_ref, ra_ref, ia_ref, p_ref,
             dgate_ref, dra_ref, dia_ref, drec_ref, dp_ref, a_s, d_s, carry):
        t_id = pl.program_id(0)

        @pl.when(t_id == 0)
        def _():
            carry[...] = jnp.zeros_like(carry)
            dp_ref[...] = jnp.zeros_like(dp_ref)

        rec_v = rec_ref[...]
        r, i, sp, a, mult = _lru_coeffs(ra_ref[...], ia_ref[...], p_ref)
        gate = gate_ref[...]
        dyv = dy_ref[...]
        h = h_ref[...]
        dgate_ref[...] = (dyv * h * _gelu_grad(gate)).astype(MXU_DTYPE)
        a_s[...] = a
        d_s[...] = dyv * _gelu(gate)

        def step(k, c):
            t = ts - 1 - k
            d = d_s[pl.ds(t, 1), :] + c
            d_s[pl.ds(t, 1), :] = d
            return a_s[pl.ds(t, 1), :] * d

        carry[pl.ds(0, 1), :] = lax.fori_loop(0, ts, step, carry[pl.ds(0, 1), :], unroll=8)
        dh = d_s[...]
        row = lax.broadcasted_iota(jnp.int32, h.shape, 0)
        first = jnp.where(t_id == nt - 1, 0.0, 1.0) * hp_ref[pl.ds(7, 1), :]
        h_prev = jnp.where(row == 0, first, pltpu.roll(h, 1, 0))
        dix = dh * mult
        dla = dh * h_prev * a - dh * (i * rec_v) * (a * a) / mult
        dra = dla * (-LRU_C * sp) * r * (1.0 - r)
        dia = dix * rec_v * i * (1.0 - i)
        dra_ref[...] = dra.astype(MXU_DTYPE)
        dia_ref[...] = dia.astype(MXU_DTYPE)
        drec_ref[...] = dix * i
        dsp = jnp.sum(dla * (-LRU_C * r), axis=0, keepdims=True)
        dp_ref[pl.ds(0, 1), :] += jnp.sum(dra, axis=0, keepdims=True)
        dp_ref[pl.ds(1, 1), :] += jnp.sum(dia, axis=0, keepdims=True)
        dp_ref[pl.ds(2, 1), :] += dsp * (-_sigmoid(-p_ref[pl.ds(2, 1), :]))

    blk = pl.BlockSpec((ts, cp), lambda t: (nt - 1 - t, 0))
    prev = pl.BlockSpec((8, cp), lambda t: (jnp.maximum((nt - 1 - t) * (ts // 8) - 1, 0), 0))
    par = pl.BlockSpec((8, cp), lambda t: (0, 0))
    lo = jax.ShapeDtypeStruct((s_dim, cp), MXU_DTYPE)
    return pl.pallas_call(
        body, name=name, grid=(nt,),
        in_specs=[blk, blk, blk, prev, blk, blk, pl.BlockSpec((ts, cp), lambda t: (nt - 1 - t, 1)), par],
        out_specs=[blk, blk, blk, blk, par],
        out_shape=[lo, lo, lo, jax.ShapeDtypeStruct((s_dim, cp), F32), jax.ShapeDtypeStruct((8, cp), F32)],
        scratch_shapes=[pltpu.VMEM((ts, cp), F32), pltpu.VMEM((ts, cp), F32), pltpu.VMEM((8, cp), F32)],
        compiler_params=_cparams(("arbitrary",), 40 * _nbytes((ts, cp), F32)),
    )(dy, gx, hrec, hrec, rec, gates, gates, pvec)


def _fgate_fwd(fpre, bias, name):
    s_dim, w = fpre.shape
    ts = _tile(s_dim, 512, 8)

    def body(f_ref, b_ref, c_ref, lf_s, carry):
        @pl.when(pl.program_id(0) == 0)
        def _():
            carry[...] = jnp.zeros_like(carry)

        lf_s[...] = -_softplus(-(f_ref[...] + b_ref[pl.ds(0, 1), :]))

        def step(t, c):
            c = c + lf_s[pl.ds(t, 1), :]
            c_ref[pl.ds(t, 1), :] = c
            return c

        carry[pl.ds(0, 1), :] = lax.fori_loop(0, ts, step, carry[pl.ds(0, 1), :], unroll=8)

    blk = pl.BlockSpec((ts, w), lambda t: (t, 0))
    return pl.pallas_call(
        body, name=name, grid=(s_dim // ts,),
        in_specs=[blk, pl.BlockSpec((8, w), lambda t: (0, 0))], out_specs=blk,
        out_shape=jax.ShapeDtypeStruct((s_dim, w), F32),
        scratch_shapes=[pltpu.VMEM((ts, w), F32), pltpu.VMEM((8, w), F32)],
        compiler_params=_cparams(("arbitrary",), 12 * _nbytes((ts, w), F32)),
    )(fpre, bias)


def _fgate_bwd(dc, fpre, bias, name):
    s_dim, w = fpre.shape
    ts = _tile(s_dim, 512, 8)
    nt = s_dim // ts

    def body(dc_ref, f_ref, b_ref, df_ref, db_ref, d_s, carry):
        @pl.when(pl.program_id(0) == 0)
        def _():
            carry[...] = jnp.zeros_like(carry)
            db_ref[...] = jnp.zeros_like(db_ref)

        d_s[...] = dc_ref[...]

        def step(k, c):
            t = ts - 1 - k
            c = c + d_s[pl.ds(t, 1), :]
            d_s[pl.ds(t, 1), :] = c
            return c

        carry[pl.ds(0, 1), :] = lax.fori_loop(0, ts, step, carry[pl.ds(0, 1), :], unroll=8)
        df = d_s[...] * _sigmoid(-(f_ref[...] + b_ref[pl.ds(0, 1), :]))
        df_ref[...] = df
        db_ref[pl.ds(0, 1), :] += jnp.sum(df, axis=0, keepdims=True)

    blk = pl.BlockSpec((ts, w), lambda t: (nt - 1 - t, 0))
    par = pl.BlockSpec((8, w), lambda t: (0, 0))
    return pl.pallas_call(
        body, name=name, grid=(nt,), in_specs=[blk, blk, par], out_specs=[blk, par],
        out_shape=[jax.ShapeDtypeStruct((s_dim, w), F32), jax.ShapeDtypeStruct((8, w), F32)],
        scratch_shapes=[pltpu.VMEM((ts, w), F32), pltpu.VMEM((8, w), F32)],
        compiler_params=_cparams(("arbitrary",), 12 * _nbytes((ts, w), F32)),
    )(dc, fpre, bias)


def _head_lanes(hh, dh):
    lane = lax.broadcasted_iota(jnp.int32, (1, LANES), 1)
    return (lane >= hh * dh) & (lane < (hh + 1) * dh)


def _pair_attn_fwd(q, kv, v_t, c_col, c_row, name):
    s_dim, da = q.shape
    n_h = c_col.shape[0]
    dh = da // n_h
    assert LANES % dh == 0 and da % LANES == 0
    hb = LANES // dh
    n_blocks = da // LANES
    t = _tile(s_dim, 512, LANES)
    nb = s_dim // t

    pairs = [(i, j) for i in range(nb) for j in range(i + 1)]
    i_tab = jnp.asarray([p[0] for p in pairs], jnp.int32)
    j_tab = jnp.asarray([p[1] for p in pairs], jnp.int32)

    def body(i_ref, j_ref, q_ref, k_ref, vt_ref, cq_ref, ck_ref, o_ref, lse_ref, m_s, l_s, acc):
        i, j = i_ref[pl.program_id(1)], j_ref[pl.program_id(1)]

        @pl.when(j == 0)
        def _():
            m_s[...] = jnp.full_like(m_s, -jnp.inf)
            l_s[...] = jnp.zeros_like(l_s)
            acc[...] = jnp.zeros_like(acc)

        def tile(masked):
            qv = q_ref[...]
            for hh in range(hb):
                st = _dot(k_ref[...], jnp.where(_head_lanes(hh, dh), qv, jnp.zeros_like(qv)), "nt")
                st = st + (cq_ref[hh] - ck_ref[hh])
                if masked:
                    keep = lax.broadcasted_iota(jnp.int32, (t, t), 0) <= lax.broadcasted_iota(jnp.int32, (t, t), 1)
                    st = jnp.where(keep, st, -jnp.inf)
                m_prev = m_s[hh]
                m_new = jnp.maximum(m_prev, jnp.max(st, axis=0, keepdims=True))
                alpha = jnp.exp(m_prev - m_new)
                p = jnp.exp(st - m_new)
                l_s[hh] = alpha * l_s[hh] + jnp.sum(p, axis=0, keepdims=True)
                acc[hh] = alpha * acc[hh] + _dot(vt_ref[...], p, "nn")
                m_s[hh] = m_new

        pl.when(j < i)(functools.partial(tile, False))
        pl.when(j == i)(functools.partial(tile, True))

        @pl.when(j == i)
        def _():
            feat = lax.broadcasted_iota(jnp.int32, (LANES, 1), 0)
            out_t = jnp.zeros((LANES, t), F32)
            for hh in range(hb):
                out_t = jnp.where((feat >= hh * dh) & (feat < (hh + 1) * dh), acc[hh] / l_s[hh], out_t)
                lse_ref[hh] = m_s[hh] + jnp.log(l_s[hh])
            o_ref[...] = out_t.T

    q_spec = pl.BlockSpec((t, LANES), lambda b, p, it, jt: (it[p], b))
    k_spec = pl.BlockSpec((t, LANES), lambda b, p, it, jt: (jt[p], b))
    vt_spec = pl.BlockSpec((LANES, t), lambda b, p, it, jt: (b, jt[p]))
    cq_spec = pl.BlockSpec((hb, 1, t), lambda b, p, it, jt: (b, 0, it[p]))
    ck_spec = pl.BlockSpec((hb, t, 1), lambda b, p, it, jt: (b, jt[p], 0))
    return pl.pallas_call(
        body, name=name,
        grid_spec=pltpu.PrefetchScalarGridSpec(
            num_scalar_prefetch=2, grid=(n_blocks, len(pairs)),
            in_specs=[q_spec, k_spec, vt_spec, cq_spec, ck_spec], out_specs=[q_spec, cq_spec],
            scratch_shapes=[pltpu.VMEM((hb, 1, t), F32), pltpu.VMEM((hb, 1, t), F32), pltpu.VMEM((hb, LANES, t), F32)]),
        out_shape=[jax.ShapeDtypeStruct((s_dim, da), F32), jax.ShapeDtypeStruct((n_h, 1, s_dim), F32)],
        compiler_params=_cparams(("parallel", "arbitrary"), 10 * hb * _nbytes((t, t), F32)),
    )(i_tab, j_tab, q, kv, v_t, c_row, c_col)


def _attn_delta(do, o, n_h, name):
    s_dim, da = o.shape
    dh = da // n_h
    hb = LANES // dh
    t = _tile(s_dim, 512, LANES)

    def body(do_ref, o_ref, d_ref):
        prod_t = (do_ref[...].astype(MXU_DTYPE).astype(F32) * o_ref[...]).T
        for hh in range(hb):
            d_ref[hh] = jnp.sum(prod_t[hh * dh:(hh + 1) * dh], axis=0, keepdims=True)

    blk = pl.BlockSpec((t, LANES), lambda b, i: (i, b))
    return pl.pallas_call(
        body, name=name, grid=(da // LANES, s_dim // t), in_specs=[blk, blk],
        out_specs=pl.BlockSpec((hb, 1, t), lambda b, i: (b, 0, i)),
        out_shape=jax.ShapeDtypeStruct((n_h, 1, s_dim), F32),
        compiler_params=_cparams(("parallel", "parallel"), 8 * _nbytes((t, LANES), F32)),
    )(do, o)


def _pair_attn_bwd(q, kv, c_col, c_row, lse, delta, do, scale, name):
    s_dim, da = q.shape
    n_h = c_col.shape[0]
    dh = da // n_h
    hb = LANES // dh
    n_blocks = da // LANES
    t = _tile(s_dim, 512, LANES)
    nb = s_dim // t

    pairs = [(i, j) for j in range(nb) for i in range(j, nb)]
    i_tab = jnp.asarray([p[0] for p in pairs], jnp.int32)
    j_tab = jnp.asarray([p[1] for p in pairs], jnp.int32)

    def body(i_ref, j_ref, q_ref, k_ref, v_ref, cq_ref, ck_ref, lse_ref, dl_ref, do_ref,
             dq_ref, dcq_ref, dk_ref, dv_ref, dck_ref, dk_acc, dv_acc, dck_acc):
        i, j = i_ref[pl.program_id(1)], j_ref[pl.program_id(1)]

        @pl.when(pl.program_id(1) == 0)
        def _():
            dq_ref[...] = jnp.zeros_like(dq_ref)
            dcq_ref[...] = jnp.zeros_like(dcq_ref)

        @pl.when(i == j)
        def _():
            dk_acc[...] = jnp.zeros_like(dk_acc)
            dv_acc[...] = jnp.zeros_like(dv_acc)
            dck_acc[...] = jnp.zeros_like(dck_acc)

        def tile(masked):
            start = pl.multiple_of(i * t, t)
            qv, kv_ = q_ref[...], k_ref[...]
            dov = do_ref[...].astype(MXU_DTYPE)
            for hh in range(hb):
                lanes = _head_lanes(hh, dh)
                qm = jnp.where(lanes, qv, jnp.zeros_like(qv))
                km = jnp.where(lanes, kv_, jnp.zeros_like(kv_))
                dom = jnp.where(lanes, dov, jnp.zeros_like(dov))
                st = _dot(kv_, qm, "nt") + (cq_ref[hh] - ck_ref[hh])
                if masked:
                    keep = lax.broadcasted_iota(jnp.int32, (t, t), 0) <= lax.broadcasted_iota(jnp.int32, (t, t), 1)
                    st = jnp.where(keep, st, -jnp.inf)
                pt = jnp.exp(st - lse_ref[hh])
                dst = pt * (_dot(v_ref[...], dom, "nt") - dl_ref[hh])
                dv_acc[...] += _dot(pt, dom, "nn")
                dk_acc[...] += _dot(dst, qm, "nn")
                dq_ref[pl.ds(start, t), :] += _dot(dst, km, "tn") * scale
                dcq_ref[hh, :, pl.ds(start, t)] += jnp.sum(dst, axis=0, keepdims=True)
                dck_acc[hh] -= jnp.sum(dst, axis=1, keepdims=True)

        pl.when(i > j)(functools.partial(tile, False))
        pl.when(i == j)(functools.partial(tile, True))

        @pl.when(i == nb - 1)
        def _():
            dk_ref[...] = dk_acc[...]
            dv_ref[...] = dv_acc[...]
            dck_ref[...] = dck_acc[...]

    q_spec = pl.BlockSpec((t, LANES), lambda b, p, it, jt: (it[p], b))
    qrow_spec = pl.BlockSpec((hb, 1, t), lambda b, p, it, jt: (b, 0, it[p]))
    k_spec = pl.BlockSpec((t, LANES), lambda b, p, it, jt: (jt[p], b))
    v_spec = pl.BlockSpec((t, LANES), lambda b, p, it, jt: (jt[p], n_blocks + b))
    kcol_spec = pl.BlockSpec((hb, t, 1), lambda b, p, it, jt: (b, jt[p], 0))
    wide = jax.ShapeDtypeStruct((s_dim, da), F32)
    return pl.pallas_call(
        body, name=name,
        grid_spec=pltpu.PrefetchScalarGridSpec(
            num_scalar_prefetch=2, grid=(n_blocks, len(pairs)),
            in_specs=[q_spec, k_spec, v_spec, qrow_spec, kcol_spec, qrow_spec, qrow_spec, q_spec],
            out_specs=[pl.BlockSpec((s_dim, LANES), lambda b, p, it, jt: (0, b)),
                       pl.BlockSpec((hb, 1, s_dim), lambda b, p, it, jt: (b, 0, 0)), k_spec, k_spec, kcol_spec],
            scratch_shapes=[pltpu.VMEM((t, LANES), F32), pltpu.VMEM((t, LANES), F32), pltpu.VMEM((hb, t, 1), F32)]),
        out_shape=[wide, jax.ShapeDtypeStruct((n_h, 1, s_dim), F32), wide, wide,
                   jax.ShapeDtypeStruct((n_h, s_dim, 1), F32)],
        compiler_params=_cparams(("parallel", "arbitrary"),
                                 10 * hb * _nbytes((t, t), F32) + 4 * _nbytes((s_dim, LANES), F32)),
    )(i_tab, j_tab, q, kv, kv, c_row, c_col, lse, delta, do)


_HBM = pl.BlockSpec(memory_space=pltpu.HBM)
_MESH_ID = pl.DeviceIdType.MESH


def _all_gather(block, name):
    r, w = block.shape

    def body(x_ref, out_ref, send_sems, recv_sems, local_sem):
        x, y, c = lax.axis_index("x"), lax.axis_index("y"), lax.axis_index("c")
        me, sibling = (x, y, c), (x, y, 1 - c)
        chips = [(1 - x, y), (x, 1 - y), (1 - x, 1 - y)]

        def slot(px, py, pc):
            return out_ref.at[4 * px + 2 * py + pc]

        def copy(k, blk, to, src=None):
            return pltpu.make_async_remote_copy(
                src_ref=slot(*blk) if src is None else src, dst_ref=slot(*blk),
                send_sem=send_sems.at[k], recv_sem=recv_sems.at[k], device_id=to, device_id_type=_MESH_ID)

        mine = pltpu.make_async_copy(x_ref, slot(*me), local_sem)
        mine.start()
        first = [copy(0, me, sibling, src=x_ref)]
        first += [copy(1 + n, me, (*chip, c), src=x_ref) for n, chip in enumerate(chips)]
        for cp in first:
            cp.start()
        passed = [copy(4 + n, (*chip, c), sibling) for n, chip in enumerate(chips)]
        for n, chip in enumerate(chips):
            copy(1 + n, (*chip, c), me).wait_recv()
            passed[n].start()
        copy(0, sibling, me).wait_recv()
        for n, chip in enumerate(chips):
            copy(4 + n, (*chip, 1 - c), me).wait_recv()
        for cp in first + passed:
            cp.wait_send()
        mine.wait()

    return pl.pallas_call(
        body, name=name, out_shape=jax.ShapeDtypeStruct((N_DEV, r, w), block.dtype),
        in_specs=[_HBM], out_specs=_HBM,
        scratch_shapes=[pltpu.SemaphoreType.DMA((7,)), pltpu.SemaphoreType.DMA((7,)), pltpu.SemaphoreType.DMA],
    )(block)


_SEM = pl.BlockSpec(memory_space=pltpu.SEMAPHORE)
_EFFECT = pltpu.SideEffectType.DATAFLOW_SIDE_EFFECTING


def _exchange_start(srcs, personalized, after, name):
    n = len(srcs)
    n_after = len(after)
    lands = [lax.empty((N_DEV,) + s.shape[-2:], s.dtype) for s in srcs]

    def body(*refs):
        src_refs, land_refs = refs[:n], refs[n:2 * n]
        outs = refs[2 * n + n_after:]
        send_sems, recv_sems, token = outs[:n], outs[n:2 * n], outs[-1]
        x, y, c = lax.axis_index("x"), lax.axis_index("y"), lax.axis_index("c")
        mine = 4 * x + 2 * y + c
        for ci in range(n):
            for k in range(1, N_DEV):
                px = 1 - x if k & 4 else x
                py = 1 - y if k & 2 else y
                pc = 1 - c if k & 1 else c
                src = src_refs[ci].at[4 * px + 2 * py + pc] if personalized else src_refs[ci]
                pltpu.make_async_remote_copy(
                    src_ref=src, dst_ref=land_refs[ci].at[mine], send_sem=send_sems[ci], recv_sem=recv_sems[ci],
                    device_id=(px, py, pc), device_id_type=_MESH_ID).start()
        token[...] = jnp.zeros_like(token)

    sem = pltpu.SemaphoreType.DMA(())
    out_shape = ([sem] * (2 * n) + [pltpu.HBM(s.shape, s.dtype) for s in srcs]
                 + [pltpu.HBM(l.shape, l.dtype) for l in lands] + [jax.ShapeDtypeStruct((8, LANES), F32)])
    res = pl.pallas_call(
        body, name=name, out_shape=tuple(out_shape),
        in_specs=[_HBM] * (2 * n) + [_ANY] * n_after,
        out_specs=tuple([_SEM] * (2 * n) + [_HBM] * (2 * n) + [pl.BlockSpec(memory_space=pltpu.VMEM)]),
        input_output_aliases={i: 2 * n + i for i in range(2 * n)},
        compiler_params=pltpu.CompilerParams(has_side_effects=_EFFECT),
    )(*[pltpu.with_memory_space_constraint(s, pltpu.HBM) for s in srcs],
      *[pltpu.with_memory_space_constraint(l, pltpu.HBM) for l in lands], *after)
    handles = [(res[ci], res[n + ci], res[2 * n + ci], res[3 * n + ci]) for ci in range(n)]
    return handles, res[-1]


def _exchange_wait(handle, after, name):
    send_sem, recv_sem, src_thru, land_thru = handle

    def body(src_ref, land_ref, send_ref, recv_ref, after_ref, src_out, land_out):
        seven = land_ref.at[pl.ds(0, N_DEV - 1)]
        copies = pltpu.make_async_remote_copy(
            src_ref=seven, dst_ref=seven, send_sem=send_ref, recv_sem=recv_ref,
            device_id=(lax.axis_index("x"), lax.axis_index("y"), lax.axis_index("c")), device_id_type=_MESH_ID)
        copies.wait_send()
        copies.wait_recv()

    return pl.pallas_call(
        body, name=name,
        out_shape=(pltpu.HBM(src_thru.shape, src_thru.dtype), pltpu.HBM(land_thru.shape, land_thru.dtype)),
        in_specs=(_HBM, _HBM, _SEM, _SEM, _ANY), out_specs=(_HBM, _HBM), input_output_aliases={0: 0, 1: 1},
        compiler_params=pltpu.CompilerParams(has_side_effects=_EFFECT),
    )(src_thru, land_thru, send_sem, recv_sem, after)[1]


def _own_slot(land, own, me):
    return lax.dynamic_update_index_in_dim(land, own, me, axis=0)


def _sum_slots(slots, name):
    n, r, w = slots.shape
    tr = _tile(r, 128, WIRE_ROW_ALIGN)

    def body(s_ref, o_ref):
        acc = s_ref[0].astype(F32)
        for d in range(1, n):
            acc = acc + s_ref[d].astype(F32)
        o_ref[...] = acc

    return pl.pallas_call(
        body, name=name, grid=(r // tr,),
        in_specs=[pl.BlockSpec((n, tr, w), lambda i: (0, i, 0))],
        out_specs=pl.BlockSpec((tr, w), lambda i: (i, 0)),
        out_shape=jax.ShapeDtypeStruct((r, w), F32),
        compiler_params=_cparams(("parallel",), 2 * _nbytes((n, tr, w), slots.dtype) + 4 * _nbytes((tr, w), F32)),
    )(slots)


def _adamw(w, g, m, v, name):
    r, c = w.shape
    tr = _tile(r, 512, 8)

    def body(w_ref, g_ref, m_ref, v_ref, d_ref, mo_ref, vo_ref):
        gv = g_ref[...]
        m_new = ADAM_B1 * m_ref[...] + (1.0 - ADAM_B1) * gv
        v_new = ADAM_B2 * v_ref[...] + (1.0 - ADAM_B2) * (gv * gv)
        m_hat = m_new / (1.0 - ADAM_B1 ** ADAM_STEP)
        v_hat = v_new / (1.0 - ADAM_B2 ** ADAM_STEP)
        d_ref[...] = -ADAM_LR * (m_hat / (jnp.sqrt(v_hat) + ADAM_EPS) + ADAM_WD * w_ref[...])
        mo_ref[...] = m_new
        vo_ref[...] = v_new

    blk = pl.BlockSpec((tr, c), lambda i: (i, 0))
    shp = jax.ShapeDtypeStruct((r, c), F32)
    return pl.pallas_call(
        body, name=name, grid=(r // tr,), in_specs=[blk] * 4, out_specs=[blk] * 3, out_shape=[shp] * 3,
        compiler_params=_cparams(("parallel",), 16 * _nbytes((tr, _round_up(c, LANES)), F32)),
    )(w, g, m, v)


def _pack_rows(parts, width, dtype, row_align):
    rows, spans, off = [], [], 0
    for p in parts:
        flat = p.reshape(-1).astype(dtype)
        n_rows = _round_up(-(-flat.shape[0] // width), row_align)
        flat = jnp.pad(flat, (0, n_rows * width - flat.shape[0]))
        rows.append(flat.reshape(n_rows, width))
        spans.append((off, n_rows))
        off += n_rows
    return jnp.concatenate(rows, axis=0), spans


def _unpack_rows(mat, span, shape):
    off, n_rows = span
    n = 1
    for s in shape:
        n *= s
    return mat[..., off:off + n_rows, :].reshape(mat.shape[:-2] + (-1,))[..., :n].reshape(mat.shape[:-2] + tuple(shape))


def _block_diag(w, size):
    n, b, _ = w.shape
    eye = jnp.eye(n, dtype=w.dtype)
    dense = (w[:, :, None, :] * eye[:, None, :, None]).reshape(n * b, n * b)
    return jnp.pad(dense, ((0, size - n * b), (0, size - n * b)))


def _diag_blocks(dense, n, b):
    return jnp.stack([dense[k * b:(k + 1) * b, k * b:(k + 1) * b] for k in range(n)])


def _pad_rows(a, rows):
    return jnp.pad(a, ((0, rows - a.shape[0]), (0, 0)))


def _pad_cols(a, cols):
    return jnp.pad(a, ((0, 0), (0, cols - a.shape[1])))


def _train_step(a):
    x = a["x"][0]
    target = a["loss_target"][0]
    s_dim, d = x.shape
    n_layers = a["ffn1_pre_g"].shape[0]
    f_shard = a["ffn1_w_gate"].shape[2]
    c_shard = a["rg_conv_b"].shape[1]
    c_dim = c_shard * N_DEV
    cp = _round_up(c_dim, LANES)
    conv_width = a["rg_conv_w"].shape[1]
    n_blocks, lru_block = a["rg_w_a"].shape[1], a["rg_w_a"].shape[2]
    d_attn = a["attn_w_q"].shape[2]
    n_heads = a["b_fgate"].shape[0]
    d_head = d_attn // n_heads
    attn_scale = d_head ** -0.5
    assert conv_width < 8 and n_heads <= LANES and n_layers == 2
    assert d_attn == d
    me = 4 * lax.axis_index("x") + 2 * lax.axis_index("y") + lax.axis_index("c")

    shard = {"rg_w_in": a["rg_w_in"][0].T, "rg_w_out": a["rg_w_out"][0], "w_kv": a["w_kv"].T,
             "attn_w_q": a["attn_w_q"][0], "attn_w_o": a["attn_w_o"][0]}
    for l in range(n_layers):
        for f in ("ffn1", "ffn2"):
            shard[(f, "gate", l)] = a[f + "_w_gate"][l].T
            shard[(f, "up", l)] = a[f + "_w_up"][l].T
            shard[(f, "down", l)] = a[f + "_w_down"][l]

    def ffn_names(f, l):
        return [(f, "gate", l), (f, "up", l), (f, "down", l)]

    def chunk_layout(names):
        spans, off = [], 0
        for nm in names:
            spans.append((nm, off, shard[nm].shape[0]))
            off += _round_up(shard[nm].shape[0], WIRE_ROW_ALIGN)
        return spans, off

    def pack_chunk(names, parts):
        return jnp.concatenate(
            [_pad_rows(parts[nm].astype(WIRE_DTYPE), _round_up(parts[nm].shape[0], WIRE_ROW_ALIGN)) for nm in names], axis=0)

    full = {}

    def unpack_chunk(names, gathered):
        for nm, o, n_rows in chunk_layout(names)[0]:
            full[nm] = gathered[:, o:o + n_rows, :].reshape(N_DEV * n_rows, d)

    fwd_chunks = [ffn_names("ffn1", 0)[:2], ffn_names("ffn1", 0)[2:], ["rg_w_in", "rg_w_out"],
                  ffn_names("ffn2", 0) + ["w_kv"], ffn_names("ffn1", 1) + ["attn_w_q", "attn_w_o"], ffn_names("ffn2", 1)]
    fwd_packs = [pack_chunk(names, shard) for names in fwd_chunks]
    unpack_chunk(fwd_chunks[0], _all_gather(fwd_packs[0], "gather_weights_first"))

    small_parts = [a["rg_conv_w"][0], a["rg_conv_b"][0], a["rg_b_a"][0], a["rg_b_x"][0], a["rg_lambda"][0], a["w_fgate"]]
    small_pack, small_spans = _pack_rows(small_parts, d, F32, 8)
    small_all = _all_gather(small_pack, "gather_small")
    fwd_handles, fwd_token = _exchange_start(fwd_packs[1:], False, [full[("ffn1", "up", 0)], small_all],
                                             "gather_weights_start")

    def land_weights(n, after):
        land = _exchange_wait(fwd_handles[n - 1], after, f"gather_weights_wait_{n}")
        unpack_chunk(fwd_chunks[n], _own_slot(land, fwd_packs[n], me))

    sm = [_unpack_rows(small_all, sp, p.shape) for sp, p in zip(small_spans, small_parts)]
    conv_w = jnp.moveaxis(sm[0], 0, 1).reshape(conv_width, c_dim)
    conv_b, b_a, b_x, lam = (v.reshape(1, c_dim) for v in sm[1:5])
    w_f = sm[5].reshape(d, n_heads)

    pconv = _pad_rows(_pad_cols(jnp.concatenate([conv_w, conv_b], axis=0), cp), 8)
    pvec = _pad_rows(_pad_cols(jnp.concatenate([b_a, b_x, lam], axis=0), cp), 8)
    wa_dense = _block_diag(a["rg_w_a"][0], cp).astype(MXU_DTYPE)
    wx_dense = _block_diag(a["rg_w_x"][0], cp).astype(MXU_DTYPE)
    wax = jnp.concatenate([wa_dense, wx_dense], axis=1)
    w_f_t = _pad_rows(w_f.T.astype(MXU_DTYPE), LANES)
    b_f = _pad_rows(_pad_cols(a["b_fgate"].reshape(1, n_heads), LANES), 8)

    def gain(name, l):
        return a[name][l].reshape(1, d)

    def ffn_fwd(h, f, l, after=None, down_chunk=None):
        xn = _rms_fwd(h, gain(f + "_pre_g", l), f"{f}_{l}_pre_norm", after)
        g, u, act = _ffn_up(xn, full[(f, "gate", l)], full[(f, "up", l)], f"{f}_{l}_up")
        if down_chunk is not None:
            land_weights(down_chunk, act)
        fo, h_new = _mm_rms_res(act, full[(f, "down", l)], h, gain(f + "_post_g", l), 0.5, f"{f}_{l}_down")
        return h_new, (h, xn, g, u, act, fo)

    h0 = x
    h0a, sv_f1_0 = ffn_fwd(h0, "ffn1", 0, fwd_token, down_chunk=1)
    land_weights(2, h0a)
    w_in_gate = _pad_rows(full["rg_w_in"][:c_dim], cp)
    w_in_rec = _pad_rows(full["rg_w_in"][c_dim:], cp)
    w_in_t = jnp.concatenate([w_in_gate, w_in_rec], axis=0)
    w_out = _pad_rows(full["rg_w_out"], cp)
    hn_rg = _rms_fwd(h0a, gain("mix_pre_g", 0), "rg_pre_norm")
    gx = _mm([(hn_rg, w_in_t)], "nt", F32, "rg_in_proj")
    rec = _conv_fwd(gx, pconv, conv_width, "rg_conv")
    gates = _mm([(rec, wax)], "nn", F32, "rg_gate_proj")
    h_rec, y_rg = _scan_fwd(gx, rec, gates, pvec, "rg_scan")
    m_rg, h0b = _mm_rms_res(y_rg, w_out, h0a, gain("mix_post_g", 0), 1.0, "rg_out_proj")
    land_weights(3, h0b)
    h1, sv_f2_0 = ffn_fwd(h0b, "ffn2", 0)
    hn_kv = _rms_fwd(h1, a["kv_norm_g"].reshape(1, d), "kv_norm")
    kv = _mm([(hn_kv, full["w_kv"])], "nt", MXU_DTYPE, "kv_proj")
    fpre = _mm([(hn_kv, w_f_t)], "nt", F32, "fgate_proj")
    c_cum = _fgate_fwd(fpre, b_f, "fgate_cumsum")
    c_heads = c_cum[:, :n_heads].T
    c_col, c_row = c_heads[:, :, None], c_heads[:, None, :]
    land_weights(4, c_cum)
    h1a, sv_f1_1 = ffn_fwd(h1, "ffn1", 1)
    hn_at = _rms_fwd(h1a, gain("mix_pre_g", 1), "attn_pre_norm")
    q_s = _mm([(hn_at, full["attn_w_q"])], "nn", MXU_DTYPE, "q_proj", out_scale=attn_scale)
    o2, lse = _pair_attn_fwd(q_s, kv, kv[:, d_attn:].T, c_col, c_row, "attn_fwd")
    m_at, h1b = _mm_rms_res(o2, full["attn_w_o"], h1a, gain("mix_post_g", 1), 1.0, "attn_out_proj")
    land_weights(5, h1b)
    y, sv_f2_1 = ffn_fwd(h1b, "ffn2", 1)
    dy, loss_part = _loss_head(y, target, "loss_head")

    grads_big = {}
    grads_rep = {}

    bwd_chunks = [ffn_names("ffn2", 1), ["attn_w_q", "attn_w_o"] + ffn_names("ffn1", 1),
                  ["w_kv"] + ffn_names("ffn2", 0), ["rg_w_in", "rg_w_out"], ffn_names("ffn1", 0)]
    bwd_sends, bwd_handles = [], []

    def send_grads(after):
        n = len(bwd_sends)
        send = jnp.concatenate(
            [jnp.pad(grads_big[nm].reshape(N_DEV, n_rows, d), ((0, 0), (0, _round_up(n_rows, WIRE_ROW_ALIGN) - n_rows), (0, 0)))
             for nm, _, n_rows in chunk_layout(bwd_chunks[n])[0]], axis=1)
        handles, token = _exchange_start([send], True, [after], f"exchange_grads_start_{n}")
        bwd_sends.append(send)
        bwd_handles.append(handles[0])
        return token

    def ffn_bwd(dh_out, saved, f, l, after=None, send_now=False):
        h, xn, g, u, act, fo = saved
        df, d_post = _rms_bwd(fo, gain(f + "_post_g", l), [dh_out], None, 0.5, MXU_DTYPE, f"{f}_{l}_post_norm_bwd", after)
        dg, du = _ffn_act_bwd(df, full[(f, "down", l)], g, u, f"{f}_{l}_act_bwd")
        grads_big[(f, "down", l)] = _mm([(act, df)], "tn", WIRE_DTYPE, f"{f}_{l}_dw_down")
        grads_big[(f, "gate", l)] = _mm([(dg, xn)], "tn", WIRE_DTYPE, f"{f}_{l}_dw_gate")
        grads_big[(f, "up", l)] = _mm([(du, xn)], "tn", WIRE_DTYPE, f"{f}_{l}_dw_up")
        sent = send_grads(df) if send_now else None
        dxn = _mm([(dg, full[(f, "gate", l)]), (du, full[(f, "up", l)])], "nn", F32, f"{f}_{l}_dx", sent)
        dh_in, d_pre = _rms_bwd(h, gain(f + "_pre_g", l), [dxn], dh_out, 1.0, F32, f"{f}_{l}_pre_norm_bwd")
        grads_rep[(f + "_post_g", l)] = d_post
        grads_rep[(f + "_pre_g", l)] = d_pre
        return dh_in

    dh = ffn_bwd(dy, sv_f2_1, "ffn2", 1)
    token = send_grads(dh)
    dm, d_post = _rms_bwd(m_at, gain("mix_post_g", 1), [dh], None, 1.0, MXU_DTYPE, "attn_post_norm_bwd", token)
    grads_rep[("mix_post_g", 1)] = d_post
    do2 = _mm([(dm, full["attn_w_o"])], "nt", F32, "attn_out_proj_dx")
    grads_big["attn_w_o"] = _mm([(o2, dm)], "tn", WIRE_DTYPE, "attn_out_proj_dw")
    delta = _attn_delta(do2, o2, n_heads, "attn_delta")
    dq2, dc_q, dk2, dv2, dc_k = _pair_attn_bwd(q_s, kv, c_col, c_row, lse, delta, do2, attn_scale, "attn_bwd")
    dc_heads = dc_q[:, 0, :] + dc_k[:, :, 0]
    dhn = _mm([(dq2, full["attn_w_q"])], "nt", F32, "q_proj_dx")
    grads_big["attn_w_q"] = _mm([(hn_at, dq2)], "tn", WIRE_DTYPE, "q_proj_dw")
    dh, d_pre = _rms_bwd(h1a, gain("mix_pre_g", 1), [dhn], dh, 1.0, F32, "attn_pre_norm_bwd")
    grads_rep[("mix_pre_g", 1)] = d_pre
    dh = ffn_bwd(dh, sv_f1_1, "ffn1", 1)
    token = send_grads(dh)
    dc_cum = _pad_cols(dc_heads.T, LANES)
    dfpre, db_f = _fgate_bwd(dc_cum, fpre, b_f, "fgate_cumsum_bwd")
    dhn_kv = _mm([(dk2, full["w_kv"][:d_attn]), (dv2, full["w_kv"][d_attn:])], "nn", F32, "kv_proj_dx")
    dhn_f = _mm([(dfpre, w_f_t)], "nn", F32, "fgate_proj_dx")
    grads_big["w_kv"] = jnp.concatenate([_mm([(dk2, hn_kv)], "tn", WIRE_DTYPE, "kv_proj_dw_k"),
                                         _mm([(dv2, hn_kv)], "tn", WIRE_DTYPE, "kv_proj_dw_v")], axis=0)
    dw_f_t = _mm([(dfpre, hn_kv)], "tn", F32, "fgate_proj_dw")
    dh, d_kvg = _rms_bwd(h1, a["kv_norm_g"].reshape(1, d), [dhn_kv, dhn_f], dh, 1.0, F32, "kv_norm_bwd", token)
    dh = ffn_bwd(dh, sv_f2_0, "ffn2", 0)
    token = send_grads(dh)
    dm, d_post = _rms_bwd(m_rg, gain("mix_post_g", 0), [dh], None, 1.0, MXU_DTYPE, "rg_post_norm_bwd", token)
    grads_rep[("mix_post_g", 0)] = d_post
    dy_rg = _mm([(dm, w_out)], "nt", F32, "rg_out_proj_dx")
    dw_out = _mm([(y_rg, dm)], "tn", WIRE_DTYPE, "rg_out_proj_dw")
    dgate, dra, dia, drec1, dpvec = _scan_bwd(dy_rg, gx, h_rec, rec, gates, pvec, "rg_scan_bwd")
    drec2 = _mm([(dra, wa_dense), (dia, wx_dense)], "nt", F32, "rg_gate_proj_dx")
    dwa_dense = _mm([(rec, dra)], "tn", F32, "rg_gate_proj_dwa")
    dwx_dense = _mm([(rec, dia)], "tn", F32, "rg_gate_proj_dwx")
    drec0, dpconv = _conv_bwd(drec1, drec2, gx, pconv, conv_width, "rg_conv_bwd")
    dhn = _mm([(dgate, w_in_gate), (drec0, w_in_rec)], "nn", F32, "rg_in_proj_dx")
    dw_in_gate = _mm([(dgate, hn_rg)], "tn", WIRE_DTYPE, "rg_in_proj_dw_gate")
    dw_in_rec = _mm([(drec0, hn_rg)], "tn", WIRE_DTYPE, "rg_in_proj_dw_rec")
    dh, d_pre = _rms_bwd(h0a, gain("mix_pre_g", 0), [dhn], dh, 1.0, F32, "rg_pre_norm_bwd")
    grads_rep[("mix_pre_g", 0)] = d_pre
    grads_big["rg_w_in"] = jnp.concatenate([dw_in_gate[:c_dim], dw_in_rec[:c_dim]], axis=0)
    grads_big["rg_w_out"] = dw_out[:c_dim]
    token = send_grads(dh)
    grad_x = ffn_bwd(dh, sv_f1_0, "ffn1", 0, token, send_now=True)

    g_shard = {}

    def land_grads(n, after):
        land = _exchange_wait(bwd_handles[n], after, f"exchange_grads_wait_{n}")
        own = lax.dynamic_index_in_dim(bwd_sends[n], me, axis=0, keepdims=False)
        g_chunk = _sum_slots(_own_slot(land, own, me), f"sum_weight_grads_{n}")
        for nm, o, n_rows in chunk_layout(bwd_chunks[n])[0]:
            g_shard[nm] = g_chunk[o:o + n_rows]

    for n in range(len(bwd_chunks) - 1):
        land_grads(n, grad_x)

    def gain_grad(name):
        return jnp.concatenate([grads_rep[(name, l)] for l in range(n_layers)], axis=0)

    rep_names = ["ffn1_pre_g", "ffn1_post_g", "mix_pre_g", "mix_post_g", "ffn2_pre_g", "ffn2_post_g"]
    rep_parts = [gain_grad(nm) for nm in rep_names]
    rep_names += ["kv_norm_g", "b_fgate", "rg_w_a", "rg_w_x", "rg_conv_w", "rg_conv_b", "rg_b_a", "rg_b_x", "rg_lambda", "w_fgate"]
    rep_parts += [
        d_kvg, db_f[0, :n_heads],
        _diag_blocks(dwa_dense, n_blocks, lru_block), _diag_blocks(dwx_dense, n_blocks, lru_block),
        dpconv[:conv_width, :c_dim], dpconv[conv_width, :c_dim],
        dpvec[0, :c_dim], dpvec[1, :c_dim], dpvec[2, :c_dim],
        dw_f_t[:n_heads].T]
    rep_pack, rep_spans = _pack_rows(rep_parts, d, F32, WIRE_ROW_ALIGN)
    rep_sum = _sum_slots(_all_gather(rep_pack, "gather_small_grads"), "sum_small_grads")
    g_rep = {nm: _unpack_rows(rep_sum, sp, p.shape) for nm, sp, p in zip(rep_names, rep_spans, rep_parts)}

    def my_cols(full_grad, n):
        return lax.dynamic_slice_in_dim(full_grad, me * n, n, axis=full_grad.ndim - 1)

    def ffn_grads(f):
        grad[f + "_w_gate"] = jnp.stack([g_shard[(f, "gate", l)].T for l in range(n_layers)])
        grad[f + "_w_up"] = jnp.stack([g_shard[(f, "up", l)].T for l in range(n_layers)])
        grad[f + "_w_down"] = jnp.stack([g_shard[(f, "down", l)] for l in range(n_layers)])

    grad = {}
    for nm in ("ffn1_pre_g", "ffn1_post_g", "mix_pre_g", "mix_post_g", "ffn2_pre_g", "ffn2_post_g"):
        grad[nm] = g_rep[nm]
    ffn_grads("ffn2")
    grad["rg_w_in"] = g_shard["rg_w_in"].T[None]
    grad["rg_conv_w"] = my_cols(g_rep["rg_conv_w"], c_shard)[None]
    for nm in ("rg_conv_b", "rg_b_a", "rg_b_x", "rg_lambda"):
        grad[nm] = my_cols(g_rep[nm], c_shard)[None]
    grad["rg_w_a"] = g_rep["rg_w_a"][None]
    grad["rg_w_x"] = g_rep["rg_w_x"][None]
    grad["rg_w_out"] = g_shard["rg_w_out"][None]
    grad["kv_norm_g"] = g_rep["kv_norm_g"].reshape(d)
    grad["w_kv"] = g_shard["w_kv"].T
    grad["w_fgate"] = lax.dynamic_slice_in_dim(g_rep["w_fgate"], me * (d // N_DEV), d // N_DEV, axis=0)
    grad["b_fgate"] = g_rep["b_fgate"]
    grad["attn_w_q"] = g_shard["attn_w_q"][None]
    grad["attn_w_o"] = g_shard["attn_w_o"][None]

    delta, new_m, new_v = {}, {}, {}

    def adamw(nm):
        w = a[nm]
        shape = w.shape
        two_d = (1, shape[0]) if w.ndim == 1 else (-1, shape[-1])
        dl, mo, vo = _adamw(w.reshape(two_d), grad[nm].reshape(two_d), a["m_" + nm].reshape(two_d),
                            a["v_" + nm].reshape(two_d), "adamw_" + nm)
        delta[nm], new_m[nm], new_v[nm] = dl.reshape(shape), mo.reshape(shape), vo.reshape(shape)
        grad[nm] = grad[nm].reshape(shape)

    last_names = ("ffn1_w_gate", "ffn1_w_up", "ffn1_w_down")
    for nm in WEIGHT_NAMES:
        if nm not in last_names:
            adamw(nm)
    land_grads(len(bwd_chunks) - 1, delta["attn_w_o"])
    ffn_grads("ffn1")
    for nm in last_names:
        adamw(nm)

    loss = lax.psum(loss_part[0, 0], AXES)
    return (loss, grad_x[None], *[grad[n] for n in WEIGHT_NAMES], *[delta[n] for n in WEIGHT_NAMES],
            *[new_m[n] for n in WEIGHT_NAMES], *[new_v[n] for n in WEIGHT_NAMES])


def kernel(x, ffn1_pre_g, ffn1_w_gate, ffn1_w_up, ffn1_w_down, ffn1_post_g, mix_pre_g, mix_post_g, ffn2_pre_g, ffn2_w_gate, ffn2_w_up, ffn2_w_down, ffn2_post_g, rg_w_in, rg_conv_w, rg_conv_b, rg_w_a, rg_b_a, rg_w_x, rg_b_x, rg_lambda, rg_w_out, kv_norm_g, w_kv, w_fgate, b_fgate, attn_w_q, attn_w_o, loss_target, m_ffn1_pre_g, m_ffn1_w_gate, m_ffn1_w_up, m_ffn1_w_down, m_ffn1_post_g, m_mix_pre_g, m_mix_post_g, m_ffn2_pre_g, m_ffn2_w_gate, m_ffn2_w_up, m_ffn2_w_down, m_ffn2_post_g, m_rg_w_in, m_rg_conv_w, m_rg_conv_b, m_rg_w_a, m_rg_b_a, m_rg_w_x, m_rg_b_x, m_rg_lambda, m_rg_w_out, m_kv_norm_g, m_w_kv, m_w_fgate, m_b_fgate, m_attn_w_q, m_attn_w_o, v_ffn1_pre_g, v_ffn1_w_gate, v_ffn1_w_up, v_ffn1_w_down, v_ffn1_post_g, v_mix_pre_g, v_mix_post_g, v_ffn2_pre_g, v_ffn2_w_gate, v_ffn2_w_up, v_ffn2_w_down, v_ffn2_post_g, v_rg_w_in, v_rg_conv_w, v_rg_conv_b, v_rg_w_a, v_rg_b_a, v_rg_w_x, v_rg_b_x, v_rg_lambda, v_rg_w_out, v_kv_norm_g, v_w_kv, v_w_fgate, v_b_fgate, v_attn_w_q, v_attn_w_o):
    return _train_step(dict(locals()))
```

```python
import functools

import jax
import jax.numpy as jnp
from jax import lax
from jax.experimental import pallas as pl
from jax.experimental.pallas import tpu as pltpu

F32 = jnp.float32
MXU_DTYPE = jnp.bfloat16
WIRE_DTYPE = jnp.bfloat16
N_DEV = 8
AXES = ("x", "y", "c")
LANES = 128
WIRE_ROW_ALIGN = 16
VMEM_LIMIT_MIN = 32 * 2 ** 20
VMEM_LIMIT_MAX = 56 * 2 ** 20

RMS_EPS = 1e-6
LRU_C = 8.0
ADAM_LR, ADAM_B1, ADAM_B2, ADAM_EPS, ADAM_WD, ADAM_STEP = 0.001, 0.9, 0.999, 1e-08, 0.01, 10

WEIGHT_NAMES = (
    "ffn1_pre_g", "ffn1_w_gate", "ffn1_w_up", "ffn1_w_down", "ffn1_post_g", "mix_pre_g", "mix_post_g",
    "ffn2_pre_g", "ffn2_w_gate", "ffn2_w_up", "ffn2_w_down", "ffn2_post_g", "rg_w_in", "rg_conv_w",
    "rg_conv_b", "rg_w_a", "rg_b_a", "rg_w_x", "rg_b_x", "rg_lambda", "rg_w_out", "kv_norm_g", "w_kv",
    "w_fgate", "b_fgate", "attn_w_q", "attn_w_o")


def _round_up(n, m):
    return (n + m - 1) // m * m


def _tile(dim, target, align=LANES):
    if dim <= target:
        return dim
    best = None
    t = align
    while t <= target:
        if dim % t == 0:
            best = t
        t += align
    return dim if best is None else best


def _cparams(semantics, vmem_estimate):
    limit = min(VMEM_LIMIT_MAX, max(VMEM_LIMIT_MIN, 2 * int(vmem_estimate)))
    return pltpu.CompilerParams(dimension_semantics=semantics, vmem_limit_bytes=limit)


def _nbytes(shape, dtype):
    n = 1
    for s in shape:
        n *= s
    return n * jnp.dtype(dtype).itemsize


def _sigmoid(x):
    return jax.nn.sigmoid(x)


def _softplus(x):
    return jnp.maximum(x, 0.0) + jnp.log1p(jnp.exp(-jnp.abs(x)))


def _expm1(x):
    series = x * (1.0 + x * (0.5 + x * (1.0 / 6.0 + x * (1.0 / 24.0 + x * (1.0 / 120.0)))))
    return jnp.where(jnp.abs(x) < 0.25, series, jnp.exp(x) - 1.0)


_GELU_C = 0.7978845608028654
_GELU_A = 0.044715


def _gelu(x):
    return 0.5 * x * (1.0 + jnp.tanh(_GELU_C * (x + _GELU_A * x * x * x)))


def _gelu_grad(x):
    t = jnp.tanh(_GELU_C * (x + _GELU_A * x * x * x))
    return 0.5 * (1.0 + t) + 0.5 * x * (1.0 - t * t) * _GELU_C * (1.0 + 3.0 * _GELU_A * x * x)


_DOT_DIMS = {"nn": ((1,), (0,)), "nt": ((1,), (1,)), "tn": ((0,), (0,))}


def _dot(a, b, mode):
    return lax.dot_general(a.astype(MXU_DTYPE), b.astype(MXU_DTYPE), (_DOT_DIMS[mode], ((), ())),
                           preferred_element_type=F32)


def _mm(pairs, mode, out_dtype, name, after=None, out_scale=None):
    a0, b0 = pairs[0]
    if mode == "tn":
        k_dim, m_dim = a0.shape
        n_dim = b0.shape[1]
    else:
        m_dim, k_dim = a0.shape
        n_dim = b0.shape[0] if mode == "nt" else b0.shape[1]
    for a, b in pairs:
        assert a.shape == a0.shape and b.shape == b0.shape
    tm = _tile(m_dim, 1408 if mode == "tn" else 512)
    whole = 1408 if mode == "tn" else 2816
    tn = _tile(n_dim, whole)
    tk = _tile(k_dim, whole)
    nk = k_dim // tk
    n_pairs = len(pairs)

    if mode == "tn":
        a_spec = pl.BlockSpec((tk, tm), lambda i, j, k: (k, i))
    else:
        a_spec = pl.BlockSpec((tm, tk), lambda i, j, k: (i, k))
    if mode == "nt":
        b_spec = pl.BlockSpec((tn, tk), lambda i, j, k: (j, k))
    else:
        b_spec = pl.BlockSpec((tk, tn), lambda i, j, k: (k, j))

    order = [] if after is None else [after]

    def body(*refs):
        ins, o_ref, acc = refs[:2 * n_pairs], refs[-2], refs[-1]
        k = pl.program_id(2)

        @pl.when(k == 0)
        def _():
            acc[...] = jnp.zeros_like(acc)

        s = acc[...]
        for p in range(n_pairs):
            s = s + _dot(ins[2 * p][...], ins[2 * p + 1][...], mode)
        acc[...] = s

        @pl.when(k == nk - 1)
        def _():
            r = acc[...] if out_scale is None else acc[...] * out_scale
            o_ref[...] = r.astype(out_dtype)

    est = (2 * n_pairs * (_nbytes((tm, tk), a0.dtype) + _nbytes((tk, tn), b0.dtype))
           + 2 * _nbytes((tm, tn), out_dtype) + 2 * _nbytes((tm, tn), F32))
    flat = [t for ab in pairs for t in ab]
    return pl.pallas_call(
        body, name=name, grid=(m_dim // tm, n_dim // tn, nk),
        in_specs=[a_spec, b_spec] * n_pairs + [_ANY] * len(order),
        out_specs=pl.BlockSpec((tm, tn), lambda i, j, k: (i, j)),
        out_shape=jax.ShapeDtypeStruct((m_dim, n_dim), out_dtype),
        scratch_shapes=[pltpu.VMEM((tm, tn), F32)],
        compiler_params=_cparams(("parallel", "parallel", "arbitrary"), est),
    )(*flat, *order)


_ANY = pl.BlockSpec(memory_space=pl.ANY)


def _rms_fwd(x, gain, name, after=None):
    s_dim, d = x.shape
    tm = _tile(s_dim, 512, 8)

    def body(*refs):
        x_ref, g_ref, o_ref = refs[0], refs[1], refs[-1]
        v = x_ref[...]
        r = lax.rsqrt(jnp.mean(v * v, axis=-1, keepdims=True) + RMS_EPS)
        o_ref[...] = (v * r * g_ref[...]).astype(MXU_DTYPE)

    order = [] if after is None else [after]
    return pl.pallas_call(
        body, name=name, grid=(s_dim // tm,),
        in_specs=[pl.BlockSpec((tm, d), lambda i: (i, 0)), pl.BlockSpec((1, d), lambda i: (0, 0))] + [_ANY] * len(order),
        out_specs=pl.BlockSpec((tm, d), lambda i: (i, 0)),
        out_shape=jax.ShapeDtypeStruct((s_dim, d), MXU_DTYPE),
        compiler_params=_cparams(("parallel",), 6 * _nbytes((tm, d), F32)),
    )(x, gain, *order)


def _rms_bwd(x, gain, dys, res, scale, out_dtype, name, after=None):
    s_dim, d = x.shape
    tm = _tile(s_dim, 512, 8)
    n_dy = len(dys)
    has_res = res is not None
    order = [] if after is None else [after]

    def body(*refs):
        x_ref, g_ref = refs[0], refs[1]
        dy_refs = refs[2:2 + n_dy]
        res_ref = refs[2 + n_dy] if has_res else None
        dx_ref, dg_ref = refs[-2], refs[-1]

        @pl.when(pl.program_id(0) == 0)
        def _():
            dg_ref[...] = jnp.zeros_like(dg_ref)

        v = x_ref[...]
        r = lax.rsqrt(jnp.mean(v * v, axis=-1, keepdims=True) + RMS_EPS)
        xh = v * r
        dy = dy_refs[0][...].astype(F32)
        for extra in dy_refs[1:]:
            dy = dy + extra[...].astype(F32)
        gd = dy * g_ref[...]
        dx = scale * r * (gd - xh * jnp.mean(gd * xh, axis=-1, keepdims=True))
        if has_res:
            dx = dx + res_ref[...]
        dx_ref[...] = dx.astype(out_dtype)
        dg_ref[...] += scale * jnp.sum(dy * xh, axis=0, keepdims=True)

    row = pl.BlockSpec((tm, d), lambda i: (i, 0))
    vec = pl.BlockSpec((1, d), lambda i: (0, 0))
    ops = [x, gain] + list(dys) + ([res] if has_res else [])
    return pl.pallas_call(
        body, name=name, grid=(s_dim // tm,),
        in_specs=[row, vec] + [row] * (n_dy + int(has_res)) + [_ANY] * len(order),
        out_specs=[row, vec],
        out_shape=[jax.ShapeDtypeStruct((s_dim, d), out_dtype), jax.ShapeDtypeStruct((1, d), F32)],
        compiler_params=_cparams(("arbitrary",), (2 * len(ops) + 6) * _nbytes((tm, d), F32)),
    )(*ops, *order)


def _mm_rms_res(a, b, h, gain, scale, name):
    s_dim, k_dim = a.shape
    d = b.shape[1]
    tm = _tile(s_dim, 512, 8)
    tk = _tile(k_dim, 2816)
    nk = k_dim // tk

    def body(a_ref, b_ref, h_ref, g_ref, f_ref, o_ref, acc):
        k = pl.program_id(1)

        @pl.when(k == 0)
        def _():
            acc[...] = jnp.zeros_like(acc)

        acc[...] += _dot(a_ref[...], b_ref[...], "nn")

        @pl.when(k == nk - 1)
        def _():
            f = acc[...]
            r = lax.rsqrt(jnp.mean(f * f, axis=-1, keepdims=True) + RMS_EPS)
            f_ref[...] = f
            o_ref[...] = h_ref[...] + scale * (f * r * g_ref[...])

    row = pl.BlockSpec((tm, d), lambda i, k: (i, 0))
    est = (2 * (_nbytes((tm, tk), a.dtype) + _nbytes((tk, d), b.dtype)) + 8 * _nbytes((tm, d), F32))
    return pl.pallas_call(
        body, name=name, grid=(s_dim // tm, nk),
        in_specs=[pl.BlockSpec((tm, tk), lambda i, k: (i, k)), pl.BlockSpec((tk, d), lambda i, k: (k, 0)),
                  row, pl.BlockSpec((1, d), lambda i, k: (0, 0))],
        out_specs=[row, row],
        out_shape=[jax.ShapeDtypeStruct((s_dim, d), F32), jax.ShapeDtypeStruct((s_dim, d), F32)],
        scratch_shapes=[pltpu.VMEM((tm, d), F32)],
        compiler_params=_cparams(("parallel", "arbitrary"), est),
    )(a, b, h, gain)


def _ffn_up(xn, wg_t, wu_t, name):
    s_dim, d = xn.shape
    f_dim = wg_t.shape[0]
    tm = _tile(s_dim, 2048, 8)
    tf = _tile(f_dim, 256)

    def body(x_ref, wg_ref, wu_ref, g_ref, u_ref, a_ref):
        x = x_ref[...]
        g = _dot(x, wg_ref[...], "nt")
        u = _dot(x, wu_ref[...], "nt")
        g_ref[...] = g.astype(MXU_DTYPE)
        u_ref[...] = u.astype(MXU_DTYPE)
        a_ref[...] = (g * _sigmoid(g) * u).astype(MXU_DTYPE)

    w_spec = pl.BlockSpec((tf, d), lambda i, j: (j, 0))
    o_spec = pl.BlockSpec((tm, tf), lambda i, j: (i, j))
    o_shape = jax.ShapeDtypeStruct((s_dim, f_dim), MXU_DTYPE)
    est = 2 * _nbytes((tm, d), xn.dtype) + 4 * _nbytes((tf, d), wg_t.dtype) + 10 * _nbytes((tm, tf), F32)
    return pl.pallas_call(
        body, name=name, grid=(s_dim // tm, f_dim // tf),
        in_specs=[pl.BlockSpec((tm, d), lambda i, j: (i, 0)), w_spec, w_spec],
        out_specs=[o_spec, o_spec, o_spec], out_shape=[o_shape, o_shape, o_shape],
        compiler_params=_cparams(("parallel", "parallel"), est),
    )(xn, wg_t, wu_t)


def _ffn_act_bwd(df, wd, g, u, name):
    s_dim, d = df.shape
    f_dim = wd.shape[0]
    tm = _tile(s_dim, 2048, 8)
    tf = _tile(f_dim, 256)

    def body(df_ref, wd_ref, g_ref, u_ref, dg_ref, du_ref):
        dh = _dot(df_ref[...], wd_ref[...], "nt")
        gv = g_ref[...].astype(F32)
        uv = u_ref[...].astype(F32)
        sg = _sigmoid(gv)
        dg_ref[...] = (dh * uv * (sg * (1.0 + gv * (1.0 - sg)))).astype(MXU_DTYPE)
        du_ref[...] = (dh * gv * sg).astype(MXU_DTYPE)

    t_spec = pl.BlockSpec((tm, tf), lambda i, j: (i, j))
    o_shape = jax.ShapeDtypeStruct((s_dim, f_dim), MXU_DTYPE)
    est = 2 * _nbytes((tm, d), df.dtype) + 2 * _nbytes((tf, d), wd.dtype) + 12 * _nbytes((tm, tf), F32)
    return pl.pallas_call(
        body, name=name, grid=(s_dim // tm, f_dim // tf),
        in_specs=[pl.BlockSpec((tm, d), lambda i, j: (i, 0)), pl.BlockSpec((tf, d), lambda i, j: (j, 0)),
                  t_spec, t_spec],
        out_specs=[t_spec, t_spec], out_shape=[o_shape, o_shape],
        compiler_params=_cparams(("parallel", "parallel"), est),
    )(df, wd, g, u)


def _loss_head(y, target, name):
    s_dim, d = y.shape
    tm = _tile(s_dim, 512, 8)
    nt = s_dim // tm

    def body(y_ref, t_ref, dy_ref, loss_ref, acc):
        i = pl.program_id(0)

        @pl.when(i == 0)
        def _():
            acc[...] = jnp.zeros_like(acc)

        e = y_ref[...] - t_ref[...]
        dy_ref[...] = e * (1.0 / d)
        acc[...] += jnp.sum(e * e, axis=0, keepdims=True)

        @pl.when(i == nt - 1)
        def _():
            loss_ref[...] = jnp.sum(acc[...], axis=1, keepdims=True) * (0.5 / d)

    row = pl.BlockSpec((tm, d), lambda i: (i, 0))
    return pl.pallas_call(
        body, name=name, grid=(nt,), in_specs=[row, row],
        out_specs=[row, pl.BlockSpec((1, 1), lambda i: (0, 0))],
        out_shape=[jax.ShapeDtypeStruct((s_dim, d), F32), jax.ShapeDtypeStruct((1, 1), F32)],
        scratch_shapes=[pltpu.VMEM((1, d), F32)],
        compiler_params=_cparams(("arbitrary",), 8 * _nbytes((tm, d), F32)),
    )(y, target)


def _shift_down(v, sh, row):
    if sh == 0:
        return v
    return jnp.where(row >= sh, pltpu.roll(v, sh, 0), 0.0)


def _shift_up(v, sh, row):
    if sh == 0:
        return v
    n = v.shape[0]
    return jnp.where(row < n - sh, pltpu.roll(v, n - sh, 0), 0.0)


def _conv_fwd(gx, pconv, width, name):
    s_dim, cp2 = gx.shape
    cp = cp2 // 2
    nc = cp // LANES

    def body(x_ref, p_ref, o_ref):
        x = x_ref[...]
        row = lax.broadcasted_iota(jnp.int32, x.shape, 0)
        y = jnp.zeros_like(x) + p_ref[pl.ds(width, 1), :]
        for k in range(width):
            y = y + p_ref[pl.ds(k, 1), :] * _shift_down(x, width - 1 - k, row)
        o_ref[...] = y

    return pl.pallas_call(
        body, name=name, grid=(nc,),
        in_specs=[pl.BlockSpec((s_dim, LANES), lambda j: (0, nc + j)), pl.BlockSpec((8, LANES), lambda j: (0, j))],
        out_specs=pl.BlockSpec((s_dim, LANES), lambda j: (0, j)),
        out_shape=jax.ShapeDtypeStruct((s_dim, cp), F32),
        compiler_params=_cparams(("parallel",), 10 * _nbytes((s_dim, LANES), F32)),
    )(gx, pconv)


def _conv_bwd(d1, d2, gx, pconv, width, name):
    s_dim, cp = d1.shape
    nc = cp // LANES

    def body(d1_ref, d2_ref, x_ref, p_ref, dx_ref, dp_ref):
        d = d1_ref[...] + d2_ref[...]
        x = x_ref[...]
        row = lax.broadcasted_iota(jnp.int32, x.shape, 0)
        dx = jnp.zeros_like(d)
        dp_ref[...] = jnp.zeros_like(dp_ref)
        for k in range(width):
            sh = width - 1 - k
            dx = dx + p_ref[pl.ds(k, 1), :] * _shift_up(d, sh, row)
            dp_ref[pl.ds(k, 1), :] = jnp.sum(d * _shift_down(x, sh, row), axis=0, keepdims=True)
        dp_ref[pl.ds(width, 1), :] = jnp.sum(d, axis=0, keepdims=True)
        dx_ref[...] = dx.astype(MXU_DTYPE)

    strip = pl.BlockSpec((s_dim, LANES), lambda j: (0, j))
    par = pl.BlockSpec((8, LANES), lambda j: (0, j))
    return pl.pallas_call(
        body, name=name, grid=(nc,),
        in_specs=[strip, strip, pl.BlockSpec((s_dim, LANES), lambda j: (0, nc + j)), par],
        out_specs=[strip, par],
        out_shape=[jax.ShapeDtypeStruct((s_dim, cp), MXU_DTYPE), jax.ShapeDtypeStruct((8, cp), F32)],
        compiler_params=_cparams(("parallel",), 14 * _nbytes((s_dim, LANES), F32)),
    )(d1, d2, gx, pconv)


def _lru_coeffs(ra, ia, p_ref):
    r = _sigmoid(ra + p_ref[pl.ds(0, 1), :])
    i = _sigmoid(ia + p_ref[pl.ds(1, 1), :])
    sp = _softplus(-p_ref[pl.ds(2, 1), :])
    log_a = -LRU_C * r * sp
    a = jnp.exp(log_a)
    mult = jnp.sqrt(-_expm1(2.0 * log_a))
    return r, i, sp, a, mult


def _scan_fwd(gx, rec, gates, pvec, name):
    s_dim, cp = rec.shape
    ts = _tile(s_dim, 256, 8)
    nt = s_dim // ts

    def body(gate_ref, rec_ref, ra_ref, ia_ref, p_ref, h_ref, y_ref, a_s, u_s, carry):
        @pl.when(pl.program_id(0) == 0)
        def _():
            carry[...] = jnp.zeros_like(carry)

        rec_v = rec_ref[...]
        _, i, _, a, mult = _lru_coeffs(ra_ref[...], ia_ref[...], p_ref)
        a_s[...] = a
        u_s[...] = mult * (i * rec_v)

        def step(t, h):
            h = a_s[pl.ds(t, 1), :] * h + u_s[pl.ds(t, 1), :]
            h_ref[pl.ds(t, 1), :] = h
            return h

        carry[pl.ds(0, 1), :] = lax.fori_loop(0, ts, step, carry[pl.ds(0, 1), :], unroll=8)
        y_ref[...] = (_gelu(gate_ref[...]) * h_ref[...]).astype(MXU_DTYPE)

    blk = pl.BlockSpec((ts, cp), lambda t: (t, 0))
    return pl.pallas_call(
        body, name=name, grid=(nt,),
        in_specs=[blk, blk, blk, pl.BlockSpec((ts, cp), lambda t: (t, 1)), pl.BlockSpec((8, cp), lambda t: (0, 0))],
        out_specs=[blk, blk],
        out_shape=[jax.ShapeDtypeStruct((s_dim, cp), F32), jax.ShapeDtypeStruct((s_dim, cp), MXU_DTYPE)],
        scratch_shapes=[pltpu.VMEM((ts, cp), F32), pltpu.VMEM((ts, cp), F32), pltpu.VMEM((8, cp), F32)],
        compiler_params=_cparams(("arbitrary",), 14 * _nbytes((ts, cp), F32)),
    )(gx, rec, gates, gates, pvec)


def _scan_bwd(dy, gx, hrec, rec, gates, pvec, name):
    s_dim, cp = rec.shape
    ts = _tile(s_dim, 128, 8)
    nt = s_dim // ts

    def body(dy_ref, gate_ref, h_ref, hp_ref, rec_ref, ra_ref, ia_ref, p_ref,
             dgate_ref, dra_ref, dia_ref, drec_ref, dp_ref, a_s, d_s, carry):
        t_id = pl.program_id(0)

        @pl.when(t_id == 0)
        def _():
            carry[...] = jnp.zeros_like(carry)
            dp_ref[...] = jnp.zeros_like(dp_ref)

        rec_v = rec_ref[...]
        r, i, sp, a, mult = _lru_coeffs(ra_ref[...], ia_ref[...], p_ref)
        gate = gate_ref[...]
        dyv = dy_ref[...]
        h = h_ref[...]
        dgate_ref[...] = (dyv * h * _gelu_grad(gate)).astype(MXU_DTYPE)
        a_s[...] = a
        d_s[...] = dyv * _gelu(gate)

        def step(k, c):
            t = ts - 1 - k
            d = d_s[pl.ds(t, 1), :] + c
            d_s[pl.ds(t, 1), :] = d
            return a_s[pl.ds(t, 1), :] * d

        carry[pl.ds(0, 1), :] = lax.fori_loop(0, ts, step, carry[pl.ds(0, 1), :], unroll=8)
        dh = d_s[...]
        row = lax.broadcasted_iota(jnp.int32, h.shape, 0)
        first = jnp.where(t_id == nt - 1, 0.0, 1.0) * hp_ref[pl.ds(7, 1), :]
        h_prev = jnp.where(row == 0, first, pltpu.roll(h, 1, 0))
        dix = dh * mult
        dla = dh * h_prev * a - dh * (i * rec_v) * (a * a) / mult
        dra = dla * (-LRU_C * sp) * r * (1.0 - r)
        dia = dix * rec_v * i * (1.0 - i)
        dra_ref[...] = dra.astype(MXU_DTYPE)
        dia_ref[...] = dia.astype(MXU_DTYPE)
        drec_ref[...] = dix * i
        dsp = jnp.sum(dla * (-LRU_C * r), axis=0, keepdims=True)
        dp_ref[pl.ds(0, 1), :] += jnp.sum(dra, axis=0, keepdims=True)
        dp_ref[pl.ds(1, 1), :] += jnp.sum(dia, axis=0, keepdims=True)
        dp_ref[pl.ds(2, 1), :] += dsp * (-_sigmoid(-p_ref[pl.ds(2, 1), :]))

    blk = pl.BlockSpec((ts, cp), lambda t: (nt - 1 - t, 0))
    prev = pl.BlockSpec((8, cp), lambda t: (jnp.maximum((nt - 1 - t) * (ts // 8) - 1, 0), 0))
    par = pl.BlockSpec((8, cp), lambda t: (0, 0))
    lo = jax.ShapeDtypeStruct((s_dim, cp), MXU_DTYPE)
    return pl.pallas_call(
        body, name=name, grid=(nt,),
        in_specs=[blk, blk, blk, prev, blk, blk, pl.BlockSpec((ts, cp), lambda t: (nt - 1 - t, 1)), par],
        out_specs=[blk, blk, blk, blk, par],
        out_shape=[lo, lo, lo, jax.ShapeDtypeStruct((s_dim, cp), F32), jax.ShapeDtypeStruct((8, cp), F32)],
        scratch_shapes=[pltpu.VMEM((ts, cp), F32), pltpu.VMEM((ts, cp), F32), pltpu.VMEM((8, cp), F32)],
        compiler_params=_cparams(("arbitrary",), 40 * _nbytes((ts, cp), F32)),
    )(dy, gx, hrec, hrec, rec, gates, gates, pvec)


def _fgate_fwd(fpre, bias, name):
    s_dim, w = fpre.shape
    ts = _tile(s_dim, 512, 8)

    def body(f_ref, b_ref, c_ref, lf_s, carry):
        @pl.when(pl.program_id(0) == 0)
        def _():
            carry[...] = jnp.zeros_like(carry)

        lf_s[...] = -_softplus(-(f_ref[...] + b_ref[pl.ds(0, 1), :]))

        def step(t, c):
            c = c + lf_s[pl.ds(t, 1), :]
            c_ref[pl.ds(t, 1), :] = c
            return c

        carry[pl.ds(0, 1), :] = lax.fori_loop(0, ts, step, carry[pl.ds(0, 1), :], unroll=8)

    blk = pl.BlockSpec((ts, w), lambda t: (t, 0))
    return pl.pallas_call(
        body, name=name, grid=(s_dim // ts,),
        in_specs=[blk, pl.BlockSpec((8, w), lambda t: (0, 0))], out_specs=blk,
        out_shape=jax.ShapeDtypeStruct((s_dim, w), F32),
        scratch_shapes=[pltpu.VMEM((ts, w), F32), pltpu.VMEM((8, w), F32)],
        compiler_params=_cparams(("arbitrary",), 12 * _nbytes((ts, w), F32)),
    )(fpre, bias)


def _fgate_bwd(dc, fpre, bias, name):
    s_dim, w = fpre.shape
    ts = _tile(s_dim, 512, 8)
    nt = s_dim // ts

    def body(dc_ref, f_ref, b_ref, df_ref, db_ref, d_s, carry):
        @pl.when(pl.program_id(0) == 0)
        def _():
            carry[...] = jnp.zeros_like(carry)
            db_ref[...] = jnp.zeros_like(db_ref)

        d_s[...] = dc_ref[...]

        def step(k, c):
            t = ts - 1 - k
            c = c + d_s[pl.ds(t, 1), :]
            d_s[pl.ds(t, 1), :] = c
            return c

        carry[pl.ds(0, 1), :] = lax.fori_loop(0, ts, step, carry[pl.ds(0, 1), :], unroll=8)
        df = d_s[...] * _sigmoid(-(f_ref[...] + b_ref[pl.ds(0, 1), :]))
        df_ref[...] = df
        db_ref[pl.ds(0, 1), :] += jnp.sum(df, axis=0, keepdims=True)

    blk = pl.BlockSpec((ts, w), lambda t: (nt - 1 - t, 0))
    par = pl.BlockSpec((8, w), lambda t: (0, 0))
    return pl.pallas_call(
        body, name=name, grid=(nt,), in_specs=[blk, blk, par], out_specs=[blk, par],
        out_shape=[jax.ShapeDtypeStruct((s_dim, w), F32), jax.ShapeDtypeStruct((8, w), F32)],
        scratch_shapes=[pltpu.VMEM((ts, w), F32), pltpu.VMEM((8, w), F32)],
        compiler_params=_cparams(("arbitrary",), 12 * _nbytes((ts, w), F32)),
    )(dc, fpre, bias)


def _head_lanes(hh, dh):
    lane = lax.broadcasted_iota(jnp.int32, (1, LANES), 1)
    return (lane >= hh * dh) & (lane < (hh + 1) * dh)


def _pair_attn_fwd(q, kv, v_t, c_col, c_row, name):
    s_dim, da = q.shape
    n_h = c_col.shape[0]
    dh = da // n_h
    assert LANES % dh == 0 and da % LANES == 0
    hb = LANES // dh
    n_blocks = da // LANES
    t = _tile(s_dim, 512, LANES)
    nb = s_dim // t

    pairs = [(i, j) for i in range(nb) for j in range(i + 1)]
    i_tab = jnp.asarray([p[0] for p in pairs], jnp.int32)
    j_tab = jnp.asarray([p[1] for p in pairs], jnp.int32)

    def body(i_ref, j_ref, q_ref, k_ref, vt_ref, cq_ref, ck_ref, o_ref, lse_ref, m_s, l_s, acc):
        i, j = i_ref[pl.program_id(1)], j_ref[pl.program_id(1)]

        @pl.when(j == 0)
        def _():
            m_s[...] = jnp.full_like(m_s, -jnp.inf)
            l_s[...] = jnp.zeros_like(l_s)
            acc[...] = jnp.zeros_like(acc)

        def tile(masked):
            qv = q_ref[...]
            for hh in range(hb):
                st = _dot(k_ref[...], jnp.where(_head_lanes(hh, dh), qv, jnp.zeros_like(qv)), "nt")
                st = st + (cq_ref[hh] - ck_ref[hh])
                if masked:
                    keep = lax.broadcasted_iota(jnp.int32, (t, t), 0) <= lax.broadcasted_iota(jnp.int32, (t, t), 1)
                    st = jnp.where(keep, st, -jnp.inf)
                m_prev = m_s[hh]
                m_new = jnp.maximum(m_prev, jnp.max(st, axis=0, keepdims=True))
                alpha = jnp.exp(m_prev - m_new)
                p = jnp.exp(st - m_new)
                l_s[hh] = alpha * l_s[hh] + jnp.sum(p, axis=0, keepdims=True)
                acc[hh] = alpha * acc[hh] + _dot(vt_ref[...], p, "nn")
                m_s[hh] = m_new

        pl.when(j < i)(functools.partial(tile, False))
        pl.when(j == i)(functools.partial(tile, True))

        @pl.when(j == i)
        def _():
            feat = lax.broadcasted_iota(jnp.int32, (LANES, 1), 0)
            out_t = jnp.zeros((LANES, t), F32)
            for hh in range(hb):
                out_t = jnp.where((feat >= hh * dh) & (feat < (hh + 1) * dh), acc[hh] / l_s[hh], out_t)
                lse_ref[hh] = m_s[hh] + jnp.log(l_s[hh])
            o_ref[...] = out_t.T

    q_spec = pl.BlockSpec((t, LANES), lambda b, p, it, jt: (it[p], b))
    k_spec = pl.BlockSpec((t, LANES), lambda b, p, it, jt: (jt[p], b))
    vt_spec = pl.BlockSpec((LANES, t), lambda b, p, it, jt: (b, jt[p]))
    cq_spec = pl.BlockSpec((hb, 1, t), lambda b, p, it, jt: (b, 0, it[p]))
    ck_spec = pl.BlockSpec((hb, t, 1), lambda b, p, it, jt: (b, jt[p], 0))
    return pl.pallas_call(
        body, name=name,
        grid_spec=pltpu.PrefetchScalarGridSpec(
            num_scalar_prefetch=2, grid=(n_blocks, len(pairs)),
            in_specs=[q_spec, k_spec, vt_spec, cq_spec, ck_spec], out_specs=[q_spec, cq_spec],
            scratch_shapes=[pltpu.VMEM((hb, 1, t), F32), pltpu.VMEM((hb, 1, t), F32), pltpu.VMEM((hb, LANES, t), F32)]),
        out_shape=[jax.ShapeDtypeStruct((s_dim, da), F32), jax.ShapeDtypeStruct((n_h, 1, s_dim), F32)],
        compiler_params=_cparams(("parallel", "arbitrary"), 10 * hb * _nbytes((t, t), F32)),
    )(i_tab, j_tab, q, kv, v_t, c_row, c_col)


def _attn_delta(do, o, n_h, name):
    s_dim, da = o.shape
    dh = da // n_h
    hb = LANES // dh
    t = _tile(s_dim, 512, LANES)

    def body(do_ref, o_ref, d_ref):
        prod_t = (do_ref[...].astype(MXU_DTYPE).astype(F32) * o_ref[...]).T
        for hh in range(hb):
            d_ref[hh] = jnp.sum(prod_t[hh * dh:(hh + 1) * dh], axis=0, keepdims=True)

    blk = pl.BlockSpec((t, LANES), lambda b, i: (i, b))
    return pl.pallas_call(
        body, name=name, grid=(da // LANES, s_dim // t), in_specs=[blk, blk],
        out_specs=pl.BlockSpec((hb, 1, t), lambda b, i: (b, 0, i)),
        out_shape=jax.ShapeDtypeStruct((n_h, 1, s_dim), F32),
        compiler_params=_cparams(("parallel", "parallel"), 8 * _nbytes((t, LANES), F32)),
    )(do, o)


def _pair_attn_bwd(q, kv, c_col, c_row, lse, delta, do, scale, name):
    s_dim, da = q.shape
    n_h = c_col.shape[0]
    dh = da // n_h
    hb = LANES // dh
    n_blocks = da // LANES
    t = _tile(s_dim, 512, LANES)
    nb = s_dim // t

    pairs = [(i, j) for j in range(nb) for i in range(j, nb)]
    i_tab = jnp.asarray([p[0] for p in pairs], jnp.int32)
    j_tab = jnp.asarray([p[1] for p in pairs], jnp.int32)

    def body(i_ref, j_ref, q_ref, k_ref, v_ref, cq_ref, ck_ref, lse_ref, dl_ref, do_ref,
             dq_ref, dcq_ref, dk_ref, dv_ref, dck_ref, dk_acc, dv_acc, dck_acc):
        i, j = i_ref[pl.program_id(1)], j_ref[pl.program_id(1)]

        @pl.when(pl.program_id(1) == 0)
        def _():
            dq_ref[...] = jnp.zeros_like(dq_ref)
            dcq_ref[...] = jnp.zeros_like(dcq_ref)

        @pl.when(i == j)
        def _():
            dk_acc[...] = jnp.zeros_like(dk_acc)
            dv_acc[...] = jnp.zeros_like(dv_acc)
            dck_acc[...] = jnp.zeros_like(dck_acc)

        def tile(masked):
            start = pl.multiple_of(i * t, t)
            qv, kv_ = q_ref[...], k_ref[...]
            dov = do_ref[...].astype(MXU_DTYPE)
            for hh in range(hb):
                lanes = _head_lanes(hh, dh)
                qm = jnp.where(lanes, qv, jnp.zeros_like(qv))
                km = jnp.where(lanes, kv_, jnp.zeros_like(kv_))
                dom = jnp.where(lanes, dov, jnp.zeros_like(dov))
                st = _dot(kv_, qm, "nt") + (cq_ref[hh] - ck_ref[hh])
                if masked:
                    keep = lax.broadcasted_iota(jnp.int32, (t, t), 0) <= lax.broadcasted_iota(jnp.int32, (t, t), 1)
                    st = jnp.where(keep, st, -jnp.inf)
                pt = jnp.exp(st - lse_ref[hh])
                dst = pt * (_dot(v_ref[...], dom, "nt") - dl_ref[hh])
                dv_acc[...] += _dot(pt, dom, "nn")
                dk_acc[...] += _dot(dst, qm, "nn")
                dq_ref[pl.ds(start, t), :] += _dot(dst, km, "tn") * scale
                dcq_ref[hh, :, pl.ds(start, t)] += jnp.sum(dst, axis=0, keepdims=True)
                dck_acc[hh] -= jnp.sum(dst, axis=1, keepdims=True)

        pl.when(i > j)(functools.partial(tile, False))
        pl.when(i == j)(functools.partial(tile, True))

        @pl.when(i == nb - 1)
        def _():
            dk_ref[...] = dk_acc[...]
            dv_ref[...] = dv_acc[...]
            dck_ref[...] = dck_acc[...]

    q_spec = pl.BlockSpec((t, LANES), lambda b, p, it, jt: (it[p], b))
    qrow_spec = pl.BlockSpec((hb, 1, t), lambda b, p, it, jt: (b, 0, it[p]))
    k_spec = pl.BlockSpec((t, LANES), lambda b, p, it, jt: (jt[p], b))
    v_spec = pl.BlockSpec((t, LANES), lambda b, p, it, jt: (jt[p], n_blocks + b))
    kcol_spec = pl.BlockSpec((hb, t, 1), lambda b, p, it, jt: (b, jt[p], 0))
    wide = jax.ShapeDtypeStruct((s_dim, da), F32)
    return pl.pallas_call(
        body, name=name,
        grid_spec=pltpu.PrefetchScalarGridSpec(
            num_scalar_prefetch=2, grid=(n_blocks, len(pairs)),
            in_specs=[q_spec, k_spec, v_spec, qrow_spec, kcol_spec, qrow_spec, qrow_spec, q_spec],
            out_specs=[pl.BlockSpec((s_dim, LANES), lambda b, p, it, jt: (0, b)),
                       pl.BlockSpec((hb, 1, s_dim), lambda b, p, it, jt: (b, 0, 0)), k_spec, k_spec, kcol_spec],
            scratch_shapes=[pltpu.VMEM((t, LANES), F32), pltpu.VMEM((t, LANES), F32), pltpu.VMEM((hb, t, 1), F32)]),
        out_shape=[wide, jax.ShapeDtypeStruct((n_h, 1, s_dim), F32), wide, wide,
                   jax.ShapeDtypeStruct((n_h, s_dim, 1), F32)],
        compiler_params=_cparams(("parallel", "arbitrary"),
                                 10 * hb * _nbytes((t, t), F32) + 4 * _nbytes((s_dim, LANES), F32)),
    )(i_tab, j_tab, q, kv, kv, c_row, c_col, lse, delta, do)


_HBM = pl.BlockSpec(memory_space=pltpu.HBM)
_MESH_ID = pl.DeviceIdType.MESH


def _all_gather(block, name):
    r, w = block.shape

    def body(x_ref, out_ref, send_sems, recv_sems, local_sem):
        x, y, c = lax.axis_index("x"), lax.axis_index("y"), lax.axis_index("c")
        me, sibling = (x, y, c), (x, y, 1 - c)
        chips = [(1 - x, y), (x, 1 - y), (1 - x, 1 - y)]

        def slot(px, py, pc):
            return out_ref.at[4 * px + 2 * py + pc]

        def copy(k, blk, to, src=None):
            return pltpu.make_async_remote_copy(
                src_ref=slot(*blk) if src is None else src, dst_ref=slot(*blk),
                send_sem=send_sems.at[k], recv_sem=recv_sems.at[k], device_id=to, device_id_type=_MESH_ID)

        mine = pltpu.make_async_copy(x_ref, slot(*me), local_sem)
        mine.start()
        first = [copy(0, me, sibling, src=x_ref)]
        first += [copy(1 + n, me, (*chip, c), src=x_ref) for n, chip in enumerate(chips)]
        for cp in first:
            cp.start()
        passed = [copy(4 + n, (*chip, c), sibling) for n, chip in enumerate(chips)]
        for n, chip in enumerate(chips):
            copy(1 + n, (*chip, c), me).wait_recv()
            passed[n].start()
        copy(0, sibling, me).wait_recv()
        for n, chip in enumerate(chips):
            copy(4 + n, (*chip, 1 - c), me).wait_recv()
        for cp in first + passed:
            cp.wait_send()
        mine.wait()

    return pl.pallas_call(
        body, name=name, out_shape=jax.ShapeDtypeStruct((N_DEV, r, w), block.dtype),
        in_specs=[_HBM], out_specs=_HBM,
        scratch_shapes=[pltpu.SemaphoreType.DMA((7,)), pltpu.SemaphoreType.DMA((7,)), pltpu.SemaphoreType.DMA],
    )(block)


_SEM = pl.BlockSpec(memory_space=pltpu.SEMAPHORE)
_EFFECT = pltpu.SideEffectType.DATAFLOW_SIDE_EFFECTING


def _exchange_start(srcs, personalized, after, name):
    n = len(srcs)
    n_after = len(after)
    lands = [lax.empty((N_DEV,) + s.shape[-2:], s.dtype) for s in srcs]

    def body(*refs):
        src_refs, land_refs = refs[:n], refs[n:2 * n]
        outs = refs[2 * n + n_after:]
        send_sems, recv_sems, token = outs[:n], outs[n:2 * n], outs[-1]
        x, y, c = lax.axis_index("x"), lax.axis_index("y"), lax.axis_index("c")
        mine = 4 * x + 2 * y + c
        for ci in range(n):
            for k in range(1, N_DEV):
                px = 1 - x if k & 4 else x
                py = 1 - y if k & 2 else y
                pc = 1 - c if k & 1 else c
                src = src_refs[ci].at[4 * px + 2 * py + pc] if personalized else src_refs[ci]
                pltpu.make_async_remote_copy(
                    src_ref=src, dst_ref=land_refs[ci].at[mine], send_sem=send_sems[ci], recv_sem=recv_sems[ci],
                    device_id=(px, py, pc), device_id_type=_MESH_ID).start()
        token[...] = jnp.zeros_like(token)

    sem = pltpu.SemaphoreType.DMA(())
    out_shape = ([sem] * (2 * n) + [pltpu.HBM(s.shape, s.dtype) for s in srcs]
                 + [pltpu.HBM(l.shape, l.dtype) for l in lands] + [jax.ShapeDtypeStruct((8, LANES), F32)])
    res = pl.pallas_call(
        body, name=name, out_shape=tuple(out_shape),
        in_specs=[_HBM] * (2 * n) + [_ANY] * n_after,
        out_specs=tuple([_SEM] * (2 * n) + [_HBM] * (2 * n) + [pl.BlockSpec(memory_space=pltpu.VMEM)]),
        input_output_aliases={i: 2 * n + i for i in range(2 * n)},
        compiler_params=pltpu.CompilerParams(has_side_effects=_EFFECT),
    )(*[pltpu.with_memory_space_constraint(s, pltpu.HBM) for s in srcs],
      *[pltpu.with_memory_space_constraint(l, pltpu.HBM) for l in lands], *after)
    handles = [(res[ci], res[n + ci], res[2 * n + ci], res[3 * n + ci]) for ci in range(n)]
    return handles, res[-1]


def _exchange_wait(handle, after, name):
    send_sem, recv_sem, src_thru, land_thru = handle

    def body(src_ref, land_ref, send_ref, recv_ref, after_ref, src_out, land_out):
        seven = land_ref.at[pl.ds(0, N_DEV - 1)]
        copies = pltpu.make_async_remote_copy(
            src_ref=seven, dst_ref=seven, send_sem=send_ref, recv_sem=recv_ref,
            device_id=(lax.axis_index("x"), lax.axis_index("y"), lax.axis_index("c")), device_id_type=_MESH_ID)
        copies.wait_send()
        copies.wait_recv()

    return pl.pallas_call(
        body, name=name,
        out_shape=(pltpu.HBM(src_thru.shape, src_thru.dtype), pltpu.HBM(land_thru.shape, land_thru.dtype)),
        in_specs=(_HBM, _HBM, _SEM, _SEM, _ANY), out_specs=(_HBM, _HBM), input_output_aliases={0: 0, 1: 1},
        compiler_params=pltpu.CompilerParams(has_side_effects=_EFFECT),
    )(src_thru, land_thru, send_sem, recv_sem, after)[1]


def _own_slot(land, own, me):
    return lax.dynamic_update_index_in_dim(land, own, me, axis=0)


def _sum_slots(slots, name):
    n, r, w = slots.shape
    tr = _tile(r, 128, WIRE_ROW_ALIGN)

    def body(s_ref, o_ref):
        acc = s_ref[0].astype(F32)
        for d in range(1, n):
            acc = acc + s_ref[d].astype(F32)
        o_ref[...] = acc

    return pl.pallas_call(
        body, name=name, grid=(r // tr,),
        in_specs=[pl.BlockSpec((n, tr, w), lambda i: (0, i, 0))],
        out_specs=pl.BlockSpec((tr, w), lambda i: (i, 0)),
        out_shape=jax.ShapeDtypeStruct((r, w), F32),
        compiler_params=_cparams(("parallel",), 2 * _nbytes((n, tr, w), slots.dtype) + 4 * _nbytes((tr, w), F32)),
    )(slots)


def _adamw(w, g, m, v, name):
    r, c = w.shape
    tr = _tile(r, 512, 8)

    def body(w_ref, g_ref, m_ref, v_ref, d_ref, mo_ref, vo_ref):
        gv = g_ref[...]
        m_new = ADAM_B1 * m_ref[...] + (1.0 - ADAM_B1) * gv
        v_new = ADAM_B2 * v_ref[...] + (1.0 - ADAM_B2) * (gv * gv)
        m_hat = m_new / (1.0 - ADAM_B1 ** ADAM_STEP)
        v_hat = v_new / (1.0 - ADAM_B2 ** ADAM_STEP)
        d_ref[...] = -ADAM_LR * (m_hat / (jnp.sqrt(v_hat) + ADAM_EPS) + ADAM_WD * w_ref[...])
        mo_ref[...] = m_new
        vo_ref[...] = v_new

    blk = pl.BlockSpec((tr, c), lambda i: (i, 0))
    shp = jax.ShapeDtypeStruct((r, c), F32)
    return pl.pallas_call(
        body, name=name, grid=(r // tr,), in_specs=[blk] * 4, out_specs=[blk] * 3, out_shape=[shp] * 3,
        compiler_params=_cparams(("parallel",), 16 * _nbytes((tr, _round_up(c, LANES)), F32)),
    )(w, g, m, v)


def _pack_rows(parts, width, dtype, row_align):
    rows, spans, off = [], [], 0
    for p in parts:
        flat = p.reshape(-1).astype(dtype)
        n_rows = _round_up(-(-flat.shape[0] // width), row_align)
        flat = jnp.pad(flat, (0, n_rows * width - flat.shape[0]))
        rows.append(flat.reshape(n_rows, width))
        spans.append((off, n_rows))
        off += n_rows
    return jnp.concatenate(rows, axis=0), spans


def _unpack_rows(mat, span, shape):
    off, n_rows = span
    n = 1
    for s in shape:
        n *= s
    return mat[..., off:off + n_rows, :].reshape(mat.shape[:-2] + (-1,))[..., :n].reshape(mat.shape[:-2] + tuple(shape))


def _block_diag(w, size):
    n, b, _ = w.shape
    eye = jnp.eye(n, dtype=w.dtype)
    dense = (w[:, :, None, :] * eye[:, None, :, None]).reshape(n * b, n * b)
    return jnp.pad(dense, ((0, size - n * b), (0, size - n * b)))


def _diag_blocks(dense, n, b):
    return jnp.stack([dense[k * b:(k + 1) * b, k * b:(k + 1) * b] for k in range(n)])


def _pad_rows(a, rows):
    return jnp.pad(a, ((0, rows - a.shape[0]), (0, 0)))


def _pad_cols(a, cols):
    return jnp.pad(a, ((0, 0), (0, cols - a.shape[1])))


def _train_step(a):
    x = a["x"][0]
    target = a["loss_target"][0]
    s_dim, d = x.shape
    n_layers = a["ffn1_pre_g"].shape[0]
    f_shard = a["ffn1_w_gate"].shape[2]
    c_shard = a["rg_conv_b"].shape[1]
    c_dim = c_shard * N_DEV
    cp = _round_up(c_dim, LANES)
    conv_width = a["rg_conv_w"].shape[1]
    n_blocks, lru_block = a["rg_w_a"].shape[1], a["rg_w_a"].shape[2]
    d_attn = a["attn_w_q"].shape[2]
    n_heads = a["b_fgate"].shape[0]
    d_head = d_attn // n_heads
    attn_scale = d_head ** -0.5
    assert conv_width < 8 and n_heads <= LANES and n_layers == 2
    assert d_attn == d
    me = 4 * lax.axis_index("x") + 2 * lax.axis_index("y") + lax.axis_index("c")

    shard = {"rg_w_in": a["rg_w_in"][0].T, "rg_w_out": a["rg_w_out"][0], "w_kv": a["w_kv"].T,
             "attn_w_q": a["attn_w_q"][0], "attn_w_o": a["attn_w_o"][0]}
    for l in range(n_layers):
        for f in ("ffn1", "ffn2"):
            shard[(f, "gate", l)] = a[f + "_w_gate"][l].T
            shard[(f, "up", l)] = a[f + "_w_up"][l].T
            shard[(f, "down", l)] = a[f + "_w_down"][l]

    def ffn_names(f, l):
        return [(f, "gate", l), (f, "up", l), (f, "down", l)]

    def chunk_layout(names):
        spans, off = [], 0
        for nm in names:
            spans.append((nm, off, shard[nm].shape[0]))
            off += _round_up(shard[nm].shape[0], WIRE_ROW_ALIGN)
        return spans, off

    def pack_chunk(names, parts):
        return jnp.concatenate(
            [_pad_rows(parts[nm].astype(WIRE_DTYPE), _round_up(parts[nm].shape[0], WIRE_ROW_ALIGN)) for nm in names], axis=0)

    full = {}

    def unpack_chunk(names, gathered):
        for nm, o, n_rows in chunk_layout(names)[0]:
            full[nm] = gathered[:, o:o + n_rows, :].reshape(N_DEV * n_rows, d)

    fwd_chunks = [ffn_names("ffn1", 0)[:2], ffn_names("ffn1", 0)[2:], ["rg_w_in"], ["rg_w_out"],
                  ffn_names("ffn2", 0) + ["w_kv"], ffn_names("ffn1", 1) + ["attn_w_q", "attn_w_o"], ffn_names("ffn2", 1)]
    fwd_packs = [pack_chunk(names, shard) for names in fwd_chunks]
    unpack_chunk(fwd_chunks[0], _all_gather(fwd_packs[0], "gather_weights_first"))

    small_parts = [a["rg_conv_w"][0], a["rg_conv_b"][0], a["rg_b_a"][0], a["rg_b_x"][0], a["rg_lambda"][0], a["w_fgate"]]
    small_pack, small_spans = _pack_rows(small_parts, d, F32, 8)
    small_all = _all_gather(small_pack, "gather_small")
    fwd_handles, fwd_token = _exchange_start(fwd_packs[1:], False, [full[("ffn1", "up", 0)], small_all],
                                             "gather_weights_start")

    def land_weights(n, after):
        land = _exchange_wait(fwd_handles[n - 1], after, f"gather_weights_wait_{n}")
        unpack_chunk(fwd_chunks[n], _own_slot(land, fwd_packs[n], me))

    sm = [_unpack_rows(small_all, sp, p.shape) for sp, p in zip(small_spans, small_parts)]
    conv_w = jnp.moveaxis(sm[0], 0, 1).reshape(conv_width, c_dim)
    conv_b, b_a, b_x, lam = (v.reshape(1, c_dim) for v in sm[1:5])
    w_f = sm[5].reshape(d, n_heads)

    pconv = _pad_rows(_pad_cols(jnp.concatenate([conv_w, conv_b], axis=0), cp), 8)
    pvec = _pad_rows(_pad_cols(jnp.concatenate([b_a, b_x, lam], axis=0), cp), 8)
    wa_dense = _block_diag(a["rg_w_a"][0], cp).astype(MXU_DTYPE)
    wx_dense = _block_diag(a["rg_w_x"][0], cp).astype(MXU_DTYPE)
    wax = jnp.concatenate([wa_dense, wx_dense], axis=1)
    w_f_t = _pad_rows(w_f.T.astype(MXU_DTYPE), LANES)
    b_f = _pad_rows(_pad_cols(a["b_fgate"].reshape(1, n_heads), LANES), 8)

    def gain(name, l):
        return a[name][l].reshape(1, d)

    def ffn_fwd(h, f, l, after=None, down_chunk=None):
        xn = _rms_fwd(h, gain(f + "_pre_g", l), f"{f}_{l}_pre_norm", after)
        g, u, act = _ffn_up(xn, full[(f, "gate", l)], full[(f, "up", l)], f"{f}_{l}_up")
        if down_chunk is not None:
            land_weights(down_chunk, act)
        fo, h_new = _mm_rms_res(act, full[(f, "down", l)], h, gain(f + "_post_g", l), 0.5, f"{f}_{l}_down")
        return h_new, (h, xn, g, u, act, fo)

    h0 = x
    h0a, sv_f1_0 = ffn_fwd(h0, "ffn1", 0, fwd_token, down_chunk=1)
    land_weights(2, h0a)
    w_in_gate = _pad_rows(full["rg_w_in"][:c_dim], cp)
    w_in_rec = _pad_rows(full["rg_w_in"][c_dim:], cp)
    w_in_t = jnp.concatenate([w_in_gate, w_in_rec], axis=0)
    hn_rg = _rms_fwd(h0a, gain("mix_pre_g", 0), "rg_pre_norm")
    gx = _mm([(hn_rg, w_in_t)], "nt", F32, "rg_in_proj")
    rec = _conv_fwd(gx, pconv, conv_width, "rg_conv")
    gates = _mm([(rec, wax)], "nn", F32, "rg_gate_proj")
    h_rec, y_rg = _scan_fwd(gx, rec, gates, pvec, "rg_scan")
    land_weights(3, y_rg)
    w_out = _pad_rows(full["rg_w_out"], cp)
    m_rg, h0b = _mm_rms_res(y_rg, w_out, h0a, gain("mix_post_g", 0), 1.0, "rg_out_proj")
    land_weights(4, h0b)
    h1, sv_f2_0 = ffn_fwd(h0b, "ffn2", 0)
    hn_kv = _rms_fwd(h1, a["kv_norm_g"].reshape(1, d), "kv_norm")
    kv = _mm([(hn_kv, full["w_kv"])], "nt", MXU_DTYPE, "kv_proj")
    fpre = _mm([(hn_kv, w_f_t)], "nt", F32, "fgate_proj")
    c_cum = _fgate_fwd(fpre, b_f, "fgate_cumsum")
    c_heads = c_cum[:, :n_heads].T
    c_col, c_row = c_heads[:, :, None], c_heads[:, None, :]
    land_weights(5, c_cum)
    h1a, sv_f1_1 = ffn_fwd(h1, "ffn1", 1)
    hn_at = _rms_fwd(h1a, gain("mix_pre_g", 1), "attn_pre_norm")
    q_s = _mm([(hn_at, full["attn_w_q"])], "nn", MXU_DTYPE, "q_proj", out_scale=attn_scale)
    o2, lse = _pair_attn_fwd(q_s, kv, kv[:, d_attn:].T, c_col, c_row, "attn_fwd")
    m_at, h1b = _mm_rms_res(o2, full["attn_w_o"], h1a, gain("mix_post_g", 1), 1.0, "attn_out_proj")
    land_weights(6, h1b)
    y, sv_f2_1 = ffn_fwd(h1b, "ffn2", 1)
    dy, loss_part = _loss_head(y, target, "loss_head")

    grads_big = {}
    grads_rep = {}

    bwd_chunks = [ffn_names("ffn2", 1), ["attn_w_q", "attn_w_o"] + ffn_names("ffn1", 1),
                  ["w_kv"] + ffn_names("ffn2", 0), ["rg_w_in", "rg_w_out"], ffn_names("ffn1", 0)]
    bwd_sends, bwd_handles = [], []

    def send_grads(after):
        n = len(bwd_sends)
        send = jnp.concatenate(
            [jnp.pad(grads_big[nm].reshape(N_DEV, n_rows, d), ((0, 0), (0, _round_up(n_rows, WIRE_ROW_ALIGN) - n_rows), (0, 0)))
             for nm, _, n_rows in chunk_layout(bwd_chunks[n])[0]], axis=1)
        handles, token = _exchange_start([send], True, [after], f"exchange_grads_start_{n}")
        bwd_sends.append(send)
        bwd_handles.append(handles[0])
        return token

    def ffn_bwd(dh_out, saved, f, l, after=None, send_now=False):
        h, xn, g, u, act, fo = saved
        df, d_post = _rms_bwd(fo, gain(f + "_post_g", l), [dh_out], None, 0.5, MXU_DTYPE, f"{f}_{l}_post_norm_bwd", after)
        dg, du = _ffn_act_bwd(df, full[(f, "down", l)], g, u, f"{f}_{l}_act_bwd")
        grads_big[(f, "down", l)] = _mm([(act, df)], "tn", WIRE_DTYPE, f"{f}_{l}_dw_down")
        grads_big[(f, "gate", l)] = _mm([(dg, xn)], "tn", WIRE_DTYPE, f"{f}_{l}_dw_gate")
        grads_big[(f, "up", l)] = _mm([(du, xn)], "tn", WIRE_DTYPE, f"{f}_{l}_dw_up")
        sent = send_grads(df) if send_now else None
        dxn = _mm([(dg, full[(f, "gate", l)]), (du, full[(f, "up", l)])], "nn", F32, f"{f}_{l}_dx", sent)
        dh_in, d_pre = _rms_bwd(h, gain(f + "_pre_g", l), [dxn], dh_out, 1.0, F32, f"{f}_{l}_pre_norm_bwd")
        grads_rep[(f + "_post_g", l)] = d_post
        grads_rep[(f + "_pre_g", l)] = d_pre
        return dh_in

    dh = ffn_bwd(dy, sv_f2_1, "ffn2", 1)
    token = send_grads(dh)
    dm, d_post = _rms_bwd(m_at, gain("mix_post_g", 1), [dh], None, 1.0, MXU_DTYPE, "attn_post_norm_bwd", token)
    grads_rep[("mix_post_g", 1)] = d_post
    do2 = _mm([(dm, full["attn_w_o"])], "nt", F32, "attn_out_proj_dx")
    grads_big["attn_w_o"] = _mm([(o2, dm)], "tn", WIRE_DTYPE, "attn_out_proj_dw")
    delta = _attn_delta(do2, o2, n_heads, "attn_delta")
    dq2, dc_q, dk2, dv2, dc_k = _pair_attn_bwd(q_s, kv, c_col, c_row, lse, delta, do2, attn_scale, "attn_bwd")
    dc_heads = dc_q[:, 0, :] + dc_k[:, :, 0]
    dhn = _mm([(dq2, full["attn_w_q"])], "nt", F32, "q_proj_dx")
    grads_big["attn_w_q"] = _mm([(hn_at, dq2)], "tn", WIRE_DTYPE, "q_proj_dw")
    dh, d_pre = _rms_bwd(h1a, gain("mix_pre_g", 1), [dhn], dh, 1.0, F32, "attn_pre_norm_bwd")
    grads_rep[("mix_pre_g", 1)] = d_pre
    dh = ffn_bwd(dh, sv_f1_1, "ffn1", 1)
    token = send_grads(dh)
    dc_cum = _pad_cols(dc_heads.T, LANES)
    dfpre, db_f = _fgate_bwd(dc_cum, fpre, b_f, "fgate_cumsum_bwd")
    dhn_kv = _mm([(dk2, full["w_kv"][:d_attn]), (dv2, full["w_kv"][d_attn:])], "nn", F32, "kv_proj_dx")
    dhn_f = _mm([(dfpre, w_f_t)], "nn", F32, "fgate_proj_dx")
    grads_big["w_kv"] = jnp.concatenate([_mm([(dk2, hn_kv)], "tn", WIRE_DTYPE, "kv_proj_dw_k"),
                                         _mm([(dv2, hn_kv)], "tn", WIRE_DTYPE, "kv_proj_dw_v")], axis=0)
    dw_f_t = _mm([(dfpre, hn_kv)], "tn", F32, "fgate_proj_dw")
    dh, d_kvg = _rms_bwd(h1, a["kv_norm_g"].reshape(1, d), [dhn_kv, dhn_f], dh, 1.0, F32, "kv_norm_bwd", token)
    dh = ffn_bwd(dh, sv_f2_0, "ffn2", 0)
    token = send_grads(dh)
    dm, d_post = _rms_bwd(m_rg, gain("mix_post_g", 0), [dh], None, 1.0, MXU_DTYPE, "rg_post_norm_bwd", token)
    grads_rep[("mix_post_g", 0)] = d_post
    dy_rg = _mm([(dm, w_out)], "nt", F32, "rg_out_proj_dx")
    dw_out = _mm([(y_rg, dm)], "tn", WIRE_DTYPE, "rg_out_proj_dw")
    dgate, dra, dia, drec1, dpvec = _scan_bwd(dy_rg, gx, h_rec, rec, gates, pvec, "rg_scan_bwd")
    drec2 = _mm([(dra, wa_dense), (dia, wx_dense)], "nt", F32, "rg_gate_proj_dx")
    dwa_dense = _mm([(rec, dra)], "tn", F32, "rg_gate_proj_dwa")
    dwx_dense = _mm([(rec, dia)], "tn", F32, "rg_gate_proj_dwx")
    drec0, dpconv = _conv_bwd(drec1, drec2, gx, pconv, conv_width, "rg_conv_bwd")
    dhn = _mm([(dgate, w_in_gate), (drec0, w_in_rec)], "nn", F32, "rg_in_proj_dx")
    dw_in_gate = _mm([(dgate, hn_rg)], "tn", WIRE_DTYPE, "rg_in_proj_dw_gate")
    dw_in_rec = _mm([(drec0, hn_rg)], "tn", WIRE_DTYPE, "rg_in_proj_dw_rec")
    dh, d_pre = _rms_bwd(h0a, gain("mix_pre_g", 0), [dhn], dh, 1.0, F32, "rg_pre_norm_bwd")
    grads_rep[("mix_pre_g", 0)] = d_pre
    grads_big["rg_w_in"] = jnp.concatenate([dw_in_gate[:c_dim], dw_in_rec[:c_dim]], axis=0)
    grads_big["rg_w_out"] = dw_out[:c_dim]
    token = send_grads(dh)
    grad_x = ffn_bwd(dh, sv_f1_0, "ffn1", 0, token, send_now=True)

    g_shard = {}

    def land_grads(n, after):
        land = _exchange_wait(bwd_handles[n], after, f"exchange_grads_wait_{n}")
        own = lax.dynamic_index_in_dim(bwd_sends[n], me, axis=0, keepdims=False)
        g_chunk = _sum_slots(_own_slot(land, own, me), f"sum_weight_grads_{n}")
        for nm, o, n_rows in chunk_layout(bwd_chunks[n])[0]:
            g_shard[nm] = g_chunk[o:o + n_rows]

    for n in range(len(bwd_chunks) - 1):
        land_grads(n, grad_x)

    def gain_grad(name):
        return jnp.concatenate([grads_rep[(name, l)] for l in range(n_layers)], axis=0)

    rep_names = ["ffn1_pre_g", "ffn1_post_g", "mix_pre_g", "mix_post_g", "ffn2_pre_g", "ffn2_post_g"]
    rep_parts = [gain_grad(nm) for nm in rep_names]
    rep_names += ["kv_norm_g", "b_fgate", "rg_w_a", "rg_w_x", "rg_conv_w", "rg_conv_b", "rg_b_a", "rg_b_x", "rg_lambda", "w_fgate"]
    rep_parts += [
        d_kvg, db_f[0, :n_heads],
        _diag_blocks(dwa_dense, n_blocks, lru_block), _diag_blocks(dwx_dense, n_blocks, lru_block),
        dpconv[:conv_width, :c_dim], dpconv[conv_width, :c_dim],
        dpvec[0, :c_dim], dpvec[1, :c_dim], dpvec[2, :c_dim],
        dw_f_t[:n_heads].T]
    rep_pack, rep_spans = _pack_rows(rep_parts, d, F32, WIRE_ROW_ALIGN)
    rep_sum = _sum_slots(_all_gather(rep_pack, "gather_small_grads"), "sum_small_grads")
    g_rep = {nm: _unpack_rows(rep_sum, sp, p.shape) for nm, sp, p in zip(rep_names, rep_spans, rep_parts)}

    def my_cols(full_grad, n):
        return lax.dynamic_slice_in_dim(full_grad, me * n, n, axis=full_grad.ndim - 1)

    def ffn_grads(f):
        grad[f + "_w_gate"] = jnp.stack([g_shard[(f, "gate", l)].T for l in range(n_layers)])
        grad[f + "_w_up"] = jnp.stack([g_shard[(f, "up", l)].T for l in range(n_layers)])
        grad[f + "_w_down"] = jnp.stack([g_shard[(f, "down", l)] for l in range(n_layers)])

    grad = {}
    for nm in ("ffn1_pre_g", "ffn1_post_g", "mix_pre_g", "mix_post_g", "ffn2_pre_g", "ffn2_post_g"):
        grad[nm] = g_rep[nm]
    ffn_grads("ffn2")
    grad["rg_w_in"] = g_shard["rg_w_in"].T[None]
    grad["rg_conv_w"] = my_cols(g_rep["rg_conv_w"], c_shard)[None]
    for nm in ("rg_conv_b", "rg_b_a", "rg_b_x", "rg_lambda"):
        grad[nm] = my_cols(g_rep[nm], c_shard)[None]
    grad["rg_w_a"] = g_rep["rg_w_a"][None]
    grad["rg_w_x"] = g_rep["rg_w_x"][None]
    grad["rg_w_out"] = g_shard["rg_w_out"][None]
    grad["kv_norm_g"] = g_rep["kv_norm_g"].reshape(d)
    grad["w_kv"] = g_shard["w_kv"].T
    grad["w_fgate"] = lax.dynamic_slice_in_dim(g_rep["w_fgate"], me * (d // N_DEV), d // N_DEV, axis=0)
    grad["b_fgate"] = g_rep["b_fgate"]
    grad["attn_w_q"] = g_shard["attn_w_q"][None]
    grad["attn_w_o"] = g_shard["attn_w_o"][None]

    delta, new_m, new_v = {}, {}, {}

    def adamw(nm):
        w = a[nm]
        shape = w.shape
        two_d = (1, shape[0]) if w.ndim == 1 else (-1, shape[-1])
        dl, mo, vo = _adamw(w.reshape(two_d), grad[nm].reshape(two_d), a["m_" + nm].reshape(two_d),
                            a["v_" + nm].reshape(two_d), "adamw_" + nm)
        delta[nm], new_m[nm], new_v[nm] = dl.reshape(shape), mo.reshape(shape), vo.reshape(shape)
        grad[nm] = grad[nm].reshape(shape)

    last_names = ("ffn1_w_gate", "ffn1_w_up", "ffn1_w_down")
    for nm in WEIGHT_NAMES:
        if nm not in last_names:
            adamw(nm)
    land_grads(len(bwd_chunks) - 1, delta["attn_w_o"])
    ffn_grads("ffn1")
    for nm in last_names:
        adamw(nm)

    loss = lax.psum(loss_part[0, 0], AXES)
    return (loss, grad_x[None], *[grad[n] for n in WEIGHT_NAMES], *[delta[n] for n in WEIGHT_NAMES],
            *[new_m[n] for n in WEIGHT_NAMES], *[new_v[n] for n in WEIGHT_NAMES])


def kernel(x, ffn1_pre_g, ffn1_w_gate, ffn1_w_up, ffn1_w_down, ffn1_post_g, mix_pre_g, mix_post_g, ffn2_pre_g, ffn2_w_gate, ffn2_w_up, ffn2_w_down, ffn2_post_g, rg_w_in, rg_conv_w, rg_conv_b, rg_w_a, rg_b_a, rg_w_x, rg_b_x, rg_lambda, rg_w_out, kv_norm_g, w_kv, w_fgate, b_fgate, attn_w_q, attn_w_o, loss_target, m_ffn1_pre_g, m_ffn1_w_gate, m_ffn1_w_up, m_ffn1_w_down, m_ffn1_post_g, m_mix_pre_g, m_mix_post_g, m_ffn2_pre_g, m_ffn2_w_gate, m_ffn2_w_up, m_ffn2_w_down, m_ffn2_post_g, m_rg_w_in, m_rg_conv_w, m_rg_conv_b, m_rg_w_a, m_rg_b_a, m_rg_w_x, m_rg_b_x, m_rg_lambda, m_rg_w_out, m_kv_norm_g, m_w_kv, m_w_fgate, m_b_fgate, m_attn_w_q, m_attn_w_o, v_ffn1_pre_g, v_ffn1_w_gate, v_ffn1_w_up, v_ffn1_w_down, v_ffn1_post_g, v_mix_pre_g, v_mix_post_g, v_ffn2_pre_g, v_ffn2_w_gate, v_ffn2_w_up, v_ffn2_w_down, v_ffn2_post_g, v_rg_w_in, v_rg_conv_w, v_rg_conv_b, v_rg_w_a, v_rg_b_a, v_rg_w_x, v_rg_b_x, v_rg_lambda, v_rg_w_out, v_kv_norm_g, v_w_kv, v_w_fgate, v_b_fgate, v_attn_w_q, v_attn_w_o):
    return _train_step(dict(locals()))
```

```python
import functools

import jax
import jax.numpy as jnp
from jax import lax
from jax.experimental import pallas as pl
from jax.experimental.pallas import tpu as pltpu

F32 = jnp.float32
MXU_DTYPE = jnp.bfloat16
WIRE_DTYPE = jnp.bfloat16
N_DEV = 8
AXES = ("x", "y", "c")
LANES = 128
WIRE_ROW_ALIGN = 16
VMEM_LIMIT_MIN = 32 * 2 ** 20
VMEM_LIMIT_MAX = 56 * 2 ** 20

RMS_EPS = 1e-6
LRU_C = 8.0
ADAM_LR, ADAM_B1, ADAM_B2, ADAM_EPS, ADAM_WD, ADAM_STEP = 0.001, 0.9, 0.999, 1e-08, 0.01, 10

WEIGHT_NAMES = (
    "ffn1_pre_g", "ffn1_w_gate", "ffn1_w_up", "ffn1_w_down", "ffn1_post_g", "mix_pre_g", "mix_post_g",
    "ffn2_pre_g", "ffn2_w_gate", "ffn2_w_up", "ffn2_w_down", "ffn2_post_g", "rg_w_in", "rg_conv_w",
    "rg_conv_b", "rg_w_a", "rg_b_a", "rg_w_x", "rg_b_x", "rg_lambda", "rg_w_out", "kv_norm_g", "w_kv",
    "w_fgate", "b_fgate", "attn_w_q", "attn_w_o")


def _round_up(n, m):
    return (n + m - 1) // m * m


def _tile(dim, target, align=LANES):
    if dim <= target:
        return dim
    best = None
    t = align
    while t <= target:
        if dim % t == 0:
            best = t
        t += align
    return dim if best is None else best


def _cparams(semantics, vmem_estimate):
    limit = min(VMEM_LIMIT_MAX, max(VMEM_LIMIT_MIN, 2 * int(vmem_estimate)))
    return pltpu.CompilerParams(dimension_semantics=semantics, vmem_limit_bytes=limit)


def _nbytes(shape, dtype):
    n = 1
    for s in shape:
        n *= s
    return n * jnp.dtype(dtype).itemsize


def _sigmoid(x):
    return jax.nn.sigmoid(x)


def _softplus(x):
    return jnp.maximum(x, 0.0) + jnp.log1p(jnp.exp(-jnp.abs(x)))


def _expm1(x):
    series = x * (1.0 + x * (0.5 + x * (1.0 / 6.0 + x * (1.0 / 24.0 + x * (1.0 / 120.0)))))
    return jnp.where(jnp.abs(x) < 0.25, series, jnp.exp(x) - 1.0)


_GELU_C = 0.7978845608028654
_GELU_A = 0.044715


def _gelu(x):
    return 0.5 * x * (1.0 + jnp.tanh(_GELU_C * (x + _GELU_A * x * x * x)))


def _gelu_grad(x):
    t = jnp.tanh(_GELU_C * (x + _GELU_A * x * x * x))
    return 0.5 * (1.0 + t) + 0.5 * x * (1.0 - t * t) * _GELU_C * (1.0 + 3.0 * _GELU_A * x * x)


_DOT_DIMS = {"nn": ((1,), (0,)), "nt": ((1,), (1,)), "tn": ((0,), (0,))}


def _dot(a, b, mode):
    return lax.dot_general(a.astype(MXU_DTYPE), b.astype(MXU_DTYPE), (_DOT_DIMS[mode], ((), ())),
                           preferred_element_type=F32)


def _mm(pairs, mode, out_dtype, name, after=None, out_scale=None):
    a0, b0 = pairs[0]
    if mode == "tn":
        k_dim, m_dim = a0.shape
        n_dim = b0.shape[1]
    else:
        m_dim, k_dim = a0.shape
        n_dim = b0.shape[0] if mode == "nt" else b0.shape[1]
    for a, b in pairs:
        assert a.shape == a0.shape and b.shape == b0.shape
    tm = _tile(m_dim, 1408 if mode == "tn" else 512)
    whole = 1408 if mode == "tn" else 2816
    tn = _tile(n_dim, whole)
    tk = _tile(k_dim, whole)
    nk = k_dim // tk
    n_pairs = len(pairs)

    if mode == "tn":
        a_spec = pl.BlockSpec((tk, tm), lambda i, j, k: (k, i))
    else:
        a_spec = pl.BlockSpec((tm, tk), lambda i, j, k: (i, k))
    if mode == "nt":
        b_spec = pl.BlockSpec((tn, tk), lambda i, j, k: (j, k))
    else:
        b_spec = pl.BlockSpec((tk, tn), lambda i, j, k: (k, j))

    order = [] if after is None else [after]

    def body(*refs):
        ins, o_ref, acc = refs[:2 * n_pairs], refs[-2], refs[-1]
        k = pl.program_id(2)

        @pl.when(k == 0)
        def _():
            acc[...] = jnp.zeros_like(acc)

        s = acc[...]
        for p in range(n_pairs):
            s = s + _dot(ins[2 * p][...], ins[2 * p + 1][...], mode)
        acc[...] = s

        @pl.when(k == nk - 1)
        def _():
            r = acc[...] if out_scale is None else acc[...] * out_scale
            o_ref[...] = r.astype(out_dtype)

    est = (2 * n_pairs * (_nbytes((tm, tk), a0.dtype) + _nbytes((tk, tn), b0.dtype))
           + 2 * _nbytes((tm, tn), out_dtype) + 2 * _nbytes((tm, tn), F32))
    flat = [t for ab in pairs for t in ab]
    return pl.pallas_call(
        body, name=name, grid=(m_dim // tm, n_dim // tn, nk),
        in_specs=[a_spec, b_spec] * n_pairs + [_ANY] * len(order),
        out_specs=pl.BlockSpec((tm, tn), lambda i, j, k: (i, j)),
        out_shape=jax.ShapeDtypeStruct((m_dim, n_dim), out_dtype),
        scratch_shapes=[pltpu.VMEM((tm, tn), F32)],
        compiler_params=_cparams(("parallel", "parallel", "arbitrary"), est),
    )(*flat, *order)


_ANY = pl.BlockSpec(memory_space=pl.ANY)


def _rms_fwd(x, gain, name, after=None):
    s_dim, d = x.shape
    tm = _tile(s_dim, 512, 8)

    def body(*refs):
        x_ref, g_ref, o_ref = refs[0], refs[1], refs[-1]
        v = x_ref[...]
        r = lax.rsqrt(jnp.mean(v * v, axis=-1, keepdims=True) + RMS_EPS)
        o_ref[...] = (v * r * g_ref[...]).astype(MXU_DTYPE)

    order = [] if after is None else [after]
    return pl.pallas_call(
        body, name=name, grid=(s_dim // tm,),
        in_specs=[pl.BlockSpec((tm, d), lambda i: (i, 0)), pl.BlockSpec((1, d), lambda i: (0, 0))] + [_ANY] * len(order),
        out_specs=pl.BlockSpec((tm, d), lambda i: (i, 0)),
        out_shape=jax.ShapeDtypeStruct((s_dim, d), MXU_DTYPE),
        compiler_params=_cparams(("parallel",), 6 * _nbytes((tm, d), F32)),
    )(x, gain, *order)


def _rms_bwd(x, gain, dys, res, scale, out_dtype, name, after=None):
    s_dim, d = x.shape
    tm = _tile(s_dim, 512, 8)
    n_dy = len(dys)
    has_res = res is not None
    order = [] if after is None else [after]

    def body(*refs):
        x_ref, g_ref = refs[0], refs[1]
        dy_refs = refs[2:2 + n_dy]
        res_ref = refs[2 + n_dy] if has_res else None
        dx_ref, dg_ref = refs[-2], refs[-1]

        @pl.when(pl.program_id(0) == 0)
        def _():
            dg_ref[...] = jnp.zeros_like(dg_ref)

        v = x_ref[...]
        r = lax.rsqrt(jnp.mean(v * v, axis=-1, keepdims=True) + RMS_EPS)
        xh = v * r
        dy = dy_refs[0][...].astype(F32)
        for extra in dy_refs[1:]:
            dy = dy + extra[...].astype(F32)
        gd = dy * g_ref[...]
        dx = scale * r * (gd - xh * jnp.mean(gd * xh, axis=-1, keepdims=True))
        if has_res:
            dx = dx + res_ref[...]
        dx_ref[...] = dx.astype(out_dtype)
        dg_ref[...] += scale * jnp.sum(dy * xh, axis=0, keepdims=True)

    row = pl.BlockSpec((tm, d), lambda i: (i, 0))
    vec = pl.BlockSpec((1, d), lambda i: (0, 0))
    ops = [x, gain] + list(dys) + ([res] if has_res else [])
    return pl.pallas_call(
        body, name=name, grid=(s_dim // tm,),
        in_specs=[row, vec] + [row] * (n_dy + int(has_res)) + [_ANY] * len(order),
        out_specs=[row, vec],
        out_shape=[jax.ShapeDtypeStruct((s_dim, d), out_dtype), jax.ShapeDtypeStruct((1, d), F32)],
        compiler_params=_cparams(("arbitrary",), (2 * len(ops) + 6) * _nbytes((tm, d), F32)),
    )(*ops, *order)


def _mm_rms_res(a, b, h, gain, scale, name):
    s_dim, k_dim = a.shape
    d = b.shape[1]
    tm = _tile(s_dim, 512, 8)
    tk = _tile(k_dim, 2816)
    nk = k_dim // tk

    def body(a_ref, b_ref, h_ref, g_ref, f_ref, o_ref, acc):
        k = pl.program_id(1)

        @pl.when(k == 0)
        def _():
            acc[...] = jnp.zeros_like(acc)

        acc[...] += _dot(a_ref[...], b_ref[...], "nn")

        @pl.when(k == nk - 1)
        def _():
            f = acc[...]
            r = lax.rsqrt(jnp.mean(f * f, axis=-1, keepdims=True) + RMS_EPS)
            f_ref[...] = f
            o_ref[...] = h_ref[...] + scale * (f * r * g_ref[...])

    row = pl.BlockSpec((tm, d), lambda i, k: (i, 0))
    est = (2 * (_nbytes((tm, tk), a.dtype) + _nbytes((tk, d), b.dtype)) + 8 * _nbytes((tm, d), F32))
    return pl.pallas_call(
        body, name=name, grid=(s_dim // tm, nk),
        in_specs=[pl.BlockSpec((tm, tk), lambda i, k: (i, k)), pl.BlockSpec((tk, d), lambda i, k: (k, 0)),
                  row, pl.BlockSpec((1, d), lambda i, k: (0, 0))],
        out_specs=[row, row],
        out_shape=[jax.ShapeDtypeStruct((s_dim, d), F32), jax.ShapeDtypeStruct((s_dim, d), F32)],
        scratch_shapes=[pltpu.VMEM((tm, d), F32)],
        compiler_params=_cparams(("parallel", "arbitrary"), est),
    )(a, b, h, gain)


def _ffn_up(xn, wg_t, wu_t, name):
    s_dim, d = xn.shape
    f_dim = wg_t.shape[0]
    tm = _tile(s_dim, 2048, 8)
    tf = _tile(f_dim, 256)

    def body(x_ref, wg_ref, wu_ref, g_ref, u_ref, a_ref):
        x = x_ref[...]
        g = _dot(x, wg_ref[...], "nt")
        u = _dot(x, wu_ref[...], "nt")
        g_ref[...] = g.astype(MXU_DTYPE)
        u_ref[...] = u.astype(MXU_DTYPE)
        a_ref[...] = (g * _sigmoid(g) * u).astype(MXU_DTYPE)

    w_spec = pl.BlockSpec((tf, d), lambda i, j: (j, 0))
    o_spec = pl.BlockSpec((tm, tf), lambda i, j: (i, j))
    o_shape = jax.ShapeDtypeStruct((s_dim, f_dim), MXU_DTYPE)
    est = 2 * _nbytes((tm, d), xn.dtype) + 4 * _nbytes((tf, d), wg_t.dtype) + 10 * _nbytes((tm, tf), F32)
    return pl.pallas_call(
        body, name=name, grid=(s_dim // tm, f_dim // tf),
        in_specs=[pl.BlockSpec((tm, d), lambda i, j: (i, 0)), w_spec, w_spec],
        out_specs=[o_spec, o_spec, o_spec], out_shape=[o_shape, o_shape, o_shape],
        compiler_params=_cparams(("parallel", "parallel"), est),
    )(xn, wg_t, wu_t)


def _ffn_act_bwd(df, wd, g, u, name):
    s_dim, d = df.shape
    f_dim = wd.shape[0]
    tm = _tile(s_dim, 2048, 8)
    tf = _tile(f_dim, 256)

    def body(df_ref, wd_ref, g_ref, u_ref, dg_ref, du_ref):
        dh = _dot(df_ref[...], wd_ref[...], "nt")
        gv = g_ref[...].astype(F32)
        uv = u_ref[...].astype(F32)
        sg = _sigmoid(gv)
        dg_ref[...] = (dh * uv * (sg * (1.0 + gv * (1.0 - sg)))).astype(MXU_DTYPE)
        du_ref[...] = (dh * gv * sg).astype(MXU_DTYPE)

    t_spec = pl.BlockSpec((tm, tf), lambda i, j: (i, j))
    o_shape = jax.ShapeDtypeStruct((s_dim, f_dim), MXU_DTYPE)
    est = 2 * _nbytes((tm, d), df.dtype) + 2 * _nbytes((tf, d), wd.dtype) + 12 * _nbytes((tm, tf), F32)
    return pl.pallas_call(
        body, name=name, grid=(s_dim // tm, f_dim // tf),
        in_specs=[pl.BlockSpec((tm, d), lambda i, j: (i, 0)), pl.BlockSpec((tf, d), lambda i, j: (j, 0)),
                  t_spec, t_spec],
        out_specs=[t_spec, t_spec], out_shape=[o_shape, o_shape],
        compiler_params=_cparams(("parallel", "parallel"), est),
    )(df, wd, g, u)


def _loss_head(y, target, name):
    s_dim, d = y.shape
    tm = _tile(s_dim, 512, 8)
    nt = s_dim // tm

    def body(y_ref, t_ref, dy_ref, loss_ref, acc):
        i = pl.program_id(0)

        @pl.when(i == 0)
        def _():
            acc[...] = jnp.zeros_like(acc)

        e = y_ref[...] - t_ref[...]
        dy_ref[...] = e * (1.0 / d)
        acc[...] += jnp.sum(e * e, axis=0, keepdims=True)

        @pl.when(i == nt - 1)
        def _():
            loss_ref[...] = jnp.sum(acc[...], axis=1, keepdims=True) * (0.5 / d)

    row = pl.BlockSpec((tm, d), lambda i: (i, 0))
    return pl.pallas_call(
        body, name=name, grid=(nt,), in_specs=[row, row],
        out_specs=[row, pl.BlockSpec((1, 1), lambda i: (0, 0))],
        out_shape=[jax.ShapeDtypeStruct((s_dim, d), F32), jax.ShapeDtypeStruct((1, 1), F32)],
        scratch_shapes=[pltpu.VMEM((1, d), F32)],
        compiler_params=_cparams(("arbitrary",), 8 * _nbytes((tm, d), F32)),
    )(y, target)


def _shift_down(v, sh, row):
    if sh == 0:
        return v
    return jnp.where(row >= sh, pltpu.roll(v, sh, 0), 0.0)


def _shift_up(v, sh, row):
    if sh == 0:
        return v
    n = v.shape[0]
    return jnp.where(row < n - sh, pltpu.roll(v, n - sh, 0), 0.0)


def _conv_fwd(gx, pconv, width, name):
    s_dim, cp2 = gx.shape
    cp = cp2 // 2
    nc = cp // LANES

    def body(x_ref, p_ref, o_ref):
        x = x_ref[...]
        row = lax.broadcasted_iota(jnp.int32, x.shape, 0)
        y = jnp.zeros_like(x) + p_ref[pl.ds(width, 1), :]
        for k in range(width):
            y = y + p_ref[pl.ds(k, 1), :] * _shift_down(x, width - 1 - k, row)
        o_ref[...] = y

    return pl.pallas_call(
        body, name=name, grid=(nc,),
        in_specs=[pl.BlockSpec((s_dim, LANES), lambda j: (0, nc + j)), pl.BlockSpec((8, LANES), lambda j: (0, j))],
        out_specs=pl.BlockSpec((s_dim, LANES), lambda j: (0, j)),
        out_shape=jax.ShapeDtypeStruct((s_dim, cp), F32),
        compiler_params=_cparams(("parallel",), 10 * _nbytes((s_dim, LANES), F32)),
    )(gx, pconv)


def _conv_bwd(d1, d2, gx, pconv, width, name):
    s_dim, cp = d1.shape
    nc = cp // LANES

    def body(d1_ref, d2_ref, x_ref, p_ref, dx_ref, dp_ref):
        d = d1_ref[...] + d2_ref[...]
        x = x_ref[...]
        row = lax.broadcasted_iota(jnp.int32, x.shape, 0)
        dx = jnp.zeros_like(d)
        dp_ref[...] = jnp.zeros_like(dp_ref)
        for k in range(width):
            sh = width - 1 - k
            dx = dx + p_ref[pl.ds(k, 1), :] * _shift_up(d, sh, row)
            dp_ref[pl.ds(k, 1), :] = jnp.sum(d * _shift_down(x, sh, row), axis=0, keepdims=True)
        dp_ref[pl.ds(width, 1), :] = jnp.sum(d, axis=0, keepdims=True)
        dx_ref[...] = dx.astype(MXU_DTYPE)

    strip = pl.BlockSpec((s_dim, LANES), lambda j: (0, j))
    par = pl.BlockSpec((8, LANES), lambda j: (0, j))
    return pl.pallas_call(
        body, name=name, grid=(nc,),
        in_specs=[strip, strip, pl.BlockSpec((s_dim, LANES), lambda j: (0, nc + j)), par],
        out_specs=[strip, par],
        out_shape=[jax.ShapeDtypeStruct((s_dim, cp), MXU_DTYPE), jax.ShapeDtypeStruct((8, cp), F32)],
        compiler_params=_cparams(("parallel",), 14 * _nbytes((s_dim, LANES), F32)),
    )(d1, d2, gx, pconv)


def _lru_coeffs(ra, ia, p_ref):
    r = _sigmoid(ra + p_ref[pl.ds(0, 1), :])
    i = _sigmoid(ia + p_ref[pl.ds(1, 1), :])
    sp = _softplus(-p_ref[pl.ds(2, 1), :])
    log_a = -LRU_C * r * sp
    a = jnp.exp(log_a)
    mult = jnp.sqrt(-_expm1(2.0 * log_a))
    return r, i, sp, a, mult


def _scan_fwd(gx, rec, gates, pvec, name):
    s_dim, cp = rec.shape
    ts = _tile(s_dim, 256, 8)
    nt = s_dim // ts

    def body(gate_ref, rec_ref, ra_ref, ia_ref, p_ref, h_ref, y_ref, a_s, u_s, carry):
        @pl.when(pl.program_id(0) == 0)
        def _():
            carry[...] = jnp.zeros_like(carry)

        rec_v = rec_ref[...]
        _, i, _, a, mult = _lru_coeffs(ra_ref[...], ia_ref[...], p_ref)
        a_s[...] = a
        u_s[...] = mult * (i * rec_v)

        def step(t, h):
            h = a_s[pl.ds(t, 1), :] * h + u_s[pl.ds(t, 1), :]
            h_ref[pl.ds(t, 1), :] = h
            return h

        carry[pl.ds(0, 1), :] = lax.fori_loop(0, ts, step, carry[pl.ds(0, 1), :], unroll=8)
        y_ref[...] = (_gelu(gate_ref[...]) * h_ref[...]).astype(MXU_DTYPE)

    blk = pl.BlockSpec((ts, cp), lambda t: (t, 0))
    return pl.pallas_call(
        body, name=name, grid=(nt,),
        in_specs=[blk, blk, blk, pl.BlockSpec((ts, cp), lambda t: (t, 1)), pl.BlockSpec((8, cp), lambda t: (0, 0))],
        out_specs=[blk, blk],
        out_shape=[jax.ShapeDtypeStruct((s_dim, cp), F32), jax.ShapeDtypeStruct((s_dim, cp), MXU_DTYPE)],
        scratch_shapes=[pltpu.VMEM((ts, cp), F32), pltpu.VMEM((ts, cp), F32), pltpu.VMEM((8, cp), F32)],
        compiler_params=_cparams(("arbitrary",), 14 * _nbytes((ts, cp), F32)),
    )(gx, rec, gates, gates, pvec)


def _scan_bwd(dy, gx, hrec, rec, gates, pvec, name):
    s_dim, cp = rec.shape
    ts = _tile(s_dim, 128, 8)
    nt = s_dim // ts

    def body(dy_ref, gate_ref, h_ref, hp_ref, rec_ref, ra_ref, ia_ref, p_ref,
             dgate_ref, dra_ref, dia_ref, drec_ref, dp_ref, a_s, d_s, carry):
        t_id = pl.program_id(0)

        @pl.when(t_id == 0)
        def _():
            carry[...] = jnp.zeros_like(carry)
            dp_ref[...] = jnp.zeros_like(dp_ref)

        rec_v = rec_ref[...]
        r, i, sp, a, mult = _lru_coeffs(ra_ref[...], ia_ref[...], p_ref)
        gate = gate_ref[...]
        dyv = dy_ref[...]
        h = h_ref[...]
        dgate_ref[...] = (dyv * h * _gelu_grad(gate)).astype(MXU_DTYPE)
        a_s[...] = a
        d_s[...] = dyv * _gelu(gate)

        def step(k, c):
            t = ts - 1 - k
            d = d_s[pl.ds(t, 1), :] + c
            d_s[pl.ds(t, 1), :] = d
            return a_s[pl.ds(t, 1), :] * d

        carry[pl.ds(0, 1), :] = lax.fori_loop(0, ts, step, carry[pl.ds(0, 1), :], unroll=8)
        dh = d_s[...]
        row = lax.broadcasted_iota(jnp.int32, h.shape, 0)
        first = jnp.where(t_id == nt - 1, 0.0, 1.0) * hp_ref[pl.ds(7, 1), :]
        h_prev = jnp.where(row == 0, first, pltpu.roll(h, 1, 0))
        dix = dh * mult
        dla = dh * h_prev * a - dh * (i * rec_v) * (a * a) / mult
        dra = dla * (-LRU_C * sp) * r * (1.0 - r)
        dia = dix * rec_v * i * (1.0 - i)
        dra_ref[...] = dra.astype(MXU_DTYPE)
        dia_ref[...] = dia.astype(MXU_DTYPE)
        drec_ref[...] = dix * i
        dsp = jnp.sum(dla * (-LRU_C * r), axis=0, keepdims=True)
        dp_ref[pl.ds(0, 1), :] += jnp.sum(dra, axis=0, keepdims=True)
        dp_ref[pl.ds(1, 1), :] += jnp.sum(dia, axis=0, keepdims=True)
        dp_ref[pl.ds(2, 1), :] += dsp * (-_sigmoid(-p_ref[pl.ds(2, 1), :]))

    blk = pl.BlockSpec((ts, cp), lambda t: (nt - 1 - t, 0))
    prev = pl.BlockSpec((8, cp), lambda t: (jnp.maximum((nt - 1 - t) * (ts // 8) - 1, 0), 0))
    par = pl.BlockSpec((8, cp), lambda t: (0, 0))
    lo = jax.ShapeDtypeStruct((s_dim, cp), MXU_DTYPE)
    return pl.pallas_call(
        body, name=name, grid=(nt,),
        in_specs=[blk, blk, blk, prev, blk, blk, pl.BlockSpec((ts, cp), lambda t: (nt - 1 - t, 1)), par],
        out_specs=[blk, blk, blk, blk, par],
        out_shape=[lo, lo, lo, jax.ShapeDtypeStruct((s_dim, cp), F32), jax.ShapeDtypeStruct((8, cp), F32)],
        scratch_shapes=[pltpu.VMEM((ts, cp), F32), pltpu.VMEM((ts, cp), F32), pltpu.VMEM((8, cp), F32)],
        compiler_params=_cparams(("arbitrary",), 40 * _nbytes((ts, cp), F32)),
    )(dy, gx, hrec, hrec, rec, gates, gates, pvec)


def _fgate_fwd(fpre, bias, name):
    s_dim, w = fpre.shape
    ts = _tile(s_dim, 512, 8)

    def body(f_ref, b_ref, c_ref, lf_s, carry):
        @pl.when(pl.program_id(0) == 0)
        def _():
            carry[...] = jnp.zeros_like(carry)

        lf_s[...] = -_softplus(-(f_ref[...] + b_ref[pl.ds(0, 1), :]))

        def step(t, c):
            c = c + lf_s[pl.ds(t, 1), :]
            c_ref[pl.ds(t, 1), :] = c
            return c

        carry[pl.ds(0, 1), :] = lax.fori_loop(0, ts, step, carry[pl.ds(0, 1), :], unroll=8)

    blk = pl.BlockSpec((ts, w), lambda t: (t, 0))
    return pl.pallas_call(
        body, name=name, grid=(s_dim // ts,),
        in_specs=[blk, pl.BlockSpec((8, w), lambda t: (0, 0))], out_specs=blk,
        out_shape=jax.ShapeDtypeStruct((s_dim, w), F32),
        scratch_shapes=[pltpu.VMEM((ts, w), F32), pltpu.VMEM((8, w), F32)],
        compiler_params=_cparams(("arbitrary",), 12 * _nbytes((ts, w), F32)),
    )(fpre, bias)


def _fgate_bwd(dc, fpre, bias, name):
    s_dim, w = fpre.shape
    ts = _tile(s_dim, 512, 8)
    nt = s_dim // ts

    def body(dc_ref, f_ref, b_ref, df_ref, db_ref, d_s, carry):
        @pl.when(pl.program_id(0) == 0)
        def _():
            carry[...] = jnp.zeros_like(carry)
            db_ref[...] = jnp.zeros_like(db_ref)

        d_s[...] = dc_ref[...]

        def step(k, c):
            t = ts - 1 - k
            c = c + d_s[pl.ds(t, 1), :]
            d_s[pl.ds(t, 1), :] = c
            return c

        carry[pl.ds(0, 1), :] = lax.fori_loop(0, ts, step, carry[pl.ds(0, 1), :], unroll=8)
        df = d_s[...] * _sigmoid(-(f_ref[...] + b_ref[pl.ds(0, 1), :]))
        df_ref[...] = df
        db_ref[pl.ds(0, 1), :] += jnp.sum(df, axis=0, keepdims=True)

    blk = pl.BlockSpec((ts, w), lambda t: (nt - 1 - t, 0))
    par = pl.BlockSpec((8, w), lambda t: (0, 0))
    return pl.pallas_call(
        body, name=name, grid=(nt,), in_specs=[blk, blk, par], out_specs=[blk, par],
        out_shape=[jax.ShapeDtypeStruct((s_dim, w), F32), jax.ShapeDtypeStruct((8, w), F32)],
        scratch_shapes=[pltpu.VMEM((ts, w), F32), pltpu.VMEM((8, w), F32)],
        compiler_params=_cparams(("arbitrary",), 12 * _nbytes((ts, w), F32)),
    )(dc, fpre, bias)


def _head_lanes(hh, dh):
    lane = lax.broadcasted_iota(jnp.int32, (1, LANES), 1)
    return (lane >= hh * dh) & (lane < (hh + 1) * dh)


def _pair_attn_fwd(q, kv, v_t, c_col, c_row, name):
    s_dim, da = q.shape
    n_h = c_col.shape[0]
    dh = da // n_h
    assert LANES % dh == 0 and da % LANES == 0
    hb = LANES // dh
    n_blocks = da // LANES
    t = _tile(s_dim, 512, LANES)
    nb = s_dim // t

    pairs = [(i, j) for i in range(nb) for j in range(i + 1)]
    i_tab = jnp.asarray([p[0] for p in pairs], jnp.int32)
    j_tab = jnp.asarray([p[1] for p in pairs], jnp.int32)

    def body(i_ref, j_ref, q_ref, k_ref, vt_ref, cq_ref, ck_ref, o_ref, lse_ref, m_s, l_s, acc):
        i, j = i_ref[pl.program_id(1)], j_ref[pl.program_id(1)]

        @pl.when(j == 0)
        def _():
            m_s[...] = jnp.full_like(m_s, -jnp.inf)
            l_s[...] = jnp.zeros_like(l_s)
            acc[...] = jnp.zeros_like(acc)

        def tile(masked):
            qv = q_ref[...]
            for hh in range(hb):
                st = _dot(k_ref[...], jnp.where(_head_lanes(hh, dh), qv, jnp.zeros_like(qv)), "nt")
                st = st + (cq_ref[hh] - ck_ref[hh])
                if masked:
                    keep = lax.broadcasted_iota(jnp.int32, (t, t), 0) <= lax.broadcasted_iota(jnp.int32, (t, t), 1)
                    st = jnp.where(keep, st, -jnp.inf)
                m_prev = m_s[hh]
                m_new = jnp.maximum(m_prev, jnp.max(st, axis=0, keepdims=True))
                alpha = jnp.exp(m_prev - m_new)
                p = jnp.exp(st - m_new)
                l_s[hh] = alpha * l_s[hh] + jnp.sum(p, axis=0, keepdims=True)
                acc[hh] = alpha * acc[hh] + _dot(vt_ref[...], p, "nn")
                m_s[hh] = m_new

        pl.when(j < i)(functools.partial(tile, False))
        pl.when(j == i)(functools.partial(tile, True))

        @pl.when(j == i)
        def _():
            feat = lax.broadcasted_iota(jnp.int32, (LANES, 1), 0)
            out_t = jnp.zeros((LANES, t), F32)
            for hh in range(hb):
                out_t = jnp.where((feat >= hh * dh) & (feat < (hh + 1) * dh), acc[hh] / l_s[hh], out_t)
                lse_ref[hh] = m_s[hh] + jnp.log(l_s[hh])
            o_ref[...] = out_t.T

    q_spec = pl.BlockSpec((t, LANES), lambda b, p, it, jt: (it[p], b))
    k_spec = pl.BlockSpec((t, LANES), lambda b, p, it, jt: (jt[p], b))
    vt_spec = pl.BlockSpec((LANES, t), lambda b, p, it, jt: (b, jt[p]))
    cq_spec = pl.BlockSpec((hb, 1, t), lambda b, p, it, jt: (b, 0, it[p]))
    ck_spec = pl.BlockSpec((hb, t, 1), lambda b, p, it, jt: (b, jt[p], 0))
    return pl.pallas_call(
        body, name=name,
        grid_spec=pltpu.PrefetchScalarGridSpec(
            num_scalar_prefetch=2, grid=(n_blocks, len(pairs)),
            in_specs=[q_spec, k_spec, vt_spec, cq_spec, ck_spec], out_specs=[q_spec, cq_spec],
            scratch_shapes=[pltpu.VMEM((hb, 1, t), F32), pltpu.VMEM((hb, 1, t), F32), pltpu.VMEM((hb, LANES, t), F32)]),
        out_shape=[jax.ShapeDtypeStruct((s_dim, da), F32), jax.ShapeDtypeStruct((n_h, 1, s_dim), F32)],
        compiler_params=_cparams(("parallel", "arbitrary"), 10 * hb * _nbytes((t, t), F32)),
    )(i_tab, j_tab, q, kv, v_t, c_row, c_col)


def _attn_delta(do, o, n_h, name):
    s_dim, da = o.shape
    dh = da // n_h
    hb = LANES // dh
    t = _tile(s_dim, 512, LANES)

    def body(do_ref, o_ref, d_ref):
        prod_t = (do_ref[...].astype(MXU_DTYPE).astype(F32) * o_ref[...]).T
        for hh in range(hb):
            d_ref[hh] = jnp.sum(prod_t[hh * dh:(hh + 1) * dh], axis=0, keepdims=True)

    blk = pl.BlockSpec((t, LANES), lambda b, i: (i, b))
    return pl.pallas_call(
        body, name=name, grid=(da // LANES, s_dim // t), in_specs=[blk, blk],
        out_specs=pl.BlockSpec((hb, 1, t), lambda b, i: (b, 0, i)),
        out_shape=jax.ShapeDtypeStruct((n_h, 1, s_dim), F32),
        compiler_params=_cparams(("parallel", "parallel"), 8 * _nbytes((t, LANES), F32)),
    )(do, o)


def _pair_attn_bwd(q, kv, c_col, c_row, lse, delta, do, scale, name):
    s_dim, da = q.shape
    n_h = c_col.shape[0]
    dh = da // n_h
    hb = LANES // dh
    n_blocks = da // LANES
    t = _tile(s_dim, 512, LANES)
    nb = s_dim // t

    pairs = [(i, j) for j in range(nb) for i in range(j, nb)]
    i_tab = jnp.asarray([p[0] for p in pairs], jnp.int32)
    j_tab = jnp.asarray([p[1] for p in pairs], jnp.int32)

    def body(i_ref, j_ref, q_ref, k_ref, v_ref, cq_ref, ck_ref, lse_ref, dl_ref, do_ref,
             dq_ref, dcq_ref, dk_ref, dv_ref, dck_ref, dk_acc, dv_acc, dck_acc):
        i, j = i_ref[pl.program_id(1)], j_ref[pl.program_id(1)]

        @pl.when(pl.program_id(1) == 0)
        def _():
            dq_ref[...] = jnp.zeros_like(dq_ref)
            dcq_ref[...] = jnp.zeros_like(dcq_ref)

        @pl.when(i == j)
        def _():
            dk_acc[...] = jnp.zeros_like(dk_acc)
            dv_acc[...] = jnp.zeros_like(dv_acc)
            dck_acc[...] = jnp.zeros_like(dck_acc)

        def tile(masked):
            start = pl.multiple_of(i * t, t)
            qv, kv_ = q_ref[...], k_ref[...]
            dov = do_ref[...].astype(MXU_DTYPE)
            for hh in range(hb):
                lanes = _head_lanes(hh, dh)
                qm = jnp.where(lanes, qv, jnp.zeros_like(qv))
                km = jnp.where(lanes, kv_, jnp.zeros_like(kv_))
                dom = jnp.where(lanes, dov, jnp.zeros_like(dov))
                st = _dot(kv_, qm, "nt") + (cq_ref[hh] - ck_ref[hh])
                if masked:
                    keep = lax.broadcasted_iota(jnp.int32, (t, t), 0) <= lax.broadcasted_iota(jnp.int32, (t, t), 1)
                    st = jnp.where(keep, st, -jnp.inf)
                pt = jnp.exp(st - lse_ref[hh])
                dst = pt * (_dot(v_ref[...], dom, "nt") - dl_ref[hh])
                dv_acc[...] += _dot(pt, dom, "nn")
                dk_acc[...] += _dot(dst, qm, "nn")
                dq_ref[pl.ds(start, t), :] += _dot(dst, km, "tn") * scale
                dcq_ref[hh, :, pl.ds(start, t)] += jnp.sum(dst, axis=0, keepdims=True)
                dck_acc[hh] -= jnp.sum(dst, axis=1, keepdims=True)

        pl.when(i > j)(functools.partial(tile, False))
        pl.when(i == j)(functools.partial(tile, True))

        @pl.when(i == nb - 1)
        def _():
            dk_ref[...] = dk_acc[...]
            dv_ref[...] = dv_acc[...]
            dck_ref[...] = dck_acc[...]

    q_spec = pl.BlockSpec((t, LANES), lambda b, p, it, jt: (it[p], b))
    qrow_spec = pl.BlockSpec((hb, 1, t), lambda b, p, it, jt: (b, 0, it[p]))
    k_spec = pl.BlockSpec((t, LANES), lambda b, p, it, jt: (jt[p], b))
    v_spec = pl.BlockSpec((t, LANES), lambda b, p, it, jt: (jt[p], n_blocks + b))
    kcol_spec = pl.BlockSpec((hb, t, 1), lambda b, p, it, jt: (b, jt[p], 0))
    wide = jax.ShapeDtypeStruct((s_dim, da), F32)
    return pl.pallas_call(
        body, name=name,
        grid_spec=pltpu.PrefetchScalarGridSpec(
            num_scalar_prefetch=2, grid=(n_blocks, len(pairs)),
            in_specs=[q_spec, k_spec, v_spec, qrow_spec, kcol_spec, qrow_spec, qrow_spec, q_spec],
            out_specs=[pl.BlockSpec((s_dim, LANES), lambda b, p, it, jt: (0, b)),
                       pl.BlockSpec((hb, 1, s_dim), lambda b, p, it, jt: (b, 0, 0)), k_spec, k_spec, kcol_spec],
            scratch_shapes=[pltpu.VMEM((t, LANES), F32), pltpu.VMEM((t, LANES), F32), pltpu.VMEM((hb, t, 1), F32)]),
        out_shape=[wide, jax.ShapeDtypeStruct((n_h, 1, s_dim), F32), wide, wide,
                   jax.ShapeDtypeStruct((n_h, s_dim, 1), F32)],
        compiler_params=_cparams(("parallel", "arbitrary"),
                                 10 * hb * _nbytes((t, t), F32) + 4 * _nbytes((s_dim, LANES), F32)),
    )(i_tab, j_tab, q, kv, kv, c_row, c_col, lse, delta, do)


_HBM = pl.BlockSpec(memory_space=pltpu.HBM)
_MESH_ID = pl.DeviceIdType.MESH


def _all_gather(block, name):
    r, w = block.shape

    def body(x_ref, out_ref, send_sems, recv_sems, local_sem):
        x, y, c = lax.axis_index("x"), lax.axis_index("y"), lax.axis_index("c")
        me, sibling = (x, y, c), (x, y, 1 - c)
        chips = [(1 - x, y), (x, 1 - y), (1 - x, 1 - y)]

        def slot(px, py, pc):
            return out_ref.at[4 * px + 2 * py + pc]

        def copy(k, blk, to, src=None):
            return pltpu.make_async_remote_copy(
                src_ref=slot(*blk) if src is None else src, dst_ref=slot(*blk),
                send_sem=send_sems.at[k], recv_sem=recv_sems.at[k], device_id=to, device_id_type=_MESH_ID)

        mine = pltpu.make_async_copy(x_ref, slot(*me), local_sem)
        mine.start()
        first = [copy(0, me, sibling, src=x_ref)]
        first += [copy(1 + n, me, (*chip, c), src=x_ref) for n, chip in enumerate(chips)]
        for cp in first:
            cp.start()
        passed = [copy(4 + n, (*chip, c), sibling) for n, chip in enumerate(chips)]
        for n, chip in enumerate(chips):
            copy(1 + n, (*chip, c), me).wait_recv()
            passed[n].start()
        copy(0, sibling, me).wait_recv()
        for n, chip in enumerate(chips):
            copy(4 + n, (*chip, 1 - c), me).wait_recv()
        for cp in first + passed:
            cp.wait_send()
        mine.wait()

    return pl.pallas_call(
        body, name=name, out_shape=jax.ShapeDtypeStruct((N_DEV, r, w), block.dtype),
        in_specs=[_HBM], out_specs=_HBM,
        scratch_shapes=[pltpu.SemaphoreType.DMA((7,)), pltpu.SemaphoreType.DMA((7,)), pltpu.SemaphoreType.DMA],
    )(block)


_SEM = pl.BlockSpec(memory_space=pltpu.SEMAPHORE)
_EFFECT = pltpu.SideEffectType.DATAFLOW_SIDE_EFFECTING


def _exchange_start(srcs, personalized, after, name):
    n = len(srcs)
    n_after = len(after)
    lands = [lax.empty((N_DEV,) + s.shape[-2:], s.dtype) for s in srcs]

    def body(*refs):
        src_refs, land_refs = refs[:n], refs[n:2 * n]
        outs = refs[2 * n + n_after:]
        send_sems, recv_sems, token = outs[:n], outs[n:2 * n], outs[-1]
        x, y, c = lax.axis_index("x"), lax.axis_index("y"), lax.axis_index("c")
        mine = 4 * x + 2 * y + c
        for ci in range(n):
            for k in range(1, N_DEV):
                px = 1 - x if k & 4 else x
                py = 1 - y if k & 2 else y
                pc = 1 - c if k & 1 else c
                src = src_refs[ci].at[4 * px + 2 * py + pc] if personalized else src_refs[ci]
                pltpu.make_async_remote_copy(
                    src_ref=src, dst_ref=land_refs[ci].at[mine], send_sem=send_sems[ci], recv_sem=recv_sems[ci],
                    device_id=(px, py, pc), device_id_type=_MESH_ID).start()
        token[...] = jnp.zeros_like(token)

    sem = pltpu.SemaphoreType.DMA(())
    out_shape = ([sem] * (2 * n) + [pltpu.HBM(s.shape, s.dtype) for s in srcs]
                 + [pltpu.HBM(l.shape, l.dtype) for l in lands] + [jax.ShapeDtypeStruct((8, LANES), F32)])
    res = pl.pallas_call(
        body, name=name, out_shape=tuple(out_shape),
        in_specs=[_HBM] * (2 * n) + [_ANY] * n_after,
        out_specs=tuple([_SEM] * (2 * n) + [_HBM] * (2 * n) + [pl.BlockSpec(memory_space=pltpu.VMEM)]),
        input_output_aliases={i: 2 * n + i for i in range(2 * n)},
        compiler_params=pltpu.CompilerParams(has_side_effects=_EFFECT),
    )(*[pltpu.with_memory_space_constraint(s, pltpu.HBM) for s in srcs],
      *[pltpu.with_memory_space_constraint(l, pltpu.HBM) for l in lands], *after)
    handles = [(res[ci], res[n + ci], res[2 * n + ci], res[3 * n + ci]) for ci in range(n)]
    return handles, res[-1]


def _exchange_wait(handle, after, name):
    send_sem, recv_sem, src_thru, land_thru = handle

    def body(src_ref, land_ref, send_ref, recv_ref, after_ref, src_out, land_out):
        seven = land_ref.at[pl.ds(0, N_DEV - 1)]
        copies = pltpu.make_async_remote_copy(
            src_ref=seven, dst_ref=seven, send_sem=send_ref, recv_sem=recv_ref,
            device_id=(lax.axis_index("x"), lax.axis_index("y"), lax.axis_index("c")), device_id_type=_MESH_ID)
        copies.wait_send()
        copies.wait_recv()

    return pl.pallas_call(
        body, name=name,
        out_shape=(pltpu.HBM(src_thru.shape, src_thru.dtype), pltpu.HBM(land_thru.shape, land_thru.dtype)),
        in_specs=(_HBM, _HBM, _SEM, _SEM, _ANY), out_specs=(_HBM, _HBM), input_output_aliases={0: 0, 1: 1},
        compiler_params=pltpu.CompilerParams(has_side_effects=_EFFECT),
    )(src_thru, land_thru, send_sem, recv_sem, after)[1]


def _own_slot(land, own, me):
    return lax.dynamic_update_index_in_dim(land, own, me, axis=0)


def _sum_slots(slots, name):
    n, r, w = slots.shape
    tr = _tile(r, 128, WIRE_ROW_ALIGN)

    def body(s_ref, o_ref):
        acc = s_ref[0].astype(F32)
        for d in range(1, n):
            acc = acc + s_ref[d].astype(F32)
        o_ref[...] = acc

    return pl.pallas_call(
        body, name=name, grid=(r // tr,),
        in_specs=[pl.BlockSpec((n, tr, w), lambda i: (0, i, 0))],
        out_specs=pl.BlockSpec((tr, w), lambda i: (i, 0)),
        out_shape=jax.ShapeDtypeStruct((r, w), F32),
        compiler_params=_cparams(("parallel",), 2 * _nbytes((n, tr, w), slots.dtype) + 4 * _nbytes((tr, w), F32)),
    )(slots)


def _adamw(w, g, m, v, name):
    r, c = w.shape
    tr = _tile(r, 512, 8)

    def body(w_ref, g_ref, m_ref, v_ref, d_ref, mo_ref, vo_ref):
        gv = g_ref[...]
        m_new = ADAM_B1 * m_ref[...] + (1.0 - ADAM_B1) * gv
        v_new = ADAM_B2 * v_ref[...] + (1.0 - ADAM_B2) * (gv * gv)
        m_hat = m_new / (1.0 - ADAM_B1 ** ADAM_STEP)
        v_hat = v_new / (1.0 - ADAM_B2 ** ADAM_STEP)
        d_ref[...] = -ADAM_LR * (m_hat / (jnp.sqrt(v_hat) + ADAM_EPS) + ADAM_WD * w_ref[...])
        mo_ref[...] = m_new
        vo_ref[...] = v_new

    blk = pl.BlockSpec((tr, c), lambda i: (i, 0))
    shp = jax.ShapeDtypeStruct((r, c), F32)
    return pl.pallas_call(
        body, name=name, grid=(r // tr,), in_specs=[blk] * 4, out_specs=[blk] * 3, out_shape=[shp] * 3,
        compiler_params=_cparams(("parallel",), 16 * _nbytes((tr, _round_up(c, LANES)), F32)),
    )(w, g, m, v)


def _pack_rows(parts, width, dtype, row_align):
    rows, spans, off = [], [], 0
    for p in parts:
        flat = p.reshape(-1).astype(dtype)
        n_rows = _round_up(-(-flat.shape[0] // width), row_align)
        flat = jnp.pad(flat, (0, n_rows * width - flat.shape[0]))
        rows.append(flat.reshape(n_rows, width))
        spans.append((off, n_rows))
        off += n_rows
    return jnp.concatenate(rows, axis=0), spans


def _unpack_rows(mat, span, shape):
    off, n_rows = span
    n = 1
    for s in shape:
        n *= s
    return mat[..., off:off + n_rows, :].reshape(mat.shape[:-2] + (-1,))[..., :n].reshape(mat.shape[:-2] + tuple(shape))


def _block_diag(w, size):
    n, b, _ = w.shape
    eye = jnp.eye(n, dtype=w.dtype)
    dense = (w[:, :, None, :] * eye[:, None, :, None]).reshape(n * b, n * b)
    return jnp.pad(dense, ((0, size - n * b), (0, size - n * b)))


def _diag_blocks(dense, n, b):
    return jnp.stack([dense[k * b:(k + 1) * b, k * b:(k + 1) * b] for k in range(n)])


def _pad_rows(a, rows):
    return jnp.pad(a, ((0, rows - a.shape[0]), (0, 0)))


def _pad_cols(a, cols):
    return jnp.pad(a, ((0, 0), (0, cols - a.shape[1])))


def _train_step(a):
    x = a["x"][0]
    target = a["loss_target"][0]
    s_dim, d = x.shape
    n_layers = a["ffn1_pre_g"].shape[0]
    f_shard = a["ffn1_w_gate"].shape[2]
    c_shard = a["rg_conv_b"].shape[1]
    c_dim = c_shard * N_DEV
    cp = _round_up(c_dim, LANES)
    conv_width = a["rg_conv_w"].shape[1]
    n_blocks, lru_block = a["rg_w_a"].shape[1], a["rg_w_a"].shape[2]
    d_attn = a["attn_w_q"].shape[2]
    n_heads = a["b_fgate"].shape[0]
    d_head = d_attn // n_heads
    attn_scale = d_head ** -0.5
    assert conv_width < 8 and n_heads <= LANES and n_layers == 2
    assert d_attn == d
    me = 4 * lax.axis_index("x") + 2 * lax.axis_index("y") + lax.axis_index("c")

    shard = {"rg_w_in": a["rg_w_in"][0].T, "rg_w_out": a["rg_w_out"][0], "w_kv": a["w_kv"].T,
             "attn_w_q": a["attn_w_q"][0], "attn_w_o": a["attn_w_o"][0]}
    for l in range(n_layers):
        for f in ("ffn1", "ffn2"):
            shard[(f, "gate", l)] = a[f + "_w_gate"][l].T
            shard[(f, "up", l)] = a[f + "_w_up"][l].T
            shard[(f, "down", l)] = a[f + "_w_down"][l]

    def ffn_names(f, l):
        return [(f, "gate", l), (f, "up", l), (f, "down", l)]

    def chunk_layout(names):
        spans, off = [], 0
        for nm in names:
            spans.append((nm, off, shard[nm].shape[0]))
            off += _round_up(shard[nm].shape[0], WIRE_ROW_ALIGN)
        return spans, off

    def pack_chunk(names, parts):
        return jnp.concatenate(
            [_pad_rows(parts[nm].astype(WIRE_DTYPE), _round_up(parts[nm].shape[0], WIRE_ROW_ALIGN)) for nm in names], axis=0)

    full = {}

    def unpack_chunk(names, gathered):
        for nm, o, n_rows in chunk_layout(names)[0]:
            full[nm] = gathered[:, o:o + n_rows, :].reshape(N_DEV * n_rows, d)

    fwd_chunks = [ffn_names("ffn1", 0)[:2], ffn_names("ffn1", 0)[2:], ["rg_w_in"], ["rg_w_out"],
                  ffn_names("ffn2", 0) + ["w_kv"], ffn_names("ffn1", 1) + ["attn_w_q", "attn_w_o"], ffn_names("ffn2", 1)]
    fwd_packs = [pack_chunk(names, shard) for names in fwd_chunks]
    unpack_chunk(fwd_chunks[0], _all_gather(fwd_packs[0], "gather_weights_first"))

    small_parts = [a["rg_conv_w"][0], a["rg_conv_b"][0], a["rg_b_a"][0], a["rg_b_x"][0], a["rg_lambda"][0], a["w_fgate"]]
    small_pack, small_spans = _pack_rows(small_parts, d, F32, 8)
    small_all = _all_gather(small_pack, "gather_small")
    fwd_handles, fwd_token = _exchange_start(fwd_packs[1:], False, [full[("ffn1", "up", 0)], small_all],
                                             "gather_weights_start")

    def land_weights(n, after):
        land = _exchange_wait(fwd_handles[n - 1], after, f"gather_weights_wait_{n}")
        unpack_chunk(fwd_chunks[n], _own_slot(land, fwd_packs[n], me))

    sm = [_unpack_rows(small_all, sp, p.shape) for sp, p in zip(small_spans, small_parts)]
    conv_w = jnp.moveaxis(sm[0], 0, 1).reshape(conv_width, c_dim)
    conv_b, b_a, b_x, lam = (v.reshape(1, c_dim) for v in sm[1:5])
    w_f = sm[5].reshape(d, n_heads)

    pconv = _pad_rows(_pad_cols(jnp.concatenate([conv_w, conv_b], axis=0), cp), 8)
    pvec = _pad_rows(_pad_cols(jnp.concatenate([b_a, b_x, lam], axis=0), cp), 8)
    wa_dense = _block_diag(a["rg_w_a"][0], cp).astype(MXU_DTYPE)
    wx_dense = _block_diag(a["rg_w_x"][0], cp).astype(MXU_DTYPE)
    wax = jnp.concatenate([wa_dense, wx_dense], axis=1)
    w_f_t = _pad_rows(w_f.T.astype(MXU_DTYPE), LANES)
    b_f = _pad_rows(_pad_cols(a["b_fgate"].reshape(1, n_heads), LANES), 8)

    def gain(name, l):
        return a[name][l].reshape(1, d)

    def ffn_fwd(h, f, l, after=None, down_chunk=None):
        xn = _rms_fwd(h, gain(f + "_pre_g", l), f"{f}_{l}_pre_norm", after)
        g, u, act = _ffn_up(xn, full[(f, "gate", l)], full[(f, "up", l)], f"{f}_{l}_up")
        if down_chunk is not None:
            land_weights(down_chunk, act)
        fo, h_new = _mm_rms_res(act, full[(f, "down", l)], h, gain(f + "_post_g", l), 0.5, f"{f}_{l}_down")
        return h_new, (h, xn, g, u, act, fo)

    h0 = x
    h0a, sv_f1_0 = ffn_fwd(h0, "ffn1", 0, fwd_token, down_chunk=1)
    land_weights(2, h0a)
    w_in_gate = _pad_rows(full["rg_w_in"][:c_dim], cp)
    w_in_rec = _pad_rows(full["rg_w_in"][c_dim:], cp)
    w_in_t = jnp.concatenate([w_in_gate, w_in_rec], axis=0)
    hn_rg = _rms_fwd(h0a, gain("mix_pre_g", 0), "rg_pre_norm")
    gx = _mm([(hn_rg, w_in_t)], "nt", F32, "rg_in_proj")
    rec = _conv_fwd(gx, pconv, conv_width, "rg_conv")
    gates = _mm([(rec, wax)], "nn", F32, "rg_gate_proj")
    h_rec, y_rg = _scan_fwd(gx, rec, gates, pvec, "rg_scan")
    land_weights(3, y_rg)
    w_out = _pad_rows(full["rg_w_out"], cp)
    m_rg, h0b = _mm_rms_res(y_rg, w_out, h0a, gain("mix_post_g", 0), 1.0, "rg_out_proj")
    land_weights(4, h0b)
    h1, sv_f2_0 = ffn_fwd(h0b, "ffn2", 0)
    hn_kv = _rms_fwd(h1, a["kv_norm_g"].reshape(1, d), "kv_norm")
    kv = _mm([(hn_kv, full["w_kv"])], "nt", MXU_DTYPE, "kv_proj")
    fpre = _mm([(hn_kv, w_f_t)], "nt", F32, "fgate_proj")
    c_cum = _fgate_fwd(fpre, b_f, "fgate_cumsum")
    c_heads = c_cum[:, :n_heads].T
    c_col, c_row = c_heads[:, :, None], c_heads[:, None, :]
    land_weights(5, c_cum)
    h1a, sv_f1_1 = ffn_fwd(h1, "ffn1", 1)
    hn_at = _rms_fwd(h1a, gain("mix_pre_g", 1), "attn_pre_norm")
    q_s = _mm([(hn_at, full["attn_w_q"])], "nn", MXU_DTYPE, "q_proj", out_scale=attn_scale)
    o2, lse = _pair_attn_fwd(q_s, kv, kv[:, d_attn:].T, c_col, c_row, "attn_fwd")
    m_at, h1b = _mm_rms_res(o2, full["attn_w_o"], h1a, gain("mix_post_g", 1), 1.0, "attn_out_proj")
    land_weights(6, h1b)
    y, sv_f2_1 = ffn_fwd(h1b, "ffn2", 1)
    dy, loss_part = _loss_head(y, target, "loss_head")

    grads_big = {}
    grads_rep = {}

    bwd_chunks = [ffn_names("ffn2", 1), ["attn_w_q", "attn_w_o"] + ffn_names("ffn1", 1),
                  ["w_kv"] + ffn_names("ffn2", 0), ["rg_w_in", "rg_w_out"], ffn_names("ffn1", 0)[2:],
                  ffn_names("ffn1", 0)[:2]]
    bwd_sends, bwd_handles = [], []

    def send_grads(after):
        n = len(bwd_sends)
        send = jnp.concatenate(
            [jnp.pad(grads_big[nm].reshape(N_DEV, n_rows, d), ((0, 0), (0, _round_up(n_rows, WIRE_ROW_ALIGN) - n_rows), (0, 0)))
             for nm, _, n_rows in chunk_layout(bwd_chunks[n])[0]], axis=1)
        handles, token = _exchange_start([send], True, [after], f"exchange_grads_start_{n}")
        bwd_sends.append(send)
        bwd_handles.append(handles[0])
        return token

    def ffn_bwd(dh_out, saved, f, l, after=None, send_now=False):
        h, xn, g, u, act, fo = saved
        df, d_post = _rms_bwd(fo, gain(f + "_post_g", l), [dh_out], None, 0.5, MXU_DTYPE, f"{f}_{l}_post_norm_bwd", after)
        dg, du = _ffn_act_bwd(df, full[(f, "down", l)], g, u, f"{f}_{l}_act_bwd")
        grads_big[(f, "down", l)] = _mm([(act, df)], "tn", WIRE_DTYPE, f"{f}_{l}_dw_down")
        sent = send_grads(df) if send_now else None
        grads_big[(f, "gate", l)] = _mm([(dg, xn)], "tn", WIRE_DTYPE, f"{f}_{l}_dw_gate", sent)
        grads_big[(f, "up", l)] = _mm([(du, xn)], "tn", WIRE_DTYPE, f"{f}_{l}_dw_up")
        sent = send_grads(df) if send_now else None
        dxn = _mm([(dg, full[(f, "gate", l)]), (du, full[(f, "up", l)])], "nn", F32, f"{f}_{l}_dx", sent)
        dh_in, d_pre = _rms_bwd(h, gain(f + "_pre_g", l), [dxn], dh_out, 1.0, F32, f"{f}_{l}_pre_norm_bwd")
        grads_rep[(f + "_post_g", l)] = d_post
        grads_rep[(f + "_pre_g", l)] = d_pre
        return dh_in

    dh = ffn_bwd(dy, sv_f2_1, "ffn2", 1)
    token = send_grads(dh)
    dm, d_post = _rms_bwd(m_at, gain("mix_post_g", 1), [dh], None, 1.0, MXU_DTYPE, "attn_post_norm_bwd", token)
    grads_rep[("mix_post_g", 1)] = d_post
    do2 = _mm([(dm, full["attn_w_o"])], "nt", F32, "attn_out_proj_dx")
    grads_big["attn_w_o"] = _mm([(o2, dm)], "tn", WIRE_DTYPE, "attn_out_proj_dw")
    delta = _attn_delta(do2, o2, n_heads, "attn_delta")
    dq2, dc_q, dk2, dv2, dc_k = _pair_attn_bwd(q_s, kv, c_col, c_row, lse, delta, do2, attn_scale, "attn_bwd")
    dc_heads = dc_q[:, 0, :] + dc_k[:, :, 0]
    dhn = _mm([(dq2, full["attn_w_q"])], "nt", F32, "q_proj_dx")
    grads_big["attn_w_q"] = _mm([(hn_at, dq2)], "tn", WIRE_DTYPE, "q_proj_dw")
    dh, d_pre = _rms_bwd(h1a, gain("mix_pre_g", 1), [dhn], dh, 1.0, F32, "attn_pre_norm_bwd")
    grads_rep[("mix_pre_g", 1)] = d_pre
    dh = ffn_bwd(dh, sv_f1_1, "ffn1", 1)
    token = send_grads(dh)
    dc_cum = _pad_cols(dc_heads.T, LANES)
    dfpre, db_f = _fgate_bwd(dc_cum, fpre, b_f, "fgate_cumsum_bwd")
    dhn_kv = _mm([(dk2, full["w_kv"][:d_attn]), (dv2, full["w_kv"][d_attn:])], "nn", F32, "kv_proj_dx")
    dhn_f = _mm([(dfpre, w_f_t)], "nn", F32, "fgate_proj_dx")
    grads_big["w_kv"] = jnp.concatenate([_mm([(dk2, hn_kv)], "tn", WIRE_DTYPE, "kv_proj_dw_k"),
                                         _mm([(dv2, hn_kv)], "tn", WIRE_DTYPE, "kv_proj_dw_v")], axis=0)
    dw_f_t = _mm([(dfpre, hn_kv)], "tn", F32, "fgate_proj_dw")
    dh, d_kvg = _rms_bwd(h1, a["kv_norm_g"].reshape(1, d), [dhn_kv, dhn_f], dh, 1.0, F32, "kv_norm_bwd", token)
    dh = ffn_bwd(dh, sv_f2_0, "ffn2", 0)
    token = send_grads(dh)
    dm, d_post = _rms_bwd(m_rg, gain("mix_post_g", 0), [dh], None, 1.0, MXU_DTYPE, "rg_post_norm_bwd", token)
    grads_rep[("mix_post_g", 0)] = d_post
    dy_rg = _mm([(dm, w_out)], "nt", F32, "rg_out_proj_dx")
    dw_out = _mm([(y_rg, dm)], "tn", WIRE_DTYPE, "rg_out_proj_dw")
    dgate, dra, dia, drec1, dpvec = _scan_bwd(dy_rg, gx, h_rec, rec, gates, pvec, "rg_scan_bwd")
    drec2 = _mm([(dra, wa_dense), (dia, wx_dense)], "nt", F32, "rg_gate_proj_dx")
    dwa_dense = _mm([(rec, dra)], "tn", F32, "rg_gate_proj_dwa")
    dwx_dense = _mm([(rec, dia)], "tn", F32, "rg_gate_proj_dwx")
    drec0, dpconv = _conv_bwd(drec1, drec2, gx, pconv, conv_width, "rg_conv_bwd")
    dhn = _mm([(dgate, w_in_gate), (drec0, w_in_rec)], "nn", F32, "rg_in_proj_dx")
    dw_in_gate = _mm([(dgate, hn_rg)], "tn", WIRE_DTYPE, "rg_in_proj_dw_gate")
    dw_in_rec = _mm([(drec0, hn_rg)], "tn", WIRE_DTYPE, "rg_in_proj_dw_rec")
    dh, d_pre = _rms_bwd(h0a, gain("mix_pre_g", 0), [dhn], dh, 1.0, F32, "rg_pre_norm_bwd")
    grads_rep[("mix_pre_g", 0)] = d_pre
    grads_big["rg_w_in"] = jnp.concatenate([dw_in_gate[:c_dim], dw_in_rec[:c_dim]], axis=0)
    grads_big["rg_w_out"] = dw_out[:c_dim]
    token = send_grads(dh)
    grad_x = ffn_bwd(dh, sv_f1_0, "ffn1", 0, token, send_now=True)

    g_shard = {}

    def land_grads(n, after):
        land = _exchange_wait(bwd_handles[n], after, f"exchange_grads_wait_{n}")
        own = lax.dynamic_index_in_dim(bwd_sends[n], me, axis=0, keepdims=False)
        g_chunk = _sum_slots(_own_slot(land, own, me), f"sum_weight_grads_{n}")
        for nm, o, n_rows in chunk_layout(bwd_chunks[n])[0]:
            g_shard[nm] = g_chunk[o:o + n_rows]

    for n in range(len(bwd_chunks) - 2):
        land_grads(n, grad_x)

    def gain_grad(name):
        return jnp.concatenate([grads_rep[(name, l)] for l in range(n_layers)], axis=0)

    rep_names = ["ffn1_pre_g", "ffn1_post_g", "mix_pre_g", "mix_post_g", "ffn2_pre_g", "ffn2_post_g"]
    rep_parts = [gain_grad(nm) for nm in rep_names]
    rep_names += ["kv_norm_g", "b_fgate", "rg_w_a", "rg_w_x", "rg_conv_w", "rg_conv_b", "rg_b_a", "rg_b_x", "rg_lambda", "w_fgate"]
    rep_parts += [
        d_kvg, db_f[0, :n_heads],
        _diag_blocks(dwa_dense, n_blocks, lru_block), _diag_blocks(dwx_dense, n_blocks, lru_block),
        dpconv[:conv_width, :c_dim], dpconv[conv_width, :c_dim],
        dpvec[0, :c_dim], dpvec[1, :c_dim], dpvec[2, :c_dim],
        dw_f_t[:n_heads].T]
    rep_pack, rep_spans = _pack_rows(rep_parts, d, F32, WIRE_ROW_ALIGN)
    rep_sum = _sum_slots(_all_gather(rep_pack, "gather_small_grads"), "sum_small_grads")
    g_rep = {nm: _unpack_rows(rep_sum, sp, p.shape) for nm, sp, p in zip(rep_names, rep_spans, rep_parts)}

    def my_cols(full_grad, n):
        return lax.dynamic_slice_in_dim(full_grad, me * n, n, axis=full_grad.ndim - 1)

    def ffn_grads(f):
        grad[f + "_w_gate"] = jnp.stack([g_shard[(f, "gate", l)].T for l in range(n_layers)])
        grad[f + "_w_up"] = jnp.stack([g_shard[(f, "up", l)].T for l in range(n_layers)])
        grad[f + "_w_down"] = jnp.stack([g_shard[(f, "down", l)] for l in range(n_layers)])

    grad = {}
    for nm in ("ffn1_pre_g", "ffn1_post_g", "mix_pre_g", "mix_post_g", "ffn2_pre_g", "ffn2_post_g"):
        grad[nm] = g_rep[nm]
    ffn_grads("ffn2")
    grad["rg_w_in"] = g_shard["rg_w_in"].T[None]
    grad["rg_conv_w"] = my_cols(g_rep["rg_conv_w"], c_shard)[None]
    for nm in ("rg_conv_b", "rg_b_a", "rg_b_x", "rg_lambda"):
        grad[nm] = my_cols(g_rep[nm], c_shard)[None]
    grad["rg_w_a"] = g_rep["rg_w_a"][None]
    grad["rg_w_x"] = g_rep["rg_w_x"][None]
    grad["rg_w_out"] = g_shard["rg_w_out"][None]
    grad["kv_norm_g"] = g_rep["kv_norm_g"].reshape(d)
    grad["w_kv"] = g_shard["w_kv"].T
    grad["w_fgate"] = lax.dynamic_slice_in_dim(g_rep["w_fgate"], me * (d // N_DEV), d // N_DEV, axis=0)
    grad["b_fgate"] = g_rep["b_fgate"]
    grad["attn_w_q"] = g_shard["attn_w_q"][None]
    grad["attn_w_o"] = g_shard["attn_w_o"][None]

    delta, new_m, new_v = {}, {}, {}

    def adamw(nm):
        w = a[nm]
        shape = w.shape
        two_d = (1, shape[0]) if w.ndim == 1 else (-1, shape[-1])
        ops = [pltpu.with_memory_space_constraint(t.reshape(two_d), pltpu.HBM)
               for t in (w, grad[nm], a["m_" + nm], a["v_" + nm])]
        dl, mo, vo = _adamw(*ops, "adamw_" + nm)
        delta[nm], new_m[nm], new_v[nm] = dl.reshape(shape), mo.reshape(shape), vo.reshape(shape)
        grad[nm] = grad[nm].reshape(shape)

    last_names = ("ffn1_w_gate", "ffn1_w_up", "ffn1_w_down")
    for nm in WEIGHT_NAMES:
        if nm not in last_names:
            adamw(nm)
    land_grads(len(bwd_chunks) - 2, delta["attn_w_o"])
    land_grads(len(bwd_chunks) - 1, delta["attn_w_o"])
    ffn_grads("ffn1")
    for nm in last_names:
        adamw(nm)

    loss = lax.psum(loss_part[0, 0], AXES)
    return (loss, grad_x[None], *[grad[n] for n in WEIGHT_NAMES], *[delta[n] for n in WEIGHT_NAMES],
            *[new_m[n] for n in WEIGHT_NAMES], *[new_v[n] for n in WEIGHT_NAMES])


def kernel(x, ffn1_pre_g, ffn1_w_gate, ffn1_w_up, ffn1_w_down, ffn1_post_g, mix_pre_g, mix_post_g, ffn2_pre_g, ffn2_w_gate, ffn2_w_up, ffn2_w_down, ffn2_post_g, rg_w_in, rg_conv_w, rg_conv_b, rg_w_a, rg_b_a, rg_w_x, rg_b_x, rg_lambda, rg_w_out, kv_norm_g, w_kv, w_fgate, b_fgate, attn_w_q, attn_w_o, loss_target, m_ffn1_pre_g, m_ffn1_w_gate, m_ffn1_w_up, m_ffn1_w_down, m_ffn1_post_g, m_mix_pre_g, m_mix_post_g, m_ffn2_pre_g, m_ffn2_w_gate, m_ffn2_w_up, m_ffn2_w_down, m_ffn2_post_g, m_rg_w_in, m_rg_conv_w, m_rg_conv_b, m_rg_w_a, m_rg_b_a, m_rg_w_x, m_rg_b_x, m_rg_lambda, m_rg_w_out, m_kv_norm_g, m_w_kv, m_w_fgate, m_b_fgate, m_attn_w_q, m_attn_w_o, v_ffn1_pre_g, v_ffn1_w_gate, v_ffn1_w_up, v_ffn1_w_down, v_ffn1_post_g, v_mix_pre_g, v_mix_post_g, v_ffn2_pre_g, v_ffn2_w_gate, v_ffn2_w_up, v_ffn2_w_down, v_ffn2_post_g, v_rg_w_in, v_rg_conv_w, v_rg_conv_b, v_rg_w_a, v_rg_b_a, v_rg_w_x, v_rg_b_x, v_rg_lambda, v_rg_w_out, v_kv_norm_g, v_w_kv, v_w_fgate, v_b_fgate, v_attn_w_q, v_attn_w_o):
    return _train_step(dict(locals()))
```

```python
import functools

import jax
import jax.numpy as jnp
from jax import lax
from jax.experimental import pallas as pl
from jax.experimental.pallas import tpu as pltpu

F32 = jnp.float32
MXU_DTYPE = jnp.bfloat16
WIRE_DTYPE = jnp.bfloat16
N_DEV = 8
AXES = ("x", "y", "c")
LANES = 128
WIRE_ROW_ALIGN = 16
VMEM_LIMIT_MIN = 32 * 2 ** 20
VMEM_LIMIT_MAX = 56 * 2 ** 20

RMS_EPS = 1e-6
LRU_C = 8.0
ADAM_LR, ADAM_B1, ADAM_B2, ADAM_EPS, ADAM_WD, ADAM_STEP = 0.001, 0.9, 0.999, 1e-08, 0.01, 10

WEIGHT_NAMES = (
    "ffn1_pre_g", "ffn1_w_gate", "ffn1_w_up", "ffn1_w_down", "ffn1_post_g", "mix_pre_g", "mix_post_g",
    "ffn2_pre_g", "ffn2_w_gate", "ffn2_w_up", "ffn2_w_down", "ffn2_post_g", "rg_w_in", "rg_conv_w",
    "rg_conv_b", "rg_w_a", "rg_b_a", "rg_w_x", "rg_b_x", "rg_lambda", "rg_w_out", "kv_norm_g", "w_kv",
    "w_fgate", "b_fgate", "attn_w_q", "attn_w_o")


def _round_up(n, m):
    return (n + m - 1) // m * m


def _tile(dim, target, align=LANES):
    if dim <= target:
        return dim
    best = None
    t = align
    while t <= target:
        if dim % t == 0:
            best = t
        t += align
    return dim if best is None else best


def _cparams(semantics, vmem_estimate):
    limit = min(VMEM_LIMIT_MAX, max(VMEM_LIMIT_MIN, 2 * int(vmem_estimate)))
    return pltpu.CompilerParams(dimension_semantics=semantics, vmem_limit_bytes=limit)


def _nbytes(shape, dtype):
    n = 1
    for s in shape:
        n *= s
    return n * jnp.dtype(dtype).itemsize


def _sigmoid(x):
    return jax.nn.sigmoid(x)


def _softplus(x):
    return jnp.maximum(x, 0.0) + jnp.log1p(jnp.exp(-jnp.abs(x)))


def _expm1(x):
    series = x * (1.0 + x * (0.5 + x * (1.0 / 6.0 + x * (1.0 / 24.0 + x * (1.0 / 120.0)))))
    return jnp.where(jnp.abs(x) < 0.25, series, jnp.exp(x) - 1.0)


_GELU_C = 0.7978845608028654
_GELU_A = 0.044715


def _gelu(x):
    return 0.5 * x * (1.0 + jnp.tanh(_GELU_C * (x + _GELU_A * x * x * x)))


def _gelu_grad(x):
    t = jnp.tanh(_GELU_C * (x + _GELU_A * x * x * x))
    return 0.5 * (1.0 + t) + 0.5 * x * (1.0 - t * t) * _GELU_C * (1.0 + 3.0 * _GELU_A * x * x)


_DOT_DIMS = {"nn": ((1,), (0,)), "nt": ((1,), (1,)), "tn": ((0,), (0,))}


def _dot(a, b, mode):
    return lax.dot_general(a.astype(MXU_DTYPE), b.astype(MXU_DTYPE), (_DOT_DIMS[mode], ((), ())),
                           preferred_element_type=F32)


def _mm(pairs, mode, out_dtype, name, after=None, out_scale=None):
    a0, b0 = pairs[0]
    if mode == "tn":
        k_dim, m_dim = a0.shape
        n_dim = b0.shape[1]
    else:
        m_dim, k_dim = a0.shape
        n_dim = b0.shape[0] if mode == "nt" else b0.shape[1]
    for a, b in pairs:
        assert a.shape == a0.shape and b.shape == b0.shape
    tm = _tile(m_dim, 1408 if mode == "tn" else 512)
    whole = 1408 if mode == "tn" else 2816
    tn = _tile(n_dim, whole)
    tk = _tile(k_dim, whole)
    nk = k_dim // tk
    n_pairs = len(pairs)

    if mode == "tn":
        a_spec = pl.BlockSpec((tk, tm), lambda i, j, k: (k, i))
    else:
        a_spec = pl.BlockSpec((tm, tk), lambda i, j, k: (i, k))
    if mode == "nt":
        b_spec = pl.BlockSpec((tn, tk), lambda i, j, k: (j, k))
    else:
        b_spec = pl.BlockSpec((tk, tn), lambda i, j, k: (k, j))

    order = [] if after is None else [after]

    def body(*refs):
        ins, o_ref, acc = refs[:2 * n_pairs], refs[-2], refs[-1]
        k = pl.program_id(2)

        @pl.when(k == 0)
        def _():
            acc[...] = jnp.zeros_like(acc)

        s = acc[...]
        for p in range(n_pairs):
            s = s + _dot(ins[2 * p][...], ins[2 * p + 1][...], mode)
        acc[...] = s

        @pl.when(k == nk - 1)
        def _():
            r = acc[...] if out_scale is None else acc[...] * out_scale
            o_ref[...] = r.astype(out_dtype)

    est = (2 * n_pairs * (_nbytes((tm, tk), a0.dtype) + _nbytes((tk, tn), b0.dtype))
           + 2 * _nbytes((tm, tn), out_dtype) + 2 * _nbytes((tm, tn), F32))
    flat = [t for ab in pairs for t in ab]
    return pl.pallas_call(
        body, name=name, grid=(m_dim // tm, n_dim // tn, nk),
        in_specs=[a_spec, b_spec] * n_pairs + [_ANY] * len(order),
        out_specs=pl.BlockSpec((tm, tn), lambda i, j, k: (i, j)),
        out_shape=jax.ShapeDtypeStruct((m_dim, n_dim), out_dtype),
        scratch_shapes=[pltpu.VMEM((tm, tn), F32)],
        compiler_params=_cparams(("parallel", "parallel", "arbitrary"), est),
    )(*flat, *order)


_ANY = pl.BlockSpec(memory_space=pl.ANY)


def _rms_fwd(x, gain, name, after=None):
    s_dim, d = x.shape
    tm = _tile(s_dim, 512, 8)

    def body(*refs):
        x_ref, g_ref, o_ref = refs[0], refs[1], refs[-1]
        v = x_ref[...]
        r = lax.rsqrt(jnp.mean(v * v, axis=-1, keepdims=True) + RMS_EPS)
        o_ref[...] = (v * r * g_ref[...]).astype(MXU_DTYPE)

    order = [] if after is None else [after]
    return pl.pallas_call(
        body, name=name, grid=(s_dim // tm,),
        in_specs=[pl.BlockSpec((tm, d), lambda i: (i, 0)), pl.BlockSpec((1, d), lambda i: (0, 0))] + [_ANY] * len(order),
        out_specs=pl.BlockSpec((tm, d), lambda i: (i, 0)),
        out_shape=jax.ShapeDtypeStruct((s_dim, d), MXU_DTYPE),
        compiler_params=_cparams(("parallel",), 6 * _nbytes((tm, d), F32)),
    )(x, gain, *order)


def _rms_bwd(x, gain, dys, res, scale, out_dtype, name, after=None):
    s_dim, d = x.shape
    tm = _tile(s_dim, 512, 8)
    n_dy = len(dys)
    has_res = res is not None
    order = [] if after is None else [after]

    def body(*refs):
        x_ref, g_ref = refs[0], refs[1]
        dy_refs = refs[2:2 + n_dy]
        res_ref = refs[2 + n_dy] if has_res else None
        dx_ref, dg_ref = refs[-2], refs[-1]

        @pl.when(pl.program_id(0) == 0)
        def _():
            dg_ref[...] = jnp.zeros_like(dg_ref)

        v = x_ref[...]
        r = lax.rsqrt(jnp.mean(v * v, axis=-1, keepdims=True) + RMS_EPS)
        xh = v * r
        dy = dy_refs[0][...].astype(F32)
        for extra in dy_refs[1:]:
            dy = dy + extra[...].astype(F32)
        gd = dy * g_ref[...]
        dx = scale * r * (gd - xh * jnp.mean(gd * xh, axis=-1, keepdims=True))
        if has_res:
            dx = dx + res_ref[...]
        dx_ref[...] = dx.astype(out_dtype)
        dg_ref[...] += scale * jnp.sum(dy * xh, axis=0, keepdims=True)

    row = pl.BlockSpec((tm, d), lambda i: (i, 0))
    vec = pl.BlockSpec((1, d), lambda i: (0, 0))
    ops = [x, gain] + list(dys) + ([res] if has_res else [])
    return pl.pallas_call(
        body, name=name, grid=(s_dim // tm,),
        in_specs=[row, vec] + [row] * (n_dy + int(has_res)) + [_ANY] * len(order),
        out_specs=[row, vec],
        out_shape=[jax.ShapeDtypeStruct((s_dim, d), out_dtype), jax.ShapeDtypeStruct((1, d), F32)],
        compiler_params=_cparams(("arbitrary",), (2 * len(ops) + 6) * _nbytes((tm, d), F32)),
    )(*ops, *order)


def _mm_rms_res(a, b, h, gain, scale, name):
    s_dim, k_dim = a.shape
    d = b.shape[1]
    tm = _tile(s_dim, 512, 8)
    tk = _tile(k_dim, 2816)
    nk = k_dim // tk

    def body(a_ref, b_ref, h_ref, g_ref, f_ref, o_ref, acc):
        k = pl.program_id(1)

        @pl.when(k == 0)
        def _():
            acc[...] = jnp.zeros_like(acc)

        acc[...] += _dot(a_ref[...], b_ref[...], "nn")

        @pl.when(k == nk - 1)
        def _():
            f = acc[...]
            r = lax.rsqrt(jnp.mean(f * f, axis=-1, keepdims=True) + RMS_EPS)
            f_ref[...] = f
            o_ref[...] = h_ref[...] + scale * (f * r * g_ref[...])

    row = pl.BlockSpec((tm, d), lambda i, k: (i, 0))
    est = (2 * (_nbytes((tm, tk), a.dtype) + _nbytes((tk, d), b.dtype)) + 8 * _nbytes((tm, d), F32))
    return pl.pallas_call(
        body, name=name, grid=(s_dim // tm, nk),
        in_specs=[pl.BlockSpec((tm, tk), lambda i, k: (i, k)), pl.BlockSpec((tk, d), lambda i, k: (k, 0)),
                  row, pl.BlockSpec((1, d), lambda i, k: (0, 0))],
        out_specs=[row, row],
        out_shape=[jax.ShapeDtypeStruct((s_dim, d), F32), jax.ShapeDtypeStruct((s_dim, d), F32)],
        scratch_shapes=[pltpu.VMEM((tm, d), F32)],
        compiler_params=_cparams(("parallel", "arbitrary"), est),
    )(a, b, h, gain)


def _ffn_up(xn, wg_t, wu_t, name):
    s_dim, d = xn.shape
    f_dim = wg_t.shape[0]
    tm = _tile(s_dim, 2048, 8)
    tf = _tile(f_dim, 256)

    def body(x_ref, wg_ref, wu_ref, g_ref, u_ref, a_ref):
        x = x_ref[...]
        g = _dot(x, wg_ref[...], "nt")
        u = _dot(x, wu_ref[...], "nt")
        g_ref[...] = g.astype(MXU_DTYPE)
        u_ref[...] = u.astype(MXU_DTYPE)
        a_ref[...] = (g * _sigmoid(g) * u).astype(MXU_DTYPE)

    w_spec = pl.BlockSpec((tf, d), lambda i, j: (j, 0))
    o_spec = pl.BlockSpec((tm, tf), lambda i, j: (i, j))
    o_shape = jax.ShapeDtypeStruct((s_dim, f_dim), MXU_DTYPE)
    est = 2 * _nbytes((tm, d), xn.dtype) + 4 * _nbytes((tf, d), wg_t.dtype) + 10 * _nbytes((tm, tf), F32)
    return pl.pallas_call(
        body, name=name, grid=(s_dim // tm, f_dim // tf),
        in_specs=[pl.BlockSpec((tm, d), lambda i, j: (i, 0)), w_spec, w_spec],
        out_specs=[o_spec, o_spec, o_spec], out_shape=[o_shape, o_shape, o_shape],
        compiler_params=_cparams(("parallel", "parallel"), est),
    )(xn, wg_t, wu_t)


def _ffn_act_bwd(df, wd, g, u, name):
    s_dim, d = df.shape
    f_dim = wd.shape[0]
    tm = _tile(s_dim, 2048, 8)
    tf = _tile(f_dim, 256)

    def body(df_ref, wd_ref, g_ref, u_ref, dg_ref, du_ref):
        dh = _dot(df_ref[...], wd_ref[...], "nt")
        gv = g_ref[...].astype(F32)
        uv = u_ref[...].astype(F32)
        sg = _sigmoid(gv)
        dg_ref[...] = (dh * uv * (sg * (1.0 + gv * (1.0 - sg)))).astype(MXU_DTYPE)
        du_ref[...] = (dh * gv * sg).astype(MXU_DTYPE)

    t_spec = pl.BlockSpec((tm, tf), lambda i, j: (i, j))
    o_shape = jax.ShapeDtypeStruct((s_dim, f_dim), MXU_DTYPE)
    est = 2 * _nbytes((tm, d), df.dtype) + 2 * _nbytes((tf, d), wd.dtype) + 12 * _nbytes((tm, tf), F32)
    return pl.pallas_call(
        body, name=name, grid=(s_dim // tm, f_dim // tf),
        in_specs=[pl.BlockSpec((tm, d), lambda i, j: (i, 0)), pl.BlockSpec((tf, d), lambda i, j: (j, 0)),
                  t_spec, t_spec],
        out_specs=[t_spec, t_spec], out_shape=[o_shape, o_shape],
        compiler_params=_cparams(("parallel", "parallel"), est),
    )(df, wd, g, u)


def _loss_head(y, target, name):
    s_dim, d = y.shape
    tm = _tile(s_dim, 512, 8)
    nt = s_dim // tm

    def body(y_ref, t_ref, dy_ref, loss_ref, acc):
        i = pl.program_id(0)

        @pl.when(i == 0)
        def _():
            acc[...] = jnp.zeros_like(acc)

        e = y_ref[...] - t_ref[...]
        dy_ref[...] = e * (1.0 / d)
        acc[...] += jnp.sum(e * e, axis=0, keepdims=True)

        @pl.when(i == nt - 1)
        def _():
            loss_ref[...] = jnp.sum(acc[...], axis=1, keepdims=True) * (0.5 / d)

    row = pl.BlockSpec((tm, d), lambda i: (i, 0))
    return pl.pallas_call(
        body, name=name, grid=(nt,), in_specs=[row, row],
        out_specs=[row, pl.BlockSpec((1, 1), lambda i: (0, 0))],
        out_shape=[jax.ShapeDtypeStruct((s_dim, d), F32), jax.ShapeDtypeStruct((1, 1), F32)],
        scratch_shapes=[pltpu.VMEM((1, d), F32)],
        compiler_params=_cparams(("arbitrary",), 8 * _nbytes((tm, d), F32)),
    )(y, target)


def _shift_down(v, sh, row):
    if sh == 0:
        return v
    return jnp.where(row >= sh, pltpu.roll(v, sh, 0), 0.0)


def _shift_up(v, sh, row):
    if sh == 0:
        return v
    n = v.shape[0]
    return jnp.where(row < n - sh, pltpu.roll(v, n - sh, 0), 0.0)


def _conv_fwd(gx, pconv, width, name):
    s_dim, cp2 = gx.shape
    cp = cp2 // 2
    nc = cp // LANES

    def body(x_ref, p_ref, o_ref):
        x = x_ref[...]
        row = lax.broadcasted_iota(jnp.int32, x.shape, 0)
        y = jnp.zeros_like(x) + p_ref[pl.ds(width, 1), :]
        for k in range(width):
            y = y + p_ref[pl.ds(k, 1), :] * _shift_down(x, width - 1 - k, row)
        o_ref[...] = y

    return pl.pallas_call(
        body, name=name, grid=(nc,),
        in_specs=[pl.BlockSpec((s_dim, LANES), lambda j: (0, nc + j)), pl.BlockSpec((8, LANES), lambda j: (0, j))],
        out_specs=pl.BlockSpec((s_dim, LANES), lambda j: (0, j)),
        out_shape=jax.ShapeDtypeStruct((s_dim, cp), F32),
        compiler_params=_cparams(("parallel",), 10 * _nbytes((s_dim, LANES), F32)),
    )(gx, pconv)


def _conv_bwd(d1, d2, gx, pconv, width, name):
    s_dim, cp = d1.shape
    nc = cp // LANES

    def body(d1_ref, d2_ref, x_ref, p_ref, dx_ref, dp_ref):
        d = d1_ref[...] + d2_ref[...]
        x = x_ref[...]
        row = lax.broadcasted_iota(jnp.int32, x.shape, 0)
        dx = jnp.zeros_like(d)
        dp_ref[...] = jnp.zeros_like(dp_ref)
        for k in range(width):
            sh = width - 1 - k
            dx = dx + p_ref[pl.ds(k, 1), :] * _shift_up(d, sh, row)
            dp_ref[pl.ds(k, 1), :] = jnp.sum(d * _shift_down(x, sh, row), axis=0, keepdims=True)
        dp_ref[pl.ds(width, 1), :] = jnp.sum(d, axis=0, keepdims=True)
        dx_ref[...] = dx.astype(MXU_DTYPE)

    strip = pl.BlockSpec((s_dim, LANES), lambda j: (0, j))
    par = pl.BlockSpec((8, LANES), lambda j: (0, j))
    return pl.pallas_call(
        body, name=name, grid=(nc,),
        in_specs=[strip, strip, pl.BlockSpec((s_dim, LANES), lambda j: (0, nc + j)), par],
        out_specs=[strip, par],
        out_shape=[jax.ShapeDtypeStruct((s_dim, cp), MXU_DTYPE), jax.ShapeDtypeStruct((8, cp), F32)],
        compiler_params=_cparams(("parallel",), 14 * _nbytes((s_dim, LANES), F32)),
    )(d1, d2, gx, pconv)


def _lru_coeffs(ra, ia, p_ref):
    r = _sigmoid(ra + p_ref[pl.ds(0, 1), :])
    i = _sigmoid(ia + p_ref[pl.ds(1, 1), :])
    sp = _softplus(-p_ref[pl.ds(2, 1), :])
    log_a = -LRU_C * r * sp
    a = jnp.exp(log_a)
    mult = jnp.sqrt(-_expm1(2.0 * log_a))
    return r, i, sp, a, mult


def _scan_fwd(gx, rec, gates, pvec, name):
    s_dim, cp = rec.shape
    ts = _tile(s_dim, 256, 8)
    nt = s_dim // ts

    def body(gate_ref, rec_ref, ra_ref, ia_ref, p_ref, h_ref, y_ref, a_s, u_s, carry):
        @pl.when(pl.program_id(0) == 0)
        def _():
            carry[...] = jnp.zeros_like(carry)

        rec_v = rec_ref[...]
        _, i, _, a, mult = _lru_coeffs(ra_ref[...], ia_ref[...], p_ref)
        a_s[...] = a
        u_s[...] = mult * (i * rec_v)

        def step(t, h):
            h = a_s[pl.ds(t, 1), :] * h + u_s[pl.ds(t, 1), :]
            h_ref[pl.ds(t, 1), :] = h
            return h

        carry[pl.ds(0, 1), :] = lax.fori_loop(0, ts, step, carry[pl.ds(0, 1), :], unroll=8)
        y_ref[...] = (_gelu(gate_ref[...]) * h_ref[...]).astype(MXU_DTYPE)

    blk = pl.BlockSpec((ts, cp), lambda t: (t, 0))
    return pl.pallas_call(
        body, name=name, grid=(nt,),
        in_specs=[blk, blk, blk, pl.BlockSpec((ts, cp), lambda t: (t, 1)), pl.BlockSpec((8, cp), lambda t: (0, 0))],
        out_specs=[blk, blk],
        out_shape=[jax.ShapeDtypeStruct((s_dim, cp), F32), jax.ShapeDtypeStruct((s_dim, cp), MXU_DTYPE)],
        scratch_shapes=[pltpu.VMEM((ts, cp), F32), pltpu.VMEM((ts, cp), F32), pltpu.VMEM((8, cp), F32)],
        compiler_params=_cparams(("arbitrary",), 14 * _nbytes((ts, cp), F32)),
    )(gx, rec, gates, gates, pvec)


def _scan_bwd(dy, gx, hrec, rec, gates, pvec, name):
    s_dim, cp = rec.shape
    ts = _tile(s_dim, 128, 8)
    nt = s_dim // ts

    def body(dy_ref, gate_ref, h_ref, hp_ref, rec_ref, ra_ref, ia_ref, p_ref,
             dgate_ref, dra_ref, dia_ref, drec_ref, dp_ref, a_s, d_s, carry):
        t_id = pl.program_id(0)

        @pl.when(t_id == 0)
        def _():
            carry[...] = jnp.zeros_like(carry)
            dp_ref[...] = jnp.zeros_like(dp_ref)

        rec_v = rec_ref[...]
        r, i, sp, a, mult = _lru_coeffs(ra_ref[...], ia_ref[...], p_ref)
        gate = gate_ref[...]
        dyv = dy_ref[...]
        h = h_ref[...]
        dgate_ref[...] = (dyv * h * _gelu_grad(gate)).astype(MXU_DTYPE)
        a_s[...] = a
        d_s[...] = dyv * _gelu(gate)

        def step(k, c):
            t = ts - 1 - k
            d = d_s[pl.ds(t, 1), :] + c
            d_s[pl.ds(t, 1), :] = d
            return a_s[pl.ds(t, 1), :] * d

        carry[pl.ds(0, 1), :] = lax.fori_loop(0, ts, step, carry[pl.ds(0, 1), :], unroll=8)
        dh = d_s[...]
        row = lax.broadcasted_iota(jnp.int32, h.shape, 0)
        first = jnp.where(t_id == nt - 1, 0.0, 1.0) * hp_ref[pl.ds(7, 1), :]
        h_prev = jnp.where(row == 0, first, pltpu.roll(h, 1, 0))
        dix = dh * mult
        dla = dh * h_prev * a - dh * (i * rec_v) * (a * a) / mult
        dra = dla * (-LRU_C * sp) * r * (1.0 - r)
        dia = dix * rec_v * i * (1.0 - i)
        dra_ref[...] = dra.astype(MXU_DTYPE)
        dia_ref[...] = dia.astype(MXU_DTYPE)
        drec_ref[...] = dix * i
        dsp = jnp.sum(dla * (-LRU_C * r), axis=0, keepdims=True)
        dp_ref[pl.ds(0, 1), :] += jnp.sum(dra, axis=0, keepdims=True)
        dp_ref[pl.ds(1, 1), :] += jnp.sum(dia, axis=0, keepdims=True)
        dp_ref[pl.ds(2, 1), :] += dsp * (-_sigmoid(-p_ref[pl.ds(2, 1), :]))

    blk = pl.BlockSpec((ts, cp), lambda t: (nt - 1 - t, 0))
    prev = pl.BlockSpec((8, cp), lambda t: (jnp.maximum((nt - 1 - t) * (ts // 8) - 1, 0), 0))
    par = pl.BlockSpec((8, cp), lambda t: (0, 0))
    lo = jax.ShapeDtypeStruct((s_dim, cp), MXU_DTYPE)
    return pl.pallas_call(
        body, name=name, grid=(nt,),
        in_specs=[blk, blk, blk, prev, blk, blk, pl.BlockSpec((ts, cp), lambda t: (nt - 1 - t, 1)), par],
        out_specs=[blk, blk, blk, blk, par],
        out_shape=[lo, lo, lo, jax.ShapeDtypeStruct((s_dim, cp), F32), jax.ShapeDtypeStruct((8, cp), F32)],
        scratch_shapes=[pltpu.VMEM((ts, cp), F32), pltpu.VMEM((ts, cp), F32), pltpu.VMEM((8, cp), F32)],
        compiler_params=_cparams(("arbitrary",), 40 * _nbytes((ts, cp), F32)),
    )(dy, gx, hrec, hrec, rec, gates, gates, pvec)


def _fgate_fwd(fpre, bias, name):
    s_dim, w = fpre.shape
    ts = _tile(s_dim, 512, 8)

    def body(f_ref, b_ref, c_ref, lf_s, carry):
        @pl.when(pl.program_id(0) == 0)
        def _():
            carry[...] = jnp.zeros_like(carry)

        lf_s[...] = -_softplus(-(f_ref[...] + b_ref[pl.ds(0, 1), :]))

        def step(t, c):
            c = c + lf_s[pl.ds(t, 1), :]
            c_ref[pl.ds(t, 1), :] = c
            return c

        carry[pl.ds(0, 1), :] = lax.fori_loop(0, ts, step, carry[pl.ds(0, 1), :], unroll=8)

    blk = pl.BlockSpec((ts, w), lambda t: (t, 0))
    return pl.pallas_call(
        body, name=name, grid=(s_dim // ts,),
        in_specs=[blk, pl.BlockSpec((8, w), lambda t: (0, 0))], out_specs=blk,
        out_shape=jax.ShapeDtypeStruct((s_dim, w), F32),
        scratch_shapes=[pltpu.VMEM((ts, w), F32), pltpu.VMEM((8, w), F32)],
        compiler_params=_cparams(("arbitrary",), 12 * _nbytes((ts, w), F32)),
    )(fpre, bias)


def _fgate_bwd(dc, fpre, bias, name):
    s_dim, w = fpre.shape
    ts = _tile(s_dim, 512, 8)
    nt = s_dim // ts

    def body(dc_ref, f_ref, b_ref, df_ref, db_ref, d_s, carry):
        @pl.when(pl.program_id(0) == 0)
        def _():
            carry[...] = jnp.zeros_like(carry)
            db_ref[...] = jnp.zeros_like(db_ref)

        d_s[...] = dc_ref[...]

        def step(k, c):
            t = ts - 1 - k
            c = c + d_s[pl.ds(t, 1), :]
            d_s[pl.ds(t, 1), :] = c
            return c

        carry[pl.ds(0, 1), :] = lax.fori_loop(0, ts, step, carry[pl.ds(0, 1), :], unroll=8)
        df = d_s[...] * _sigmoid(-(f_ref[...] + b_ref[pl.ds(0, 1), :]))
        df_ref[...] = df
        db_ref[pl.ds(0, 1), :] += jnp.sum(df, axis=0, keepdims=True)

    blk = pl.BlockSpec((ts, w), lambda t: (nt - 1 - t, 0))
    par = pl.BlockSpec((8, w), lambda t: (0, 0))
    return pl.pallas_call(
        body, name=name, grid=(nt,), in_specs=[blk, blk, par], out_specs=[blk, par],
        out_shape=[jax.ShapeDtypeStruct((s_dim, w), F32), jax.ShapeDtypeStruct((8, w), F32)],
        scratch_shapes=[pltpu.VMEM((ts, w), F32), pltpu.VMEM((8, w), F32)],
        compiler_params=_cparams(("arbitrary",), 12 * _nbytes((ts, w), F32)),
    )(dc, fpre, bias)


def _head_lanes(hh, dh):
    lane = lax.broadcasted_iota(jnp.int32, (1, LANES), 1)
    return (lane >= hh * dh) & (lane < (hh + 1) * dh)


def _pair_attn_fwd(q, kv, v_t, c_col, c_row, name):
    s_dim, da = q.shape
    n_h = c_col.shape[0]
    dh = da // n_h
    assert LANES % dh == 0 and da % LANES == 0
    hb = LANES // dh
    n_blocks = da // LANES
    t = _tile(s_dim, 1024, LANES)
    nb = s_dim // t

    pairs = [(i, j) for i in range(nb) for j in range(i + 1)]
    i_tab = jnp.asarray([p[0] for p in pairs], jnp.int32)
    j_tab = jnp.asarray([p[1] for p in pairs], jnp.int32)

    def body(i_ref, j_ref, q_ref, k_ref, vt_ref, cq_ref, ck_ref, o_ref, lse_ref, m_s, l_s, acc):
        i, j = i_ref[pl.program_id(1)], j_ref[pl.program_id(1)]

        @pl.when(j == 0)
        def _():
            m_s[...] = jnp.full_like(m_s, -jnp.inf)
            l_s[...] = jnp.zeros_like(l_s)
            acc[...] = jnp.zeros_like(acc)

        def tile(masked):
            qv = q_ref[...]
            for hh in range(hb):
                st = _dot(k_ref[...], jnp.where(_head_lanes(hh, dh), qv, jnp.zeros_like(qv)), "nt")
                st = st + (cq_ref[hh] - ck_ref[hh])
                if masked:
                    keep = lax.broadcasted_iota(jnp.int32, (t, t), 0) <= lax.broadcasted_iota(jnp.int32, (t, t), 1)
                    st = jnp.where(keep, st, -jnp.inf)
                m_prev = m_s[hh]
                m_new = jnp.maximum(m_prev, jnp.max(st, axis=0, keepdims=True))
                alpha = jnp.exp(m_prev - m_new)
                p = jnp.exp(st - m_new)
                l_s[hh] = alpha * l_s[hh] + jnp.sum(p, axis=0, keepdims=True)
                acc[hh] = alpha * acc[hh] + _dot(vt_ref[...], p, "nn")
                m_s[hh] = m_new

        pl.when(j < i)(functools.partial(tile, False))
        pl.when(j == i)(functools.partial(tile, True))

        @pl.when(j == i)
        def _():
            feat = lax.broadcasted_iota(jnp.int32, (LANES, 1), 0)
            out_t = jnp.zeros((LANES, t), F32)
            for hh in range(hb):
                out_t = jnp.where((feat >= hh * dh) & (feat < (hh + 1) * dh), acc[hh] / l_s[hh], out_t)
                lse_ref[hh] = m_s[hh] + jnp.log(l_s[hh])
            o_ref[...] = out_t.T

    q_spec = pl.BlockSpec((t, LANES), lambda b, p, it, jt: (it[p], b))
    k_spec = pl.BlockSpec((t, LANES), lambda b, p, it, jt: (jt[p], b))
    vt_spec = pl.BlockSpec((LANES, t), lambda b, p, it, jt: (b, jt[p]))
    cq_spec = pl.BlockSpec((hb, 1, t), lambda b, p, it, jt: (b, 0, it[p]))
    ck_spec = pl.BlockSpec((hb, t, 1), lambda b, p, it, jt: (b, jt[p], 0))
    return pl.pallas_call(
        body, name=name,
        grid_spec=pltpu.PrefetchScalarGridSpec(
            num_scalar_prefetch=2, grid=(n_blocks, len(pairs)),
            in_specs=[q_spec, k_spec, vt_spec, cq_spec, ck_spec], out_specs=[q_spec, cq_spec],
            scratch_shapes=[pltpu.VMEM((hb, 1, t), F32), pltpu.VMEM((hb, 1, t), F32), pltpu.VMEM((hb, LANES, t), F32)]),
        out_shape=[jax.ShapeDtypeStruct((s_dim, da), F32), jax.ShapeDtypeStruct((n_h, 1, s_dim), F32)],
        compiler_params=_cparams(("parallel", "arbitrary"), 10 * hb * _nbytes((t, t), F32)),
    )(i_tab, j_tab, q, kv, v_t, c_row, c_col)


def _attn_delta(do, o, n_h, name):
    s_dim, da = o.shape
    dh = da // n_h
    hb = LANES // dh
    t = _tile(s_dim, 512, LANES)

    def body(do_ref, o_ref, d_ref):
        prod_t = (do_ref[...].astype(MXU_DTYPE).astype(F32) * o_ref[...]).T
        for hh in range(hb):
            d_ref[hh] = jnp.sum(prod_t[hh * dh:(hh + 1) * dh], axis=0, keepdims=True)

    blk = pl.BlockSpec((t, LANES), lambda b, i: (i, b))
    return pl.pallas_call(
        body, name=name, grid=(da // LANES, s_dim // t), in_specs=[blk, blk],
        out_specs=pl.BlockSpec((hb, 1, t), lambda b, i: (b, 0, i)),
        out_shape=jax.ShapeDtypeStruct((n_h, 1, s_dim), F32),
        compiler_params=_cparams(("parallel", "parallel"), 8 * _nbytes((t, LANES), F32)),
    )(do, o)


def _pair_attn_bwd(q, kv, c_col, c_row, lse, delta, do, scale, name):
    s_dim, da = q.shape
    n_h = c_col.shape[0]
    dh = da // n_h
    hb = LANES // dh
    n_blocks = da // LANES
    t = _tile(s_dim, 1024, LANES)
    nb = s_dim // t

    pairs = [(i, j) for j in range(nb) for i in range(j, nb)]
    i_tab = jnp.asarray([p[0] for p in pairs], jnp.int32)
    j_tab = jnp.asarray([p[1] for p in pairs], jnp.int32)

    def body(i_ref, j_ref, q_ref, k_ref, v_ref, cq_ref, ck_ref, lse_ref, dl_ref, do_ref,
             dq_ref, dcq_ref, dk_ref, dv_ref, dck_ref, dk_acc, dv_acc, dck_acc):
        i, j = i_ref[pl.program_id(1)], j_ref[pl.program_id(1)]

        @pl.when(pl.program_id(1) == 0)
        def _():
            dq_ref[...] = jnp.zeros_like(dq_ref)
            dcq_ref[...] = jnp.zeros_like(dcq_ref)

        @pl.when(i == j)
        def _():
            dk_acc[...] = jnp.zeros_like(dk_acc)
            dv_acc[...] = jnp.zeros_like(dv_acc)
            dck_acc[...] = jnp.zeros_like(dck_acc)

        def tile(masked):
            start = pl.multiple_of(i * t, t)
            qv, kv_ = q_ref[...], k_ref[...]
            dov = do_ref[...].astype(MXU_DTYPE)
            for hh in range(hb):
                lanes = _head_lanes(hh, dh)
                qm = jnp.where(lanes, qv, jnp.zeros_like(qv))
                km = jnp.where(lanes, kv_, jnp.zeros_like(kv_))
                dom = jnp.where(lanes, dov, jnp.zeros_like(dov))
                st = _dot(kv_, qm, "nt") + (cq_ref[hh] - ck_ref[hh])
                if masked:
                    keep = lax.broadcasted_iota(jnp.int32, (t, t), 0) <= lax.broadcasted_iota(jnp.int32, (t, t), 1)
                    st = jnp.where(keep, st, -jnp.inf)
                pt = jnp.exp(st - lse_ref[hh])
                dst = pt * (_dot(v_ref[...], dom, "nt") - dl_ref[hh])
                dv_acc[...] += _dot(pt, dom, "nn")
                dk_acc[...] += _dot(dst, qm, "nn")
                dq_ref[pl.ds(start, t), :] += _dot(dst, km, "tn") * scale
                dcq_ref[hh, :, pl.ds(start, t)] += jnp.sum(dst, axis=0, keepdims=True)
                dck_acc[hh] -= jnp.sum(dst, axis=1, keepdims=True)

        pl.when(i > j)(functools.partial(tile, False))
        pl.when(i == j)(functools.partial(tile, True))

        @pl.when(i == nb - 1)
        def _():
            dk_ref[...] = dk_acc[...]
            dv_ref[...] = dv_acc[...]
            dck_ref[...] = dck_acc[...]

    q_spec = pl.BlockSpec((t, LANES), lambda b, p, it, jt: (it[p], b))
    qrow_spec = pl.BlockSpec((hb, 1, t), lambda b, p, it, jt: (b, 0, it[p]))
    k_spec = pl.BlockSpec((t, LANES), lambda b, p, it, jt: (jt[p], b))
    v_spec = pl.BlockSpec((t, LANES), lambda b, p, it, jt: (jt[p], n_blocks + b))
    kcol_spec = pl.BlockSpec((hb, t, 1), lambda b, p, it, jt: (b, jt[p], 0))
    wide = jax.ShapeDtypeStruct((s_dim, da), F32)
    return pl.pallas_call(
        body, name=name,
        grid_spec=pltpu.PrefetchScalarGridSpec(
            num_scalar_prefetch=2, grid=(n_blocks, len(pairs)),
            in_specs=[q_spec, k_spec, v_spec, qrow_spec, kcol_spec, qrow_spec, qrow_spec, q_spec],
            out_specs=[pl.BlockSpec((s_dim, LANES), lambda b, p, it, jt: (0, b)),
                       pl.BlockSpec((hb, 1, s_dim), lambda b, p, it, jt: (b, 0, 0)), k_spec, k_spec, kcol_spec],
            scratch_shapes=[pltpu.VMEM((t, LANES), F32), pltpu.VMEM((t, LANES), F32), pltpu.VMEM((hb, t, 1), F32)]),
        out_shape=[wide, jax.ShapeDtypeStruct((n_h, 1, s_dim), F32), wide, wide,
                   jax.ShapeDtypeStruct((n_h, s_dim, 1), F32)],
        compiler_params=_cparams(("parallel", "arbitrary"),
                                 10 * hb * _nbytes((t, t), F32) + 4 * _nbytes((s_dim, LANES), F32)),
    )(i_tab, j_tab, q, kv, kv, c_row, c_col, lse, delta, do)


_HBM = pl.BlockSpec(memory_space=pltpu.HBM)
_MESH_ID = pl.DeviceIdType.MESH


def _all_gather(block, name):
    r, w = block.shape

    def body(x_ref, out_ref, send_sems, recv_sems, local_sem):
        x, y, c = lax.axis_index("x"), lax.axis_index("y"), lax.axis_index("c")
        me, sibling = (x, y, c), (x, y, 1 - c)
        chips = [(1 - x, y), (x, 1 - y), (1 - x, 1 - y)]

        def slot(px, py, pc):
            return out_ref.at[4 * px + 2 * py + pc]

        def copy(k, blk, to, src=None):
            return pltpu.make_async_remote_copy(
                src_ref=slot(*blk) if src is None else src, dst_ref=slot(*blk),
                send_sem=send_sems.at[k], recv_sem=recv_sems.at[k], device_id=to, device_id_type=_MESH_ID)

        mine = pltpu.make_async_copy(x_ref, slot(*me), local_sem)
        mine.start()
        first = [copy(0, me, sibling, src=x_ref)]
        first += [copy(1 + n, me, (*chip, c), src=x_ref) for n, chip in enumerate(chips)]
        for cp in first:
            cp.start()
        passed = [copy(4 + n, (*chip, c), sibling) for n, chip in enumerate(chips)]
        for n, chip in enumerate(chips):
            copy(1 + n, (*chip, c), me).wait_recv()
            passed[n].start()
        copy(0, sibling, me).wait_recv()
        for n, chip in enumerate(chips):
            copy(4 + n, (*chip, 1 - c), me).wait_recv()
        for cp in first + passed:
            cp.wait_send()
        mine.wait()

    return pl.pallas_call(
        body, name=name, out_shape=jax.ShapeDtypeStruct((N_DEV, r, w), block.dtype),
        in_specs=[_HBM], out_specs=_HBM,
        scratch_shapes=[pltpu.SemaphoreType.DMA((7,)), pltpu.SemaphoreType.DMA((7,)), pltpu.SemaphoreType.DMA],
    )(block)


_SEM = pl.BlockSpec(memory_space=pltpu.SEMAPHORE)
_EFFECT = pltpu.SideEffectType.DATAFLOW_SIDE_EFFECTING


def _exchange_start(srcs, personalized, after, name):
    n = len(srcs)
    n_after = len(after)
    lands = [lax.empty((N_DEV,) + s.shape[-2:], s.dtype) for s in srcs]

    def body(*refs):
        src_refs, land_refs = refs[:n], refs[n:2 * n]
        outs = refs[2 * n + n_after:]
        send_sems, recv_sems, token = outs[:n], outs[n:2 * n], outs[-1]
        x, y, c = lax.axis_index("x"), lax.axis_index("y"), lax.axis_index("c")
        mine = 4 * x + 2 * y + c
        for ci in range(n):
            for k in range(1, N_DEV):
                px = 1 - x if k & 4 else x
                py = 1 - y if k & 2 else y
                pc = 1 - c if k & 1 else c
                src = src_refs[ci].at[4 * px + 2 * py + pc] if personalized else src_refs[ci]
                pltpu.make_async_remote_copy(
                    src_ref=src, dst_ref=land_refs[ci].at[mine], send_sem=send_sems[ci], recv_sem=recv_sems[ci],
                    device_id=(px, py, pc), device_id_type=_MESH_ID).start()
        token[...] = jnp.zeros_like(token)

    sem = pltpu.SemaphoreType.DMA(())
    out_shape = ([sem] * (2 * n) + [pltpu.HBM(s.shape, s.dtype) for s in srcs]
                 + [pltpu.HBM(l.shape, l.dtype) for l in lands] + [jax.ShapeDtypeStruct((8, LANES), F32)])
    res = pl.pallas_call(
        body, name=name, out_shape=tuple(out_shape),
        in_specs=[_HBM] * (2 * n) + [_ANY] * n_after,
        out_specs=tuple([_SEM] * (2 * n) + [_HBM] * (2 * n) + [pl.BlockSpec(memory_space=pltpu.VMEM)]),
        input_output_aliases={i: 2 * n + i for i in range(2 * n)},
        compiler_params=pltpu.CompilerParams(has_side_effects=_EFFECT),
    )(*[pltpu.with_memory_space_constraint(s, pltpu.HBM) for s in srcs],
      *[pltpu.with_memory_space_constraint(l, pltpu.HBM) for l in lands], *after)
    handles = [(res[ci], res[n + ci], res[2 * n + ci], res[3 * n + ci]) for ci in range(n)]
    return handles, res[-1]


def _exchange_wait(handle, after, name):
    send_sem, recv_sem, src_thru, land_thru = handle

    def body(src_ref, land_ref, send_ref, recv_ref, after_ref, src_out, land_out):
        seven = land_ref.at[pl.ds(0, N_DEV - 1)]
        copies = pltpu.make_async_remote_copy(
            src_ref=seven, dst_ref=seven, send_sem=send_ref, recv_sem=recv_ref,
            device_id=(lax.axis_index("x"), lax.axis_index("y"), lax.axis_index("c")), device_id_type=_MESH_ID)
        copies.wait_send()
        copies.wait_recv()

    return pl.pallas_call(
        body, name=name,
        out_shape=(pltpu.HBM(src_thru.shape, src_thru.dtype), pltpu.HBM(land_thru.shape, land_thru.dtype)),
        in_specs=(_HBM, _HBM, _SEM, _SEM, _ANY), out_specs=(_HBM, _HBM), input_output_aliases={0: 0, 1: 1},
        compiler_params=pltpu.CompilerParams(has_side_effects=_EFFECT),
    )(src_thru, land_thru, send_sem, recv_sem, after)[1]


def _own_slot(land, own, me):
    return lax.dynamic_update_index_in_dim(land, own, me, axis=0)


def _sum_slots(slots, name):
    n, r, w = slots.shape
    tr = _tile(r, 128, WIRE_ROW_ALIGN)

    def body(s_ref, o_ref):
        acc = s_ref[0].astype(F32)
        for d in range(1, n):
            acc = acc + s_ref[d].astype(F32)
        o_ref[...] = acc

    return pl.pallas_call(
        body, name=name, grid=(r // tr,),
        in_specs=[pl.BlockSpec((n, tr, w), lambda i: (0, i, 0))],
        out_specs=pl.BlockSpec((tr, w), lambda i: (i, 0)),
        out_shape=jax.ShapeDtypeStruct((r, w), F32),
        compiler_params=_cparams(("parallel",), 2 * _nbytes((n, tr, w), slots.dtype) + 4 * _nbytes((tr, w), F32)),
    )(slots)


def _adamw(w, g, m, v, name):
    r, c = w.shape
    tr = _tile(r, 512, 8)

    def body(w_ref, g_ref, m_ref, v_ref, d_ref, mo_ref, vo_ref):
        gv = g_ref[...]
        m_new = ADAM_B1 * m_ref[...] + (1.0 - ADAM_B1) * gv
        v_new = ADAM_B2 * v_ref[...] + (1.0 - ADAM_B2) * (gv * gv)
        m_hat = m_new / (1.0 - ADAM_B1 ** ADAM_STEP)
        v_hat = v_new / (1.0 - ADAM_B2 ** ADAM_STEP)
        d_ref[...] = -ADAM_LR * (m_hat / (jnp.sqrt(v_hat) + ADAM_EPS) + ADAM_WD * w_ref[...])
        mo_ref[...] = m_new
        vo_ref[...] = v_new

    blk = pl.BlockSpec((tr, c), lambda i: (i, 0))
    shp = jax.ShapeDtypeStruct((r, c), F32)
    return pl.pallas_call(
        body, name=name, grid=(r // tr,), in_specs=[blk] * 4, out_specs=[blk] * 3, out_shape=[shp] * 3,
        compiler_params=_cparams(("parallel",), 16 * _nbytes((tr, _round_up(c, LANES)), F32)),
    )(w, g, m, v)


def _pack_rows(parts, width, dtype, row_align):
    rows, spans, off = [], [], 0
    for p in parts:
        flat = p.reshape(-1).astype(dtype)
        n_rows = _round_up(-(-flat.shape[0] // width), row_align)
        flat = jnp.pad(flat, (0, n_rows * width - flat.shape[0]))
        rows.append(flat.reshape(n_rows, width))
        spans.append((off, n_rows))
        off += n_rows
    return jnp.concatenate(rows, axis=0), spans


def _unpack_rows(mat, span, shape):
    off, n_rows = span
    n = 1
    for s in shape:
        n *= s
    return mat[..., off:off + n_rows, :].reshape(mat.shape[:-2] + (-1,))[..., :n].reshape(mat.shape[:-2] + tuple(shape))


def _block_diag(w, size):
    n, b, _ = w.shape
    eye = jnp.eye(n, dtype=w.dtype)
    dense = (w[:, :, None, :] * eye[:, None, :, None]).reshape(n * b, n * b)
    return jnp.pad(dense, ((0, size - n * b), (0, size - n * b)))


def _diag_blocks(dense, n, b):
    return jnp.stack([dense[k * b:(k + 1) * b, k * b:(k + 1) * b] for k in range(n)])


def _pad_rows(a, rows):
    return jnp.pad(a, ((0, rows - a.shape[0]), (0, 0)))


def _pad_cols(a, cols):
    return jnp.pad(a, ((0, 0), (0, cols - a.shape[1])))


def _train_step(a):
    x = a["x"][0]
    target = a["loss_target"][0]
    s_dim, d = x.shape
    n_layers = a["ffn1_pre_g"].shape[0]
    f_shard = a["ffn1_w_gate"].shape[2]
    c_shard = a["rg_conv_b"].shape[1]
    c_dim = c_shard * N_DEV
    cp = _round_up(c_dim, LANES)
    conv_width = a["rg_conv_w"].shape[1]
    n_blocks, lru_block = a["rg_w_a"].shape[1], a["rg_w_a"].shape[2]
    d_attn = a["attn_w_q"].shape[2]
    n_heads = a["b_fgate"].shape[0]
    d_head = d_attn // n_heads
    attn_scale = d_head ** -0.5
    assert conv_width < 8 and n_heads <= LANES and n_layers == 2
    assert d_attn == d
    me = 4 * lax.axis_index("x") + 2 * lax.axis_index("y") + lax.axis_index("c")

    shard = {"rg_w_in": a["rg_w_in"][0].T, "rg_w_out": a["rg_w_out"][0], "w_kv": a["w_kv"].T,
             "attn_w_q": a["attn_w_q"][0], "attn_w_o": a["attn_w_o"][0]}
    for l in range(n_layers):
        for f in ("ffn1", "ffn2"):
            shard[(f, "gate", l)] = a[f + "_w_gate"][l].T
            shard[(f, "up", l)] = a[f + "_w_up"][l].T
            shard[(f, "down", l)] = a[f + "_w_down"][l]

    def ffn_names(f, l):
        return [(f, "gate", l), (f, "up", l), (f, "down", l)]

    def chunk_layout(names):
        spans, off = [], 0
        for nm in names:
            spans.append((nm, off, shard[nm].shape[0]))
            off += _round_up(shard[nm].shape[0], WIRE_ROW_ALIGN)
        return spans, off

    def pack_chunk(names, parts):
        return jnp.concatenate(
            [_pad_rows(parts[nm].astype(WIRE_DTYPE), _round_up(parts[nm].shape[0], WIRE_ROW_ALIGN)) for nm in names], axis=0)

    full = {}

    def unpack_chunk(names, gathered):
        for nm, o, n_rows in chunk_layout(names)[0]:
            full[nm] = gathered[:, o:o + n_rows, :].reshape(N_DEV * n_rows, d)

    fwd_chunks = [ffn_names("ffn1", 0)[:2], ffn_names("ffn1", 0)[2:], ["rg_w_in"], ["rg_w_out"],
                  ffn_names("ffn2", 0) + ["w_kv"], ffn_names("ffn1", 1) + ["attn_w_q", "attn_w_o"], ffn_names("ffn2", 1)]
    fwd_packs = [pack_chunk(names, shard) for names in fwd_chunks]
    unpack_chunk(fwd_chunks[0], _all_gather(fwd_packs[0], "gather_weights_first"))

    small_parts = [a["rg_conv_w"][0], a["rg_conv_b"][0], a["rg_b_a"][0], a["rg_b_x"][0], a["rg_lambda"][0], a["w_fgate"]]
    small_pack, small_spans = _pack_rows(small_parts, d, F32, 8)
    small_all = _all_gather(small_pack, "gather_small")
    fwd_handles, fwd_token = _exchange_start(fwd_packs[1:], False, [full[("ffn1", "up", 0)], small_all],
                                             "gather_weights_start")

    def land_weights(n, after):
        land = _exchange_wait(fwd_handles[n - 1], after, f"gather_weights_wait_{n}")
        unpack_chunk(fwd_chunks[n], _own_slot(land, fwd_packs[n], me))

    sm = [_unpack_rows(small_all, sp, p.shape) for sp, p in zip(small_spans, small_parts)]
    conv_w = jnp.moveaxis(sm[0], 0, 1).reshape(conv_width, c_dim)
    conv_b, b_a, b_x, lam = (v.reshape(1, c_dim) for v in sm[1:5])
    w_f = sm[5].reshape(d, n_heads)

    pconv = _pad_rows(_pad_cols(jnp.concatenate([conv_w, conv_b], axis=0), cp), 8)
    pvec = _pad_rows(_pad_cols(jnp.concatenate([b_a, b_x, lam], axis=0), cp), 8)
    wa_dense = _block_diag(a["rg_w_a"][0], cp).astype(MXU_DTYPE)
    wx_dense = _block_diag(a["rg_w_x"][0], cp).astype(MXU_DTYPE)
    wax = jnp.concatenate([wa_dense, wx_dense], axis=1)
    w_f_t = _pad_rows(w_f.T.astype(MXU_DTYPE), LANES)
    b_f = _pad_rows(_pad_cols(a["b_fgate"].reshape(1, n_heads), LANES), 8)

    def gain(name, l):
        return a[name][l].reshape(1, d)

    def ffn_fwd(h, f, l, after=None, down_chunk=None):
        xn = _rms_fwd(h, gain(f + "_pre_g", l), f"{f}_{l}_pre_norm", after)
        g, u, act = _ffn_up(xn, full[(f, "gate", l)], full[(f, "up", l)], f"{f}_{l}_up")
        if down_chunk is not None:
            land_weights(down_chunk, act)
        fo, h_new = _mm_rms_res(act, full[(f, "down", l)], h, gain(f + "_post_g", l), 0.5, f"{f}_{l}_down")
        return h_new, (h, xn, g, u, act, fo)

    h0 = x
    h0a, sv_f1_0 = ffn_fwd(h0, "ffn1", 0, fwd_token, down_chunk=1)
    land_weights(2, h0a)
    w_in_gate = _pad_rows(full["rg_w_in"][:c_dim], cp)
    w_in_rec = _pad_rows(full["rg_w_in"][c_dim:], cp)
    w_in_t = jnp.concatenate([w_in_gate, w_in_rec], axis=0)
    hn_rg = _rms_fwd(h0a, gain("mix_pre_g", 0), "rg_pre_norm")
    gx = _mm([(hn_rg, w_in_t)], "nt", F32, "rg_in_proj")
    rec = _conv_fwd(gx, pconv, conv_width, "rg_conv")
    gates = _mm([(rec, wax)], "nn", F32, "rg_gate_proj")
    h_rec, y_rg = _scan_fwd(gx, rec, gates, pvec, "rg_scan")
    land_weights(3, y_rg)
    w_out = _pad_rows(full["rg_w_out"], cp)
    m_rg, h0b = _mm_rms_res(y_rg, w_out, h0a, gain("mix_post_g", 0), 1.0, "rg_out_proj")
    land_weights(4, h0b)
    h1, sv_f2_0 = ffn_fwd(h0b, "ffn2", 0)
    hn_kv = _rms_fwd(h1, a["kv_norm_g"].reshape(1, d), "kv_norm")
    kv = _mm([(hn_kv, full["w_kv"])], "nt", MXU_DTYPE, "kv_proj")
    fpre = _mm([(hn_kv, w_f_t)], "nt", F32, "fgate_proj")
    c_cum = _fgate_fwd(fpre, b_f, "fgate_cumsum")
    c_heads = c_cum[:, :n_heads].T
    c_col, c_row = c_heads[:, :, None], c_heads[:, None, :]
    land_weights(5, c_cum)
    h1a, sv_f1_1 = ffn_fwd(h1, "ffn1", 1)
    hn_at = _rms_fwd(h1a, gain("mix_pre_g", 1), "attn_pre_norm")
    q_s = _mm([(hn_at, full["attn_w_q"])], "nn", MXU_DTYPE, "q_proj", out_scale=attn_scale)
    o2, lse = _pair_attn_fwd(q_s, kv, kv[:, d_attn:].T, c_col, c_row, "attn_fwd")
    m_at, h1b = _mm_rms_res(o2, full["attn_w_o"], h1a, gain("mix_post_g", 1), 1.0, "attn_out_proj")
    land_weights(6, h1b)
    y, sv_f2_1 = ffn_fwd(h1b, "ffn2", 1)
    dy, loss_part = _loss_head(y, target, "loss_head")

    grads_big = {}
    grads_rep = {}

    bwd_chunks = [ffn_names("ffn2", 1), ["attn_w_q", "attn_w_o"] + ffn_names("ffn1", 1),
                  ["w_kv"] + ffn_names("ffn2", 0), ["rg_w_in", "rg_w_out"], ffn_names("ffn1", 0)[2:],
                  ffn_names("ffn1", 0)[:2]]
    bwd_sends, bwd_handles = [], []

    def send_grads(after):
        n = len(bwd_sends)
        send = jnp.concatenate(
            [jnp.pad(grads_big[nm].reshape(N_DEV, n_rows, d), ((0, 0), (0, _round_up(n_rows, WIRE_ROW_ALIGN) - n_rows), (0, 0)))
             for nm, _, n_rows in chunk_layout(bwd_chunks[n])[0]], axis=1)
        handles, token = _exchange_start([send], True, [after], f"exchange_grads_start_{n}")
        bwd_sends.append(send)
        bwd_handles.append(handles[0])
        return token

    def ffn_bwd(dh_out, saved, f, l, after=None, send_now=False):
        h, xn, g, u, act, fo = saved
        df, d_post = _rms_bwd(fo, gain(f + "_post_g", l), [dh_out], None, 0.5, MXU_DTYPE, f"{f}_{l}_post_norm_bwd", after)
        dg, du = _ffn_act_bwd(df, full[(f, "down", l)], g, u, f"{f}_{l}_act_bwd")
        grads_big[(f, "down", l)] = _mm([(act, df)], "tn", WIRE_DTYPE, f"{f}_{l}_dw_down")
        sent = send_grads(df) if send_now else None
        grads_big[(f, "gate", l)] = _mm([(dg, xn)], "tn", WIRE_DTYPE, f"{f}_{l}_dw_gate", sent)
        grads_big[(f, "up", l)] = _mm([(du, xn)], "tn", WIRE_DTYPE, f"{f}_{l}_dw_up")
        sent = send_grads(df) if send_now else None
        dxn = _mm([(dg, full[(f, "gate", l)]), (du, full[(f, "up", l)])], "nn", F32, f"{f}_{l}_dx", sent)
        dh_in, d_pre = _rms_bwd(h, gain(f + "_pre_g", l), [dxn], dh_out, 1.0, F32, f"{f}_{l}_pre_norm_bwd")
        grads_rep[(f + "_post_g", l)] = d_post
        grads_rep[(f + "_pre_g", l)] = d_pre
        return dh_in

    dh = ffn_bwd(dy, sv_f2_1, "ffn2", 1)
    token = send_grads(dh)
    dm, d_post = _rms_bwd(m_at, gain("mix_post_g", 1), [dh], None, 1.0, MXU_DTYPE, "attn_post_norm_bwd", token)
    grads_rep[("mix_post_g", 1)] = d_post
    do2 = _mm([(dm, full["attn_w_o"])], "nt", F32, "attn_out_proj_dx")
    grads_big["attn_w_o"] = _mm([(o2, dm)], "tn", WIRE_DTYPE, "attn_out_proj_dw")
    delta = _attn_delta(do2, o2, n_heads, "attn_delta")
    dq2, dc_q, dk2, dv2, dc_k = _pair_attn_bwd(q_s, kv, c_col, c_row, lse, delta, do2, attn_scale, "attn_bwd")
    dc_heads = dc_q[:, 0, :] + dc_k[:, :, 0]
    dhn = _mm([(dq2, full["attn_w_q"])], "nt", F32, "q_proj_dx")
    grads_big["attn_w_q"] = _mm([(hn_at, dq2)], "tn", WIRE_DTYPE, "q_proj_dw")
    dh, d_pre = _rms_bwd(h1a, gain("mix_pre_g", 1), [dhn], dh, 1.0, F32, "attn_pre_norm_bwd")
    grads_rep[("mix_pre_g", 1)] = d_pre
    dh = ffn_bwd(dh, sv_f1_1, "ffn1", 1)
    token = send_grads(dh)
    dc_cum = _pad_cols(dc_heads.T, LANES)
    dfpre, db_f = _fgate_bwd(dc_cum, fpre, b_f, "fgate_cumsum_bwd")
    dhn_kv = _mm([(dk2, full["w_kv"][:d_attn]), (dv2, full["w_kv"][d_attn:])], "nn", F32, "kv_proj_dx")
    dhn_f = _mm([(dfpre, w_f_t)], "nn", F32, "fgate_proj_dx")
    grads_big["w_kv"] = jnp.concatenate([_mm([(dk2, hn_kv)], "tn", WIRE_DTYPE, "kv_proj_dw_k"),
                                         _mm([(dv2, hn_kv)], "tn", WIRE_DTYPE, "kv_proj_dw_v")], axis=0)
    dw_f_t = _mm([(dfpre, hn_kv)], "tn", F32, "fgate_proj_dw")
    dh, d_kvg = _rms_bwd(h1, a["kv_norm_g"].reshape(1, d), [dhn_kv, dhn_f], dh, 1.0, F32, "kv_norm_bwd", token)
    dh = ffn_bwd(dh, sv_f2_0, "ffn2", 0)
    token = send_grads(dh)
    dm, d_post = _rms_bwd(m_rg, gain("mix_post_g", 0), [dh], None, 1.0, MXU_DTYPE, "rg_post_norm_bwd", token)
    grads_rep[("mix_post_g", 0)] = d_post
    dy_rg = _mm([(dm, w_out)], "nt", F32, "rg_out_proj_dx")
    dw_out = _mm([(y_rg, dm)], "tn", WIRE_DTYPE, "rg_out_proj_dw")
    dgate, dra, dia, drec1, dpvec = _scan_bwd(dy_rg, gx, h_rec, rec, gates, pvec, "rg_scan_bwd")
    drec2 = _mm([(dra, wa_dense), (dia, wx_dense)], "nt", F32, "rg_gate_proj_dx")
    dwa_dense = _mm([(rec, dra)], "tn", F32, "rg_gate_proj_dwa")
    dwx_dense = _mm([(rec, dia)], "tn", F32, "rg_gate_proj_dwx")
    drec0, dpconv = _conv_bwd(drec1, drec2, gx, pconv, conv_width, "rg_conv_bwd")
    dhn = _mm([(dgate, w_in_gate), (drec0, w_in_rec)], "nn", F32, "rg_in_proj_dx")
    dw_in_gate = _mm([(dgate, hn_rg)], "tn", WIRE_DTYPE, "rg_in_proj_dw_gate")
    dw_in_rec = _mm([(drec0, hn_rg)], "tn", WIRE_DTYPE, "rg_in_proj_dw_rec")
    dh, d_pre = _rms_bwd(h0a, gain("mix_pre_g", 0), [dhn], dh, 1.0, F32, "rg_pre_norm_bwd")
    grads_rep[("mix_pre_g", 0)] = d_pre
    grads_big["rg_w_in"] = jnp.concatenate([dw_in_gate[:c_dim], dw_in_rec[:c_dim]], axis=0)
    grads_big["rg_w_out"] = dw_out[:c_dim]
    token = send_grads(dh)
    grad_x = ffn_bwd(dh, sv_f1_0, "ffn1", 0, token, send_now=True)

    g_shard = {}

    def land_grads(n, after):
        land = _exchange_wait(bwd_handles[n], after, f"exchange_grads_wait_{n}")
        own = lax.dynamic_index_in_dim(bwd_sends[n], me, axis=0, keepdims=False)
        g_chunk = _sum_slots(_own_slot(land, own, me), f"sum_weight_grads_{n}")
        for nm, o, n_rows in chunk_layout(bwd_chunks[n])[0]:
            g_shard[nm] = g_chunk[o:o + n_rows]

    for n in range(len(bwd_chunks) - 2):
        land_grads(n, grad_x)

    def gain_grad(name):
        return jnp.concatenate([grads_rep[(name, l)] for l in range(n_layers)], axis=0)

    rep_names = ["ffn1_pre_g", "ffn1_post_g", "mix_pre_g", "mix_post_g", "ffn2_pre_g", "ffn2_post_g"]
    rep_parts = [gain_grad(nm) for nm in rep_names]
    rep_names += ["kv_norm_g", "b_fgate", "rg_w_a", "rg_w_x", "rg_conv_w", "rg_conv_b", "rg_b_a", "rg_b_x", "rg_lambda", "w_fgate"]
    rep_parts += [
        d_kvg, db_f[0, :n_heads],
        _diag_blocks(dwa_dense, n_blocks, lru_block), _diag_blocks(dwx_dense, n_blocks, lru_block),
        dpconv[:conv_width, :c_dim], dpconv[conv_width, :c_dim],
        dpvec[0, :c_dim], dpvec[1, :c_dim], dpvec[2, :c_dim],
        dw_f_t[:n_heads].T]
    rep_pack, rep_spans = _pack_rows(rep_parts, d, F32, WIRE_ROW_ALIGN)
    rep_sum = _sum_slots(_all_gather(rep_pack, "gather_small_grads"), "sum_small_grads")
    g_rep = {nm: _unpack_rows(rep_sum, sp, p.shape) for nm, sp, p in zip(rep_names, rep_spans, rep_parts)}

    def my_cols(full_grad, n):
        return lax.dynamic_slice_in_dim(full_grad, me * n, n, axis=full_grad.ndim - 1)

    def ffn_grads(f):
        grad[f + "_w_gate"] = jnp.stack([g_shard[(f, "gate", l)].T for l in range(n_layers)])
        grad[f + "_w_up"] = jnp.stack([g_shard[(f, "up", l)].T for l in range(n_layers)])
        grad[f + "_w_down"] = jnp.stack([g_shard[(f, "down", l)] for l in range(n_layers)])

    grad = {}
    for nm in ("ffn1_pre_g", "ffn1_post_g", "mix_pre_g", "mix_post_g", "ffn2_pre_g", "ffn2_post_g"):
        grad[nm] = g_rep[nm]
    ffn_grads("ffn2")
    grad["rg_w_in"] = g_shard["rg_w_in"].T[None]
    grad["rg_conv_w"] = my_cols(g_rep["rg_conv_w"], c_shard)[None]
    for nm in ("rg_conv_b", "rg_b_a", "rg_b_x", "rg_lambda"):
        grad[nm] = my_cols(g_rep[nm], c_shard)[None]
    grad["rg_w_a"] = g_rep["rg_w_a"][None]
    grad["rg_w_x"] = g_rep["rg_w_x"][None]
    grad["rg_w_out"] = g_shard["rg_w_out"][None]
    grad["kv_norm_g"] = g_rep["kv_norm_g"].reshape(d)
    grad["w_kv"] = g_shard["w_kv"].T
    grad["w_fgate"] = lax.dynamic_slice_in_dim(g_rep["w_fgate"], me * (d // N_DEV), d // N_DEV, axis=0)
    grad["b_fgate"] = g_rep["b_fgate"]
    grad["attn_w_q"] = g_shard["attn_w_q"][None]
    grad["attn_w_o"] = g_shard["attn_w_o"][None]

    delta, new_m, new_v = {}, {}, {}

    def adamw(nm):
        w = a[nm]
        shape = w.shape
        two_d = (1, shape[0]) if w.ndim == 1 else (-1, shape[-1])
        ops = [pltpu.with_memory_space_constraint(t.reshape(two_d), pltpu.HBM)
               for t in (w, grad[nm], a["m_" + nm], a["v_" + nm])]
        dl, mo, vo = _adamw(*ops, "adamw_" + nm)
        delta[nm], new_m[nm], new_v[nm] = dl.reshape(shape), mo.reshape(shape), vo.reshape(shape)
        grad[nm] = grad[nm].reshape(shape)

    last_names = ("ffn1_w_gate", "ffn1_w_up", "ffn1_w_down")
    for nm in WEIGHT_NAMES:
        if nm not in last_names:
            adamw(nm)
    land_grads(len(bwd_chunks) - 2, delta["attn_w_o"])
    land_grads(len(bwd_chunks) - 1, delta["attn_w_o"])
    ffn_grads("ffn1")
    for nm in last_names:
        adamw(nm)

    loss = lax.psum(loss_part[0, 0], AXES)
    return (loss, grad_x[None], *[grad[n] for n in WEIGHT_NAMES], *[delta[n] for n in WEIGHT_NAMES],
            *[new_m[n] for n in WEIGHT_NAMES], *[new_v[n] for n in WEIGHT_NAMES])


def kernel(x, ffn1_pre_g, ffn1_w_gate, ffn1_w_up, ffn1_w_down, ffn1_post_g, mix_pre_g, mix_post_g, ffn2_pre_g, ffn2_w_gate, ffn2_w_up, ffn2_w_down, ffn2_post_g, rg_w_in, rg_conv_w, rg_conv_b, rg_w_a, rg_b_a, rg_w_x, rg_b_x, rg_lambda, rg_w_out, kv_norm_g, w_kv, w_fgate, b_fgate, attn_w_q, attn_w_o, loss_target, m_ffn1_pre_g, m_ffn1_w_gate, m_ffn1_w_up, m_ffn1_w_down, m_ffn1_post_g, m_mix_pre_g, m_mix_post_g, m_ffn2_pre_g, m_ffn2_w_gate, m_ffn2_w_up, m_ffn2_w_down, m_ffn2_post_g, m_rg_w_in, m_rg_conv_w, m_rg_conv_b, m_rg_w_a, m_rg_b_a, m_rg_w_x, m_rg_b_x, m_rg_lambda, m_rg_w_out, m_kv_norm_g, m_w_kv, m_w_fgate, m_b_fgate, m_attn_w_q, m_attn_w_o, v_ffn1_pre_g, v_ffn1_w_gate, v_ffn1_w_up, v_ffn1_w_down, v_ffn1_post_g, v_mix_pre_g, v_mix_post_g, v_ffn2_pre_g, v_ffn2_w_gate, v_ffn2_w_up, v_ffn2_w_down, v_ffn2_post_g, v_rg_w_in, v_rg_conv_w, v_rg_conv_b, v_rg_w_a, v_rg_b_a, v_rg_w_x, v_rg_b_x, v_rg_lambda, v_rg_w_out, v_kv_norm_g, v_w_kv, v_w_fgate, v_b_fgate, v_attn_w_q, v_attn_w_o):
    return _train_step(dict(locals()))
```

```python
import functools

import jax
import jax.numpy as jnp
from jax import lax
from jax.experimental import pallas as pl
from jax.experimental.pallas import tpu as pltpu

F32 = jnp.float32
MXU_DTYPE = jnp.bfloat16
WIRE_DTYPE = jnp.bfloat16
N_DEV = 8
AXES = ("x", "y", "c")
LANES = 128
WIRE_ROW_ALIGN = 16
VMEM_LIMIT_MIN = 32 * 2 ** 20
VMEM_LIMIT_MAX = 56 * 2 ** 20

RMS_EPS = 1e-6
LRU_C = 8.0
ADAM_LR, ADAM_B1, ADAM_B2, ADAM_EPS, ADAM_WD, ADAM_STEP = 0.001, 0.9, 0.999, 1e-08, 0.01, 10

WEIGHT_NAMES = (
    "ffn1_pre_g", "ffn1_w_gate", "ffn1_w_up", "ffn1_w_down", "ffn1_post_g", "mix_pre_g", "mix_post_g",
    "ffn2_pre_g", "ffn2_w_gate", "ffn2_w_up", "ffn2_w_down", "ffn2_post_g", "rg_w_in", "rg_conv_w",
    "rg_conv_b", "rg_w_a", "rg_b_a", "rg_w_x", "rg_b_x", "rg_lambda", "rg_w_out", "kv_norm_g", "w_kv",
    "w_fgate", "b_fgate", "attn_w_q", "attn_w_o")


def _round_up(n, m):
    return (n + m - 1) // m * m


def _tile(dim, target, align=LANES):
    if dim <= target:
        return dim
    best = None
    t = align
    while t <= target:
        if dim % t == 0:
            best = t
        t += align
    return dim if best is None else best


def _cparams(semantics, vmem_estimate):
    limit = min(VMEM_LIMIT_MAX, max(VMEM_LIMIT_MIN, 2 * int(vmem_estimate)))
    return pltpu.CompilerParams(dimension_semantics=semantics, vmem_limit_bytes=limit)


def _nbytes(shape, dtype):
    n = 1
    for s in shape:
        n *= s
    return n * jnp.dtype(dtype).itemsize


def _sigmoid(x):
    return jax.nn.sigmoid(x)


def _softplus(x):
    return jnp.maximum(x, 0.0) + jnp.log1p(jnp.exp(-jnp.abs(x)))


def _expm1(x):
    series = x * (1.0 + x * (0.5 + x * (1.0 / 6.0 + x * (1.0 / 24.0 + x * (1.0 / 120.0)))))
    return jnp.where(jnp.abs(x) < 0.25, series, jnp.exp(x) - 1.0)


_GELU_C = 0.7978845608028654
_GELU_A = 0.044715


def _gelu(x):
    return 0.5 * x * (1.0 + jnp.tanh(_GELU_C * (x + _GELU_A * x * x * x)))


def _gelu_grad(x):
    t = jnp.tanh(_GELU_C * (x + _GELU_A * x * x * x))
    return 0.5 * (1.0 + t) + 0.5 * x * (1.0 - t * t) * _GELU_C * (1.0 + 3.0 * _GELU_A * x * x)


_DOT_DIMS = {"nn": ((1,), (0,)), "nt": ((1,), (1,)), "tn": ((0,), (0,))}


def _dot(a, b, mode):
    return lax.dot_general(a.astype(MXU_DTYPE), b.astype(MXU_DTYPE), (_DOT_DIMS[mode], ((), ())),
                           preferred_element_type=F32)


def _mm(pairs, mode, out_dtype, name, after=None, out_scale=None):
    a0, b0 = pairs[0]
    if mode == "tn":
        k_dim, m_dim = a0.shape
        n_dim = b0.shape[1]
    else:
        m_dim, k_dim = a0.shape
        n_dim = b0.shape[0] if mode == "nt" else b0.shape[1]
    for a, b in pairs:
        assert a.shape == a0.shape and b.shape == b0.shape
    tm = _tile(m_dim, 1408 if mode == "tn" else 512)
    whole = 1408 if mode == "tn" else 2816
    tn = _tile(n_dim, whole)
    tk = _tile(k_dim, whole)
    nk = k_dim // tk
    n_pairs = len(pairs)

    if mode == "tn":
        a_spec = pl.BlockSpec((tk, tm), lambda i, j, k: (k, i))
    else:
        a_spec = pl.BlockSpec((tm, tk), lambda i, j, k: (i, k))
    if mode == "nt":
        b_spec = pl.BlockSpec((tn, tk), lambda i, j, k: (j, k))
    else:
        b_spec = pl.BlockSpec((tk, tn), lambda i, j, k: (k, j))

    order = [] if after is None else [after]

    def body(*refs):
        ins, o_ref, acc = refs[:2 * n_pairs], refs[-2], refs[-1]
        k = pl.program_id(2)

        @pl.when(k == 0)
        def _():
            acc[...] = jnp.zeros_like(acc)

        s = acc[...]
        for p in range(n_pairs):
            s = s + _dot(ins[2 * p][...], ins[2 * p + 1][...], mode)
        acc[...] = s

        @pl.when(k == nk - 1)
        def _():
            r = acc[...] if out_scale is None else acc[...] * out_scale
            o_ref[...] = r.astype(out_dtype)

    est = (2 * n_pairs * (_nbytes((tm, tk), a0.dtype) + _nbytes((tk, tn), b0.dtype))
           + 2 * _nbytes((tm, tn), out_dtype) + 2 * _nbytes((tm, tn), F32))
    flat = [t for ab in pairs for t in ab]
    return pl.pallas_call(
        body, name=name, grid=(m_dim // tm, n_dim // tn, nk),
        in_specs=[a_spec, b_spec] * n_pairs + [_ANY] * len(order),
        out_specs=pl.BlockSpec((tm, tn), lambda i, j, k: (i, j)),
        out_shape=jax.ShapeDtypeStruct((m_dim, n_dim), out_dtype),
        scratch_shapes=[pltpu.VMEM((tm, tn), F32)],
        compiler_params=_cparams(("parallel", "parallel", "arbitrary"), est),
    )(*flat, *order)


_ANY = pl.BlockSpec(memory_space=pl.ANY)


def _rms_fwd(x, gain, name, after=None):
    s_dim, d = x.shape
    tm = _tile(s_dim, 512, 8)

    def body(*refs):
        x_ref, g_ref, o_ref = refs[0], refs[1], refs[-1]
        v = x_ref[...]
        r = lax.rsqrt(jnp.mean(v * v, axis=-1, keepdims=True) + RMS_EPS)
        o_ref[...] = (v * r * g_ref[...]).astype(MXU_DTYPE)

    order = [] if after is None else [after]
    return pl.pallas_call(
        body, name=name, grid=(s_dim // tm,),
        in_specs=[pl.BlockSpec((tm, d), lambda i: (i, 0)), pl.BlockSpec((1, d), lambda i: (0, 0))] + [_ANY] * len(order),
        out_specs=pl.BlockSpec((tm, d), lambda i: (i, 0)),
        out_shape=jax.ShapeDtypeStruct((s_dim, d), MXU_DTYPE),
        compiler_params=_cparams(("parallel",), 6 * _nbytes((tm, d), F32)),
    )(x, gain, *order)


def _rms_bwd(x, gain, dys, res, scale, out_dtype, name, after=None):
    s_dim, d = x.shape
    tm = _tile(s_dim, 512, 8)
    n_dy = len(dys)
    has_res = res is not None
    order = [] if after is None else [after]

    def body(*refs):
        x_ref, g_ref = refs[0], refs[1]
        dy_refs = refs[2:2 + n_dy]
        res_ref = refs[2 + n_dy] if has_res else None
        dx_ref, dg_ref = refs[-2], refs[-1]

        @pl.when(pl.program_id(0) == 0)
        def _():
            dg_ref[...] = jnp.zeros_like(dg_ref)

        v = x_ref[...]
        r = lax.rsqrt(jnp.mean(v * v, axis=-1, keepdims=True) + RMS_EPS)
        xh = v * r
        dy = dy_refs[0][...].astype(F32)
        for extra in dy_refs[1:]:
            dy = dy + extra[...].astype(F32)
        gd = dy * g_ref[...]
        dx = scale * r * (gd - xh * jnp.mean(gd * xh, axis=-1, keepdims=True))
        if has_res:
            dx = dx + res_ref[...]
        dx_ref[...] = dx.astype(out_dtype)
        dg_ref[...] += scale * jnp.sum(dy * xh, axis=0, keepdims=True)

    row = pl.BlockSpec((tm, d), lambda i: (i, 0))
    vec = pl.BlockSpec((1, d), lambda i: (0, 0))
    ops = [x, gain] + list(dys) + ([res] if has_res else [])
    return pl.pallas_call(
        body, name=name, grid=(s_dim // tm,),
        in_specs=[row, vec] + [row] * (n_dy + int(has_res)) + [_ANY] * len(order),
        out_specs=[row, vec],
        out_shape=[jax.ShapeDtypeStruct((s_dim, d), out_dtype), jax.ShapeDtypeStruct((1, d), F32)],
        compiler_params=_cparams(("arbitrary",), (2 * len(ops) + 6) * _nbytes((tm, d), F32)),
    )(*ops, *order)


def _mm_rms_res(a, b, h, gain, scale, name):
    s_dim, k_dim = a.shape
    d = b.shape[1]
    tm = _tile(s_dim, 512, 8)
    tk = _tile(k_dim, 2816)
    nk = k_dim // tk

    def body(a_ref, b_ref, h_ref, g_ref, f_ref, o_ref, acc):
        k = pl.program_id(1)

        @pl.when(k == 0)
        def _():
            acc[...] = jnp.zeros_like(acc)

        acc[...] += _dot(a_ref[...], b_ref[...], "nn")

        @pl.when(k == nk - 1)
        def _():
            f = acc[...]
            r = lax.rsqrt(jnp.mean(f * f, axis=-1, keepdims=True) + RMS_EPS)
            f_ref[...] = f
            o_ref[...] = h_ref[...] + scale * (f * r * g_ref[...])

    row = pl.BlockSpec((tm, d), lambda i, k: (i, 0))
    est = (2 * (_nbytes((tm, tk), a.dtype) + _nbytes((tk, d), b.dtype)) + 8 * _nbytes((tm, d), F32))
    return pl.pallas_call(
        body, name=name, grid=(s_dim // tm, nk),
        in_specs=[pl.BlockSpec((tm, tk), lambda i, k: (i, k)), pl.BlockSpec((tk, d), lambda i, k: (k, 0)),
                  row, pl.BlockSpec((1, d), lambda i, k: (0, 0))],
        out_specs=[row, row],
        out_shape=[jax.ShapeDtypeStruct((s_dim, d), F32), jax.ShapeDtypeStruct((s_dim, d), F32)],
        scratch_shapes=[pltpu.VMEM((tm, d), F32)],
        compiler_params=_cparams(("parallel", "arbitrary"), est),
    )(a, b, h, gain)


def _ffn_up(xn, wg_t, wu_t, name):
    s_dim, d = xn.shape
    f_dim = wg_t.shape[0]
    tm = _tile(s_dim, 2048, 8)
    tf = _tile(f_dim, 256)

    def body(x_ref, wg_ref, wu_ref, g_ref, u_ref, a_ref):
        x = x_ref[...]
        g = _dot(x, wg_ref[...], "nt")
        u = _dot(x, wu_ref[...], "nt")
        g_ref[...] = g.astype(MXU_DTYPE)
        u_ref[...] = u.astype(MXU_DTYPE)
        a_ref[...] = (g * _sigmoid(g) * u).astype(MXU_DTYPE)

    w_spec = pl.BlockSpec((tf, d), lambda i, j: (j, 0))
    o_spec = pl.BlockSpec((tm, tf), lambda i, j: (i, j))
    o_shape = jax.ShapeDtypeStruct((s_dim, f_dim), MXU_DTYPE)
    est = 2 * _nbytes((tm, d), xn.dtype) + 4 * _nbytes((tf, d), wg_t.dtype) + 10 * _nbytes((tm, tf), F32)
    return pl.pallas_call(
        body, name=name, grid=(s_dim // tm, f_dim // tf),
        in_specs=[pl.BlockSpec((tm, d), lambda i, j: (i, 0)), w_spec, w_spec],
        out_specs=[o_spec, o_spec, o_spec], out_shape=[o_shape, o_shape, o_shape],
        compiler_params=_cparams(("parallel", "parallel"), est),
    )(xn, wg_t, wu_t)


def _ffn_act_bwd(df, wd, g, u, name):
    s_dim, d = df.shape
    f_dim = wd.shape[0]
    tm = _tile(s_dim, 2048, 8)
    tf = _tile(f_dim, 256)

    def body(df_ref, wd_ref, g_ref, u_ref, dg_ref, du_ref):
        dh = _dot(df_ref[...], wd_ref[...], "nt")
        gv = g_ref[...].astype(F32)
        uv = u_ref[...].astype(F32)
        sg = _sigmoid(gv)
        dg_ref[...] = (dh * uv * (sg * (1.0 + gv * (1.0 - sg)))).astype(MXU_DTYPE)
        du_ref[...] = (dh * gv * sg).astype(MXU_DTYPE)

    t_spec = pl.BlockSpec((tm, tf), lambda i, j: (i, j))
    o_shape = jax.ShapeDtypeStruct((s_dim, f_dim), MXU_DTYPE)
    est = 2 * _nbytes((tm, d), df.dtype) + 2 * _nbytes((tf, d), wd.dtype) + 12 * _nbytes((tm, tf), F32)
    return pl.pallas_call(
        body, name=name, grid=(s_dim // tm, f_dim // tf),
        in_specs=[pl.BlockSpec((tm, d), lambda i, j: (i, 0)), pl.BlockSpec((tf, d), lambda i, j: (j, 0)),
                  t_spec, t_spec],
        out_specs=[t_spec, t_spec], out_shape=[o_shape, o_shape],
        compiler_params=_cparams(("parallel", "parallel"), est),
    )(df, wd, g, u)


def _loss_head(y, target, name):
    s_dim, d = y.shape
    tm = _tile(s_dim, 512, 8)
    nt = s_dim // tm

    def body(y_ref, t_ref, dy_ref, loss_ref, acc):
        i = pl.program_id(0)

        @pl.when(i == 0)
        def _():
            acc[...] = jnp.zeros_like(acc)

        e = y_ref[...] - t_ref[...]
        dy_ref[...] = e * (1.0 / d)
        acc[...] += jnp.sum(e * e, axis=0, keepdims=True)

        @pl.when(i == nt - 1)
        def _():
            loss_ref[...] = jnp.sum(acc[...], axis=1, keepdims=True) * (0.5 / d)

    row = pl.BlockSpec((tm, d), lambda i: (i, 0))
    return pl.pallas_call(
        body, name=name, grid=(nt,), in_specs=[row, row],
        out_specs=[row, pl.BlockSpec((1, 1), lambda i: (0, 0))],
        out_shape=[jax.ShapeDtypeStruct((s_dim, d), F32), jax.ShapeDtypeStruct((1, 1), F32)],
        scratch_shapes=[pltpu.VMEM((1, d), F32)],
        compiler_params=_cparams(("arbitrary",), 8 * _nbytes((tm, d), F32)),
    )(y, target)


def _shift_down(v, sh, row):
    if sh == 0:
        return v
    return jnp.where(row >= sh, pltpu.roll(v, sh, 0), 0.0)


def _shift_up(v, sh, row):
    if sh == 0:
        return v
    n = v.shape[0]
    return jnp.where(row < n - sh, pltpu.roll(v, n - sh, 0), 0.0)


def _conv_fwd(gx, pconv, width, name):
    s_dim, cp2 = gx.shape
    cp = cp2 // 2
    nc = cp // LANES

    def body(x_ref, p_ref, o_ref):
        x = x_ref[...]
        row = lax.broadcasted_iota(jnp.int32, x.shape, 0)
        y = jnp.zeros_like(x) + p_ref[pl.ds(width, 1), :]
        for k in range(width):
            y = y + p_ref[pl.ds(k, 1), :] * _shift_down(x, width - 1 - k, row)
        o_ref[...] = y

    return pl.pallas_call(
        body, name=name, grid=(nc,),
        in_specs=[pl.BlockSpec((s_dim, LANES), lambda j: (0, nc + j)), pl.BlockSpec((8, LANES), lambda j: (0, j))],
        out_specs=pl.BlockSpec((s_dim, LANES), lambda j: (0, j)),
        out_shape=jax.ShapeDtypeStruct((s_dim, cp), F32),
        compiler_params=_cparams(("parallel",), 10 * _nbytes((s_dim, LANES), F32)),
    )(gx, pconv)


def _conv_bwd(d1, d2, gx, pconv, width, name):
    s_dim, cp = d1.shape
    nc = cp // LANES

    def body(d1_ref, d2_ref, x_ref, p_ref, dx_ref, dp_ref):
        d = d1_ref[...] + d2_ref[...]
        x = x_ref[...]
        row = lax.broadcasted_iota(jnp.int32, x.shape, 0)
        dx = jnp.zeros_like(d)
        dp_ref[...] = jnp.zeros_like(dp_ref)
        for k in range(width):
            sh = width - 1 - k
            dx = dx + p_ref[pl.ds(k, 1), :] * _shift_up(d, sh, row)
            dp_ref[pl.ds(k, 1), :] = jnp.sum(d * _shift_down(x, sh, row), axis=0, keepdims=True)
        dp_ref[pl.ds(width, 1), :] = jnp.sum(d, axis=0, keepdims=True)
        dx_ref[...] = dx.astype(MXU_DTYPE)

    strip = pl.BlockSpec((s_dim, LANES), lambda j: (0, j))
    par = pl.BlockSpec((8, LANES), lambda j: (0, j))
    return pl.pallas_call(
        body, name=name, grid=(nc,),
        in_specs=[strip, strip, pl.BlockSpec((s_dim, LANES), lambda j: (0, nc + j)), par],
        out_specs=[strip, par],
        out_shape=[jax.ShapeDtypeStruct((s_dim, cp), MXU_DTYPE), jax.ShapeDtypeStruct((8, cp), F32)],
        compiler_params=_cparams(("parallel",), 14 * _nbytes((s_dim, LANES), F32)),
    )(d1, d2, gx, pconv)


def _lru_coeffs(ra, ia, p_ref):
    r = _sigmoid(ra + p_ref[pl.ds(0, 1), :])
    i = _sigmoid(ia + p_ref[pl.ds(1, 1), :])
    sp = _softplus(-p_ref[pl.ds(2, 1), :])
    log_a = -LRU_C * r * sp
    a = jnp.exp(log_a)
    mult = jnp.sqrt(-_expm1(2.0 * log_a))
    return r, i, sp, a, mult


def _scan_fwd(gx, rec, gates, pvec, name):
    s_dim, cp = rec.shape
    ts = _tile(s_dim, 256, 8)
    nt = s_dim // ts

    def body(gate_ref, rec_ref, ra_ref, ia_ref, p_ref, h_ref, y_ref, a_s, u_s, carry):
        @pl.when(pl.program_id(0) == 0)
        def _():
            carry[...] = jnp.zeros_like(carry)

        rec_v = rec_ref[...]
        _, i, _, a, mult = _lru_coeffs(ra_ref[...], ia_ref[...], p_ref)
        a_s[...] = a
        u_s[...] = mult * (i * rec_v)

        def step(t, h):
            h = a_s[pl.ds(t, 1), :] * h + u_s[pl.ds(t, 1), :]
            h_ref[pl.ds(t, 1), :] = h
            return h

        carry[pl.ds(0, 1), :] = lax.fori_loop(0, ts, step, carry[pl.ds(0, 1), :], unroll=8)
        y_ref[...] = (_gelu(gate_ref[...]) * h_ref[...]).astype(MXU_DTYPE)

    blk = pl.BlockSpec((ts, cp), lambda t: (t, 0))
    return pl.pallas_call(
        body, name=name, grid=(nt,),
        in_specs=[blk, blk, blk, pl.BlockSpec((ts, cp), lambda t: (t, 1)), pl.BlockSpec((8, cp), lambda t: (0, 0))],
        out_specs=[blk, blk],
        out_shape=[jax.ShapeDtypeStruct((s_dim, cp), F32), jax.ShapeDtypeStruct((s_dim, cp), MXU_DTYPE)],
        scratch_shapes=[pltpu.VMEM((ts, cp), F32), pltpu.VMEM((ts, cp), F32), pltpu.VMEM((8, cp), F32)],
        compiler_params=_cparams(("arbitrary",), 14 * _nbytes((ts, cp), F32)),
    )(gx, rec, gates, gates, pvec)


def _scan_bwd(dy, gx, hrec, rec, gates, pvec, name):
    s_dim, cp = rec.shape
    ts = _tile(s_dim, 128, 8)
    nt = s_dim // ts

    def body(dy_ref, gate_ref, h_ref, hp_ref, rec_ref, ra_ref, ia_ref, p_ref,
             dgate_ref, dra_ref, dia_ref, drec_ref, dp_ref, a_s, d_s, carry):
        t_id = pl.program_id(0)

        @pl.when(t_id == 0)
        def _():
            carry[...] = jnp.zeros_like(carry)
            dp_ref[...] = jnp.zeros_like(dp_ref)

        rec_v = rec_ref[...]
        r, i, sp, a, mult = _lru_coeffs(ra_ref[...], ia_ref[...], p_ref)
        gate = gate_ref[...]
        dyv = dy_ref[...]
        h = h_ref[...]
        dgate_ref[...] = (dyv * h * _gelu_grad(gate)).astype(MXU_DTYPE)
        a_s[...] = a
        d_s[...] = dyv * _gelu(gate)

        def step(k, c):
            t = ts - 1 - k
            d = d_s[pl.ds(t, 1), :] + c
            d_s[pl.ds(t, 1), :] = d
            return a_s[pl.ds(t, 1), :] * d

        carry[pl.ds(0, 1), :] = lax.fori_loop(0, ts, step, carry[pl.ds(0, 1), :], unroll=8)
        dh = d_s[...]
        row = lax.broadcasted_iota(jnp.int32, h.shape, 0)
        first = jnp.where(t_id == nt - 1, 0.0, 1.0) * hp_ref[pl.ds(7, 1), :]
        h_prev = jnp.where(row == 0, first, pltpu.roll(h, 1, 0))
        dix = dh * mult
        dla = dh * h_prev * a - dh * (i * rec_v) * (a * a) / mult
        dra = dla * (-LRU_C * sp) * r * (1.0 - r)
        dia = dix * rec_v * i * (1.0 - i)
        dra_ref[...] = dra.astype(MXU_DTYPE)
        dia_ref[...] = dia.astype(MXU_DTYPE)
        drec_ref[...] = dix * i
        dsp = jnp.sum(dla * (-LRU_C * r), axis=0, keepdims=True)
        dp_ref[pl.ds(0, 1), :] += jnp.sum(dra, axis=0, keepdims=True)
        dp_ref[pl.ds(1, 1), :] += jnp.sum(dia, axis=0, keepdims=True)
        dp_ref[pl.ds(2, 1), :] += dsp * (-_sigmoid(-p_ref[pl.ds(2, 1), :]))

    blk = pl.BlockSpec((ts, cp), lambda t: (nt - 1 - t, 0))
    prev = pl.BlockSpec((8, cp), lambda t: (jnp.maximum((nt - 1 - t) * (ts // 8) - 1, 0), 0))
    par = pl.BlockSpec((8, cp), lambda t: (0, 0))
    lo = jax.ShapeDtypeStruct((s_dim, cp), MXU_DTYPE)
    return pl.pallas_call(
        body, name=name, grid=(nt,),
        in_specs=[blk, blk, blk, prev, blk, blk, pl.BlockSpec((ts, cp), lambda t: (nt - 1 - t, 1)), par],
        out_specs=[blk, blk, blk, blk, par],
        out_shape=[lo, lo, lo, jax.ShapeDtypeStruct((s_dim, cp), F32), jax.ShapeDtypeStruct((8, cp), F32)],
        scratch_shapes=[pltpu.VMEM((ts, cp), F32), pltpu.VMEM((ts, cp), F32), pltpu.VMEM((8, cp), F32)],
        compiler_params=_cparams(("arbitrary",), 40 * _nbytes((ts, cp), F32)),
    )(dy, gx, hrec, hrec, rec, gates, gates, pvec)


def _fgate_fwd(fpre, bias, name):
    s_dim, w = fpre.shape
    ts = _tile(s_dim, 512, 8)

    def body(f_ref, b_ref, c_ref, lf_s, carry):
        @pl.when(pl.program_id(0) == 0)
        def _():
            carry[...] = jnp.zeros_like(carry)

        lf_s[...] = -_softplus(-(f_ref[...] + b_ref[pl.ds(0, 1), :]))

        def step(t, c):
            c = c + lf_s[pl.ds(t, 1), :]
            c_ref[pl.ds(t, 1), :] = c
            return c

        carry[pl.ds(0, 1), :] = lax.fori_loop(0, ts, step, carry[pl.ds(0, 1), :], unroll=8)

    blk = pl.BlockSpec((ts, w), lambda t: (t, 0))
    return pl.pallas_call(
        body, name=name, grid=(s_dim // ts,),
        in_specs=[blk, pl.BlockSpec((8, w), lambda t: (0, 0))], out_specs=blk,
        out_shape=jax.ShapeDtypeStruct((s_dim, w), F32),
        scratch_shapes=[pltpu.VMEM((ts, w), F32), pltpu.VMEM((8, w), F32)],
        compiler_params=_cparams(("arbitrary",), 12 * _nbytes((ts, w), F32)),
    )(fpre, bias)


def _fgate_bwd(dc, fpre, bias, name):
    s_dim, w = fpre.shape
    ts = _tile(s_dim, 512, 8)
    nt = s_dim // ts

    def body(dc_ref, f_ref, b_ref, df_ref, db_ref, d_s, carry):
        @pl.when(pl.program_id(0) == 0)
        def _():
            carry[...] = jnp.zeros_like(carry)
            db_ref[...] = jnp.zeros_like(db_ref)

        d_s[...] = dc_ref[...]

        def step(k, c):
            t = ts - 1 - k
            c = c + d_s[pl.ds(t, 1), :]
            d_s[pl.ds(t, 1), :] = c
            return c

        carry[pl.ds(0, 1), :] = lax.fori_loop(0, ts, step, carry[pl.ds(0, 1), :], unroll=8)
        df = d_s[...] * _sigmoid(-(f_ref[...] + b_ref[pl.ds(0, 1), :]))
        df_ref[...] = df
        db_ref[pl.ds(0, 1), :] += jnp.sum(df, axis=0, keepdims=True)

    blk = pl.BlockSpec((ts, w), lambda t: (nt - 1 - t, 0))
    par = pl.BlockSpec((8, w), lambda t: (0, 0))
    return pl.pallas_call(
        body, name=name, grid=(nt,), in_specs=[blk, blk, par], out_specs=[blk, par],
        out_shape=[jax.ShapeDtypeStruct((s_dim, w), F32), jax.ShapeDtypeStruct((8, w), F32)],
        scratch_shapes=[pltpu.VMEM((ts, w), F32), pltpu.VMEM((8, w), F32)],
        compiler_params=_cparams(("arbitrary",), 12 * _nbytes((ts, w), F32)),
    )(dc, fpre, bias)


def _head_lanes(hh, dh):
    lane = lax.broadcasted_iota(jnp.int32, (1, LANES), 1)
    return (lane >= hh * dh) & (lane < (hh + 1) * dh)


def _pair_attn_fwd(q, kv, v_t, c_col, c_row, name):
    s_dim, da = q.shape
    n_h = c_col.shape[0]
    dh = da // n_h
    assert LANES % dh == 0 and da % LANES == 0
    hb = LANES // dh
    n_blocks = da // LANES
    t = _tile(s_dim, 1024, LANES)
    nb = s_dim // t

    pairs = [(i, j) for i in range(nb) for j in range(i + 1)]
    i_tab = jnp.asarray([p[0] for p in pairs], jnp.int32)
    j_tab = jnp.asarray([p[1] for p in pairs], jnp.int32)

    def body(i_ref, j_ref, q_ref, k_ref, vt_ref, cq_ref, ck_ref, o_ref, lse_ref, m_s, l_s, acc):
        i, j = i_ref[pl.program_id(1)], j_ref[pl.program_id(1)]

        @pl.when(j == 0)
        def _():
            m_s[...] = jnp.full_like(m_s, -jnp.inf)
            l_s[...] = jnp.zeros_like(l_s)
            acc[...] = jnp.zeros_like(acc)

        def tile(masked):
            qv = q_ref[...]
            for hh in range(hb):
                st = _dot(k_ref[...], jnp.where(_head_lanes(hh, dh), qv, jnp.zeros_like(qv)), "nt")
                st = st + (cq_ref[hh] - ck_ref[hh])
                if masked:
                    keep = lax.broadcasted_iota(jnp.int32, (t, t), 0) <= lax.broadcasted_iota(jnp.int32, (t, t), 1)
                    st = jnp.where(keep, st, -jnp.inf)
                m_prev = m_s[hh]
                m_new = jnp.maximum(m_prev, jnp.max(st, axis=0, keepdims=True))
                alpha = jnp.exp(m_prev - m_new)
                p = jnp.exp(st - m_new)
                l_s[hh] = alpha * l_s[hh] + jnp.sum(p, axis=0, keepdims=True)
                acc[hh] = alpha * acc[hh] + _dot(vt_ref[...], p, "nn")
                m_s[hh] = m_new

        pl.when(j < i)(functools.partial(tile, False))
        pl.when(j == i)(functools.partial(tile, True))

        @pl.when(j == i)
        def _():
            feat = lax.broadcasted_iota(jnp.int32, (LANES, 1), 0)
            out_t = jnp.zeros((LANES, t), F32)
            for hh in range(hb):
                out_t = jnp.where((feat >= hh * dh) & (feat < (hh + 1) * dh), acc[hh] / l_s[hh], out_t)
                lse_ref[hh] = m_s[hh] + jnp.log(l_s[hh])
            o_ref[...] = out_t.T

    q_spec = pl.BlockSpec((t, LANES), lambda b, p, it, jt: (it[p], b))
    k_spec = pl.BlockSpec((t, LANES), lambda b, p, it, jt: (jt[p], b))
    vt_spec = pl.BlockSpec((LANES, t), lambda b, p, it, jt: (b, jt[p]))
    cq_spec = pl.BlockSpec((hb, 1, t), lambda b, p, it, jt: (b, 0, it[p]))
    ck_spec = pl.BlockSpec((hb, t, 1), lambda b, p, it, jt: (b, jt[p], 0))
    return pl.pallas_call(
        body, name=name,
        grid_spec=pltpu.PrefetchScalarGridSpec(
            num_scalar_prefetch=2, grid=(n_blocks, len(pairs)),
            in_specs=[q_spec, k_spec, vt_spec, cq_spec, ck_spec], out_specs=[q_spec, cq_spec],
            scratch_shapes=[pltpu.VMEM((hb, 1, t), F32), pltpu.VMEM((hb, 1, t), F32), pltpu.VMEM((hb, LANES, t), F32)]),
        out_shape=[jax.ShapeDtypeStruct((s_dim, da), F32), jax.ShapeDtypeStruct((n_h, 1, s_dim), F32)],
        compiler_params=_cparams(("parallel", "arbitrary"), 10 * hb * _nbytes((t, t), F32)),
    )(i_tab, j_tab, q, kv, v_t, c_row, c_col)


def _attn_delta(do, o, n_h, name):
    s_dim, da = o.shape
    dh = da // n_h
    hb = LANES // dh
    t = _tile(s_dim, 512, LANES)

    def body(do_ref, o_ref, d_ref):
        prod_t = (do_ref[...].astype(MXU_DTYPE).astype(F32) * o_ref[...]).T
        for hh in range(hb):
            d_ref[hh] = jnp.sum(prod_t[hh * dh:(hh + 1) * dh], axis=0, keepdims=True)

    blk = pl.BlockSpec((t, LANES), lambda b, i: (i, b))
    return pl.pallas_call(
        body, name=name, grid=(da // LANES, s_dim // t), in_specs=[blk, blk],
        out_specs=pl.BlockSpec((hb, 1, t), lambda b, i: (b, 0, i)),
        out_shape=jax.ShapeDtypeStruct((n_h, 1, s_dim), F32),
        compiler_params=_cparams(("parallel", "parallel"), 8 * _nbytes((t, LANES), F32)),
    )(do, o)


def _pair_attn_bwd(q, kv, c_col, c_row, lse, delta, do, scale, name):
    s_dim, da = q.shape
    n_h = c_col.shape[0]
    dh = da // n_h
    hb = LANES // dh
    n_blocks = da // LANES
    t = _tile(s_dim, 1024, LANES)
    nb = s_dim // t

    pairs = [(i, j) for j in range(nb) for i in range(j, nb)]
    i_tab = jnp.asarray([p[0] for p in pairs], jnp.int32)
    j_tab = jnp.asarray([p[1] for p in pairs], jnp.int32)

    def body(i_ref, j_ref, q_ref, k_ref, v_ref, cq_ref, ck_ref, lse_ref, dl_ref, do_ref,
             dq_ref, dcq_ref, dk_ref, dv_ref, dck_ref, dk_acc, dv_acc, dck_acc):
        i, j = i_ref[pl.program_id(1)], j_ref[pl.program_id(1)]

        @pl.when(pl.program_id(1) == 0)
        def _():
            dq_ref[...] = jnp.zeros_like(dq_ref)
            dcq_ref[...] = jnp.zeros_like(dcq_ref)

        @pl.when(i == j)
        def _():
            dk_acc[...] = jnp.zeros_like(dk_acc)
            dv_acc[...] = jnp.zeros_like(dv_acc)
            dck_acc[...] = jnp.zeros_like(dck_acc)

        def tile(masked):
            start = pl.multiple_of(i * t, t)
            qv, kv_ = q_ref[...], k_ref[...]
            dov = do_ref[...].astype(MXU_DTYPE)
            for hh in range(hb):
                lanes = _head_lanes(hh, dh)
                qm = jnp.where(lanes, qv, jnp.zeros_like(qv))
                km = jnp.where(lanes, kv_, jnp.zeros_like(kv_))
                dom = jnp.where(lanes, dov, jnp.zeros_like(dov))
                st = _dot(kv_, qm, "nt") + (cq_ref[hh] - ck_ref[hh])
                if masked:
                    keep = lax.broadcasted_iota(jnp.int32, (t, t), 0) <= lax.broadcasted_iota(jnp.int32, (t, t), 1)
                    st = jnp.where(keep, st, -jnp.inf)
                pt = jnp.exp(st - lse_ref[hh])
                dst = pt * (_dot(v_ref[...], dom, "nt") - dl_ref[hh])
                dv_acc[...] += _dot(pt, dom, "nn")
                dk_acc[...] += _dot(dst, qm, "nn")
                dq_ref[pl.ds(start, t), :] += _dot(dst, km, "tn") * scale
                dcq_ref[hh, :, pl.ds(start, t)] += jnp.sum(dst, axis=0, keepdims=True)
                dck_acc[hh] -= jnp.sum(dst, axis=1, keepdims=True)

        pl.when(i > j)(functools.partial(tile, False))
        pl.when(i == j)(functools.partial(tile, True))

        @pl.when(i == nb - 1)
        def _():
            dk_ref[...] = dk_acc[...]
            dv_ref[...] = dv_acc[...]
            dck_ref[...] = dck_acc[...]

    q_spec = pl.BlockSpec((t, LANES), lambda b, p, it, jt: (it[p], b))
    qrow_spec = pl.BlockSpec((hb, 1, t), lambda b, p, it, jt: (b, 0, it[p]))
    k_spec = pl.BlockSpec((t, LANES), lambda b, p, it, jt: (jt[p], b))
    v_spec = pl.BlockSpec((t, LANES), lambda b, p, it, jt: (jt[p], n_blocks + b))
    kcol_spec = pl.BlockSpec((hb, t, 1), lambda b, p, it, jt: (b, jt[p], 0))
    wide = jax.ShapeDtypeStruct((s_dim, da), F32)
    return pl.pallas_call(
        body, name=name,
        grid_spec=pltpu.PrefetchScalarGridSpec(
            num_scalar_prefetch=2, grid=(n_blocks, len(pairs)),
            in_specs=[q_spec, k_spec, v_spec, qrow_spec, kcol_spec, qrow_spec, qrow_spec, q_spec],
            out_specs=[pl.BlockSpec((s_dim, LANES), lambda b, p, it, jt: (0, b)),
                       pl.BlockSpec((hb, 1, s_dim), lambda b, p, it, jt: (b, 0, 0)), k_spec, k_spec, kcol_spec],
            scratch_shapes=[pltpu.VMEM((t, LANES), F32), pltpu.VMEM((t, LANES), F32), pltpu.VMEM((hb, t, 1), F32)]),
        out_shape=[wide, jax.ShapeDtypeStruct((n_h, 1, s_dim), F32), wide, wide,
                   jax.ShapeDtypeStruct((n_h, s_dim, 1), F32)],
        compiler_params=_cparams(("parallel", "arbitrary"),
                                 10 * hb * _nbytes((t, t), F32) + 4 * _nbytes((s_dim, LANES), F32)),
    )(i_tab, j_tab, q, kv, kv, c_row, c_col, lse, delta, do)


_HBM = pl.BlockSpec(memory_space=pltpu.HBM)
_MESH_ID = pl.DeviceIdType.MESH


def _all_gather(block, name):
    r, w = block.shape

    def body(x_ref, out_ref, send_sems, recv_sems, local_sem):
        x, y, c = lax.axis_index("x"), lax.axis_index("y"), lax.axis_index("c")
        me, sibling = (x, y, c), (x, y, 1 - c)
        chips = [(1 - x, y), (x, 1 - y), (1 - x, 1 - y)]

        def slot(px, py, pc):
            return out_ref.at[4 * px + 2 * py + pc]

        def copy(k, blk, to, src=None):
            return pltpu.make_async_remote_copy(
                src_ref=slot(*blk) if src is None else src, dst_ref=slot(*blk),
                send_sem=send_sems.at[k], recv_sem=recv_sems.at[k], device_id=to, device_id_type=_MESH_ID)

        mine = pltpu.make_async_copy(x_ref, slot(*me), local_sem)
        mine.start()
        first = [copy(0, me, sibling, src=x_ref)]
        first += [copy(1 + n, me, (*chip, c), src=x_ref) for n, chip in enumerate(chips)]
        for cp in first:
            cp.start()
        passed = [copy(4 + n, (*chip, c), sibling) for n, chip in enumerate(chips)]
        for n, chip in enumerate(chips):
            copy(1 + n, (*chip, c), me).wait_recv()
            passed[n].start()
        copy(0, sibling, me).wait_recv()
        for n, chip in enumerate(chips):
            copy(4 + n, (*chip, 1 - c), me).wait_recv()
        for cp in first + passed:
            cp.wait_send()
        mine.wait()

    return pl.pallas_call(
        body, name=name, out_shape=jax.ShapeDtypeStruct((N_DEV, r, w), block.dtype),
        in_specs=[_HBM], out_specs=_HBM,
        scratch_shapes=[pltpu.SemaphoreType.DMA((7,)), pltpu.SemaphoreType.DMA((7,)), pltpu.SemaphoreType.DMA],
    )(block)


_SEM = pl.BlockSpec(memory_space=pltpu.SEMAPHORE)
_EFFECT = pltpu.SideEffectType.DATAFLOW_SIDE_EFFECTING


def _exchange_start(srcs, personalized, after, name):
    n = len(srcs)
    n_after = len(after)
    lands = [lax.empty((N_DEV,) + s.shape[-2:], s.dtype) for s in srcs]

    def body(*refs):
        src_refs, land_refs = refs[:n], refs[n:2 * n]
        outs = refs[2 * n + n_after:]
        send_sems, recv_sems, token = outs[:n], outs[n:2 * n], outs[-1]
        x, y, c = lax.axis_index("x"), lax.axis_index("y"), lax.axis_index("c")
        mine = 4 * x + 2 * y + c
        for ci in range(n):
            for k in range(1, N_DEV):
                px = 1 - x if k & 4 else x
                py = 1 - y if k & 2 else y
                pc = 1 - c if k & 1 else c
                src = src_refs[ci].at[4 * px + 2 * py + pc] if personalized else src_refs[ci]
                pltpu.make_async_remote_copy(
                    src_ref=src, dst_ref=land_refs[ci].at[mine], send_sem=send_sems[ci], recv_sem=recv_sems[ci],
                    device_id=(px, py, pc), device_id_type=_MESH_ID).start()
        token[...] = jnp.zeros_like(token)

    sem = pltpu.SemaphoreType.DMA(())
    out_shape = ([sem] * (2 * n) + [pltpu.HBM(s.shape, s.dtype) for s in srcs]
                 + [pltpu.HBM(l.shape, l.dtype) for l in lands] + [jax.ShapeDtypeStruct((8, LANES), F32)])
    res = pl.pallas_call(
        body, name=name, out_shape=tuple(out_shape),
        in_specs=[_HBM] * (2 * n) + [_ANY] * n_after,
        out_specs=tuple([_SEM] * (2 * n) + [_HBM] * (2 * n) + [pl.BlockSpec(memory_space=pltpu.VMEM)]),
        input_output_aliases={i: 2 * n + i for i in range(2 * n)},
        compiler_params=pltpu.CompilerParams(has_side_effects=_EFFECT),
    )(*[pltpu.with_memory_space_constraint(s, pltpu.HBM) for s in srcs],
      *[pltpu.with_memory_space_constraint(l, pltpu.HBM) for l in lands], *after)
    handles = [(res[ci], res[n + ci], res[2 * n + ci], res[3 * n + ci]) for ci in range(n)]
    return handles, res[-1]


def _exchange_wait(handle, after, name):
    send_sem, recv_sem, src_thru, land_thru = handle

    def body(src_ref, land_ref, send_ref, recv_ref, after_ref, src_out, land_out):
        seven = land_ref.at[pl.ds(0, N_DEV - 1)]
        copies = pltpu.make_async_remote_copy(
            src_ref=seven, dst_ref=seven, send_sem=send_ref, recv_sem=recv_ref,
            device_id=(lax.axis_index("x"), lax.axis_index("y"), lax.axis_index("c")), device_id_type=_MESH_ID)
        copies.wait_send()
        copies.wait_recv()

    return pl.pallas_call(
        body, name=name,
        out_shape=(pltpu.HBM(src_thru.shape, src_thru.dtype), pltpu.HBM(land_thru.shape, land_thru.dtype)),
        in_specs=(_HBM, _HBM, _SEM, _SEM, _ANY), out_specs=(_HBM, _HBM), input_output_aliases={0: 0, 1: 1},
        compiler_params=pltpu.CompilerParams(has_side_effects=_EFFECT),
    )(src_thru, land_thru, send_sem, recv_sem, after)[1]


def _own_slot(land, own, me):
    return lax.dynamic_update_index_in_dim(land, own, me, axis=0)


def _sum_slots(slots, name):
    n, r, w = slots.shape
    tr = _tile(r, 128, WIRE_ROW_ALIGN)

    def body(s_ref, o_ref):
        acc = s_ref[0].astype(F32)
        for d in range(1, n):
            acc = acc + s_ref[d].astype(F32)
        o_ref[...] = acc

    return pl.pallas_call(
        body, name=name, grid=(r // tr,),
        in_specs=[pl.BlockSpec((n, tr, w), lambda i: (0, i, 0))],
        out_specs=pl.BlockSpec((tr, w), lambda i: (i, 0)),
        out_shape=jax.ShapeDtypeStruct((r, w), F32),
        compiler_params=_cparams(("parallel",), 2 * _nbytes((n, tr, w), slots.dtype) + 4 * _nbytes((tr, w), F32)),
    )(slots)


def _adamw(w, g, m, v, name):
    r, c = w.shape
    tr = _tile(r, 512, 8)

    def body(w_ref, g_ref, m_ref, v_ref, d_ref, mo_ref, vo_ref):
        gv = g_ref[...]
        m_new = ADAM_B1 * m_ref[...] + (1.0 - ADAM_B1) * gv
        v_new = ADAM_B2 * v_ref[...] + (1.0 - ADAM_B2) * (gv * gv)
        m_hat = m_new / (1.0 - ADAM_B1 ** ADAM_STEP)
        v_hat = v_new / (1.0 - ADAM_B2 ** ADAM_STEP)
        d_ref[...] = -ADAM_LR * (m_hat / (jnp.sqrt(v_hat) + ADAM_EPS) + ADAM_WD * w_ref[...])
        mo_ref[...] = m_new
        vo_ref[...] = v_new

    blk = pl.BlockSpec((tr, c), lambda i: (i, 0))
    shp = jax.ShapeDtypeStruct((r, c), F32)
    return pl.pallas_call(
        body, name=name, grid=(r // tr,), in_specs=[blk] * 4, out_specs=[blk] * 3, out_shape=[shp] * 3,
        compiler_params=_cparams(("parallel",), 16 * _nbytes((tr, _round_up(c, LANES)), F32)),
    )(w, g, m, v)


def _pack_rows(parts, width, dtype, row_align):
    rows, spans, off = [], [], 0
    for p in parts:
        flat = p.reshape(-1).astype(dtype)
        n_rows = _round_up(-(-flat.shape[0] // width), row_align)
        flat = jnp.pad(flat, (0, n_rows * width - flat.shape[0]))
        rows.append(flat.reshape(n_rows, width))
        spans.append((off, n_rows))
        off += n_rows
    return jnp.concatenate(rows, axis=0), spans


def _unpack_rows(mat, span, shape):
    off, n_rows = span
    n = 1
    for s in shape:
        n *= s
    return mat[..., off:off + n_rows, :].reshape(mat.shape[:-2] + (-1,))[..., :n].reshape(mat.shape[:-2] + tuple(shape))


def _block_diag(w, size):
    n, b, _ = w.shape
    eye = jnp.eye(n, dtype=w.dtype)
    dense = (w[:, :, None, :] * eye[:, None, :, None]).reshape(n * b, n * b)
    return jnp.pad(dense, ((0, size - n * b), (0, size - n * b)))


def _diag_blocks(dense, n, b):
    return jnp.stack([dense[k * b:(k + 1) * b, k * b:(k + 1) * b] for k in range(n)])


def _pad_rows(a, rows):
    return jnp.pad(a, ((0, rows - a.shape[0]), (0, 0)))


def _pad_cols(a, cols):
    return jnp.pad(a, ((0, 0), (0, cols - a.shape[1])))


def _train_step(a):
    x = a["x"][0]
    target = a["loss_target"][0]
    s_dim, d = x.shape
    n_layers = a["ffn1_pre_g"].shape[0]
    f_shard = a["ffn1_w_gate"].shape[2]
    c_shard = a["rg_conv_b"].shape[1]
    c_dim = c_shard * N_DEV
    cp = _round_up(c_dim, LANES)
    conv_width = a["rg_conv_w"].shape[1]
    n_blocks, lru_block = a["rg_w_a"].shape[1], a["rg_w_a"].shape[2]
    d_attn = a["attn_w_q"].shape[2]
    n_heads = a["b_fgate"].shape[0]
    d_head = d_attn // n_heads
    attn_scale = d_head ** -0.5
    assert conv_width < 8 and n_heads <= LANES and n_layers == 2
    assert d_attn == d
    me = 4 * lax.axis_index("x") + 2 * lax.axis_index("y") + lax.axis_index("c")

    shard = {"rg_w_in": a["rg_w_in"][0].T, "rg_w_out": a["rg_w_out"][0], "w_kv": a["w_kv"].T,
             "attn_w_q": a["attn_w_q"][0], "attn_w_o": a["attn_w_o"][0]}
    for l in range(n_layers):
        for f in ("ffn1", "ffn2"):
            shard[(f, "gate", l)] = a[f + "_w_gate"][l].T
            shard[(f, "up", l)] = a[f + "_w_up"][l].T
            shard[(f, "down", l)] = a[f + "_w_down"][l]

    def ffn_names(f, l):
        return [(f, "gate", l), (f, "up", l), (f, "down", l)]

    def chunk_layout(names):
        spans, off = [], 0
        for nm in names:
            spans.append((nm, off, shard[nm].shape[0]))
            off += _round_up(shard[nm].shape[0], WIRE_ROW_ALIGN)
        return spans, off

    def pack_chunk(names, parts):
        return jnp.concatenate(
            [_pad_rows(parts[nm].astype(WIRE_DTYPE), _round_up(parts[nm].shape[0], WIRE_ROW_ALIGN)) for nm in names], axis=0)

    full = {}

    def unpack_chunk(names, gathered):
        for nm, o, n_rows in chunk_layout(names)[0]:
            full[nm] = gathered[:, o:o + n_rows, :].reshape(N_DEV * n_rows, d)

    fwd_chunks = [ffn_names("ffn1", 0)[:2], ffn_names("ffn1", 0)[2:], ["rg_w_in"], ["rg_w_out"],
                  ffn_names("ffn2", 0) + ["w_kv"], ffn_names("ffn1", 1) + ["attn_w_q", "attn_w_o"], ffn_names("ffn2", 1)]
    fwd_packs = [pack_chunk(names, shard) for names in fwd_chunks]
    unpack_chunk(fwd_chunks[0], _all_gather(fwd_packs[0], "gather_weights_first"))

    small_parts = [a["rg_conv_w"][0], a["rg_conv_b"][0], a["rg_b_a"][0], a["rg_b_x"][0], a["rg_lambda"][0], a["w_fgate"]]
    small_pack, small_spans = _pack_rows(small_parts, d, F32, 8)
    small_all = _all_gather(small_pack, "gather_small")
    fwd_handles, fwd_token = _exchange_start(fwd_packs[1:], False, [full[("ffn1", "up", 0)], small_all],
                                             "gather_weights_start")

    def land_weights(n, after):
        land = _exchange_wait(fwd_handles[n - 1], after, f"gather_weights_wait_{n}")
        unpack_chunk(fwd_chunks[n], _own_slot(land, fwd_packs[n], me))

    sm = [_unpack_rows(small_all, sp, p.shape) for sp, p in zip(small_spans, small_parts)]
    conv_w = jnp.moveaxis(sm[0], 0, 1).reshape(conv_width, c_dim)
    conv_b, b_a, b_x, lam = (v.reshape(1, c_dim) for v in sm[1:5])
    w_f = sm[5].reshape(d, n_heads)

    pconv = _pad_rows(_pad_cols(jnp.concatenate([conv_w, conv_b], axis=0), cp), 8)
    pvec = _pad_rows(_pad_cols(jnp.concatenate([b_a, b_x, lam], axis=0), cp), 8)
    wa_dense = _block_diag(a["rg_w_a"][0], cp).astype(MXU_DTYPE)
    wx_dense = _block_diag(a["rg_w_x"][0], cp).astype(MXU_DTYPE)
    wax = jnp.concatenate([wa_dense, wx_dense], axis=1)
    w_f_t = _pad_rows(w_f.T.astype(MXU_DTYPE), LANES)
    b_f = _pad_rows(_pad_cols(a["b_fgate"].reshape(1, n_heads), LANES), 8)

    def gain(name, l):
        return a[name][l].reshape(1, d)

    def ffn_fwd(h, f, l, after=None, down_chunk=None):
        xn = _rms_fwd(h, gain(f + "_pre_g", l), f"{f}_{l}_pre_norm", after)
        g, u, act = _ffn_up(xn, full[(f, "gate", l)], full[(f, "up", l)], f"{f}_{l}_up")
        if down_chunk is not None:
            land_weights(down_chunk, act)
        fo, h_new = _mm_rms_res(act, full[(f, "down", l)], h, gain(f + "_post_g", l), 0.5, f"{f}_{l}_down")
        return h_new, (h, xn, g, u, act, fo)

    h0 = x
    h0a, sv_f1_0 = ffn_fwd(h0, "ffn1", 0, fwd_token, down_chunk=1)
    land_weights(2, h0a)
    w_in_gate = _pad_rows(full["rg_w_in"][:c_dim], cp)
    w_in_rec = _pad_rows(full["rg_w_in"][c_dim:], cp)
    w_in_t = jnp.concatenate([w_in_gate, w_in_rec], axis=0)
    hn_rg = _rms_fwd(h0a, gain("mix_pre_g", 0), "rg_pre_norm")
    gx = _mm([(hn_rg, w_in_t)], "nt", F32, "rg_in_proj")
    rec = _conv_fwd(gx, pconv, conv_width, "rg_conv")
    gates = _mm([(rec, wax)], "nn", F32, "rg_gate_proj")
    h_rec, y_rg = _scan_fwd(gx, rec, gates, pvec, "rg_scan")
    land_weights(3, y_rg)
    w_out = _pad_rows(full["rg_w_out"], cp)
    m_rg, h0b = _mm_rms_res(y_rg, w_out, h0a, gain("mix_post_g", 0), 1.0, "rg_out_proj")
    land_weights(4, h0b)
    h1, sv_f2_0 = ffn_fwd(h0b, "ffn2", 0)
    hn_kv = _rms_fwd(h1, a["kv_norm_g"].reshape(1, d), "kv_norm")
    kv = _mm([(hn_kv, full["w_kv"])], "nt", MXU_DTYPE, "kv_proj")
    fpre = _mm([(hn_kv, w_f_t)], "nt", F32, "fgate_proj")
    c_cum = _fgate_fwd(fpre, b_f, "fgate_cumsum")
    c_heads = c_cum[:, :n_heads].T
    c_col, c_row = c_heads[:, :, None], c_heads[:, None, :]
    land_weights(5, c_cum)
    h1a, sv_f1_1 = ffn_fwd(h1, "ffn1", 1)
    hn_at = _rms_fwd(h1a, gain("mix_pre_g", 1), "attn_pre_norm")
    q_s = _mm([(hn_at, full["attn_w_q"])], "nn", MXU_DTYPE, "q_proj", out_scale=attn_scale)
    o2, lse = _pair_attn_fwd(q_s, kv, kv[:, d_attn:].T, c_col, c_row, "attn_fwd")
    m_at, h1b = _mm_rms_res(o2, full["attn_w_o"], h1a, gain("mix_post_g", 1), 1.0, "attn_out_proj")
    land_weights(6, h1b)
    y, sv_f2_1 = ffn_fwd(h1b, "ffn2", 1)
    dy, loss_part = _loss_head(y, target, "loss_head")

    grads_big = {}
    grads_rep = {}

    bwd_chunks = [ffn_names("ffn2", 1), ["attn_w_q", "attn_w_o"] + ffn_names("ffn1", 1),
                  ["w_kv"] + ffn_names("ffn2", 0), ["rg_w_in", "rg_w_out"], ffn_names("ffn1", 0)[2:],
                  ffn_names("ffn1", 0)[:2]]
    bwd_sends, bwd_handles = [], []

    def send_grads(after):
        n = len(bwd_sends)
        send = jnp.concatenate(
            [jnp.pad(grads_big[nm].reshape(N_DEV, n_rows, d), ((0, 0), (0, _round_up(n_rows, WIRE_ROW_ALIGN) - n_rows), (0, 0)))
             for nm, _, n_rows in chunk_layout(bwd_chunks[n])[0]], axis=1)
        handles, token = _exchange_start([send], True, [after], f"exchange_grads_start_{n}")
        bwd_sends.append(send)
        bwd_handles.append(handles[0])
        return token

    def ffn_bwd(dh_out, saved, f, l, after=None, send_now=False):
        h, xn, g, u, act, fo = saved
        df, d_post = _rms_bwd(fo, gain(f + "_post_g", l), [dh_out], None, 0.5, MXU_DTYPE, f"{f}_{l}_post_norm_bwd", after)
        dg, du = _ffn_act_bwd(df, full[(f, "down", l)], g, u, f"{f}_{l}_act_bwd")
        grads_big[(f, "down", l)] = _mm([(act, df)], "tn", WIRE_DTYPE, f"{f}_{l}_dw_down")
        sent = send_grads(df) if send_now else None
        grads_big[(f, "gate", l)] = _mm([(dg, xn)], "tn", WIRE_DTYPE, f"{f}_{l}_dw_gate", sent)
        grads_big[(f, "up", l)] = _mm([(du, xn)], "tn", WIRE_DTYPE, f"{f}_{l}_dw_up")
        sent = send_grads(df) if send_now else None
        dxn = _mm([(dg, full[(f, "gate", l)]), (du, full[(f, "up", l)])], "nn", F32, f"{f}_{l}_dx", sent)
        dh_in, d_pre = _rms_bwd(h, gain(f + "_pre_g", l), [dxn], dh_out, 1.0, F32, f"{f}_{l}_pre_norm_bwd")
        grads_rep[(f + "_post_g", l)] = d_post
        grads_rep[(f + "_pre_g", l)] = d_pre
        return dh_in

    dh = ffn_bwd(dy, sv_f2_1, "ffn2", 1)
    token = send_grads(dh)
    dm, d_post = _rms_bwd(m_at, gain("mix_post_g", 1), [dh], None, 1.0, MXU_DTYPE, "attn_post_norm_bwd", token)
    grads_rep[("mix_post_g", 1)] = d_post
    do2 = _mm([(dm, full["attn_w_o"])], "nt", F32, "attn_out_proj_dx")
    grads_big["attn_w_o"] = _mm([(o2, dm)], "tn", WIRE_DTYPE, "attn_out_proj_dw")
    delta = _attn_delta(do2, o2, n_heads, "attn_delta")
    dq2, dc_q, dk2, dv2, dc_k = _pair_attn_bwd(q_s, kv, c_col, c_row, lse, delta, do2, attn_scale, "attn_bwd")
    dc_heads = dc_q[:, 0, :] + dc_k[:, :, 0]
    dhn = _mm([(dq2, full["attn_w_q"])], "nt", F32, "q_proj_dx")
    grads_big["attn_w_q"] = _mm([(hn_at, dq2)], "tn", WIRE_DTYPE, "q_proj_dw")
    dh, d_pre = _rms_bwd(h1a, gain("mix_pre_g", 1), [dhn], dh, 1.0, F32, "attn_pre_norm_bwd")
    grads_rep[("mix_pre_g", 1)] = d_pre
    dh = ffn_bwd(dh, sv_f1_1, "ffn1", 1)
    token = send_grads(dh)
    dc_cum = _pad_cols(dc_heads.T, LANES)
    dfpre, db_f = _fgate_bwd(dc_cum, fpre, b_f, "fgate_cumsum_bwd")
    dhn_kv = _mm([(dk2, full["w_kv"][:d_attn]), (dv2, full["w_kv"][d_attn:])], "nn", F32, "kv_proj_dx")
    dhn_f = _mm([(dfpre, w_f_t)], "nn", F32, "fgate_proj_dx")
    grads_big["w_kv"] = jnp.concatenate([_mm([(dk2, hn_kv)], "tn", WIRE_DTYPE, "kv_proj_dw_k"),
                                         _mm([(dv2, hn_kv)], "tn", WIRE_DTYPE, "kv_proj_dw_v")], axis=0)
    dw_f_t = _mm([(dfpre, hn_kv)], "tn", F32, "fgate_proj_dw")
    dh, d_kvg = _rms_bwd(h1, a["kv_norm_g"].reshape(1, d), [dhn_kv, dhn_f], dh, 1.0, F32, "kv_norm_bwd", token)
    dh = ffn_bwd(dh, sv_f2_0, "ffn2", 0)
    token = send_grads(dh)
    dm, d_post = _rms_bwd(m_rg, gain("mix_post_g", 0), [dh], None, 1.0, MXU_DTYPE, "rg_post_norm_bwd", token)
    grads_rep[("mix_post_g", 0)] = d_post
    dy_rg = _mm([(dm, w_out)], "nt", F32, "rg_out_proj_dx")
    dw_out = _mm([(y_rg, dm)], "tn", WIRE_DTYPE, "rg_out_proj_dw")
    dgate, dra, dia, drec1, dpvec = _scan_bwd(dy_rg, gx, h_rec, rec, gates, pvec, "rg_scan_bwd")
    drec2 = _mm([(dra, wa_dense), (dia, wx_dense)], "nt", F32, "rg_gate_proj_dx")
    dwa_dense = _mm([(rec, dra)], "tn", F32, "rg_gate_proj_dwa")
    dwx_dense = _mm([(rec, dia)], "tn", F32, "rg_gate_proj_dwx")
    drec0, dpconv = _conv_bwd(drec1, drec2, gx, pconv, conv_width, "rg_conv_bwd")
    dhn = _mm([(dgate, w_in_gate), (drec0, w_in_rec)], "nn", F32, "rg_in_proj_dx")
    dw_in_gate = _mm([(dgate, hn_rg)], "tn", WIRE_DTYPE, "rg_in_proj_dw_gate")
    dw_in_rec = _mm([(drec0, hn_rg)], "tn", WIRE_DTYPE, "rg_in_proj_dw_rec")
    dh, d_pre = _rms_bwd(h0a, gain("mix_pre_g", 0), [dhn], dh, 1.0, F32, "rg_pre_norm_bwd")
    grads_rep[("mix_pre_g", 0)] = d_pre
    grads_big["rg_w_in"] = jnp.concatenate([dw_in_gate[:c_dim], dw_in_rec[:c_dim]], axis=0)
    grads_big["rg_w_out"] = dw_out[:c_dim]
    token = send_grads(dh)
    grad_x = ffn_bwd(dh, sv_f1_0, "ffn1", 0, token, send_now=True)

    g_shard = {}

    def land_grads(n, after):
        land = _exchange_wait(bwd_handles[n], after, f"exchange_grads_wait_{n}")
        own = lax.dynamic_index_in_dim(bwd_sends[n], me, axis=0, keepdims=False)
        g_chunk = _sum_slots(_own_slot(land, own, me), f"sum_weight_grads_{n}")
        for nm, o, n_rows in chunk_layout(bwd_chunks[n])[0]:
            g_shard[nm] = g_chunk[o:o + n_rows]

    for n in range(len(bwd_chunks) - 2):
        land_grads(n, grad_x)

    def gain_grad(name):
        return jnp.concatenate([grads_rep[(name, l)] for l in range(n_layers)], axis=0)

    rep_names = ["ffn1_pre_g", "ffn1_post_g", "mix_pre_g", "mix_post_g", "ffn2_pre_g", "ffn2_post_g"]
    rep_parts = [gain_grad(nm) for nm in rep_names]
    rep_names += ["kv_norm_g", "b_fgate", "rg_w_a", "rg_w_x", "rg_conv_w", "rg_conv_b", "rg_b_a", "rg_b_x", "rg_lambda", "w_fgate"]
    rep_parts += [
        d_kvg, db_f[0, :n_heads],
        _diag_blocks(dwa_dense, n_blocks, lru_block), _diag_blocks(dwx_dense, n_blocks, lru_block),
        dpconv[:conv_width, :c_dim], dpconv[conv_width, :c_dim],
        dpvec[0, :c_dim], dpvec[1, :c_dim], dpvec[2, :c_dim],
        dw_f_t[:n_heads].T]
    rep_pack, rep_spans = _pack_rows(rep_parts, d, F32, 8)
    rep_sum = _sum_slots(_all_gather(rep_pack, "gather_small_grads"), "sum_small_grads")
    g_rep = {nm: _unpack_rows(rep_sum, sp, p.shape) for nm, sp, p in zip(rep_names, rep_spans, rep_parts)}

    def my_cols(full_grad, n):
        return lax.dynamic_slice_in_dim(full_grad, me * n, n, axis=full_grad.ndim - 1)

    def ffn_grads(f):
        grad[f + "_w_gate"] = jnp.stack([g_shard[(f, "gate", l)].T for l in range(n_layers)])
        grad[f + "_w_up"] = jnp.stack([g_shard[(f, "up", l)].T for l in range(n_layers)])
        grad[f + "_w_down"] = jnp.stack([g_shard[(f, "down", l)] for l in range(n_layers)])

    grad = {}
    for nm in ("ffn1_pre_g", "ffn1_post_g", "mix_pre_g", "mix_post_g", "ffn2_pre_g", "ffn2_post_g"):
        grad[nm] = g_rep[nm]
    ffn_grads("ffn2")
    grad["rg_w_in"] = g_shard["rg_w_in"].T[None]
    grad["rg_conv_w"] = my_cols(g_rep["rg_conv_w"], c_shard)[None]
    for nm in ("rg_conv_b", "rg_b_a", "rg_b_x", "rg_lambda"):
        grad[nm] = my_cols(g_rep[nm], c_shard)[None]
    grad["rg_w_a"] = g_rep["rg_w_a"][None]
    grad["rg_w_x"] = g_rep["rg_w_x"][None]
    grad["rg_w_out"] = g_shard["rg_w_out"][None]
    grad["kv_norm_g"] = g_rep["kv_norm_g"].reshape(d)
    grad["w_kv"] = g_shard["w_kv"].T
    grad["w_fgate"] = lax.dynamic_slice_in_dim(g_rep["w_fgate"], me * (d // N_DEV), d // N_DEV, axis=0)
    grad["b_fgate"] = g_rep["b_fgate"]
    grad["attn_w_q"] = g_shard["attn_w_q"][None]
    grad["attn_w_o"] = g_shard["attn_w_o"][None]

    delta, new_m, new_v = {}, {}, {}

    def adamw(nm):
        w = a[nm]
        shape = w.shape
        two_d = (1, shape[0]) if w.ndim == 1 else (-1, shape[-1])
        ops = [pltpu.with_memory_space_constraint(t.reshape(two_d), pltpu.HBM)
               for t in (w, grad[nm], a["m_" + nm], a["v_" + nm])]
        dl, mo, vo = _adamw(*ops, "adamw_" + nm)
        delta[nm], new_m[nm], new_v[nm] = dl.reshape(shape), mo.reshape(shape), vo.reshape(shape)
        grad[nm] = grad[nm].reshape(shape)

    last_names = ("ffn1_w_gate", "ffn1_w_up", "ffn1_w_down")
    for nm in WEIGHT_NAMES:
        if nm not in last_names:
            adamw(nm)
    land_grads(len(bwd_chunks) - 2, delta["attn_w_o"])
    land_grads(len(bwd_chunks) - 1, delta["attn_w_o"])
    ffn_grads("ffn1")
    for nm in last_names:
        adamw(nm)

    loss = lax.psum(loss_part[0, 0], AXES)
    return (loss, grad_x[None], *[grad[n] for n in WEIGHT_NAMES], *[delta[n] for n in WEIGHT_NAMES],
            *[new_m[n] for n in WEIGHT_NAMES], *[new_v[n] for n in WEIGHT_NAMES])


def kernel(x, ffn1_pre_g, ffn1_w_gate, ffn1_w_up, ffn1_w_down, ffn1_post_g, mix_pre_g, mix_post_g, ffn2_pre_g, ffn2_w_gate, ffn2_w_up, ffn2_w_down, ffn2_post_g, rg_w_in, rg_conv_w, rg_conv_b, rg_w_a, rg_b_a, rg_w_x, rg_b_x, rg_lambda, rg_w_out, kv_norm_g, w_kv, w_fgate, b_fgate, attn_w_q, attn_w_o, loss_target, m_ffn1_pre_g, m_ffn1_w_gate, m_ffn1_w_up, m_ffn1_w_down, m_ffn1_post_g, m_mix_pre_g, m_mix_post_g, m_ffn2_pre_g, m_ffn2_w_gate, m_ffn2_w_up, m_ffn2_w_down, m_ffn2_post_g, m_rg_w_in, m_rg_conv_w, m_rg_conv_b, m_rg_w_a, m_rg_b_a, m_rg_w_x, m_rg_b_x, m_rg_lambda, m_rg_w_out, m_kv_norm_g, m_w_kv, m_w_fgate, m_b_fgate, m_attn_w_q, m_attn_w_o, v_ffn1_pre_g, v_ffn1_w_gate, v_ffn1_w_up, v_ffn1_w_down, v_ffn1_post_g, v_mix_pre_g, v_mix_post_g, v_ffn2_pre_g, v_ffn2_w_gate, v_ffn2_w_up, v_ffn2_w_down, v_ffn2_post_g, v_rg_w_in, v_rg_conv_w, v_rg_conv_b, v_rg_w_a, v_rg_b_a, v_rg_w_x, v_rg_b_x, v_rg_lambda, v_rg_w_out, v_kv_norm_g, v_w_kv, v_w_fgate, v_b_fgate, v_attn_w_q, v_attn_w_o):
    return _train_step(dict(locals()))
```

```python
import functools

import jax
import jax.numpy as jnp
from jax import lax
from jax.experimental import pallas as pl
from jax.experimental.pallas import tpu as pltpu

F32 = jnp.float32
MXU_DTYPE = jnp.bfloat16
WIRE_DTYPE = jnp.bfloat16
N_DEV = 8
AXES = ("x", "y", "c")
LANES = 128
WIRE_ROW_ALIGN = 16
VMEM_LIMIT_MIN = 32 * 2 ** 20
VMEM_LIMIT_MAX = 56 * 2 ** 20

RMS_EPS = 1e-6
LRU_C = 8.0
ADAM_LR, ADAM_B1, ADAM_B2, ADAM_EPS, ADAM_WD, ADAM_STEP = 0.001, 0.9, 0.999, 1e-08, 0.01, 10

WEIGHT_NAMES = (
    "ffn1_pre_g", "ffn1_w_gate", "ffn1_w_up", "ffn1_w_down", "ffn1_post_g", "mix_pre_g", "mix_post_g",
    "ffn2_pre_g", "ffn2_w_gate", "ffn2_w_up", "ffn2_w_down", "ffn2_post_g", "rg_w_in", "rg_conv_w",
    "rg_conv_b", "rg_w_a", "rg_b_a", "rg_w_x", "rg_b_x", "rg_lambda", "rg_w_out", "kv_norm_g", "w_kv",
    "w_fgate", "b_fgate", "attn_w_q", "attn_w_o")


def _round_up(n, m):
    return (n + m - 1) // m * m


def _tile(dim, target, align=LANES):
    if dim <= target:
        return dim
    best = None
    t = align
    while t <= target:
        if dim % t == 0:
            best = t
        t += align
    return dim if best is None else best


def _cparams(semantics, vmem_estimate):
    limit = min(VMEM_LIMIT_MAX, max(VMEM_LIMIT_MIN, 2 * int(vmem_estimate)))
    return pltpu.CompilerParams(dimension_semantics=semantics, vmem_limit_bytes=limit)


def _nbytes(shape, dtype):
    n = 1
    for s in shape:
        n *= s
    return n * jnp.dtype(dtype).itemsize


def _sigmoid(x):
    return jax.nn.sigmoid(x)


def _softplus(x):
    return jnp.maximum(x, 0.0) + jnp.log1p(jnp.exp(-jnp.abs(x)))


def _expm1(x):
    series = x * (1.0 + x * (0.5 + x * (1.0 / 6.0 + x * (1.0 / 24.0 + x * (1.0 / 120.0)))))
    return jnp.where(jnp.abs(x) < 0.25, series, jnp.exp(x) - 1.0)


_GELU_C = 0.7978845608028654
_GELU_A = 0.044715


def _gelu(x):
    return 0.5 * x * (1.0 + jnp.tanh(_GELU_C * (x + _GELU_A * x * x * x)))


def _gelu_grad(x):
    t = jnp.tanh(_GELU_C * (x + _GELU_A * x * x * x))
    return 0.5 * (1.0 + t) + 0.5 * x * (1.0 - t * t) * _GELU_C * (1.0 + 3.0 * _GELU_A * x * x)


_DOT_DIMS = {"nn": ((1,), (0,)), "nt": ((1,), (1,)), "tn": ((0,), (0,))}


def _dot(a, b, mode):
    return lax.dot_general(a.astype(MXU_DTYPE), b.astype(MXU_DTYPE), (_DOT_DIMS[mode], ((), ())),
                           preferred_element_type=F32)


def _mm(pairs, mode, out_dtype, name, after=None, out_scale=None):
    a0, b0 = pairs[0]
    if mode == "tn":
        k_dim, m_dim = a0.shape
        n_dim = b0.shape[1]
    else:
        m_dim, k_dim = a0.shape
        n_dim = b0.shape[0] if mode == "nt" else b0.shape[1]
    for a, b in pairs:
        assert a.shape == a0.shape and b.shape == b0.shape
    tm = _tile(m_dim, 1408 if mode == "tn" else 512)
    whole = 1408 if mode == "tn" else 2816
    tn = _tile(n_dim, whole)
    tk = _tile(k_dim, whole)
    nk = k_dim // tk
    n_pairs = len(pairs)

    if mode == "tn":
        a_spec = pl.BlockSpec((tk, tm), lambda i, j, k: (k, i))
    else:
        a_spec = pl.BlockSpec((tm, tk), lambda i, j, k: (i, k))
    if mode == "nt":
        b_spec = pl.BlockSpec((tn, tk), lambda i, j, k: (j, k))
    else:
        b_spec = pl.BlockSpec((tk, tn), lambda i, j, k: (k, j))

    order = [] if after is None else [after]

    def body(*refs):
        ins, o_ref, acc = refs[:2 * n_pairs], refs[-2], refs[-1]
        k = pl.program_id(2)

        @pl.when(k == 0)
        def _():
            acc[...] = jnp.zeros_like(acc)

        s = acc[...]
        for p in range(n_pairs):
            s = s + _dot(ins[2 * p][...], ins[2 * p + 1][...], mode)
        acc[...] = s

        @pl.when(k == nk - 1)
        def _():
            r = acc[...] if out_scale is None else acc[...] * out_scale
            o_ref[...] = r.astype(out_dtype)

    est = (2 * n_pairs * (_nbytes((tm, tk), a0.dtype) + _nbytes((tk, tn), b0.dtype))
           + 2 * _nbytes((tm, tn), out_dtype) + 2 * _nbytes((tm, tn), F32))
    flat = [t for ab in pairs for t in ab]
    return pl.pallas_call(
        body, name=name, grid=(m_dim // tm, n_dim // tn, nk),
        in_specs=[a_spec, b_spec] * n_pairs + [_ANY] * len(order),
        out_specs=pl.BlockSpec((tm, tn), lambda i, j, k: (i, j)),
        out_shape=jax.ShapeDtypeStruct((m_dim, n_dim), out_dtype),
        scratch_shapes=[pltpu.VMEM((tm, tn), F32)],
        compiler_params=_cparams(("parallel", "parallel", "arbitrary"), est),
    )(*flat, *order)


_ANY = pl.BlockSpec(memory_space=pl.ANY)


def _rms_fwd(x, gain, name, after=None):
    s_dim, d = x.shape
    tm = _tile(s_dim, 512, 8)

    def body(*refs):
        x_ref, g_ref, o_ref = refs[0], refs[1], refs[-1]
        v = x_ref[...]
        r = lax.rsqrt(jnp.mean(v * v, axis=-1, keepdims=True) + RMS_EPS)
        o_ref[...] = (v * r * g_ref[...]).astype(MXU_DTYPE)

    order = [] if after is None else [after]
    return pl.pallas_call(
        body, name=name, grid=(s_dim // tm,),
        in_specs=[pl.BlockSpec((tm, d), lambda i: (i, 0)), pl.BlockSpec((1, d), lambda i: (0, 0))] + [_ANY] * len(order),
        out_specs=pl.BlockSpec((tm, d), lambda i: (i, 0)),
        out_shape=jax.ShapeDtypeStruct((s_dim, d), MXU_DTYPE),
        compiler_params=_cparams(("parallel",), 6 * _nbytes((tm, d), F32)),
    )(x, gain, *order)


def _rms_bwd(x, gain, dys, res, scale, out_dtype, name, after=None):
    s_dim, d = x.shape
    tm = _tile(s_dim, 512, 8)
    n_dy = len(dys)
    has_res = res is not None
    order = [] if after is None else [after]

    def body(*refs):
        x_ref, g_ref = refs[0], refs[1]
        dy_refs = refs[2:2 + n_dy]
        res_ref = refs[2 + n_dy] if has_res else None
        dx_ref, dg_ref = refs[-2], refs[-1]

        @pl.when(pl.program_id(0) == 0)
        def _():
            dg_ref[...] = jnp.zeros_like(dg_ref)

        v = x_ref[...]
        r = lax.rsqrt(jnp.mean(v * v, axis=-1, keepdims=True) + RMS_EPS)
        xh = v * r
        dy = dy_refs[0][...].astype(F32)
        for extra in dy_refs[1:]:
            dy = dy + extra[...].astype(F32)
        gd = dy * g_ref[...]
        dx = scale * r * (gd - xh * jnp.mean(gd * xh, axis=-1, keepdims=True))
        if has_res:
            dx = dx + res_ref[...]
        dx_ref[...] = dx.astype(out_dtype)
        dg_ref[...] += scale * jnp.sum(dy * xh, axis=0, keepdims=True)

    row = pl.BlockSpec((tm, d), lambda i: (i, 0))
    vec = pl.BlockSpec((1, d), lambda i: (0, 0))
    ops = [x, gain] + list(dys) + ([res] if has_res else [])
    return pl.pallas_call(
        body, name=name, grid=(s_dim // tm,),
        in_specs=[row, vec] + [row] * (n_dy + int(has_res)) + [_ANY] * len(order),
        out_specs=[row, vec],
        out_shape=[jax.ShapeDtypeStruct((s_dim, d), out_dtype), jax.ShapeDtypeStruct((1, d), F32)],
        compiler_params=_cparams(("arbitrary",), (2 * len(ops) + 6) * _nbytes((tm, d), F32)),
    )(*ops, *order)


def _mm_rms_res(a, b, h, gain, scale, name, next_gain=None):
    s_dim, k_dim = a.shape
    d = b.shape[1]
    tm = _tile(s_dim, 512, 8)
    tk = _tile(k_dim, 2816)
    nk = k_dim // tk

    with_next = next_gain is not None

    def body(*refs):
        a_ref, b_ref, h_ref, g_ref = refs[:4]
        g2_ref = refs[4] if with_next else None
        f_ref, o_ref = refs[4 + int(with_next)], refs[5 + int(with_next)]
        xn_ref = refs[6 + int(with_next)] if with_next else None
        acc = refs[-1]
        k = pl.program_id(1)

        @pl.when(k == 0)
        def _():
            acc[...] = jnp.zeros_like(acc)

        acc[...] += _dot(a_ref[...], b_ref[...], "nn")

        @pl.when(k == nk - 1)
        def _():
            f = acc[...]
            r = lax.rsqrt(jnp.mean(f * f, axis=-1, keepdims=True) + RMS_EPS)
            f_ref[...] = f
            o = h_ref[...] + scale * (f * r * g_ref[...])
            o_ref[...] = o
            if with_next:
                r2 = lax.rsqrt(jnp.mean(o * o, axis=-1, keepdims=True) + RMS_EPS)
                xn_ref[...] = (o * r2 * g2_ref[...]).astype(MXU_DTYPE)

    row = pl.BlockSpec((tm, d), lambda i, k: (i, 0))
    vec = pl.BlockSpec((1, d), lambda i, k: (0, 0))
    est = (2 * (_nbytes((tm, tk), a.dtype) + _nbytes((tk, d), b.dtype)) + 10 * _nbytes((tm, d), F32))
    wide = jax.ShapeDtypeStruct((s_dim, d), F32)
    return pl.pallas_call(
        body, name=name, grid=(s_dim // tm, nk),
        in_specs=[pl.BlockSpec((tm, tk), lambda i, k: (i, k)), pl.BlockSpec((tk, d), lambda i, k: (k, 0)),
                  row, vec] + [vec] * int(with_next),
        out_specs=[row, row] + [row] * int(with_next),
        out_shape=[wide, wide] + [jax.ShapeDtypeStruct((s_dim, d), MXU_DTYPE)] * int(with_next),
        scratch_shapes=[pltpu.VMEM((tm, d), F32)],
        compiler_params=_cparams(("parallel", "arbitrary"), est),
    )(a, b, h, gain, *([next_gain] if with_next else []))


def _ffn_up(xn, wg_t, wu_t, name):
    s_dim, d = xn.shape
    f_dim = wg_t.shape[0]
    tm = _tile(s_dim, 2048, 8)
    tf = _tile(f_dim, 256)

    def body(x_ref, wg_ref, wu_ref, g_ref, u_ref, a_ref):
        x = x_ref[...]
        g = _dot(x, wg_ref[...], "nt")
        u = _dot(x, wu_ref[...], "nt")
        g_ref[...] = g.astype(MXU_DTYPE)
        u_ref[...] = u.astype(MXU_DTYPE)
        a_ref[...] = (g * _sigmoid(g) * u).astype(MXU_DTYPE)

    w_spec = pl.BlockSpec((tf, d), lambda i, j: (j, 0))
    o_spec = pl.BlockSpec((tm, tf), lambda i, j: (i, j))
    o_shape = jax.ShapeDtypeStruct((s_dim, f_dim), MXU_DTYPE)
    est = 2 * _nbytes((tm, d), xn.dtype) + 4 * _nbytes((tf, d), wg_t.dtype) + 10 * _nbytes((tm, tf), F32)
    return pl.pallas_call(
        body, name=name, grid=(s_dim // tm, f_dim // tf),
        in_specs=[pl.BlockSpec((tm, d), lambda i, j: (i, 0)), w_spec, w_spec],
        out_specs=[o_spec, o_spec, o_spec], out_shape=[o_shape, o_shape, o_shape],
        compiler_params=_cparams(("parallel", "parallel"), est),
    )(xn, wg_t, wu_t)


def _ffn_act_bwd(df, wd, g, u, name):
    s_dim, d = df.shape
    f_dim = wd.shape[0]
    tm = _tile(s_dim, 2048, 8)
    tf = _tile(f_dim, 256)

    def body(df_ref, wd_ref, g_ref, u_ref, dg_ref, du_ref):
        dh = _dot(df_ref[...], wd_ref[...], "nt")
        gv = g_ref[...].astype(F32)
        uv = u_ref[...].astype(F32)
        sg = _sigmoid(gv)
        dg_ref[...] = (dh * uv * (sg * (1.0 + gv * (1.0 - sg)))).astype(MXU_DTYPE)
        du_ref[...] = (dh * gv * sg).astype(MXU_DTYPE)

    t_spec = pl.BlockSpec((tm, tf), lambda i, j: (i, j))
    o_shape = jax.ShapeDtypeStruct((s_dim, f_dim), MXU_DTYPE)
    est = 2 * _nbytes((tm, d), df.dtype) + 2 * _nbytes((tf, d), wd.dtype) + 12 * _nbytes((tm, tf), F32)
    return pl.pallas_call(
        body, name=name, grid=(s_dim // tm, f_dim // tf),
        in_specs=[pl.BlockSpec((tm, d), lambda i, j: (i, 0)), pl.BlockSpec((tf, d), lambda i, j: (j, 0)),
                  t_spec, t_spec],
        out_specs=[t_spec, t_spec], out_shape=[o_shape, o_shape],
        compiler_params=_cparams(("parallel", "parallel"), est),
    )(df, wd, g, u)


def _loss_head(y, target, name):
    s_dim, d = y.shape
    tm = _tile(s_dim, 512, 8)
    nt = s_dim // tm

    def body(y_ref, t_ref, dy_ref, loss_ref, acc):
        i = pl.program_id(0)

        @pl.when(i == 0)
        def _():
            acc[...] = jnp.zeros_like(acc)

        e = y_ref[...] - t_ref[...]
        dy_ref[...] = e * (1.0 / d)
        acc[...] += jnp.sum(e * e, axis=0, keepdims=True)

        @pl.when(i == nt - 1)
        def _():
            loss_ref[...] = jnp.sum(acc[...], axis=1, keepdims=True) * (0.5 / d)

    row = pl.BlockSpec((tm, d), lambda i: (i, 0))
    return pl.pallas_call(
        body, name=name, grid=(nt,), in_specs=[row, row],
        out_specs=[row, pl.BlockSpec((1, 1), lambda i: (0, 0))],
        out_shape=[jax.ShapeDtypeStruct((s_dim, d), F32), jax.ShapeDtypeStruct((1, 1), F32)],
        scratch_shapes=[pltpu.VMEM((1, d), F32)],
        compiler_params=_cparams(("arbitrary",), 8 * _nbytes((tm, d), F32)),
    )(y, target)


def _shift_down(v, sh, row):
    if sh == 0:
        return v
    return jnp.where(row >= sh, pltpu.roll(v, sh, 0), 0.0)


def _shift_up(v, sh, row):
    if sh == 0:
        return v
    n = v.shape[0]
    return jnp.where(row < n - sh, pltpu.roll(v, n - sh, 0), 0.0)


def _conv_fwd(gx, pconv, width, name):
    s_dim, cp2 = gx.shape
    cp = cp2 // 2
    nc = cp // LANES

    def body(x_ref, p_ref, o_ref):
        x = x_ref[...]
        row = lax.broadcasted_iota(jnp.int32, x.shape, 0)
        y = jnp.zeros_like(x) + p_ref[pl.ds(width, 1), :]
        for k in range(width):
            y = y + p_ref[pl.ds(k, 1), :] * _shift_down(x, width - 1 - k, row)
        o_ref[...] = y

    return pl.pallas_call(
        body, name=name, grid=(nc,),
        in_specs=[pl.BlockSpec((s_dim, LANES), lambda j: (0, nc + j)), pl.BlockSpec((8, LANES), lambda j: (0, j))],
        out_specs=pl.BlockSpec((s_dim, LANES), lambda j: (0, j)),
        out_shape=jax.ShapeDtypeStruct((s_dim, cp), F32),
        compiler_params=_cparams(("parallel",), 10 * _nbytes((s_dim, LANES), F32)),
    )(gx, pconv)


def _conv_bwd(d1, d2, gx, pconv, width, name):
    s_dim, cp = d1.shape
    nc = cp // LANES

    def body(d1_ref, d2_ref, x_ref, p_ref, dx_ref, dp_ref):
        d = d1_ref[...] + d2_ref[...]
        x = x_ref[...]
        row = lax.broadcasted_iota(jnp.int32, x.shape, 0)
        dx = jnp.zeros_like(d)
        dp_ref[...] = jnp.zeros_like(dp_ref)
        for k in range(width):
            sh = width - 1 - k
            dx = dx + p_ref[pl.ds(k, 1), :] * _shift_up(d, sh, row)
            dp_ref[pl.ds(k, 1), :] = jnp.sum(d * _shift_down(x, sh, row), axis=0, keepdims=True)
        dp_ref[pl.ds(width, 1), :] = jnp.sum(d, axis=0, keepdims=True)
        dx_ref[...] = dx.astype(MXU_DTYPE)

    strip = pl.BlockSpec((s_dim, LANES), lambda j: (0, j))
    par = pl.BlockSpec((8, LANES), lambda j: (0, j))
    return pl.pallas_call(
        body, name=name, grid=(nc,),
        in_specs=[strip, strip, pl.BlockSpec((s_dim, LANES), lambda j: (0, nc + j)), par],
        out_specs=[strip, par],
        out_shape=[jax.ShapeDtypeStruct((s_dim, cp), MXU_DTYPE), jax.ShapeDtypeStruct((8, cp), F32)],
        compiler_params=_cparams(("parallel",), 14 * _nbytes((s_dim, LANES), F32)),
    )(d1, d2, gx, pconv)


def _lru_coeffs(ra, ia, p_ref):
    r = _sigmoid(ra + p_ref[pl.ds(0, 1), :])
    i = _sigmoid(ia + p_ref[pl.ds(1, 1), :])
    sp = _softplus(-p_ref[pl.ds(2, 1), :])
    log_a = -LRU_C * r * sp
    a = jnp.exp(log_a)
    mult = jnp.sqrt(-_expm1(2.0 * log_a))
    return r, i, sp, a, mult


def _scan_fwd(gx, rec, gates, pvec, name):
    s_dim, cp = rec.shape
    ts = _tile(s_dim, 256, 8)
    nt = s_dim // ts

    def body(gate_ref, rec_ref, ra_ref, ia_ref, p_ref, h_ref, y_ref, a_s, u_s, carry):
        @pl.when(pl.program_id(0) == 0)
        def _():
            carry[...] = jnp.zeros_like(carry)

        rec_v = rec_ref[...]
        _, i, _, a, mult = _lru_coeffs(ra_ref[...], ia_ref[...], p_ref)
        a_s[...] = a
        u_s[...] = mult * (i * rec_v)

        def step(t, h):
            h = a_s[pl.ds(t, 1), :] * h + u_s[pl.ds(t, 1), :]
            h_ref[pl.ds(t, 1), :] = h
            return h

        carry[pl.ds(0, 1), :] = lax.fori_loop(0, ts, step, carry[pl.ds(0, 1), :], unroll=8)
        y_ref[...] = (_gelu(gate_ref[...]) * h_ref[...]).astype(MXU_DTYPE)

    blk = pl.BlockSpec((ts, cp), lambda t: (t, 0))
    return pl.pallas_call(
        body, name=name, grid=(nt,),
        in_specs=[blk, blk, blk, pl.BlockSpec((ts, cp), lambda t: (t, 1)), pl.BlockSpec((8, cp), lambda t: (0, 0))],
        out_specs=[blk, blk],
        out_shape=[jax.ShapeDtypeStruct((s_dim, cp), F32), jax.ShapeDtypeStruct((s_dim, cp), MXU_DTYPE)],
        scratch_shapes=[pltpu.VMEM((ts, cp), F32), pltpu.VMEM((ts, cp), F32), pltpu.VMEM((8, cp), F32)],
        compiler_params=_cparams(("arbitrary",), 14 * _nbytes((ts, cp), F32)),
    )(gx, rec, gates, gates, pvec)


def _scan_bwd(dy, gx, hrec, rec, gates, pvec, name):
    s_dim, cp = rec.shape
    ts = _tile(s_dim, 128, 8)
    nt = s_dim // ts

    def body(dy_ref, gate_ref, h_ref, hp_ref, rec_ref, ra_ref, ia_ref, p_ref,
             dgate_ref, dra_ref, dia_ref, drec_ref, dp_ref, a_s, d_s, carry):
        t_id = pl.program_id(0)

        @pl.when(t_id == 0)
        def _():
            carry[...] = jnp.zeros_like(carry)
            dp_ref[...] = jnp.zeros_like(dp_ref)

        rec_v = rec_ref[...]
        r, i, sp, a, mult = _lru_coeffs(ra_ref[...], ia_ref[...], p_ref)
        gate = gate_ref[...]
        dyv = dy_ref[...]
        h = h_ref[...]
        dgate_ref[...] = (dyv * h * _gelu_grad(gate)).astype(MXU_DTYPE)
        a_s[...] = a
        d_s[...] = dyv * _gelu(gate)

        def step(k, c):
            t = ts - 1 - k
            d = d_s[pl.ds(t, 1), :] + c
            d_s[pl.ds(t, 1), :] = d
            return a_s[pl.ds(t, 1), :] * d

        carry[pl.ds(0, 1), :] = lax.fori_loop(0, ts, step, carry[pl.ds(0, 1), :], unroll=8)
        dh = d_s[...]
        row = lax.broadcasted_iota(jnp.int32, h.shape, 0)
        first = jnp.where(t_id == nt - 1, 0.0, 1.0) * hp_ref[pl.ds(7, 1), :]
        h_prev = jnp.where(row == 0, first, pltpu.roll(h, 1, 0))
        dix = dh * mult
        dla = dh * h_prev * a - dh * (i * rec_v) * (a * a) / mult
        dra = dla * (-LRU_C * sp) * r * (1.0 - r)
        dia = dix * rec_v * i * (1.0 - i)
        dra_ref[...] = dra.astype(MXU_DTYPE)
        dia_ref[...] = dia.astype(MXU_DTYPE)
        drec_ref[...] = dix * i
        dsp = jnp.sum(dla * (-LRU_C * r), axis=0, keepdims=True)
        dp_ref[pl.ds(0, 1), :] += jnp.sum(dra, axis=0, keepdims=True)
        dp_ref[pl.ds(1, 1), :] += jnp.sum(dia, axis=0, keepdims=True)
        dp_ref[pl.ds(2, 1), :] += dsp * (-_sigmoid(-p_ref[pl.ds(2, 1), :]))

    blk = pl.BlockSpec((ts, cp), lambda t: (nt - 1 - t, 0))
    prev = pl.BlockSpec((8, cp), lambda t: (jnp.maximum((nt - 1 - t) * (ts // 8) - 1, 0), 0))
    par = pl.BlockSpec((8, cp), lambda t: (0, 0))
    lo = jax.ShapeDtypeStruct((s_dim, cp), MXU_DTYPE)
    return pl.pallas_call(
        body, name=name, grid=(nt,),
        in_specs=[blk, blk, blk, prev, blk, blk, pl.BlockSpec((ts, cp), lambda t: (nt - 1 - t, 1)), par],
        out_specs=[blk, blk, blk, blk, par],
        out_shape=[lo, lo, lo, jax.ShapeDtypeStruct((s_dim, cp), F32), jax.ShapeDtypeStruct((8, cp), F32)],
        scratch_shapes=[pltpu.VMEM((ts, cp), F32), pltpu.VMEM((ts, cp), F32), pltpu.VMEM((8, cp), F32)],
        compiler_params=_cparams(("arbitrary",), 40 * _nbytes((ts, cp), F32)),
    )(dy, gx, hrec, hrec, rec, gates, gates, pvec)


def _fgate_fwd(fpre, bias, name):
    s_dim, w = fpre.shape
    ts = _tile(s_dim, 512, 8)

    def body(f_ref, b_ref, c_ref, lf_s, carry):
        @pl.when(pl.program_id(0) == 0)
        def _():
            carry[...] = jnp.zeros_like(carry)

        lf_s[...] = -_softplus(-(f_ref[...] + b_ref[pl.ds(0, 1), :]))

        def step(t, c):
            c = c + lf_s[pl.ds(t, 1), :]
            c_ref[pl.ds(t, 1), :] = c
            return c

        carry[pl.ds(0, 1), :] = lax.fori_loop(0, ts, step, carry[pl.ds(0, 1), :], unroll=8)

    blk = pl.BlockSpec((ts, w), lambda t: (t, 0))
    return pl.pallas_call(
        body, name=name, grid=(s_dim // ts,),
        in_specs=[blk, pl.BlockSpec((8, w), lambda t: (0, 0))], out_specs=blk,
        out_shape=jax.ShapeDtypeStruct((s_dim, w), F32),
        scratch_shapes=[pltpu.VMEM((ts, w), F32), pltpu.VMEM((8, w), F32)],
        compiler_params=_cparams(("arbitrary",), 12 * _nbytes((ts, w), F32)),
    )(fpre, bias)


def _fgate_bwd(dc, fpre, bias, name):
    s_dim, w = fpre.shape
    ts = _tile(s_dim, 512, 8)
    nt = s_dim // ts

    def body(dc_ref, f_ref, b_ref, df_ref, db_ref, d_s, carry):
        @pl.when(pl.program_id(0) == 0)
        def _():
            carry[...] = jnp.zeros_like(carry)
            db_ref[...] = jnp.zeros_like(db_ref)

        d_s[...] = dc_ref[...]

        def step(k, c):
            t = ts - 1 - k
            c = c + d_s[pl.ds(t, 1), :]
            d_s[pl.ds(t, 1), :] = c
            return c

        carry[pl.ds(0, 1), :] = lax.fori_loop(0, ts, step, carry[pl.ds(0, 1), :], unroll=8)
        df = d_s[...] * _sigmoid(-(f_ref[...] + b_ref[pl.ds(0, 1), :]))
        df_ref[...] = df
        db_ref[pl.ds(0, 1), :] += jnp.sum(df, axis=0, keepdims=True)

    blk = pl.BlockSpec((ts, w), lambda t: (nt - 1 - t, 0))
    par = pl.BlockSpec((8, w), lambda t: (0, 0))
    return pl.pallas_call(
        body, name=name, grid=(nt,), in_specs=[blk, blk, par], out_specs=[blk, par],
        out_shape=[jax.ShapeDtypeStruct((s_dim, w), F32), jax.ShapeDtypeStruct((8, w), F32)],
        scratch_shapes=[pltpu.VMEM((ts, w), F32), pltpu.VMEM((8, w), F32)],
        compiler_params=_cparams(("arbitrary",), 12 * _nbytes((ts, w), F32)),
    )(dc, fpre, bias)


def _head_lanes(hh, dh):
    lane = lax.broadcasted_iota(jnp.int32, (1, LANES), 1)
    return (lane >= hh * dh) & (lane < (hh + 1) * dh)


def _pair_attn_fwd(q, kv, v_t, c_col, c_row, name):
    s_dim, da = q.shape
    n_h = c_col.shape[0]
    dh = da // n_h
    assert LANES % dh == 0 and da % LANES == 0
    hb = LANES // dh
    n_blocks = da // LANES
    t = _tile(s_dim, 1024, LANES)
    nb = s_dim // t

    pairs = [(i, j) for i in range(nb) for j in range(i + 1)]
    i_tab = jnp.asarray([p[0] for p in pairs], jnp.int32)
    j_tab = jnp.asarray([p[1] for p in pairs], jnp.int32)

    def body(i_ref, j_ref, q_ref, k_ref, vt_ref, cq_ref, ck_ref, o_ref, lse_ref, m_s, l_s, acc):
        i, j = i_ref[pl.program_id(1)], j_ref[pl.program_id(1)]

        @pl.when(j == 0)
        def _():
            m_s[...] = jnp.full_like(m_s, -jnp.inf)
            l_s[...] = jnp.zeros_like(l_s)
            acc[...] = jnp.zeros_like(acc)

        def tile(masked):
            qv = q_ref[...]
            for hh in range(hb):
                st = _dot(k_ref[...], jnp.where(_head_lanes(hh, dh), qv, jnp.zeros_like(qv)), "nt")
                st = st + (cq_ref[hh] - ck_ref[hh])
                if masked:
                    keep = lax.broadcasted_iota(jnp.int32, (t, t), 0) <= lax.broadcasted_iota(jnp.int32, (t, t), 1)
                    st = jnp.where(keep, st, -jnp.inf)
                m_prev = m_s[hh]
                m_new = jnp.maximum(m_prev, jnp.max(st, axis=0, keepdims=True))
                alpha = jnp.exp(m_prev - m_new)
                p = jnp.exp(st - m_new)
                l_s[hh] = alpha * l_s[hh] + jnp.sum(p, axis=0, keepdims=True)
                acc[hh] = alpha * acc[hh] + _dot(vt_ref[...], p, "nn")
                m_s[hh] = m_new

        pl.when(j < i)(functools.partial(tile, False))
        pl.when(j == i)(functools.partial(tile, True))

        @pl.when(j == i)
        def _():
            feat = lax.broadcasted_iota(jnp.int32, (LANES, 1), 0)
            out_t = jnp.zeros((LANES, t), F32)
            for hh in range(hb):
                out_t = jnp.where((feat >= hh * dh) & (feat < (hh + 1) * dh), acc[hh] / l_s[hh], out_t)
                lse_ref[hh] = m_s[hh] + jnp.log(l_s[hh])
            o_ref[...] = out_t.T

    q_spec = pl.BlockSpec((t, LANES), lambda b, p, it, jt: (it[p], b))
    k_spec = pl.BlockSpec((t, LANES), lambda b, p, it, jt: (jt[p], b))
    vt_spec = pl.BlockSpec((LANES, t), lambda b, p, it, jt: (b, jt[p]))
    cq_spec = pl.BlockSpec((hb, 1, t), lambda b, p, it, jt: (b, 0, it[p]))
    ck_spec = pl.BlockSpec((hb, t, 1), lambda b, p, it, jt: (b, jt[p], 0))
    return pl.pallas_call(
        body, name=name,
        grid_spec=pltpu.PrefetchScalarGridSpec(
            num_scalar_prefetch=2, grid=(n_blocks, len(pairs)),
            in_specs=[q_spec, k_spec, vt_spec, cq_spec, ck_spec], out_specs=[q_spec, cq_spec],
            scratch_shapes=[pltpu.VMEM((hb, 1, t), F32), pltpu.VMEM((hb, 1, t), F32), pltpu.VMEM((hb, LANES, t), F32)]),
        out_shape=[jax.ShapeDtypeStruct((s_dim, da), F32), jax.ShapeDtypeStruct((n_h, 1, s_dim), F32)],
        compiler_params=_cparams(("parallel", "arbitrary"), 10 * hb * _nbytes((t, t), F32)),
    )(i_tab, j_tab, q, kv, v_t, c_row, c_col)


def _attn_delta(do, o, n_h, name):
    s_dim, da = o.shape
    dh = da // n_h
    hb = LANES // dh
    t = _tile(s_dim, 512, LANES)

    def body(do_ref, o_ref, d_ref):
        prod_t = (do_ref[...].astype(MXU_DTYPE).astype(F32) * o_ref[...]).T
        for hh in range(hb):
            d_ref[hh] = jnp.sum(prod_t[hh * dh:(hh + 1) * dh], axis=0, keepdims=True)

    blk = pl.BlockSpec((t, LANES), lambda b, i: (i, b))
    return pl.pallas_call(
        body, name=name, grid=(da // LANES, s_dim // t), in_specs=[blk, blk],
        out_specs=pl.BlockSpec((hb, 1, t), lambda b, i: (b, 0, i)),
        out_shape=jax.ShapeDtypeStruct((n_h, 1, s_dim), F32),
        compiler_params=_cparams(("parallel", "parallel"), 8 * _nbytes((t, LANES), F32)),
    )(do, o)


def _pair_attn_bwd(q, kv, c_col, c_row, lse, delta, do, scale, name):
    s_dim, da = q.shape
    n_h = c_col.shape[0]
    dh = da // n_h
    hb = LANES // dh
    n_blocks = da // LANES
    t = _tile(s_dim, 1024, LANES)
    nb = s_dim // t

    pairs = [(i, j) for j in range(nb) for i in range(j, nb)]
    i_tab = jnp.asarray([p[0] for p in pairs], jnp.int32)
    j_tab = jnp.asarray([p[1] for p in pairs], jnp.int32)

    def body(i_ref, j_ref, q_ref, k_ref, v_ref, cq_ref, ck_ref, lse_ref, dl_ref, do_ref,
             dq_ref, dcq_ref, dk_ref, dv_ref, dck_ref, dk_acc, dv_acc, dck_acc):
        i, j = i_ref[pl.program_id(1)], j_ref[pl.program_id(1)]

        @pl.when(pl.program_id(1) == 0)
        def _():
            dq_ref[...] = jnp.zeros_like(dq_ref)
            dcq_ref[...] = jnp.zeros_like(dcq_ref)

        @pl.when(i == j)
        def _():
            dk_acc[...] = jnp.zeros_like(dk_acc)
            dv_acc[...] = jnp.zeros_like(dv_acc)
            dck_acc[...] = jnp.zeros_like(dck_acc)

        def tile(masked):
            start = pl.multiple_of(i * t, t)
            qv, kv_ = q_ref[...], k_ref[...]
            dov = do_ref[...].astype(MXU_DTYPE)
            for hh in range(hb):
                lanes = _head_lanes(hh, dh)
                qm = jnp.where(lanes, qv, jnp.zeros_like(qv))
                km = jnp.where(lanes, kv_, jnp.zeros_like(kv_))
                dom = jnp.where(lanes, dov, jnp.zeros_like(dov))
                st = _dot(kv_, qm, "nt") + (cq_ref[hh] - ck_ref[hh])
                if masked:
                    keep = lax.broadcasted_iota(jnp.int32, (t, t), 0) <= lax.broadcasted_iota(jnp.int32, (t, t), 1)
                    st = jnp.where(keep, st, -jnp.inf)
                pt = jnp.exp(st - lse_ref[hh])
                dst = pt * (_dot(v_ref[...], dom, "nt") - dl_ref[hh])
                dv_acc[...] += _dot(pt, dom, "nn")
                dk_acc[...] += _dot(dst, qm, "nn")
                dq_ref[pl.ds(start, t), :] += _dot(dst, km, "tn") * scale
                dcq_ref[hh, :, pl.ds(start, t)] += jnp.sum(dst, axis=0, keepdims=True)
                dck_acc[hh] -= jnp.sum(dst, axis=1, keepdims=True)

        pl.when(i > j)(functools.partial(tile, False))
        pl.when(i == j)(functools.partial(tile, True))

        @pl.when(i == nb - 1)
        def _():
            dk_ref[...] = dk_acc[...]
            dv_ref[...] = dv_acc[...]
            dck_ref[...] = dck_acc[...]

    q_spec = pl.BlockSpec((t, LANES), lambda b, p, it, jt: (it[p], b))
    qrow_spec = pl.BlockSpec((hb, 1, t), lambda b, p, it, jt: (b, 0, it[p]))
    k_spec = pl.BlockSpec((t, LANES), lambda b, p, it, jt: (jt[p], b))
    v_spec = pl.BlockSpec((t, LANES), lambda b, p, it, jt: (jt[p], n_blocks + b))
    kcol_spec = pl.BlockSpec((hb, t, 1), lambda b, p, it, jt: (b, jt[p], 0))
    wide = jax.ShapeDtypeStruct((s_dim, da), F32)
    return pl.pallas_call(
        body, name=name,
        grid_spec=pltpu.PrefetchScalarGridSpec(
            num_scalar_prefetch=2, grid=(n_blocks, len(pairs)),
            in_specs=[q_spec, k_spec, v_spec, qrow_spec, kcol_spec, qrow_spec, qrow_spec, q_spec],
            out_specs=[pl.BlockSpec((s_dim, LANES), lambda b, p, it, jt: (0, b)),
                       pl.BlockSpec((hb, 1, s_dim), lambda b, p, it, jt: (b, 0, 0)), k_spec, k_spec, kcol_spec],
            scratch_shapes=[pltpu.VMEM((t, LANES), F32), pltpu.VMEM((t, LANES), F32), pltpu.VMEM((hb, t, 1), F32)]),
        out_shape=[wide, jax.ShapeDtypeStruct((n_h, 1, s_dim), F32), wide, wide,
                   jax.ShapeDtypeStruct((n_h, s_dim, 1), F32)],
        compiler_params=_cparams(("parallel", "arbitrary"),
                                 10 * hb * _nbytes((t, t), F32) + 4 * _nbytes((s_dim, LANES), F32)),
    )(i_tab, j_tab, q, kv, kv, c_row, c_col, lse, delta, do)


_HBM = pl.BlockSpec(memory_space=pltpu.HBM)
_MESH_ID = pl.DeviceIdType.MESH


def _all_gather(block, name):
    r, w = block.shape

    def body(x_ref, out_ref, send_sems, recv_sems, local_sem):
        x, y, c = lax.axis_index("x"), lax.axis_index("y"), lax.axis_index("c")
        me, sibling = (x, y, c), (x, y, 1 - c)
        chips = [(1 - x, y), (x, 1 - y), (1 - x, 1 - y)]

        def slot(px, py, pc):
            return out_ref.at[4 * px + 2 * py + pc]

        def copy(k, blk, to, src=None):
            return pltpu.make_async_remote_copy(
                src_ref=slot(*blk) if src is None else src, dst_ref=slot(*blk),
                send_sem=send_sems.at[k], recv_sem=recv_sems.at[k], device_id=to, device_id_type=_MESH_ID)

        mine = pltpu.make_async_copy(x_ref, slot(*me), local_sem)
        mine.start()
        first = [copy(0, me, sibling, src=x_ref)]
        first += [copy(1 + n, me, (*chip, c), src=x_ref) for n, chip in enumerate(chips)]
        for cp in first:
            cp.start()
        passed = [copy(4 + n, (*chip, c), sibling) for n, chip in enumerate(chips)]
        for n, chip in enumerate(chips):
            copy(1 + n, (*chip, c), me).wait_recv()
            passed[n].start()
        copy(0, sibling, me).wait_recv()
        for n, chip in enumerate(chips):
            copy(4 + n, (*chip, 1 - c), me).wait_recv()
        for cp in first + passed:
            cp.wait_send()
        mine.wait()

    return pl.pallas_call(
        body, name=name, out_shape=jax.ShapeDtypeStruct((N_DEV, r, w), block.dtype),
        in_specs=[_HBM], out_specs=_HBM,
        scratch_shapes=[pltpu.SemaphoreType.DMA((7,)), pltpu.SemaphoreType.DMA((7,)), pltpu.SemaphoreType.DMA],
    )(block)


_SEM = pl.BlockSpec(memory_space=pltpu.SEMAPHORE)
_EFFECT = pltpu.SideEffectType.DATAFLOW_SIDE_EFFECTING


def _exchange_start(srcs, personalized, after, name):
    n = len(srcs)
    n_after = len(after)
    lands = [lax.empty((N_DEV,) + s.shape[-2:], s.dtype) for s in srcs]

    def body(*refs):
        src_refs, land_refs = refs[:n], refs[n:2 * n]
        outs = refs[2 * n + n_after:]
        send_sems, recv_sems, token = outs[:n], outs[n:2 * n], outs[-1]
        x, y, c = lax.axis_index("x"), lax.axis_index("y"), lax.axis_index("c")
        mine = 4 * x + 2 * y + c
        for ci in range(n):
            for k in range(1, N_DEV):
                px = 1 - x if k & 4 else x
                py = 1 - y if k & 2 else y
                pc = 1 - c if k & 1 else c
                src = src_refs[ci].at[4 * px + 2 * py + pc] if personalized else src_refs[ci]
                pltpu.make_async_remote_copy(
                    src_ref=src, dst_ref=land_refs[ci].at[mine], send_sem=send_sems[ci], recv_sem=recv_sems[ci],
                    device_id=(px, py, pc), device_id_type=_MESH_ID).start()
        token[...] = jnp.zeros_like(token)

    sem = pltpu.SemaphoreType.DMA(())
    out_shape = ([sem] * (2 * n) + [pltpu.HBM(s.shape, s.dtype) for s in srcs]
                 + [pltpu.HBM(l.shape, l.dtype) for l in lands] + [jax.ShapeDtypeStruct((8, LANES), F32)])
    res = pl.pallas_call(
        body, name=name, out_shape=tuple(out_shape),
        in_specs=[_HBM] * (2 * n) + [_ANY] * n_after,
        out_specs=tuple([_SEM] * (2 * n) + [_HBM] * (2 * n) + [pl.BlockSpec(memory_space=pltpu.VMEM)]),
        input_output_aliases={i: 2 * n + i for i in range(2 * n)},
        compiler_params=pltpu.CompilerParams(has_side_effects=_EFFECT),
    )(*[pltpu.with_memory_space_constraint(s, pltpu.HBM) for s in srcs],
      *[pltpu.with_memory_space_constraint(l, pltpu.HBM) for l in lands], *after)
    handles = [(res[ci], res[n + ci], res[2 * n + ci], res[3 * n + ci]) for ci in range(n)]
    return handles, res[-1]


def _exchange_wait(handle, after, name):
    send_sem, recv_sem, src_thru, land_thru = handle

    def body(src_ref, land_ref, send_ref, recv_ref, after_ref, src_out, land_out):
        seven = land_ref.at[pl.ds(0, N_DEV - 1)]
        copies = pltpu.make_async_remote_copy(
            src_ref=seven, dst_ref=seven, send_sem=send_ref, recv_sem=recv_ref,
            device_id=(lax.axis_index("x"), lax.axis_index("y"), lax.axis_index("c")), device_id_type=_MESH_ID)
        copies.wait_send()
        copies.wait_recv()

    return pl.pallas_call(
        body, name=name,
        out_shape=(pltpu.HBM(src_thru.shape, src_thru.dtype), pltpu.HBM(land_thru.shape, land_thru.dtype)),
        in_specs=(_HBM, _HBM, _SEM, _SEM, _ANY), out_specs=(_HBM, _HBM), input_output_aliases={0: 0, 1: 1},
        compiler_params=pltpu.CompilerParams(has_side_effects=_EFFECT),
    )(src_thru, land_thru, send_sem, recv_sem, after)[1]


def _own_slot(land, own, me):
    return lax.dynamic_update_index_in_dim(land, own, me, axis=0)


def _sum_slots(slots, name):
    n, r, w = slots.shape
    tr = _tile(r, 128, WIRE_ROW_ALIGN)

    def body(s_ref, o_ref):
        acc = s_ref[0].astype(F32)
        for d in range(1, n):
            acc = acc + s_ref[d].astype(F32)
        o_ref[...] = acc

    return pl.pallas_call(
        body, name=name, grid=(r // tr,),
        in_specs=[pl.BlockSpec((n, tr, w), lambda i: (0, i, 0))],
        out_specs=pl.BlockSpec((tr, w), lambda i: (i, 0)),
        out_shape=jax.ShapeDtypeStruct((r, w), F32),
        compiler_params=_cparams(("parallel",), 2 * _nbytes((n, tr, w), slots.dtype) + 4 * _nbytes((tr, w), F32)),
    )(slots)


def _adamw(w, g, m, v, name):
    r, c = w.shape
    tr = _tile(r, 512, 8)

    def body(w_ref, g_ref, m_ref, v_ref, d_ref, mo_ref, vo_ref):
        gv = g_ref[...]
        m_new = ADAM_B1 * m_ref[...] + (1.0 - ADAM_B1) * gv
        v_new = ADAM_B2 * v_ref[...] + (1.0 - ADAM_B2) * (gv * gv)
        m_hat = m_new / (1.0 - ADAM_B1 ** ADAM_STEP)
        v_hat = v_new / (1.0 - ADAM_B2 ** ADAM_STEP)
        d_ref[...] = -ADAM_LR * (m_hat / (jnp.sqrt(v_hat) + ADAM_EPS) + ADAM_WD * w_ref[...])
        mo_ref[...] = m_new
        vo_ref[...] = v_new

    blk = pl.BlockSpec((tr, c), lambda i: (i, 0))
    shp = jax.ShapeDtypeStruct((r, c), F32)
    return pl.pallas_call(
        body, name=name, grid=(r // tr,), in_specs=[blk] * 4, out_specs=[blk] * 3, out_shape=[shp] * 3,
        compiler_params=_cparams(("parallel",), 16 * _nbytes((tr, _round_up(c, LANES)), F32)),
    )(w, g, m, v)


def _pack_rows(parts, width, dtype, row_align):
    rows, spans, off = [], [], 0
    for p in parts:
        flat = p.reshape(-1).astype(dtype)
        n_rows = _round_up(-(-flat.shape[0] // width), row_align)
        flat = jnp.pad(flat, (0, n_rows * width - flat.shape[0]))
        rows.append(flat.reshape(n_rows, width))
        spans.append((off, n_rows))
        off += n_rows
    return jnp.concatenate(rows, axis=0), spans


def _unpack_rows(mat, span, shape):
    off, n_rows = span
    n = 1
    for s in shape:
        n *= s
    return mat[..., off:off + n_rows, :].reshape(mat.shape[:-2] + (-1,))[..., :n].reshape(mat.shape[:-2] + tuple(shape))


def _block_diag(w, size):
    n, b, _ = w.shape
    eye = jnp.eye(n, dtype=w.dtype)
    dense = (w[:, :, None, :] * eye[:, None, :, None]).reshape(n * b, n * b)
    return jnp.pad(dense, ((0, size - n * b), (0, size - n * b)))


def _diag_blocks(dense, n, b):
    return jnp.stack([dense[k * b:(k + 1) * b, k * b:(k + 1) * b] for k in range(n)])


def _pad_rows(a, rows):
    return jnp.pad(a, ((0, rows - a.shape[0]), (0, 0)))


def _pad_cols(a, cols):
    return jnp.pad(a, ((0, 0), (0, cols - a.shape[1])))


def _train_step(a):
    x = a["x"][0]
    target = a["loss_target"][0]
    s_dim, d = x.shape
    n_layers = a["ffn1_pre_g"].shape[0]
    f_shard = a["ffn1_w_gate"].shape[2]
    c_shard = a["rg_conv_b"].shape[1]
    c_dim = c_shard * N_DEV
    cp = _round_up(c_dim, LANES)
    conv_width = a["rg_conv_w"].shape[1]
    n_blocks, lru_block = a["rg_w_a"].shape[1], a["rg_w_a"].shape[2]
    d_attn = a["attn_w_q"].shape[2]
    n_heads = a["b_fgate"].shape[0]
    d_head = d_attn // n_heads
    attn_scale = d_head ** -0.5
    assert conv_width < 8 and n_heads <= LANES and n_layers == 2
    assert d_attn == d
    me = 4 * lax.axis_index("x") + 2 * lax.axis_index("y") + lax.axis_index("c")

    shard = {"rg_w_in": a["rg_w_in"][0].T, "rg_w_out": a["rg_w_out"][0], "w_kv": a["w_kv"].T,
             "attn_w_q": a["attn_w_q"][0], "attn_w_o": a["attn_w_o"][0]}
    for l in range(n_layers):
        for f in ("ffn1", "ffn2"):
            shard[(f, "gate", l)] = a[f + "_w_gate"][l].T
            shard[(f, "up", l)] = a[f + "_w_up"][l].T
            shard[(f, "down", l)] = a[f + "_w_down"][l]

    def ffn_names(f, l):
        return [(f, "gate", l), (f, "up", l), (f, "down", l)]

    def chunk_layout(names):
        spans, off = [], 0
        for nm in names:
            spans.append((nm, off, shard[nm].shape[0]))
            off += _round_up(shard[nm].shape[0], WIRE_ROW_ALIGN)
        return spans, off

    def pack_chunk(names, parts):
        return jnp.concatenate(
            [_pad_rows(parts[nm].astype(WIRE_DTYPE), _round_up(parts[nm].shape[0], WIRE_ROW_ALIGN)) for nm in names], axis=0)

    full = {}

    def unpack_chunk(names, gathered):
        for nm, o, n_rows in chunk_layout(names)[0]:
            full[nm] = gathered[:, o:o + n_rows, :].reshape(N_DEV * n_rows, d)

    fwd_chunks = [ffn_names("ffn1", 0)[:2], ffn_names("ffn1", 0)[2:], ["rg_w_in"], ["rg_w_out"],
                  ffn_names("ffn2", 0) + ["w_kv"], ffn_names("ffn1", 1) + ["attn_w_q", "attn_w_o"], ffn_names("ffn2", 1)]
    fwd_packs = [pack_chunk(names, shard) for names in fwd_chunks]
    unpack_chunk(fwd_chunks[0], _all_gather(fwd_packs[0], "gather_weights_first"))

    small_parts = [a["rg_conv_w"][0], a["rg_conv_b"][0], a["rg_b_a"][0], a["rg_b_x"][0], a["rg_lambda"][0], a["w_fgate"]]
    small_pack, small_spans = _pack_rows(small_parts, d, F32, 8)
    small_all = _all_gather(small_pack, "gather_small")
    fwd_handles, fwd_token = _exchange_start(fwd_packs[1:], False, [full[("ffn1", "up", 0)], small_all],
                                             "gather_weights_start")

    def land_weights(n, after):
        land = _exchange_wait(fwd_handles[n - 1], after, f"gather_weights_wait_{n}")
        unpack_chunk(fwd_chunks[n], _own_slot(land, fwd_packs[n], me))

    sm = [_unpack_rows(small_all, sp, p.shape) for sp, p in zip(small_spans, small_parts)]
    conv_w = jnp.moveaxis(sm[0], 0, 1).reshape(conv_width, c_dim)
    conv_b, b_a, b_x, lam = (v.reshape(1, c_dim) for v in sm[1:5])
    w_f = sm[5].reshape(d, n_heads)

    pconv = _pad_rows(_pad_cols(jnp.concatenate([conv_w, conv_b], axis=0), cp), 8)
    pvec = _pad_rows(_pad_cols(jnp.concatenate([b_a, b_x, lam], axis=0), cp), 8)
    wa_dense = _block_diag(a["rg_w_a"][0], cp).astype(MXU_DTYPE)
    wx_dense = _block_diag(a["rg_w_x"][0], cp).astype(MXU_DTYPE)
    wax = jnp.concatenate([wa_dense, wx_dense], axis=1)
    w_f_t = _pad_rows(w_f.T.astype(MXU_DTYPE), LANES)
    b_f = _pad_rows(_pad_cols(a["b_fgate"].reshape(1, n_heads), LANES), 8)

    def gain(name, l):
        return a[name][l].reshape(1, d)

    def ffn_fwd(h, f, l, after=None, down_chunk=None, xn=None, next_gain=None):
        if xn is None:
            xn = _rms_fwd(h, gain(f + "_pre_g", l), f"{f}_{l}_pre_norm", after)
        g, u, act = _ffn_up(xn, full[(f, "gate", l)], full[(f, "up", l)], f"{f}_{l}_up")
        if down_chunk is not None:
            land_weights(down_chunk, act)
        fo, h_new, *nxt = _mm_rms_res(act, full[(f, "down", l)], h, gain(f + "_post_g", l), 0.5, f"{f}_{l}_down",
                                      next_gain)
        return h_new, (h, xn, g, u, act, fo), (nxt[0] if nxt else None)

    h0 = x
    h0a, sv_f1_0, hn_rg = ffn_fwd(h0, "ffn1", 0, fwd_token, down_chunk=1, next_gain=gain("mix_pre_g", 0))
    land_weights(2, h0a)
    w_in_gate = _pad_rows(full["rg_w_in"][:c_dim], cp)
    w_in_rec = _pad_rows(full["rg_w_in"][c_dim:], cp)
    w_in_t = jnp.concatenate([w_in_gate, w_in_rec], axis=0)
    gx = _mm([(hn_rg, w_in_t)], "nt", F32, "rg_in_proj")
    rec = _conv_fwd(gx, pconv, conv_width, "rg_conv")
    gates = _mm([(rec, wax)], "nn", F32, "rg_gate_proj")
    h_rec, y_rg = _scan_fwd(gx, rec, gates, pvec, "rg_scan")
    land_weights(3, y_rg)
    w_out = _pad_rows(full["rg_w_out"], cp)
    m_rg, h0b, xn_f2 = _mm_rms_res(y_rg, w_out, h0a, gain("mix_post_g", 0), 1.0, "rg_out_proj", gain("ffn2_pre_g", 0))
    land_weights(4, h0b)
    h1, sv_f2_0, hn_kv = ffn_fwd(h0b, "ffn2", 0, xn=xn_f2, next_gain=a["kv_norm_g"].reshape(1, d))
    kv = _mm([(hn_kv, full["w_kv"])], "nt", MXU_DTYPE, "kv_proj")
    fpre = _mm([(hn_kv, w_f_t)], "nt", F32, "fgate_proj")
    c_cum = _fgate_fwd(fpre, b_f, "fgate_cumsum")
    c_heads = c_cum[:, :n_heads].T
    c_col, c_row = c_heads[:, :, None], c_heads[:, None, :]
    land_weights(5, c_cum)
    h1a, sv_f1_1, hn_at = ffn_fwd(h1, "ffn1", 1, next_gain=gain("mix_pre_g", 1))
    q_s = _mm([(hn_at, full["attn_w_q"])], "nn", MXU_DTYPE, "q_proj", out_scale=attn_scale)
    o2, lse = _pair_attn_fwd(q_s, kv, kv[:, d_attn:].T, c_col, c_row, "attn_fwd")
    m_at, h1b, xn_f2 = _mm_rms_res(o2, full["attn_w_o"], h1a, gain("mix_post_g", 1), 1.0, "attn_out_proj",
                                   gain("ffn2_pre_g", 1))
    land_weights(6, h1b)
    y, sv_f2_1, _ = ffn_fwd(h1b, "ffn2", 1, xn=xn_f2)
    dy, loss_part = _loss_head(y, target, "loss_head")

    grads_big = {}
    grads_rep = {}

    bwd_chunks = [ffn_names("ffn2", 1), ["attn_w_q", "attn_w_o"] + ffn_names("ffn1", 1),
                  ["w_kv"] + ffn_names("ffn2", 0), ["rg_w_in", "rg_w_out"], ffn_names("ffn1", 0)[2:],
                  ffn_names("ffn1", 0)[:2]]
    bwd_sends, bwd_handles = [], []

    def send_grads(after):
        n = len(bwd_sends)
        send = jnp.concatenate(
            [jnp.pad(grads_big[nm].reshape(N_DEV, n_rows, d), ((0, 0), (0, _round_up(n_rows, WIRE_ROW_ALIGN) - n_rows), (0, 0)))
             for nm, _, n_rows in chunk_layout(bwd_chunks[n])[0]], axis=1)
        handles, token = _exchange_start([send], True, [after], f"exchange_grads_start_{n}")
        bwd_sends.append(send)
        bwd_handles.append(handles[0])
        return token

    def ffn_bwd(dh_out, saved, f, l, after=None, send_now=False):
        h, xn, g, u, act, fo = saved
        df, d_post = _rms_bwd(fo, gain(f + "_post_g", l), [dh_out], None, 0.5, MXU_DTYPE, f"{f}_{l}_post_norm_bwd", after)
        dg, du = _ffn_act_bwd(df, full[(f, "down", l)], g, u, f"{f}_{l}_act_bwd")
        grads_big[(f, "down", l)] = _mm([(act, df)], "tn", WIRE_DTYPE, f"{f}_{l}_dw_down")
        sent = send_grads(df) if send_now else None
        grads_big[(f, "gate", l)] = _mm([(dg, xn)], "tn", WIRE_DTYPE, f"{f}_{l}_dw_gate", sent)
        grads_big[(f, "up", l)] = _mm([(du, xn)], "tn", WIRE_DTYPE, f"{f}_{l}_dw_up")
        sent = send_grads(df) if send_now else None
        dxn = _mm([(dg, full[(f, "gate", l)]), (du, full[(f, "up", l)])], "nn", F32, f"{f}_{l}_dx", sent)
        dh_in, d_pre = _rms_bwd(h, gain(f + "_pre_g", l), [dxn], dh_out, 1.0, F32, f"{f}_{l}_pre_norm_bwd")
        grads_rep[(f + "_post_g", l)] = d_post
        grads_rep[(f + "_pre_g", l)] = d_pre
        return dh_in

    dh = ffn_bwd(dy, sv_f2_1, "ffn2", 1)
    token = send_grads(dh)
    dm, d_post = _rms_bwd(m_at, gain("mix_post_g", 1), [dh], None, 1.0, MXU_DTYPE, "attn_post_norm_bwd", token)
    grads_rep[("mix_post_g", 1)] = d_post
    do2 = _mm([(dm, full["attn_w_o"])], "nt", F32, "attn_out_proj_dx")
    grads_big["attn_w_o"] = _mm([(o2, dm)], "tn", WIRE_DTYPE, "attn_out_proj_dw")
    delta = _attn_delta(do2, o2, n_heads, "attn_delta")
    dq2, dc_q, dk2, dv2, dc_k = _pair_attn_bwd(q_s, kv, c_col, c_row, lse, delta, do2, attn_scale, "attn_bwd")
    dc_heads = dc_q[:, 0, :] + dc_k[:, :, 0]
    dhn = _mm([(dq2, full["attn_w_q"])], "nt", F32, "q_proj_dx")
    grads_big["attn_w_q"] = _mm([(hn_at, dq2)], "tn", WIRE_DTYPE, "q_proj_dw")
    dh, d_pre = _rms_bwd(h1a, gain("mix_pre_g", 1), [dhn], dh, 1.0, F32, "attn_pre_norm_bwd")
    grads_rep[("mix_pre_g", 1)] = d_pre
    dh = ffn_bwd(dh, sv_f1_1, "ffn1", 1)
    token = send_grads(dh)
    dc_cum = _pad_cols(dc_heads.T, LANES)
    dfpre, db_f = _fgate_bwd(dc_cum, fpre, b_f, "fgate_cumsum_bwd")
    dhn_kv = _mm([(dk2, full["w_kv"][:d_attn]), (dv2, full["w_kv"][d_attn:])], "nn", F32, "kv_proj_dx")
    dhn_f = _mm([(dfpre, w_f_t)], "nn", F32, "fgate_proj_dx")
    grads_big["w_kv"] = jnp.concatenate([_mm([(dk2, hn_kv)], "tn", WIRE_DTYPE, "kv_proj_dw_k"),
                                         _mm([(dv2, hn_kv)], "tn", WIRE_DTYPE, "kv_proj_dw_v")], axis=0)
    dw_f_t = _mm([(dfpre, hn_kv)], "tn", F32, "fgate_proj_dw")
    dh, d_kvg = _rms_bwd(h1, a["kv_norm_g"].reshape(1, d), [dhn_kv, dhn_f], dh, 1.0, F32, "kv_norm_bwd", token)
    dh = ffn_bwd(dh, sv_f2_0, "ffn2", 0)
    token = send_grads(dh)
    dm, d_post = _rms_bwd(m_rg, gain("mix_post_g", 0), [dh], None, 1.0, MXU_DTYPE, "rg_post_norm_bwd", token)
    grads_rep[("mix_post_g", 0)] = d_post
    dy_rg = _mm([(dm, w_out)], "nt", F32, "rg_out_proj_dx")
    dw_out = _mm([(y_rg, dm)], "tn", WIRE_DTYPE, "rg_out_proj_dw")
    dgate, dra, dia, drec1, dpvec = _scan_bwd(dy_rg, gx, h_rec, rec, gates, pvec, "rg_scan_bwd")
    drec2 = _mm([(dra, wa_dense), (dia, wx_dense)], "nt", F32, "rg_gate_proj_dx")
    dwa_dense = _mm([(rec, dra)], "tn", F32, "rg_gate_proj_dwa")
    dwx_dense = _mm([(rec, dia)], "tn", F32, "rg_gate_proj_dwx")
    drec0, dpconv = _conv_bwd(drec1, drec2, gx, pconv, conv_width, "rg_conv_bwd")
    dhn = _mm([(dgate, w_in_gate), (drec0, w_in_rec)], "nn", F32, "rg_in_proj_dx")
    dw_in_gate = _mm([(dgate, hn_rg)], "tn", WIRE_DTYPE, "rg_in_proj_dw_gate")
    dw_in_rec = _mm([(drec0, hn_rg)], "tn", WIRE_DTYPE, "rg_in_proj_dw_rec")
    dh, d_pre = _rms_bwd(h0a, gain("mix_pre_g", 0), [dhn], dh, 1.0, F32, "rg_pre_norm_bwd")
    grads_rep[("mix_pre_g", 0)] = d_pre
    grads_big["rg_w_in"] = jnp.concatenate([dw_in_gate[:c_dim], dw_in_rec[:c_dim]], axis=0)
    grads_big["rg_w_out"] = dw_out[:c_dim]
    token = send_grads(dh)
    grad_x = ffn_bwd(dh, sv_f1_0, "ffn1", 0, token, send_now=True)

    g_shard = {}

    def land_grads(n, after):
        land = _exchange_wait(bwd_handles[n], after, f"exchange_grads_wait_{n}")
        own = lax.dynamic_index_in_dim(bwd_sends[n], me, axis=0, keepdims=False)
        g_chunk = _sum_slots(_own_slot(land, own, me), f"sum_weight_grads_{n}")
        for nm, o, n_rows in chunk_layout(bwd_chunks[n])[0]:
            g_shard[nm] = g_chunk[o:o + n_rows]

    for n in range(len(bwd_chunks) - 2):
        land_grads(n, grad_x)

    def gain_grad(name):
        return jnp.concatenate([grads_rep[(name, l)] for l in range(n_layers)], axis=0)

    rep_names = ["ffn1_pre_g", "ffn1_post_g", "mix_pre_g", "mix_post_g", "ffn2_pre_g", "ffn2_post_g"]
    rep_parts = [gain_grad(nm) for nm in rep_names]
    rep_names += ["kv_norm_g", "b_fgate", "rg_w_a", "rg_w_x", "rg_conv_w", "rg_conv_b", "rg_b_a", "rg_b_x", "rg_lambda", "w_fgate"]
    rep_parts += [
        d_kvg, db_f[0, :n_heads],
        _diag_blocks(dwa_dense, n_blocks, lru_block), _diag_blocks(dwx_dense, n_blocks, lru_block),
        dpconv[:conv_width, :c_dim], dpconv[conv_width, :c_dim],
        dpvec[0, :c_dim], dpvec[1, :c_dim], dpvec[2, :c_dim],
        dw_f_t[:n_heads].T]
    rep_pack, rep_spans = _pack_rows(rep_parts, d, F32, 8)
    rep_sum = _sum_slots(_all_gather(rep_pack, "gather_small_grads"), "sum_small_grads")
    g_rep = {nm: _unpack_rows(rep_sum, sp, p.shape) for nm, sp, p in zip(rep_names, rep_spans, rep_parts)}

    def my_cols(full_grad, n):
        return lax.dynamic_slice_in_dim(full_grad, me * n, n, axis=full_grad.ndim - 1)

    def ffn_grads(f):
        grad[f + "_w_gate"] = jnp.stack([g_shard[(f, "gate", l)].T for l in range(n_layers)])
        grad[f + "_w_up"] = jnp.stack([g_shard[(f, "up", l)].T for l in range(n_layers)])
        grad[f + "_w_down"] = jnp.stack([g_shard[(f, "down", l)] for l in range(n_layers)])

    grad = {}
    for nm in ("ffn1_pre_g", "ffn1_post_g", "mix_pre_g", "mix_post_g", "ffn2_pre_g", "ffn2_post_g"):
        grad[nm] = g_rep[nm]
    ffn_grads("ffn2")
    grad["rg_w_in"] = g_shard["rg_w_in"].T[None]
    grad["rg_conv_w"] = my_cols(g_rep["rg_conv_w"], c_shard)[None]
    for nm in ("rg_conv_b", "rg_b_a", "rg_b_x", "rg_lambda"):
        grad[nm] = my_cols(g_rep[nm], c_shard)[None]
    grad["rg_w_a"] = g_rep["rg_w_a"][None]
    grad["rg_w_x"] = g_rep["rg_w_x"][None]
    grad["rg_w_out"] = g_shard["rg_w_out"][None]
    grad["kv_norm_g"] = g_rep["kv_norm_g"].reshape(d)
    grad["w_kv"] = g_shard["w_kv"].T
    grad["w_fgate"] = lax.dynamic_slice_in_dim(g_rep["w_fgate"], me * (d // N_DEV), d // N_DEV, axis=0)
    grad["b_fgate"] = g_rep["b_fgate"]
    grad["attn_w_q"] = g_shard["attn_w_q"][None]
    grad["attn_w_o"] = g_shard["attn_w_o"][None]

    delta, new_m, new_v = {}, {}, {}

    def adamw(nm):
        w = a[nm]
        shape = w.shape
        two_d = (1, shape[0]) if w.ndim == 1 else (-1, shape[-1])
        ops = [pltpu.with_memory_space_constraint(t.reshape(two_d), pltpu.HBM)
               for t in (w, grad[nm], a["m_" + nm], a["v_" + nm])]
        dl, mo, vo = _adamw(*ops, "adamw_" + nm)
        delta[nm], new_m[nm], new_v[nm] = dl.reshape(shape), mo.reshape(shape), vo.reshape(shape)
        grad[nm] = grad[nm].reshape(shape)

    last_names = ("ffn1_w_gate", "ffn1_w_up", "ffn1_w_down")
    for nm in WEIGHT_NAMES:
        if nm not in last_names:
            adamw(nm)
    land_grads(len(bwd_chunks) - 2, delta["attn_w_o"])
    land_grads(len(bwd_chunks) - 1, delta["attn_w_o"])
    ffn_grads("ffn1")
    for nm in last_names:
        adamw(nm)

    loss = lax.psum(loss_part[0, 0], AXES)
    return (loss, grad_x[None], *[grad[n] for n in WEIGHT_NAMES], *[delta[n] for n in WEIGHT_NAMES],
            *[new_m[n] for n in WEIGHT_NAMES], *[new_v[n] for n in WEIGHT_NAMES])


def kernel(x, ffn1_pre_g, ffn1_w_gate, ffn1_w_up, ffn1_w_down, ffn1_post_g, mix_pre_g, mix_post_g, ffn2_pre_g, ffn2_w_gate, ffn2_w_up, ffn2_w_down, ffn2_post_g, rg_w_in, rg_conv_w, rg_conv_b, rg_w_a, rg_b_a, rg_w_x, rg_b_x, rg_lambda, rg_w_out, kv_norm_g, w_kv, w_fgate, b_fgate, attn_w_q, attn_w_o, loss_target, m_ffn1_pre_g, m_ffn1_w_gate, m_ffn1_w_up, m_ffn1_w_down, m_ffn1_post_g, m_mix_pre_g, m_mix_post_g, m_ffn2_pre_g, m_ffn2_w_gate, m_ffn2_w_up, m_ffn2_w_down, m_ffn2_post_g, m_rg_w_in, m_rg_conv_w, m_rg_conv_b, m_rg_w_a, m_rg_b_a, m_rg_w_x, m_rg_b_x, m_rg_lambda, m_rg_w_out, m_kv_norm_g, m_w_kv, m_w_fgate, m_b_fgate, m_attn_w_q, m_attn_w_o, v_ffn1_pre_g, v_ffn1_w_gate, v_ffn1_w_up, v_ffn1_w_down, v_ffn1_post_g, v_mix_pre_g, v_mix_post_g, v_ffn2_pre_g, v_ffn2_w_gate, v_ffn2_w_up, v_ffn2_w_down, v_ffn2_post_g, v_rg_w_in, v_rg_conv_w, v_rg_conv_b, v_rg_w_a, v_rg_b_a, v_rg_w_x, v_rg_b_x, v_rg_lambda, v_rg_w_out, v_kv_norm_g, v_w_kv, v_w_fgate, v_b_fgate, v_attn_w_q, v_attn_w_o):
    return _train_step(dict(locals()))
```

```python
import functools

import jax
import jax.numpy as jnp
from jax import lax
from jax.experimental import pallas as pl
from jax.experimental.pallas import tpu as pltpu

F32 = jnp.float32
MXU_DTYPE = jnp.bfloat16
WIRE_DTYPE = jnp.bfloat16
N_DEV = 8
AXES = ("x", "y", "c")
LANES = 128
WIRE_ROW_ALIGN = 16
VMEM_LIMIT_MIN = 32 * 2 ** 20
VMEM_LIMIT_MAX = 56 * 2 ** 20

RMS_EPS = 1e-6
LRU_C = 8.0
ADAM_LR, ADAM_B1, ADAM_B2, ADAM_EPS, ADAM_WD, ADAM_STEP = 0.001, 0.9, 0.999, 1e-08, 0.01, 10

WEIGHT_NAMES = (
    "ffn1_pre_g", "ffn1_w_gate", "ffn1_w_up", "ffn1_w_down", "ffn1_post_g", "mix_pre_g", "mix_post_g",
    "ffn2_pre_g", "ffn2_w_gate", "ffn2_w_up", "ffn2_w_down", "ffn2_post_g", "rg_w_in", "rg_conv_w",
    "rg_conv_b", "rg_w_a", "rg_b_a", "rg_w_x", "rg_b_x", "rg_lambda", "rg_w_out", "kv_norm_g", "w_kv",
    "w_fgate", "b_fgate", "attn_w_q", "attn_w_o")


def _round_up(n, m):
    return (n + m - 1) // m * m


def _tile(dim, target, align=LANES):
    if dim <= target:
        return dim
    best = None
    t = align
    while t <= target:
        if dim % t == 0:
            best = t
        t += align
    return dim if best is None else best


def _cparams(semantics, vmem_estimate):
    limit = min(VMEM_LIMIT_MAX, max(VMEM_LIMIT_MIN, 2 * int(vmem_estimate)))
    return pltpu.CompilerParams(dimension_semantics=semantics, vmem_limit_bytes=limit)


def _nbytes(shape, dtype):
    n = 1
    for s in shape:
        n *= s
    return n * jnp.dtype(dtype).itemsize


def _sigmoid(x):
    return jax.nn.sigmoid(x)


def _softplus(x):
    return jnp.maximum(x, 0.0) + jnp.log1p(jnp.exp(-jnp.abs(x)))


def _expm1(x):
    series = x * (1.0 + x * (0.5 + x * (1.0 / 6.0 + x * (1.0 / 24.0 + x * (1.0 / 120.0)))))
    return jnp.where(jnp.abs(x) < 0.25, series, jnp.exp(x) - 1.0)


_GELU_C = 0.7978845608028654
_GELU_A = 0.044715


def _gelu(x):
    return 0.5 * x * (1.0 + jnp.tanh(_GELU_C * (x + _GELU_A * x * x * x)))


def _gelu_grad(x):
    t = jnp.tanh(_GELU_C * (x + _GELU_A * x * x * x))
    return 0.5 * (1.0 + t) + 0.5 * x * (1.0 - t * t) * _GELU_C * (1.0 + 3.0 * _GELU_A * x * x)


_DOT_DIMS = {"nn": ((1,), (0,)), "nt": ((1,), (1,)), "tn": ((0,), (0,))}


def _dot(a, b, mode):
    return lax.dot_general(a.astype(MXU_DTYPE), b.astype(MXU_DTYPE), (_DOT_DIMS[mode], ((), ())),
                           preferred_element_type=F32)


def _mm(pairs, mode, out_dtype, name, after=None, out_scale=None):
    a0, b0 = pairs[0]
    if mode == "tn":
        k_dim, m_dim = a0.shape
        n_dim = b0.shape[1]
    else:
        m_dim, k_dim = a0.shape
        n_dim = b0.shape[0] if mode == "nt" else b0.shape[1]
    for a, b in pairs:
        assert a.shape == a0.shape and b.shape == b0.shape
    tm = _tile(m_dim, 1408 if mode == "tn" else 512)
    whole = 1408 if mode == "tn" else 2816
    tn = _tile(n_dim, whole)
    tk = _tile(k_dim, whole)
    nk = k_dim // tk
    n_pairs = len(pairs)

    if mode == "tn":
        a_spec = pl.BlockSpec((tk, tm), lambda i, j, k: (k, i))
    else:
        a_spec = pl.BlockSpec((tm, tk), lambda i, j, k: (i, k))
    if mode == "nt":
        b_spec = pl.BlockSpec((tn, tk), lambda i, j, k: (j, k))
    else:
        b_spec = pl.BlockSpec((tk, tn), lambda i, j, k: (k, j))

    order = [] if after is None else [after]

    def body(*refs):
        ins, o_ref, acc = refs[:2 * n_pairs], refs[-2], refs[-1]
        k = pl.program_id(2)

        @pl.when(k == 0)
        def _():
            acc[...] = jnp.zeros_like(acc)

        s = acc[...]
        for p in range(n_pairs):
            s = s + _dot(ins[2 * p][...], ins[2 * p + 1][...], mode)
        acc[...] = s

        @pl.when(k == nk - 1)
        def _():
            r = acc[...] if out_scale is None else acc[...] * out_scale
            o_ref[...] = r.astype(out_dtype)

    est = (2 * n_pairs * (_nbytes((tm, tk), a0.dtype) + _nbytes((tk, tn), b0.dtype))
           + 2 * _nbytes((tm, tn), out_dtype) + 2 * _nbytes((tm, tn), F32))
    flat = [t for ab in pairs for t in ab]
    return pl.pallas_call(
        body, name=name, grid=(m_dim // tm, n_dim // tn, nk),
        in_specs=[a_spec, b_spec] * n_pairs + [_ANY] * len(order),
        out_specs=pl.BlockSpec((tm, tn), lambda i, j, k: (i, j)),
        out_shape=jax.ShapeDtypeStruct((m_dim, n_dim), out_dtype),
        scratch_shapes=[pltpu.VMEM((tm, tn), F32)],
        compiler_params=_cparams(("parallel", "parallel", "arbitrary"), est),
    )(*flat, *order)


_ANY = pl.BlockSpec(memory_space=pl.ANY)


def _rms_fwd(x, gain, name, after=None):
    s_dim, d = x.shape
    tm = _tile(s_dim, 512, 8)

    def body(*refs):
        x_ref, g_ref, o_ref = refs[0], refs[1], refs[-1]
        v = x_ref[...]
        r = lax.rsqrt(jnp.mean(v * v, axis=-1, keepdims=True) + RMS_EPS)
        o_ref[...] = (v * r * g_ref[...]).astype(MXU_DTYPE)

    order = [] if after is None else [after]
    return pl.pallas_call(
        body, name=name, grid=(s_dim // tm,),
        in_specs=[pl.BlockSpec((tm, d), lambda i: (i, 0)), pl.BlockSpec((1, d), lambda i: (0, 0))] + [_ANY] * len(order),
        out_specs=pl.BlockSpec((tm, d), lambda i: (i, 0)),
        out_shape=jax.ShapeDtypeStruct((s_dim, d), MXU_DTYPE),
        compiler_params=_cparams(("parallel",), 6 * _nbytes((tm, d), F32)),
    )(x, gain, *order)


def _rms_bwd(x, gain, dys, res, scale, out_dtype, name, after=None):
    s_dim, d = x.shape
    tm = _tile(s_dim, 512, 8)
    n_dy = len(dys)
    has_res = res is not None
    order = [] if after is None else [after]

    def body(*refs):
        x_ref, g_ref = refs[0], refs[1]
        dy_refs = refs[2:2 + n_dy]
        res_ref = refs[2 + n_dy] if has_res else None
        dx_ref, dg_ref = refs[-2], refs[-1]

        @pl.when(pl.program_id(0) == 0)
        def _():
            dg_ref[...] = jnp.zeros_like(dg_ref)

        v = x_ref[...]
        r = lax.rsqrt(jnp.mean(v * v, axis=-1, keepdims=True) + RMS_EPS)
        xh = v * r
        dy = dy_refs[0][...].astype(F32)
        for extra in dy_refs[1:]:
            dy = dy + extra[...].astype(F32)
        gd = dy * g_ref[...]
        dx = scale * r * (gd - xh * jnp.mean(gd * xh, axis=-1, keepdims=True))
        if has_res:
            dx = dx + res_ref[...]
        dx_ref[...] = dx.astype(out_dtype)
        dg_ref[...] += scale * jnp.sum(dy * xh, axis=0, keepdims=True)

    row = pl.BlockSpec((tm, d), lambda i: (i, 0))
    vec = pl.BlockSpec((1, d), lambda i: (0, 0))
    ops = [x, gain] + list(dys) + ([res] if has_res else [])
    return pl.pallas_call(
        body, name=name, grid=(s_dim // tm,),
        in_specs=[row, vec] + [row] * (n_dy + int(has_res)) + [_ANY] * len(order),
        out_specs=[row, vec],
        out_shape=[jax.ShapeDtypeStruct((s_dim, d), out_dtype), jax.ShapeDtypeStruct((1, d), F32)],
        compiler_params=_cparams(("arbitrary",), (2 * len(ops) + 6) * _nbytes((tm, d), F32)),
    )(*ops, *order)


def _mm_rms_res(a, b, h, gain, scale, name, next_gain=None):
    s_dim, k_dim = a.shape
    d = b.shape[1]
    tm = _tile(s_dim, 512, 8)
    tk = _tile(k_dim, 2816)
    nk = k_dim // tk

    with_next = next_gain is not None

    def body(*refs):
        a_ref, b_ref, h_ref, g_ref = refs[:4]
        g2_ref = refs[4] if with_next else None
        f_ref, o_ref = refs[4 + int(with_next)], refs[5 + int(with_next)]
        xn_ref = refs[6 + int(with_next)] if with_next else None
        acc = refs[-1]
        k = pl.program_id(1)

        @pl.when(k == 0)
        def _():
            acc[...] = jnp.zeros_like(acc)

        acc[...] += _dot(a_ref[...], b_ref[...], "nn")

        @pl.when(k == nk - 1)
        def _():
            f = acc[...]
            r = lax.rsqrt(jnp.mean(f * f, axis=-1, keepdims=True) + RMS_EPS)
            f_ref[...] = f
            o = h_ref[...] + scale * (f * r * g_ref[...])
            o_ref[...] = o
            if with_next:
                r2 = lax.rsqrt(jnp.mean(o * o, axis=-1, keepdims=True) + RMS_EPS)
                xn_ref[...] = (o * r2 * g2_ref[...]).astype(MXU_DTYPE)

    row = pl.BlockSpec((tm, d), lambda i, k: (i, 0))
    vec = pl.BlockSpec((1, d), lambda i, k: (0, 0))
    est = (2 * (_nbytes((tm, tk), a.dtype) + _nbytes((tk, d), b.dtype)) + 10 * _nbytes((tm, d), F32))
    wide = jax.ShapeDtypeStruct((s_dim, d), F32)
    return pl.pallas_call(
        body, name=name, grid=(s_dim // tm, nk),
        in_specs=[pl.BlockSpec((tm, tk), lambda i, k: (i, k)), pl.BlockSpec((tk, d), lambda i, k: (k, 0)),
                  row, vec] + [vec] * int(with_next),
        out_specs=[row, row] + [row] * int(with_next),
        out_shape=[wide, wide] + [jax.ShapeDtypeStruct((s_dim, d), MXU_DTYPE)] * int(with_next),
        scratch_shapes=[pltpu.VMEM((tm, d), F32)],
        compiler_params=_cparams(("parallel", "arbitrary"), est),
    )(a, b, h, gain, *([next_gain] if with_next else []))


def _ffn_up(xn, wg_t, wu_t, name):
    s_dim, d = xn.shape
    f_dim = wg_t.shape[0]
    tm = _tile(s_dim, 2048, 8)
    tf = _tile(f_dim, 256)

    def body(x_ref, wg_ref, wu_ref, g_ref, u_ref, a_ref):
        x = x_ref[...]
        g = _dot(x, wg_ref[...], "nt")
        u = _dot(x, wu_ref[...], "nt")
        g_ref[...] = g.astype(MXU_DTYPE)
        u_ref[...] = u.astype(MXU_DTYPE)
        a_ref[...] = (g * _sigmoid(g) * u).astype(MXU_DTYPE)

    w_spec = pl.BlockSpec((tf, d), lambda i, j: (j, 0))
    o_spec = pl.BlockSpec((tm, tf), lambda i, j: (i, j))
    o_shape = jax.ShapeDtypeStruct((s_dim, f_dim), MXU_DTYPE)
    est = 2 * _nbytes((tm, d), xn.dtype) + 4 * _nbytes((tf, d), wg_t.dtype) + 10 * _nbytes((tm, tf), F32)
    return pl.pallas_call(
        body, name=name, grid=(s_dim // tm, f_dim // tf),
        in_specs=[pl.BlockSpec((tm, d), lambda i, j: (i, 0)), w_spec, w_spec],
        out_specs=[o_spec, o_spec, o_spec], out_shape=[o_shape, o_shape, o_shape],
        compiler_params=_cparams(("parallel", "parallel"), est),
    )(xn, wg_t, wu_t)


def _ffn_act_bwd(df, wd, g, u, name):
    s_dim, d = df.shape
    f_dim = wd.shape[0]
    tm = _tile(s_dim, 2048, 8)
    tf = _tile(f_dim, 256)

    def body(df_ref, wd_ref, g_ref, u_ref, dg_ref, du_ref):
        dh = _dot(df_ref[...], wd_ref[...], "nt")
        gv = g_ref[...].astype(F32)
        uv = u_ref[...].astype(F32)
        sg = _sigmoid(gv)
        dg_ref[...] = (dh * uv * (sg * (1.0 + gv * (1.0 - sg)))).astype(MXU_DTYPE)
        du_ref[...] = (dh * gv * sg).astype(MXU_DTYPE)

    t_spec = pl.BlockSpec((tm, tf), lambda i, j: (i, j))
    o_shape = jax.ShapeDtypeStruct((s_dim, f_dim), MXU_DTYPE)
    est = 2 * _nbytes((tm, d), df.dtype) + 2 * _nbytes((tf, d), wd.dtype) + 12 * _nbytes((tm, tf), F32)
    return pl.pallas_call(
        body, name=name, grid=(s_dim // tm, f_dim // tf),
        in_specs=[pl.BlockSpec((tm, d), lambda i, j: (i, 0)), pl.BlockSpec((tf, d), lambda i, j: (j, 0)),
                  t_spec, t_spec],
        out_specs=[t_spec, t_spec], out_shape=[o_shape, o_shape],
        compiler_params=_cparams(("parallel", "parallel"), est),
    )(df, wd, g, u)


def _loss_head(y, target, name):
    s_dim, d = y.shape
    tm = _tile(s_dim, 512, 8)
    nt = s_dim // tm

    def body(y_ref, t_ref, dy_ref, loss_ref, acc):
        i = pl.program_id(0)

        @pl.when(i == 0)
        def _():
            acc[...] = jnp.zeros_like(acc)

        e = y_ref[...] - t_ref[...]
        dy_ref[...] = e * (1.0 / d)
        acc[...] += jnp.sum(e * e, axis=0, keepdims=True)

        @pl.when(i == nt - 1)
        def _():
            loss_ref[...] = jnp.sum(acc[...], axis=1, keepdims=True) * (0.5 / d)

    row = pl.BlockSpec((tm, d), lambda i: (i, 0))
    return pl.pallas_call(
        body, name=name, grid=(nt,), in_specs=[row, row],
        out_specs=[row, pl.BlockSpec((1, 1), lambda i: (0, 0))],
        out_shape=[jax.ShapeDtypeStruct((s_dim, d), F32), jax.ShapeDtypeStruct((1, 1), F32)],
        scratch_shapes=[pltpu.VMEM((1, d), F32)],
        compiler_params=_cparams(("arbitrary",), 8 * _nbytes((tm, d), F32)),
    )(y, target)


def _shift_down(v, sh, row):
    if sh == 0:
        return v
    return jnp.where(row >= sh, pltpu.roll(v, sh, 0), 0.0)


def _shift_up(v, sh, row):
    if sh == 0:
        return v
    n = v.shape[0]
    return jnp.where(row < n - sh, pltpu.roll(v, n - sh, 0), 0.0)


def _conv_fwd(gx, pconv, width, name):
    s_dim, cp2 = gx.shape
    cp = cp2 // 2
    nc = cp // LANES

    def body(x_ref, p_ref, o_ref):
        x = x_ref[...]
        row = lax.broadcasted_iota(jnp.int32, x.shape, 0)
        y = jnp.zeros_like(x) + p_ref[pl.ds(width, 1), :]
        for k in range(width):
            y = y + p_ref[pl.ds(k, 1), :] * _shift_down(x, width - 1 - k, row)
        o_ref[...] = y

    return pl.pallas_call(
        body, name=name, grid=(nc,),
        in_specs=[pl.BlockSpec((s_dim, LANES), lambda j: (0, nc + j)), pl.BlockSpec((8, LANES), lambda j: (0, j))],
        out_specs=pl.BlockSpec((s_dim, LANES), lambda j: (0, j)),
        out_shape=jax.ShapeDtypeStruct((s_dim, cp), F32),
        compiler_params=_cparams(("parallel",), 10 * _nbytes((s_dim, LANES), F32)),
    )(gx, pconv)


def _conv_bwd(d1, d2, gx, pconv, width, name):
    s_dim, cp = d1.shape
    nc = cp // LANES

    def body(d1_ref, d2_ref, x_ref, p_ref, dx_ref, dp_ref):
        d = d1_ref[...] + d2_ref[...]
        x = x_ref[...]
        row = lax.broadcasted_iota(jnp.int32, x.shape, 0)
        dx = jnp.zeros_like(d)
        dp_ref[...] = jnp.zeros_like(dp_ref)
        for k in range(width):
            sh = width - 1 - k
            dx = dx + p_ref[pl.ds(k, 1), :] * _shift_up(d, sh, row)
            dp_ref[pl.ds(k, 1), :] = jnp.sum(d * _shift_down(x, sh, row), axis=0, keepdims=True)
        dp_ref[pl.ds(width, 1), :] = jnp.sum(d, axis=0, keepdims=True)
        dx_ref[...] = dx.astype(MXU_DTYPE)

    strip = pl.BlockSpec((s_dim, LANES), lambda j: (0, j))
    par = pl.BlockSpec((8, LANES), lambda j: (0, j))
    return pl.pallas_call(
        body, name=name, grid=(nc,),
        in_specs=[strip, strip, pl.BlockSpec((s_dim, LANES), lambda j: (0, nc + j)), par],
        out_specs=[strip, par],
        out_shape=[jax.ShapeDtypeStruct((s_dim, cp), MXU_DTYPE), jax.ShapeDtypeStruct((8, cp), F32)],
        compiler_params=_cparams(("parallel",), 14 * _nbytes((s_dim, LANES), F32)),
    )(d1, d2, gx, pconv)


def _lru_coeffs(ra, ia, p_ref):
    r = _sigmoid(ra + p_ref[pl.ds(0, 1), :])
    i = _sigmoid(ia + p_ref[pl.ds(1, 1), :])
    sp = _softplus(-p_ref[pl.ds(2, 1), :])
    log_a = -LRU_C * r * sp
    a = jnp.exp(log_a)
    mult = jnp.sqrt(-_expm1(2.0 * log_a))
    return r, i, sp, a, mult


def _scan_fwd(gx, rec, gates, pvec, name):
    s_dim, cp = rec.shape
    ts = _tile(s_dim, 256, 8)
    nt = s_dim // ts

    def body(gate_ref, rec_ref, ra_ref, ia_ref, p_ref, h_ref, y_ref, a_s, u_s, carry):
        @pl.when(pl.program_id(0) == 0)
        def _():
            carry[...] = jnp.zeros_like(carry)

        rec_v = rec_ref[...]
        _, i, _, a, mult = _lru_coeffs(ra_ref[...], ia_ref[...], p_ref)
        a_s[...] = a
        u_s[...] = mult * (i * rec_v)

        def step(t, h):
            h = a_s[pl.ds(t, 1), :] * h + u_s[pl.ds(t, 1), :]
            h_ref[pl.ds(t, 1), :] = h
            return h

        carry[pl.ds(0, 1), :] = lax.fori_loop(0, ts, step, carry[pl.ds(0, 1), :], unroll=8)
        y_ref[...] = (_gelu(gate_ref[...]) * h_ref[...]).astype(MXU_DTYPE)

    blk = pl.BlockSpec((ts, cp), lambda t: (t, 0))
    return pl.pallas_call(
        body, name=name, grid=(nt,),
        in_specs=[blk, blk, blk, pl.BlockSpec((ts, cp), lambda t: (t, 1)), pl.BlockSpec((8, cp), lambda t: (0, 0))],
        out_specs=[blk, blk],
        out_shape=[jax.ShapeDtypeStruct((s_dim, cp), F32), jax.ShapeDtypeStruct((s_dim, cp), MXU_DTYPE)],
        scratch_shapes=[pltpu.VMEM((ts, cp), F32), pltpu.VMEM((ts, cp), F32), pltpu.VMEM((8, cp), F32)],
        compiler_params=_cparams(("arbitrary",), 14 * _nbytes((ts, cp), F32)),
    )(gx, rec, gates, gates, pvec)


def _scan_bwd(dy, gx, hrec, rec, gates, pvec, name):
    s_dim, cp = rec.shape
    ts = _tile(s_dim, 128, 8)
    nt = s_dim // ts

    def body(dy_ref, gate_ref, h_ref, hp_ref, rec_ref, ra_ref, ia_ref, p_ref,
             dgate_ref, dra_ref, dia_ref, drec_ref, dp_ref, a_s, d_s, carry):
        t_id = pl.program_id(0)

        @pl.when(t_id == 0)
        def _():
            carry[...] = jnp.zeros_like(carry)
            dp_ref[...] = jnp.zeros_like(dp_ref)

        rec_v = rec_ref[...]
        r, i, sp, a, mult = _lru_coeffs(ra_ref[...], ia_ref[...], p_ref)
        gate = gate_ref[...]
        dyv = dy_ref[...]
        h = h_ref[...]
        dgate_ref[...] = (dyv * h * _gelu_grad(gate)).astype(MXU_DTYPE)
        a_s[...] = a
        d_s[...] = dyv * _gelu(gate)

        def step(k, c):
            t = ts - 1 - k
            d = d_s[pl.ds(t, 1), :] + c
            d_s[pl.ds(t, 1), :] = d
            return a_s[pl.ds(t, 1), :] * d

        carry[pl.ds(0, 1), :] = lax.fori_loop(0, ts, step, carry[pl.ds(0, 1), :], unroll=8)
        dh = d_s[...]
        row = lax.broadcasted_iota(jnp.int32, h.shape, 0)
        first = jnp.where(t_id == nt - 1, 0.0, 1.0) * hp_ref[pl.ds(7, 1), :]
        h_prev = jnp.where(row == 0, first, pltpu.roll(h, 1, 0))
        dix = dh * mult
        dla = dh * h_prev * a - dh * (i * rec_v) * (a * a) / mult
        dra = dla * (-LRU_C * sp) * r * (1.0 - r)
        dia = dix * rec_v * i * (1.0 - i)
        dra_ref[...] = dra.astype(MXU_DTYPE)
        dia_ref[...] = dia.astype(MXU_DTYPE)
        drec_ref[...] = dix * i
        dsp = jnp.sum(dla * (-LRU_C * r), axis=0, keepdims=True)
        dp_ref[pl.ds(0, 1), :] += jnp.sum(dra, axis=0, keepdims=True)
        dp_ref[pl.ds(1, 1), :] += jnp.sum(dia, axis=0, keepdims=True)
        dp_ref[pl.ds(2, 1), :] += dsp * (-_sigmoid(-p_ref[pl.ds(2, 1), :]))

    blk = pl.BlockSpec((ts, cp), lambda t: (nt - 1 - t, 0))
    prev = pl.BlockSpec((8, cp), lambda t: (jnp.maximum((nt - 1 - t) * (ts // 8) - 1, 0), 0))
    par = pl.BlockSpec((8, cp), lambda t: (0, 0))
    lo = jax.ShapeDtypeStruct((s_dim, cp), MXU_DTYPE)
    return pl.pallas_call(
        body, name=name, grid=(nt,),
        in_specs=[blk, blk, blk, prev, blk, blk, pl.BlockSpec((ts, cp), lambda t: (nt - 1 - t, 1)), par],
        out_specs=[blk, blk, blk, blk, par],
        out_shape=[lo, lo, lo, jax.ShapeDtypeStruct((s_dim, cp), F32), jax.ShapeDtypeStruct((8, cp), F32)],
        scratch_shapes=[pltpu.VMEM((ts, cp), F32), pltpu.VMEM((ts, cp), F32), pltpu.VMEM((8, cp), F32)],
        compiler_params=_cparams(("arbitrary",), 40 * _nbytes((ts, cp), F32)),
    )(dy, gx, hrec, hrec, rec, gates, gates, pvec)


def _fgate_fwd(fpre, bias, name):
    s_dim, w = fpre.shape
    ts = _tile(s_dim, 512, 8)

    def body(f_ref, b_ref, c_ref, lf_s, carry):
        @pl.when(pl.program_id(0) == 0)
        def _():
            carry[...] = jnp.zeros_like(carry)

        lf_s[...] = -_softplus(-(f_ref[...] + b_ref[pl.ds(0, 1), :]))

        def step(t, c):
            c = c + lf_s[pl.ds(t, 1), :]
            c_ref[pl.ds(t, 1), :] = c
            return c

        carry[pl.ds(0, 1), :] = lax.fori_loop(0, ts, step, carry[pl.ds(0, 1), :], unroll=8)

    blk = pl.BlockSpec((ts, w), lambda t: (t, 0))
    return pl.pallas_call(
        body, name=name, grid=(s_dim // ts,),
        in_specs=[blk, pl.BlockSpec((8, w), lambda t: (0, 0))], out_specs=blk,
        out_shape=jax.ShapeDtypeStruct((s_dim, w), F32),
        scratch_shapes=[pltpu.VMEM((ts, w), F32), pltpu.VMEM((8, w), F32)],
        compiler_params=_cparams(("arbitrary",), 12 * _nbytes((ts, w), F32)),
    )(fpre, bias)


def _fgate_bwd(dc, fpre, bias, name):
    s_dim, w = fpre.shape
    ts = _tile(s_dim, 512, 8)
    nt = s_dim // ts

    def body(dc_ref, f_ref, b_ref, df_ref, db_ref, d_s, carry):
        @pl.when(pl.program_id(0) == 0)
        def _():
            carry[...] = jnp.zeros_like(carry)
            db_ref[...] = jnp.zeros_like(db_ref)

        d_s[...] = dc_ref[...]

        def step(k, c):
            t = ts - 1 - k
            c = c + d_s[pl.ds(t, 1), :]
            d_s[pl.ds(t, 1), :] = c
            return c

        carry[pl.ds(0, 1), :] = lax.fori_loop(0, ts, step, carry[pl.ds(0, 1), :], unroll=8)
        df = d_s[...] * _sigmoid(-(f_ref[...] + b_ref[pl.ds(0, 1), :]))
        df_ref[...] = df
        db_ref[pl.ds(0, 1), :] += jnp.sum(df, axis=0, keepdims=True)

    blk = pl.BlockSpec((ts, w), lambda t: (nt - 1 - t, 0))
    par = pl.BlockSpec((8, w), lambda t: (0, 0))
    return pl.pallas_call(
        body, name=name, grid=(nt,), in_specs=[blk, blk, par], out_specs=[blk, par],
        out_shape=[jax.ShapeDtypeStruct((s_dim, w), F32), jax.ShapeDtypeStruct((8, w), F32)],
        scratch_shapes=[pltpu.VMEM((ts, w), F32), pltpu.VMEM((8, w), F32)],
        compiler_params=_cparams(("arbitrary",), 12 * _nbytes((ts, w), F32)),
    )(dc, fpre, bias)


def _head_lanes(hh, dh):
    lane = lax.broadcasted_iota(jnp.int32, (1, LANES), 1)
    return (lane >= hh * dh) & (lane < (hh + 1) * dh)


def _pair_attn_fwd(q, kv, v_t, c_col, c_row, name):
    s_dim, da = q.shape
    n_h = c_col.shape[0]
    dh = da // n_h
    assert LANES % dh == 0 and da % LANES == 0
    hb = LANES // dh
    n_blocks = da // LANES
    t = _tile(s_dim, 1024, LANES)
    nb = s_dim // t

    pairs = [(i, j) for i in range(nb) for j in range(i + 1)]
    i_tab = jnp.asarray([p[0] for p in pairs], jnp.int32)
    j_tab = jnp.asarray([p[1] for p in pairs], jnp.int32)

    def body(i_ref, j_ref, q_ref, k_ref, vt_ref, cq_ref, ck_ref, o_ref, lse_ref, m_s, l_s, acc):
        i, j = i_ref[pl.program_id(1)], j_ref[pl.program_id(1)]

        @pl.when(j == 0)
        def _():
            m_s[...] = jnp.full_like(m_s, -jnp.inf)
            l_s[...] = jnp.zeros_like(l_s)
            acc[...] = jnp.zeros_like(acc)

        def tile(masked):
            qv = q_ref[...]
            for hh in range(hb):
                st = _dot(k_ref[...], jnp.where(_head_lanes(hh, dh), qv, jnp.zeros_like(qv)), "nt")
                st = st + (cq_ref[hh] - ck_ref[hh])
                if masked:
                    keep = lax.broadcasted_iota(jnp.int32, (t, t), 0) <= lax.broadcasted_iota(jnp.int32, (t, t), 1)
                    st = jnp.where(keep, st, -jnp.inf)
                m_prev = m_s[hh]
                m_new = jnp.maximum(m_prev, jnp.max(st, axis=0, keepdims=True))
                alpha = jnp.exp(m_prev - m_new)
                p = jnp.exp(st - m_new)
                l_s[hh] = alpha * l_s[hh] + jnp.sum(p, axis=0, keepdims=True)
                acc[hh] = alpha * acc[hh] + _dot(vt_ref[...], p, "nn")
                m_s[hh] = m_new

        pl.when(j < i)(functools.partial(tile, False))
        pl.when(j == i)(functools.partial(tile, True))

        @pl.when(j == i)
        def _():
            feat = lax.broadcasted_iota(jnp.int32, (LANES, 1), 0)
            out_t = jnp.zeros((LANES, t), F32)
            for hh in range(hb):
                out_t = jnp.where((feat >= hh * dh) & (feat < (hh + 1) * dh), acc[hh] / l_s[hh], out_t)
                lse_ref[hh] = m_s[hh] + jnp.log(l_s[hh])
            o_ref[...] = out_t.T

    q_spec = pl.BlockSpec((t, LANES), lambda b, p, it, jt: (it[p], b))
    k_spec = pl.BlockSpec((t, LANES), lambda b, p, it, jt: (jt[p], b))
    vt_spec = pl.BlockSpec((LANES, t), lambda b, p, it, jt: (b, jt[p]))
    cq_spec = pl.BlockSpec((hb, 1, t), lambda b, p, it, jt: (b, 0, it[p]))
    ck_spec = pl.BlockSpec((hb, t, 1), lambda b, p, it, jt: (b, jt[p], 0))
    return pl.pallas_call(
        body, name=name,
        grid_spec=pltpu.PrefetchScalarGridSpec(
            num_scalar_prefetch=2, grid=(n_blocks, len(pairs)),
            in_specs=[q_spec, k_spec, vt_spec, cq_spec, ck_spec], out_specs=[q_spec, cq_spec],
            scratch_shapes=[pltpu.VMEM((hb, 1, t), F32), pltpu.VMEM((hb, 1, t), F32), pltpu.VMEM((hb, LANES, t), F32)]),
        out_shape=[jax.ShapeDtypeStruct((s_dim, da), F32), jax.ShapeDtypeStruct((n_h, 1, s_dim), F32)],
        compiler_params=_cparams(("parallel", "arbitrary"), 10 * hb * _nbytes((t, t), F32)),
    )(i_tab, j_tab, q, kv, v_t, c_row, c_col)


def _attn_delta(do, o, n_h, name):
    s_dim, da = o.shape
    dh = da // n_h
    hb = LANES // dh
    t = _tile(s_dim, 1024, LANES)

    def body(do_ref, o_ref, d_ref):
        prod_t = (do_ref[...].astype(MXU_DTYPE).astype(F32) * o_ref[...]).T
        for hh in range(hb):
            d_ref[hh] = jnp.sum(prod_t[hh * dh:(hh + 1) * dh], axis=0, keepdims=True)

    blk = pl.BlockSpec((t, LANES), lambda b, i: (i, b))
    return pl.pallas_call(
        body, name=name, grid=(da // LANES, s_dim // t), in_specs=[blk, blk],
        out_specs=pl.BlockSpec((hb, 1, t), lambda b, i: (b, 0, i)),
        out_shape=jax.ShapeDtypeStruct((n_h, 1, s_dim), F32),
        compiler_params=_cparams(("parallel", "parallel"), 8 * _nbytes((t, LANES), F32)),
    )(do, o)


def _pair_attn_bwd(q, kv, c_col, c_row, lse, delta, do, scale, name):
    s_dim, da = q.shape
    n_h = c_col.shape[0]
    dh = da // n_h
    hb = LANES // dh
    n_blocks = da // LANES
    t = _tile(s_dim, 1024, LANES)
    nb = s_dim // t

    pairs = [(i, j) for j in range(nb) for i in range(j, nb)]
    i_tab = jnp.asarray([p[0] for p in pairs], jnp.int32)
    j_tab = jnp.asarray([p[1] for p in pairs], jnp.int32)

    def body(i_ref, j_ref, q_ref, k_ref, v_ref, cq_ref, ck_ref, lse_ref, dl_ref, do_ref,
             dq_ref, dcq_ref, dk_ref, dv_ref, dck_ref, dk_acc, dv_acc, dck_acc):
        i, j = i_ref[pl.program_id(1)], j_ref[pl.program_id(1)]

        @pl.when(pl.program_id(1) == 0)
        def _():
            dq_ref[...] = jnp.zeros_like(dq_ref)
            dcq_ref[...] = jnp.zeros_like(dcq_ref)

        @pl.when(i == j)
        def _():
            dk_acc[...] = jnp.zeros_like(dk_acc)
            dv_acc[...] = jnp.zeros_like(dv_acc)
            dck_acc[...] = jnp.zeros_like(dck_acc)

        def tile(masked):
            start = pl.multiple_of(i * t, t)
            qv, kv_ = q_ref[...], k_ref[...]
            dov = do_ref[...].astype(MXU_DTYPE)
            for hh in range(hb):
                lanes = _head_lanes(hh, dh)
                qm = jnp.where(lanes, qv, jnp.zeros_like(qv))
                km = jnp.where(lanes, kv_, jnp.zeros_like(kv_))
                dom = jnp.where(lanes, dov, jnp.zeros_like(dov))
                st = _dot(kv_, qm, "nt") + (cq_ref[hh] - ck_ref[hh])
                if masked:
                    keep = lax.broadcasted_iota(jnp.int32, (t, t), 0) <= lax.broadcasted_iota(jnp.int32, (t, t), 1)
                    st = jnp.where(keep, st, -jnp.inf)
                pt = jnp.exp(st - lse_ref[hh])
                dst = pt * (_dot(v_ref[...], dom, "nt") - dl_ref[hh])
                dv_acc[...] += _dot(pt, dom, "nn")
                dk_acc[...] += _dot(dst, qm, "nn")
                dq_ref[pl.ds(start, t), :] += _dot(dst, km, "tn") * scale
                dcq_ref[hh, :, pl.ds(start, t)] += jnp.sum(dst, axis=0, keepdims=True)
                dck_acc[hh] -= jnp.sum(dst, axis=1, keepdims=True)

        pl.when(i > j)(functools.partial(tile, False))
        pl.when(i == j)(functools.partial(tile, True))

        @pl.when(i == nb - 1)
        def _():
            dk_ref[...] = dk_acc[...]
            dv_ref[...] = dv_acc[...]
            dck_ref[...] = dck_acc[...]

    q_spec = pl.BlockSpec((t, LANES), lambda b, p, it, jt: (it[p], b))
    qrow_spec = pl.BlockSpec((hb, 1, t), lambda b, p, it, jt: (b, 0, it[p]))
    k_spec = pl.BlockSpec((t, LANES), lambda b, p, it, jt: (jt[p], b))
    v_spec = pl.BlockSpec((t, LANES), lambda b, p, it, jt: (jt[p], n_blocks + b))
    kcol_spec = pl.BlockSpec((hb, t, 1), lambda b, p, it, jt: (b, jt[p], 0))
    wide = jax.ShapeDtypeStruct((s_dim, da), F32)
    return pl.pallas_call(
        body, name=name,
        grid_spec=pltpu.PrefetchScalarGridSpec(
            num_scalar_prefetch=2, grid=(n_blocks, len(pairs)),
            in_specs=[q_spec, k_spec, v_spec, qrow_spec, kcol_spec, qrow_spec, qrow_spec, q_spec],
            out_specs=[pl.BlockSpec((s_dim, LANES), lambda b, p, it, jt: (0, b)),
                       pl.BlockSpec((hb, 1, s_dim), lambda b, p, it, jt: (b, 0, 0)), k_spec, k_spec, kcol_spec],
            scratch_shapes=[pltpu.VMEM((t, LANES), F32), pltpu.VMEM((t, LANES), F32), pltpu.VMEM((hb, t, 1), F32)]),
        out_shape=[wide, jax.ShapeDtypeStruct((n_h, 1, s_dim), F32), wide, wide,
                   jax.ShapeDtypeStruct((n_h, s_dim, 1), F32)],
        compiler_params=_cparams(("parallel", "arbitrary"),
                                 10 * hb * _nbytes((t, t), F32) + 4 * _nbytes((s_dim, LANES), F32)),
    )(i_tab, j_tab, q, kv, kv, c_row, c_col, lse, delta, do)


_HBM = pl.BlockSpec(memory_space=pltpu.HBM)
_MESH_ID = pl.DeviceIdType.MESH


def _all_gather(block, name):
    r, w = block.shape

    def body(x_ref, out_ref, send_sems, recv_sems, local_sem):
        x, y, c = lax.axis_index("x"), lax.axis_index("y"), lax.axis_index("c")
        me, sibling = (x, y, c), (x, y, 1 - c)
        chips = [(1 - x, y), (x, 1 - y), (1 - x, 1 - y)]

        def slot(px, py, pc):
            return out_ref.at[4 * px + 2 * py + pc]

        def copy(k, blk, to, src=None):
            return pltpu.make_async_remote_copy(
                src_ref=slot(*blk) if src is None else src, dst_ref=slot(*blk),
                send_sem=send_sems.at[k], recv_sem=recv_sems.at[k], device_id=to, device_id_type=_MESH_ID)

        mine = pltpu.make_async_copy(x_ref, slot(*me), local_sem)
        mine.start()
        first = [copy(0, me, sibling, src=x_ref)]
        first += [copy(1 + n, me, (*chip, c), src=x_ref) for n, chip in enumerate(chips)]
        for cp in first:
            cp.start()
        passed = [copy(4 + n, (*chip, c), sibling) for n, chip in enumerate(chips)]
        for n, chip in enumerate(chips):
            copy(1 + n, (*chip, c), me).wait_recv()
            passed[n].start()
        copy(0, sibling, me).wait_recv()
        for n, chip in enumerate(chips):
            copy(4 + n, (*chip, 1 - c), me).wait_recv()
        for cp in first + passed:
            cp.wait_send()
        mine.wait()

    return pl.pallas_call(
        body, name=name, out_shape=jax.ShapeDtypeStruct((N_DEV, r, w), block.dtype),
        in_specs=[_HBM], out_specs=_HBM,
        scratch_shapes=[pltpu.SemaphoreType.DMA((7,)), pltpu.SemaphoreType.DMA((7,)), pltpu.SemaphoreType.DMA],
    )(block)


_SEM = pl.BlockSpec(memory_space=pltpu.SEMAPHORE)
_EFFECT = pltpu.SideEffectType.DATAFLOW_SIDE_EFFECTING


def _exchange_start(srcs, personalized, after, name):
    n = len(srcs)
    n_after = len(after)
    lands = [lax.empty((N_DEV,) + s.shape[-2:], s.dtype) for s in srcs]

    def body(*refs):
        src_refs, land_refs = refs[:n], refs[n:2 * n]
        outs = refs[2 * n + n_after:]
        send_sems, recv_sems, token = outs[:n], outs[n:2 * n], outs[-1]
        x, y, c = lax.axis_index("x"), lax.axis_index("y"), lax.axis_index("c")
        mine = 4 * x + 2 * y + c
        for ci in range(n):
            for k in range(1, N_DEV):
                px = 1 - x if k & 4 else x
                py = 1 - y if k & 2 else y
                pc = 1 - c if k & 1 else c
                src = src_refs[ci].at[4 * px + 2 * py + pc] if personalized else src_refs[ci]
                pltpu.make_async_remote_copy(
                    src_ref=src, dst_ref=land_refs[ci].at[mine], send_sem=send_sems[ci], recv_sem=recv_sems[ci],
                    device_id=(px, py, pc), device_id_type=_MESH_ID).start()
        token[...] = jnp.zeros_like(token)

    sem = pltpu.SemaphoreType.DMA(())
    out_shape = ([sem] * (2 * n) + [pltpu.HBM(s.shape, s.dtype) for s in srcs]
                 + [pltpu.HBM(l.shape, l.dtype) for l in lands] + [jax.ShapeDtypeStruct((8, LANES), F32)])
    res = pl.pallas_call(
        body, name=name, out_shape=tuple(out_shape),
        in_specs=[_HBM] * (2 * n) + [_ANY] * n_after,
        out_specs=tuple([_SEM] * (2 * n) + [_HBM] * (2 * n) + [pl.BlockSpec(memory_space=pltpu.VMEM)]),
        input_output_aliases={i: 2 * n + i for i in range(2 * n)},
        compiler_params=pltpu.CompilerParams(has_side_effects=_EFFECT),
    )(*[pltpu.with_memory_space_constraint(s, pltpu.HBM) for s in srcs],
      *[pltpu.with_memory_space_constraint(l, pltpu.HBM) for l in lands], *after)
    handles = [(res[ci], res[n + ci], res[2 * n + ci], res[3 * n + ci]) for ci in range(n)]
    return handles, res[-1]


def _exchange_wait(handle, after, name):
    send_sem, recv_sem, src_thru, land_thru = handle

    def body(src_ref, land_ref, send_ref, recv_ref, after_ref, src_out, land_out):
        seven = land_ref.at[pl.ds(0, N_DEV - 1)]
        copies = pltpu.make_async_remote_copy(
            src_ref=seven, dst_ref=seven, send_sem=send_ref, recv_sem=recv_ref,
            device_id=(lax.axis_index("x"), lax.axis_index("y"), lax.axis_index("c")), device_id_type=_MESH_ID)
        copies.wait_send()
        copies.wait_recv()

    return pl.pallas_call(
        body, name=name,
        out_shape=(pltpu.HBM(src_thru.shape, src_thru.dtype), pltpu.HBM(land_thru.shape, land_thru.dtype)),
        in_specs=(_HBM, _HBM, _SEM, _SEM, _ANY), out_specs=(_HBM, _HBM), input_output_aliases={0: 0, 1: 1},
        compiler_params=pltpu.CompilerParams(has_side_effects=_EFFECT),
    )(src_thru, land_thru, send_sem, recv_sem, after)[1]


def _own_slot(land, own, me):
    return lax.dynamic_update_index_in_dim(land, own, me, axis=0)


def _sum_slots(slots, name):
    n, r, w = slots.shape
    tr = _tile(r, 256, WIRE_ROW_ALIGN)

    def body(s_ref, o_ref):
        acc = s_ref[0].astype(F32)
        for d in range(1, n):
            acc = acc + s_ref[d].astype(F32)
        o_ref[...] = acc

    return pl.pallas_call(
        body, name=name, grid=(r // tr,),
        in_specs=[pl.BlockSpec((n, tr, w), lambda i: (0, i, 0))],
        out_specs=pl.BlockSpec((tr, w), lambda i: (i, 0)),
        out_shape=jax.ShapeDtypeStruct((r, w), F32),
        compiler_params=_cparams(("parallel",), 2 * _nbytes((n, tr, w), slots.dtype) + 4 * _nbytes((tr, w), F32)),
    )(slots)


def _adamw(w, g, m, v, name):
    r, c = w.shape
    tr = _tile(r, 512, 8)

    def body(w_ref, g_ref, m_ref, v_ref, d_ref, mo_ref, vo_ref):
        gv = g_ref[...]
        m_new = ADAM_B1 * m_ref[...] + (1.0 - ADAM_B1) * gv
        v_new = ADAM_B2 * v_ref[...] + (1.0 - ADAM_B2) * (gv * gv)
        m_hat = m_new / (1.0 - ADAM_B1 ** ADAM_STEP)
        v_hat = v_new / (1.0 - ADAM_B2 ** ADAM_STEP)
        d_ref[...] = -ADAM_LR * (m_hat / (jnp.sqrt(v_hat) + ADAM_EPS) + ADAM_WD * w_ref[...])
        mo_ref[...] = m_new
        vo_ref[...] = v_new

    blk = pl.BlockSpec((tr, c), lambda i: (i, 0))
    shp = jax.ShapeDtypeStruct((r, c), F32)
    return pl.pallas_call(
        body, name=name, grid=(r // tr,), in_specs=[blk] * 4, out_specs=[blk] * 3, out_shape=[shp] * 3,
        compiler_params=_cparams(("parallel",), 16 * _nbytes((tr, _round_up(c, LANES)), F32)),
    )(w, g, m, v)


def _pack_rows(parts, width, dtype, row_align):
    rows, spans, off = [], [], 0
    for p in parts:
        flat = p.reshape(-1).astype(dtype)
        n_rows = _round_up(-(-flat.shape[0] // width), row_align)
        flat = jnp.pad(flat, (0, n_rows * width - flat.shape[0]))
        rows.append(flat.reshape(n_rows, width))
        spans.append((off, n_rows))
        off += n_rows
    return jnp.concatenate(rows, axis=0), spans


def _unpack_rows(mat, span, shape):
    off, n_rows = span
    n = 1
    for s in shape:
        n *= s
    return mat[..., off:off + n_rows, :].reshape(mat.shape[:-2] + (-1,))[..., :n].reshape(mat.shape[:-2] + tuple(shape))


def _block_diag(w, size):
    n, b, _ = w.shape
    eye = jnp.eye(n, dtype=w.dtype)
    dense = (w[:, :, None, :] * eye[:, None, :, None]).reshape(n * b, n * b)
    return jnp.pad(dense, ((0, size - n * b), (0, size - n * b)))


def _diag_blocks(dense, n, b):
    return jnp.stack([dense[k * b:(k + 1) * b, k * b:(k + 1) * b] for k in range(n)])


def _pad_rows(a, rows):
    return jnp.pad(a, ((0, rows - a.shape[0]), (0, 0)))


def _pad_cols(a, cols):
    return jnp.pad(a, ((0, 0), (0, cols - a.shape[1])))


def _train_step(a):
    x = a["x"][0]
    target = a["loss_target"][0]
    s_dim, d = x.shape
    n_layers = a["ffn1_pre_g"].shape[0]
    f_shard = a["ffn1_w_gate"].shape[2]
    c_shard = a["rg_conv_b"].shape[1]
    c_dim = c_shard * N_DEV
    cp = _round_up(c_dim, LANES)
    conv_width = a["rg_conv_w"].shape[1]
    n_blocks, lru_block = a["rg_w_a"].shape[1], a["rg_w_a"].shape[2]
    d_attn = a["attn_w_q"].shape[2]
    n_heads = a["b_fgate"].shape[0]
    d_head = d_attn // n_heads
    attn_scale = d_head ** -0.5
    assert conv_width < 8 and n_heads <= LANES and n_layers == 2
    assert d_attn == d
    me = 4 * lax.axis_index("x") + 2 * lax.axis_index("y") + lax.axis_index("c")

    shard = {"rg_w_in": a["rg_w_in"][0].T, "rg_w_out": a["rg_w_out"][0], "w_kv": a["w_kv"].T,
             "attn_w_q": a["attn_w_q"][0], "attn_w_o": a["attn_w_o"][0]}
    for l in range(n_layers):
        for f in ("ffn1", "ffn2"):
            shard[(f, "gate", l)] = a[f + "_w_gate"][l].T
            shard[(f, "up", l)] = a[f + "_w_up"][l].T
            shard[(f, "down", l)] = a[f + "_w_down"][l]

    def ffn_names(f, l):
        return [(f, "gate", l), (f, "up", l), (f, "down", l)]

    def chunk_layout(names):
        spans, off = [], 0
        for nm in names:
            spans.append((nm, off, shard[nm].shape[0]))
            off += _round_up(shard[nm].shape[0], WIRE_ROW_ALIGN)
        return spans, off

    def pack_chunk(names, parts):
        return jnp.concatenate(
            [_pad_rows(parts[nm].astype(WIRE_DTYPE), _round_up(parts[nm].shape[0], WIRE_ROW_ALIGN)) for nm in names], axis=0)

    full = {}

    def unpack_chunk(names, gathered):
        for nm, o, n_rows in chunk_layout(names)[0]:
            full[nm] = gathered[:, o:o + n_rows, :].reshape(N_DEV * n_rows, d)

    fwd_chunks = [ffn_names("ffn1", 0)[:2], ffn_names("ffn1", 0)[2:], ["rg_w_in"], ["rg_w_out"],
                  ffn_names("ffn2", 0) + ["w_kv"], ffn_names("ffn1", 1) + ["attn_w_q", "attn_w_o"], ffn_names("ffn2", 1)]
    fwd_packs = [pack_chunk(names, shard) for names in fwd_chunks]
    unpack_chunk(fwd_chunks[0], _all_gather(fwd_packs[0], "gather_weights_first"))

    small_parts = [a["rg_conv_w"][0], a["rg_conv_b"][0], a["rg_b_a"][0], a["rg_b_x"][0], a["rg_lambda"][0], a["w_fgate"]]
    small_pack, small_spans = _pack_rows(small_parts, d, F32, 8)
    small_all = _all_gather(small_pack, "gather_small")
    fwd_handles, fwd_token = _exchange_start(fwd_packs[1:], False, [full[("ffn1", "up", 0)], small_all],
                                             "gather_weights_start")

    def land_weights(n, after):
        land = _exchange_wait(fwd_handles[n - 1], after, f"gather_weights_wait_{n}")
        unpack_chunk(fwd_chunks[n], _own_slot(land, fwd_packs[n], me))

    sm = [_unpack_rows(small_all, sp, p.shape) for sp, p in zip(small_spans, small_parts)]
    conv_w = jnp.moveaxis(sm[0], 0, 1).reshape(conv_width, c_dim)
    conv_b, b_a, b_x, lam = (v.reshape(1, c_dim) for v in sm[1:5])
    w_f = sm[5].reshape(d, n_heads)

    pconv = _pad_rows(_pad_cols(jnp.concatenate([conv_w, conv_b], axis=0), cp), 8)
    pvec = _pad_rows(_pad_cols(jnp.concatenate([b_a, b_x, lam], axis=0), cp), 8)
    wa_dense = _block_diag(a["rg_w_a"][0], cp).astype(MXU_DTYPE)
    wx_dense = _block_diag(a["rg_w_x"][0], cp).astype(MXU_DTYPE)
    wax = jnp.concatenate([wa_dense, wx_dense], axis=1)
    w_f_t = _pad_rows(w_f.T.astype(MXU_DTYPE), LANES)
    b_f = _pad_rows(_pad_cols(a["b_fgate"].reshape(1, n_heads), LANES), 8)

    def gain(name, l):
        return a[name][l].reshape(1, d)

    def ffn_fwd(h, f, l, after=None, down_chunk=None, xn=None, next_gain=None):
        if xn is None:
            xn = _rms_fwd(h, gain(f + "_pre_g", l), f"{f}_{l}_pre_norm", after)
        g, u, act = _ffn_up(xn, full[(f, "gate", l)], full[(f, "up", l)], f"{f}_{l}_up")
        if down_chunk is not None:
            land_weights(down_chunk, act)
        fo, h_new, *nxt = _mm_rms_res(act, full[(f, "down", l)], h, gain(f + "_post_g", l), 0.5, f"{f}_{l}_down",
                                      next_gain)
        return h_new, (h, xn, g, u, act, fo), (nxt[0] if nxt else None)

    h0 = x
    h0a, sv_f1_0, hn_rg = ffn_fwd(h0, "ffn1", 0, fwd_token, down_chunk=1, next_gain=gain("mix_pre_g", 0))
    land_weights(2, h0a)
    w_in_gate = _pad_rows(full["rg_w_in"][:c_dim], cp)
    w_in_rec = _pad_rows(full["rg_w_in"][c_dim:], cp)
    w_in_t = jnp.concatenate([w_in_gate, w_in_rec], axis=0)
    gx = _mm([(hn_rg, w_in_t)], "nt", F32, "rg_in_proj")
    rec = _conv_fwd(gx, pconv, conv_width, "rg_conv")
    gates = _mm([(rec, wax)], "nn", F32, "rg_gate_proj")
    h_rec, y_rg = _scan_fwd(gx, rec, gates, pvec, "rg_scan")
    land_weights(3, y_rg)
    w_out = _pad_rows(full["rg_w_out"], cp)
    m_rg, h0b, xn_f2 = _mm_rms_res(y_rg, w_out, h0a, gain("mix_post_g", 0), 1.0, "rg_out_proj", gain("ffn2_pre_g", 0))
    land_weights(4, h0b)
    h1, sv_f2_0, hn_kv = ffn_fwd(h0b, "ffn2", 0, xn=xn_f2, next_gain=a["kv_norm_g"].reshape(1, d))
    kv = _mm([(hn_kv, full["w_kv"])], "nt", MXU_DTYPE, "kv_proj")
    fpre = _mm([(hn_kv, w_f_t)], "nt", F32, "fgate_proj")
    c_cum = _fgate_fwd(fpre, b_f, "fgate_cumsum")
    c_heads = c_cum[:, :n_heads].T
    c_col, c_row = c_heads[:, :, None], c_heads[:, None, :]
    land_weights(5, c_cum)
    h1a, sv_f1_1, hn_at = ffn_fwd(h1, "ffn1", 1, next_gain=gain("mix_pre_g", 1))
    q_s = _mm([(hn_at, full["attn_w_q"])], "nn", MXU_DTYPE, "q_proj", out_scale=attn_scale)
    o2, lse = _pair_attn_fwd(q_s, kv, kv[:, d_attn:].T, c_col, c_row, "attn_fwd")
    m_at, h1b, xn_f2 = _mm_rms_res(o2, full["attn_w_o"], h1a, gain("mix_post_g", 1), 1.0, "attn_out_proj",
                                   gain("ffn2_pre_g", 1))
    land_weights(6, h1b)
    y, sv_f2_1, _ = ffn_fwd(h1b, "ffn2", 1, xn=xn_f2)
    dy, loss_part = _loss_head(y, target, "loss_head")

    grads_big = {}
    grads_rep = {}

    bwd_chunks = [ffn_names("ffn2", 1), ["attn_w_q", "attn_w_o"] + ffn_names("ffn1", 1),
                  ["w_kv"] + ffn_names("ffn2", 0), ["rg_w_in", "rg_w_out"], ffn_names("ffn1", 0)[2:],
                  ffn_names("ffn1", 0)[:2]]
    bwd_sends, bwd_handles = [], []

    def send_grads(after):
        n = len(bwd_sends)
        send = jnp.concatenate(
            [jnp.pad(grads_big[nm].reshape(N_DEV, n_rows, d), ((0, 0), (0, _round_up(n_rows, WIRE_ROW_ALIGN) - n_rows), (0, 0)))
             for nm, _, n_rows in chunk_layout(bwd_chunks[n])[0]], axis=1)
        handles, token = _exchange_start([send], True, [after], f"exchange_grads_start_{n}")
        bwd_sends.append(send)
        bwd_handles.append(handles[0])
        return token

    def ffn_bwd(dh_out, saved, f, l, after=None, send_now=False):
        h, xn, g, u, act, fo = saved
        df, d_post = _rms_bwd(fo, gain(f + "_post_g", l), [dh_out], None, 0.5, MXU_DTYPE, f"{f}_{l}_post_norm_bwd", after)
        dg, du = _ffn_act_bwd(df, full[(f, "down", l)], g, u, f"{f}_{l}_act_bwd")
        grads_big[(f, "down", l)] = _mm([(act, df)], "tn", WIRE_DTYPE, f"{f}_{l}_dw_down")
        sent = send_grads(df) if send_now else None
        grads_big[(f, "gate", l)] = _mm([(dg, xn)], "tn", WIRE_DTYPE, f"{f}_{l}_dw_gate", sent)
        grads_big[(f, "up", l)] = _mm([(du, xn)], "tn", WIRE_DTYPE, f"{f}_{l}_dw_up")
        sent = send_grads(df) if send_now else None
        dxn = _mm([(dg, full[(f, "gate", l)]), (du, full[(f, "up", l)])], "nn", F32, f"{f}_{l}_dx", sent)
        dh_in, d_pre = _rms_bwd(h, gain(f + "_pre_g", l), [dxn], dh_out, 1.0, F32, f"{f}_{l}_pre_norm_bwd")
        grads_rep[(f + "_post_g", l)] = d_post
        grads_rep[(f + "_pre_g", l)] = d_pre
        return dh_in

    dh = ffn_bwd(dy, sv_f2_1, "ffn2", 1)
    token = send_grads(dh)
    dm, d_post = _rms_bwd(m_at, gain("mix_post_g", 1), [dh], None, 1.0, MXU_DTYPE, "attn_post_norm_bwd", token)
    grads_rep[("mix_post_g", 1)] = d_post
    do2 = _mm([(dm, full["attn_w_o"])], "nt", F32, "attn_out_proj_dx")
    grads_big["attn_w_o"] = _mm([(o2, dm)], "tn", WIRE_DTYPE, "attn_out_proj_dw")
    delta = _attn_delta(do2, o2, n_heads, "attn_delta")
    dq2, dc_q, dk2, dv2, dc_k = _pair_attn_bwd(q_s, kv, c_col, c_row, lse, delta, do2, attn_scale, "attn_bwd")
    dc_heads = dc_q[:, 0, :] + dc_k[:, :, 0]
    dhn = _mm([(dq2, full["attn_w_q"])], "nt", F32, "q_proj_dx")
    grads_big["attn_w_q"] = _mm([(hn_at, dq2)], "tn", WIRE_DTYPE, "q_proj_dw")
    dh, d_pre = _rms_bwd(h1a, gain("mix_pre_g", 1), [dhn], dh, 1.0, F32, "attn_pre_norm_bwd")
    grads_rep[("mix_pre_g", 1)] = d_pre
    dh = ffn_bwd(dh, sv_f1_1, "ffn1", 1)
    token = send_grads(dh)
    dc_cum = _pad_cols(dc_heads.T, LANES)
    dfpre, db_f = _fgate_bwd(dc_cum, fpre, b_f, "fgate_cumsum_bwd")
    dhn_kv = _mm([(dk2, full["w_kv"][:d_attn]), (dv2, full["w_kv"][d_attn:])], "nn", F32, "kv_proj_dx")
    dhn_f = _mm([(dfpre, w_f_t)], "nn", F32, "fgate_proj_dx")
    grads_big["w_kv"] = jnp.concatenate([_mm([(dk2, hn_kv)], "tn", WIRE_DTYPE, "kv_proj_dw_k"),
                                         _mm([(dv2, hn_kv)], "tn", WIRE_DTYPE, "kv_proj_dw_v")], axis=0)
    dw_f_t = _mm([(dfpre, hn_kv)], "tn", F32, "fgate_proj_dw")
    dh, d_kvg = _rms_bwd(h1, a["kv_norm_g"].reshape(1, d), [dhn_kv, dhn_f], dh, 1.0, F32, "kv_norm_bwd", token)
    dh = ffn_bwd(dh, sv_f2_0, "ffn2", 0)
    token = send_grads(dh)
    dm, d_post = _rms_bwd(m_rg, gain("mix_post_g", 0), [dh], None, 1.0, MXU_DTYPE, "rg_post_norm_bwd", token)
    grads_rep[("mix_post_g", 0)] = d_post
    dy_rg = _mm([(dm, w_out)], "nt", F32, "rg_out_proj_dx")
    dw_out = _mm([(y_rg, dm)], "tn", WIRE_DTYPE, "rg_out_proj_dw")
    dgate, dra, dia, drec1, dpvec = _scan_bwd(dy_rg, gx, h_rec, rec, gates, pvec, "rg_scan_bwd")
    drec2 = _mm([(dra, wa_dense), (dia, wx_dense)], "nt", F32, "rg_gate_proj_dx")
    dwa_dense = _mm([(rec, dra)], "tn", F32, "rg_gate_proj_dwa")
    dwx_dense = _mm([(rec, dia)], "tn", F32, "rg_gate_proj_dwx")
    drec0, dpconv = _conv_bwd(drec1, drec2, gx, pconv, conv_width, "rg_conv_bwd")
    dhn = _mm([(dgate, w_in_gate), (drec0, w_in_rec)], "nn", F32, "rg_in_proj_dx")
    dw_in_gate = _mm([(dgate, hn_rg)], "tn", WIRE_DTYPE, "rg_in_proj_dw_gate")
    dw_in_rec = _mm([(drec0, hn_rg)], "tn", WIRE_DTYPE, "rg_in_proj_dw_rec")
    dh, d_pre = _rms_bwd(h0a, gain("mix_pre_g", 0), [dhn], dh, 1.0, F32, "rg_pre_norm_bwd")
    grads_rep[("mix_pre_g", 0)] = d_pre
    grads_big["rg_w_in"] = jnp.concatenate([dw_in_gate[:c_dim], dw_in_rec[:c_dim]], axis=0)
    grads_big["rg_w_out"] = dw_out[:c_dim]
    token = send_grads(dh)
    grad_x = ffn_bwd(dh, sv_f1_0, "ffn1", 0, token, send_now=True)

    g_shard = {}

    def land_grads(n, after):
        land = _exchange_wait(bwd_handles[n], after, f"exchange_grads_wait_{n}")
        own = lax.dynamic_index_in_dim(bwd_sends[n], me, axis=0, keepdims=False)
        g_chunk = _sum_slots(_own_slot(land, own, me), f"sum_weight_grads_{n}")
        for nm, o, n_rows in chunk_layout(bwd_chunks[n])[0]:
            g_shard[nm] = g_chunk[o:o + n_rows]

    for n in range(len(bwd_chunks) - 2):
        land_grads(n, grad_x)

    def gain_grad(name):
        return jnp.concatenate([grads_rep[(name, l)] for l in range(n_layers)], axis=0)

    rep_names = ["ffn1_pre_g", "ffn1_post_g", "mix_pre_g", "mix_post_g", "ffn2_pre_g", "ffn2_post_g"]
    rep_parts = [gain_grad(nm) for nm in rep_names]
    rep_names += ["kv_norm_g", "b_fgate", "rg_w_a", "rg_w_x", "rg_conv_w", "rg_conv_b", "rg_b_a", "rg_b_x", "rg_lambda", "w_fgate"]
    rep_parts += [
        d_kvg, db_f[0, :n_heads],
        _diag_blocks(dwa_dense, n_blocks, lru_block), _diag_blocks(dwx_dense, n_blocks, lru_block),
        dpconv[:conv_width, :c_dim], dpconv[conv_width, :c_dim],
        dpvec[0, :c_dim], dpvec[1, :c_dim], dpvec[2, :c_dim],
        dw_f_t[:n_heads].T]
    rep_pack, rep_spans = _pack_rows(rep_parts, d, F32, 8)
    rep_sum = _sum_slots(_all_gather(rep_pack, "gather_small_grads"), "sum_small_grads")
    g_rep = {nm: _unpack_rows(rep_sum, sp, p.shape) for nm, sp, p in zip(rep_names, rep_spans, rep_parts)}

    def my_cols(full_grad, n):
        return lax.dynamic_slice_in_dim(full_grad, me * n, n, axis=full_grad.ndim - 1)

    def ffn_grads(f):
        grad[f + "_w_gate"] = jnp.stack([g_shard[(f, "gate", l)].T for l in range(n_layers)])
        grad[f + "_w_up"] = jnp.stack([g_shard[(f, "up", l)].T for l in range(n_layers)])
        grad[f + "_w_down"] = jnp.stack([g_shard[(f, "down", l)] for l in range(n_layers)])

    grad = {}
    for nm in ("ffn1_pre_g", "ffn1_post_g", "mix_pre_g", "mix_post_g", "ffn2_pre_g", "ffn2_post_g"):
        grad[nm] = g_rep[nm]
    ffn_grads("ffn2")
    grad["rg_w_in"] = g_shard["rg_w_in"].T[None]
    grad["rg_conv_w"] = my_cols(g_rep["rg_conv_w"], c_shard)[None]
    for nm in ("rg_conv_b", "rg_b_a", "rg_b_x", "rg_lambda"):
        grad[nm] = my_cols(g_rep[nm], c_shard)[None]
    grad["rg_w_a"] = g_rep["rg_w_a"][None]
    grad["rg_w_x"] = g_rep["rg_w_x"][None]
    grad["rg_w_out"] = g_shard["rg_w_out"][None]
    grad["kv_norm_g"] = g_rep["kv_norm_g"].reshape(d)
    grad["w_kv"] = g_shard["w_kv"].T
    grad["w_fgate"] = lax.dynamic_slice_in_dim(g_rep["w_fgate"], me * (d // N_DEV), d // N_DEV, axis=0)
    grad["b_fgate"] = g_rep["b_fgate"]
    grad["attn_w_q"] = g_shard["attn_w_q"][None]
    grad["attn_w_o"] = g_shard["attn_w_o"][None]

    delta, new_m, new_v = {}, {}, {}

    def adamw(nm):
        w = a[nm]
        shape = w.shape
        two_d = (1, shape[0]) if w.ndim == 1 else (-1, shape[-1])
        ops = [pltpu.with_memory_space_constraint(t.reshape(two_d), pltpu.HBM)
               for t in (w, grad[nm], a["m_" + nm], a["v_" + nm])]
        dl, mo, vo = _adamw(*ops, "adamw_" + nm)
        delta[nm], new_m[nm], new_v[nm] = dl.reshape(shape), mo.reshape(shape), vo.reshape(shape)
        grad[nm] = grad[nm].reshape(shape)

    last_names = ("ffn1_w_gate", "ffn1_w_up", "ffn1_w_down")
    for nm in WEIGHT_NAMES:
        if nm not in last_names:
            adamw(nm)
    land_grads(len(bwd_chunks) - 2, delta["attn_w_o"])
    land_grads(len(bwd_chunks) - 1, delta["attn_w_o"])
    ffn_grads("ffn1")
    for nm in last_names:
        adamw(nm)

    loss = lax.psum(loss_part[0, 0], AXES)
    return (loss, grad_x[None], *[grad[n] for n in WEIGHT_NAMES], *[delta[n] for n in WEIGHT_NAMES],
            *[new_m[n] for n in WEIGHT_NAMES], *[new_v[n] for n in WEIGHT_NAMES])


def kernel(x, ffn1_pre_g, ffn1_w_gate, ffn1_w_up, ffn1_w_down, ffn1_post_g, mix_pre_g, mix_post_g, ffn2_pre_g, ffn2_w_gate, ffn2_w_up, ffn2_w_down, ffn2_post_g, rg_w_in, rg_conv_w, rg_conv_b, rg_w_a, rg_b_a, rg_w_x, rg_b_x, rg_lambda, rg_w_out, kv_norm_g, w_kv, w_fgate, b_fgate, attn_w_q, attn_w_o, loss_target, m_ffn1_pre_g, m_ffn1_w_gate, m_ffn1_w_up, m_ffn1_w_down, m_ffn1_post_g, m_mix_pre_g, m_mix_post_g, m_ffn2_pre_g, m_ffn2_w_gate, m_ffn2_w_up, m_ffn2_w_down, m_ffn2_post_g, m_rg_w_in, m_rg_conv_w, m_rg_conv_b, m_rg_w_a, m_rg_b_a, m_rg_w_x, m_rg_b_x, m_rg_lambda, m_rg_w_out, m_kv_norm_g, m_w_kv, m_w_fgate, m_b_fgate, m_attn_w_q, m_attn_w_o, v_ffn1_pre_g, v_ffn1_w_gate, v_ffn1_w_up, v_ffn1_w_down, v_ffn1_post_g, v_mix_pre_g, v_mix_post_g, v_ffn2_pre_g, v_ffn2_w_gate, v_ffn2_w_up, v_ffn2_w_down, v_ffn2_post_g, v_rg_w_in, v_rg_conv_w, v_rg_conv_b, v_rg_w_a, v_rg_b_a, v_rg_w_x, v_rg_b_x, v_rg_lambda, v_rg_w_out, v_kv_norm_g, v_w_kv, v_w_fgate, v_b_fgate, v_attn_w_q, v_attn_w_o):
    return _train_step(dict(locals()))
```

```python
import functools

import jax
import jax.numpy as jnp
from jax import lax
from jax.experimental import pallas as pl
from jax.experimental.pallas import tpu as pltpu

F32 = jnp.float32
MXU_DTYPE = jnp.bfloat16
WIRE_DTYPE = jnp.bfloat16
N_DEV = 8
AXES = ("x", "y", "c")
LANES = 128
WIRE_ROW_ALIGN = 16
VMEM_LIMIT_MIN = 32 * 2 ** 20
VMEM_LIMIT_MAX = 56 * 2 ** 20

RMS_EPS = 1e-6
LRU_C = 8.0
ADAM_LR, ADAM_B1, ADAM_B2, ADAM_EPS, ADAM_WD, ADAM_STEP = 0.001, 0.9, 0.999, 1e-08, 0.01, 10

WEIGHT_NAMES = (
    "ffn1_pre_g", "ffn1_w_gate", "ffn1_w_up", "ffn1_w_down", "ffn1_post_g", "mix_pre_g", "mix_post_g",
    "ffn2_pre_g", "ffn2_w_gate", "ffn2_w_up", "ffn2_w_down", "ffn2_post_g", "rg_w_in", "rg_conv_w",
    "rg_conv_b", "rg_w_a", "rg_b_a", "rg_w_x", "rg_b_x", "rg_lambda", "rg_w_out", "kv_norm_g", "w_kv",
    "w_fgate", "b_fgate", "attn_w_q", "attn_w_o")


def _round_up(n, m):
    return (n + m - 1) // m * m


def _tile(dim, target, align=LANES):
    if dim <= target:
        return dim
    best = None
    t = align
    while t <= target:
        if dim % t == 0:
            best = t
        t += align
    return dim if best is None else best


def _cparams(semantics, vmem_estimate):
    limit = min(VMEM_LIMIT_MAX, max(VMEM_LIMIT_MIN, 2 * int(vmem_estimate)))
    return pltpu.CompilerParams(dimension_semantics=semantics, vmem_limit_bytes=limit)


def _nbytes(shape, dtype):
    n = 1
    for s in shape:
        n *= s
    return n * jnp.dtype(dtype).itemsize


def _sigmoid(x):
    return jax.nn.sigmoid(x)


def _softplus(x):
    return jnp.maximum(x, 0.0) + jnp.log1p(jnp.exp(-jnp.abs(x)))


def _expm1(x):
    series = x * (1.0 + x * (0.5 + x * (1.0 / 6.0 + x * (1.0 / 24.0 + x * (1.0 / 120.0)))))
    return jnp.where(jnp.abs(x) < 0.25, series, jnp.exp(x) - 1.0)


_GELU_C = 0.7978845608028654
_GELU_A = 0.044715


def _gelu(x):
    return 0.5 * x * (1.0 + jnp.tanh(_GELU_C * (x + _GELU_A * x * x * x)))


def _gelu_grad(x):
    t = jnp.tanh(_GELU_C * (x + _GELU_A * x * x * x))
    return 0.5 * (1.0 + t) + 0.5 * x * (1.0 - t * t) * _GELU_C * (1.0 + 3.0 * _GELU_A * x * x)


_DOT_DIMS = {"nn": ((1,), (0,)), "nt": ((1,), (1,)), "tn": ((0,), (0,))}


def _dot(a, b, mode):
    return lax.dot_general(a.astype(MXU_DTYPE), b.astype(MXU_DTYPE), (_DOT_DIMS[mode], ((), ())),
                           preferred_element_type=F32)


def _mm(pairs, mode, out_dtype, name, after=None, out_scale=None):
    a0, b0 = pairs[0]
    if mode == "tn":
        k_dim, m_dim = a0.shape
        n_dim = b0.shape[1]
    else:
        m_dim, k_dim = a0.shape
        n_dim = b0.shape[0] if mode == "nt" else b0.shape[1]
    for a, b in pairs:
        assert a.shape == a0.shape and b.shape == b0.shape
    tm = _tile(m_dim, 1408 if mode == "tn" else 512)
    whole = 1408 if mode == "tn" else 2816
    tn = _tile(n_dim, whole)
    tk = _tile(k_dim, whole)
    if mode == "tn" and jnp.dtype(a0.dtype).itemsize == 2 and jnp.dtype(b0.dtype).itemsize == 2:
        tk = _tile(k_dim, 2048)
    nk = k_dim // tk
    n_pairs = len(pairs)

    if mode == "tn":
        a_spec = pl.BlockSpec((tk, tm), lambda i, j, k: (k, i))
    else:
        a_spec = pl.BlockSpec((tm, tk), lambda i, j, k: (i, k))
    if mode == "nt":
        b_spec = pl.BlockSpec((tn, tk), lambda i, j, k: (j, k))
    else:
        b_spec = pl.BlockSpec((tk, tn), lambda i, j, k: (k, j))

    order = [] if after is None else [after]

    def body(*refs):
        ins, o_ref, acc = refs[:2 * n_pairs], refs[-2], refs[-1]
        k = pl.program_id(2)

        @pl.when(k == 0)
        def _():
            acc[...] = jnp.zeros_like(acc)

        s = acc[...]
        for p in range(n_pairs):
            s = s + _dot(ins[2 * p][...], ins[2 * p + 1][...], mode)
        acc[...] = s

        @pl.when(k == nk - 1)
        def _():
            r = acc[...] if out_scale is None else acc[...] * out_scale
            o_ref[...] = r.astype(out_dtype)

    est = (2 * n_pairs * (_nbytes((tm, tk), a0.dtype) + _nbytes((tk, tn), b0.dtype))
           + 2 * _nbytes((tm, tn), out_dtype) + 2 * _nbytes((tm, tn), F32))
    flat = [t for ab in pairs for t in ab]
    return pl.pallas_call(
        body, name=name, grid=(m_dim // tm, n_dim // tn, nk),
        in_specs=[a_spec, b_spec] * n_pairs + [_ANY] * len(order),
        out_specs=pl.BlockSpec((tm, tn), lambda i, j, k: (i, j)),
        out_shape=jax.ShapeDtypeStruct((m_dim, n_dim), out_dtype),
        scratch_shapes=[pltpu.VMEM((tm, tn), F32)],
        compiler_params=_cparams(("parallel", "parallel", "arbitrary"), est),
    )(*flat, *order)


_ANY = pl.BlockSpec(memory_space=pl.ANY)


def _rms_fwd(x, gain, name, after=None):
    s_dim, d = x.shape
    tm = _tile(s_dim, 512, 8)

    def body(*refs):
        x_ref, g_ref, o_ref = refs[0], refs[1], refs[-1]
        v = x_ref[...]
        r = lax.rsqrt(jnp.mean(v * v, axis=-1, keepdims=True) + RMS_EPS)
        o_ref[...] = (v * r * g_ref[...]).astype(MXU_DTYPE)

    order = [] if after is None else [after]
    return pl.pallas_call(
        body, name=name, grid=(s_dim // tm,),
        in_specs=[pl.BlockSpec((tm, d), lambda i: (i, 0)), pl.BlockSpec((1, d), lambda i: (0, 0))] + [_ANY] * len(order),
        out_specs=pl.BlockSpec((tm, d), lambda i: (i, 0)),
        out_shape=jax.ShapeDtypeStruct((s_dim, d), MXU_DTYPE),
        compiler_params=_cparams(("parallel",), 6 * _nbytes((tm, d), F32)),
    )(x, gain, *order)


def _rms_bwd(x, gain, dys, res, scale, out_dtype, name, after=None):
    s_dim, d = x.shape
    tm = _tile(s_dim, 512, 8)
    n_dy = len(dys)
    has_res = res is not None
    order = [] if after is None else [after]

    def body(*refs):
        x_ref, g_ref = refs[0], refs[1]
        dy_refs = refs[2:2 + n_dy]
        res_ref = refs[2 + n_dy] if has_res else None
        dx_ref, dg_ref = refs[-2], refs[-1]

        @pl.when(pl.program_id(0) == 0)
        def _():
            dg_ref[...] = jnp.zeros_like(dg_ref)

        v = x_ref[...]
        r = lax.rsqrt(jnp.mean(v * v, axis=-1, keepdims=True) + RMS_EPS)
        xh = v * r
        dy = dy_refs[0][...].astype(F32)
        for extra in dy_refs[1:]:
            dy = dy + extra[...].astype(F32)
        gd = dy * g_ref[...]
        dx = scale * r * (gd - xh * jnp.mean(gd * xh, axis=-1, keepdims=True))
        if has_res:
            dx = dx + res_ref[...]
        dx_ref[...] = dx.astype(out_dtype)
        dg_ref[...] += scale * jnp.sum(dy * xh, axis=0, keepdims=True)

    row = pl.BlockSpec((tm, d), lambda i: (i, 0))
    vec = pl.BlockSpec((1, d), lambda i: (0, 0))
    ops = [x, gain] + list(dys) + ([res] if has_res else [])
    return pl.pallas_call(
        body, name=name, grid=(s_dim // tm,),
        in_specs=[row, vec] + [row] * (n_dy + int(has_res)) + [_ANY] * len(order),
        out_specs=[row, vec],
        out_shape=[jax.ShapeDtypeStruct((s_dim, d), out_dtype), jax.ShapeDtypeStruct((1, d), F32)],
        compiler_params=_cparams(("arbitrary",), (2 * len(ops) + 6) * _nbytes((tm, d), F32)),
    )(*ops, *order)


def _mm_rms_res(a, b, h, gain, scale, name, next_gain=None):
    s_dim, k_dim = a.shape
    d = b.shape[1]
    tm = _tile(s_dim, 512, 8)
    tk = _tile(k_dim, 2816)
    nk = k_dim // tk

    with_next = next_gain is not None

    def body(*refs):
        a_ref, b_ref, h_ref, g_ref = refs[:4]
        g2_ref = refs[4] if with_next else None
        f_ref, o_ref = refs[4 + int(with_next)], refs[5 + int(with_next)]
        xn_ref = refs[6 + int(with_next)] if with_next else None
        acc = refs[-1]
        k = pl.program_id(1)

        @pl.when(k == 0)
        def _():
            acc[...] = jnp.zeros_like(acc)

        acc[...] += _dot(a_ref[...], b_ref[...], "nn")

        @pl.when(k == nk - 1)
        def _():
            f = acc[...]
            r = lax.rsqrt(jnp.mean(f * f, axis=-1, keepdims=True) + RMS_EPS)
            f_ref[...] = f
            o = h_ref[...] + scale * (f * r * g_ref[...])
            o_ref[...] = o
            if with_next:
                r2 = lax.rsqrt(jnp.mean(o * o, axis=-1, keepdims=True) + RMS_EPS)
                xn_ref[...] = (o * r2 * g2_ref[...]).astype(MXU_DTYPE)

    row = pl.BlockSpec((tm, d), lambda i, k: (i, 0))
    vec = pl.BlockSpec((1, d), lambda i, k: (0, 0))
    est = (2 * (_nbytes((tm, tk), a.dtype) + _nbytes((tk, d), b.dtype)) + 10 * _nbytes((tm, d), F32))
    wide = jax.ShapeDtypeStruct((s_dim, d), F32)
    return pl.pallas_call(
        body, name=name, grid=(s_dim // tm, nk),
        in_specs=[pl.BlockSpec((tm, tk), lambda i, k: (i, k)), pl.BlockSpec((tk, d), lambda i, k: (k, 0)),
                  row, vec] + [vec] * int(with_next),
        out_specs=[row, row] + [row] * int(with_next),
        out_shape=[wide, wide] + [jax.ShapeDtypeStruct((s_dim, d), MXU_DTYPE)] * int(with_next),
        scratch_shapes=[pltpu.VMEM((tm, d), F32)],
        compiler_params=_cparams(("parallel", "arbitrary"), est),
    )(a, b, h, gain, *([next_gain] if with_next else []))


def _ffn_up(xn, wg_t, wu_t, name):
    s_dim, d = xn.shape
    f_dim = wg_t.shape[0]
    tm = _tile(s_dim, 2048, 8)
    tf = _tile(f_dim, 256)

    def body(x_ref, wg_ref, wu_ref, g_ref, u_ref, a_ref):
        x = x_ref[...]
        g = _dot(x, wg_ref[...], "nt")
        u = _dot(x, wu_ref[...], "nt")
        g_ref[...] = g.astype(MXU_DTYPE)
        u_ref[...] = u.astype(MXU_DTYPE)
        a_ref[...] = (g * _sigmoid(g) * u).astype(MXU_DTYPE)

    w_spec = pl.BlockSpec((tf, d), lambda i, j: (j, 0))
    o_spec = pl.BlockSpec((tm, tf), lambda i, j: (i, j))
    o_shape = jax.ShapeDtypeStruct((s_dim, f_dim), MXU_DTYPE)
    est = 2 * _nbytes((tm, d), xn.dtype) + 4 * _nbytes((tf, d), wg_t.dtype) + 10 * _nbytes((tm, tf), F32)
    return pl.pallas_call(
        body, name=name, grid=(s_dim // tm, f_dim // tf),
        in_specs=[pl.BlockSpec((tm, d), lambda i, j: (i, 0)), w_spec, w_spec],
        out_specs=[o_spec, o_spec, o_spec], out_shape=[o_shape, o_shape, o_shape],
        compiler_params=_cparams(("parallel", "parallel"), est),
    )(xn, wg_t, wu_t)


def _ffn_act_bwd(df, wd, g, u, name):
    s_dim, d = df.shape
    f_dim = wd.shape[0]
    tm = _tile(s_dim, 2048, 8)
    tf = _tile(f_dim, 256)

    def body(df_ref, wd_ref, g_ref, u_ref, dg_ref, du_ref):
        dh = _dot(df_ref[...], wd_ref[...], "nt")
        gv = g_ref[...].astype(F32)
        uv = u_ref[...].astype(F32)
        sg = _sigmoid(gv)
        dg_ref[...] = (dh * uv * (sg * (1.0 + gv * (1.0 - sg)))).astype(MXU_DTYPE)
        du_ref[...] = (dh * gv * sg).astype(MXU_DTYPE)

    t_spec = pl.BlockSpec((tm, tf), lambda i, j: (i, j))
    o_shape = jax.ShapeDtypeStruct((s_dim, f_dim), MXU_DTYPE)
    est = 2 * _nbytes((tm, d), df.dtype) + 2 * _nbytes((tf, d), wd.dtype) + 12 * _nbytes((tm, tf), F32)
    return pl.pallas_call(
        body, name=name, grid=(s_dim // tm, f_dim // tf),
        in_specs=[pl.BlockSpec((tm, d), lambda i, j: (i, 0)), pl.BlockSpec((tf, d), lambda i, j: (j, 0)),
                  t_spec, t_spec],
        out_specs=[t_spec, t_spec], out_shape=[o_shape, o_shape],
        compiler_params=_cparams(("parallel", "parallel"), est),
    )(df, wd, g, u)


def _loss_head(y, target, name):
    s_dim, d = y.shape
    tm = _tile(s_dim, 512, 8)
    nt = s_dim // tm

    def body(y_ref, t_ref, dy_ref, loss_ref, acc):
        i = pl.program_id(0)

        @pl.when(i == 0)
        def _():
            acc[...] = jnp.zeros_like(acc)

        e = y_ref[...] - t_ref[...]
        dy_ref[...] = e * (1.0 / d)
        acc[...] += jnp.sum(e * e, axis=0, keepdims=True)

        @pl.when(i == nt - 1)
        def _():
            loss_ref[...] = jnp.sum(acc[...], axis=1, keepdims=True) * (0.5 / d)

    row = pl.BlockSpec((tm, d), lambda i: (i, 0))
    return pl.pallas_call(
        body, name=name, grid=(nt,), in_specs=[row, row],
        out_specs=[row, pl.BlockSpec((1, 1), lambda i: (0, 0))],
        out_shape=[jax.ShapeDtypeStruct((s_dim, d), F32), jax.ShapeDtypeStruct((1, 1), F32)],
        scratch_shapes=[pltpu.VMEM((1, d), F32)],
        compiler_params=_cparams(("arbitrary",), 8 * _nbytes((tm, d), F32)),
    )(y, target)


def _shift_down(v, sh, row):
    if sh == 0:
        return v
    return jnp.where(row >= sh, pltpu.roll(v, sh, 0), 0.0)


def _shift_up(v, sh, row):
    if sh == 0:
        return v
    n = v.shape[0]
    return jnp.where(row < n - sh, pltpu.roll(v, n - sh, 0), 0.0)


def _conv_fwd(gx, pconv, width, name):
    s_dim, cp2 = gx.shape
    cp = cp2 // 2
    nc = cp // LANES

    def body(x_ref, p_ref, o_ref):
        x = x_ref[...]
        row = lax.broadcasted_iota(jnp.int32, x.shape, 0)
        y = jnp.zeros_like(x) + p_ref[pl.ds(width, 1), :]
        for k in range(width):
            y = y + p_ref[pl.ds(k, 1), :] * _shift_down(x, width - 1 - k, row)
        o_ref[...] = y

    return pl.pallas_call(
        body, name=name, grid=(nc,),
        in_specs=[pl.BlockSpec((s_dim, LANES), lambda j: (0, nc + j)), pl.BlockSpec((8, LANES), lambda j: (0, j))],
        out_specs=pl.BlockSpec((s_dim, LANES), lambda j: (0, j)),
        out_shape=jax.ShapeDtypeStruct((s_dim, cp), F32),
        compiler_params=_cparams(("parallel",), 10 * _nbytes((s_dim, LANES), F32)),
    )(gx, pconv)


def _conv_bwd(d1, d2, gx, pconv, width, name):
    s_dim, cp = d1.shape
    nc = cp // LANES

    def body(d1_ref, d2_ref, x_ref, p_ref, dx_ref, dp_ref):
        d = d1_ref[...] + d2_ref[...]
        x = x_ref[...]
        row = lax.broadcasted_iota(jnp.int32, x.shape, 0)
        dx = jnp.zeros_like(d)
        dp_ref[...] = jnp.zeros_like(dp_ref)
        for k in range(width):
            sh = width - 1 - k
            dx = dx + p_ref[pl.ds(k, 1), :] * _shift_up(d, sh, row)
            dp_ref[pl.ds(k, 1), :] = jnp.sum(d * _shift_down(x, sh, row), axis=0, keepdims=True)
        dp_ref[pl.ds(width, 1), :] = jnp.sum(d, axis=0, keepdims=True)
        dx_ref[...] = dx.astype(MXU_DTYPE)

    strip = pl.BlockSpec((s_dim, LANES), lambda j: (0, j))
    par = pl.BlockSpec((8, LANES), lambda j: (0, j))
    return pl.pallas_call(
        body, name=name, grid=(nc,),
        in_specs=[strip, strip, pl.BlockSpec((s_dim, LANES), lambda j: (0, nc + j)), par],
        out_specs=[strip, par],
        out_shape=[jax.ShapeDtypeStruct((s_dim, cp), MXU_DTYPE), jax.ShapeDtypeStruct((8, cp), F32)],
        compiler_params=_cparams(("parallel",), 14 * _nbytes((s_dim, LANES), F32)),
    )(d1, d2, gx, pconv)


def _lru_coeffs(ra, ia, p_ref):
    r = _sigmoid(ra + p_ref[pl.ds(0, 1), :])
    i = _sigmoid(ia + p_ref[pl.ds(1, 1), :])
    sp = _softplus(-p_ref[pl.ds(2, 1), :])
    log_a = -LRU_C * r * sp
    a = jnp.exp(log_a)
    mult = jnp.sqrt(-_expm1(2.0 * log_a))
    return r, i, sp, a, mult


def _scan_fwd(gx, rec, gates, pvec, name):
    s_dim, cp = rec.shape
    ts = _tile(s_dim, 256, 8)
    nt = s_dim // ts

    def body(gate_ref, rec_ref, ra_ref, ia_ref, p_ref, h_ref, y_ref, a_s, u_s, carry):
        @pl.when(pl.program_id(0) == 0)
        def _():
            carry[...] = jnp.zeros_like(carry)

        rec_v = rec_ref[...]
        _, i, _, a, mult = _lru_coeffs(ra_ref[...], ia_ref[...], p_ref)
        a_s[...] = a
        u_s[...] = mult * (i * rec_v)

        def step(t, h):
            h = a_s[pl.ds(t, 1), :] * h + u_s[pl.ds(t, 1), :]
            h_ref[pl.ds(t, 1), :] = h
            return h

        carry[pl.ds(0, 1), :] = lax.fori_loop(0, ts, step, carry[pl.ds(0, 1), :], unroll=8)
        y_ref[...] = (_gelu(gate_ref[...]) * h_ref[...]).astype(MXU_DTYPE)

    blk = pl.BlockSpec((ts, cp), lambda t: (t, 0))
    return pl.pallas_call(
        body, name=name, grid=(nt,),
        in_specs=[blk, blk, blk, pl.BlockSpec((ts, cp), lambda t: (t, 1)), pl.BlockSpec((8, cp), lambda t: (0, 0))],
        out_specs=[blk, blk],
        out_shape=[jax.ShapeDtypeStruct((s_dim, cp), F32), jax.ShapeDtypeStruct((s_dim, cp), MXU_DTYPE)],
        scratch_shapes=[pltpu.VMEM((ts, cp), F32), pltpu.VMEM((ts, cp), F32), pltpu.VMEM((8, cp), F32)],
        compiler_params=_cparams(("arbitrary",), 14 * _nbytes((ts, cp), F32)),
    )(gx, rec, gates, gates, pvec)


def _scan_bwd(dy, gx, hrec, rec, gates, pvec, name):
    s_dim, cp = rec.shape
    ts = _tile(s_dim, 128, 8)
    nt = s_dim // ts

    def body(dy_ref, gate_ref, h_ref, hp_ref, rec_ref, ra_ref, ia_ref, p_ref,
             dgate_ref, dra_ref, dia_ref, drec_ref, dp_ref, a_s, d_s, carry):
        t_id = pl.program_id(0)

        @pl.when(t_id == 0)
        def _():
            carry[...] = jnp.zeros_like(carry)
            dp_ref[...] = jnp.zeros_like(dp_ref)

        rec_v = rec_ref[...]
        r, i, sp, a, mult = _lru_coeffs(ra_ref[...], ia_ref[...], p_ref)
        gate = gate_ref[...]
        dyv = dy_ref[...]
        h = h_ref[...]
        dgate_ref[...] = (dyv * h * _gelu_grad(gate)).astype(MXU_DTYPE)
        a_s[...] = a
        d_s[...] = dyv * _gelu(gate)

        def step(k, c):
            t = ts - 1 - k
            d = d_s[pl.ds(t, 1), :] + c
            d_s[pl.ds(t, 1), :] = d
            return a_s[pl.ds(t, 1), :] * d

        carry[pl.ds(0, 1), :] = lax.fori_loop(0, ts, step, carry[pl.ds(0, 1), :], unroll=8)
        dh = d_s[...]
        row = lax.broadcasted_iota(jnp.int32, h.shape, 0)
        first = jnp.where(t_id == nt - 1, 0.0, 1.0) * hp_ref[pl.ds(7, 1), :]
        h_prev = jnp.where(row == 0, first, pltpu.roll(h, 1, 0))
        dix = dh * mult
        dla = dh * h_prev * a - dh * (i * rec_v) * (a * a) / mult
        dra = dla * (-LRU_C * sp) * r * (1.0 - r)
        dia = dix * rec_v * i * (1.0 - i)
        dra_ref[...] = dra.astype(MXU_DTYPE)
        dia_ref[...] = dia.astype(MXU_DTYPE)
        drec_ref[...] = dix * i
        dsp = jnp.sum(dla * (-LRU_C * r), axis=0, keepdims=True)
        dp_ref[pl.ds(0, 1), :] += jnp.sum(dra, axis=0, keepdims=True)
        dp_ref[pl.ds(1, 1), :] += jnp.sum(dia, axis=0, keepdims=True)
        dp_ref[pl.ds(2, 1), :] += dsp * (-_sigmoid(-p_ref[pl.ds(2, 1), :]))

    blk = pl.BlockSpec((ts, cp), lambda t: (nt - 1 - t, 0))
    prev = pl.BlockSpec((8, cp), lambda t: (jnp.maximum((nt - 1 - t) * (ts // 8) - 1, 0), 0))
    par = pl.BlockSpec((8, cp), lambda t: (0, 0))
    lo = jax.ShapeDtypeStruct((s_dim, cp), MXU_DTYPE)
    return pl.pallas_call(
        body, name=name, grid=(nt,),
        in_specs=[blk, blk, blk, prev, blk, blk, pl.BlockSpec((ts, cp), lambda t: (nt - 1 - t, 1)), par],
        out_specs=[blk, blk, blk, blk, par],
        out_shape=[lo, lo, lo, jax.ShapeDtypeStruct((s_dim, cp), F32), jax.ShapeDtypeStruct((8, cp), F32)],
        scratch_shapes=[pltpu.VMEM((ts, cp), F32), pltpu.VMEM((ts, cp), F32), pltpu.VMEM((8, cp), F32)],
        compiler_params=_cparams(("arbitrary",), 40 * _nbytes((ts, cp), F32)),
    )(dy, gx, hrec, hrec, rec, gates, gates, pvec)


def _fgate_fwd(fpre, bias, name):
    s_dim, w = fpre.shape
    ts = _tile(s_dim, 512, 8)

    def body(f_ref, b_ref, c_ref, lf_s, carry):
        @pl.when(pl.program_id(0) == 0)
        def _():
            carry[...] = jnp.zeros_like(carry)

        lf_s[...] = -_softplus(-(f_ref[...] + b_ref[pl.ds(0, 1), :]))

        def step(t, c):
            c = c + lf_s[pl.ds(t, 1), :]
            c_ref[pl.ds(t, 1), :] = c
            return c

        carry[pl.ds(0, 1), :] = lax.fori_loop(0, ts, step, carry[pl.ds(0, 1), :], unroll=8)

    blk = pl.BlockSpec((ts, w), lambda t: (t, 0))
    return pl.pallas_call(
        body, name=name, grid=(s_dim // ts,),
        in_specs=[blk, pl.BlockSpec((8, w), lambda t: (0, 0))], out_specs=blk,
        out_shape=jax.ShapeDtypeStruct((s_dim, w), F32),
        scratch_shapes=[pltpu.VMEM((ts, w), F32), pltpu.VMEM((8, w), F32)],
        compiler_params=_cparams(("arbitrary",), 12 * _nbytes((ts, w), F32)),
    )(fpre, bias)


def _fgate_bwd(dc, fpre, bias, name):
    s_dim, w = fpre.shape
    ts = _tile(s_dim, 512, 8)
    nt = s_dim // ts

    def body(dc_ref, f_ref, b_ref, df_ref, db_ref, d_s, carry):
        @pl.when(pl.program_id(0) == 0)
        def _():
            carry[...] = jnp.zeros_like(carry)
            db_ref[...] = jnp.zeros_like(db_ref)

        d_s[...] = dc_ref[...]

        def step(k, c):
            t = ts - 1 - k
            c = c + d_s[pl.ds(t, 1), :]
            d_s[pl.ds(t, 1), :] = c
            return c

        carry[pl.ds(0, 1), :] = lax.fori_loop(0, ts, step, carry[pl.ds(0, 1), :], unroll=8)
        df = d_s[...] * _sigmoid(-(f_ref[...] + b_ref[pl.ds(0, 1), :]))
        df_ref[...] = df
        db_ref[pl.ds(0, 1), :] += jnp.sum(df, axis=0, keepdims=True)

    blk = pl.BlockSpec((ts, w), lambda t: (nt - 1 - t, 0))
    par = pl.BlockSpec((8, w), lambda t: (0, 0))
    return pl.pallas_call(
        body, name=name, grid=(nt,), in_specs=[blk, blk, par], out_specs=[blk, par],
        out_shape=[jax.ShapeDtypeStruct((s_dim, w), F32), jax.ShapeDtypeStruct((8, w), F32)],
        scratch_shapes=[pltpu.VMEM((ts, w), F32), pltpu.VMEM((8, w), F32)],
        compiler_params=_cparams(("arbitrary",), 12 * _nbytes((ts, w), F32)),
    )(dc, fpre, bias)


def _head_lanes(hh, dh):
    lane = lax.broadcasted_iota(jnp.int32, (1, LANES), 1)
    return (lane >= hh * dh) & (lane < (hh + 1) * dh)


def _pair_attn_fwd(q, kv, v_t, c_col, c_row, name):
    s_dim, da = q.shape
    n_h = c_col.shape[0]
    dh = da // n_h
    assert LANES % dh == 0 and da % LANES == 0
    hb = LANES // dh
    n_blocks = da // LANES
    t = _tile(s_dim, 1024, LANES)
    nb = s_dim // t

    pairs = [(i, j) for i in range(nb) for j in range(i + 1)]
    i_tab = jnp.asarray([p[0] for p in pairs], jnp.int32)
    j_tab = jnp.asarray([p[1] for p in pairs], jnp.int32)

    def body(i_ref, j_ref, q_ref, k_ref, vt_ref, cq_ref, ck_ref, o_ref, lse_ref, m_s, l_s, acc):
        i, j = i_ref[pl.program_id(1)], j_ref[pl.program_id(1)]

        @pl.when(j == 0)
        def _():
            m_s[...] = jnp.full_like(m_s, -jnp.inf)
            l_s[...] = jnp.zeros_like(l_s)
            acc[...] = jnp.zeros_like(acc)

        def tile(masked):
            qv = q_ref[...]
            for hh in range(hb):
                st = _dot(k_ref[...], jnp.where(_head_lanes(hh, dh), qv, jnp.zeros_like(qv)), "nt")
                st = st + (cq_ref[hh] - ck_ref[hh])
                if masked:
                    keep = lax.broadcasted_iota(jnp.int32, (t, t), 0) <= lax.broadcasted_iota(jnp.int32, (t, t), 1)
                    st = jnp.where(keep, st, -jnp.inf)
                m_prev = m_s[hh]
                m_new = jnp.maximum(m_prev, jnp.max(st, axis=0, keepdims=True))
                alpha = jnp.exp(m_prev - m_new)
                p = jnp.exp(st - m_new)
                l_s[hh] = alpha * l_s[hh] + jnp.sum(p, axis=0, keepdims=True)
                acc[hh] = alpha * acc[hh] + _dot(vt_ref[...], p, "nn")
                m_s[hh] = m_new

        pl.when(j < i)(functools.partial(tile, False))
        pl.when(j == i)(functools.partial(tile, True))

        @pl.when(j == i)
        def _():
            feat = lax.broadcasted_iota(jnp.int32, (LANES, 1), 0)
            out_t = jnp.zeros((LANES, t), F32)
            for hh in range(hb):
                out_t = jnp.where((feat >= hh * dh) & (feat < (hh + 1) * dh), acc[hh] / l_s[hh], out_t)
                lse_ref[hh] = m_s[hh] + jnp.log(l_s[hh])
            o_ref[...] = out_t.T

    q_spec = pl.BlockSpec((t, LANES), lambda b, p, it, jt: (it[p], b))
    k_spec = pl.BlockSpec((t, LANES), lambda b, p, it, jt: (jt[p], b))
    vt_spec = pl.BlockSpec((LANES, t), lambda b, p, it, jt: (b, jt[p]))
    cq_spec = pl.BlockSpec((hb, 1, t), lambda b, p, it, jt: (b, 0, it[p]))
    ck_spec = pl.BlockSpec((hb, t, 1), lambda b, p, it, jt: (b, jt[p], 0))
    return pl.pallas_call(
        body, name=name,
        grid_spec=pltpu.PrefetchScalarGridSpec(
            num_scalar_prefetch=2, grid=(n_blocks, len(pairs)),
            in_specs=[q_spec, k_spec, vt_spec, cq_spec, ck_spec], out_specs=[q_spec, cq_spec],
            scratch_shapes=[pltpu.VMEM((hb, 1, t), F32), pltpu.VMEM((hb, 1, t), F32), pltpu.VMEM((hb, LANES, t), F32)]),
        out_shape=[jax.ShapeDtypeStruct((s_dim, da), F32), jax.ShapeDtypeStruct((n_h, 1, s_dim), F32)],
        compiler_params=_cparams(("parallel", "arbitrary"), 10 * hb * _nbytes((t, t), F32)),
    )(i_tab, j_tab, q, kv, v_t, c_row, c_col)


def _attn_delta(do, o, n_h, name):
    s_dim, da = o.shape
    dh = da // n_h
    hb = LANES // dh
    t = _tile(s_dim, 1024, LANES)

    def body(do_ref, o_ref, d_ref):
        prod_t = (do_ref[...].astype(MXU_DTYPE).astype(F32) * o_ref[...]).T
        for hh in range(hb):
            d_ref[hh] = jnp.sum(prod_t[hh * dh:(hh + 1) * dh], axis=0, keepdims=True)

    blk = pl.BlockSpec((t, LANES), lambda b, i: (i, b))
    return pl.pallas_call(
        body, name=name, grid=(da // LANES, s_dim // t), in_specs=[blk, blk],
        out_specs=pl.BlockSpec((hb, 1, t), lambda b, i: (b, 0, i)),
        out_shape=jax.ShapeDtypeStruct((n_h, 1, s_dim), F32),
        compiler_params=_cparams(("parallel", "parallel"), 8 * _nbytes((t, LANES), F32)),
    )(do, o)


def _pair_attn_bwd(q, kv, c_col, c_row, lse, delta, do, scale, name):
    s_dim, da = q.shape
    n_h = c_col.shape[0]
    dh = da // n_h
    hb = LANES // dh
    n_blocks = da // LANES
    t = _tile(s_dim, 1024, LANES)
    nb = s_dim // t

    pairs = [(i, j) for j in range(nb) for i in range(j, nb)]
    i_tab = jnp.asarray([p[0] for p in pairs], jnp.int32)
    j_tab = jnp.asarray([p[1] for p in pairs], jnp.int32)

    def body(i_ref, j_ref, q_ref, k_ref, v_ref, cq_ref, ck_ref, lse_ref, dl_ref, do_ref,
             dq_ref, dcq_ref, dk_ref, dv_ref, dck_ref, dk_acc, dv_acc, dck_acc):
        i, j = i_ref[pl.program_id(1)], j_ref[pl.program_id(1)]

        @pl.when(pl.program_id(1) == 0)
        def _():
            dq_ref[...] = jnp.zeros_like(dq_ref)
            dcq_ref[...] = jnp.zeros_like(dcq_ref)

        @pl.when(i == j)
        def _():
            dk_acc[...] = jnp.zeros_like(dk_acc)
            dv_acc[...] = jnp.zeros_like(dv_acc)
            dck_acc[...] = jnp.zeros_like(dck_acc)

        def tile(masked):
            start = pl.multiple_of(i * t, t)
            qv, kv_ = q_ref[...], k_ref[...]
            dov = do_ref[...].astype(MXU_DTYPE)
            for hh in range(hb):
                lanes = _head_lanes(hh, dh)
                qm = jnp.where(lanes, qv, jnp.zeros_like(qv))
                km = jnp.where(lanes, kv_, jnp.zeros_like(kv_))
                dom = jnp.where(lanes, dov, jnp.zeros_like(dov))
                st = _dot(kv_, qm, "nt") + (cq_ref[hh] - ck_ref[hh])
                if masked:
                    keep = lax.broadcasted_iota(jnp.int32, (t, t), 0) <= lax.broadcasted_iota(jnp.int32, (t, t), 1)
                    st = jnp.where(keep, st, -jnp.inf)
                pt = jnp.exp(st - lse_ref[hh])
                dst = pt * (_dot(v_ref[...], dom, "nt") - dl_ref[hh])
                dv_acc[...] += _dot(pt, dom, "nn")
                dk_acc[...] += _dot(dst, qm, "nn")
                dq_ref[pl.ds(start, t), :] += _dot(dst, km, "tn") * scale
                dcq_ref[hh, :, pl.ds(start, t)] += jnp.sum(dst, axis=0, keepdims=True)
                dck_acc[hh] -= jnp.sum(dst, axis=1, keepdims=True)

        pl.when(i > j)(functools.partial(tile, False))
        pl.when(i == j)(functools.partial(tile, True))

        @pl.when(i == nb - 1)
        def _():
            dk_ref[...] = dk_acc[...]
            dv_ref[...] = dv_acc[...]
            dck_ref[...] = dck_acc[...]

    q_spec = pl.BlockSpec((t, LANES), lambda b, p, it, jt: (it[p], b))
    qrow_spec = pl.BlockSpec((hb, 1, t), lambda b, p, it, jt: (b, 0, it[p]))
    k_spec = pl.BlockSpec((t, LANES), lambda b, p, it, jt: (jt[p], b))
    v_spec = pl.BlockSpec((t, LANES), lambda b, p, it, jt: (jt[p], n_blocks + b))
    kcol_spec = pl.BlockSpec((hb, t, 1), lambda b, p, it, jt: (b, jt[p], 0))
    wide = jax.ShapeDtypeStruct((s_dim, da), F32)
    return pl.pallas_call(
        body, name=name,
        grid_spec=pltpu.PrefetchScalarGridSpec(
            num_scalar_prefetch=2, grid=(n_blocks, len(pairs)),
            in_specs=[q_spec, k_spec, v_spec, qrow_spec, kcol_spec, qrow_spec, qrow_spec, q_spec],
            out_specs=[pl.BlockSpec((s_dim, LANES), lambda b, p, it, jt: (0, b)),
                       pl.BlockSpec((hb, 1, s_dim), lambda b, p, it, jt: (b, 0, 0)), k_spec, k_spec, kcol_spec],
            scratch_shapes=[pltpu.VMEM((t, LANES), F32), pltpu.VMEM((t, LANES), F32), pltpu.VMEM((hb, t, 1), F32)]),
        out_shape=[wide, jax.ShapeDtypeStruct((n_h, 1, s_dim), F32), wide, wide,
                   jax.ShapeDtypeStruct((n_h, s_dim, 1), F32)],
        compiler_params=_cparams(("parallel", "arbitrary"),
                                 10 * hb * _nbytes((t, t), F32) + 4 * _nbytes((s_dim, LANES), F32)),
    )(i_tab, j_tab, q, kv, kv, c_row, c_col, lse, delta, do)


_HBM = pl.BlockSpec(memory_space=pltpu.HBM)
_MESH_ID = pl.DeviceIdType.MESH


def _all_gather(block, name):
    r, w = block.shape

    def body(x_ref, out_ref, send_sems, recv_sems, local_sem):
        x, y, c = lax.axis_index("x"), lax.axis_index("y"), lax.axis_index("c")
        me, sibling = (x, y, c), (x, y, 1 - c)
        chips = [(1 - x, y), (x, 1 - y), (1 - x, 1 - y)]

        def slot(px, py, pc):
            return out_ref.at[4 * px + 2 * py + pc]

        def copy(k, blk, to, src=None):
            return pltpu.make_async_remote_copy(
                src_ref=slot(*blk) if src is None else src, dst_ref=slot(*blk),
                send_sem=send_sems.at[k], recv_sem=recv_sems.at[k], device_id=to, device_id_type=_MESH_ID)

        mine = pltpu.make_async_copy(x_ref, slot(*me), local_sem)
        mine.start()
        first = [copy(0, me, sibling, src=x_ref)]
        first += [copy(1 + n, me, (*chip, c), src=x_ref) for n, chip in enumerate(chips)]
        for cp in first:
            cp.start()
        passed = [copy(4 + n, (*chip, c), sibling) for n, chip in enumerate(chips)]
        for n, chip in enumerate(chips):
            copy(1 + n, (*chip, c), me).wait_recv()
            passed[n].start()
        copy(0, sibling, me).wait_recv()
        for n, chip in enumerate(chips):
            copy(4 + n, (*chip, 1 - c), me).wait_recv()
        for cp in first + passed:
            cp.wait_send()
        mine.wait()

    return pl.pallas_call(
        body, name=name, out_shape=jax.ShapeDtypeStruct((N_DEV, r, w), block.dtype),
        in_specs=[_HBM], out_specs=_HBM,
        scratch_shapes=[pltpu.SemaphoreType.DMA((7,)), pltpu.SemaphoreType.DMA((7,)), pltpu.SemaphoreType.DMA],
    )(block)


_SEM = pl.BlockSpec(memory_space=pltpu.SEMAPHORE)
_EFFECT = pltpu.SideEffectType.DATAFLOW_SIDE_EFFECTING


def _exchange_start(srcs, personalized, after, name):
    n = len(srcs)
    n_after = len(after)
    lands = [lax.empty((N_DEV,) + s.shape[-2:], s.dtype) for s in srcs]

    def body(*refs):
        src_refs, land_refs = refs[:n], refs[n:2 * n]
        outs = refs[2 * n + n_after:]
        send_sems, recv_sems, token = outs[:n], outs[n:2 * n], outs[-1]
        x, y, c = lax.axis_index("x"), lax.axis_index("y"), lax.axis_index("c")
        mine = 4 * x + 2 * y + c
        for ci in range(n):
            for k in range(1, N_DEV):
                px = 1 - x if k & 4 else x
                py = 1 - y if k & 2 else y
                pc = 1 - c if k & 1 else c
                src = src_refs[ci].at[4 * px + 2 * py + pc] if personalized else src_refs[ci]
                pltpu.make_async_remote_copy(
                    src_ref=src, dst_ref=land_refs[ci].at[mine], send_sem=send_sems[ci], recv_sem=recv_sems[ci],
                    device_id=(px, py, pc), device_id_type=_MESH_ID).start()
        token[...] = jnp.zeros_like(token)

    sem = pltpu.SemaphoreType.DMA(())
    out_shape = ([sem] * (2 * n) + [pltpu.HBM(s.shape, s.dtype) for s in srcs]
                 + [pltpu.HBM(l.shape, l.dtype) for l in lands] + [jax.ShapeDtypeStruct((8, LANES), F32)])
    res = pl.pallas_call(
        body, name=name, out_shape=tuple(out_shape),
        in_specs=[_HBM] * (2 * n) + [_ANY] * n_after,
        out_specs=tuple([_SEM] * (2 * n) + [_HBM] * (2 * n) + [pl.BlockSpec(memory_space=pltpu.VMEM)]),
        input_output_aliases={i: 2 * n + i for i in range(2 * n)},
        compiler_params=pltpu.CompilerParams(has_side_effects=_EFFECT),
    )(*[pltpu.with_memory_space_constraint(s, pltpu.HBM) for s in srcs],
      *[pltpu.with_memory_space_constraint(l, pltpu.HBM) for l in lands], *after)
    handles = [(res[ci], res[n + ci], res[2 * n + ci], res[3 * n + ci]) for ci in range(n)]
    return handles, res[-1]


def _exchange_wait(handle, after, name):
    send_sem, recv_sem, src_thru, land_thru = handle

    def body(src_ref, land_ref, send_ref, recv_ref, after_ref, src_out, land_out):
        seven = land_ref.at[pl.ds(0, N_DEV - 1)]
        copies = pltpu.make_async_remote_copy(
            src_ref=seven, dst_ref=seven, send_sem=send_ref, recv_sem=recv_ref,
            device_id=(lax.axis_index("x"), lax.axis_index("y"), lax.axis_index("c")), device_id_type=_MESH_ID)
        copies.wait_send()
        copies.wait_recv()

    return pl.pallas_call(
        body, name=name,
        out_shape=(pltpu.HBM(src_thru.shape, src_thru.dtype), pltpu.HBM(land_thru.shape, land_thru.dtype)),
        in_specs=(_HBM, _HBM, _SEM, _SEM, _ANY), out_specs=(_HBM, _HBM), input_output_aliases={0: 0, 1: 1},
        compiler_params=pltpu.CompilerParams(has_side_effects=_EFFECT),
    )(src_thru, land_thru, send_sem, recv_sem, after)[1]


def _own_slot(land, own, me):
    return lax.dynamic_update_index_in_dim(land, own, me, axis=0)


def _sum_slots(slots, name):
    n, r, w = slots.shape
    tr = _tile(r, 256, WIRE_ROW_ALIGN)

    def body(s_ref, o_ref):
        acc = s_ref[0].astype(F32)
        for d in range(1, n):
            acc = acc + s_ref[d].astype(F32)
        o_ref[...] = acc

    return pl.pallas_call(
        body, name=name, grid=(r // tr,),
        in_specs=[pl.BlockSpec((n, tr, w), lambda i: (0, i, 0))],
        out_specs=pl.BlockSpec((tr, w), lambda i: (i, 0)),
        out_shape=jax.ShapeDtypeStruct((r, w), F32),
        compiler_params=_cparams(("parallel",), 2 * _nbytes((n, tr, w), slots.dtype) + 4 * _nbytes((tr, w), F32)),
    )(slots)


def _adamw(w, g, m, v, name):
    r, c = w.shape
    tr = _tile(r, 512, 8)

    def body(w_ref, g_ref, m_ref, v_ref, d_ref, mo_ref, vo_ref):
        gv = g_ref[...]
        m_new = ADAM_B1 * m_ref[...] + (1.0 - ADAM_B1) * gv
        v_new = ADAM_B2 * v_ref[...] + (1.0 - ADAM_B2) * (gv * gv)
        m_hat = m_new / (1.0 - ADAM_B1 ** ADAM_STEP)
        v_hat = v_new / (1.0 - ADAM_B2 ** ADAM_STEP)
        d_ref[...] = -ADAM_LR * (m_hat / (jnp.sqrt(v_hat) + ADAM_EPS) + ADAM_WD * w_ref[...])
        mo_ref[...] = m_new
        vo_ref[...] = v_new

    blk = pl.BlockSpec((tr, c), lambda i: (i, 0))
    shp = jax.ShapeDtypeStruct((r, c), F32)
    return pl.pallas_call(
        body, name=name, grid=(r // tr,), in_specs=[blk] * 4, out_specs=[blk] * 3, out_shape=[shp] * 3,
        compiler_params=_cparams(("parallel",), 16 * _nbytes((tr, _round_up(c, LANES)), F32)),
    )(w, g, m, v)


def _pack_rows(parts, width, dtype, row_align):
    rows, spans, off = [], [], 0
    for p in parts:
        flat = p.reshape(-1).astype(dtype)
        n_rows = _round_up(-(-flat.shape[0] // width), row_align)
        flat = jnp.pad(flat, (0, n_rows * width - flat.shape[0]))
        rows.append(flat.reshape(n_rows, width))
        spans.append((off, n_rows))
        off += n_rows
    return jnp.concatenate(rows, axis=0), spans


def _unpack_rows(mat, span, shape):
    off, n_rows = span
    n = 1
    for s in shape:
        n *= s
    return mat[..., off:off + n_rows, :].reshape(mat.shape[:-2] + (-1,))[..., :n].reshape(mat.shape[:-2] + tuple(shape))


def _block_diag(w, size):
    n, b, _ = w.shape
    eye = jnp.eye(n, dtype=w.dtype)
    dense = (w[:, :, None, :] * eye[:, None, :, None]).reshape(n * b, n * b)
    return jnp.pad(dense, ((0, size - n * b), (0, size - n * b)))


def _diag_blocks(dense, n, b):
    return jnp.stack([dense[k * b:(k + 1) * b, k * b:(k + 1) * b] for k in range(n)])


def _pad_rows(a, rows):
    return jnp.pad(a, ((0, rows - a.shape[0]), (0, 0)))


def _pad_cols(a, cols):
    return jnp.pad(a, ((0, 0), (0, cols - a.shape[1])))


def _train_step(a):
    x = a["x"][0]
    target = a["loss_target"][0]
    s_dim, d = x.shape
    n_layers = a["ffn1_pre_g"].shape[0]
    f_shard = a["ffn1_w_gate"].shape[2]
    c_shard = a["rg_conv_b"].shape[1]
    c_dim = c_shard * N_DEV
    cp = _round_up(c_dim, LANES)
    conv_width = a["rg_conv_w"].shape[1]
    n_blocks, lru_block = a["rg_w_a"].shape[1], a["rg_w_a"].shape[2]
    d_attn = a["attn_w_q"].shape[2]
    n_heads = a["b_fgate"].shape[0]
    d_head = d_attn // n_heads
    attn_scale = d_head ** -0.5
    assert conv_width < 8 and n_heads <= LANES and n_layers == 2
    assert d_attn == d
    me = 4 * lax.axis_index("x") + 2 * lax.axis_index("y") + lax.axis_index("c")

    shard = {"rg_w_in": a["rg_w_in"][0].T, "rg_w_out": a["rg_w_out"][0], "w_kv": a["w_kv"].T,
             "attn_w_q": a["attn_w_q"][0], "attn_w_o": a["attn_w_o"][0]}
    for l in range(n_layers):
        for f in ("ffn1", "ffn2"):
            shard[(f, "gate", l)] = a[f + "_w_gate"][l].T
            shard[(f, "up", l)] = a[f + "_w_up"][l].T
            shard[(f, "down", l)] = a[f + "_w_down"][l]

    def ffn_names(f, l):
        return [(f, "gate", l), (f, "up", l), (f, "down", l)]

    def chunk_layout(names):
        spans, off = [], 0
        for nm in names:
            spans.append((nm, off, shard[nm].shape[0]))
            off += _round_up(shard[nm].shape[0], WIRE_ROW_ALIGN)
        return spans, off

    def pack_chunk(names, parts):
        return jnp.concatenate(
            [_pad_rows(parts[nm].astype(WIRE_DTYPE), _round_up(parts[nm].shape[0], WIRE_ROW_ALIGN)) for nm in names], axis=0)

    full = {}

    def unpack_chunk(names, gathered):
        for nm, o, n_rows in chunk_layout(names)[0]:
            full[nm] = gathered[:, o:o + n_rows, :].reshape(N_DEV * n_rows, d)

    fwd_chunks = [ffn_names("ffn1", 0)[:2], ffn_names("ffn1", 0)[2:], ["rg_w_in"], ["rg_w_out"],
                  ffn_names("ffn2", 0) + ["w_kv"], ffn_names("ffn1", 1) + ["attn_w_q", "attn_w_o"], ffn_names("ffn2", 1)]
    fwd_packs = [pack_chunk(names, shard) for names in fwd_chunks]
    unpack_chunk(fwd_chunks[0], _all_gather(fwd_packs[0], "gather_weights_first"))

    small_parts = [a["rg_conv_w"][0], a["rg_conv_b"][0], a["rg_b_a"][0], a["rg_b_x"][0], a["rg_lambda"][0], a["w_fgate"]]
    small_pack, small_spans = _pack_rows(small_parts, d, F32, 8)
    small_all = _all_gather(small_pack, "gather_small")
    fwd_handles, fwd_token = _exchange_start(fwd_packs[1:], False, [full[("ffn1", "up", 0)], small_all],
                                             "gather_weights_start")

    def land_weights(n, after):
        land = _exchange_wait(fwd_handles[n - 1], after, f"gather_weights_wait_{n}")
        unpack_chunk(fwd_chunks[n], _own_slot(land, fwd_packs[n], me))

    sm = [_unpack_rows(small_all, sp, p.shape) for sp, p in zip(small_spans, small_parts)]
    conv_w = jnp.moveaxis(sm[0], 0, 1).reshape(conv_width, c_dim)
    conv_b, b_a, b_x, lam = (v.reshape(1, c_dim) for v in sm[1:5])
    w_f = sm[5].reshape(d, n_heads)

    pconv = _pad_rows(_pad_cols(jnp.concatenate([conv_w, conv_b], axis=0), cp), 8)
    pvec = _pad_rows(_pad_cols(jnp.concatenate([b_a, b_x, lam], axis=0), cp), 8)
    wa_dense = _block_diag(a["rg_w_a"][0], cp).astype(MXU_DTYPE)
    wx_dense = _block_diag(a["rg_w_x"][0], cp).astype(MXU_DTYPE)
    wax = jnp.concatenate([wa_dense, wx_dense], axis=1)
    w_f_t = _pad_rows(w_f.T.astype(MXU_DTYPE), LANES)
    b_f = _pad_rows(_pad_cols(a["b_fgate"].reshape(1, n_heads), LANES), 8)

    def gain(name, l):
        return a[name][l].reshape(1, d)

    def ffn_fwd(h, f, l, after=None, down_chunk=None, xn=None, next_gain=None):
        if xn is None:
            xn = _rms_fwd(h, gain(f + "_pre_g", l), f"{f}_{l}_pre_norm", after)
        g, u, act = _ffn_up(xn, full[(f, "gate", l)], full[(f, "up", l)], f"{f}_{l}_up")
        if down_chunk is not None:
            land_weights(down_chunk, act)
        fo, h_new, *nxt = _mm_rms_res(act, full[(f, "down", l)], h, gain(f + "_post_g", l), 0.5, f"{f}_{l}_down",
                                      next_gain)
        return h_new, (h, xn, g, u, act, fo), (nxt[0] if nxt else None)

    h0 = x
    h0a, sv_f1_0, hn_rg = ffn_fwd(h0, "ffn1", 0, fwd_token, down_chunk=1, next_gain=gain("mix_pre_g", 0))
    land_weights(2, h0a)
    w_in_gate = _pad_rows(full["rg_w_in"][:c_dim], cp)
    w_in_rec = _pad_rows(full["rg_w_in"][c_dim:], cp)
    w_in_t = jnp.concatenate([w_in_gate, w_in_rec], axis=0)
    gx = _mm([(hn_rg, w_in_t)], "nt", F32, "rg_in_proj")
    rec = _conv_fwd(gx, pconv, conv_width, "rg_conv")
    gates = _mm([(rec, wax)], "nn", F32, "rg_gate_proj")
    h_rec, y_rg = _scan_fwd(gx, rec, gates, pvec, "rg_scan")
    land_weights(3, y_rg)
    w_out = _pad_rows(full["rg_w_out"], cp)
    m_rg, h0b, xn_f2 = _mm_rms_res(y_rg, w_out, h0a, gain("mix_post_g", 0), 1.0, "rg_out_proj", gain("ffn2_pre_g", 0))
    land_weights(4, h0b)
    h1, sv_f2_0, hn_kv = ffn_fwd(h0b, "ffn2", 0, xn=xn_f2, next_gain=a["kv_norm_g"].reshape(1, d))
    kv = _mm([(hn_kv, full["w_kv"])], "nt", MXU_DTYPE, "kv_proj")
    fpre = _mm([(hn_kv, w_f_t)], "nt", F32, "fgate_proj")
    c_cum = _fgate_fwd(fpre, b_f, "fgate_cumsum")
    c_heads = c_cum[:, :n_heads].T
    c_col, c_row = c_heads[:, :, None], c_heads[:, None, :]
    land_weights(5, c_cum)
    h1a, sv_f1_1, hn_at = ffn_fwd(h1, "ffn1", 1, next_gain=gain("mix_pre_g", 1))
    q_s = _mm([(hn_at, full["attn_w_q"])], "nn", MXU_DTYPE, "q_proj", out_scale=attn_scale)
    o2, lse = _pair_attn_fwd(q_s, kv, kv[:, d_attn:].T, c_col, c_row, "attn_fwd")
    m_at, h1b, xn_f2 = _mm_rms_res(o2, full["attn_w_o"], h1a, gain("mix_post_g", 1), 1.0, "attn_out_proj",
                                   gain("ffn2_pre_g", 1))
    land_weights(6, h1b)
    y, sv_f2_1, _ = ffn_fwd(h1b, "ffn2", 1, xn=xn_f2)
    dy, loss_part = _loss_head(y, target, "loss_head")

    grads_big = {}
    grads_rep = {}

    bwd_chunks = [ffn_names("ffn2", 1), ["attn_w_q", "attn_w_o"] + ffn_names("ffn1", 1),
                  ["w_kv"] + ffn_names("ffn2", 0), ["rg_w_in", "rg_w_out"], ffn_names("ffn1", 0)[2:],
                  ffn_names("ffn1", 0)[:2]]
    bwd_sends, bwd_handles = [], []

    def send_grads(after):
        n = len(bwd_sends)
        send = jnp.concatenate(
            [jnp.pad(grads_big[nm].reshape(N_DEV, n_rows, d), ((0, 0), (0, _round_up(n_rows, WIRE_ROW_ALIGN) - n_rows), (0, 0)))
             for nm, _, n_rows in chunk_layout(bwd_chunks[n])[0]], axis=1)
        handles, token = _exchange_start([send], True, [after], f"exchange_grads_start_{n}")
        bwd_sends.append(send)
        bwd_handles.append(handles[0])
        return token

    def ffn_bwd(dh_out, saved, f, l, after=None, send_now=False):
        h, xn, g, u, act, fo = saved
        df, d_post = _rms_bwd(fo, gain(f + "_post_g", l), [dh_out], None, 0.5, MXU_DTYPE, f"{f}_{l}_post_norm_bwd", after)
        dg, du = _ffn_act_bwd(df, full[(f, "down", l)], g, u, f"{f}_{l}_act_bwd")
        grads_big[(f, "down", l)] = _mm([(act, df)], "tn", WIRE_DTYPE, f"{f}_{l}_dw_down")
        sent = send_grads(df) if send_now else None
        grads_big[(f, "gate", l)] = _mm([(dg, xn)], "tn", WIRE_DTYPE, f"{f}_{l}_dw_gate", sent)
        grads_big[(f, "up", l)] = _mm([(du, xn)], "tn", WIRE_DTYPE, f"{f}_{l}_dw_up")
        sent = send_grads(df) if send_now else None
        dxn = _mm([(dg, full[(f, "gate", l)]), (du, full[(f, "up", l)])], "nn", F32, f"{f}_{l}_dx", sent)
        dh_in, d_pre = _rms_bwd(h, gain(f + "_pre_g", l), [dxn], dh_out, 1.0, F32, f"{f}_{l}_pre_norm_bwd")
        grads_rep[(f + "_post_g", l)] = d_post
        grads_rep[(f + "_pre_g", l)] = d_pre
        return dh_in

    dh = ffn_bwd(dy, sv_f2_1, "ffn2", 1)
    token = send_grads(dh)
    dm, d_post = _rms_bwd(m_at, gain("mix_post_g", 1), [dh], None, 1.0, MXU_DTYPE, "attn_post_norm_bwd", token)
    grads_rep[("mix_post_g", 1)] = d_post
    do2 = _mm([(dm, full["attn_w_o"])], "nt", F32, "attn_out_proj_dx")
    grads_big["attn_w_o"] = _mm([(o2, dm)], "tn", WIRE_DTYPE, "attn_out_proj_dw")
    delta = _attn_delta(do2, o2, n_heads, "attn_delta")
    dq2, dc_q, dk2, dv2, dc_k = _pair_attn_bwd(q_s, kv, c_col, c_row, lse, delta, do2, attn_scale, "attn_bwd")
    dc_heads = dc_q[:, 0, :] + dc_k[:, :, 0]
    dhn = _mm([(dq2, full["attn_w_q"])], "nt", F32, "q_proj_dx")
    grads_big["attn_w_q"] = _mm([(hn_at, dq2)], "tn", WIRE_DTYPE, "q_proj_dw")
    dh, d_pre = _rms_bwd(h1a, gain("mix_pre_g", 1), [dhn], dh, 1.0, F32, "attn_pre_norm_bwd")
    grads_rep[("mix_pre_g", 1)] = d_pre
    dh = ffn_bwd(dh, sv_f1_1, "ffn1", 1)
    token = send_grads(dh)
    dc_cum = _pad_cols(dc_heads.T, LANES)
    dfpre, db_f = _fgate_bwd(dc_cum, fpre, b_f, "fgate_cumsum_bwd")
    dhn_kv = _mm([(dk2, full["w_kv"][:d_attn]), (dv2, full["w_kv"][d_attn:])], "nn", F32, "kv_proj_dx")
    dhn_f = _mm([(dfpre, w_f_t)], "nn", F32, "fgate_proj_dx")
    grads_big["w_kv"] = jnp.concatenate([_mm([(dk2, hn_kv)], "tn", WIRE_DTYPE, "kv_proj_dw_k"),
                                         _mm([(dv2, hn_kv)], "tn", WIRE_DTYPE, "kv_proj_dw_v")], axis=0)
    dw_f_t = _mm([(dfpre, hn_kv)], "tn", F32, "fgate_proj_dw")
    dh, d_kvg = _rms_bwd(h1, a["kv_norm_g"].reshape(1, d), [dhn_kv, dhn_f], dh, 1.0, F32, "kv_norm_bwd", token)
    dh = ffn_bwd(dh, sv_f2_0, "ffn2", 0)
    token = send_grads(dh)
    dm, d_post = _rms_bwd(m_rg, gain("mix_post_g", 0), [dh], None, 1.0, MXU_DTYPE, "rg_post_norm_bwd", token)
    grads_rep[("mix_post_g", 0)] = d_post
    dy_rg = _mm([(dm, w_out)], "nt", F32, "rg_out_proj_dx")
    dw_out = _mm([(y_rg, dm)], "tn", WIRE_DTYPE, "rg_out_proj_dw")
    dgate, dra, dia, drec1, dpvec = _scan_bwd(dy_rg, gx, h_rec, rec, gates, pvec, "rg_scan_bwd")
    drec2 = _mm([(dra, wa_dense), (dia, wx_dense)], "nt", F32, "rg_gate_proj_dx")
    dwa_dense = _mm([(rec, dra)], "tn", F32, "rg_gate_proj_dwa")
    dwx_dense = _mm([(rec, dia)], "tn", F32, "rg_gate_proj_dwx")
    drec0, dpconv = _conv_bwd(drec1, drec2, gx, pconv, conv_width, "rg_conv_bwd")
    dhn = _mm([(dgate, w_in_gate), (drec0, w_in_rec)], "nn", F32, "rg_in_proj_dx")
    dw_in_gate = _mm([(dgate, hn_rg)], "tn", WIRE_DTYPE, "rg_in_proj_dw_gate")
    dw_in_rec = _mm([(drec0, hn_rg)], "tn", WIRE_DTYPE, "rg_in_proj_dw_rec")
    dh, d_pre = _rms_bwd(h0a, gain("mix_pre_g", 0), [dhn], dh, 1.0, F32, "rg_pre_norm_bwd")
    grads_rep[("mix_pre_g", 0)] = d_pre
    grads_big["rg_w_in"] = jnp.concatenate([dw_in_gate[:c_dim], dw_in_rec[:c_dim]], axis=0)
    grads_big["rg_w_out"] = dw_out[:c_dim]
    token = send_grads(dh)
    grad_x = ffn_bwd(dh, sv_f1_0, "ffn1", 0, token, send_now=True)

    g_shard = {}

    def land_grads(n, after):
        land = _exchange_wait(bwd_handles[n], after, f"exchange_grads_wait_{n}")
        own = lax.dynamic_index_in_dim(bwd_sends[n], me, axis=0, keepdims=False)
        g_chunk = _sum_slots(_own_slot(land, own, me), f"sum_weight_grads_{n}")
        for nm, o, n_rows in chunk_layout(bwd_chunks[n])[0]:
            g_shard[nm] = g_chunk[o:o + n_rows]

    for n in range(len(bwd_chunks) - 2):
        land_grads(n, grad_x)

    def gain_grad(name):
        return jnp.concatenate([grads_rep[(name, l)] for l in range(n_layers)], axis=0)

    rep_names = ["ffn1_pre_g", "ffn1_post_g", "mix_pre_g", "mix_post_g", "ffn2_pre_g", "ffn2_post_g"]
    rep_parts = [gain_grad(nm) for nm in rep_names]
    rep_names += ["kv_norm_g", "b_fgate", "rg_w_a", "rg_w_x", "rg_conv_w", "rg_conv_b", "rg_b_a", "rg_b_x", "rg_lambda", "w_fgate"]
    rep_parts += [
        d_kvg, db_f[0, :n_heads],
        _diag_blocks(dwa_dense, n_blocks, lru_block), _diag_blocks(dwx_dense, n_blocks, lru_block),
        dpconv[:conv_width, :c_dim], dpconv[conv_width, :c_dim],
        dpvec[0, :c_dim], dpvec[1, :c_dim], dpvec[2, :c_dim],
        dw_f_t[:n_heads].T]
    rep_pack, rep_spans = _pack_rows(rep_parts, d, F32, 8)
    rep_sum = _sum_slots(_all_gather(rep_pack, "gather_small_grads"), "sum_small_grads")
    g_rep = {nm: _unpack_rows(rep_sum, sp, p.shape) for nm, sp, p in zip(rep_names, rep_spans, rep_parts)}

    def my_cols(full_grad, n):
        return lax.dynamic_slice_in_dim(full_grad, me * n, n, axis=full_grad.ndim - 1)

    def ffn_grads(f):
        grad[f + "_w_gate"] = jnp.stack([g_shard[(f, "gate", l)].T for l in range(n_layers)])
        grad[f + "_w_up"] = jnp.stack([g_shard[(f, "up", l)].T for l in range(n_layers)])
        grad[f + "_w_down"] = jnp.stack([g_shard[(f, "down", l)] for l in range(n_layers)])

    grad = {}
    for nm in ("ffn1_pre_g", "ffn1_post_g", "mix_pre_g", "mix_post_g", "ffn2_pre_g", "ffn2_post_g"):
        grad[nm] = g_rep[nm]
    ffn_grads("ffn2")
    grad["rg_w_in"] = g_shard["rg_w_in"].T[None]
    grad["rg_conv_w"] = my_cols(g_rep["rg_conv_w"], c_shard)[None]
    for nm in ("rg_conv_b", "rg_b_a", "rg_b_x", "rg_lambda"):
        grad[nm] = my_cols(g_rep[nm], c_shard)[None]
    grad["rg_w_a"] = g_rep["rg_w_a"][None]
    grad["rg_w_x"] = g_rep["rg_w_x"][None]
    grad["rg_w_out"] = g_shard["rg_w_out"][None]
    grad["kv_norm_g"] = g_rep["kv_norm_g"].reshape(d)
    grad["w_kv"] = g_shard["w_kv"].T
    grad["w_fgate"] = lax.dynamic_slice_in_dim(g_rep["w_fgate"], me * (d // N_DEV), d // N_DEV, axis=0)
    grad["b_fgate"] = g_rep["b_fgate"]
    grad["attn_w_q"] = g_shard["attn_w_q"][None]
    grad["attn_w_o"] = g_shard["attn_w_o"][None]

    delta, new_m, new_v = {}, {}, {}

    def adamw(nm):
        w = a[nm]
        shape = w.shape
        two_d = (1, shape[0]) if w.ndim == 1 else (-1, shape[-1])
        ops = [pltpu.with_memory_space_constraint(t.reshape(two_d), pltpu.HBM)
               for t in (w, grad[nm], a["m_" + nm], a["v_" + nm])]
        dl, mo, vo = _adamw(*ops, "adamw_" + nm)
        delta[nm], new_m[nm], new_v[nm] = dl.reshape(shape), mo.reshape(shape), vo.reshape(shape)
        grad[nm] = grad[nm].reshape(shape)

    last_names = ("ffn1_w_gate", "ffn1_w_up", "ffn1_w_down")
    for nm in WEIGHT_NAMES:
        if nm not in last_names:
            adamw(nm)
    land_grads(len(bwd_chunks) - 2, delta["attn_w_o"])
    land_grads(len(bwd_chunks) - 1, delta["attn_w_o"])
    ffn_grads("ffn1")
    for nm in last_names:
        adamw(nm)

    loss = lax.psum(loss_part[0, 0], AXES)
    return (loss, grad_x[None], *[grad[n] for n in WEIGHT_NAMES], *[delta[n] for n in WEIGHT_NAMES],
            *[new_m[n] for n in WEIGHT_NAMES], *[new_v[n] for n in WEIGHT_NAMES])


def kernel(x, ffn1_pre_g, ffn1_w_gate, ffn1_w_up, ffn1_w_down, ffn1_post_g, mix_pre_g, mix_post_g, ffn2_pre_g, ffn2_w_gate, ffn2_w_up, ffn2_w_down, ffn2_post_g, rg_w_in, rg_conv_w, rg_conv_b, rg_w_a, rg_b_a, rg_w_x, rg_b_x, rg_lambda, rg_w_out, kv_norm_g, w_kv, w_fgate, b_fgate, attn_w_q, attn_w_o, loss_target, m_ffn1_pre_g, m_ffn1_w_gate, m_ffn1_w_up, m_ffn1_w_down, m_ffn1_post_g, m_mix_pre_g, m_mix_post_g, m_ffn2_pre_g, m_ffn2_w_gate, m_ffn2_w_up, m_ffn2_w_down, m_ffn2_post_g, m_rg_w_in, m_rg_conv_w, m_rg_conv_b, m_rg_w_a, m_rg_b_a, m_rg_w_x, m_rg_b_x, m_rg_lambda, m_rg_w_out, m_kv_norm_g, m_w_kv, m_w_fgate, m_b_fgate, m_attn_w_q, m_attn_w_o, v_ffn1_pre_g, v_ffn1_w_gate, v_ffn1_w_up, v_ffn1_w_down, v_ffn1_post_g, v_mix_pre_g, v_mix_post_g, v_ffn2_pre_g, v_ffn2_w_gate, v_ffn2_w_up, v_ffn2_w_down, v_ffn2_post_g, v_rg_w_in, v_rg_conv_w, v_rg_conv_b, v_rg_w_a, v_rg_b_a, v_rg_w_x, v_rg_b_x, v_rg_lambda, v_rg_w_out, v_kv_norm_g, v_w_kv, v_w_fgate, v_b_fgate, v_attn_w_q, v_attn_w_o):
    return _train_step(dict(locals()))
```
